```python
import jax, jax.numpy as jnp
from jax import lax
import numpy as np

D_MODEL = 1024
BATCH = 8
SEQ = 4096
DEPTH = 2

MLA_HEADS = 8
QK_NOPE_DIM = 64
QK_ROPE_DIM = 32
V_HEAD_DIM = 64
Q_LORA_RANK = 256
KV_LORA_RANK = 128
QK_HEAD_DIM = QK_NOPE_DIM + QK_ROPE_DIM
D_ATTN = MLA_HEADS * V_HEAD_DIM
ROPE_THETA = 10000.0
Q_BLOCK = 128
SSD_HEADS = 8
SSD_HEAD_DIM = 64
SSD_GROUPS = 2
SSD_STATE = 128
CONV_WIDTH = 4
CHUNK = 128
D_SSD = SSD_HEADS * SSD_HEAD_DIM
D_CONV = D_SSD + 2 * SSD_GROUPS * SSD_STATE
D_MIX = D_ATTN + D_SSD
D_IN = Q_LORA_RANK + KV_LORA_RANK + QK_ROPE_DIM + D_SSD + D_CONV + SSD_HEADS
D_FF = ((-(-8 * D_MODEL // 3) + 255) // 256) * 256
N_MOD = 6
EPS = 1e-6

kernel_name = "hymba_mla_ssd_adaln_block"


def rmsnorm(x, w):
    xf = x.astype(jnp.float32)
    y = xf * lax.rsqrt(jnp.mean(xf * xf, axis=-1, keepdims=True) + EPS)
    return (y * w.astype(jnp.float32)).astype(x.dtype)


def rope(x, cos, sin):
    x1, x2 = jnp.split(x, 2, axis=-1)
    return jnp.concatenate([x1 * cos - x2 * sin, x2 * cos + x1 * sin], axis=-1)


def causal_attention(q, k, v):
    b, h, s, d = q.shape
    nb = s // Q_BLOCK
    scale = d ** -0.5
    qb = q.reshape(b, h, nb, Q_BLOCK, d).transpose(2, 0, 1, 3, 4)
    kpos = jnp.arange(s)

    def one_block(args):
        qi, i = args
        sc = jnp.einsum('bhqd,bhkd->bhqk', qi, k).astype(jnp.float32) * scale
        qpos = i * Q_BLOCK + jnp.arange(Q_BLOCK)
        mask = kpos[None, :] <= qpos[:, None]
        sc = jnp.where(mask, sc, jnp.finfo(jnp.float32).min)
        p = jax.nn.softmax(sc, axis=-1)
        return jnp.einsum('bhqk,bhkd->bhqd', p.astype(v.dtype), v)

    out = lax.map(one_block, (qb, jnp.arange(nb)))
    return out.transpose(1, 0, 3, 2, 4).reshape(b, s, h * v.shape[-1])


def ssd_chunked(x, dt, a, bm, cm):
    b, l, h, p = x.shape
    rep = h // bm.shape[2]
    nc = l // CHUNK
    xf = x.astype(jnp.float32)
    xdt = xf * dt[..., None]
    adt = dt * a
    bh = jnp.repeat(bm.astype(jnp.float32), rep, axis=2)
    ch = jnp.repeat(cm.astype(jnp.float32), rep, axis=2)
    n = bh.shape[-1]
    xc = xdt.reshape(b, nc, CHUNK, h, p)
    bc = bh.reshape(b, nc, CHUNK, h, n)
    cc = ch.reshape(b, nc, CHUNK, h, n)
    acs = jnp.cumsum(adt.reshape(b, nc, CHUNK, h), axis=2)
    seg = acs[:, :, :, None, :] - acs[:, :, None, :, :]
    tri = jnp.tril(jnp.ones((CHUNK, CHUNK), dtype=bool))[None, None, :, :, None]
    lmat = jnp.exp(jnp.where(tri, seg, -jnp.inf))
    scores = jnp.einsum('bclhn,bcshn->bclsh', cc, bc) * lmat
    y_diag = jnp.einsum('bclsh,bcshp->bclhp', scores, xc)
    decay_states = jnp.exp(acs[:, :, -1:, :] - acs)
    states = jnp.einsum('bcshn,bcsh,bcshp->bchpn', bc, decay_states, xc)
    chunk_decay = jnp.exp(acs[:, :, -1, :])

    def step(carry, inp):
        s_c, d_c = inp
        new = d_c[:, :, None, None] * carry + s_c
        return new, carry

    init = jnp.zeros((b, h, p, n), jnp.float32)
    _, prev = lax.scan(step, init, (states.transpose(1, 0, 2, 3, 4), chunk_decay.transpose(1, 0, 2)))
    prev = prev.transpose(1, 0, 2, 3, 4)
    y_off = jnp.einsum('bclhn,bchpn,bclh->bclhp', cc, prev, jnp.exp(acs))
    return (y_diag + y_off).reshape(b, l, h, p)


def hybrid_mixer(h, cos, sin, w_in, q_a_norm_w, w_q_up, kv_a_norm_w, w_kv_up,
                 q_nope_norm_w, q_pe_norm_w, k_nope_norm_w, k_pe_norm_w,
                 conv_w, conv_b, dt_bias, a_log, d_skip, ssd_norm_w, w_out):
    b, s, _ = h.shape
    proj = h @ w_in
    cuts = np.cumsum([Q_LORA_RANK, KV_LORA_RANK, QK_ROPE_DIM, D_SSD, D_CONV])
    q_a, kv_a, k_pe, z, xbc, dt_raw = jnp.split(proj, [int(i) for i in cuts], axis=-1)

    q = (rmsnorm(q_a, q_a_norm_w) @ w_q_up).reshape(b, s, MLA_HEADS, QK_HEAD_DIM)
    q_nope, q_pe = jnp.split(q, [QK_NOPE_DIM], axis=-1)
    kv = (rmsnorm(kv_a, kv_a_norm_w) @ w_kv_up).reshape(b, s, MLA_HEADS, QK_NOPE_DIM + V_HEAD_DIM)
    k_nope, v = jnp.split(kv, [QK_NOPE_DIM], axis=-1)
    q_nope = rmsnorm(q_nope, q_nope_norm_w)
    q_pe = rope(rmsnorm(q_pe, q_pe_norm_w), cos, sin)
    k_nope = rmsnorm(k_nope, k_nope_norm_w)
    k_pe = rope(rmsnorm(k_pe[:, :, None, :], k_pe_norm_w), cos, sin)
    q_full = jnp.concatenate([q_nope, q_pe], axis=-1)
    k_full = jnp.concatenate([k_nope, jnp.broadcast_to(k_pe, (b, s, MLA_HEADS, QK_ROPE_DIM))], axis=-1)
    attn_out = causal_attention(q_full.transpose(0, 2, 1, 3), k_full.transpose(0, 2, 1, 3),
                                v.transpose(0, 2, 1, 3))

    xbc = lax.conv_general_dilated(xbc, conv_w[:, None, :], window_strides=(1,),
                                   padding=[(CONV_WIDTH - 1, 0)],
                                   dimension_numbers=('NWC', 'WIO', 'NWC'),
                                   feature_group_count=D_CONV)
    xbc = jax.nn.silu(xbc + conv_b)
    xs, bm, cm = jnp.split(xbc, [D_SSD, D_SSD + SSD_GROUPS * SSD_STATE], axis=-1)
    xs = xs.reshape(b, s, SSD_HEADS, SSD_HEAD_DIM)
    bm = bm.reshape(b, s, SSD_GROUPS, SSD_STATE)
    cm = cm.reshape(b, s, SSD_GROUPS, SSD_STATE)
    dt = jax.nn.softplus(dt_raw.astype(jnp.float32) + dt_bias.astype(jnp.float32))
    a = -jnp.exp(a_log.astype(jnp.float32))
    y = ssd_chunked(xs, dt, a, bm, cm) + d_skip.astype(jnp.float32)[:, None] * xs.astype(jnp.float32)
    y = y.astype(h.dtype).reshape(b, s, D_SSD)
    yg = (y * jax.nn.silu(z)).reshape(b, s, SSD_GROUPS, D_SSD // SSD_GROUPS)
    yg = rmsnorm(yg, jnp.ones((D_SSD // SSD_GROUPS,), h.dtype)).reshape(b, s, D_SSD) * ssd_norm_w

    return jnp.concatenate([attn_out, yg], axis=-1) @ w_out


def swiglu(h, w_gate_up, w_down):
    g, u = jnp.split(h @ w_gate_up, 2, axis=-1)
    return (jax.nn.silu(g) * u) @ w_down


def _fwd_setup_inputs(seed: int = 0) -> dict:
    key = jax.random.key(seed)
    ks = jax.random.split(key, 32)
    f32 = jnp.float32

    def nrm(k, shape, scale):
        return jax.random.normal(k, shape, f32) * scale

    def gain(k, shape):
        return 1.0 + 0.05 * jax.random.normal(k, shape, f32)

    L = DEPTH
    dt0 = jnp.exp(jax.random.uniform(ks[18], (L, SSD_HEADS), f32, np.log(1e-3), np.log(1e-1)))
    dt_bias = dt0 + jnp.log(-jnp.expm1(-dt0))
    pos_off = jax.random.randint(ks[2], (BATCH, 1), 0, 1024, dtype=jnp.int32)
    return {
        "x": nrm(ks[0], (BATCH, SEQ, D_MODEL), 1.0),
        "c": nrm(ks[1], (BATCH, D_MODEL), 1.0),
        "positions": pos_off + jnp.arange(SEQ, dtype=jnp.int32)[None, :],
        "norm1_w": gain(ks[3], (L, D_MODEL)),
        "norm2_w": gain(ks[4], (L, D_MODEL)),
        "w_ada": nrm(ks[5], (L, D_MODEL, N_MOD * D_MODEL), 0.5 * D_MODEL ** -0.5),
        "b_ada": nrm(ks[6], (L, N_MOD * D_MODEL), 0.01),
        "w_in": nrm(ks[7], (L, D_MODEL, D_IN), D_MODEL ** -0.5),
        "q_a_norm_w": gain(ks[8], (L, Q_LORA_RANK)),
        "w_q_up": nrm(ks[9], (L, Q_LORA_RANK, MLA_HEADS * QK_HEAD_DIM), Q_LORA_RANK ** -0.5),
        "kv_a_norm_w": gain(ks[10], (L, KV_LORA_RANK)),
        "w_kv_up": nrm(ks[11], (L, KV_LORA_RANK, MLA_HEADS * (QK_NOPE_DIM + V_HEAD_DIM)), KV_LORA_RANK ** -0.5),
        "q_nope_norm_w": gain(ks[12], (L, QK_NOPE_DIM)),
        "q_pe_norm_w": gain(ks[13], (L, QK_ROPE_DIM)),
        "k_nope_norm_w": gain(ks[14], (L, QK_NOPE_DIM)),
        "k_pe_norm_w": gain(ks[15], (L, QK_ROPE_DIM)),
        "conv_w": nrm(ks[16], (L, CONV_WIDTH, D_CONV), CONV_WIDTH ** -0.5),
        "conv_b": nrm(ks[17], (L, D_CONV), 0.01),
        "dt_bias": dt_bias,
        "a_log": jnp.log(jax.random.uniform(ks[19], (L, SSD_HEADS), f32, 1.0, 16.0)),
        "d_skip": 1.0 + 0.1 * jax.random.normal(ks[20], (L, SSD_HEADS), f32),
        "ssd_norm_w": gain(ks[21], (L, D_SSD)),
        "w_out": nrm(ks[22], (L, D_MIX, D_MODEL), D_MIX ** -0.5),
        "w_gate_up": nrm(ks[23], (L, D_MODEL, 2 * D_FF), D_MODEL ** -0.5),
        "w_down": nrm(ks[24], (L, D_FF, D_MODEL), D_FF ** -0.5),
    }


def _fwd_reference(x, c, positions, norm1_w, norm2_w, w_ada, b_ada, w_in, q_a_norm_w, w_q_up,
              kv_a_norm_w, w_kv_up, q_nope_norm_w, q_pe_norm_w, k_nope_norm_w, k_pe_norm_w,
              conv_w, conv_b, dt_bias, a_log, d_skip, ssd_norm_w, w_out, w_gate_up, w_down):
    inv_freq = 1.0 / (ROPE_THETA ** (jnp.arange(0, QK_ROPE_DIM, 2, dtype=jnp.float32) / QK_ROPE_DIM))
    ang = positions.astype(jnp.float32)[..., None] * inv_freq
    cos = jnp.cos(ang)[:, :, None, :].astype(x.dtype)
    sin = jnp.sin(ang)[:, :, None, :].astype(x.dtype)
    c_act = jax.nn.silu(c)
    for l in range(DEPTH):
        mod = (c_act @ w_ada[l] + b_ada[l])[:, None, :]
        sh1, sc1, g1, sh2, sc2, g2 = jnp.split(mod, N_MOD, axis=-1)
        h = rmsnorm(x, norm1_w[l]) * (1.0 + sc1) + sh1
        x = x + g1 * hybrid_mixer(h, cos, sin, w_in[l], q_a_norm_w[l], w_q_up[l], kv_a_norm_w[l],
                                  w_kv_up[l], q_nope_norm_w[l], q_pe_norm_w[l], k_nope_norm_w[l],
                                  k_pe_norm_w[l], conv_w[l], conv_b[l], dt_bias[l], a_log[l],
                                  d_skip[l], ssd_norm_w[l], w_out[l])
        h = rmsnorm(x, norm2_w[l]) * (1.0 + sc2) + sh2
        x = x + g2 * swiglu(h, w_gate_up[l], w_down[l])
    return x


import jax as _jax
import jax.numpy as _jnp

TWIN_FORMAT = 'train_step'
FWD_PARAMS = ['x', 'c', 'positions', 'norm1_w', 'norm2_w', 'w_ada', 'b_ada', 'w_in', 'q_a_norm_w', 'w_q_up', 'kv_a_norm_w', 'w_kv_up', 'q_nope_norm_w', 'q_pe_norm_w', 'k_nope_norm_w', 'k_pe_norm_w', 'conv_w', 'conv_b', 'dt_bias', 'a_log', 'd_skip', 'ssd_norm_w', 'w_out', 'w_gate_up', 'w_down']
TWIN_WEIGHTS = ['norm1_w', 'norm2_w', 'w_ada', 'b_ada', 'w_in', 'q_a_norm_w', 'w_q_up', 'kv_a_norm_w', 'w_kv_up', 'q_nope_norm_w', 'q_pe_norm_w', 'k_nope_norm_w', 'k_pe_norm_w', 'conv_w', 'conv_b', 'dt_bias', 'a_log', 'd_skip', 'ssd_norm_w', 'w_out', 'w_gate_up', 'w_down']
TWIN_DIFF_INPUT = 'x'
TWIN_INPUTS = ['x', 'c', 'positions', 'norm1_w', 'norm2_w', 'w_ada', 'b_ada', 'w_in', 'q_a_norm_w', 'w_q_up', 'kv_a_norm_w', 'w_kv_up', 'q_nope_norm_w', 'q_pe_norm_w', 'k_nope_norm_w', 'k_pe_norm_w', 'conv_w', 'conv_b', 'dt_bias', 'a_log', 'd_skip', 'ssd_norm_w', 'w_out', 'w_gate_up', 'w_down', 'loss_target', 'm_norm1_w', 'm_norm2_w', 'm_w_ada', 'm_b_ada', 'm_w_in', 'm_q_a_norm_w', 'm_w_q_up', 'm_kv_a_norm_w', 'm_w_kv_up', 'm_q_nope_norm_w', 'm_q_pe_norm_w', 'm_k_nope_norm_w', 'm_k_pe_norm_w', 'm_conv_w', 'm_conv_b', 'm_dt_bias', 'm_a_log', 'm_d_skip', 'm_ssd_norm_w', 'm_w_out', 'm_w_gate_up', 'm_w_down', 'v_norm1_w', 'v_norm2_w', 'v_w_ada', 'v_b_ada', 'v_w_in', 'v_q_a_norm_w', 'v_w_q_up', 'v_kv_a_norm_w', 'v_w_kv_up', 'v_q_nope_norm_w', 'v_q_pe_norm_w', 'v_k_nope_norm_w', 'v_k_pe_norm_w', 'v_conv_w', 'v_conv_b', 'v_dt_bias', 'v_a_log', 'v_d_skip', 'v_ssd_norm_w', 'v_w_out', 'v_w_gate_up', 'v_w_down']
TWIN_OUTPUTS = ['loss', 'grad_x', 'grad_norm1_w', 'grad_norm2_w', 'grad_w_ada', 'grad_b_ada', 'grad_w_in', 'grad_q_a_norm_w', 'grad_w_q_up', 'grad_kv_a_norm_w', 'grad_w_kv_up', 'grad_q_nope_norm_w', 'grad_q_pe_norm_w', 'grad_k_nope_norm_w', 'grad_k_pe_norm_w', 'grad_conv_w', 'grad_conv_b', 'grad_dt_bias', 'grad_a_log', 'grad_d_skip', 'grad_ssd_norm_w', 'grad_w_out', 'grad_w_gate_up', 'grad_w_down', 'delta_norm1_w', 'delta_norm2_w', 'delta_w_ada', 'delta_b_ada', 'delta_w_in', 'delta_q_a_norm_w', 'delta_w_q_up', 'delta_kv_a_norm_w', 'delta_w_kv_up', 'delta_q_nope_norm_w', 'delta_q_pe_norm_w', 'delta_k_nope_norm_w', 'delta_k_pe_norm_w', 'delta_conv_w', 'delta_conv_b', 'delta_dt_bias', 'delta_a_log', 'delta_d_skip', 'delta_ssd_norm_w', 'delta_w_out', 'delta_w_gate_up', 'delta_w_down', 'new_m_norm1_w', 'new_m_norm2_w', 'new_m_w_ada', 'new_m_b_ada', 'new_m_w_in', 'new_m_q_a_norm_w', 'new_m_w_q_up', 'new_m_kv_a_norm_w', 'new_m_w_kv_up', 'new_m_q_nope_norm_w', 'new_m_q_pe_norm_w', 'new_m_k_nope_norm_w', 'new_m_k_pe_norm_w', 'new_m_conv_w', 'new_m_conv_b', 'new_m_dt_bias', 'new_m_a_log', 'new_m_d_skip', 'new_m_ssd_norm_w', 'new_m_w_out', 'new_m_w_gate_up', 'new_m_w_down', 'new_v_norm1_w', 'new_v_norm2_w', 'new_v_w_ada', 'new_v_b_ada', 'new_v_w_in', 'new_v_q_a_norm_w', 'new_v_w_q_up', 'new_v_kv_a_norm_w', 'new_v_w_kv_up', 'new_v_q_nope_norm_w', 'new_v_q_pe_norm_w', 'new_v_k_nope_norm_w', 'new_v_k_pe_norm_w', 'new_v_conv_w', 'new_v_conv_b', 'new_v_dt_bias', 'new_v_a_log', 'new_v_d_skip', 'new_v_ssd_norm_w', 'new_v_w_out', 'new_v_w_gate_up', 'new_v_w_down']
TWIN_LEAF_KINDS = {'loss': 'loss', 'grad_x': 'grad_x', 'grad_norm1_w': 'grad_w', 'grad_norm2_w': 'grad_w', 'grad_w_ada': 'grad_w', 'grad_b_ada': 'grad_w', 'grad_w_in': 'grad_w', 'grad_q_a_norm_w': 'grad_w', 'grad_w_q_up': 'grad_w', 'grad_kv_a_norm_w': 'grad_w', 'grad_w_kv_up': 'grad_w', 'grad_q_nope_norm_w': 'grad_w', 'grad_q_pe_norm_w': 'grad_w', 'grad_k_nope_norm_w': 'grad_w', 'grad_k_pe_norm_w': 'grad_w', 'grad_conv_w': 'grad_w', 'grad_conv_b': 'grad_w', 'grad_dt_bias': 'grad_w', 'grad_a_log': 'grad_w', 'grad_d_skip': 'grad_w', 'grad_ssd_norm_w': 'grad_w', 'grad_w_out': 'grad_w', 'grad_w_gate_up': 'grad_w', 'grad_w_down': 'grad_w', 'delta_norm1_w': 'delta_w', 'delta_norm2_w': 'delta_w', 'delta_w_ada': 'delta_w', 'delta_b_ada': 'delta_w', 'delta_w_in': 'delta_w', 'delta_q_a_norm_w': 'delta_w', 'delta_w_q_up': 'delta_w', 'delta_kv_a_norm_w': 'delta_w', 'delta_w_kv_up': 'delta_w', 'delta_q_nope_norm_w': 'delta_w', 'delta_q_pe_norm_w': 'delta_w', 'delta_k_nope_norm_w': 'delta_w', 'delta_k_pe_norm_w': 'delta_w', 'delta_conv_w': 'delta_w', 'delta_conv_b': 'delta_w', 'delta_dt_bias': 'delta_w', 'delta_a_log': 'delta_w', 'delta_d_skip': 'delta_w', 'delta_ssd_norm_w': 'delta_w', 'delta_w_out': 'delta_w', 'delta_w_gate_up': 'delta_w', 'delta_w_down': 'delta_w', 'new_m_norm1_w': 'new_m', 'new_m_norm2_w': 'new_m', 'new_m_w_ada': 'new_m', 'new_m_b_ada': 'new_m', 'new_m_w_in': 'new_m', 'new_m_q_a_norm_w': 'new_m', 'new_m_w_q_up': 'new_m', 'new_m_kv_a_norm_w': 'new_m', 'new_m_w_kv_up': 'new_m', 'new_m_q_nope_norm_w': 'new_m', 'new_m_q_pe_norm_w': 'new_m', 'new_m_k_nope_norm_w': 'new_m', 'new_m_k_pe_norm_w': 'new_m', 'new_m_conv_w': 'new_m', 'new_m_conv_b': 'new_m', 'new_m_dt_bias': 'new_m', 'new_m_a_log': 'new_m', 'new_m_d_skip': 'new_m', 'new_m_ssd_norm_w': 'new_m', 'new_m_w_out': 'new_m', 'new_m_w_gate_up': 'new_m', 'new_m_w_down': 'new_m', 'new_v_norm1_w': 'new_v', 'new_v_norm2_w': 'new_v', 'new_v_w_ada': 'new_v', 'new_v_b_ada': 'new_v', 'new_v_w_in': 'new_v', 'new_v_q_a_norm_w': 'new_v', 'new_v_w_q_up': 'new_v', 'new_v_kv_a_norm_w': 'new_v', 'new_v_w_kv_up': 'new_v', 'new_v_q_nope_norm_w': 'new_v', 'new_v_q_pe_norm_w': 'new_v', 'new_v_k_nope_norm_w': 'new_v', 'new_v_k_pe_norm_w': 'new_v', 'new_v_conv_w': 'new_v', 'new_v_conv_b': 'new_v', 'new_v_dt_bias': 'new_v', 'new_v_a_log': 'new_v', 'new_v_d_skip': 'new_v', 'new_v_ssd_norm_w': 'new_v', 'new_v_w_out': 'new_v', 'new_v_w_gate_up': 'new_v', 'new_v_w_down': 'new_v'}


def _forward(args):
    return _fwd_reference(*[args[k] for k in FWD_PARAMS])


def _output_shape():
    out = _jax.eval_shape(lambda: _forward(_fwd_setup_inputs(0)))
    return out.shape, out.dtype

N_MICROBATCH = 1
ADAM_LR = 0.001
ADAM_B1 = 0.9
ADAM_B2 = 0.999
ADAM_EPS = 1e-08
ADAM_WD = 0.01
ADAM_STEP = 10
PER_EXAMPLE_BATCH_AXIS = {'x': 0, 'c': 0, 'positions': 0, 'loss_target': 0}
SHARED_INPUTS = []
_WEIGHT_DTYPES = {'norm1_w': _jnp.float32, 'norm2_w': _jnp.float32, 'w_ada': _jnp.float32, 'b_ada': _jnp.float32, 'w_in': _jnp.float32, 'q_a_norm_w': _jnp.float32, 'w_q_up': _jnp.float32, 'kv_a_norm_w': _jnp.float32, 'w_kv_up': _jnp.float32, 'q_nope_norm_w': _jnp.float32, 'q_pe_norm_w': _jnp.float32, 'k_nope_norm_w': _jnp.float32, 'k_pe_norm_w': _jnp.float32, 'conv_w': _jnp.float32, 'conv_b': _jnp.float32, 'dt_bias': _jnp.float32, 'a_log': _jnp.float32, 'd_skip': _jnp.float32, 'ssd_norm_w': _jnp.float32, 'w_out': _jnp.float32, 'w_gate_up': _jnp.float32, 'w_down': _jnp.float32}
MOMENT_SCALE = {'norm1_w': 1.214139e-01, 'norm2_w': 3.260611e+00, 'w_ada': 1.028550e+00, 'b_ada': 2.110510e+00, 'w_in': 1.693156e-01, 'q_a_norm_w': 2.625349e-02, 'w_q_up': 1.426781e-02, 'kv_a_norm_w': 1.181448e+00, 'w_kv_up': 1.809480e-01, 'q_nope_norm_w': 9.877539e-02, 'q_pe_norm_w': 7.522626e-02, 'k_nope_norm_w': 1.006337e-01, 'k_pe_norm_w': 7.745381e-02, 'conv_w': 1.718720e-01, 'conv_b': 3.205594e-01, 'dt_bias': 6.602847e-01, 'a_log': 8.451592e-01, 'd_skip': 8.420568e-01, 'ssd_norm_w': 4.484713e+00, 'w_out': 2.319892e-01, 'w_gate_up': 6.301957e-02, 'w_down': 8.539794e-02}


def _to_microbatches(a, axis):
    t = _jnp.moveaxis(a, axis, 0)
    t = t.reshape((N_MICROBATCH, t.shape[0] // N_MICROBATCH) + t.shape[1:])
    return _jnp.moveaxis(t, 1, axis + 1)


def setup_inputs(seed: int = 0) -> dict:
    inp = _fwd_setup_inputs(seed)
    key = _jax.random.fold_in(_jax.random.key(seed), 7919)
    shape, _ = _output_shape()
    out = dict(inp)
    out["loss_target"] = _jax.random.normal(_jax.random.fold_in(key, 0), shape, _jnp.float32)
    for i, name in enumerate(TWIN_WEIGHTS):
        w = inp[name].astype(_jnp.float32)
        if MOMENT_SCALE is None:
            s = _jnp.sqrt(_jnp.mean(_jnp.square(w)) + 1e-30)
        else:
            s = MOMENT_SCALE[name]
        km, kv = _jax.random.split(_jax.random.fold_in(key, i + 1))
        out[name] = w
        out["m_" + name] = s * _jax.random.normal(km, w.shape, _jnp.float32)
        out["v_" + name] = (s * s) * _jax.random.uniform(kv, w.shape, _jnp.float32, 0.5, 1.5)
    if N_MICROBATCH > 1:
        for name, axis in PER_EXAMPLE_BATCH_AXIS.items():
            out[name] = _to_microbatches(out[name], axis)
    return {'x': out['x'], 'c': out['c'], 'positions': out['positions'], 'norm1_w': out['norm1_w'], 'norm2_w': out['norm2_w'], 'w_ada': out['w_ada'], 'b_ada': out['b_ada'], 'w_in': out['w_in'], 'q_a_norm_w': out['q_a_norm_w'], 'w_q_up': out['w_q_up'], 'kv_a_norm_w': out['kv_a_norm_w'], 'w_kv_up': out['w_kv_up'], 'q_nope_norm_w': out['q_nope_norm_w'], 'q_pe_norm_w': out['q_pe_norm_w'], 'k_nope_norm_w': out['k_nope_norm_w'], 'k_pe_norm_w': out['k_pe_norm_w'], 'conv_w': out['conv_w'], 'conv_b': out['conv_b'], 'dt_bias': out['dt_bias'], 'a_log': out['a_log'], 'd_skip': out['d_skip'], 'ssd_norm_w': out['ssd_norm_w'], 'w_out': out['w_out'], 'w_gate_up': out['w_gate_up'], 'w_down': out['w_down'], 'loss_target': out['loss_target'], 'm_norm1_w': out['m_norm1_w'], 'm_norm2_w': out['m_norm2_w'], 'm_w_ada': out['m_w_ada'], 'm_b_ada': out['m_b_ada'], 'm_w_in': out['m_w_in'], 'm_q_a_norm_w': out['m_q_a_norm_w'], 'm_w_q_up': out['m_w_q_up'], 'm_kv_a_norm_w': out['m_kv_a_norm_w'], 'm_w_kv_up': out['m_w_kv_up'], 'm_q_nope_norm_w': out['m_q_nope_norm_w'], 'm_q_pe_norm_w': out['m_q_pe_norm_w'], 'm_k_nope_norm_w': out['m_k_nope_norm_w'], 'm_k_pe_norm_w': out['m_k_pe_norm_w'], 'm_conv_w': out['m_conv_w'], 'm_conv_b': out['m_conv_b'], 'm_dt_bias': out['m_dt_bias'], 'm_a_log': out['m_a_log'], 'm_d_skip': out['m_d_skip'], 'm_ssd_norm_w': out['m_ssd_norm_w'], 'm_w_out': out['m_w_out'], 'm_w_gate_up': out['m_w_gate_up'], 'm_w_down': out['m_w_down'], 'v_norm1_w': out['v_norm1_w'], 'v_norm2_w': out['v_norm2_w'], 'v_w_ada': out['v_w_ada'], 'v_b_ada': out['v_b_ada'], 'v_w_in': out['v_w_in'], 'v_q_a_norm_w': out['v_q_a_norm_w'], 'v_w_q_up': out['v_w_q_up'], 'v_kv_a_norm_w': out['v_kv_a_norm_w'], 'v_w_kv_up': out['v_w_kv_up'], 'v_q_nope_norm_w': out['v_q_nope_norm_w'], 'v_q_pe_norm_w': out['v_q_pe_norm_w'], 'v_k_nope_norm_w': out['v_k_nope_norm_w'], 'v_k_pe_norm_w': out['v_k_pe_norm_w'], 'v_conv_w': out['v_conv_w'], 'v_conv_b': out['v_conv_b'], 'v_dt_bias': out['v_dt_bias'], 'v_a_log': out['v_a_log'], 'v_d_skip': out['v_d_skip'], 'v_ssd_norm_w': out['v_ssd_norm_w'], 'v_w_out': out['v_w_out'], 'v_w_gate_up': out['v_w_gate_up'], 'v_w_down': out['v_w_down']}


def _loss(weights, diff, rest, loss_target):
    with _jax.named_scope("forward"):
        args = {**rest, TWIN_DIFF_INPUT: diff, **{k: w.astype(_WEIGHT_DTYPES[k]) for k, w in weights.items()}}
        y = _forward(args)
    with _jax.named_scope("loss_head"):
        err = _jnp.square(y.astype(_jnp.float32) - loss_target)
        return 0.5 * _jnp.sum(_jnp.mean(err, axis=-1)) if err.ndim else 0.5 * err


def _adamw(w, g, m, v):
    m = ADAM_B1 * m + (1.0 - ADAM_B1) * g
    v = ADAM_B2 * v + (1.0 - ADAM_B2) * _jnp.square(g)
    m_hat = m / (1.0 - ADAM_B1 ** ADAM_STEP)
    v_hat = v / (1.0 - ADAM_B2 ** ADAM_STEP)
    delta = -ADAM_LR * (m_hat / (_jnp.sqrt(v_hat) + ADAM_EPS) + ADAM_WD * w)
    return delta, m, v


def reference(x, c, positions, norm1_w, norm2_w, w_ada, b_ada, w_in, q_a_norm_w, w_q_up, kv_a_norm_w, w_kv_up, q_nope_norm_w, q_pe_norm_w, k_nope_norm_w, k_pe_norm_w, conv_w, conv_b, dt_bias, a_log, d_skip, ssd_norm_w, w_out, w_gate_up, w_down, loss_target, m_norm1_w, m_norm2_w, m_w_ada, m_b_ada, m_w_in, m_q_a_norm_w, m_w_q_up, m_kv_a_norm_w, m_w_kv_up, m_q_nope_norm_w, m_q_pe_norm_w, m_k_nope_norm_w, m_k_pe_norm_w, m_conv_w, m_conv_b, m_dt_bias, m_a_log, m_d_skip, m_ssd_norm_w, m_w_out, m_w_gate_up, m_w_down, v_norm1_w, v_norm2_w, v_w_ada, v_b_ada, v_w_in, v_q_a_norm_w, v_w_q_up, v_kv_a_norm_w, v_w_kv_up, v_q_nope_norm_w, v_q_pe_norm_w, v_k_nope_norm_w, v_k_pe_norm_w, v_conv_w, v_conv_b, v_dt_bias, v_a_log, v_d_skip, v_ssd_norm_w, v_w_out, v_w_gate_up, v_w_down):
    given = dict(x=x, c=c, positions=positions, norm1_w=norm1_w, norm2_w=norm2_w, w_ada=w_ada, b_ada=b_ada, w_in=w_in, q_a_norm_w=q_a_norm_w, w_q_up=w_q_up, kv_a_norm_w=kv_a_norm_w, w_kv_up=w_kv_up, q_nope_norm_w=q_nope_norm_w, q_pe_norm_w=q_pe_norm_w, k_nope_norm_w=k_nope_norm_w, k_pe_norm_w=k_pe_norm_w, conv_w=conv_w, conv_b=conv_b, dt_bias=dt_bias, a_log=a_log, d_skip=d_skip, ssd_norm_w=ssd_norm_w, w_out=w_out, w_gate_up=w_gate_up, w_down=w_down, loss_target=loss_target, m_norm1_w=m_norm1_w, m_norm2_w=m_norm2_w, m_w_ada=m_w_ada, m_b_ada=m_b_ada, m_w_in=m_w_in, m_q_a_norm_w=m_q_a_norm_w, m_w_q_up=m_w_q_up, m_kv_a_norm_w=m_kv_a_norm_w, m_w_kv_up=m_w_kv_up, m_q_nope_norm_w=m_q_nope_norm_w, m_q_pe_norm_w=m_q_pe_norm_w, m_k_nope_norm_w=m_k_nope_norm_w, m_k_pe_norm_w=m_k_pe_norm_w, m_conv_w=m_conv_w, m_conv_b=m_conv_b, m_dt_bias=m_dt_bias, m_a_log=m_a_log, m_d_skip=m_d_skip, m_ssd_norm_w=m_ssd_norm_w, m_w_out=m_w_out, m_w_gate_up=m_w_gate_up, m_w_down=m_w_down, v_norm1_w=v_norm1_w, v_norm2_w=v_norm2_w, v_w_ada=v_w_ada, v_b_ada=v_b_ada, v_w_in=v_w_in, v_q_a_norm_w=v_q_a_norm_w, v_w_q_up=v_w_q_up, v_kv_a_norm_w=v_kv_a_norm_w, v_w_kv_up=v_w_kv_up, v_q_nope_norm_w=v_q_nope_norm_w, v_q_pe_norm_w=v_q_pe_norm_w, v_k_nope_norm_w=v_k_nope_norm_w, v_k_pe_norm_w=v_k_pe_norm_w, v_conv_w=v_conv_w, v_conv_b=v_conv_b, v_dt_bias=v_dt_bias, v_a_log=v_a_log, v_d_skip=v_d_skip, v_ssd_norm_w=v_ssd_norm_w, v_w_out=v_w_out, v_w_gate_up=v_w_gate_up, v_w_down=v_w_down)
    weights = {n: given[n] for n in TWIN_WEIGHTS}
    shared = {n: given[n] for n in SHARED_INPUTS}
    per_example = {n: given[n] for n in ['x', 'c', 'positions']}
    grad_fn = _jax.value_and_grad(_loss, argnums=(0, 1))

    def one_microbatch(ex, loss_target):
        ex = dict(ex)
        diff = ex.pop(TWIN_DIFF_INPUT)
        return grad_fn(weights, diff, {**shared, **ex}, loss_target)

    if N_MICROBATCH == 1:
        loss, (grad_w, grad_x) = one_microbatch(per_example, given["loss_target"])
    else:
        def body(carry, xs):
            loss_sum, grad_sum = carry
            l_k, (gw_k, gx_k) = one_microbatch(xs[0], xs[1])
            with _jax.named_scope("update"):
                return (loss_sum + l_k, _jax.tree.map(_jnp.add, grad_sum, gw_k)), gx_k

        init = (_jnp.zeros((), _jnp.float32), _jax.tree.map(_jnp.zeros_like, weights))
        (loss, grad_w), grad_x = _jax.lax.scan(body, init, (per_example, given["loss_target"]))
    with _jax.named_scope("update"):
        delta_w, new_m, new_v = {}, {}, {}
        for n in TWIN_WEIGHTS:
            delta_w[n], new_m[n], new_v[n] = _adamw(weights[n], grad_w[n], given["m_" + n], given["v_" + n])
    return (loss, grad_x, *[grad_w[n] for n in TWIN_WEIGHTS], *[delta_w[n] for n in TWIN_WEIGHTS],
            *[new_m[n] for n in TWIN_WEIGHTS], *[new_v[n] for n in TWIN_WEIGHTS])
```

```python
import functools

import jax
import jax.numpy as jnp
from jax import lax
from jax.experimental import pallas as pl
from jax.experimental.pallas import tpu as pltpu

F32 = jnp.float32
BF16 = jnp.bfloat16
MESH = pl.DeviceIdType.MESH

D_MODEL = 1024
DEPTH = 2
HEADS = 8
NOPE = 64
ROPE = 32
QK_DIM = NOPE + ROPE
Q_LORA = 256
KV_LORA = 128
SSD_HEADS = 8
SSD_P = 64
SSD_N = 128
CHUNK = 128
CONV_TAPS = 4
D_SSD = 512
D_CONV = 1024
D_FF = 2816
D_IN = 1960
D_IN_PAD = 2176
EPS = 1e-6
ROPE_THETA = 10000.0
ATT_SCALE = QK_DIM ** -0.5
NEG = -1e30
LANES = 128
VMEM_LIMIT = 48 * 1024 * 1024

ADAM_LR, ADAM_B1, ADAM_B2, ADAM_EPS, ADAM_WD, ADAM_STEP = 0.001, 0.9, 0.999, 1e-08, 0.01, 10

WEIGHTS = ['norm1_w', 'norm2_w', 'w_ada', 'b_ada', 'w_in', 'q_a_norm_w', 'w_q_up', 'kv_a_norm_w', 'w_kv_up',
           'q_nope_norm_w', 'q_pe_norm_w', 'k_nope_norm_w', 'k_pe_norm_w', 'conv_w', 'conv_b', 'dt_bias', 'a_log',
           'd_skip', 'ssd_norm_w', 'w_out', 'w_gate_up', 'w_down']
BIG = [('w_in', (1024, 490), 1), ('w_q_up', (256, 192), 1), ('w_kv_up', (128, 256), 1), ('conv_w', (4, 256), 1),
       ('w_out', (256, 1024), 0), ('w_gate_up', (1024, 1408), 1), ('w_down', (704, 1024), 0)]
BIG_ROWS = 5888
HALF_ROWS = BIG_ROWS // 2
SMALL = [('b_ada', 6144), ('norm1_w', 1024), ('norm2_w', 1024), ('conv_b', 1024), ('ssd_norm_w', 512),
         ('q_a_norm_w', 256), ('kv_a_norm_w', 128), ('q_nope_norm_w', 64), ('q_pe_norm_w', 32),
         ('k_nope_norm_w', 64), ('k_pe_norm_w', 32), ('dt_bias', 8), ('a_log', 8), ('d_skip', 8)]
SMALL_ROWS = 176


def _cp(sem=None, **kw):
    return pltpu.CompilerParams(dimension_semantics=sem, vmem_limit_bytes=VMEM_LIMIT, **kw)


def _dot(a, b, dims, prec=None):
    return lax.dot_general(a, b, (dims, ((), ())), preferred_element_type=F32, precision=prec)


def _tile(dim, target):
    best = 0
    for t in range(LANES, min(dim, target) + 1, LANES):
        if dim % t == 0:
            best = t
    if best < 256 and dim <= 2304:
        return dim
    return best


def _mm(a, b, mode, name, out_dtype=F32):
    if mode == 'nn':
        (M, K), (_, N) = a.shape, b.shape
    elif mode == 'nt':
        (M, K), (N, _) = a.shape, b.shape
    else:
        (K, M), (_, N) = a.shape, b.shape
    tm, tn, tk = _tile(M, 512), _tile(N, 1408), _tile(K, 1408)
    nk = K // tk
    dims = {'nn': ((1,), (0,)), 'nt': ((1,), (1,)), 'tn': ((0,), (0,))}[mode]

    def body(a_ref, b_ref, o_ref, acc):
        k = pl.program_id(2)

        @pl.when(k == 0)
        def _():
            acc[...] = jnp.zeros_like(acc)

        acc[...] += _dot(a_ref[...].astype(BF16), b_ref[...].astype(BF16), dims)

        @pl.when(k == nk - 1)
        def _():
            o_ref[...] = acc[...].astype(o_ref.dtype)

    a_spec = pl.BlockSpec((tk, tm), lambda i, j, k: (k, i)) if mode == 'tn' else pl.BlockSpec((tm, tk), lambda i, j, k: (i, k))
    b_spec = pl.BlockSpec((tn, tk), lambda i, j, k: (j, k)) if mode == 'nt' else pl.BlockSpec((tk, tn), lambda i, j, k: (k, j))
    return pl.pallas_call(
        body, name=name, grid=(M // tm, N // tn, nk),
        in_specs=[a_spec, b_spec], out_specs=pl.BlockSpec((tm, tn), lambda i, j, k: (i, j)),
        out_shape=jax.ShapeDtypeStruct((M, N), out_dtype),
        scratch_shapes=[pltpu.VMEM((tm, tn), F32)],
        compiler_params=_cp(("parallel", "parallel", "arbitrary")),
    )(a, b)


def _rspec(tm, w, cb):
    return pl.BlockSpec((tm, w), lambda i: (i, cb))


def _vspec(shape):
    return pl.BlockSpec(shape, lambda i: (0,) * len(shape))


def _row_fwd(fn, name, rows, vecs, outs, tm=256):
    S = rows[0][0].shape[0]
    tm = min(tm, S)
    nin = len(rows) + len(vecs)

    def body(*refs):
        res = fn(*[r[...] for r in refs[:nin]])
        for o_ref, r in zip(refs[nin:], res):
            o_ref[...] = r.astype(o_ref.dtype)

    return pl.pallas_call(
        body, name=name, grid=(S // tm,),
        in_specs=[_rspec(tm, w, cb) for (_, cb, w) in rows] + [_vspec(v.shape) for v in vecs],
        out_specs=[_rspec(tm, w, 0) for (w, _) in outs],
        out_shape=[jax.ShapeDtypeStruct((S, w), dt) for (w, dt) in outs],
        compiler_params=_cp(("parallel",)),
    )(*[r[0] for r in rows], *vecs)


def _row_bwd(fn, name, rows, vecs, cts, drows, dvecs, tm=256):
    S = rows[0][0].shape[0]
    tm = min(tm, S)
    nr, nv, nc = len(rows), len(vecs), len(cts)
    didx = list(drows) + [nr + j for j in dvecs]

    def body(*refs):
        vals = [r[...] for r in refs[:nr + nv]]
        ct = tuple(r[...].astype(F32) for r in refs[nr + nv:nr + nv + nc])
        outs = refs[nr + nv + nc:]

        def g(*d):
            a = list(vals)
            for k, val in zip(didx, d):
                a[k] = val
            return tuple(fn(*a))

        _, vjp = jax.vjp(g, *[vals[k] for k in didx])
        grads = vjp(ct)
        for o, gr in zip(outs[:len(drows)], grads[:len(drows)]):
            o[...] = gr.astype(o.dtype)

        @pl.when(pl.program_id(0) == 0)
        def _():
            for o in outs[len(drows):]:
                o[...] = jnp.zeros_like(o)

        for o, gr in zip(outs[len(drows):], grads[len(drows):]):
            o[...] += gr

    return pl.pallas_call(
        body, name=name, grid=(S // tm,),
        in_specs=[_rspec(tm, w, cb) for (_, cb, w) in rows] + [_vspec(v.shape) for v in vecs]
        + [_rspec(tm, w, cb) for (_, cb, w) in cts],
        out_specs=[_rspec(tm, rows[k][2], 0) for k in drows] + [_vspec(vecs[j].shape) for j in dvecs],
        out_shape=[jax.ShapeDtypeStruct((S, rows[k][2]), F32) for k in drows]
        + [jax.ShapeDtypeStruct(vecs[j].shape, F32) for j in dvecs],
        compiler_params=_cp(("arbitrary",)),
    )(*[r[0] for r in rows], *vecs, *[c[0] for c in cts])


def _rms(x):
    return x * lax.rsqrt(jnp.mean(x * x, axis=-1, keepdims=True) + EPS)


def fn_norm_mod(x, nw, sc, sh):
    return (_rms(x) * nw * (1.0 + sc) + sh,)


def fn_norm_mod_pass(x, nw, sc, sh):
    return (x, _rms(x) * nw * (1.0 + sc) + sh)


def fn_resid_norm(x, d, g, nw, sc, sh):
    xn = x + g * d
    return (xn, _rms(xn) * nw * (1.0 + sc) + sh)


def fn_lat_norm(qa, kva, qw, kvw):
    return (_rms(qa) * qw, _rms(kva) * kvw)


@functools.partial(jax.custom_vjp, nondiff_argnums=(1,))
def _lroll(x, s):
    return pltpu.roll(x, s, 1)


def _lroll_fwd(x, s):
    return pltpu.roll(x, s, 1), None


def _lroll_bwd(s, _, g):
    return (pltpu.roll(g, (LANES - s) % LANES, 1),)


_lroll.defvjp(_lroll_fwd, _lroll_bwd)


def _lane_masks(shape):
    lane = lax.broadcasted_iota(jnp.int32, shape, 1)
    return (lane < NOPE).astype(F32), ((lane >= NOPE) & (lane < QK_DIM)).astype(F32)


def _rope(t, tc, ts1, ts2):
    return t * tc + _lroll(t, 16) * ts1 + _lroll(t, LANES - 16) * ts2


def fn_qk_prep(q, kv, kpe, tc, ts1, ts2, wq, wk):
    mn, mp = _lane_masks((1, LANES))
    mhi = 1.0 - mn

    def head_norm(t, w):
        rn = lax.rsqrt(jnp.sum(t * t * mn, axis=-1, keepdims=True) * (1.0 / NOPE) + EPS)
        rp = lax.rsqrt(jnp.sum(t * t * mp, axis=-1, keepdims=True) * (1.0 / ROPE) + EPS)
        return t * (rn * mn + rp * mp) * w

    kp = _rope(head_norm(_lroll(kpe, NOPE), wk) * mp, tc, ts1, ts2)
    qs, ks, vs = [], [], []
    for h in range(HEADS):
        qs.append(_rope(head_norm(q[:, h * LANES:(h + 1) * LANES], wq), tc, ts1, ts2))
        t = kv[:, h * LANES:(h + 1) * LANES]
        ks.append(head_norm(t, wk) * mn + kp)
    for hp in range(HEADS // 2):
        t0 = kv[:, (2 * hp) * LANES:(2 * hp + 1) * LANES]
        t1 = kv[:, (2 * hp + 1) * LANES:(2 * hp + 2) * LANES]
        vs.append(_lroll(t0, NOPE) * mn + t1 * mhi)
    return (jnp.concatenate(qs, axis=1), jnp.concatenate(ks, axis=1), jnp.concatenate(vs, axis=1))


def fn_gated_norm(y, z, w):
    u = y * jax.nn.silu(z)
    half = D_SSD // 2
    return (jnp.concatenate([_rms(u[:, :half]), _rms(u[:, half:])], axis=1) * w,)


def fn_gated_mix(y, z, ao, w):
    return (jnp.concatenate([ao, fn_gated_norm(y, z, w)[0]], axis=1),)


def fn_swiglu(g, u):
    return (jax.nn.silu(g) * u,)


def _final(x1, ff, tgt, g2, name):
    S = x1.shape[0]
    tm = min(256, S)

    def body(x_ref, f_ref, t_ref, g_ref, dx_ref, df_ref, dg_ref, l_ref):
        @pl.when(pl.program_id(0) == 0)
        def _():
            dg_ref[...] = jnp.zeros_like(dg_ref)
            l_ref[...] = jnp.zeros_like(l_ref)

        f = f_ref[...]
        g = g_ref[...]
        e = x_ref[...] + g * f - t_ref[...]
        dx = e * (1.0 / D_MODEL)
        dx_ref[...] = dx
        df_ref[...] = g * dx
        dg_ref[...] += jnp.sum(dx * f, axis=0, keepdims=True)
        l_ref[...] += jnp.sum(e * e) * (0.5 / D_MODEL)

    r = _rspec(tm, D_MODEL, 0)
    return pl.pallas_call(
        body, name=name, grid=(S // tm,),
        in_specs=[r, r, r, _vspec((1, D_MODEL))],
        out_specs=[r, r, _vspec((1, D_MODEL)), _vspec((1, LANES))],
        out_shape=[jax.ShapeDtypeStruct((S, D_MODEL), F32), jax.ShapeDtypeStruct((S, D_MODEL), F32),
                   jax.ShapeDtypeStruct((1, D_MODEL), F32), jax.ShapeDtypeStruct((1, LANES), F32)],
        compiler_params=_cp(("arbitrary",)),
    )(x1, ff, tgt, g2)


def _causal_mask(t):
    r = lax.broadcasted_iota(jnp.int32, (t, t), 0)
    c = lax.broadcasted_iota(jnp.int32, (t, t), 1)
    return c <= r


def _attn_fwd(qf, kf, vv, name):
    S = qf.shape[0]
    T = min(256, S)

    def body(q_ref, k_ref, v_ref, o_ref, l_ref):
        i = pl.program_id(1)
        causal = _causal_mask(T)
        for hh in range(2):
            q = q_ref[:, hh * LANES:(hh + 1) * LANES]

            def blk(j, carry, masked):
                m, l, acc = carry
                off = pl.multiple_of(j * T, T)
                kj = k_ref[pl.ds(off, T), hh * LANES:(hh + 1) * LANES]
                vj = v_ref[pl.ds(off, T), hh * 64:(hh + 1) * 64]
                s = _dot(q, kj, ((1,), (1,))) * ATT_SCALE
                if masked:
                    s = jnp.where(causal, s, NEG)
                mn = jnp.maximum(m, jnp.max(s, axis=1, keepdims=True))
                p = jnp.exp(s - mn)
                al = jnp.exp(m - mn)
                return mn, al * l + jnp.sum(p, axis=1, keepdims=True), al * acc + _dot(p.astype(BF16), vj, ((1,), (0,)))

            init = (jnp.full((T, 1), NEG, F32), jnp.zeros((T, 1), F32), jnp.zeros((T, 64), F32))
            carry = lax.fori_loop(0, i, lambda j, cr: blk(j, cr, False), init)
            m, l, acc = blk(i, carry, True)
            o_ref[:, hh * 64:(hh + 1) * 64] = acc / l
            l_ref[:, hh * 64:(hh + 1) * 64] = jnp.broadcast_to(m + jnp.log(l), (T, 64))

    return pl.pallas_call(
        body, name=name, grid=(HEADS // 2, S // T),
        in_specs=[pl.BlockSpec((T, 256), lambda h, i: (i, h)), pl.BlockSpec((S, 256), lambda h, i: (0, h)),
                  pl.BlockSpec((S, LANES), lambda h, i: (0, h))],
        out_specs=[pl.BlockSpec((T, LANES), lambda h, i: (i, h)), pl.BlockSpec((T, LANES), lambda h, i: (i, h))],
        out_shape=[jax.ShapeDtypeStruct((S, D_SSD), F32), jax.ShapeDtypeStruct((S, D_SSD), F32)],
        compiler_params=_cp(("parallel", "parallel")),
    )(qf, kf, vv)


def _attn_bwd_dq(qf, kf, vv, o, lse, dmix, name):
    S = qf.shape[0]
    T = min(256, S)

    def body(q_ref, k_ref, v_ref, o_ref, l_ref, do_ref, dq_ref):
        i = pl.program_id(1)
        causal = _causal_mask(T)
        for hh in range(2):
            q = q_ref[:, hh * LANES:(hh + 1) * LANES]
            do = do_ref[:, hh * 64:(hh + 1) * 64]
            delta = jnp.sum(do * o_ref[:, hh * 64:(hh + 1) * 64], axis=1, keepdims=True)
            lse_c = l_ref[:, hh * 64:hh * 64 + 1]
            dob = do.astype(BF16)

            def blk(j, dq, masked):
                off = pl.multiple_of(j * T, T)
                kj = k_ref[pl.ds(off, T), hh * LANES:(hh + 1) * LANES]
                vj = v_ref[pl.ds(off, T), hh * 64:(hh + 1) * 64]
                s = _dot(q, kj, ((1,), (1,))) * ATT_SCALE
                if masked:
                    s = jnp.where(causal, s, NEG)
                p = jnp.exp(s - lse_c)
                dp = _dot(dob, vj, ((1,), (1,)))
                ds = p * (dp - delta) * ATT_SCALE
                return dq + _dot(ds.astype(BF16), kj, ((1,), (0,)))

            dq = lax.fori_loop(0, i, lambda j, a: blk(j, a, False), jnp.zeros((T, LANES), F32))
            dq_ref[:, hh * LANES:(hh + 1) * LANES] = blk(i, dq, True)

    return pl.pallas_call(
        body, name=name, grid=(HEADS // 2, S // T),
        in_specs=[pl.BlockSpec((T, 256), lambda h, i: (i, h)), pl.BlockSpec((S, 256), lambda h, i: (0, h)),
                  pl.BlockSpec((S, LANES), lambda h, i: (0, h)), pl.BlockSpec((T, LANES), lambda h, i: (i, h)),
                  pl.BlockSpec((T, LANES), lambda h, i: (i, h)), pl.BlockSpec((T, LANES), lambda h, i: (i, h))],
        out_specs=pl.BlockSpec((T, 256), lambda h, i: (i, h)),
        out_shape=jax.ShapeDtypeStruct((S, D_MODEL), F32),
        compiler_params=_cp(("parallel", "parallel")),
    )(qf, kf, vv, o, lse, dmix)


def _attn_bwd_dkv(qf, kf, vv, o, lse, dmix, name):
    S = qf.shape[0]
    T = min(256, S)
    nq = S // T

    def body(q_ref, k_ref, v_ref, o_ref, l_ref, do_ref, dk_ref, dv_ref):
        j = pl.program_id(1)
        causal = _causal_mask(T)
        for hh in range(2):
            kj = k_ref[:, hh * LANES:(hh + 1) * LANES]
            vj = v_ref[:, hh * 64:(hh + 1) * 64]

            def blk(i, carry, masked):
                dk, dv = carry
                off = pl.multiple_of(i * T, T)
                q = q_ref[pl.ds(off, T), hh * LANES:(hh + 1) * LANES]
                do = do_ref[pl.ds(off, T), hh * 64:(hh + 1) * 64]
                delta = jnp.sum(do * o_ref[pl.ds(off, T), hh * 64:(hh + 1) * 64], axis=1, keepdims=True)
                lse_c = l_ref[pl.ds(off, T), hh * 64:hh * 64 + 1]
                dob = do.astype(BF16)
                s = _dot(q, kj, ((1,), (1,))) * ATT_SCALE
                if masked:
                    s = jnp.where(causal, s, NEG)
                p = jnp.exp(s - lse_c)
                dp = _dot(dob, vj, ((1,), (1,)))
                ds = p * (dp - delta) * ATT_SCALE
                dv = dv + _dot(p.astype(BF16), dob, ((0,), (0,)))
                dk = dk + _dot(ds.astype(BF16), q, ((0,), (0,)))
                return dk, dv

            carry = blk(j, (jnp.zeros((T, LANES), F32), jnp.zeros((T, 64), F32)), True)
            dk, dv = lax.fori_loop(j + 1, nq, lambda i, cr: blk(i, cr, False), carry)
            dk_ref[:, hh * LANES:(hh + 1) * LANES] = dk
            dv_ref[:, hh * 64:(hh + 1) * 64] = dv

    return pl.pallas_call(
        body, name=name, grid=(HEADS // 2, nq),
        in_specs=[pl.BlockSpec((S, 256), lambda h, j: (0, h)), pl.BlockSpec((T, 256), lambda h, j: (j, h)),
                  pl.BlockSpec((T, LANES), lambda h, j: (j, h)), pl.BlockSpec((S, LANES), lambda h, j: (0, h)),
                  pl.BlockSpec((S, LANES), lambda h, j: (0, h)), pl.BlockSpec((S, LANES), lambda h, j: (0, h))],
        out_specs=[pl.BlockSpec((T, 256), lambda h, j: (j, h)), pl.BlockSpec((T, LANES), lambda h, j: (j, h))],
        out_shape=[jax.ShapeDtypeStruct((S, D_MODEL), F32), jax.ShapeDtypeStruct((S, D_SSD), F32)],
        compiler_params=_cp(("parallel", "parallel")),
    )(qf, kf, vv, o, lse, dmix)


def _shift_down(x, s):
    if s == 0:
        return x
    rows = lax.broadcasted_iota(jnp.int32, x.shape, 0)
    return jnp.where(rows >= s, pltpu.roll(x, s, 0), 0.0)


def _shift_up(x, s):
    if s == 0:
        return x
    n = x.shape[0]
    rows = lax.broadcasted_iota(jnp.int32, x.shape, 0)
    return jnp.where(rows < n - s, pltpu.roll(x, n - s, 0), 0.0)


def _conv_fwd(proj, cvec, name):
    S = proj.shape[0]

    def body(x_ref, c_ref, o_ref):
        x = x_ref[...]
        y = jnp.broadcast_to(c_ref[4:5, :], x.shape)
        for k in range(CONV_TAPS):
            y = y + c_ref[k:k + 1, :] * _shift_down(x, CONV_TAPS - 1 - k)
        o_ref[...] = y * jax.nn.sigmoid(y)

    return pl.pallas_call(
        body, name=name, grid=(D_CONV // LANES,),
        in_specs=[pl.BlockSpec((S, LANES), lambda j: (0, 8 + j)), pl.BlockSpec((8, LANES), lambda j: (0, j))],
        out_specs=pl.BlockSpec((S, LANES), lambda j: (0, j)),
        out_shape=jax.ShapeDtypeStruct((S, D_CONV), F32),
        compiler_params=_cp(("parallel",)),
    )(proj, cvec)


def _conv_bwd(proj, cvec, dact, name):
    S = proj.shape[0]

    def body(x_ref, c_ref, d_ref, dx_ref, dc_ref):
        x = x_ref[...]
        y = jnp.broadcast_to(c_ref[4:5, :], x.shape)
        for k in range(CONV_TAPS):
            y = y + c_ref[k:k + 1, :] * _shift_down(x, CONV_TAPS - 1 - k)
        sg = jax.nn.sigmoid(y)
        dy = d_ref[...] * (sg * (1.0 + y * (1.0 - sg)))
        dx = jnp.zeros_like(x)
        for k in range(CONV_TAPS):
            s = CONV_TAPS - 1 - k
            dx = dx + c_ref[k:k + 1, :] * _shift_up(dy, s)
            dc_ref[k:k + 1, :] = jnp.sum(dy * _shift_down(x, s), axis=0, keepdims=True)
        dx_ref[...] = dx
        dc_ref[4:5, :] = jnp.sum(dy, axis=0, keepdims=True)
        dc_ref[5:8, :] = jnp.zeros((3, LANES), F32)

    return pl.pallas_call(
        body, name=name, grid=(D_CONV // LANES,),
        in_specs=[pl.BlockSpec((S, LANES), lambda j: (0, 8 + j)), pl.BlockSpec((8, LANES), lambda j: (0, j)),
                  pl.BlockSpec((S, LANES), lambda j: (0, j))],
        out_specs=[pl.BlockSpec((S, LANES), lambda j: (0, j)), pl.BlockSpec((8, LANES), lambda j: (0, j))],
        out_shape=[jax.ShapeDtypeStruct((S, D_CONV), F32), jax.ShapeDtypeStruct((8, D_CONV), F32)],
        compiler_params=_cp(("parallel",)),
    )(proj, cvec, dact)


def fn_ssd_chunk(xs, bm, cm, dtr, state, vecs):
    Q = CHUNK
    dt = jax.nn.softplus(dtr + vecs[0:1])
    a = -jnp.exp(vecs[1:2])
    d_skip = vecs[2:3]
    adt = dt * a
    tril = _causal_mask(Q)
    acs = _dot(tril.astype(F32), adt, ((1,), (0,)), lax.Precision.HIGHEST)
    acs_t = acs.T
    ys, new_states = [], []
    for h in range(SSD_HEADS):
        g = h // (SSD_HEADS // 2)
        x = xs[:, h * SSD_P:(h + 1) * SSD_P]
        B = bm[:, g * SSD_N:(g + 1) * SSD_N].astype(BF16)
        C = cm[:, g * SSD_N:(g + 1) * SSD_N].astype(BF16)
        xdt = x * dt[:, h:h + 1]
        A = acs[:, h:h + 1]
        L = jnp.exp(jnp.where(tril, A - acs_t[h:h + 1, :], -jnp.inf))
        M = _dot(C, B, ((1,), (1,))) * L
        yd = _dot(M.astype(BF16), xdt.astype(BF16), ((1,), (0,)))
        st = state[h]
        yo = _dot(C, st.astype(BF16), ((1,), (1,))) * jnp.exp(A)
        alast = acs[Q - 1:Q, h:h + 1]
        U = xdt * jnp.exp(alast - A)
        new_states.append(jnp.exp(alast) * st + _dot(U.astype(BF16), B, ((0,), (0,))))
        ys.append(yd + yo + d_skip[:, h:h + 1] * x)
    return jnp.concatenate(ys, axis=1), jnp.stack(new_states)


def _ssd_fwd(xact, proj, svec, name):
    S = xact.shape[0]
    nc = S // CHUNK

    def body(x_ref, dt_ref, v_ref, y_ref, st_ref, state):
        @pl.when(pl.program_id(0) == 0)
        def _():
            state[...] = jnp.zeros_like(state)

        st_ref[0] = state[...]
        x = x_ref[...]
        y, sn = fn_ssd_chunk(x[:, 0:512], x[:, 512:768], x[:, 768:1024], dt_ref[...], state[...], v_ref[...])
        y_ref[...] = y
        state[...] = sn

    return pl.pallas_call(
        body, name=name, grid=(nc,),
        in_specs=[pl.BlockSpec((CHUNK, D_CONV), lambda i: (i, 0)), pl.BlockSpec((CHUNK, LANES), lambda i: (i, 16)),
                  pl.BlockSpec((8, LANES), lambda i: (0, 0))],
        out_specs=[pl.BlockSpec((CHUNK, D_SSD), lambda i: (i, 0)),
                   pl.BlockSpec((1, SSD_HEADS, SSD_P, SSD_N), lambda i: (i, 0, 0, 0))],
        out_shape=[jax.ShapeDtypeStruct((S, D_SSD), F32), jax.ShapeDtypeStruct((nc, SSD_HEADS, SSD_P, SSD_N), F32)],
        scratch_shapes=[pltpu.VMEM((SSD_HEADS, SSD_P, SSD_N), F32)],
        compiler_params=_cp(("arbitrary",)),
    )(xact, proj, svec)


def _ssd_bwd(xact, proj, svec, states, dy, name):
    S = xact.shape[0]
    nc = S // CHUNK

    def body(x_ref, dt_ref, v_ref, st_ref, dy_ref, dx_ref, ddt_ref, dv_ref, dstate):
        @pl.when(pl.program_id(0) == 0)
        def _():
            dstate[...] = jnp.zeros_like(dstate)
            dv_ref[...] = jnp.zeros_like(dv_ref)

        x = x_ref[...]
        _, vjp = jax.vjp(fn_ssd_chunk, x[:, 0:512], x[:, 512:768], x[:, 768:1024], dt_ref[...], st_ref[0], v_ref[...])
        dxs, dbm, dcm, ddt, dst, dvec = vjp((dy_ref[...], dstate[...]))
        dx_ref[:, 0:512] = dxs
        dx_ref[:, 512:768] = dbm
        dx_ref[:, 768:1024] = dcm
        ddt_ref[...] = ddt
        dstate[...] = dst
        dv_ref[...] += dvec

    rev = lambda i: (nc - 1 - i, 0)
    return pl.pallas_call(
        body, name=name, grid=(nc,),
        in_specs=[pl.BlockSpec((CHUNK, D_CONV), rev), pl.BlockSpec((CHUNK, LANES), lambda i: (nc - 1 - i, 16)),
                  pl.BlockSpec((8, LANES), lambda i: (0, 0)),
                  pl.BlockSpec((1, SSD_HEADS, SSD_P, SSD_N), lambda i: (nc - 1 - i, 0, 0, 0)),
                  pl.BlockSpec((CHUNK, D_SSD), rev)],
        out_specs=[pl.BlockSpec((CHUNK, D_CONV), rev), pl.BlockSpec((CHUNK, LANES), rev),
                   pl.BlockSpec((8, LANES), lambda i: (0, 0))],
        out_shape=[jax.ShapeDtypeStruct((S, D_CONV), F32), jax.ShapeDtypeStruct((S, LANES), F32),
                   jax.ShapeDtypeStruct((8, LANES), F32)],
        scratch_shapes=[pltpu.VMEM((SSD_HEADS, SSD_P, SSD_N), F32)],
        compiler_params=_cp(("arbitrary",)),
    )(xact, proj, svec, states, dy)


def _ada_fwd(c_all, w_ada, b_sh, name):
    nb = 1536 // 512

    def body(c_ref, w_ref, b_ref, o_ref):
        ca = jax.nn.silu(c_ref[...]).astype(BF16)
        o_ref[0] = _dot(ca, w_ref[0].astype(BF16), ((1,), (0,))) + b_ref[0]

    return pl.pallas_call(
        body, name=name, grid=(DEPTH, nb),
        in_specs=[pl.BlockSpec((8, D_MODEL), lambda l, j: (0, 0)), pl.BlockSpec((1, D_MODEL, 512), lambda l, j: (l, 0, j)),
                  pl.BlockSpec((1, 1, 512), lambda l, j: (l, 0, j))],
        out_specs=pl.BlockSpec((1, 8, 512), lambda l, j: (l, 0, j)),
        out_shape=jax.ShapeDtypeStruct((DEPTH, 8, 1536), F32),
        compiler_params=_cp(("parallel", "parallel")),
    )(c_all, w_ada, b_sh)


def _ada_bwd(c_all_t, dmod_sh, name):
    nb = 1536 // 512

    def body(c_ref, d_ref, o_ref):
        ca = jax.nn.silu(c_ref[...])
        acc = ca[:, 0:1] * d_ref[0, 0:1, :]
        for b in range(1, 8):
            acc = acc + ca[:, b:b + 1] * d_ref[0, b:b + 1, :]
        o_ref[0] = acc

    return pl.pallas_call(
        body, name=name, grid=(DEPTH, nb),
        in_specs=[pl.BlockSpec((D_MODEL, 8), lambda l, j: (0, 0)), pl.BlockSpec((1, 8, 512), lambda l, j: (l, 0, j))],
        out_specs=pl.BlockSpec((1, D_MODEL, 512), lambda l, j: (l, 0, j)),
        out_shape=jax.ShapeDtypeStruct((DEPTH, D_MODEL, 1536), F32),
        compiler_params=_cp(("parallel", "parallel")),
    )(c_all_t, dmod_sh)


def _rows_tile(rows):
    return next(t for t in (512, 256, 128, 64, 32, 16, 8) if rows % t == 0)


def _sum_parts(parts, rows, width, name):
    bm = _rows_tile(rows)

    def body(*refs):
        acc = refs[0][...]
        for r in refs[1:-1]:
            acc = acc + r[...]
        refs[-1][...] = acc

    return pl.pallas_call(
        body, name=name, grid=(rows // bm,),
        in_specs=[pl.BlockSpec((bm, width), functools.partial(lambda i, o: (i + o, 0), o=off // bm)) for (_, off) in parts],
        out_specs=pl.BlockSpec((bm, width), lambda i: (i, 0)),
        out_shape=jax.ShapeDtypeStruct((rows, width), F32),
        compiler_params=_cp(("parallel",)),
    )(*[p[0] for p in parts])


def _adam(w, m, v, parts, name):
    rows, width = w.shape
    bm = min(256, _rows_tile(rows))
    np_ = len(parts)
    c1 = 1.0 / (1.0 - ADAM_B1 ** ADAM_STEP)
    c2 = 1.0 / (1.0 - ADAM_B2 ** ADAM_STEP)

    def body(*refs):
        w_ref, m_ref, v_ref = refs[:3]
        g = refs[3][...]
        for r in refs[4:3 + np_]:
            g = g + r[...]
        g_ref, d_ref, nm_ref, nv_ref = refs[3 + np_:]
        nm = ADAM_B1 * m_ref[...] + (1.0 - ADAM_B1) * g
        nv = ADAM_B2 * v_ref[...] + (1.0 - ADAM_B2) * (g * g)
        g_ref[...] = g
        nm_ref[...] = nm
        nv_ref[...] = nv
        d_ref[...] = -ADAM_LR * ((nm * c1) / (jnp.sqrt(nv * c2) + ADAM_EPS) + ADAM_WD * w_ref[...])

    blk = pl.BlockSpec((bm, width), lambda i: (i, 0))
    return pl.pallas_call(
        body, name=name, grid=(rows // bm,),
        in_specs=[blk, blk, blk] + [pl.BlockSpec((bm, width), functools.partial(lambda i, o: (i + o, 0), o=off // bm))
                                    for (_, off) in parts],
        out_specs=[blk, blk, blk, blk],
        out_shape=[jax.ShapeDtypeStruct((rows, width), F32)] * 4,
        compiler_params=_cp(("parallel",)),
    )(w, m, v, *[p[0] for p in parts])


def _coords():
    return lax.axis_index("x"), lax.axis_index("y"), lax.axis_index("c")


def _other_chips(x, y):
    return [(1 - x, y), (x, 1 - y), (1 - x, 1 - y)]


def _ag8(blk, name):
    m_per, n = blk.shape

    def body(x_ref, out_ref, send_sems, recv_sems, local_sem):
        x, y, c = _coords()
        me, sibling = (x, y, c), (x, y, 1 - c)
        chips = _other_chips(x, y)

        def rows(px, py, pc):
            return out_ref.at[pl.ds((4 * px + 2 * py + pc) * m_per, m_per), :]

        def copy(k, block, to, src=None):
            return pltpu.make_async_remote_copy(
                src_ref=rows(*block) if src is None else src, dst_ref=rows(*block),
                send_sem=send_sems.at[k], recv_sem=recv_sems.at[k], device_id=to, device_id_type=MESH)

        mine = pltpu.make_async_copy(x_ref, rows(*me), local_sem)
        mine.start()
        first = [copy(0, me, sibling, src=x_ref)]
        first += [copy(1 + j, me, (*chip, c), src=x_ref) for j, chip in enumerate(chips)]
        for cp in first:
            cp.start()
        passed = [copy(4 + j, (*chip, c), sibling) for j, chip in enumerate(chips)]
        for j, chip in enumerate(chips):
            copy(1 + j, (*chip, c), me).wait_recv()
            passed[j].start()
        copy(0, sibling, me).wait_recv()
        for j, chip in enumerate(chips):
            copy(4 + j, (*chip, 1 - c), me).wait_recv()
        for cp in first + passed:
            cp.wait_send()
        mine.wait()

    return pl.pallas_call(
        body, name=name,
        out_shape=jax.ShapeDtypeStruct((8 * m_per, n), blk.dtype),
        in_specs=[pl.BlockSpec(memory_space=pltpu.VMEM)], out_specs=pl.BlockSpec(memory_space=pltpu.VMEM),
        scratch_shapes=[pltpu.SemaphoreType.DMA((7,)), pltpu.SemaphoreType.DMA((7,)), pltpu.SemaphoreType.DMA],
    )(blk)


def _ag_weights(shard, name):
    _, hh, wd = shard.shape

    def body(sh_ref, out_ref, send_sems, recv_sems, local_sem):
        x, y, c = _coords()
        sibling = (x, y, 1 - c)
        chips = _other_chips(x, y)

        def blk(px, py, pc):
            return out_ref.at[2 * px + py, pc]

        def copy(k, src, dst, to):
            return pltpu.make_async_remote_copy(src_ref=src, dst_ref=dst, send_sem=send_sems.at[k],
                                                recv_sem=recv_sems.at[k], device_id=to, device_id_type=MESH)

        mine = pltpu.make_async_copy(sh_ref, out_ref.at[2 * x + y], local_sem)
        mine.start()
        first = [copy(j, sh_ref.at[c], blk(x, y, c), (*chip, c)) for j, chip in enumerate(chips)]
        for cp in first:
            cp.start()
        passed = [copy(3 + j, blk(*chip, c), blk(*chip, c), sibling) for j, chip in enumerate(chips)]
        for j, chip in enumerate(chips):
            copy(j, sh_ref.at[c], blk(*chip, c), (x, y, c)).wait_recv()
            passed[j].start()
        for j, chip in enumerate(chips):
            copy(3 + j, sh_ref.at[c], blk(*chip, 1 - c), (x, y, c)).wait_recv()
        for cp in first + passed:
            cp.wait_send()
        mine.wait()

    return pl.pallas_call(
        body, name=name,
        out_shape=jax.ShapeDtypeStruct((4, 2, hh, wd), shard.dtype),
        in_specs=[pl.BlockSpec(memory_space=pl.ANY)], out_specs=pl.BlockSpec(memory_space=pl.ANY),
        scratch_shapes=[pltpu.SemaphoreType.DMA((6,)), pltpu.SemaphoreType.DMA((6,)), pltpu.SemaphoreType.DMA],
    )(shard)


def _rs_sibling(g, name):
    _, _, hh, wd = g.shape

    def body(g_ref, out_ref, send_sems, recv_sems):
        x, y, c = _coords()
        cps = [pltpu.make_async_remote_copy(src_ref=g_ref.at[s, 1 - c], dst_ref=out_ref.at[s], send_sem=send_sems.at[s],
                                            recv_sem=recv_sems.at[s], device_id=(x, y, 1 - c), device_id_type=MESH)
               for s in range(4)]
        for cp in cps:
            cp.start()
        for cp in cps:
            cp.wait_recv()
        for cp in cps:
            cp.wait_send()

    return pl.pallas_call(
        body, name=name,
        out_shape=jax.ShapeDtypeStruct((4, hh, wd), g.dtype),
        in_specs=[pl.BlockSpec(memory_space=pl.ANY)], out_specs=pl.BlockSpec(memory_space=pl.ANY),
        scratch_shapes=[pltpu.SemaphoreType.DMA((4,)), pltpu.SemaphoreType.DMA((4,))],
    )(g)


def _rs_chips(cs, name):
    _, hh, wd = cs.shape

    def body(c_ref, out_ref, send_sems, recv_sems):
        x, y, c = _coords()
        cps = [pltpu.make_async_remote_copy(src_ref=c_ref.at[2 * px + py], dst_ref=out_ref.at[j], send_sem=send_sems.at[j],
                                            recv_sem=recv_sems.at[j], device_id=(px, py, c), device_id_type=MESH)
               for j, (px, py) in enumerate(_other_chips(x, y))]
        for cp in cps:
            cp.start()
        for cp in cps:
            cp.wait_recv()
        for cp in cps:
            cp.wait_send()

    return pl.pallas_call(
        body, name=name,
        out_shape=jax.ShapeDtypeStruct((3, hh, wd), cs.dtype),
        in_specs=[pl.BlockSpec(memory_space=pl.ANY)], out_specs=pl.BlockSpec(memory_space=pl.ANY),
        scratch_shapes=[pltpu.SemaphoreType.DMA((3,)), pltpu.SemaphoreType.DMA((3,))],
    )(cs)


def _swap_halves(gh, name):
    hh, wd = gh.shape

    def body(g_ref, out_ref, send_sem, recv_sem, local_sem):
        x, y, c = _coords()
        mine = pltpu.make_async_copy(g_ref, out_ref.at[c], local_sem)
        mine.start()
        cp = pltpu.make_async_remote_copy(src_ref=g_ref, dst_ref=out_ref.at[c], send_sem=send_sem, recv_sem=recv_sem,
                                          device_id=(x, y, 1 - c), device_id_type=MESH)
        cp.start()
        pltpu.make_async_remote_copy(src_ref=g_ref, dst_ref=out_ref.at[1 - c], send_sem=send_sem, recv_sem=recv_sem,
                                     device_id=(x, y, 1 - c), device_id_type=MESH).wait_recv()
        cp.wait_send()
        mine.wait()

    return pl.pallas_call(
        body, name=name,
        out_shape=jax.ShapeDtypeStruct((2, hh, wd), gh.dtype),
        in_specs=[pl.BlockSpec(memory_space=pl.ANY)], out_specs=pl.BlockSpec(memory_space=pl.ANY),
        scratch_shapes=[pltpu.SemaphoreType.DMA, pltpu.SemaphoreType.DMA, pltpu.SemaphoreType.DMA],
    )(gh)


def _pad_win(w):
    return jnp.concatenate([w[:, :416], jnp.zeros((w.shape[0], 96), w.dtype), w[:, 416:1952],
                            w[:, 1952:1960], jnp.zeros((w.shape[0], 120), w.dtype)], axis=1)


def _unpad_win(g):
    return jnp.concatenate([g[:, :416], g[:, 512:2048], g[:, 2048:2056]], axis=1)


def _pad_wq(w):
    return jnp.pad(w.reshape(Q_LORA, HEADS, QK_DIM), ((0, 0), (0, 0), (0, LANES - QK_DIM))).reshape(Q_LORA, HEADS * LANES)


def _unpad_wq(g):
    return g.reshape(Q_LORA, HEADS, LANES)[:, :, :QK_DIM].reshape(Q_LORA, HEADS * QK_DIM)


def _pack_big_shard(tree):
    flat = jnp.concatenate([tree[n][l].reshape(-1) for l in range(DEPTH) for (n, _, _) in BIG])
    return jnp.pad(flat, (0, BIG_ROWS * 1024 - flat.shape[0])).reshape(BIG_ROWS, 1024)


def _unpack_big_shard(buf):
    flat = buf.reshape(-1)
    out = {n: [] for (n, _, _) in BIG}
    o = 0
    for l in range(DEPTH):
        for (n, shp, _) in BIG:
            k = shp[0] * shp[1]
            out[n].append(flat[o:o + k].reshape(shp))
            o += k
    return {n: jnp.stack(v) for n, v in out.items()}


def _unpack_big_full(buf):
    out = {n: [] for (n, _, _) in BIG}
    o = 0
    for l in range(DEPTH):
        for (n, shp, ax) in BIG:
            k = shp[0] * shp[1]
            p = buf[:, o:o + k].reshape(4, *shp)
            out[n].append(p.reshape(4 * shp[0], shp[1]) if ax == 0 else p.transpose(1, 0, 2).reshape(shp[0], 4 * shp[1]))
            o += k
    return out


def _pack_big_full(full):
    cols = []
    for l in range(DEPTH):
        for (n, shp, ax) in BIG:
            g = full[n][l]
            cols.append(g.reshape(4, -1) if ax == 0 else g.reshape(shp[0], 4, shp[1]).transpose(1, 0, 2).reshape(4, -1))
    flat = jnp.concatenate(cols, axis=1)
    return jnp.pad(flat, ((0, 0), (0, BIG_ROWS * 1024 - flat.shape[1])))


def _pack_small(tree):
    parts = []
    for l in range(DEPTH):
        for (n, k) in SMALL:
            parts.append(jnp.pad(tree[n][l].reshape(-1), (0, -k % LANES)))
    flat = jnp.concatenate(parts)
    return jnp.pad(flat, (0, SMALL_ROWS * LANES - flat.shape[0])).reshape(SMALL_ROWS, LANES)


def _unpack_small(buf):
    flat = buf.reshape(-1)
    out = {n: [] for (n, _) in SMALL}
    o = 0
    for l in range(DEPTH):
        for (n, k) in SMALL:
            out[n].append(flat[o:o + k])
            o += k + (-k % LANES)
    return {n: jnp.stack(v) for n, v in out.items()}


def _vec(v, width=LANES):
    return jnp.pad(v.reshape(1, -1), ((0, 0), (0, width - v.shape[-1])))


def kernel(x, c, positions, norm1_w, norm2_w, w_ada, b_ada, w_in, q_a_norm_w, w_q_up, kv_a_norm_w, w_kv_up, q_nope_norm_w, q_pe_norm_w, k_nope_norm_w, k_pe_norm_w, conv_w, conv_b, dt_bias, a_log, d_skip, ssd_norm_w, w_out, w_gate_up, w_down, loss_target, m_norm1_w, m_norm2_w, m_w_ada, m_b_ada, m_w_in, m_q_a_norm_w, m_w_q_up, m_kv_a_norm_w, m_w_kv_up, m_q_nope_norm_w, m_q_pe_norm_w, m_k_nope_norm_w, m_k_pe_norm_w, m_conv_w, m_conv_b, m_dt_bias, m_a_log, m_d_skip, m_ssd_norm_w, m_w_out, m_w_gate_up, m_w_down, v_norm1_w, v_norm2_w, v_w_ada, v_b_ada, v_w_in, v_q_a_norm_w, v_w_q_up, v_kv_a_norm_w, v_w_kv_up, v_q_nope_norm_w, v_q_pe_norm_w, v_k_nope_norm_w, v_k_pe_norm_w, v_conv_w, v_conv_b, v_dt_bias, v_a_log, v_d_skip, v_ssd_norm_w, v_w_out, v_w_gate_up, v_w_down):
    W = dict(zip(WEIGHTS, (norm1_w, norm2_w, w_ada, b_ada, w_in, q_a_norm_w, w_q_up, kv_a_norm_w, w_kv_up, q_nope_norm_w, q_pe_norm_w, k_nope_norm_w, k_pe_norm_w, conv_w, conv_b, dt_bias, a_log, d_skip, ssd_norm_w, w_out, w_gate_up, w_down)))
    M = dict(zip(WEIGHTS, (m_norm1_w, m_norm2_w, m_w_ada, m_b_ada, m_w_in, m_q_a_norm_w, m_w_q_up, m_kv_a_norm_w, m_w_kv_up, m_q_nope_norm_w, m_q_pe_norm_w, m_k_nope_norm_w, m_k_pe_norm_w, m_conv_w, m_conv_b, m_dt_bias, m_a_log, m_d_skip, m_ssd_norm_w, m_w_out, m_w_gate_up, m_w_down)))
    V = dict(zip(WEIGHTS, (v_norm1_w, v_norm2_w, v_w_ada, v_b_ada, v_w_in, v_q_a_norm_w, v_w_q_up, v_kv_a_norm_w, v_w_kv_up, v_q_nope_norm_w, v_q_pe_norm_w, v_k_nope_norm_w, v_k_pe_norm_w, v_conv_w, v_conv_b, v_dt_bias, v_a_log, v_d_skip, v_ssd_norm_w, v_w_out, v_w_gate_up, v_w_down)))
    S = x.shape[1]
    xi, yi, ci = _coords()
    chip = 2 * xi + yi
    dev = 2 * chip + ci
    x0 = x[0]
    tgt = loss_target[0]

    inv_freq = 1.0 / (ROPE_THETA ** (jnp.arange(0, ROPE, 2, dtype=F32) / ROPE))
    ang = positions[0].astype(F32)[:, None] * inv_freq
    cos, sin = jnp.cos(ang), jnp.sin(ang)
    z16, z32, z64 = jnp.zeros((S, 16), F32), jnp.zeros((S, 32), F32), jnp.zeros((S, 64), F32)
    tab_c = jnp.concatenate([jnp.ones((S, 64), F32), cos, cos, z32], axis=1)
    tab_s1 = jnp.concatenate([z64, z16, sin, z32], axis=1)
    tab_s2 = jnp.concatenate([z64, -sin, z16, z32], axis=1)

    blk0 = jnp.concatenate([c.reshape(-1), W['conv_w'].reshape(-1)]).reshape(24, LANES)
    g0 = _ag8(blk0, "ag_c_conv").reshape(8, 24 * LANES)
    c_all = g0[:, :D_MODEL]
    conv_full = g0[0::2, D_MODEL:].reshape(4, DEPTH, CONV_TAPS, 256).transpose(1, 2, 0, 3).reshape(DEPTH, CONV_TAPS, D_CONV)

    wsh = _pack_big_shard(W).astype(BF16).reshape(2, HALF_ROWS, 1024)
    wall = _ag_weights(wsh, "ag_weights").reshape(4, BIG_ROWS * 1024)
    full = _unpack_big_full(wall)
    win_p = [_pad_win(full['w_in'][l]) for l in range(DEPTH)]
    wq_p = [_pad_wq(full['w_q_up'][l]) for l in range(DEPTH)]

    b_sh = lax.dynamic_slice_in_dim(W['b_ada'], chip * 1536, 1536, axis=1).reshape(DEPTH, 1, 1536)
    mod_sh = _ada_fwd(c_all, W['w_ada'], b_sh, "ada_fwd")
    g1 = _ag8(mod_sh.reshape(192, LANES), "ag_mod").reshape(8, DEPTH, 8, 1536)
    mod_all = g1[0::2].transpose(1, 2, 0, 3).reshape(DEPTH, 8, 6 * D_MODEL)
    mod = lax.dynamic_index_in_dim(mod_all, dev, axis=1, keepdims=False)

    def mvec(l, k):
        return mod[l, k * D_MODEL:(k + 1) * D_MODEL].reshape(1, D_MODEL)

    def small(name, l, width=None):
        v = W[name][l]
        return _vec(v, width or v.shape[-1])

    def wq_vec(l):
        return _vec(jnp.concatenate([W['q_nope_norm_w'][l], W['q_pe_norm_w'][l]]))

    def wk_vec(l):
        return _vec(jnp.concatenate([W['k_nope_norm_w'][l], W['k_pe_norm_w'][l]]))

    def conv_vec(l):
        return jnp.concatenate([conv_full[l], W['conv_b'][l].reshape(1, D_CONV), jnp.zeros((3, D_CONV), F32)], axis=0)

    def ssd_vec(l):
        return jnp.concatenate([_vec(W['dt_bias'][l]), _vec(W['a_log'][l]), _vec(W['d_skip'][l]), jnp.zeros((5, LANES), F32)], axis=0)

    sv = []
    xcur = x0
    h1 = _row_fwd(fn_norm_mod, "norm_mod_f", [(x0, 0, D_MODEL)], [small('norm1_w', 0), mvec(0, 1), mvec(0, 0)],
                  [(D_MODEL, BF16)])[0]
    fin = None
    for l in range(DEPTH):
        t = dict(xcur=xcur, h1=h1)
        t['proj'] = proj = _mm(h1, win_p[l], 'nn', f"mm_in_{l}")
        t['qa_n'], t['kva_n'] = _row_fwd(fn_lat_norm, f"lat_norm_f{l}", [(proj, 0, 256), (proj, 2, 128)],
                                         [small('q_a_norm_w', l), small('kv_a_norm_w', l)], [(256, BF16), (128, BF16)])
        t['q'] = _mm(t['qa_n'], wq_p[l], 'nn', f"mm_q_{l}")
        t['kv'] = _mm(t['kva_n'], full['w_kv_up'][l], 'nn', f"mm_kv_{l}")
        t['qf'], t['kf'], t['vv'] = _row_fwd(
            fn_qk_prep, f"qk_prep_f{l}",
            [(t['q'], 0, 1024), (t['kv'], 0, 1024), (proj, 3, 128), (tab_c, 0, 128), (tab_s1, 0, 128), (tab_s2, 0, 128)],
            [wq_vec(l), wk_vec(l)], [(1024, BF16), (1024, BF16), (512, BF16)])
        t['ao'], t['lse'] = _attn_fwd(t['qf'], t['kf'], t['vv'], f"attn_f{l}")
        t['xact'] = _conv_fwd(proj, conv_vec(l), f"conv_f{l}")
        t['y'], t['states'] = _ssd_fwd(t['xact'], proj, ssd_vec(l), f"ssd_f{l}")
        t['mix'] = _row_fwd(fn_gated_mix, f"gated_f{l}", [(t['y'], 0, 512), (proj, 1, 512), (t['ao'], 0, 512)],
                            [small('ssd_norm_w', l)], [(1024, BF16)])[0]
        t['mo'] = _mm(t['mix'], full['w_out'][l], 'nn', f"mm_out_{l}")
        t['x1'], t['h2'] = _row_fwd(fn_resid_norm, f"resid_mid_f{l}", [(xcur, 0, D_MODEL), (t['mo'], 0, D_MODEL)],
                                    [mvec(l, 2), small('norm2_w', l), mvec(l, 4), mvec(l, 3)],
                                    [(D_MODEL, F32), (D_MODEL, BF16)])
        t['gu'] = _mm(t['h2'], full['w_gate_up'][l], 'nn', f"mm_gu_{l}")
        t['act'] = _row_fwd(fn_swiglu, f"swiglu_f{l}", [(t['gu'], 0, D_FF), (t['gu'], 1, D_FF)], [], [(D_FF, BF16)], tm=128)[0]
        t['ff'] = _mm(t['act'], full['w_down'][l], 'nn', f"mm_down_{l}")
        if l + 1 < DEPTH:
            xcur, h1 = _row_fwd(fn_resid_norm, f"resid_end_f{l}", [(t['x1'], 0, D_MODEL), (t['ff'], 0, D_MODEL)],
                                [mvec(l, 5), small('norm1_w', l + 1), mvec(l + 1, 1), mvec(l + 1, 0)],
                                [(D_MODEL, F32), (D_MODEL, BF16)])
        else:
            fin = _final(t['x1'], t['ff'], tgt, mvec(l, 5), "final_loss")
        sv.append(t)

    dx1, dff, dg2_last, loss_acc = fin
    loss = lax.psum(loss_acc[0, 0], ("x", "y", "c"))
    gfull = {n: [None] * DEPTH for (n, _, _) in BIG}
    gsm = {n: [None] * DEPTH for (n, _) in SMALL}
    dmod = [[None] * 6 for _ in range(DEPTH)]
    dmod[DEPTH - 1][5] = dg2_last
    grad_x = None
    for l in reversed(range(DEPTH)):
        t = sv[l]
        proj = t['proj']
        dact = _mm(dff, full['w_down'][l], 'nt', f"mm_down_dx{l}")
        gfull['w_down'][l] = _mm(t['act'], dff, 'tn', f"mm_down_dw{l}")
        dg, du = _row_bwd(fn_swiglu, f"swiglu_b{l}", [(t['gu'], 0, D_FF), (t['gu'], 1, D_FF)], [], [(dact, 0, D_FF)], [0, 1], [], tm=128)
        dgu = jnp.concatenate([dg, du], axis=1)
        dh2 = _mm(dgu, full['w_gate_up'][l], 'nt', f"mm_gu_dx{l}")
        gfull['w_gate_up'][l] = _mm(t['h2'], dgu, 'tn', f"mm_gu_dw{l}")
        dxc, dmo, dmod[l][2], gsm['norm2_w'][l], dmod[l][4], dmod[l][3] = _row_bwd(
            fn_resid_norm, f"resid_mid_b{l}", [(t['xcur'], 0, D_MODEL), (t['mo'], 0, D_MODEL)],
            [mvec(l, 2), small('norm2_w', l), mvec(l, 4), mvec(l, 3)], [(dx1, 0, D_MODEL), (dh2, 0, D_MODEL)], [0, 1], [0, 1, 2, 3])
        dmix = _mm(dmo, full['w_out'][l], 'nt', f"mm_out_dx{l}")
        gfull['w_out'][l] = _mm(t['mix'], dmo, 'tn', f"mm_out_dw{l}")
        dy, dz, gsm['ssd_norm_w'][l] = _row_bwd(fn_gated_norm, f"gated_b{l}", [(t['y'], 0, 512), (proj, 1, 512)],
                                                [small('ssd_norm_w', l)], [(dmix, 1, 512)], [0, 1], [0])
        dxact, ddt, dsv = _ssd_bwd(t['xact'], proj, ssd_vec(l), t['states'], dy, f"ssd_b{l}")
        gsm['dt_bias'][l], gsm['a_log'][l], gsm['d_skip'][l] = dsv[0, :8], dsv[1, :8], dsv[2, :8]
        dxbc, dcv = _conv_bwd(proj, conv_vec(l), dxact, f"conv_b{l}")
        gfull['conv_w'][l] = dcv[:CONV_TAPS]
        gsm['conv_b'][l] = dcv[CONV_TAPS]
        dqf = _attn_bwd_dq(t['qf'], t['kf'], t['vv'], t['ao'], t['lse'], dmix, f"attn_dq{l}")
        dkf, dvv = _attn_bwd_dkv(t['qf'], t['kf'], t['vv'], t['ao'], t['lse'], dmix, f"attn_dkv{l}")
        dq, dkv, dkpe, dwq, dwk = _row_bwd(
            fn_qk_prep, f"qk_prep_b{l}",
            [(t['q'], 0, 1024), (t['kv'], 0, 1024), (proj, 3, 128), (tab_c, 0, 128), (tab_s1, 0, 128), (tab_s2, 0, 128)],
            [wq_vec(l), wk_vec(l)], [(dqf, 0, 1024), (dkf, 0, 1024), (dvv, 0, 512)], [0, 1, 2], [0, 1])
        gsm['q_nope_norm_w'][l], gsm['q_pe_norm_w'][l] = dwq[0, :NOPE], dwq[0, NOPE:QK_DIM]
        gsm['k_nope_norm_w'][l], gsm['k_pe_norm_w'][l] = dwk[0, :NOPE], dwk[0, NOPE:QK_DIM]
        dqa_n = _mm(dq, wq_p[l], 'nt', f"mm_q_dx{l}")
        gfull['w_q_up'][l] = _unpad_wq(_mm(t['qa_n'], dq, 'tn', f"mm_q_dw{l}"))
        dkva_n = _mm(dkv, full['w_kv_up'][l], 'nt', f"mm_kv_dx{l}")
        gfull['w_kv_up'][l] = _mm(t['kva_n'], dkv, 'tn', f"mm_kv_dw{l}")
        dqa, dkva, dqw, dkvw = _row_bwd(fn_lat_norm, f"lat_norm_b{l}", [(proj, 0, 256), (proj, 2, 128)],
                                        [small('q_a_norm_w', l), small('kv_a_norm_w', l)],
                                        [(dqa_n, 0, 256), (dkva_n, 0, 128)], [0, 1], [0, 1])
        gsm['q_a_norm_w'][l], gsm['kv_a_norm_w'][l] = dqw[0], dkvw[0]
        dproj = jnp.concatenate([dqa, dkva, dkpe, dz, dxbc, ddt], axis=1)
        dh1 = _mm(dproj, win_p[l], 'nt', f"mm_in_dx{l}")
        gfull['w_in'][l] = _unpad_win(_mm(t['h1'], dproj, 'tn', f"mm_in_dw{l}"))
        if l > 0:
            p = sv[l - 1]
            dx1, dff, dmod[l - 1][5], gsm['norm1_w'][l], dmod[l][1], dmod[l][0] = _row_bwd(
                fn_resid_norm, f"resid_end_b{l - 1}", [(p['x1'], 0, D_MODEL), (p['ff'], 0, D_MODEL)],
                [mvec(l - 1, 5), small('norm1_w', l), mvec(l, 1), mvec(l, 0)], [(dxc, 0, D_MODEL), (dh1, 0, D_MODEL)],
                [0, 1], [0, 1, 2, 3])
        else:
            grad_x, gsm['norm1_w'][l], dmod[l][1], dmod[l][0] = _row_bwd(
                fn_norm_mod_pass, "norm_mod_b", [(x0, 0, D_MODEL)], [small('norm1_w', 0), mvec(0, 1), mvec(0, 0)],
                [(dxc, 0, D_MODEL), (dh1, 0, D_MODEL)], [0], [0, 1, 2])
        for n in ('norm1_w', 'norm2_w', 'ssd_norm_w'):
            gsm[n][l] = gsm[n][l][0]

    for l in range(DEPTH):
        gsm['b_ada'][l] = jnp.concatenate([d[0] for d in dmod[l]])
    sm_part = _pack_small({n: jnp.stack(v) for n, v in gsm.items()})
    sm_all = _ag8(sm_part, "ag_small")
    g_sm, d_sm, m_sm, v_sm = _adam(_pack_small(W), _pack_small(M), _pack_small(V),
                                   [(sm_all, d * SMALL_ROWS) for d in range(8)], "adam_small")
    out_small = [_unpack_small(b) for b in (g_sm, d_sm, m_sm, v_sm)]

    dmod_all = sm_all.reshape(8, SMALL_ROWS * LANES)
    per_layer = sum(k + (-k % LANES) for (_, k) in SMALL)
    dmod_sh = jnp.stack([lax.dynamic_slice_in_dim(dmod_all[:, l * per_layer:l * per_layer + 6 * D_MODEL], chip * 1536, 1536, axis=1)
                         for l in range(DEPTH)])
    g_ada = _ada_bwd(c_all.T, dmod_sh, "ada_bwd")
    ada = _adam(W['w_ada'].reshape(DEPTH * D_MODEL, 1536), M['w_ada'].reshape(DEPTH * D_MODEL, 1536),
                V['w_ada'].reshape(DEPTH * D_MODEL, 1536), [(g_ada.reshape(DEPTH * D_MODEL, 1536), 0)], "adam_ada")
    out_ada = [a.reshape(DEPTH, D_MODEL, 1536) for a in ada]

    gp = _pack_big_full(gfull).reshape(4, 2, HALF_ROWS, 1024)
    sib = _rs_sibling(gp, "rs_sibling")
    mine = lax.dynamic_index_in_dim(gp, ci, axis=1, keepdims=False)
    chipsum = _sum_parts([(mine.reshape(4 * HALF_ROWS, 1024), 0), (sib.reshape(4 * HALF_ROWS, 1024), 0)],
                         4 * HALF_ROWS, 1024, "sum_chip").reshape(4, HALF_ROWS, 1024)
    landed = _rs_chips(chipsum, "rs_chips")
    own = lax.dynamic_index_in_dim(chipsum, chip, axis=0, keepdims=False)
    ghalf = _sum_parts([(own, 0), (landed[0], 0), (landed[1], 0), (landed[2], 0)], HALF_ROWS, 1024, "sum_all")
    gshard = _swap_halves(ghalf, "swap_halves").reshape(BIG_ROWS, 1024)
    big = _adam(_pack_big_shard(W), _pack_big_shard(M), _pack_big_shard(V), [(gshard, 0)], "adam_big")
    out_big = [_unpack_big_shard(b) for b in big]

    outs = [loss, grad_x[None]]
    big_names = {n for (n, _, _) in BIG}
    for k in range(4):
        for n in WEIGHTS:
            if n == 'w_ada':
                outs.append(out_ada[k])
            elif n in big_names:
                outs.append(out_big[k][n])
            else:
                outs.append(out_small[k][n])
    return tuple(outs)
```

```python
import functools

import jax
import jax.numpy as jnp
from jax import lax
from jax.experimental import pallas as pl
from jax.experimental.pallas import tpu as pltpu

F32 = jnp.float32
BF16 = jnp.bfloat16
MESH = pl.DeviceIdType.MESH

D_MODEL = 1024
DEPTH = 2
HEADS = 8
NOPE = 64
ROPE = 32
QK_DIM = NOPE + ROPE
Q_LORA = 256
KV_LORA = 128
SSD_HEADS = 8
SSD_P = 64
SSD_N = 128
CHUNK = 128
CONV_TAPS = 4
D_SSD = 512
D_CONV = 1024
D_FF = 2816
D_IN = 1960
D_IN_PAD = 2176
EPS = 1e-6
ROPE_THETA = 10000.0
ATT_SCALE = QK_DIM ** -0.5
NEG = -1e30
LANES = 128
VMEM_LIMIT = 48 * 1024 * 1024

ADAM_LR, ADAM_B1, ADAM_B2, ADAM_EPS, ADAM_WD, ADAM_STEP = 0.001, 0.9, 0.999, 1e-08, 0.01, 10

WEIGHTS = ['norm1_w', 'norm2_w', 'w_ada', 'b_ada', 'w_in', 'q_a_norm_w', 'w_q_up', 'kv_a_norm_w', 'w_kv_up',
           'q_nope_norm_w', 'q_pe_norm_w', 'k_nope_norm_w', 'k_pe_norm_w', 'conv_w', 'conv_b', 'dt_bias', 'a_log',
           'd_skip', 'ssd_norm_w', 'w_out', 'w_gate_up', 'w_down']
BIG = [('w_in', (1024, 490), 1), ('w_q_up', (256, 192), 1), ('w_kv_up', (128, 256), 1), ('conv_w', (4, 256), 1),
       ('w_out', (256, 1024), 0), ('w_gate_up', (1024, 1408), 1), ('w_down', (704, 1024), 0)]
BIG_ROWS = 5888
HALF_ROWS = BIG_ROWS // 2
SMALL = [('b_ada', 6144), ('norm1_w', 1024), ('norm2_w', 1024), ('conv_b', 1024), ('ssd_norm_w', 512),
         ('q_a_norm_w', 256), ('kv_a_norm_w', 128), ('q_nope_norm_w', 64), ('q_pe_norm_w', 32),
         ('k_nope_norm_w', 64), ('k_pe_norm_w', 32), ('dt_bias', 8), ('a_log', 8), ('d_skip', 8)]
SMALL_ROWS = 176


def _cp(sem=None, **kw):
    return pltpu.CompilerParams(dimension_semantics=sem, vmem_limit_bytes=VMEM_LIMIT, **kw)


def _dot(a, b, dims, prec=None):
    return lax.dot_general(a, b, (dims, ((), ())), preferred_element_type=F32, precision=prec)


def _tile(dim, target):
    best = 0
    for t in range(LANES, min(dim, target) + 1, LANES):
        if dim % t == 0:
            best = t
    if best < 256 and dim <= 2304:
        return dim
    return best


def _mm(a, b, mode, name, out_dtype=F32):
    if mode == 'nn':
        (M, K), (_, N) = a.shape, b.shape
    elif mode == 'nt':
        (M, K), (N, _) = a.shape, b.shape
    else:
        (K, M), (_, N) = a.shape, b.shape
    tm, tn, tk = _tile(M, 512), _tile(N, 1408), _tile(K, 1408)
    nk = K // tk
    dims = {'nn': ((1,), (0,)), 'nt': ((1,), (1,)), 'tn': ((0,), (0,))}[mode]

    def body(a_ref, b_ref, o_ref, acc):
        k = pl.program_id(2)

        @pl.when(k == 0)
        def _():
            acc[...] = jnp.zeros_like(acc)

        acc[...] += _dot(a_ref[...].astype(BF16), b_ref[...].astype(BF16), dims)

        @pl.when(k == nk - 1)
        def _():
            o_ref[...] = acc[...].astype(o_ref.dtype)

    a_spec = pl.BlockSpec((tk, tm), lambda i, j, k: (k, i)) if mode == 'tn' else pl.BlockSpec((tm, tk), lambda i, j, k: (i, k))
    b_spec = pl.BlockSpec((tn, tk), lambda i, j, k: (j, k)) if mode == 'nt' else pl.BlockSpec((tk, tn), lambda i, j, k: (k, j))
    return pl.pallas_call(
        body, name=name, grid=(M // tm, N // tn, nk),
        in_specs=[a_spec, b_spec], out_specs=pl.BlockSpec((tm, tn), lambda i, j, k: (i, j)),
        out_shape=jax.ShapeDtypeStruct((M, N), out_dtype),
        scratch_shapes=[pltpu.VMEM((tm, tn), F32)],
        compiler_params=_cp(("parallel", "parallel", "arbitrary")),
    )(a, b)


def _rspec(tm, w, cb):
    return pl.BlockSpec((tm, w), lambda i: (i, cb))


def _vspec(shape):
    return pl.BlockSpec(shape, lambda i: (0,) * len(shape))


def _row_fwd(fn, name, rows, vecs, outs, tm=256):
    S = rows[0][0].shape[0]
    tm = min(tm, S)
    nin = len(rows) + len(vecs)

    def body(*refs):
        res = fn(*[r[...] for r in refs[:nin]])
        for o_ref, r in zip(refs[nin:], res):
            o_ref[...] = r.astype(o_ref.dtype)

    return pl.pallas_call(
        body, name=name, grid=(S // tm,),
        in_specs=[_rspec(tm, w, cb) for (_, cb, w) in rows] + [_vspec(v.shape) for v in vecs],
        out_specs=[_rspec(tm, w, 0) for (w, _) in outs],
        out_shape=[jax.ShapeDtypeStruct((S, w), dt) for (w, dt) in outs],
        compiler_params=_cp(("parallel",)),
    )(*[r[0] for r in rows], *vecs)


def _row_bwd(fn, name, rows, vecs, cts, drows, dvecs, tm=256):
    S = rows[0][0].shape[0]
    tm = min(tm, S)
    nr, nv, nc = len(rows), len(vecs), len(cts)
    didx = list(drows) + [nr + j for j in dvecs]

    def body(*refs):
        vals = [r[...] for r in refs[:nr + nv]]
        ct = tuple(r[...].astype(F32) for r in refs[nr + nv:nr + nv + nc])
        outs = refs[nr + nv + nc:]

        def g(*d):
            a = list(vals)
            for k, val in zip(didx, d):
                a[k] = val
            return tuple(fn(*a))

        _, vjp = jax.vjp(g, *[vals[k] for k in didx])
        grads = vjp(ct)
        for o, gr in zip(outs[:len(drows)], grads[:len(drows)]):
            o[...] = gr.astype(o.dtype)

        @pl.when(pl.program_id(0) == 0)
        def _():
            for o in outs[len(drows):]:
                o[...] = jnp.zeros_like(o)

        for o, gr in zip(outs[len(drows):], grads[len(drows):]):
            o[...] += gr

    return pl.pallas_call(
        body, name=name, grid=(S // tm,),
        in_specs=[_rspec(tm, w, cb) for (_, cb, w) in rows] + [_vspec(v.shape) for v in vecs]
        + [_rspec(tm, w, cb) for (_, cb, w) in cts],
        out_specs=[_rspec(tm, rows[k][2], 0) for k in drows] + [_vspec(vecs[j].shape) for j in dvecs],
        out_shape=[jax.ShapeDtypeStruct((S, rows[k][2]), F32) for k in drows]
        + [jax.ShapeDtypeStruct(vecs[j].shape, F32) for j in dvecs],
        compiler_params=_cp(("arbitrary",)),
    )(*[r[0] for r in rows], *vecs, *[c[0] for c in cts])


def _rms(x):
    return x * lax.rsqrt(jnp.mean(x * x, axis=-1, keepdims=True) + EPS)


def fn_norm_mod(x, nw, sc, sh):
    return (_rms(x) * nw * (1.0 + sc) + sh,)


def fn_norm_mod_pass(x, nw, sc, sh):
    return (x, _rms(x) * nw * (1.0 + sc) + sh)


def fn_resid_norm(x, d, g, nw, sc, sh):
    xn = x + g * d
    return (xn, _rms(xn) * nw * (1.0 + sc) + sh)


def fn_lat_norm(qa, kva, qw, kvw):
    return (_rms(qa) * qw, _rms(kva) * kvw)


@functools.partial(jax.custom_vjp, nondiff_argnums=(1,))
def _lroll(x, s):
    return pltpu.roll(x, s, 1)


def _lroll_fwd(x, s):
    return pltpu.roll(x, s, 1), None


def _lroll_bwd(s, _, g):
    return (pltpu.roll(g, (LANES - s) % LANES, 1),)


_lroll.defvjp(_lroll_fwd, _lroll_bwd)


def _lane_masks(shape):
    lane = lax.broadcasted_iota(jnp.int32, shape, 1)
    return (lane < NOPE).astype(F32), ((lane >= NOPE) & (lane < QK_DIM)).astype(F32)


def _rope(t, tc, ts1, ts2):
    return t * tc + _lroll(t, 16) * ts1 + _lroll(t, LANES - 16) * ts2


def fn_qk_prep(q, kv, kpe, tc, ts1, ts2, wq, wk):
    mn, mp = _lane_masks((1, LANES))
    mhi = 1.0 - mn

    def head_norm(t, w):
        rn = lax.rsqrt(jnp.sum(t * t * mn, axis=-1, keepdims=True) * (1.0 / NOPE) + EPS)
        rp = lax.rsqrt(jnp.sum(t * t * mp, axis=-1, keepdims=True) * (1.0 / ROPE) + EPS)
        return t * (rn * mn + rp * mp) * w

    kp = _rope(head_norm(_lroll(kpe, NOPE), wk) * mp, tc, ts1, ts2)
    qs, ks, vs = [], [], []
    for h in range(HEADS):
        qs.append(_rope(head_norm(q[:, h * LANES:(h + 1) * LANES], wq), tc, ts1, ts2))
        t = kv[:, h * LANES:(h + 1) * LANES]
        ks.append(head_norm(t, wk) * mn + kp)
        vs.append(_lroll(t, NOPE) * mn + mhi)
    return (jnp.concatenate(qs, axis=1), jnp.concatenate(ks, axis=1), jnp.concatenate(vs, axis=1))


def fn_attn_delta(do, o):
    mn, _ = _lane_masks((1, LANES))
    mhi = 1.0 - mn
    out = []
    for hp in range(HEADS // 2):
        y = do[:, hp * LANES:(hp + 1) * LANES] * o[:, hp * LANES:(hp + 1) * LANES]
        out.append(jnp.sum(y * mn, axis=-1, keepdims=True) * mn + jnp.sum(y * mhi, axis=-1, keepdims=True) * mhi)
    return (jnp.concatenate(out, axis=1),)


def fn_gated_norm(y, z, w):
    u = y * jax.nn.silu(z)
    half = D_SSD // 2
    return (jnp.concatenate([_rms(u[:, :half]), _rms(u[:, half:])], axis=1) * w,)


def fn_gated_mix(y, z, ao, w):
    return (jnp.concatenate([ao, fn_gated_norm(y, z, w)[0]], axis=1),)


def fn_swiglu(g, u):
    return (jax.nn.silu(g) * u,)


def _final(x1, ff, tgt, g2, name):
    S = x1.shape[0]
    tm = min(256, S)

    def body(x_ref, f_ref, t_ref, g_ref, dx_ref, df_ref, dg_ref, l_ref):
        @pl.when(pl.program_id(0) == 0)
        def _():
            dg_ref[...] = jnp.zeros_like(dg_ref)
            l_ref[...] = jnp.zeros_like(l_ref)

        f = f_ref[...]
        g = g_ref[...]
        e = x_ref[...] + g * f - t_ref[...]
        dx = e * (1.0 / D_MODEL)
        dx_ref[...] = dx
        df_ref[...] = g * dx
        dg_ref[...] += jnp.sum(dx * f, axis=0, keepdims=True)
        l_ref[...] += jnp.sum(e * e) * (0.5 / D_MODEL)

    r = _rspec(tm, D_MODEL, 0)
    return pl.pallas_call(
        body, name=name, grid=(S // tm,),
        in_specs=[r, r, r, _vspec((1, D_MODEL))],
        out_specs=[r, r, _vspec((1, D_MODEL)), _vspec((1, LANES))],
        out_shape=[jax.ShapeDtypeStruct((S, D_MODEL), F32), jax.ShapeDtypeStruct((S, D_MODEL), F32),
                   jax.ShapeDtypeStruct((1, D_MODEL), F32), jax.ShapeDtypeStruct((1, LANES), F32)],
        compiler_params=_cp(("arbitrary",)),
    )(x1, ff, tgt, g2)


def _causal_mask(t):
    r = lax.broadcasted_iota(jnp.int32, (t, t), 0)
    c = lax.broadcasted_iota(jnp.int32, (t, t), 1)
    return c <= r


LOG2E = 1.4426950408889634
EXP2_SCALE = ATT_SCALE * LOG2E
ATT_TQ, ATT_TK = 512, 1024
ATT_BQ, ATT_BK = 1024, 512


def _attn_fwd(qf, kf, va, name):
    S = qf.shape[0]
    T, TK = min(ATT_TQ, S), min(ATT_TK, S)
    nmask = max(1, T // TK)

    def body(q_ref, k_ref, v_ref, o_ref, l_ref):
        i = pl.program_id(1)
        r = lax.broadcasted_iota(jnp.int32, (T, TK), 0)
        c = lax.broadcasted_iota(jnp.int32, (T, TK), 1)
        qs = [q_ref[:, hh * LANES:(hh + 1) * LANES] for hh in range(2)]

        def blk(j, carry, masked):
            off = pl.multiple_of(j * TK, TK)
            out = []
            for hh in range(2):
                m, acc = carry[hh]
                s = _dot(qs[hh], k_ref[pl.ds(off, TK), hh * LANES:(hh + 1) * LANES], ((1,), (1,)))
                if masked:
                    s = jnp.where(c + j * TK <= r + i * T, s, NEG)
                mn = jnp.maximum(m, jnp.max(s, axis=1, keepdims=True))
                p = jnp.exp2((s - mn) * EXP2_SCALE)
                al = jnp.exp2((m - mn) * EXP2_SCALE)
                vj = v_ref[pl.ds(off, TK), hh * LANES:(hh + 1) * LANES]
                out.append((mn, al * acc + _dot(p.astype(BF16), vj, ((1,), (0,)))))
            return tuple(out)

        one = (jnp.full((T, 1), NEG, F32), jnp.zeros((T, LANES), F32))
        nfull = lax.div(i * T, TK)
        carry = lax.fori_loop(0, nfull, lambda j, cr: blk(j, cr, False), (one, one))
        for t in range(nmask):
            carry = blk(nfull + t, carry, True)
        for hh in range(2):
            m, acc = carry[hh]
            l = acc[:, 64:65]
            o_ref[:, hh * 64:(hh + 1) * 64] = (acc / l)[:, :64]
            l_ref[:, hh * 64:(hh + 1) * 64] = jnp.broadcast_to(m * ATT_SCALE + jnp.log(l), (T, 64))

    return pl.pallas_call(
        body, name=name, grid=(HEADS // 2, S // T),
        in_specs=[pl.BlockSpec((T, 256), lambda h, i: (i, h)), pl.BlockSpec((S, 256), lambda h, i: (0, h)),
                  pl.BlockSpec((S, 256), lambda h, i: (0, h))],
        out_specs=[pl.BlockSpec((T, LANES), lambda h, i: (i, h)), pl.BlockSpec((T, LANES), lambda h, i: (i, h))],
        out_shape=[jax.ShapeDtypeStruct((S, D_SSD), F32), jax.ShapeDtypeStruct((S, D_SSD), F32)],
        compiler_params=_cp(("parallel", "parallel")),
    )(qf, kf, va)


def _attn_bwd(qf, kf, kT, va, do, lse_r, delta_r, name):
    S = qf.shape[0]
    T, TK = min(ATT_BQ, S), min(ATT_BK, S)
    nq = S // T
    nmask = max(1, TK // T)

    def body(q_ref, k_ref, kT_ref, v_ref, do_ref, l_ref, d_ref, dqT_ref, dk_ref, dv_ref):
        j = pl.program_id(1)

        @pl.when(j == 0)
        def _():
            dqT_ref[...] = jnp.zeros_like(dqT_ref)

        r = lax.broadcasted_iota(jnp.int32, (TK, T), 0)
        c = lax.broadcasted_iota(jnp.int32, (TK, T), 1)
        lo = (lax.broadcasted_iota(jnp.int32, (1, LANES), 1) < 64).astype(F32)
        ks = [k_ref[:, hh * LANES:(hh + 1) * LANES] for hh in range(2)]
        vs = [v_ref[:, hh * LANES:(hh + 1) * LANES] for hh in range(2)]
        kTs = [kT_ref[hh * LANES:(hh + 1) * LANES, :] for hh in range(2)]

        def blk(i, carry, masked):
            off = pl.multiple_of(i * T, T)
            dall = do_ref[pl.ds(off, T), :]
            out = []
            for hh in range(2):
                dk, dv = carry[hh]
                q = q_ref[pl.ds(off, T), hh * LANES:(hh + 1) * LANES]
                dop = ((dall if hh == 0 else pltpu.roll(dall, 64, 1)) * lo).astype(BF16)
                lrow = l_ref[0, hh:hh + 1, pl.ds(off, T)]
                drow = d_ref[0, hh:hh + 1, pl.ds(off, T)]
                pT = jnp.exp2(_dot(ks[hh], q, ((1,), (1,))) * EXP2_SCALE - lrow)
                if masked:
                    pT = jnp.where(r + j * TK <= c + i * T, pT, 0.0)
                dpT = _dot(vs[hh], dop, ((1,), (1,)))
                dsT = (pT * (dpT - drow) * ATT_SCALE).astype(BF16)
                dv = dv + _dot(pT.astype(BF16), dop, ((1,), (0,)))
                dk = dk + _dot(dsT, q, ((1,), (0,)))
                dqT_ref[hh * LANES:(hh + 1) * LANES, pl.ds(off, T)] += _dot(kTs[hh], dsT, ((1,), (0,)))
                out.append((dk, dv))
            return tuple(out)

        z = (jnp.zeros((TK, LANES), F32), jnp.zeros((TK, LANES), F32))
        first = lax.div(j * TK, T)
        carry = (z, z)
        for t in range(nmask):
            carry = blk(first + t, carry, True)
        carry = lax.fori_loop(first + nmask, nq, lambda i, cr: blk(i, cr, False), carry)
        for hh in range(2):
            dk_ref[:, hh * LANES:(hh + 1) * LANES] = carry[hh][0]
            dv_ref[:, hh * LANES:(hh + 1) * LANES] = carry[hh][1]

    return pl.pallas_call(
        body, name=name, grid=(HEADS // 2, S // TK),
        in_specs=[pl.BlockSpec((S, 256), lambda h, j: (0, h)), pl.BlockSpec((TK, 256), lambda h, j: (j, h)),
                  pl.BlockSpec((256, TK), lambda h, j: (h, j)), pl.BlockSpec((TK, 256), lambda h, j: (j, h)),
                  pl.BlockSpec((S, LANES), lambda h, j: (0, h)), pl.BlockSpec((1, 2, S), lambda h, j: (h, 0, 0)),
                  pl.BlockSpec((1, 2, S), lambda h, j: (h, 0, 0))],
        out_specs=[pl.BlockSpec((256, S), lambda h, j: (h, 0)), pl.BlockSpec((TK, 256), lambda h, j: (j, h)),
                   pl.BlockSpec((TK, 256), lambda h, j: (j, h))],
        out_shape=[jax.ShapeDtypeStruct((D_MODEL, S), F32), jax.ShapeDtypeStruct((S, D_MODEL), F32),
                   jax.ShapeDtypeStruct((S, D_MODEL), F32)],
        compiler_params=_cp(("parallel", "arbitrary")),
    )(qf, kf, kT, va, do, lse_r, delta_r)


def _shift_down(x, s):
    if s == 0:
        return x
    rows = lax.broadcasted_iota(jnp.int32, x.shape, 0)
    return jnp.where(rows >= s, pltpu.roll(x, s, 0), 0.0)


def _shift_up(x, s):
    if s == 0:
        return x
    n = x.shape[0]
    rows = lax.broadcasted_iota(jnp.int32, x.shape, 0)
    return jnp.where(rows < n - s, pltpu.roll(x, n - s, 0), 0.0)


def _conv_fwd(proj, cvec, name):
    S = proj.shape[0]

    def body(x_ref, c_ref, o_ref):
        x = x_ref[...]
        y = jnp.broadcast_to(c_ref[4:5, :], x.shape)
        for k in range(CONV_TAPS):
            y = y + c_ref[k:k + 1, :] * _shift_down(x, CONV_TAPS - 1 - k)
        o_ref[...] = y * jax.nn.sigmoid(y)

    return pl.pallas_call(
        body, name=name, grid=(D_CONV // LANES,),
        in_specs=[pl.BlockSpec((S, LANES), lambda j: (0, 8 + j)), pl.BlockSpec((8, LANES), lambda j: (0, j))],
        out_specs=pl.BlockSpec((S, LANES), lambda j: (0, j)),
        out_shape=jax.ShapeDtypeStruct((S, D_CONV), F32),
        compiler_params=_cp(("parallel",)),
    )(proj, cvec)


def _conv_bwd(proj, cvec, dact, name):
    S = proj.shape[0]

    def body(x_ref, c_ref, d_ref, dx_ref, dc_ref):
        x = x_ref[...]
        y = jnp.broadcast_to(c_ref[4:5, :], x.shape)
        for k in range(CONV_TAPS):
            y = y + c_ref[k:k + 1, :] * _shift_down(x, CONV_TAPS - 1 - k)
        sg = jax.nn.sigmoid(y)
        dy = d_ref[...] * (sg * (1.0 + y * (1.0 - sg)))
        dx = jnp.zeros_like(x)
        for k in range(CONV_TAPS):
            s = CONV_TAPS - 1 - k
            dx = dx + c_ref[k:k + 1, :] * _shift_up(dy, s)
            dc_ref[k:k + 1, :] = jnp.sum(dy * _shift_down(x, s), axis=0, keepdims=True)
        dx_ref[...] = dx
        dc_ref[4:5, :] = jnp.sum(dy, axis=0, keepdims=True)
        dc_ref[5:8, :] = jnp.zeros((3, LANES), F32)

    return pl.pallas_call(
        body, name=name, grid=(D_CONV // LANES,),
        in_specs=[pl.BlockSpec((S, LANES), lambda j: (0, 8 + j)), pl.BlockSpec((8, LANES), lambda j: (0, j)),
                  pl.BlockSpec((S, LANES), lambda j: (0, j))],
        out_specs=[pl.BlockSpec((S, LANES), lambda j: (0, j)), pl.BlockSpec((8, LANES), lambda j: (0, j))],
        out_shape=[jax.ShapeDtypeStruct((S, D_CONV), F32), jax.ShapeDtypeStruct((8, D_CONV), F32)],
        compiler_params=_cp(("parallel",)),
    )(proj, cvec, dact)


def fn_ssd_chunk(xs, bm, cm, dtr, state, vecs):
    Q = CHUNK
    dt = jax.nn.softplus(dtr + vecs[0:1])
    a = -jnp.exp(vecs[1:2])
    d_skip = vecs[2:3]
    adt = dt * a
    tril = _causal_mask(Q)
    acs = _dot(tril.astype(F32), adt, ((1,), (0,)), lax.Precision.HIGHEST)
    acs_t = acs.T
    ys, new_states = [], []
    Bs = [bm[:, g * SSD_N:(g + 1) * SSD_N].astype(BF16) for g in range(2)]
    Cs = [cm[:, g * SSD_N:(g + 1) * SSD_N].astype(BF16) for g in range(2)]
    Gs = [_dot(Cs[g], Bs[g], ((1,), (1,))) for g in range(2)]
    for h in range(SSD_HEADS):
        g = h // (SSD_HEADS // 2)
        x = xs[:, h * SSD_P:(h + 1) * SSD_P]
        B, C = Bs[g], Cs[g]
        xdt = x * dt[:, h:h + 1]
        A = acs[:, h:h + 1]
        L = jnp.exp(jnp.where(tril, A - acs_t[h:h + 1, :], -jnp.inf))
        M = Gs[g] * L
        yd = _dot(M.astype(BF16), xdt.astype(BF16), ((1,), (0,)))
        st = state[h]
        yo = _dot(C, st.astype(BF16), ((1,), (1,))) * jnp.exp(A)
        alast = acs[Q - 1:Q, h:h + 1]
        U = xdt * jnp.exp(alast - A)
        new_states.append(jnp.exp(alast) * st + _dot(U.astype(BF16), B, ((0,), (0,))))
        ys.append(yd + yo + d_skip[:, h:h + 1] * x)
    return jnp.concatenate(ys, axis=1), jnp.stack(new_states)


def _ssd_fwd(xact, proj, svec, name):
    S = xact.shape[0]
    nc = S // CHUNK

    def body(x_ref, dt_ref, v_ref, y_ref, st_ref, state):
        @pl.when(pl.program_id(0) == 0)
        def _():
            state[...] = jnp.zeros_like(state)

        st_ref[0] = state[...]
        x = x_ref[...]
        y, sn = fn_ssd_chunk(x[:, 0:512], x[:, 512:768], x[:, 768:1024], dt_ref[...], state[...], v_ref[...])
        y_ref[...] = y
        state[...] = sn

    return pl.pallas_call(
        body, name=name, grid=(nc,),
        in_specs=[pl.BlockSpec((CHUNK, D_CONV), lambda i: (i, 0)), pl.BlockSpec((CHUNK, LANES), lambda i: (i, 16)),
                  pl.BlockSpec((8, LANES), lambda i: (0, 0))],
        out_specs=[pl.BlockSpec((CHUNK, D_SSD), lambda i: (i, 0)),
                   pl.BlockSpec((1, SSD_HEADS, SSD_P, SSD_N), lambda i: (i, 0, 0, 0))],
        out_shape=[jax.ShapeDtypeStruct((S, D_SSD), F32), jax.ShapeDtypeStruct((nc, SSD_HEADS, SSD_P, SSD_N), F32)],
        scratch_shapes=[pltpu.VMEM((SSD_HEADS, SSD_P, SSD_N), F32)],
        compiler_params=_cp(("arbitrary",)),
    )(xact, proj, svec)


def _ssd_bwd(xact, proj, svec, states, dy, name):
    S = xact.shape[0]
    nc = S // CHUNK

    def body(x_ref, dt_ref, v_ref, st_ref, dy_ref, dx_ref, ddt_ref, dv_ref, dstate):
        @pl.when(pl.program_id(0) == 0)
        def _():
            dstate[...] = jnp.zeros_like(dstate)
            dv_ref[...] = jnp.zeros_like(dv_ref)

        x = x_ref[...]
        _, vjp = jax.vjp(fn_ssd_chunk, x[:, 0:512], x[:, 512:768], x[:, 768:1024], dt_ref[...], st_ref[0], v_ref[...])
        dxs, dbm, dcm, ddt, dst, dvec = vjp((dy_ref[...], dstate[...]))
        dx_ref[:, 0:512] = dxs
        dx_ref[:, 512:768] = dbm
        dx_ref[:, 768:1024] = dcm
        ddt_ref[...] = ddt
        dstate[...] = dst
        dv_ref[...] += dvec

    rev = lambda i: (nc - 1 - i, 0)
    return pl.pallas_call(
        body, name=name, grid=(nc,),
        in_specs=[pl.BlockSpec((CHUNK, D_CONV), rev), pl.BlockSpec((CHUNK, LANES), lambda i: (nc - 1 - i, 16)),
                  pl.BlockSpec((8, LANES), lambda i: (0, 0)),
                  pl.BlockSpec((1, SSD_HEADS, SSD_P, SSD_N), lambda i: (nc - 1 - i, 0, 0, 0)),
                  pl.BlockSpec((CHUNK, D_SSD), rev)],
        out_specs=[pl.BlockSpec((CHUNK, D_CONV), rev), pl.BlockSpec((CHUNK, LANES), rev),
                   pl.BlockSpec((8, LANES), lambda i: (0, 0))],
        out_shape=[jax.ShapeDtypeStruct((S, D_CONV), F32), jax.ShapeDtypeStruct((S, LANES), F32),
                   jax.ShapeDtypeStruct((8, LANES), F32)],
        scratch_shapes=[pltpu.VMEM((SSD_HEADS, SSD_P, SSD_N), F32)],
        compiler_params=_cp(("arbitrary",)),
    )(xact, proj, svec, states, dy)


def _ada_fwd(c_all, w_ada, b_sh, name):
    nb = 1536 // 512

    def body(c_ref, w_ref, b_ref, o_ref):
        ca = jax.nn.silu(c_ref[...]).astype(BF16)
        o_ref[0] = _dot(ca, w_ref[0].astype(BF16), ((1,), (0,))) + b_ref[0]

    return pl.pallas_call(
        body, name=name, grid=(DEPTH, nb),
        in_specs=[pl.BlockSpec((8, D_MODEL), lambda l, j: (0, 0)), pl.BlockSpec((1, D_MODEL, 512), lambda l, j: (l, 0, j)),
                  pl.BlockSpec((1, 1, 512), lambda l, j: (l, 0, j))],
        out_specs=pl.BlockSpec((1, 8, 512), lambda l, j: (l, 0, j)),
        out_shape=jax.ShapeDtypeStruct((DEPTH, 8, 1536), F32),
        compiler_params=_cp(("parallel", "parallel")),
    )(c_all, w_ada, b_sh)


def _ada_bwd(c_all_t, dmod_sh, name):
    nb = 1536 // 512

    def body(c_ref, d_ref, o_ref):
        ca = jax.nn.silu(c_ref[...])
        acc = ca[:, 0:1] * d_ref[0, 0:1, :]
        for b in range(1, 8):
            acc = acc + ca[:, b:b + 1] * d_ref[0, b:b + 1, :]
        o_ref[0] = acc

    return pl.pallas_call(
        body, name=name, grid=(DEPTH, nb),
        in_specs=[pl.BlockSpec((D_MODEL, 8), lambda l, j: (0, 0)), pl.BlockSpec((1, 8, 512), lambda l, j: (l, 0, j))],
        out_specs=pl.BlockSpec((1, D_MODEL, 512), lambda l, j: (l, 0, j)),
        out_shape=jax.ShapeDtypeStruct((DEPTH, D_MODEL, 1536), F32),
        compiler_params=_cp(("parallel", "parallel")),
    )(c_all_t, dmod_sh)


def _rows_tile(rows):
    return next(t for t in (512, 256, 128, 64, 32, 16, 8) if rows % t == 0)


def _sum_parts(parts, rows, width, name):
    bm = _rows_tile(rows)

    def body(*refs):
        acc = refs[0][...]
        for r in refs[1:-1]:
            acc = acc + r[...]
        refs[-1][...] = acc

    return pl.pallas_call(
        body, name=name, grid=(rows // bm,),
        in_specs=[pl.BlockSpec((bm, width), functools.partial(lambda i, o: (i + o, 0), o=off // bm)) for (_, off) in parts],
        out_specs=pl.BlockSpec((bm, width), lambda i: (i, 0)),
        out_shape=jax.ShapeDtypeStruct((rows, width), F32),
        compiler_params=_cp(("parallel",)),
    )(*[p[0] for p in parts])


def _adam(w, m, v, parts, name):
    rows, width = w.shape
    bm = min(256, _rows_tile(rows))
    np_ = len(parts)
    c1 = 1.0 / (1.0 - ADAM_B1 ** ADAM_STEP)
    c2 = 1.0 / (1.0 - ADAM_B2 ** ADAM_STEP)

    def body(*refs):
        w_ref, m_ref, v_ref = refs[:3]
        g = refs[3][...]
        for r in refs[4:3 + np_]:
            g = g + r[...]
        g_ref, d_ref, nm_ref, nv_ref = refs[3 + np_:]
        nm = ADAM_B1 * m_ref[...] + (1.0 - ADAM_B1) * g
        nv = ADAM_B2 * v_ref[...] + (1.0 - ADAM_B2) * (g * g)
        g_ref[...] = g
        nm_ref[...] = nm
        nv_ref[...] = nv
        d_ref[...] = -ADAM_LR * ((nm * c1) / (jnp.sqrt(nv * c2) + ADAM_EPS) + ADAM_WD * w_ref[...])

    blk = pl.BlockSpec((bm, width), lambda i: (i, 0))
    return pl.pallas_call(
        body, name=name, grid=(rows // bm,),
        in_specs=[blk, blk, blk] + [pl.BlockSpec((bm, width), functools.partial(lambda i, o: (i + o, 0), o=off // bm))
                                    for (_, off) in parts],
        out_specs=[blk, blk, blk, blk],
        out_shape=[jax.ShapeDtypeStruct((rows, width), F32)] * 4,
        compiler_params=_cp(("parallel",)),
    )(w, m, v, *[p[0] for p in parts])


def _coords():
    return lax.axis_index("x"), lax.axis_index("y"), lax.axis_index("c")


def _other_chips(x, y):
    return [(1 - x, y), (x, 1 - y), (1 - x, 1 - y)]


def _ag8(blk, name):
    m_per, n = blk.shape

    def body(x_ref, out_ref, send_sems, recv_sems, local_sem):
        x, y, c = _coords()
        me, sibling = (x, y, c), (x, y, 1 - c)
        chips = _other_chips(x, y)

        def rows(px, py, pc):
            return out_ref.at[pl.ds((4 * px + 2 * py + pc) * m_per, m_per), :]

        def copy(k, block, to, src=None):
            return pltpu.make_async_remote_copy(
                src_ref=rows(*block) if src is None else src, dst_ref=rows(*block),
                send_sem=send_sems.at[k], recv_sem=recv_sems.at[k], device_id=to, device_id_type=MESH)

        mine = pltpu.make_async_copy(x_ref, rows(*me), local_sem)
        mine.start()
        first = [copy(0, me, sibling, src=x_ref)]
        first += [copy(1 + j, me, (*chip, c), src=x_ref) for j, chip in enumerate(chips)]
        for cp in first:
            cp.start()
        passed = [copy(4 + j, (*chip, c), sibling) for j, chip in enumerate(chips)]
        for j, chip in enumerate(chips):
            copy(1 + j, (*chip, c), me).wait_recv()
            passed[j].start()
        copy(0, sibling, me).wait_recv()
        for j, chip in enumerate(chips):
            copy(4 + j, (*chip, 1 - c), me).wait_recv()
        for cp in first + passed:
            cp.wait_send()
        mine.wait()

    return pl.pallas_call(
        body, name=name,
        out_shape=jax.ShapeDtypeStruct((8 * m_per, n), blk.dtype),
        in_specs=[pl.BlockSpec(memory_space=pltpu.VMEM)], out_specs=pl.BlockSpec(memory_space=pltpu.VMEM),
        scratch_shapes=[pltpu.SemaphoreType.DMA((7,)), pltpu.SemaphoreType.DMA((7,)), pltpu.SemaphoreType.DMA],
    )(blk)


def _ag_weights(shard, name):
    _, hh, wd = shard.shape

    def body(sh_ref, out_ref, send_sems, recv_sems, local_sem):
        x, y, c = _coords()
        sibling = (x, y, 1 - c)
        chips = _other_chips(x, y)

        def blk(px, py, pc):
            return out_ref.at[2 * px + py, pc]

        def copy(k, src, dst, to):
            return pltpu.make_async_remote_copy(src_ref=src, dst_ref=dst, send_sem=send_sems.at[k],
                                                recv_sem=recv_sems.at[k], device_id=to, device_id_type=MESH)

        mine = pltpu.make_async_copy(sh_ref, out_ref.at[2 * x + y], local_sem)
        mine.start()
        first = [copy(j, sh_ref.at[c], blk(x, y, c), (*chip, c)) for j, chip in enumerate(chips)]
        for cp in first:
            cp.start()
        passed = [copy(3 + j, blk(*chip, c), blk(*chip, c), sibling) for j, chip in enumerate(chips)]
        for j, chip in enumerate(chips):
            copy(j, sh_ref.at[c], blk(*chip, c), (x, y, c)).wait_recv()
            passed[j].start()
        for j, chip in enumerate(chips):
            copy(3 + j, sh_ref.at[c], blk(*chip, 1 - c), (x, y, c)).wait_recv()
        for cp in first + passed:
            cp.wait_send()
        mine.wait()

    return pl.pallas_call(
        body, name=name,
        out_shape=jax.ShapeDtypeStruct((4, 2, hh, wd), shard.dtype),
        in_specs=[pl.BlockSpec(memory_space=pl.ANY)], out_specs=pl.BlockSpec(memory_space=pl.ANY),
        scratch_shapes=[pltpu.SemaphoreType.DMA((6,)), pltpu.SemaphoreType.DMA((6,)), pltpu.SemaphoreType.DMA],
    )(shard)


def _rs_sibling(g, name):
    _, _, hh, wd = g.shape

    def body(g_ref, out_ref, send_sems, recv_sems):
        x, y, c = _coords()
        cps = [pltpu.make_async_remote_copy(src_ref=g_ref.at[s, 1 - c], dst_ref=out_ref.at[s], send_sem=send_sems.at[s],
                                            recv_sem=recv_sems.at[s], device_id=(x, y, 1 - c), device_id_type=MESH)
               for s in range(4)]
        for cp in cps:
            cp.start()
        for cp in cps:
            cp.wait_recv()
        for cp in cps:
            cp.wait_send()

    return pl.pallas_call(
        body, name=name,
        out_shape=jax.ShapeDtypeStruct((4, hh, wd), g.dtype),
        in_specs=[pl.BlockSpec(memory_space=pl.ANY)], out_specs=pl.BlockSpec(memory_space=pl.ANY),
        scratch_shapes=[pltpu.SemaphoreType.DMA((4,)), pltpu.SemaphoreType.DMA((4,))],
    )(g)


def _rs_chips(cs, name):
    _, hh, wd = cs.shape

    def body(c_ref, out_ref, send_sems, recv_sems):
        x, y, c = _coords()
        cps = [pltpu.make_async_remote_copy(src_ref=c_ref.at[2 * px + py], dst_ref=out_ref.at[j], send_sem=send_sems.at[j],
                                            recv_sem=recv_sems.at[j], device_id=(px, py, c), device_id_type=MESH)
               for j, (px, py) in enumerate(_other_chips(x, y))]
        for cp in cps:
            cp.start()
        for cp in cps:
            cp.wait_recv()
        for cp in cps:
            cp.wait_send()

    return pl.pallas_call(
        body, name=name,
        out_shape=jax.ShapeDtypeStruct((3, hh, wd), cs.dtype),
        in_specs=[pl.BlockSpec(memory_space=pl.ANY)], out_specs=pl.BlockSpec(memory_space=pl.ANY),
        scratch_shapes=[pltpu.SemaphoreType.DMA((3,)), pltpu.SemaphoreType.DMA((3,))],
    )(cs)


def _swap_halves(gh, name):
    hh, wd = gh.shape

    def body(g_ref, out_ref, send_sem, recv_sem, local_sem):
        x, y, c = _coords()
        mine = pltpu.make_async_copy(g_ref, out_ref.at[c], local_sem)
        mine.start()
        cp = pltpu.make_async_remote_copy(src_ref=g_ref, dst_ref=out_ref.at[c], send_sem=send_sem, recv_sem=recv_sem,
                                          device_id=(x, y, 1 - c), device_id_type=MESH)
        cp.start()
        pltpu.make_async_remote_copy(src_ref=g_ref, dst_ref=out_ref.at[1 - c], send_sem=send_sem, recv_sem=recv_sem,
                                     device_id=(x, y, 1 - c), device_id_type=MESH).wait_recv()
        cp.wait_send()
        mine.wait()

    return pl.pallas_call(
        body, name=name,
        out_shape=jax.ShapeDtypeStruct((2, hh, wd), gh.dtype),
        in_specs=[pl.BlockSpec(memory_space=pl.ANY)], out_specs=pl.BlockSpec(memory_space=pl.ANY),
        scratch_shapes=[pltpu.SemaphoreType.DMA, pltpu.SemaphoreType.DMA, pltpu.SemaphoreType.DMA],
    )(gh)


def _pad_win(w):
    return jnp.concatenate([w[:, :416], jnp.zeros((w.shape[0], 96), w.dtype), w[:, 416:1952],
                            w[:, 1952:1960], jnp.zeros((w.shape[0], 120), w.dtype)], axis=1)


def _unpad_win(g):
    return jnp.concatenate([g[:, :416], g[:, 512:2048], g[:, 2048:2056]], axis=1)


def _pad_wq(w):
    return jnp.pad(w.reshape(Q_LORA, HEADS, QK_DIM), ((0, 0), (0, 0), (0, LANES - QK_DIM))).reshape(Q_LORA, HEADS * LANES)


def _unpad_wq(g):
    return g.reshape(Q_LORA, HEADS, LANES)[:, :, :QK_DIM].reshape(Q_LORA, HEADS * QK_DIM)


def _pack_big_shard(tree):
    flat = jnp.concatenate([tree[n][l].reshape(-1) for l in range(DEPTH) for (n, _, _) in BIG])
    return jnp.pad(flat, (0, BIG_ROWS * 1024 - flat.shape[0])).reshape(BIG_ROWS, 1024)


def _unpack_big_shard(buf):
    flat = buf.reshape(-1)
    out = {n: [] for (n, _, _) in BIG}
    o = 0
    for l in range(DEPTH):
        for (n, shp, _) in BIG:
            k = shp[0] * shp[1]
            out[n].append(flat[o:o + k].reshape(shp))
            o += k
    return {n: jnp.stack(v) for n, v in out.items()}


def _unpack_big_full(buf):
    out = {n: [] for (n, _, _) in BIG}
    o = 0
    for l in range(DEPTH):
        for (n, shp, ax) in BIG:
            k = shp[0] * shp[1]
            p = buf[:, o:o + k].reshape(4, *shp)
            out[n].append(p.reshape(4 * shp[0], shp[1]) if ax == 0 else p.transpose(1, 0, 2).reshape(shp[0], 4 * shp[1]))
            o += k
    return out


def _pack_big_full(full):
    cols = []
    for l in range(DEPTH):
        for (n, shp, ax) in BIG:
            g = full[n][l]
            cols.append(g.reshape(4, -1) if ax == 0 else g.reshape(shp[0], 4, shp[1]).transpose(1, 0, 2).reshape(4, -1))
    flat = jnp.concatenate(cols, axis=1)
    return jnp.pad(flat, ((0, 0), (0, BIG_ROWS * 1024 - flat.shape[1])))


def _pack_small(tree):
    parts = []
    for l in range(DEPTH):
        for (n, k) in SMALL:
            parts.append(jnp.pad(tree[n][l].reshape(-1), (0, -k % LANES)))
    flat = jnp.concatenate(parts)
    return jnp.pad(flat, (0, SMALL_ROWS * LANES - flat.shape[0])).reshape(SMALL_ROWS, LANES)


def _unpack_small(buf):
    flat = buf.reshape(-1)
    out = {n: [] for (n, _) in SMALL}
    o = 0
    for l in range(DEPTH):
        for (n, k) in SMALL:
            out[n].append(flat[o:o + k])
            o += k + (-k % LANES)
    return {n: jnp.stack(v) for n, v in out.items()}


def _vec(v, width=LANES):
    return jnp.pad(v.reshape(1, -1), ((0, 0), (0, width - v.shape[-1])))


def kernel(x, c, positions, norm1_w, norm2_w, w_ada, b_ada, w_in, q_a_norm_w, w_q_up, kv_a_norm_w, w_kv_up, q_nope_norm_w, q_pe_norm_w, k_nope_norm_w, k_pe_norm_w, conv_w, conv_b, dt_bias, a_log, d_skip, ssd_norm_w, w_out, w_gate_up, w_down, loss_target, m_norm1_w, m_norm2_w, m_w_ada, m_b_ada, m_w_in, m_q_a_norm_w, m_w_q_up, m_kv_a_norm_w, m_w_kv_up, m_q_nope_norm_w, m_q_pe_norm_w, m_k_nope_norm_w, m_k_pe_norm_w, m_conv_w, m_conv_b, m_dt_bias, m_a_log, m_d_skip, m_ssd_norm_w, m_w_out, m_w_gate_up, m_w_down, v_norm1_w, v_norm2_w, v_w_ada, v_b_ada, v_w_in, v_q_a_norm_w, v_w_q_up, v_kv_a_norm_w, v_w_kv_up, v_q_nope_norm_w, v_q_pe_norm_w, v_k_nope_norm_w, v_k_pe_norm_w, v_conv_w, v_conv_b, v_dt_bias, v_a_log, v_d_skip, v_ssd_norm_w, v_w_out, v_w_gate_up, v_w_down):
    W = dict(zip(WEIGHTS, (norm1_w, norm2_w, w_ada, b_ada, w_in, q_a_norm_w, w_q_up, kv_a_norm_w, w_kv_up, q_nope_norm_w, q_pe_norm_w, k_nope_norm_w, k_pe_norm_w, conv_w, conv_b, dt_bias, a_log, d_skip, ssd_norm_w, w_out, w_gate_up, w_down)))
    M = dict(zip(WEIGHTS, (m_norm1_w, m_norm2_w, m_w_ada, m_b_ada, m_w_in, m_q_a_norm_w, m_w_q_up, m_kv_a_norm_w, m_w_kv_up, m_q_nope_norm_w, m_q_pe_norm_w, m_k_nope_norm_w, m_k_pe_norm_w, m_conv_w, m_conv_b, m_dt_bias, m_a_log, m_d_skip, m_ssd_norm_w, m_w_out, m_w_gate_up, m_w_down)))
    V = dict(zip(WEIGHTS, (v_norm1_w, v_norm2_w, v_w_ada, v_b_ada, v_w_in, v_q_a_norm_w, v_w_q_up, v_kv_a_norm_w, v_w_kv_up, v_q_nope_norm_w, v_q_pe_norm_w, v_k_nope_norm_w, v_k_pe_norm_w, v_conv_w, v_conv_b, v_dt_bias, v_a_log, v_d_skip, v_ssd_norm_w, v_w_out, v_w_gate_up, v_w_down)))
    S = x.shape[1]
    xi, yi, ci = _coords()
    chip = 2 * xi + yi
    dev = 2 * chip + ci
    x0 = x[0]
    tgt = loss_target[0]

    inv_freq = 1.0 / (ROPE_THETA ** (jnp.arange(0, ROPE, 2, dtype=F32) / ROPE))
    ang = positions[0].astype(F32)[:, None] * inv_freq
    cos, sin = jnp.cos(ang), jnp.sin(ang)
    z16, z32, z64 = jnp.zeros((S, 16), F32), jnp.zeros((S, 32), F32), jnp.zeros((S, 64), F32)
    tab_c = jnp.concatenate([jnp.ones((S, 64), F32), cos, cos, z32], axis=1)
    tab_s1 = jnp.concatenate([z64, z16, sin, z32], axis=1)
    tab_s2 = jnp.concatenate([z64, -sin, z16, z32], axis=1)

    blk0 = jnp.concatenate([c.reshape(-1), W['conv_w'].reshape(-1)]).reshape(24, LANES)
    g0 = _ag8(blk0, "ag_c_conv").reshape(8, 24 * LANES)
    c_all = g0[:, :D_MODEL]
    conv_full = g0[0::2, D_MODEL:].reshape(4, DEPTH, CONV_TAPS, 256).transpose(1, 2, 0, 3).reshape(DEPTH, CONV_TAPS, D_CONV)

    wsh = _pack_big_shard(W).astype(BF16).reshape(2, HALF_ROWS, 1024)
    wall = _ag_weights(wsh, "ag_weights").reshape(4, BIG_ROWS * 1024)
    full = _unpack_big_full(wall)
    win_p = [_pad_win(full['w_in'][l]) for l in range(DEPTH)]
    wq_p = [_pad_wq(full['w_q_up'][l]) for l in range(DEPTH)]

    b_sh = lax.dynamic_slice_in_dim(W['b_ada'], chip * 1536, 1536, axis=1).reshape(DEPTH, 1, 1536)
    mod_sh = _ada_fwd(c_all, W['w_ada'], b_sh, "ada_fwd")
    g1 = _ag8(mod_sh.reshape(192, LANES), "ag_mod").reshape(8, DEPTH, 8, 1536)
    mod_all = g1[0::2].transpose(1, 2, 0, 3).reshape(DEPTH, 8, 6 * D_MODEL)
    mod = lax.dynamic_index_in_dim(mod_all, dev, axis=1, keepdims=False)

    def mvec(l, k):
        return mod[l, k * D_MODEL:(k + 1) * D_MODEL].reshape(1, D_MODEL)

    def small(name, l, width=None):
        v = W[name][l]
        return _vec(v, width or v.shape[-1])

    def wq_vec(l):
        return _vec(jnp.concatenate([W['q_nope_norm_w'][l], W['q_pe_norm_w'][l]]))

    def wk_vec(l):
        return _vec(jnp.concatenate([W['k_nope_norm_w'][l], W['k_pe_norm_w'][l]]))

    def conv_vec(l):
        return jnp.concatenate([conv_full[l], W['conv_b'][l].reshape(1, D_CONV), jnp.zeros((3, D_CONV), F32)], axis=0)

    def ssd_vec(l):
        return jnp.concatenate([_vec(W['dt_bias'][l]), _vec(W['a_log'][l]), _vec(W['d_skip'][l]), jnp.zeros((5, LANES), F32)], axis=0)

    sv = []
    xcur = x0
    h1 = _row_fwd(fn_norm_mod, "norm_mod_f", [(x0, 0, D_MODEL)], [small('norm1_w', 0), mvec(0, 1), mvec(0, 0)],
                  [(D_MODEL, BF16)])[0]
    fin = None
    for l in range(DEPTH):
        t = dict(xcur=xcur, h1=h1)
        t['proj'] = proj = _mm(h1, win_p[l], 'nn', f"mm_in_{l}")
        t['qa_n'], t['kva_n'] = _row_fwd(fn_lat_norm, f"lat_norm_f{l}", [(proj, 0, 256), (proj, 2, 128)],
                                         [small('q_a_norm_w', l), small('kv_a_norm_w', l)], [(256, BF16), (128, BF16)])
        t['q'] = _mm(t['qa_n'], wq_p[l], 'nn', f"mm_q_{l}")
        t['kv'] = _mm(t['kva_n'], full['w_kv_up'][l], 'nn', f"mm_kv_{l}")
        t['qf'], t['kf'], t['vv'] = _row_fwd(
            fn_qk_prep, f"qk_prep_f{l}",
            [(t['q'], 0, 1024), (t['kv'], 0, 1024), (proj, 3, 128), (tab_c, 0, 128), (tab_s1, 0, 128), (tab_s2, 0, 128)],
            [wq_vec(l), wk_vec(l)], [(1024, BF16), (1024, BF16), (1024, BF16)])
        t['ao'], t['lse'] = _attn_fwd(t['qf'], t['kf'], t['vv'], f"attn_f{l}")
        t['xact'] = _conv_fwd(proj, conv_vec(l), f"conv_f{l}")
        t['y'], t['states'] = _ssd_fwd(t['xact'], proj, ssd_vec(l), f"ssd_f{l}")
        t['mix'] = _row_fwd(fn_gated_mix, f"gated_f{l}", [(t['y'], 0, 512), (proj, 1, 512), (t['ao'], 0, 512)],
                            [small('ssd_norm_w', l)], [(1024, BF16)])[0]
        t['mo'] = _mm(t['mix'], full['w_out'][l], 'nn', f"mm_out_{l}")
        t['x1'], t['h2'] = _row_fwd(fn_resid_norm, f"resid_mid_f{l}", [(xcur, 0, D_MODEL), (t['mo'], 0, D_MODEL)],
                                    [mvec(l, 2), small('norm2_w', l), mvec(l, 4), mvec(l, 3)],
                                    [(D_MODEL, F32), (D_MODEL, BF16)])
        t['gu'] = _mm(t['h2'], full['w_gate_up'][l], 'nn', f"mm_gu_{l}")
        t['act'] = _row_fwd(fn_swiglu, f"swiglu_f{l}", [(t['gu'], 0, D_FF), (t['gu'], 1, D_FF)], [], [(D_FF, BF16)], tm=128)[0]
        t['ff'] = _mm(t['act'], full['w_down'][l], 'nn', f"mm_down_{l}")
        if l + 1 < DEPTH:
            xcur, h1 = _row_fwd(fn_resid_norm, f"resid_end_f{l}", [(t['x1'], 0, D_MODEL), (t['ff'], 0, D_MODEL)],
                                [mvec(l, 5), small('norm1_w', l + 1), mvec(l + 1, 1), mvec(l + 1, 0)],
                                [(D_MODEL, F32), (D_MODEL, BF16)])
        else:
            fin = _final(t['x1'], t['ff'], tgt, mvec(l, 5), "final_loss")
        sv.append(t)

    dx1, dff, dg2_last, loss_acc = fin
    loss = lax.psum(loss_acc[0, 0], ("x", "y", "c"))
    gfull = {n: [None] * DEPTH for (n, _, _) in BIG}
    gsm = {n: [None] * DEPTH for (n, _) in SMALL}
    dmod = [[None] * 6 for _ in range(DEPTH)]
    dmod[DEPTH - 1][5] = dg2_last
    grad_x = None
    for l in reversed(range(DEPTH)):
        t = sv[l]
        proj = t['proj']
        dact = _mm(dff, full['w_down'][l], 'nt', f"mm_down_dx{l}")
        gfull['w_down'][l] = _mm(t['act'], dff, 'tn', f"mm_down_dw{l}")
        dg, du = _row_bwd(fn_swiglu, f"swiglu_b{l}", [(t['gu'], 0, D_FF), (t['gu'], 1, D_FF)], [], [(dact, 0, D_FF)], [0, 1], [], tm=128)
        dgu = jnp.concatenate([dg, du], axis=1)
        dh2 = _mm(dgu, full['w_gate_up'][l], 'nt', f"mm_gu_dx{l}")
        gfull['w_gate_up'][l] = _mm(t['h2'], dgu, 'tn', f"mm_gu_dw{l}")
        dxc, dmo, dmod[l][2], gsm['norm2_w'][l], dmod[l][4], dmod[l][3] = _row_bwd(
            fn_resid_norm, f"resid_mid_b{l}", [(t['xcur'], 0, D_MODEL), (t['mo'], 0, D_MODEL)],
            [mvec(l, 2), small('norm2_w', l), mvec(l, 4), mvec(l, 3)], [(dx1, 0, D_MODEL), (dh2, 0, D_MODEL)], [0, 1], [0, 1, 2, 3])
        dmix = _mm(dmo, full['w_out'][l], 'nt', f"mm_out_dx{l}")
        gfull['w_out'][l] = _mm(t['mix'], dmo, 'tn', f"mm_out_dw{l}")
        dy, dz, gsm['ssd_norm_w'][l] = _row_bwd(fn_gated_norm, f"gated_b{l}", [(t['y'], 0, 512), (proj, 1, 512)],
                                                [small('ssd_norm_w', l)], [(dmix, 1, 512)], [0, 1], [0])
        dxact, ddt, dsv = _ssd_bwd(t['xact'], proj, ssd_vec(l), t['states'], dy, f"ssd_b{l}")
        gsm['dt_bias'][l], gsm['a_log'][l], gsm['d_skip'][l] = dsv[0, :8], dsv[1, :8], dsv[2, :8]
        dxbc, dcv = _conv_bwd(proj, conv_vec(l), dxact, f"conv_b{l}")
        gfull['conv_w'][l] = dcv[:CONV_TAPS]
        gsm['conv_b'][l] = dcv[CONV_TAPS]
        delta = _row_fwd(fn_attn_delta, f"attn_delta{l}", [(dmix, 0, 512), (t['ao'], 0, 512)], [], [(512, F32)])[0]
        lse_r = (t['lse'][:, ::64] * LOG2E).T.reshape(HEADS // 2, 2, S)
        delta_r = delta[:, ::64].T.reshape(HEADS // 2, 2, S)
        dqT, dkf, dvv = _attn_bwd(t['qf'], t['kf'], t['kf'].T, t['vv'], dmix, lse_r, delta_r, f"attn_b{l}")
        dqf = dqT.T
        dq, dkv, dkpe, dwq, dwk = _row_bwd(
            fn_qk_prep, f"qk_prep_b{l}",
            [(t['q'], 0, 1024), (t['kv'], 0, 1024), (proj, 3, 128), (tab_c, 0, 128), (tab_s1, 0, 128), (tab_s2, 0, 128)],
            [wq_vec(l), wk_vec(l)], [(dqf, 0, 1024), (dkf, 0, 1024), (dvv, 0, 1024)], [0, 1, 2], [0, 1])
        gsm['q_nope_norm_w'][l], gsm['q_pe_norm_w'][l] = dwq[0, :NOPE], dwq[0, NOPE:QK_DIM]
        gsm['k_nope_norm_w'][l], gsm['k_pe_norm_w'][l] = dwk[0, :NOPE], dwk[0, NOPE:QK_DIM]
        dqa_n = _mm(dq, wq_p[l], 'nt', f"mm_q_dx{l}")
        gfull['w_q_up'][l] = _unpad_wq(_mm(t['qa_n'], dq, 'tn', f"mm_q_dw{l}"))
        dkva_n = _mm(dkv, full['w_kv_up'][l], 'nt', f"mm_kv_dx{l}")
        gfull['w_kv_up'][l] = _mm(t['kva_n'], dkv, 'tn', f"mm_kv_dw{l}")
        dqa, dkva, dqw, dkvw = _row_bwd(fn_lat_norm, f"lat_norm_b{l}", [(proj, 0, 256), (proj, 2, 128)],
                                        [small('q_a_norm_w', l), small('kv_a_norm_w', l)],
                                        [(dqa_n, 0, 256), (dkva_n, 0, 128)], [0, 1], [0, 1])
        gsm['q_a_norm_w'][l], gsm['kv_a_norm_w'][l] = dqw[0], dkvw[0]
        dproj = jnp.concatenate([dqa, dkva, dkpe, dz, dxbc, ddt], axis=1)
        dh1 = _mm(dproj, win_p[l], 'nt', f"mm_in_dx{l}")
        gfull['w_in'][l] = _unpad_win(_mm(t['h1'], dproj, 'tn', f"mm_in_dw{l}"))
        if l > 0:
            p = sv[l - 1]
            dx1, dff, dmod[l - 1][5], gsm['norm1_w'][l], dmod[l][1], dmod[l][0] = _row_bwd(
                fn_resid_norm, f"resid_end_b{l - 1}", [(p['x1'], 0, D_MODEL), (p['ff'], 0, D_MODEL)],
                [mvec(l - 1, 5), small('norm1_w', l), mvec(l, 1), mvec(l, 0)], [(dxc, 0, D_MODEL), (dh1, 0, D_MODEL)],
                [0, 1], [0, 1, 2, 3])
        else:
            grad_x, gsm['norm1_w'][l], dmod[l][1], dmod[l][0] = _row_bwd(
                fn_norm_mod_pass, "norm_mod_b", [(x0, 0, D_MODEL)], [small('norm1_w', 0), mvec(0, 1), mvec(0, 0)],
                [(dxc, 0, D_MODEL), (dh1, 0, D_MODEL)], [0], [0, 1, 2])
        for n in ('norm1_w', 'norm2_w', 'ssd_norm_w'):
            gsm[n][l] = gsm[n][l][0]

    for l in range(DEPTH):
        gsm['b_ada'][l] = jnp.concatenate([d[0] for d in dmod[l]])
    sm_part = _pack_small({n: jnp.stack(v) for n, v in gsm.items()})
    sm_all = _ag8(sm_part, "ag_small")
    g_sm, d_sm, m_sm, v_sm = _adam(_pack_small(W), _pack_small(M), _pack_small(V),
                                   [(sm_all, d * SMALL_ROWS) for d in range(8)], "adam_small")
    out_small = [_unpack_small(b) for b in (g_sm, d_sm, m_sm, v_sm)]

    dmod_all = sm_all.reshape(8, SMALL_ROWS * LANES)
    per_layer = sum(k + (-k % LANES) for (_, k) in SMALL)
    dmod_sh = jnp.stack([lax.dynamic_slice_in_dim(dmod_all[:, l * per_layer:l * per_layer + 6 * D_MODEL], chip * 1536, 1536, axis=1)
                         for l in range(DEPTH)])
    g_ada = _ada_bwd(c_all.T, dmod_sh, "ada_bwd")
    ada = _adam(W['w_ada'].reshape(DEPTH * D_MODEL, 1536), M['w_ada'].reshape(DEPTH * D_MODEL, 1536),
                V['w_ada'].reshape(DEPTH * D_MODEL, 1536), [(g_ada.reshape(DEPTH * D_MODEL, 1536), 0)], "adam_ada")
    out_ada = [a.reshape(DEPTH, D_MODEL, 1536) for a in ada]

    gp = _pack_big_full(gfull).reshape(4, 2, HALF_ROWS, 1024)
    sib = _rs_sibling(gp, "rs_sibling")
    mine = lax.dynamic_index_in_dim(gp, ci, axis=1, keepdims=False)
    chipsum = _sum_parts([(mine.reshape(4 * HALF_ROWS, 1024), 0), (sib.reshape(4 * HALF_ROWS, 1024), 0)],
                         4 * HALF_ROWS, 1024, "sum_chip").reshape(4, HALF_ROWS, 1024)
    landed = _rs_chips(chipsum, "rs_chips")
    own = lax.dynamic_index_in_dim(chipsum, chip, axis=0, keepdims=False)
    ghalf = _sum_parts([(own, 0), (landed[0], 0), (landed[1], 0), (landed[2], 0)], HALF_ROWS, 1024, "sum_all")
    gshard = _swap_halves(ghalf, "swap_halves").reshape(BIG_ROWS, 1024)
    big = _adam(_pack_big_shard(W), _pack_big_shard(M), _pack_big_shard(V), [(gshard, 0)], "adam_big")
    out_big = [_unpack_big_shard(b) for b in big]

    outs = [loss, grad_x[None]]
    big_names = {n for (n, _, _) in BIG}
    for k in range(4):
        for n in WEIGHTS:
            if n == 'w_ada':
                outs.append(out_ada[k])
            elif n in big_names:
                outs.append(out_big[k][n])
            else:
                outs.append(out_small[k][n])
    return tuple(outs)
```

```python
import functools

import jax
import jax.numpy as jnp
from jax import lax
from jax.experimental import pallas as pl
from jax.experimental.pallas import tpu as pltpu

F32 = jnp.float32
BF16 = jnp.bfloat16
MESH = pl.DeviceIdType.MESH

D_MODEL = 1024
DEPTH = 2
HEADS = 8
NOPE = 64
ROPE = 32
QK_DIM = NOPE + ROPE
Q_LORA = 256
KV_LORA = 128
SSD_HEADS = 8
SSD_P = 64
SSD_N = 128
CHUNK = 128
CONV_TAPS = 4
D_SSD = 512
D_CONV = 1024
D_FF = 2816
D_IN = 1960
D_IN_PAD = 2176
EPS = 1e-6
ROPE_THETA = 10000.0
ATT_SCALE = QK_DIM ** -0.5
NEG = -1e30
LANES = 128
VMEM_LIMIT = 48 * 1024 * 1024

ADAM_LR, ADAM_B1, ADAM_B2, ADAM_EPS, ADAM_WD, ADAM_STEP = 0.001, 0.9, 0.999, 1e-08, 0.01, 10

WEIGHTS = ['norm1_w', 'norm2_w', 'w_ada', 'b_ada', 'w_in', 'q_a_norm_w', 'w_q_up', 'kv_a_norm_w', 'w_kv_up',
           'q_nope_norm_w', 'q_pe_norm_w', 'k_nope_norm_w', 'k_pe_norm_w', 'conv_w', 'conv_b', 'dt_bias', 'a_log',
           'd_skip', 'ssd_norm_w', 'w_out', 'w_gate_up', 'w_down']
BIG = ['w_down', 'w_gate_up', 'w_out', 'w_kv_up', 'w_q_up', 'w_in']
EARLY = BIG[:2]
SMALL = [('b_ada', 6144), ('conv_w', 4096), ('norm1_w', 1024), ('norm2_w', 1024), ('conv_b', 1024), ('ssd_norm_w', 512),
         ('q_a_norm_w', 256), ('kv_a_norm_w', 128), ('q_nope_norm_w', 64), ('q_pe_norm_w', 32),
         ('k_nope_norm_w', 64), ('k_pe_norm_w', 32), ('dt_bias', 8), ('a_log', 8), ('d_skip', 8)]
SMALL_ROWS = 240


def _cp(sem=None, **kw):
    return pltpu.CompilerParams(dimension_semantics=sem, vmem_limit_bytes=VMEM_LIMIT, **kw)


def _dot(a, b, dims, prec=None):
    return lax.dot_general(a, b, (dims, ((), ())), preferred_element_type=F32, precision=prec)


def _tile(dim, target):
    best = 0
    for t in range(LANES, min(dim, target) + 1, LANES):
        if dim % t == 0:
            best = t
    if best < 256 and dim <= 2304:
        return dim
    return best


def _mm(a, b, mode, name, out_dtype=F32, stack=None):
    ns = None
    if stack == 'b':
        ns = b.shape[2]
        if mode == 'nn':
            (M, K), N = a.shape, 4 * ns
        else:
            (M, K), N = a.shape, b.shape[1]
    elif mode == 'nn':
        (M, K), (_, N) = a.shape, b.shape
    elif mode == 'nt':
        (M, K), (N, _) = a.shape, b.shape
    else:
        (K, M), (_, N) = a.shape, b.shape
    if stack == 'out':
        ns = N // 4
    tm, tn, tk = _tile(M, 512), _tile(N, 1408), _tile(K, 1408)
    if stack == 'b' and mode == 'nt':
        tk = ns
    elif stack is not None:
        tn = ns
    nk = K // tk
    dims = {'nn': ((1,), (0,)), 'nt': ((1,), (1,)), 'tn': ((0,), (0,))}[mode]

    def body(a_ref, b_ref, o_ref, acc):
        k = pl.program_id(2)

        @pl.when(k == 0)
        def _():
            acc[...] = jnp.zeros_like(acc)

        acc[...] += _dot(a_ref[...].astype(BF16), b_ref[...].astype(BF16), dims)

        @pl.when(k == nk - 1)
        def _():
            o_ref[...] = acc[...].astype(o_ref.dtype)

    a_spec = pl.BlockSpec((tk, tm), lambda i, j, k: (k, i)) if mode == 'tn' else pl.BlockSpec((tm, tk), lambda i, j, k: (i, k))
    b_spec = pl.BlockSpec((tn, tk), lambda i, j, k: (j, k)) if mode == 'nt' else pl.BlockSpec((tk, tn), lambda i, j, k: (k, j))
    o_spec, o_shape = pl.BlockSpec((tm, tn), lambda i, j, k: (i, j)), (M, N)
    if stack == 'b':
        b_spec = (pl.BlockSpec((None, tn, ns), lambda i, j, k: (k, j, 0)) if mode == 'nt'
                  else pl.BlockSpec((None, tk, ns), lambda i, j, k: (j, k, 0)))
    if stack == 'out':
        o_spec, o_shape = pl.BlockSpec((None, tm, ns), lambda i, j, k: (j, i, 0)), (4, M, ns)
    return pl.pallas_call(
        body, name=name, grid=(M // tm, N // tn, nk),
        in_specs=[a_spec, b_spec], out_specs=o_spec,
        out_shape=jax.ShapeDtypeStruct(o_shape, out_dtype),
        scratch_shapes=[pltpu.VMEM((tm, tn), F32)],
        compiler_params=_cp(("parallel", "parallel", "arbitrary")),
    )(a, b)


def _rspec(tm, w, cb):
    return pl.BlockSpec((tm, w), lambda i: (i, cb))


def _vspec(shape):
    return pl.BlockSpec(shape, lambda i: (0,) * len(shape))


def _row_fwd(fn, name, rows, vecs, outs, tm=256):
    S = rows[0][0].shape[0]
    tm = min(tm, S)
    nin = len(rows) + len(vecs)

    def body(*refs):
        res = fn(*[r[...] for r in refs[:nin]])
        for o_ref, r in zip(refs[nin:], res):
            o_ref[...] = r.astype(o_ref.dtype)

    return pl.pallas_call(
        body, name=name, grid=(S // tm,),
        in_specs=[_rspec(tm, w, cb) for (_, cb, w) in rows] + [_vspec(v.shape) for v in vecs],
        out_specs=[_rspec(tm, w, 0) for (w, _) in outs],
        out_shape=[jax.ShapeDtypeStruct((S, w), dt) for (w, dt) in outs],
        compiler_params=_cp(("parallel",)),
    )(*[r[0] for r in rows], *vecs)


def _row_bwd(fn, name, rows, vecs, cts, drows, dvecs, tm=256):
    S = rows[0][0].shape[0]
    tm = min(tm, S)
    nr, nv, nc = len(rows), len(vecs), len(cts)
    didx = list(drows) + [nr + j for j in dvecs]

    def body(*refs):
        vals = [r[...] for r in refs[:nr + nv]]
        ct = tuple(r[...].astype(F32) for r in refs[nr + nv:nr + nv + nc])
        outs = refs[nr + nv + nc:]

        def g(*d):
            a = list(vals)
            for k, val in zip(didx, d):
                a[k] = val
            return tuple(fn(*a))

        _, vjp = jax.vjp(g, *[vals[k] for k in didx])
        grads = vjp(ct)
        for o, gr in zip(outs[:len(drows)], grads[:len(drows)]):
            o[...] = gr.astype(o.dtype)

        @pl.when(pl.program_id(0) == 0)
        def _():
            for o in outs[len(drows):]:
                o[...] = jnp.zeros_like(o)

        for o, gr in zip(outs[len(drows):], grads[len(drows):]):
            o[...] += gr

    return pl.pallas_call(
        body, name=name, grid=(S // tm,),
        in_specs=[_rspec(tm, w, cb) for (_, cb, w) in rows] + [_vspec(v.shape) for v in vecs]
        + [_rspec(tm, w, cb) for (_, cb, w) in cts],
        out_specs=[_rspec(tm, rows[k][2], 0) for k in drows] + [_vspec(vecs[j].shape) for j in dvecs],
        out_shape=[jax.ShapeDtypeStruct((S, rows[k][2]), F32) for k in drows]
        + [jax.ShapeDtypeStruct(vecs[j].shape, F32) for j in dvecs],
        compiler_params=_cp(("arbitrary",)),
    )(*[r[0] for r in rows], *vecs, *[c[0] for c in cts])


def _rms(x):
    return x * lax.rsqrt(jnp.mean(x * x, axis=-1, keepdims=True) + EPS)


def fn_norm_mod(x, nw, sc, sh):
    return (_rms(x) * nw * (1.0 + sc) + sh,)


def fn_norm_mod_pass(x, nw, sc, sh):
    return (x, _rms(x) * nw * (1.0 + sc) + sh)


def fn_resid_norm(x, d, g, nw, sc, sh):
    xn = x + g * d
    return (xn, _rms(xn) * nw * (1.0 + sc) + sh)


def fn_lat_norm(qa, kva, qw, kvw):
    return (_rms(qa) * qw, _rms(kva) * kvw)


@functools.partial(jax.custom_vjp, nondiff_argnums=(1,))
def _lroll(x, s):
    return pltpu.roll(x, s, 1)


def _lroll_fwd(x, s):
    return pltpu.roll(x, s, 1), None


def _lroll_bwd(s, _, g):
    return (pltpu.roll(g, (LANES - s) % LANES, 1),)


_lroll.defvjp(_lroll_fwd, _lroll_bwd)


def _lane_masks(shape):
    lane = lax.broadcasted_iota(jnp.int32, shape, 1)
    return (lane < NOPE).astype(F32), ((lane >= NOPE) & (lane < QK_DIM)).astype(F32)


def _rope(t, tc, ts1, ts2):
    return t * tc + _lroll(t, 16) * ts1 + _lroll(t, LANES - 16) * ts2


def fn_qk_prep(q, kv, kpe, tc, ts1, ts2, wq, wk):
    mn, mp = _lane_masks((1, LANES))
    mhi = 1.0 - mn

    def head_norm(t, w):
        rn = lax.rsqrt(jnp.sum(t * t * mn, axis=-1, keepdims=True) * (1.0 / NOPE) + EPS)
        rp = lax.rsqrt(jnp.sum(t * t * mp, axis=-1, keepdims=True) * (1.0 / ROPE) + EPS)
        return t * (rn * mn + rp * mp) * w

    kp = _rope(head_norm(_lroll(kpe, NOPE), wk) * mp, tc, ts1, ts2)
    qs, ks, vs = [], [], []
    for h in range(HEADS):
        qs.append(_rope(head_norm(q[:, h * LANES:(h + 1) * LANES], wq), tc, ts1, ts2))
        t = kv[:, h * LANES:(h + 1) * LANES]
        ks.append(head_norm(t, wk) * mn + kp)
        vs.append(_lroll(t, NOPE) * mn + mhi)
    return (jnp.concatenate(qs, axis=1), jnp.concatenate(ks, axis=1), jnp.concatenate(vs, axis=1))


def fn_attn_delta(do, o):
    mn, _ = _lane_masks((1, LANES))
    mhi = 1.0 - mn
    out = []
    for hp in range(HEADS // 2):
        y = do[:, hp * LANES:(hp + 1) * LANES] * o[:, hp * LANES:(hp + 1) * LANES]
        out.append(jnp.sum(y * mn, axis=-1, keepdims=True) * mn + jnp.sum(y * mhi, axis=-1, keepdims=True) * mhi)
    return (jnp.concatenate(out, axis=1),)


def fn_gated_norm(y, z, w):
    u = y * jax.nn.silu(z)
    half = D_SSD // 2
    return (jnp.concatenate([_rms(u[:, :half]), _rms(u[:, half:])], axis=1) * w,)


def fn_gated_mix(y, z, ao, w):
    return (jnp.concatenate([ao, fn_gated_norm(y, z, w)[0]], axis=1),)


def fn_swiglu(gu):
    return (jax.nn.silu(gu[:, :D_FF]) * gu[:, D_FF:],)


def _final(x1, ff, tgt, g2, name):
    S = x1.shape[0]
    tm = min(256, S)

    def body(x_ref, f_ref, t_ref, g_ref, dx_ref, df_ref, dg_ref, l_ref):
        @pl.when(pl.program_id(0) == 0)
        def _():
            dg_ref[...] = jnp.zeros_like(dg_ref)
            l_ref[...] = jnp.zeros_like(l_ref)

        f = f_ref[...]
        g = g_ref[...]
        e = x_ref[...] + g * f - t_ref[...]
        dx = e * (1.0 / D_MODEL)
        dx_ref[...] = dx
        df_ref[...] = g * dx
        dg_ref[...] += jnp.sum(dx * f, axis=0, keepdims=True)
        l_ref[...] += jnp.sum(e * e) * (0.5 / D_MODEL)

    r = _rspec(tm, D_MODEL, 0)
    return pl.pallas_call(
        body, name=name, grid=(S // tm,),
        in_specs=[r, r, r, _vspec((1, D_MODEL))],
        out_specs=[r, r, _vspec((1, D_MODEL)), _vspec((1, LANES))],
        out_shape=[jax.ShapeDtypeStruct((S, D_MODEL), F32), jax.ShapeDtypeStruct((S, D_MODEL), F32),
                   jax.ShapeDtypeStruct((1, D_MODEL), F32), jax.ShapeDtypeStruct((1, LANES), F32)],
        compiler_params=_cp(("arbitrary",)),
    )(x1, ff, tgt, g2)


def _causal_mask(t):
    r = lax.broadcasted_iota(jnp.int32, (t, t), 0)
    c = lax.broadcasted_iota(jnp.int32, (t, t), 1)
    return c <= r


LOG2E = 1.4426950408889634
EXP2_SCALE = ATT_SCALE * LOG2E
ATT_TQ, ATT_TK = 512, 1024
ATT_BQ, ATT_BK = 1024, 512


def _attn_fwd(qf, kf, va, name):
    S = qf.shape[0]
    T, TK = min(ATT_TQ, S), min(ATT_TK, S)
    nmask = max(1, T // TK)

    def body(q_ref, k_ref, v_ref, o_ref, l_ref):
        i = pl.program_id(1)
        r = lax.broadcasted_iota(jnp.int32, (T, TK), 0)
        c = lax.broadcasted_iota(jnp.int32, (T, TK), 1)
        qs = [q_ref[:, hh * LANES:(hh + 1) * LANES] for hh in range(2)]

        def blk(j, carry, masked):
            off = pl.multiple_of(j * TK, TK)
            out = []
            for hh in range(2):
                m, acc = carry[hh]
                s = _dot(qs[hh], k_ref[pl.ds(off, TK), hh * LANES:(hh + 1) * LANES], ((1,), (1,)))
                if masked:
                    s = jnp.where(c + j * TK <= r + i * T, s, NEG)
                mn = jnp.maximum(m, jnp.max(s, axis=1, keepdims=True))
                p = jnp.exp2((s - mn) * EXP2_SCALE)
                al = jnp.exp2((m - mn) * EXP2_SCALE)
                vj = v_ref[pl.ds(off, TK), hh * LANES:(hh + 1) * LANES]
                out.append((mn, al * acc + _dot(p.astype(BF16), vj, ((1,), (0,)))))
            return tuple(out)

        one = (jnp.full((T, 1), NEG, F32), jnp.zeros((T, LANES), F32))
        nfull = lax.div(i * T, TK)
        carry = lax.fori_loop(0, nfull, lambda j, cr: blk(j, cr, False), (one, one))
        for t in range(nmask):
            carry = blk(nfull + t, carry, True)
        for hh in range(2):
            m, acc = carry[hh]
            l = acc[:, 64:65]
            o_ref[:, hh * 64:(hh + 1) * 64] = (acc / l)[:, :64]
            l_ref[:, hh * 64:(hh + 1) * 64] = jnp.broadcast_to(m * ATT_SCALE + jnp.log(l), (T, 64))

    return pl.pallas_call(
        body, name=name, grid=(HEADS // 2, S // T),
        in_specs=[pl.BlockSpec((T, 256), lambda h, i: (i, h)), pl.BlockSpec((S, 256), lambda h, i: (0, h)),
                  pl.BlockSpec((S, 256), lambda h, i: (0, h))],
        out_specs=[pl.BlockSpec((T, LANES), lambda h, i: (i, h)), pl.BlockSpec((T, LANES), lambda h, i: (i, h))],
        out_shape=[jax.ShapeDtypeStruct((S, D_SSD), F32), jax.ShapeDtypeStruct((S, D_SSD), F32)],
        compiler_params=_cp(("parallel", "parallel")),
    )(qf, kf, va)


def _attn_bwd(qf, kf, kT, va, do, lse_r, delta_r, name):
    S = qf.shape[0]
    T, TK = min(ATT_BQ, S), min(ATT_BK, S)
    nq = S // T
    nmask = max(1, TK // T)

    def body(q_ref, k_ref, kT_ref, v_ref, do_ref, l_ref, d_ref, dqT_ref, dk_ref, dv_ref):
        j = pl.program_id(1)

        @pl.when(j == 0)
        def _():
            dqT_ref[...] = jnp.zeros_like(dqT_ref)

        r = lax.broadcasted_iota(jnp.int32, (TK, T), 0)
        c = lax.broadcasted_iota(jnp.int32, (TK, T), 1)
        lo = (lax.broadcasted_iota(jnp.int32, (1, LANES), 1) < 64).astype(F32)
        ks = [k_ref[:, hh * LANES:(hh + 1) * LANES] for hh in range(2)]
        vs = [v_ref[:, hh * LANES:(hh + 1) * LANES] for hh in range(2)]
        kTs = [kT_ref[hh * LANES:(hh + 1) * LANES, :] for hh in range(2)]

        def blk(i, carry, masked):
            off = pl.multiple_of(i * T, T)
            dall = do_ref[pl.ds(off, T), :]
            out = []
            for hh in range(2):
                dk, dv = carry[hh]
                q = q_ref[pl.ds(off, T), hh * LANES:(hh + 1) * LANES]
                dop = ((dall if hh == 0 else pltpu.roll(dall, 64, 1)) * lo).astype(BF16)
                lrow = l_ref[0, hh:hh + 1, pl.ds(off, T)]
                drow = d_ref[0, hh:hh + 1, pl.ds(off, T)]
                pT = jnp.exp2(_dot(ks[hh], q, ((1,), (1,))) * EXP2_SCALE - lrow)
                if masked:
                    pT = jnp.where(r + j * TK <= c + i * T, pT, 0.0)
                dpT = _dot(vs[hh], dop, ((1,), (1,)))
                dsT = (pT * (dpT - drow) * ATT_SCALE).astype(BF16)
                dv = dv + _dot(pT.astype(BF16), dop, ((1,), (0,)))
                dk = dk + _dot(dsT, q, ((1,), (0,)))
                dqT_ref[hh * LANES:(hh + 1) * LANES, pl.ds(off, T)] += _dot(kTs[hh], dsT, ((1,), (0,)))
                out.append((dk, dv))
            return tuple(out)

        z = (jnp.zeros((TK, LANES), F32), jnp.zeros((TK, LANES), F32))
        first = lax.div(j * TK, T)
        carry = (z, z)
        for t in range(nmask):
            carry = blk(first + t, carry, True)
        carry = lax.fori_loop(first + nmask, nq, lambda i, cr: blk(i, cr, False), carry)
        for hh in range(2):
            dk_ref[:, hh * LANES:(hh + 1) * LANES] = carry[hh][0]
            dv_ref[:, hh * LANES:(hh + 1) * LANES] = carry[hh][1]

    return pl.pallas_call(
        body, name=name, grid=(HEADS // 2, S // TK),
        in_specs=[pl.BlockSpec((S, 256), lambda h, j: (0, h)), pl.BlockSpec((TK, 256), lambda h, j: (j, h)),
                  pl.BlockSpec((256, TK), lambda h, j: (h, j)), pl.BlockSpec((TK, 256), lambda h, j: (j, h)),
                  pl.BlockSpec((S, LANES), lambda h, j: (0, h)), pl.BlockSpec((1, 2, S), lambda h, j: (h, 0, 0)),
                  pl.BlockSpec((1, 2, S), lambda h, j: (h, 0, 0))],
        out_specs=[pl.BlockSpec((256, S), lambda h, j: (h, 0)), pl.BlockSpec((TK, 256), lambda h, j: (j, h)),
                   pl.BlockSpec((TK, 256), lambda h, j: (j, h))],
        out_shape=[jax.ShapeDtypeStruct((D_MODEL, S), F32), jax.ShapeDtypeStruct((S, D_MODEL), F32),
                   jax.ShapeDtypeStruct((S, D_MODEL), F32)],
        compiler_params=_cp(("parallel", "arbitrary")),
    )(qf, kf, kT, va, do, lse_r, delta_r)


def _shift_down(x, s):
    if s == 0:
        return x
    rows = lax.broadcasted_iota(jnp.int32, x.shape, 0)
    return jnp.where(rows >= s, pltpu.roll(x, s, 0), 0.0)


def _shift_up(x, s):
    if s == 0:
        return x
    n = x.shape[0]
    rows = lax.broadcasted_iota(jnp.int32, x.shape, 0)
    return jnp.where(rows < n - s, pltpu.roll(x, n - s, 0), 0.0)


def _conv_fwd(proj, cvec, name):
    S = proj.shape[0]

    def body(x_ref, c_ref, o_ref):
        x = x_ref[...]
        y = jnp.broadcast_to(c_ref[4:5, :], x.shape)
        for k in range(CONV_TAPS):
            y = y + c_ref[k:k + 1, :] * _shift_down(x, CONV_TAPS - 1 - k)
        o_ref[...] = y * jax.nn.sigmoid(y)

    return pl.pallas_call(
        body, name=name, grid=(D_CONV // LANES,),
        in_specs=[pl.BlockSpec((S, LANES), lambda j: (0, 8 + j)), pl.BlockSpec((8, LANES), lambda j: (0, j))],
        out_specs=pl.BlockSpec((S, LANES), lambda j: (0, j)),
        out_shape=jax.ShapeDtypeStruct((S, D_CONV), F32),
        compiler_params=_cp(("parallel",)),
    )(proj, cvec)


def _conv_bwd(proj, cvec, dact, name):
    S = proj.shape[0]

    def body(x_ref, c_ref, d_ref, dx_ref, dc_ref):
        x = x_ref[...]
        y = jnp.broadcast_to(c_ref[4:5, :], x.shape)
        for k in range(CONV_TAPS):
            y = y + c_ref[k:k + 1, :] * _shift_down(x, CONV_TAPS - 1 - k)
        sg = jax.nn.sigmoid(y)
        dy = d_ref[...] * (sg * (1.0 + y * (1.0 - sg)))
        dx = jnp.zeros_like(x)
        for k in range(CONV_TAPS):
            s = CONV_TAPS - 1 - k
            dx = dx + c_ref[k:k + 1, :] * _shift_up(dy, s)
            dc_ref[k:k + 1, :] = jnp.sum(dy * _shift_down(x, s), axis=0, keepdims=True)
        dx_ref[...] = dx
        dc_ref[4:5, :] = jnp.sum(dy, axis=0, keepdims=True)
        dc_ref[5:8, :] = jnp.zeros((3, LANES), F32)

    return pl.pallas_call(
        body, name=name, grid=(D_CONV // LANES,),
        in_specs=[pl.BlockSpec((S, LANES), lambda j: (0, 8 + j)), pl.BlockSpec((8, LANES), lambda j: (0, j)),
                  pl.BlockSpec((S, LANES), lambda j: (0, j))],
        out_specs=[pl.BlockSpec((S, LANES), lambda j: (0, j)), pl.BlockSpec((8, LANES), lambda j: (0, j))],
        out_shape=[jax.ShapeDtypeStruct((S, D_CONV), F32), jax.ShapeDtypeStruct((8, D_CONV), F32)],
        compiler_params=_cp(("parallel",)),
    )(proj, cvec, dact)


def fn_ssd_chunk(xs, bm, cm, dtr, state, vecs):
    Q = CHUNK
    dt = jax.nn.softplus(dtr + vecs[0:1])
    a = -jnp.exp(vecs[1:2])
    d_skip = vecs[2:3]
    adt = dt * a
    tril = _causal_mask(Q)
    acs = _dot(tril.astype(F32), adt, ((1,), (0,)), lax.Precision.HIGHEST)
    acs_t = acs.T
    ys, new_states = [], []
    Bs = [bm[:, g * SSD_N:(g + 1) * SSD_N].astype(BF16) for g in range(2)]
    Cs = [cm[:, g * SSD_N:(g + 1) * SSD_N].astype(BF16) for g in range(2)]
    Gs = [_dot(Cs[g], Bs[g], ((1,), (1,))) for g in range(2)]
    for h in range(SSD_HEADS):
        g = h // (SSD_HEADS // 2)
        x = xs[:, h * SSD_P:(h + 1) * SSD_P]
        B, C = Bs[g], Cs[g]
        xdt = x * dt[:, h:h + 1]
        A = acs[:, h:h + 1]
        L = jnp.exp(jnp.where(tril, A - acs_t[h:h + 1, :], -jnp.inf))
        M = Gs[g] * L
        yd = _dot(M.astype(BF16), xdt.astype(BF16), ((1,), (0,)))
        st = state[h]
        yo = _dot(C, st.astype(BF16), ((1,), (1,))) * jnp.exp(A)
        alast = acs[Q - 1:Q, h:h + 1]
        U = xdt * jnp.exp(alast - A)
        new_states.append(jnp.exp(alast) * st + _dot(U.astype(BF16), B, ((0,), (0,))))
        ys.append(yd + yo + d_skip[:, h:h + 1] * x)
    return jnp.concatenate(ys, axis=1), jnp.stack(new_states)


def _ssd_fwd(xact, proj, svec, name):
    S = xact.shape[0]
    nc = S // CHUNK

    def body(x_ref, dt_ref, v_ref, y_ref, st_ref, state):
        @pl.when(pl.program_id(0) == 0)
        def _():
            state[...] = jnp.zeros_like(state)

        st_ref[0] = state[...]
        x = x_ref[...]
        y, sn = fn_ssd_chunk(x[:, 0:512], x[:, 512:768], x[:, 768:1024], dt_ref[...], state[...], v_ref[...])
        y_ref[...] = y
        state[...] = sn

    return pl.pallas_call(
        body, name=name, grid=(nc,),
        in_specs=[pl.BlockSpec((CHUNK, D_CONV), lambda i: (i, 0)), pl.BlockSpec((CHUNK, LANES), lambda i: (i, 16)),
                  pl.BlockSpec((8, LANES), lambda i: (0, 0))],
        out_specs=[pl.BlockSpec((CHUNK, D_SSD), lambda i: (i, 0)),
                   pl.BlockSpec((1, SSD_HEADS, SSD_P, SSD_N), lambda i: (i, 0, 0, 0))],
        out_shape=[jax.ShapeDtypeStruct((S, D_SSD), F32), jax.ShapeDtypeStruct((nc, SSD_HEADS, SSD_P, SSD_N), F32)],
        scratch_shapes=[pltpu.VMEM((SSD_HEADS, SSD_P, SSD_N), F32)],
        compiler_params=_cp(("arbitrary",)),
    )(xact, proj, svec)


def _ssd_bwd(xact, proj, svec, states, dy, name):
    S = xact.shape[0]
    nc = S // CHUNK

    def body(x_ref, dt_ref, v_ref, st_ref, dy_ref, dx_ref, ddt_ref, dv_ref, dstate):
        @pl.when(pl.program_id(0) == 0)
        def _():
            dstate[...] = jnp.zeros_like(dstate)
            dv_ref[...] = jnp.zeros_like(dv_ref)

        x = x_ref[...]
        _, vjp = jax.vjp(fn_ssd_chunk, x[:, 0:512], x[:, 512:768], x[:, 768:1024], dt_ref[...], st_ref[0], v_ref[...])
        dxs, dbm, dcm, ddt, dst, dvec = vjp((dy_ref[...], dstate[...]))
        dx_ref[:, 0:512] = dxs
        dx_ref[:, 512:768] = dbm
        dx_ref[:, 768:1024] = dcm
        ddt_ref[...] = ddt
        dstate[...] = dst
        dv_ref[...] += dvec

    rev = lambda i: (nc - 1 - i, 0)
    return pl.pallas_call(
        body, name=name, grid=(nc,),
        in_specs=[pl.BlockSpec((CHUNK, D_CONV), rev), pl.BlockSpec((CHUNK, LANES), lambda i: (nc - 1 - i, 16)),
                  pl.BlockSpec((8, LANES), lambda i: (0, 0)),
                  pl.BlockSpec((1, SSD_HEADS, SSD_P, SSD_N), lambda i: (nc - 1 - i, 0, 0, 0)),
                  pl.BlockSpec((CHUNK, D_SSD), rev)],
        out_specs=[pl.BlockSpec((CHUNK, D_CONV), rev), pl.BlockSpec((CHUNK, LANES), rev),
                   pl.BlockSpec((8, LANES), lambda i: (0, 0))],
        out_shape=[jax.ShapeDtypeStruct((S, D_CONV), F32), jax.ShapeDtypeStruct((S, LANES), F32),
                   jax.ShapeDtypeStruct((8, LANES), F32)],
        scratch_shapes=[pltpu.VMEM((SSD_HEADS, SSD_P, SSD_N), F32)],
        compiler_params=_cp(("arbitrary",)),
    )(xact, proj, svec, states, dy)


def _ada_fwd(c_all, w_ada, b_sh, name):
    nb = 1536 // 512

    def body(c_ref, w_ref, b_ref, o_ref):
        ca = jax.nn.silu(c_ref[...]).astype(BF16)
        o_ref[0] = _dot(ca, w_ref[0].astype(BF16), ((1,), (0,))) + b_ref[0]

    return pl.pallas_call(
        body, name=name, grid=(DEPTH, nb),
        in_specs=[pl.BlockSpec((8, D_MODEL), lambda l, j: (0, 0)), pl.BlockSpec((1, D_MODEL, 512), lambda l, j: (l, 0, j)),
                  pl.BlockSpec((1, 1, 512), lambda l, j: (l, 0, j))],
        out_specs=pl.BlockSpec((1, 8, 512), lambda l, j: (l, 0, j)),
        out_shape=jax.ShapeDtypeStruct((DEPTH, 8, 1536), F32),
        compiler_params=_cp(("parallel", "parallel")),
    )(c_all, w_ada, b_sh)


def _ada_bwd(c_all_t, dmod_sh, name):
    nb = 1536 // 512

    def body(c_ref, d_ref, o_ref):
        ca = jax.nn.silu(c_ref[...])
        acc = ca[:, 0:1] * d_ref[0, 0:1, :]
        for b in range(1, 8):
            acc = acc + ca[:, b:b + 1] * d_ref[0, b:b + 1, :]
        o_ref[0] = acc

    return pl.pallas_call(
        body, name=name, grid=(DEPTH, nb),
        in_specs=[pl.BlockSpec((D_MODEL, 8), lambda l, j: (0, 0)), pl.BlockSpec((1, 8, 512), lambda l, j: (l, 0, j))],
        out_specs=pl.BlockSpec((1, D_MODEL, 512), lambda l, j: (l, 0, j)),
        out_shape=jax.ShapeDtypeStruct((DEPTH, D_MODEL, 1536), F32),
        compiler_params=_cp(("parallel", "parallel")),
    )(c_all_t, dmod_sh)


def _rows_tile(rows):
    return next(t for t in (512, 256, 128, 64, 32, 16, 8) if rows % t == 0)


SUM_BLOCKS = 4
ADAM_BLOCKS = 8


def _sum_sibling(gs, ls, ci, name):
    n = len(gs)

    def body(c_ref, *refs):
        for p in range(n):
            refs[2 * n + p][...] = refs[2 * p][...] + refs[2 * p + 1][...]

    in_specs, out_specs, out_shape = [], [], []
    for g in gs:
        _, _, rh, cw = g.shape
        rb = rh // SUM_BLOCKS
        in_specs += [pl.BlockSpec((None, None, rb, cw), lambda s, i, c: (s, c[0], i, 0)),
                     pl.BlockSpec((None, rb, cw), lambda s, i, c: (s, i, 0))]
        out_specs.append(pl.BlockSpec((None, rb, cw), lambda s, i, c: (s, i, 0)))
        out_shape.append(jax.ShapeDtypeStruct((4, rh, cw), F32))
    ops = [a for pair in zip(gs, ls) for a in pair]
    return pl.pallas_call(
        body, name=name,
        grid_spec=pltpu.PrefetchScalarGridSpec(num_scalar_prefetch=1, grid=(4, SUM_BLOCKS), in_specs=in_specs, out_specs=out_specs),
        out_shape=out_shape, compiler_params=_cp(("parallel", "parallel")),
    )(ci.reshape(1).astype(jnp.int32), *ops)


def _sum_chips(cs, lands, chip, name):
    n = len(cs)

    def body(c_ref, *refs):
        for p in range(n):
            a = refs[4 * p:4 * p + 4]
            refs[4 * n + p][...] = ((a[0][...] + a[1][...]) + a[2][...]) + a[3][...]

    in_specs, out_specs, out_shape = [], [], []
    for c in cs:
        _, rh, cw = c.shape
        rb = rh // SUM_BLOCKS
        in_specs.append(pl.BlockSpec((None, rb, cw), lambda i, ch: (ch[0], i, 0)))
        in_specs += [pl.BlockSpec((None, rb, cw), functools.partial(lambda i, ch, k: (k, i, 0), k=k)) for k in range(3)]
        out_specs.append(pl.BlockSpec((rb, cw), lambda i, ch: (i, 0)))
        out_shape.append(jax.ShapeDtypeStruct((rh, cw), F32))
    ops = [a for c, l in zip(cs, lands) for a in (c, l, l, l)]
    return pl.pallas_call(
        body, name=name,
        grid_spec=pltpu.PrefetchScalarGridSpec(num_scalar_prefetch=1, grid=(SUM_BLOCKS,), in_specs=in_specs, out_specs=out_specs),
        out_shape=out_shape, compiler_params=_cp(("parallel",)),
    )(chip.reshape(1).astype(jnp.int32), *ops)


def _adam_update(w, m, v, g):
    c1 = 1.0 / (1.0 - ADAM_B1 ** ADAM_STEP)
    c2 = 1.0 / (1.0 - ADAM_B2 ** ADAM_STEP)
    nm = ADAM_B1 * m + (1.0 - ADAM_B1) * g
    nv = ADAM_B2 * v + (1.0 - ADAM_B2) * (g * g)
    return -ADAM_LR * ((nm * c1) / (jnp.sqrt(nv * c2) + ADAM_EPS) + ADAM_WD * w), nm, nv


def _adam_multi(ws, ms, vs, gs, name):
    n = len(ws)

    def body(*refs):
        for p in range(n):
            d, nm, nv = _adam_update(*[refs[4 * p + k][...] for k in range(4)])
            refs[4 * n + 3 * p][...] = d
            refs[4 * n + 3 * p + 1][...] = nm
            refs[4 * n + 3 * p + 2][...] = nv

    in_specs, out_specs, out_shape = [], [], []
    for w in ws:
        _, r, cw = w.shape
        spec = pl.BlockSpec((None, r // ADAM_BLOCKS, cw), lambda l, i: (l, i, 0))
        in_specs += [spec] * 4
        out_specs += [spec] * 3
        out_shape += [jax.ShapeDtypeStruct(w.shape, F32)] * 3
    ops = [a for q in zip(ws, ms, vs, gs) for a in q]
    res = pl.pallas_call(
        body, name=name, grid=(DEPTH, ADAM_BLOCKS), in_specs=in_specs, out_specs=out_specs, out_shape=out_shape,
        compiler_params=_cp(("parallel", "parallel")),
    )(*ops)
    return res[0::3], res[1::3], res[2::3]


def _adam(w, m, v, parts, name):
    rows, width = w.shape
    bm = min(256, _rows_tile(rows))
    np_ = len(parts)
    c1 = 1.0 / (1.0 - ADAM_B1 ** ADAM_STEP)
    c2 = 1.0 / (1.0 - ADAM_B2 ** ADAM_STEP)

    def body(*refs):
        w_ref, m_ref, v_ref = refs[:3]
        g = refs[3][...]
        for r in refs[4:3 + np_]:
            g = g + r[...]
        g_ref, d_ref, nm_ref, nv_ref = refs[3 + np_:]
        nm = ADAM_B1 * m_ref[...] + (1.0 - ADAM_B1) * g
        nv = ADAM_B2 * v_ref[...] + (1.0 - ADAM_B2) * (g * g)
        g_ref[...] = g
        nm_ref[...] = nm
        nv_ref[...] = nv
        d_ref[...] = -ADAM_LR * ((nm * c1) / (jnp.sqrt(nv * c2) + ADAM_EPS) + ADAM_WD * w_ref[...])

    blk = pl.BlockSpec((bm, width), lambda i: (i, 0))
    return pl.pallas_call(
        body, name=name, grid=(rows // bm,),
        in_specs=[blk, blk, blk] + [pl.BlockSpec((bm, width), functools.partial(lambda i, o: (i + o, 0), o=off // bm))
                                    for (_, off) in parts],
        out_specs=[blk, blk, blk, blk],
        out_shape=[jax.ShapeDtypeStruct((rows, width), F32)] * 4,
        compiler_params=_cp(("parallel",)),
    )(w, m, v, *[p[0] for p in parts])


def _coords():
    return lax.axis_index("x"), lax.axis_index("y"), lax.axis_index("c")


def _other_chips(x, y):
    return [(1 - x, y), (x, 1 - y), (1 - x, 1 - y)]


def _ag8(blk, name):
    m_per, n = blk.shape

    def body(x_ref, out_ref, send_sems, recv_sems, local_sem):
        x, y, c = _coords()
        me, sibling = (x, y, c), (x, y, 1 - c)
        chips = _other_chips(x, y)

        def rows(px, py, pc):
            return out_ref.at[pl.ds((4 * px + 2 * py + pc) * m_per, m_per), :]

        def copy(k, block, to, src=None):
            return pltpu.make_async_remote_copy(
                src_ref=rows(*block) if src is None else src, dst_ref=rows(*block),
                send_sem=send_sems.at[k], recv_sem=recv_sems.at[k], device_id=to, device_id_type=MESH)

        mine = pltpu.make_async_copy(x_ref, rows(*me), local_sem)
        mine.start()
        first = [copy(0, me, sibling, src=x_ref)]
        first += [copy(1 + j, me, (*chip, c), src=x_ref) for j, chip in enumerate(chips)]
        for cp in first:
            cp.start()
        passed = [copy(4 + j, (*chip, c), sibling) for j, chip in enumerate(chips)]
        for j, chip in enumerate(chips):
            copy(1 + j, (*chip, c), me).wait_recv()
            passed[j].start()
        copy(0, sibling, me).wait_recv()
        for j, chip in enumerate(chips):
            copy(4 + j, (*chip, 1 - c), me).wait_recv()
        for cp in first + passed:
            cp.wait_send()
        mine.wait()

    return pl.pallas_call(
        body, name=name,
        out_shape=jax.ShapeDtypeStruct((8 * m_per, n), blk.dtype),
        in_specs=[pl.BlockSpec(memory_space=pltpu.VMEM)], out_specs=pl.BlockSpec(memory_space=pltpu.VMEM),
        scratch_shapes=[pltpu.SemaphoreType.DMA((7,)), pltpu.SemaphoreType.DMA((7,)), pltpu.SemaphoreType.DMA],
    )(blk)


HBM_SPEC = pl.BlockSpec(memory_space=pltpu.HBM)
SEM_SPEC = pl.BlockSpec(memory_space=pltpu.SEMAPHORE)
EFFECT = pltpu.SideEffectType.DATAFLOW_SIDE_EFFECTING


def _remote(src, dst, send_sem, recv_sem, to):
    return pltpu.make_async_remote_copy(src_ref=src, dst_ref=dst, send_sem=send_sem, recv_sem=recv_sem,
                                        device_id=to, device_id_type=MESH)


def _ag_list(shards, name):
    n = len(shards)

    def body(*refs):
        sh, out = refs[:n], refs[n:2 * n]
        send_sems, recv_sems, local_sems = refs[2 * n:]
        x, y, c = _coords()
        sibling = (x, y, 1 - c)
        chips = _other_chips(x, y)
        mine = [pltpu.make_async_copy(sh[p], out[p].at[2 * x + y], local_sems.at[p]) for p in range(n)]
        for cp in mine:
            cp.start()
        first = [_remote(sh[p].at[c], out[p].at[2 * x + y, c], send_sems.at[6 * p + j], recv_sems.at[6 * p + j], (px, py, c))
                 for p in range(n) for j, (px, py) in enumerate(chips)]
        for cp in first:
            cp.start()
        passed = []
        for j, (px, py) in enumerate(chips):
            for p in range(n):
                got = out[p].at[2 * px + py, c]
                _remote(got, got, send_sems.at[6 * p + j], recv_sems.at[6 * p + j], (x, y, c)).wait_recv()
                cp = _remote(got, got, send_sems.at[6 * p + 3 + j], recv_sems.at[6 * p + 3 + j], sibling)
                cp.start()
                passed.append(cp)
        for j, (px, py) in enumerate(chips):
            for p in range(n):
                got = out[p].at[2 * px + py, 1 - c]
                _remote(got, got, send_sems.at[6 * p + 3 + j], recv_sems.at[6 * p + 3 + j], (x, y, c)).wait_recv()
        for cp in first + passed:
            cp.wait_send()
        for cp in mine:
            cp.wait()

    return pl.pallas_call(
        body, name=name,
        out_shape=[jax.ShapeDtypeStruct((4,) + s.shape, s.dtype) for s in shards],
        in_specs=[pl.BlockSpec(memory_space=pl.ANY)] * n, out_specs=[pl.BlockSpec(memory_space=pl.ANY)] * n,
        scratch_shapes=[pltpu.SemaphoreType.DMA((6 * n,)), pltpu.SemaphoreType.DMA((6 * n,)), pltpu.SemaphoreType.DMA((n,))],
    )(*shards)


def _ag_direct_copies(sh, land, send_sems, recv_sems, starting):
    x, y, c = _coords()
    return [_remote(sh[p], land[p].at[2 * x + y] if starting else land[p].at[2 * px + py],
                    send_sems.at[3 * p + j], recv_sems.at[3 * p + j], (px, py, c))
            for p in range(len(sh)) for j, (px, py) in enumerate(_other_chips(x, y))]


def _ag_direct_start(shards, name):
    n = len(shards)

    def body(*refs):
        sh, land = refs[:n], refs[n:2 * n]
        send_sems, recv_sems = refs[2 * n], refs[2 * n + 1]
        token, local_sems = refs[4 * n + 2], refs[4 * n + 3]
        x, y, _ = _coords()
        for cp in _ag_direct_copies(sh, land, send_sems, recv_sems, True):
            cp.start()
        mine = [pltpu.make_async_copy(sh[p], land[p].at[2 * x + y], local_sems.at[p]) for p in range(n)]
        for cp in mine:
            cp.start()
        for cp in mine:
            cp.wait()
        token[...] = jnp.zeros_like(token)

    lands = [pltpu.with_memory_space_constraint(lax.empty((4,) + s.shape, s.dtype), pltpu.HBM) for s in shards]
    res = pl.pallas_call(
        body, name=name,
        out_shape=(pltpu.SemaphoreType.DMA((3 * n,)), pltpu.SemaphoreType.DMA((3 * n,)))
        + tuple(pltpu.HBM(s.shape, s.dtype) for s in shards) + tuple(pltpu.HBM(l.shape, l.dtype) for l in lands)
        + (jax.ShapeDtypeStruct((8, LANES), F32),),
        in_specs=(HBM_SPEC,) * (2 * n), out_specs=(SEM_SPEC, SEM_SPEC) + (HBM_SPEC,) * (2 * n) + (pl.BlockSpec(memory_space=pltpu.VMEM),),
        input_output_aliases={i: 2 + i for i in range(2 * n)},
        scratch_shapes=[pltpu.SemaphoreType.DMA((n,))],
        compiler_params=pltpu.CompilerParams(has_side_effects=EFFECT),
    )(*[pltpu.with_memory_space_constraint(s, pltpu.HBM) for s in shards], *lands)
    return res[0], res[1], res[2:2 + n], res[2 + n:2 + 2 * n], res[2 + 2 * n]


def _ag_direct_wait(send_sems, recv_sems, sh_thru, land_thru, after, name):
    n = len(sh_thru)

    def body(*refs):
        sh, land = refs[:n], refs[n:2 * n]
        for cp in _ag_direct_copies(sh, land, refs[2 * n], refs[2 * n + 1], False):
            cp.wait_send()
            cp.wait_recv()

    res = pl.pallas_call(
        body, name=name,
        out_shape=tuple(pltpu.HBM(s.shape, s.dtype) for s in sh_thru) + tuple(pltpu.HBM(l.shape, l.dtype) for l in land_thru),
        in_specs=(HBM_SPEC,) * (2 * n) + (SEM_SPEC, SEM_SPEC, pl.BlockSpec(memory_space=pl.ANY)),
        out_specs=(HBM_SPEC,) * (2 * n), input_output_aliases={i: i for i in range(2 * n)},
        compiler_params=pltpu.CompilerParams(has_side_effects=EFFECT),
    )(*sh_thru, *land_thru, send_sems, recv_sems, after)
    return res[n:]


def _rs_sibling_list(gs, name):
    n = len(gs)

    def body(*refs):
        g, out, send_sems, recv_sems = refs[:n], refs[n:2 * n], refs[2 * n], refs[2 * n + 1]
        x, y, c = _coords()
        cps = [_remote(g[p].at[s, 1 - c], out[p].at[s], send_sems.at[4 * p + s], recv_sems.at[4 * p + s], (x, y, 1 - c))
               for p in range(n) for s in range(4)]
        for cp in cps:
            cp.start()
        for cp in cps:
            cp.wait_recv()
        for cp in cps:
            cp.wait_send()

    return pl.pallas_call(
        body, name=name,
        out_shape=[jax.ShapeDtypeStruct((4,) + g.shape[2:], g.dtype) for g in gs],
        in_specs=[pl.BlockSpec(memory_space=pl.ANY)] * n, out_specs=[pl.BlockSpec(memory_space=pl.ANY)] * n,
        scratch_shapes=[pltpu.SemaphoreType.DMA((4 * n,)), pltpu.SemaphoreType.DMA((4 * n,))],
    )(*gs)


def _rs_chips_copies(cs, land, send_sems, recv_sems):
    x, y, c = _coords()
    return [_remote(cs[p].at[2 * px + py], land[p].at[j], send_sems.at[3 * p + j], recv_sems.at[3 * p + j], (px, py, c))
            for p in range(len(cs)) for j, (px, py) in enumerate(_other_chips(x, y))]


def _rs_chips_start(cs, name):
    n = len(cs)

    def body(*refs):
        for cp in _rs_chips_copies(refs[:n], refs[n:2 * n], refs[2 * n], refs[2 * n + 1]):
            cp.start()
        token = refs[4 * n + 2]
        token[...] = jnp.zeros_like(token)

    lands = [pltpu.with_memory_space_constraint(lax.empty((3,) + c.shape[1:], c.dtype), pltpu.HBM) for c in cs]
    res = pl.pallas_call(
        body, name=name,
        out_shape=(pltpu.SemaphoreType.DMA((3 * n,)), pltpu.SemaphoreType.DMA((3 * n,)))
        + tuple(pltpu.HBM(c.shape, c.dtype) for c in cs) + tuple(pltpu.HBM(l.shape, l.dtype) for l in lands)
        + (jax.ShapeDtypeStruct((8, LANES), F32),),
        in_specs=(HBM_SPEC,) * (2 * n), out_specs=(SEM_SPEC, SEM_SPEC) + (HBM_SPEC,) * (2 * n) + (pl.BlockSpec(memory_space=pltpu.VMEM),),
        input_output_aliases={i: 2 + i for i in range(2 * n)},
        compiler_params=pltpu.CompilerParams(has_side_effects=EFFECT),
    )(*[pltpu.with_memory_space_constraint(c, pltpu.HBM) for c in cs], *lands)
    return res[0], res[1], res[2:2 + n], res[2 + n:2 + 2 * n], res[2 + 2 * n]


def _rs_chips_wait(send_sems, recv_sems, cs_thru, land_thru, after, name):
    n = len(cs_thru)

    def body(*refs):
        for cp in _rs_chips_copies(refs[:n], refs[n:2 * n], refs[2 * n], refs[2 * n + 1]):
            cp.wait_send()
            cp.wait_recv()

    res = pl.pallas_call(
        body, name=name,
        out_shape=tuple(pltpu.HBM(c.shape, c.dtype) for c in cs_thru) + tuple(pltpu.HBM(l.shape, l.dtype) for l in land_thru),
        in_specs=(HBM_SPEC,) * (2 * n) + (SEM_SPEC, SEM_SPEC, pl.BlockSpec(memory_space=pl.ANY)),
        out_specs=(HBM_SPEC,) * (2 * n), input_output_aliases={i: i for i in range(2 * n)},
        compiler_params=pltpu.CompilerParams(has_side_effects=EFFECT),
    )(*cs_thru, *land_thru, send_sems, recv_sems, after)
    return res[:n], res[n:]


def _swap_list(ghs, name):
    n = len(ghs)

    def body(*refs):
        g, out, send_sems, recv_sems, local_sems = refs[:n], refs[n:2 * n], refs[2 * n], refs[2 * n + 1], refs[2 * n + 2]
        x, y, c = _coords()
        mine = [pltpu.make_async_copy(g[p], out[p].at[c], local_sems.at[p]) for p in range(n)]
        cps = [_remote(g[p], out[p].at[c], send_sems.at[p], recv_sems.at[p], (x, y, 1 - c)) for p in range(n)]
        for cp in mine + cps:
            cp.start()
        for p in range(n):
            _remote(g[p], out[p].at[1 - c], send_sems.at[p], recv_sems.at[p], (x, y, 1 - c)).wait_recv()
        for cp in cps:
            cp.wait_send()
        for cp in mine:
            cp.wait()

    return pl.pallas_call(
        body, name=name,
        out_shape=[jax.ShapeDtypeStruct((2,) + g.shape, g.dtype) for g in ghs],
        in_specs=[pl.BlockSpec(memory_space=pl.ANY)] * n, out_specs=[pl.BlockSpec(memory_space=pl.ANY)] * n,
        scratch_shapes=[pltpu.SemaphoreType.DMA((n,)), pltpu.SemaphoreType.DMA((n,)), pltpu.SemaphoreType.DMA((n,))],
    )(*ghs)


def _pad_win(w):
    return jnp.concatenate([w[:, :416], jnp.zeros((w.shape[0], 96), w.dtype), w[:, 416:1952],
                            w[:, 1952:1960], jnp.zeros((w.shape[0], 120), w.dtype)], axis=1)


def _unpad_win(g):
    return jnp.concatenate([g[:, :416], g[:, 512:2048], g[:, 2048:2056]], axis=1)


def _pad_wq(w):
    return jnp.pad(w.reshape(Q_LORA, HEADS, QK_DIM), ((0, 0), (0, 0), (0, LANES - QK_DIM))).reshape(Q_LORA, HEADS * LANES)


def _unpad_wq(g):
    return g.reshape(Q_LORA, HEADS, LANES)[:, :, :QK_DIM].reshape(Q_LORA, HEADS * QK_DIM)


def _cols_to_shards(a):
    r, c4 = a.shape
    return a.reshape(r, 4, c4 // 4).transpose(1, 0, 2)


def _shards_to_cols(a):
    _, r, c = a.shape
    return a.transpose(1, 0, 2).reshape(r, 4 * c)


def _pack_small(tree):
    parts = []
    for l in range(DEPTH):
        for (n, k) in SMALL:
            parts.append(jnp.pad(tree[n][l].reshape(-1), (0, -k % LANES)))
    flat = jnp.concatenate(parts)
    return jnp.pad(flat, (0, SMALL_ROWS * LANES - flat.shape[0])).reshape(SMALL_ROWS, LANES)


def _unpack_small(buf):
    flat = buf.reshape(-1)
    out = {n: [] for (n, _) in SMALL}
    o = 0
    for l in range(DEPTH):
        for (n, k) in SMALL:
            out[n].append(flat[o:o + k])
            o += k + (-k % LANES)
    return {n: jnp.stack(v) for n, v in out.items()}


def _vec(v, width=LANES):
    return jnp.pad(v.reshape(1, -1), ((0, 0), (0, width - v.shape[-1])))


def kernel(x, c, positions, norm1_w, norm2_w, w_ada, b_ada, w_in, q_a_norm_w, w_q_up, kv_a_norm_w, w_kv_up, q_nope_norm_w, q_pe_norm_w, k_nope_norm_w, k_pe_norm_w, conv_w, conv_b, dt_bias, a_log, d_skip, ssd_norm_w, w_out, w_gate_up, w_down, loss_target, m_norm1_w, m_norm2_w, m_w_ada, m_b_ada, m_w_in, m_q_a_norm_w, m_w_q_up, m_kv_a_norm_w, m_w_kv_up, m_q_nope_norm_w, m_q_pe_norm_w, m_k_nope_norm_w, m_k_pe_norm_w, m_conv_w, m_conv_b, m_dt_bias, m_a_log, m_d_skip, m_ssd_norm_w, m_w_out, m_w_gate_up, m_w_down, v_norm1_w, v_norm2_w, v_w_ada, v_b_ada, v_w_in, v_q_a_norm_w, v_w_q_up, v_kv_a_norm_w, v_w_kv_up, v_q_nope_norm_w, v_q_pe_norm_w, v_k_nope_norm_w, v_k_pe_norm_w, v_conv_w, v_conv_b, v_dt_bias, v_a_log, v_d_skip, v_ssd_norm_w, v_w_out, v_w_gate_up, v_w_down):
    W = dict(zip(WEIGHTS, (norm1_w, norm2_w, w_ada, b_ada, w_in, q_a_norm_w, w_q_up, kv_a_norm_w, w_kv_up, q_nope_norm_w, q_pe_norm_w, k_nope_norm_w, k_pe_norm_w, conv_w, conv_b, dt_bias, a_log, d_skip, ssd_norm_w, w_out, w_gate_up, w_down)))
    M = dict(zip(WEIGHTS, (m_norm1_w, m_norm2_w, m_w_ada, m_b_ada, m_w_in, m_q_a_norm_w, m_w_q_up, m_kv_a_norm_w, m_w_kv_up, m_q_nope_norm_w, m_q_pe_norm_w, m_k_nope_norm_w, m_k_pe_norm_w, m_conv_w, m_conv_b, m_dt_bias, m_a_log, m_d_skip, m_ssd_norm_w, m_w_out, m_w_gate_up, m_w_down)))
    V = dict(zip(WEIGHTS, (v_norm1_w, v_norm2_w, v_w_ada, v_b_ada, v_w_in, v_q_a_norm_w, v_w_q_up, v_kv_a_norm_w, v_w_kv_up, v_q_nope_norm_w, v_q_pe_norm_w, v_k_nope_norm_w, v_k_pe_norm_w, v_conv_w, v_conv_b, v_dt_bias, v_a_log, v_d_skip, v_ssd_norm_w, v_w_out, v_w_gate_up, v_w_down)))
    S = x.shape[1]
    xi, yi, ci = _coords()
    chip = 2 * xi + yi
    dev = 2 * chip + ci
    x0 = x[0]
    tgt = loss_target[0]

    inv_freq = 1.0 / (ROPE_THETA ** (jnp.arange(0, ROPE, 2, dtype=F32) / ROPE))
    ang = positions[0].astype(F32)[:, None] * inv_freq
    cos, sin = jnp.cos(ang), jnp.sin(ang)
    z16, z32, z64 = jnp.zeros((S, 16), F32), jnp.zeros((S, 32), F32), jnp.zeros((S, 64), F32)
    tab_c = jnp.concatenate([jnp.ones((S, 64), F32), cos, cos, z32], axis=1)
    tab_s1 = jnp.concatenate([z64, z16, sin, z32], axis=1)
    tab_s2 = jnp.concatenate([z64, -sin, z16, z32], axis=1)

    blk0 = jnp.concatenate([c.reshape(-1), W['conv_w'].reshape(-1)]).reshape(24, LANES)
    g0 = _ag8(blk0, "ag_c_conv").reshape(8, 24 * LANES)
    c_all = g0[:, :D_MODEL]
    conv_full = g0[0::2, D_MODEL:].reshape(4, DEPTH, CONV_TAPS, 256).transpose(1, 2, 0, 3).reshape(DEPTH, CONV_TAPS, D_CONV)

    sh = [[W[n][l].astype(BF16) for n in BIG] for l in range(DEPTH)]
    got0 = _ag_list([a.reshape(2, a.shape[0] // 2, a.shape[1]) for a in sh[0]], "ag_w0")
    ag1 = _ag_direct_start(sh[1], "ag_w1_start")

    def layer_weights(shards):
        st = {n: a.reshape(4, -1, a.shape[-1]) for n, a in zip(BIG, shards)}
        return dict(w_in=_pad_win(_shards_to_cols(st['w_in'])), w_q_up=_pad_wq(_shards_to_cols(st['w_q_up'])),
                    w_kv_up=_shards_to_cols(st['w_kv_up']), w_out=st['w_out'].reshape(D_MODEL, D_MODEL),
                    w_gate_up=st['w_gate_up'], w_down=st['w_down'].reshape(D_FF, D_MODEL))

    LW = [layer_weights(got0), None]

    b_sh = lax.dynamic_slice_in_dim(W['b_ada'], chip * 1536, 1536, axis=1).reshape(DEPTH, 1, 1536)
    mod_sh = _ada_fwd(c_all, W['w_ada'], b_sh, "ada_fwd")
    g1 = _ag8(mod_sh.reshape(192, LANES), "ag_mod").reshape(8, DEPTH, 8, 1536)
    mod_all = g1[0::2].transpose(1, 2, 0, 3).reshape(DEPTH, 8, 6 * D_MODEL)
    mod = lax.dynamic_index_in_dim(mod_all, dev, axis=1, keepdims=False)

    def mvec(l, k):
        return mod[l, k * D_MODEL:(k + 1) * D_MODEL].reshape(1, D_MODEL)

    def small(name, l, width=None):
        v = W[name][l]
        return _vec(v, width or v.shape[-1])

    def wq_vec(l):
        return _vec(jnp.concatenate([W['q_nope_norm_w'][l], W['q_pe_norm_w'][l]]))

    def wk_vec(l):
        return _vec(jnp.concatenate([W['k_nope_norm_w'][l], W['k_pe_norm_w'][l]]))

    def conv_vec(l):
        return jnp.concatenate([conv_full[l], W['conv_b'][l].reshape(1, D_CONV), jnp.zeros((3, D_CONV), F32)], axis=0)

    def ssd_vec(l):
        return jnp.concatenate([_vec(W['dt_bias'][l]), _vec(W['a_log'][l]), _vec(W['d_skip'][l]), jnp.zeros((5, LANES), F32)], axis=0)

    sv = []
    xcur = x0
    h1 = _row_fwd(fn_norm_mod, "norm_mod_f", [(x0, 0, D_MODEL)], [small('norm1_w', 0), mvec(0, 1), mvec(0, 0)],
                  [(D_MODEL, BF16)])[0]
    _, h1 = lax.optimization_barrier((ag1[4], h1))
    fin = None
    for l in range(DEPTH):
        if l == 1:
            LW[1] = layer_weights(_ag_direct_wait(ag1[0], ag1[1], ag1[2], ag1[3], xcur, "ag_w1_wait"))
        lw = LW[l]
        t = dict(xcur=xcur, h1=h1)
        t['proj'] = proj = _mm(h1, lw['w_in'], 'nn', f"mm_in_{l}")
        t['qa_n'], t['kva_n'] = _row_fwd(fn_lat_norm, f"lat_norm_f{l}", [(proj, 0, 256), (proj, 2, 128)],
                                         [small('q_a_norm_w', l), small('kv_a_norm_w', l)], [(256, BF16), (128, BF16)])
        t['q'] = _mm(t['qa_n'], lw['w_q_up'], 'nn', f"mm_q_{l}")
        t['kv'] = _mm(t['kva_n'], lw['w_kv_up'], 'nn', f"mm_kv_{l}")
        t['qf'], t['kf'], t['vv'] = _row_fwd(
            fn_qk_prep, f"qk_prep_f{l}",
            [(t['q'], 0, 1024), (t['kv'], 0, 1024), (proj, 3, 128), (tab_c, 0, 128), (tab_s1, 0, 128), (tab_s2, 0, 128)],
            [wq_vec(l), wk_vec(l)], [(1024, BF16), (1024, BF16), (1024, BF16)])
        t['ao'], t['lse'] = _attn_fwd(t['qf'], t['kf'], t['vv'], f"attn_f{l}")
        t['xact'] = _conv_fwd(proj, conv_vec(l), f"conv_f{l}")
        t['y'], t['states'] = _ssd_fwd(t['xact'], proj, ssd_vec(l), f"ssd_f{l}")
        t['mix'] = _row_fwd(fn_gated_mix, f"gated_f{l}", [(t['y'], 0, 512), (proj, 1, 512), (t['ao'], 0, 512)],
                            [small('ssd_norm_w', l)], [(1024, BF16)])[0]
        t['mo'] = _mm(t['mix'], lw['w_out'], 'nn', f"mm_out_{l}")
        t['x1'], t['h2'] = _row_fwd(fn_resid_norm, f"resid_mid_f{l}", [(xcur, 0, D_MODEL), (t['mo'], 0, D_MODEL)],
                                    [mvec(l, 2), small('norm2_w', l), mvec(l, 4), mvec(l, 3)],
                                    [(D_MODEL, F32), (D_MODEL, BF16)])
        t['gu'] = _mm(t['h2'], lw['w_gate_up'], 'nn', f"mm_gu_{l}", stack='b')
        t['act'] = _row_fwd(fn_swiglu, f"swiglu_f{l}", [(t['gu'], 0, 2 * D_FF)], [], [(D_FF, BF16)], tm=128)[0]
        t['ff'] = _mm(t['act'], lw['w_down'], 'nn', f"mm_down_{l}")
        if l + 1 < DEPTH:
            xcur, h1 = _row_fwd(fn_resid_norm, f"resid_end_f{l}", [(t['x1'], 0, D_MODEL), (t['ff'], 0, D_MODEL)],
                                [mvec(l, 5), small('norm1_w', l + 1), mvec(l + 1, 1), mvec(l + 1, 0)],
                                [(D_MODEL, F32), (D_MODEL, BF16)])
        else:
            fin = _final(t['x1'], t['ff'], tgt, mvec(l, 5), "final_loss")
        sv.append(t)

    dx1, dff, dg2_last, loss_acc = fin
    loss = lax.psum(loss_acc[0, 0], ("x", "y", "c"))
    gfull = {n: [None] * DEPTH for n in BIG}
    gsm = {n: [None] * DEPTH for (n, _) in SMALL}
    dmod = [[None] * 6 for _ in range(DEPTH)]
    dmod[DEPTH - 1][5] = dg2_last
    grad_x = None
    pending = []

    def rs_begin(l, names, tag):
        g4 = [gfull[n][l].reshape(4, 2, gfull[n][l].shape[1] // 2, gfull[n][l].shape[2]) for n in names]
        sib = _rs_sibling_list(g4, f"rs_sibling_{tag}")
        cs = _sum_sibling(g4, sib, ci, f"sum_sibling_{tag}")
        h = _rs_chips_start(cs, f"rs_chips_start_{tag}")
        pending.append((l, names, h))
        return h[4]

    for l in reversed(range(DEPTH)):
        t = sv[l]
        lw = LW[l]
        proj = t['proj']
        dact = _mm(dff, lw['w_down'], 'nt', f"mm_down_dx{l}")
        gfull['w_down'][l] = _mm(t['act'], dff, 'tn', f"mm_down_dw{l}").reshape(4, D_FF // 4, D_MODEL)
        dgu = _row_bwd(fn_swiglu, f"swiglu_b{l}", [(t['gu'], 0, 2 * D_FF)], [], [(dact, 0, D_FF)], [0], [], tm=128)[0]
        dh2 = _mm(dgu, lw['w_gate_up'], 'nt', f"mm_gu_dx{l}", stack='b')
        gfull['w_gate_up'][l] = _mm(t['h2'], dgu, 'tn', f"mm_gu_dw{l}", stack='out')
        if l == 0:
            _, dh2 = lax.optimization_barrier((rs_begin(0, EARLY, "l0a"), dh2))
        dxc, dmo, dmod[l][2], gsm['norm2_w'][l], dmod[l][4], dmod[l][3] = _row_bwd(
            fn_resid_norm, f"resid_mid_b{l}", [(t['xcur'], 0, D_MODEL), (t['mo'], 0, D_MODEL)],
            [mvec(l, 2), small('norm2_w', l), mvec(l, 4), mvec(l, 3)], [(dx1, 0, D_MODEL), (dh2, 0, D_MODEL)], [0, 1], [0, 1, 2, 3])
        dmix = _mm(dmo, lw['w_out'], 'nt', f"mm_out_dx{l}")
        gfull['w_out'][l] = _mm(t['mix'], dmo, 'tn', f"mm_out_dw{l}").reshape(4, D_MODEL // 4, D_MODEL)
        dy, dz, gsm['ssd_norm_w'][l] = _row_bwd(fn_gated_norm, f"gated_b{l}", [(t['y'], 0, 512), (proj, 1, 512)],
                                                [small('ssd_norm_w', l)], [(dmix, 1, 512)], [0, 1], [0])
        dxact, ddt, dsv = _ssd_bwd(t['xact'], proj, ssd_vec(l), t['states'], dy, f"ssd_b{l}")
        gsm['dt_bias'][l], gsm['a_log'][l], gsm['d_skip'][l] = dsv[0, :8], dsv[1, :8], dsv[2, :8]
        dxbc, dcv = _conv_bwd(proj, conv_vec(l), dxact, f"conv_b{l}")
        gsm['conv_w'][l] = dcv[:CONV_TAPS]
        gsm['conv_b'][l] = dcv[CONV_TAPS]
        delta = _row_fwd(fn_attn_delta, f"attn_delta{l}", [(dmix, 0, 512), (t['ao'], 0, 512)], [], [(512, F32)])[0]
        lse_r = (t['lse'][:, ::64] * LOG2E).T.reshape(HEADS // 2, 2, S)
        delta_r = delta[:, ::64].T.reshape(HEADS // 2, 2, S)
        dqT, dkf, dvv = _attn_bwd(t['qf'], t['kf'], t['kf'].T, t['vv'], dmix, lse_r, delta_r, f"attn_b{l}")
        dqf = dqT.T
        dq, dkv, dkpe, dwq, dwk = _row_bwd(
            fn_qk_prep, f"qk_prep_b{l}",
            [(t['q'], 0, 1024), (t['kv'], 0, 1024), (proj, 3, 128), (tab_c, 0, 128), (tab_s1, 0, 128), (tab_s2, 0, 128)],
            [wq_vec(l), wk_vec(l)], [(dqf, 0, 1024), (dkf, 0, 1024), (dvv, 0, 1024)], [0, 1, 2], [0, 1])
        gsm['q_nope_norm_w'][l], gsm['q_pe_norm_w'][l] = dwq[0, :NOPE], dwq[0, NOPE:QK_DIM]
        gsm['k_nope_norm_w'][l], gsm['k_pe_norm_w'][l] = dwk[0, :NOPE], dwk[0, NOPE:QK_DIM]
        dqa_n = _mm(dq, lw['w_q_up'], 'nt', f"mm_q_dx{l}")
        gfull['w_q_up'][l] = _cols_to_shards(_unpad_wq(_mm(t['qa_n'], dq, 'tn', f"mm_q_dw{l}")))
        dkva_n = _mm(dkv, lw['w_kv_up'], 'nt', f"mm_kv_dx{l}")
        gfull['w_kv_up'][l] = _cols_to_shards(_mm(t['kva_n'], dkv, 'tn', f"mm_kv_dw{l}"))
        dqa, dkva, dqw, dkvw = _row_bwd(fn_lat_norm, f"lat_norm_b{l}", [(proj, 0, 256), (proj, 2, 128)],
                                        [small('q_a_norm_w', l), small('kv_a_norm_w', l)],
                                        [(dqa_n, 0, 256), (dkva_n, 0, 128)], [0, 1], [0, 1])
        gsm['q_a_norm_w'][l], gsm['kv_a_norm_w'][l] = dqw[0], dkvw[0]
        dproj = jnp.concatenate([dqa, dkva, dkpe, dz, dxbc, ddt], axis=1)
        dh1 = _mm(dproj, lw['w_in'], 'nt', f"mm_in_dx{l}")
        gfull['w_in'][l] = _cols_to_shards(_unpad_win(_mm(t['h1'], dproj, 'tn', f"mm_in_dw{l}")))
        if l > 0:
            p = sv[l - 1]
            dx1, dff, dmod[l - 1][5], gsm['norm1_w'][l], dmod[l][1], dmod[l][0] = _row_bwd(
                fn_resid_norm, f"resid_end_b{l - 1}", [(p['x1'], 0, D_MODEL), (p['ff'], 0, D_MODEL)],
                [mvec(l - 1, 5), small('norm1_w', l), mvec(l, 1), mvec(l, 0)], [(dxc, 0, D_MODEL), (dh1, 0, D_MODEL)],
                [0, 1], [0, 1, 2, 3])
            _, dff = lax.optimization_barrier((rs_begin(l, BIG, f"l{l}"), dff))
        else:
            grad_x, gsm['norm1_w'][l], dmod[l][1], dmod[l][0] = _row_bwd(
                fn_norm_mod_pass, "norm_mod_b", [(x0, 0, D_MODEL)], [small('norm1_w', 0), mvec(0, 1), mvec(0, 0)],
                [(dxc, 0, D_MODEL), (dh1, 0, D_MODEL)], [0], [0, 1, 2])
        for n in ('norm1_w', 'norm2_w', 'ssd_norm_w'):
            gsm[n][l] = gsm[n][l][0]

    for l in range(DEPTH):
        gsm['b_ada'][l] = jnp.concatenate([d[0] for d in dmod[l]])
    sm_part = _pack_small({n: jnp.stack(v) for n, v in gsm.items()})
    _, sm_part = lax.optimization_barrier((rs_begin(0, BIG[2:], "l0b"), sm_part))
    sm_all = _ag8(sm_part, "ag_small")

    def with_conv(tree):
        wide = lax.dynamic_update_slice_in_dim(jnp.zeros((DEPTH, CONV_TAPS, D_CONV), F32), tree['conv_w'], chip * 256, axis=2)
        return {**tree, 'conv_w': wide}

    g_sm, d_sm, m_sm, v_sm = _adam(_pack_small(with_conv(W)), _pack_small(with_conv(M)), _pack_small(with_conv(V)),
                                   [(sm_all, d * SMALL_ROWS) for d in range(8)], "adam_small")
    out_small = [_unpack_small(b) for b in (g_sm, d_sm, m_sm, v_sm)]
    for o in out_small:
        o['conv_w'] = lax.dynamic_slice_in_dim(o['conv_w'].reshape(DEPTH, CONV_TAPS, D_CONV), chip * 256, 256, axis=2)

    dmod_all = sm_all.reshape(8, SMALL_ROWS * LANES)
    per_layer = sum(k + (-k % LANES) for (_, k) in SMALL)
    dmod_sh = jnp.stack([lax.dynamic_slice_in_dim(dmod_all[:, l * per_layer:l * per_layer + 6 * D_MODEL], chip * 1536, 1536, axis=1)
                         for l in range(DEPTH)])
    g_ada = _ada_bwd(c_all.T, dmod_sh, "ada_bwd")
    ada = _adam(W['w_ada'].reshape(DEPTH * D_MODEL, 1536), M['w_ada'].reshape(DEPTH * D_MODEL, 1536),
                V['w_ada'].reshape(DEPTH * D_MODEL, 1536), [(g_ada.reshape(DEPTH * D_MODEL, 1536), 0)], "adam_ada")
    out_ada = [a.reshape(DEPTH, D_MODEL, 1536) for a in ada]

    keys, cs_all, land_all = [], [], []
    for (l, names, (send_sems, recv_sems, cs_thru, land_thru, _)) in pending:
        cs, lands = _rs_chips_wait(send_sems, recv_sems, cs_thru, land_thru, ada[3], f"rs_chips_wait_l{l}{len(names)}")
        keys += [(l, n) for n in names]
        cs_all += list(cs)
        land_all += list(lands)
    ghalf = _sum_chips(cs_all, land_all, chip, "sum_chips")
    gboth = _swap_list(ghalf, "swap_halves")
    gshard = {k: g.reshape(2 * g.shape[1], g.shape[2]) for k, g in zip(keys, gboth)}
    g_big = [jnp.stack([gshard[(l, n)] for l in range(DEPTH)]) for n in BIG]
    d_big, m_big, v_big = _adam_multi([W[n] for n in BIG], [M[n] for n in BIG], [V[n] for n in BIG], g_big, "adam_big")
    out_big = [dict(zip(BIG, o)) for o in (g_big, d_big, m_big, v_big)]

    outs = [loss, grad_x[None]]
    for k in range(4):
        for n in WEIGHTS:
            if n == 'w_ada':
                outs.append(out_ada[k])
            elif n in BIG:
                outs.append(out_big[k][n])
            else:
                outs.append(out_small[k][n])
    return tuple(outs)
```

```python
import functools

import jax
import jax.numpy as jnp
from jax import lax
from jax.experimental import pallas as pl
from jax.experimental.pallas import tpu as pltpu

F32 = jnp.float32
BF16 = jnp.bfloat16
MESH = pl.DeviceIdType.MESH

D_MODEL = 1024
DEPTH = 2
HEADS = 8
NOPE = 64
ROPE = 32
QK_DIM = NOPE + ROPE
Q_LORA = 256
KV_LORA = 128
SSD_HEADS = 8
SSD_P = 64
SSD_N = 128
CHUNK = 128
CONV_TAPS = 4
D_SSD = 512
D_CONV = 1024
D_FF = 2816
D_IN = 1960
D_IN_PAD = 2176
EPS = 1e-6
ROPE_THETA = 10000.0
ATT_SCALE = QK_DIM ** -0.5
NEG = -1e30
LANES = 128
VMEM_LIMIT = 48 * 1024 * 1024

ADAM_LR, ADAM_B1, ADAM_B2, ADAM_EPS, ADAM_WD, ADAM_STEP = 0.001, 0.9, 0.999, 1e-08, 0.01, 10

WEIGHTS = ['norm1_w', 'norm2_w', 'w_ada', 'b_ada', 'w_in', 'q_a_norm_w', 'w_q_up', 'kv_a_norm_w', 'w_kv_up',
           'q_nope_norm_w', 'q_pe_norm_w', 'k_nope_norm_w', 'k_pe_norm_w', 'conv_w', 'conv_b', 'dt_bias', 'a_log',
           'd_skip', 'ssd_norm_w', 'w_out', 'w_gate_up', 'w_down']
BIG = ['w_down', 'w_gate_up', 'w_out', 'w_kv_up', 'w_q_up', 'w_in']
EARLY = BIG[:2]
SMALL = [('b_ada', 6144), ('conv_w', 4096), ('norm1_w', 1024), ('norm2_w', 1024), ('conv_b', 1024), ('ssd_norm_w', 512),
         ('q_a_norm_w', 256), ('kv_a_norm_w', 128), ('q_nope_norm_w', 64), ('q_pe_norm_w', 32),
         ('k_nope_norm_w', 64), ('k_pe_norm_w', 32), ('dt_bias', 8), ('a_log', 8), ('d_skip', 8)]
SMALL_ROWS = 240


def _cp(sem=None, **kw):
    return pltpu.CompilerParams(dimension_semantics=sem, vmem_limit_bytes=VMEM_LIMIT, **kw)


def _dot(a, b, dims, prec=None):
    return lax.dot_general(a, b, (dims, ((), ())), preferred_element_type=F32, precision=prec)


def _tile(dim, target):
    best = 0
    for t in range(LANES, min(dim, target) + 1, LANES):
        if dim % t == 0:
            best = t
    if best < 256 and dim <= 2304:
        return dim
    return best


def _mm(a, b, mode, name, out_dtype=F32, stack=None):
    ns = None
    if stack == 'b':
        ns = b.shape[2]
        if mode == 'nn':
            (M, K), N = a.shape, 4 * ns
        else:
            (M, K), N = a.shape, b.shape[1]
    elif mode == 'nn':
        (M, K), (_, N) = a.shape, b.shape
    elif mode == 'nt':
        (M, K), (N, _) = a.shape, b.shape
    else:
        (K, M), (_, N) = a.shape, b.shape
    if stack == 'out':
        ns = N // 4
    tm, tn, tk = _tile(M, 512), _tile(N, 1408), _tile(K, 1408)
    if stack == 'b' and mode == 'nt':
        tk = ns
    elif stack is not None:
        tn = ns
    nk = K // tk
    dims = {'nn': ((1,), (0,)), 'nt': ((1,), (1,)), 'tn': ((0,), (0,))}[mode]

    def body(a_ref, b_ref, o_ref, acc):
        k = pl.program_id(2)

        @pl.when(k == 0)
        def _():
            acc[...] = jnp.zeros_like(acc)

        acc[...] += _dot(a_ref[...].astype(BF16), b_ref[...].astype(BF16), dims)

        @pl.when(k == nk - 1)
        def _():
            o_ref[...] = acc[...].astype(o_ref.dtype)

    a_spec = pl.BlockSpec((tk, tm), lambda i, j, k: (k, i)) if mode == 'tn' else pl.BlockSpec((tm, tk), lambda i, j, k: (i, k))
    b_spec = pl.BlockSpec((tn, tk), lambda i, j, k: (j, k)) if mode == 'nt' else pl.BlockSpec((tk, tn), lambda i, j, k: (k, j))
    o_spec, o_shape = pl.BlockSpec((tm, tn), lambda i, j, k: (i, j)), (M, N)
    if stack == 'b':
        b_spec = (pl.BlockSpec((None, tn, ns), lambda i, j, k: (k, j, 0)) if mode == 'nt'
                  else pl.BlockSpec((None, tk, ns), lambda i, j, k: (j, k, 0)))
    if stack == 'out':
        o_spec, o_shape = pl.BlockSpec((None, tm, ns), lambda i, j, k: (j, i, 0)), (4, M, ns)
    return pl.pallas_call(
        body, name=name, grid=(M // tm, N // tn, nk),
        in_specs=[a_spec, b_spec], out_specs=o_spec,
        out_shape=jax.ShapeDtypeStruct(o_shape, out_dtype),
        scratch_shapes=[pltpu.VMEM((tm, tn), F32)],
        compiler_params=_cp(("parallel", "parallel", "arbitrary")),
    )(a, b)


def _rspec(tm, w, cb):
    return pl.BlockSpec((tm, w), lambda i: (i, cb))


def _vspec(shape):
    return pl.BlockSpec(shape, lambda i: (0,) * len(shape))


def _row_fwd(fn, name, rows, vecs, outs, tm=256):
    S = rows[0][0].shape[0]
    tm = min(tm, S)
    nin = len(rows) + len(vecs)

    def body(*refs):
        res = fn(*[r[...] for r in refs[:nin]])
        for o_ref, r in zip(refs[nin:], res):
            o_ref[...] = r.astype(o_ref.dtype)

    return pl.pallas_call(
        body, name=name, grid=(S // tm,),
        in_specs=[_rspec(tm, w, cb) for (_, cb, w) in rows] + [_vspec(v.shape) for v in vecs],
        out_specs=[_rspec(tm, w, 0) for (w, _) in outs],
        out_shape=[jax.ShapeDtypeStruct((S, w), dt) for (w, dt) in outs],
        compiler_params=_cp(("parallel",)),
    )(*[r[0] for r in rows], *vecs)


def _row_bwd(fn, name, rows, vecs, cts, drows, dvecs, tm=256):
    S = rows[0][0].shape[0]
    tm = min(tm, S)
    nr, nv, nc = len(rows), len(vecs), len(cts)
    didx = list(drows) + [nr + j for j in dvecs]

    def body(*refs):
        vals = [r[...] for r in refs[:nr + nv]]
        ct = tuple(r[...].astype(F32) for r in refs[nr + nv:nr + nv + nc])
        outs = refs[nr + nv + nc:]

        def g(*d):
            a = list(vals)
            for k, val in zip(didx, d):
                a[k] = val
            return tuple(fn(*a))

        _, vjp = jax.vjp(g, *[vals[k] for k in didx])
        grads = vjp(ct)
        for o, gr in zip(outs[:len(drows)], grads[:len(drows)]):
            o[...] = gr.astype(o.dtype)

        @pl.when(pl.program_id(0) == 0)
        def _():
            for o in outs[len(drows):]:
                o[...] = jnp.zeros_like(o)

        for o, gr in zip(outs[len(drows):], grads[len(drows):]):
            o[...] += gr

    return pl.pallas_call(
        body, name=name, grid=(S // tm,),
        in_specs=[_rspec(tm, w, cb) for (_, cb, w) in rows] + [_vspec(v.shape) for v in vecs]
        + [_rspec(tm, w, cb) for (_, cb, w) in cts],
        out_specs=[_rspec(tm, rows[k][2], 0) for k in drows] + [_vspec(vecs[j].shape) for j in dvecs],
        out_shape=[jax.ShapeDtypeStruct((S, rows[k][2]), F32) for k in drows]
        + [jax.ShapeDtypeStruct(vecs[j].shape, F32) for j in dvecs],
        compiler_params=_cp(("arbitrary",)),
    )(*[r[0] for r in rows], *vecs, *[c[0] for c in cts])


def _rms(x):
    return x * lax.rsqrt(jnp.mean(x * x, axis=-1, keepdims=True) + EPS)


def fn_norm_mod(x, nw, sc, sh):
    return (_rms(x) * nw * (1.0 + sc) + sh,)


def fn_norm_mod_pass(x, nw, sc, sh):
    return (x, _rms(x) * nw * (1.0 + sc) + sh)


def fn_resid_norm(x, d, g, nw, sc, sh):
    xn = x + g * d
    return (xn, _rms(xn) * nw * (1.0 + sc) + sh)


def fn_lat_norm(qa, kva, qw, kvw):
    return (_rms(qa) * qw, _rms(kva) * kvw)


@functools.partial(jax.custom_vjp, nondiff_argnums=(1,))
def _lroll(x, s):
    return pltpu.roll(x, s, 1)


def _lroll_fwd(x, s):
    return pltpu.roll(x, s, 1), None


def _lroll_bwd(s, _, g):
    return (pltpu.roll(g, (LANES - s) % LANES, 1),)


_lroll.defvjp(_lroll_fwd, _lroll_bwd)


def _lane_masks(shape):
    lane = lax.broadcasted_iota(jnp.int32, shape, 1)
    return (lane < NOPE).astype(F32), ((lane >= NOPE) & (lane < QK_DIM)).astype(F32)


def _rope(t, tc, ts1, ts2):
    return t * tc + _lroll(t, 16) * ts1 + _lroll(t, LANES - 16) * ts2


def fn_qk_prep(q, kv, kpe, tc, ts1, ts2, wq, wk):
    mn, mp = _lane_masks((1, LANES))
    mhi = 1.0 - mn

    def head_norm(t, w):
        rn = lax.rsqrt(jnp.sum(t * t * mn, axis=-1, keepdims=True) * (1.0 / NOPE) + EPS)
        rp = lax.rsqrt(jnp.sum(t * t * mp, axis=-1, keepdims=True) * (1.0 / ROPE) + EPS)
        return t * (rn * mn + rp * mp) * w

    kp = _rope(head_norm(_lroll(kpe, NOPE), wk) * mp, tc, ts1, ts2)
    qs, ks, vs = [], [], []
    for h in range(HEADS):
        qs.append(_rope(head_norm(q[:, h * LANES:(h + 1) * LANES], wq), tc, ts1, ts2))
        t = kv[:, h * LANES:(h + 1) * LANES]
        ks.append(head_norm(t, wk) * mn + kp)
        vs.append(_lroll(t, NOPE) * mn + mhi)
    return (jnp.concatenate(qs, axis=1), jnp.concatenate(ks, axis=1), jnp.concatenate(vs, axis=1))


def fn_attn_delta(do, o):
    mn, _ = _lane_masks((1, LANES))
    mhi = 1.0 - mn
    out = []
    for hp in range(HEADS // 2):
        y = do[:, hp * LANES:(hp + 1) * LANES] * o[:, hp * LANES:(hp + 1) * LANES]
        out.append(jnp.sum(y * mn, axis=-1, keepdims=True) * mn + jnp.sum(y * mhi, axis=-1, keepdims=True) * mhi)
    return (jnp.concatenate(out, axis=1),)


def fn_gated_norm(y, z, w):
    u = y * jax.nn.silu(z)
    half = D_SSD // 2
    return (jnp.concatenate([_rms(u[:, :half]), _rms(u[:, half:])], axis=1) * w,)


def fn_gated_mix(y, z, ao, w):
    return (jnp.concatenate([ao, fn_gated_norm(y, z, w)[0]], axis=1),)


def fn_swiglu(gu):
    return (jax.nn.silu(gu[:, :D_FF]) * gu[:, D_FF:],)


def _final(x1, ff, tgt, g2, name):
    S = x1.shape[0]
    tm = min(256, S)

    def body(x_ref, f_ref, t_ref, g_ref, dx_ref, df_ref, dg_ref, l_ref):
        @pl.when(pl.program_id(0) == 0)
        def _():
            dg_ref[...] = jnp.zeros_like(dg_ref)
            l_ref[...] = jnp.zeros_like(l_ref)

        f = f_ref[...]
        g = g_ref[...]
        e = x_ref[...] + g * f - t_ref[...]
        dx = e * (1.0 / D_MODEL)
        dx_ref[...] = dx
        df_ref[...] = g * dx
        dg_ref[...] += jnp.sum(dx * f, axis=0, keepdims=True)
        l_ref[...] += jnp.sum(e * e) * (0.5 / D_MODEL)

    r = _rspec(tm, D_MODEL, 0)
    return pl.pallas_call(
        body, name=name, grid=(S // tm,),
        in_specs=[r, r, r, _vspec((1, D_MODEL))],
        out_specs=[r, r, _vspec((1, D_MODEL)), _vspec((1, LANES))],
        out_shape=[jax.ShapeDtypeStruct((S, D_MODEL), F32), jax.ShapeDtypeStruct((S, D_MODEL), F32),
                   jax.ShapeDtypeStruct((1, D_MODEL), F32), jax.ShapeDtypeStruct((1, LANES), F32)],
        compiler_params=_cp(("arbitrary",)),
    )(x1, ff, tgt, g2)


def _causal_mask(t):
    r = lax.broadcasted_iota(jnp.int32, (t, t), 0)
    c = lax.broadcasted_iota(jnp.int32, (t, t), 1)
    return c <= r


LOG2E = 1.4426950408889634
EXP2_SCALE = ATT_SCALE * LOG2E
ATT_TQ, ATT_TK = 512, 1024
ATT_BQ, ATT_BK = 1024, 512


def _attn_fwd(qf, kf, va, name):
    S = qf.shape[0]
    T, TK = min(ATT_TQ, S), min(ATT_TK, S)
    nmask = max(1, T // TK)

    def body(q_ref, k_ref, v_ref, o_ref, l_ref):
        i = pl.program_id(1)
        r = lax.broadcasted_iota(jnp.int32, (T, TK), 0)
        c = lax.broadcasted_iota(jnp.int32, (T, TK), 1)
        qs = [q_ref[:, hh * LANES:(hh + 1) * LANES] for hh in range(2)]

        def blk(j, carry, masked):
            off = pl.multiple_of(j * TK, TK)
            out = []
            for hh in range(2):
                m, acc = carry[hh]
                s = _dot(qs[hh], k_ref[pl.ds(off, TK), hh * LANES:(hh + 1) * LANES], ((1,), (1,)))
                if masked:
                    s = jnp.where(c + j * TK <= r + i * T, s, NEG)
                mn = jnp.maximum(m, jnp.max(s, axis=1, keepdims=True))
                p = jnp.exp2((s - mn) * EXP2_SCALE)
                al = jnp.exp2((m - mn) * EXP2_SCALE)
                vj = v_ref[pl.ds(off, TK), hh * LANES:(hh + 1) * LANES]
                out.append((mn, al * acc + _dot(p.astype(BF16), vj, ((1,), (0,)))))
            return tuple(out)

        one = (jnp.full((T, 1), NEG, F32), jnp.zeros((T, LANES), F32))
        nfull = lax.div(i * T, TK)
        carry = lax.fori_loop(0, nfull, lambda j, cr: blk(j, cr, False), (one, one))
        for t in range(nmask):
            carry = blk(nfull + t, carry, True)
        for hh in range(2):
            m, acc = carry[hh]
            l = acc[:, 64:65]
            o_ref[:, hh * 64:(hh + 1) * 64] = (acc / l)[:, :64]
            l_ref[:, hh * 64:(hh + 1) * 64] = jnp.broadcast_to(m * ATT_SCALE + jnp.log(l), (T, 64))

    return pl.pallas_call(
        body, name=name, grid=(HEADS // 2, S // T),
        in_specs=[pl.BlockSpec((T, 256), lambda h, i: (i, h)), pl.BlockSpec((S, 256), lambda h, i: (0, h)),
                  pl.BlockSpec((S, 256), lambda h, i: (0, h))],
        out_specs=[pl.BlockSpec((T, LANES), lambda h, i: (i, h)), pl.BlockSpec((T, LANES), lambda h, i: (i, h))],
        out_shape=[jax.ShapeDtypeStruct((S, D_SSD), F32), jax.ShapeDtypeStruct((S, D_SSD), F32)],
        compiler_params=_cp(("parallel", "parallel")),
    )(qf, kf, va)


def _attn_bwd(qf, kf, kT, va, do, lse_r, delta_r, name):
    S = qf.shape[0]
    T, TK = min(ATT_BQ, S), min(ATT_BK, S)
    nq = S // T
    nmask = max(1, TK // T)

    def body(q_ref, k_ref, kT_ref, v_ref, do_ref, l_ref, d_ref, dqT_ref, dk_ref, dv_ref):
        j = pl.program_id(1)

        @pl.when(j == 0)
        def _():
            dqT_ref[...] = jnp.zeros_like(dqT_ref)

        r = lax.broadcasted_iota(jnp.int32, (TK, T), 0)
        c = lax.broadcasted_iota(jnp.int32, (TK, T), 1)
        lo = (lax.broadcasted_iota(jnp.int32, (1, LANES), 1) < 64).astype(F32)
        ks = [k_ref[:, hh * LANES:(hh + 1) * LANES] for hh in range(2)]
        vs = [v_ref[:, hh * LANES:(hh + 1) * LANES] for hh in range(2)]
        kTs = [kT_ref[hh * LANES:(hh + 1) * LANES, :] for hh in range(2)]

        def blk(i, carry, masked):
            off = pl.multiple_of(i * T, T)
            dall = do_ref[pl.ds(off, T), :]
            out = []
            for hh in range(2):
                dk, dv = carry[hh]
                q = q_ref[pl.ds(off, T), hh * LANES:(hh + 1) * LANES]
                dop = ((dall if hh == 0 else pltpu.roll(dall, 64, 1)) * lo).astype(BF16)
                lrow = l_ref[0, hh:hh + 1, pl.ds(off, T)]
                drow = d_ref[0, hh:hh + 1, pl.ds(off, T)]
                pT = jnp.exp2(_dot(ks[hh], q, ((1,), (1,))) * EXP2_SCALE - lrow)
                if masked:
                    pT = jnp.where(r + j * TK <= c + i * T, pT, 0.0)
                dpT = _dot(vs[hh], dop, ((1,), (1,)))
                dsT = (pT * (dpT - drow) * ATT_SCALE).astype(BF16)
                dv = dv + _dot(pT.astype(BF16), dop, ((1,), (0,)))
                dk = dk + _dot(dsT, q, ((1,), (0,)))
                dqT_ref[hh * LANES:(hh + 1) * LANES, pl.ds(off, T)] += _dot(kTs[hh], dsT, ((1,), (0,)))
                out.append((dk, dv))
            return tuple(out)

        z = (jnp.zeros((TK, LANES), F32), jnp.zeros((TK, LANES), F32))
        first = lax.div(j * TK, T)
        carry = (z, z)
        for t in range(nmask):
            carry = blk(first + t, carry, True)
        carry = lax.fori_loop(first + nmask, nq, lambda i, cr: blk(i, cr, False), carry)
        for hh in range(2):
            dk_ref[:, hh * LANES:(hh + 1) * LANES] = carry[hh][0]
            dv_ref[:, hh * LANES:(hh + 1) * LANES] = carry[hh][1]

    return pl.pallas_call(
        body, name=name, grid=(HEADS // 2, S // TK),
        in_specs=[pl.BlockSpec((S, 256), lambda h, j: (0, h)), pl.BlockSpec((TK, 256), lambda h, j: (j, h)),
                  pl.BlockSpec((256, TK), lambda h, j: (h, j)), pl.BlockSpec((TK, 256), lambda h, j: (j, h)),
                  pl.BlockSpec((S, LANES), lambda h, j: (0, h)), pl.BlockSpec((1, 2, S), lambda h, j: (h, 0, 0)),
                  pl.BlockSpec((1, 2, S), lambda h, j: (h, 0, 0))],
        out_specs=[pl.BlockSpec((256, S), lambda h, j: (h, 0)), pl.BlockSpec((TK, 256), lambda h, j: (j, h)),
                   pl.BlockSpec((TK, 256), lambda h, j: (j, h))],
        out_shape=[jax.ShapeDtypeStruct((D_MODEL, S), F32), jax.ShapeDtypeStruct((S, D_MODEL), F32),
                   jax.ShapeDtypeStruct((S, D_MODEL), F32)],
        compiler_params=_cp(("parallel", "arbitrary")),
    )(qf, kf, kT, va, do, lse_r, delta_r)


def _shift_down(x, s):
    if s == 0:
        return x
    rows = lax.broadcasted_iota(jnp.int32, x.shape, 0)
    return jnp.where(rows >= s, pltpu.roll(x, s, 0), 0.0)


def _shift_up(x, s):
    if s == 0:
        return x
    n = x.shape[0]
    rows = lax.broadcasted_iota(jnp.int32, x.shape, 0)
    return jnp.where(rows < n - s, pltpu.roll(x, n - s, 0), 0.0)


def _conv_fwd(proj, cvec, name):
    S = proj.shape[0]

    def body(x_ref, c_ref, o_ref):
        x = x_ref[...]
        y = jnp.broadcast_to(c_ref[4:5, :], x.shape)
        for k in range(CONV_TAPS):
            y = y + c_ref[k:k + 1, :] * _shift_down(x, CONV_TAPS - 1 - k)
        o_ref[...] = y * jax.nn.sigmoid(y)

    return pl.pallas_call(
        body, name=name, grid=(D_CONV // LANES,),
        in_specs=[pl.BlockSpec((S, LANES), lambda j: (0, 8 + j)), pl.BlockSpec((8, LANES), lambda j: (0, j))],
        out_specs=pl.BlockSpec((S, LANES), lambda j: (0, j)),
        out_shape=jax.ShapeDtypeStruct((S, D_CONV), F32),
        compiler_params=_cp(("parallel",)),
    )(proj, cvec)


def _conv_bwd(proj, cvec, dact, name):
    S = proj.shape[0]

    def body(x_ref, c_ref, d_ref, dx_ref, dc_ref):
        x = x_ref[...]
        y = jnp.broadcast_to(c_ref[4:5, :], x.shape)
        for k in range(CONV_TAPS):
            y = y + c_ref[k:k + 1, :] * _shift_down(x, CONV_TAPS - 1 - k)
        sg = jax.nn.sigmoid(y)
        dy = d_ref[...] * (sg * (1.0 + y * (1.0 - sg)))
        dx = jnp.zeros_like(x)
        for k in range(CONV_TAPS):
            s = CONV_TAPS - 1 - k
            dx = dx + c_ref[k:k + 1, :] * _shift_up(dy, s)
            dc_ref[k:k + 1, :] = jnp.sum(dy * _shift_down(x, s), axis=0, keepdims=True)
        dx_ref[...] = dx
        dc_ref[4:5, :] = jnp.sum(dy, axis=0, keepdims=True)
        dc_ref[5:8, :] = jnp.zeros((3, LANES), F32)

    return pl.pallas_call(
        body, name=name, grid=(D_CONV // LANES,),
        in_specs=[pl.BlockSpec((S, LANES), lambda j: (0, 8 + j)), pl.BlockSpec((8, LANES), lambda j: (0, j)),
                  pl.BlockSpec((S, LANES), lambda j: (0, j))],
        out_specs=[pl.BlockSpec((S, LANES), lambda j: (0, j)), pl.BlockSpec((8, LANES), lambda j: (0, j))],
        out_shape=[jax.ShapeDtypeStruct((S, D_CONV), F32), jax.ShapeDtypeStruct((8, D_CONV), F32)],
        compiler_params=_cp(("parallel",)),
    )(proj, cvec, dact)


def fn_ssd_chunk(xs, bm, cm, dtr, state, vecs):
    Q = CHUNK
    dt = jax.nn.softplus(dtr + vecs[0:1])
    a = -jnp.exp(vecs[1:2])
    d_skip = vecs[2:3]
    adt = dt * a
    tril = _causal_mask(Q)
    acs = _dot(tril.astype(F32), adt, ((1,), (0,)), lax.Precision.HIGHEST)
    acs_t = acs.T
    ys, new_states = [], []
    Bs = [bm[:, g * SSD_N:(g + 1) * SSD_N].astype(BF16) for g in range(2)]
    Cs = [cm[:, g * SSD_N:(g + 1) * SSD_N].astype(BF16) for g in range(2)]
    Gs = [_dot(Cs[g], Bs[g], ((1,), (1,))) for g in range(2)]
    for h in range(SSD_HEADS):
        g = h // (SSD_HEADS // 2)
        x = xs[:, h * SSD_P:(h + 1) * SSD_P]
        B, C = Bs[g], Cs[g]
        xdt = x * dt[:, h:h + 1]
        A = acs[:, h:h + 1]
        L = jnp.exp(jnp.where(tril, A - acs_t[h:h + 1, :], -jnp.inf))
        M = Gs[g] * L
        yd = _dot(M.astype(BF16), xdt.astype(BF16), ((1,), (0,)))
        st = state[h]
        yo = _dot(C, st.astype(BF16), ((1,), (1,))) * jnp.exp(A)
        alast = acs[Q - 1:Q, h:h + 1]
        U = xdt * jnp.exp(alast - A)
        new_states.append(jnp.exp(alast) * st + _dot(U.astype(BF16), B, ((0,), (0,))))
        ys.append(yd + yo + d_skip[:, h:h + 1] * x)
    return jnp.concatenate(ys, axis=1), jnp.stack(new_states)


def _ssd_fwd(xact, proj, svec, name):
    S = xact.shape[0]
    nc = S // CHUNK

    def body(x_ref, dt_ref, v_ref, y_ref, st_ref, state):
        @pl.when(pl.program_id(0) == 0)
        def _():
            state[...] = jnp.zeros_like(state)

        st_ref[0] = state[...]
        x = x_ref[...]
        y, sn = fn_ssd_chunk(x[:, 0:512], x[:, 512:768], x[:, 768:1024], dt_ref[...], state[...], v_ref[...])
        y_ref[...] = y
        state[...] = sn

    return pl.pallas_call(
        body, name=name, grid=(nc,),
        in_specs=[pl.BlockSpec((CHUNK, D_CONV), lambda i: (i, 0)), pl.BlockSpec((CHUNK, LANES), lambda i: (i, 16)),
                  pl.BlockSpec((8, LANES), lambda i: (0, 0))],
        out_specs=[pl.BlockSpec((CHUNK, D_SSD), lambda i: (i, 0)),
                   pl.BlockSpec((1, SSD_HEADS, SSD_P, SSD_N), lambda i: (i, 0, 0, 0))],
        out_shape=[jax.ShapeDtypeStruct((S, D_SSD), F32), jax.ShapeDtypeStruct((nc, SSD_HEADS, SSD_P, SSD_N), F32)],
        scratch_shapes=[pltpu.VMEM((SSD_HEADS, SSD_P, SSD_N), F32)],
        compiler_params=_cp(("arbitrary",)),
    )(xact, proj, svec)


def _ssd_bwd(xact, proj, svec, states, dy, name):
    S = xact.shape[0]
    nc = S // CHUNK

    def body(x_ref, dt_ref, v_ref, st_ref, dy_ref, dx_ref, ddt_ref, dv_ref, dstate):
        @pl.when(pl.program_id(0) == 0)
        def _():
            dstate[...] = jnp.zeros_like(dstate)
            dv_ref[...] = jnp.zeros_like(dv_ref)

        x = x_ref[...]
        _, vjp = jax.vjp(fn_ssd_chunk, x[:, 0:512], x[:, 512:768], x[:, 768:1024], dt_ref[...], st_ref[0], v_ref[...])
        dxs, dbm, dcm, ddt, dst, dvec = vjp((dy_ref[...], dstate[...]))
        dx_ref[:, 0:512] = dxs
        dx_ref[:, 512:768] = dbm
        dx_ref[:, 768:1024] = dcm
        ddt_ref[...] = ddt
        dstate[...] = dst
        dv_ref[...] += dvec

    rev = lambda i: (nc - 1 - i, 0)
    return pl.pallas_call(
        body, name=name, grid=(nc,),
        in_specs=[pl.BlockSpec((CHUNK, D_CONV), rev), pl.BlockSpec((CHUNK, LANES), lambda i: (nc - 1 - i, 16)),
                  pl.BlockSpec((8, LANES), lambda i: (0, 0)),
                  pl.BlockSpec((1, SSD_HEADS, SSD_P, SSD_N), lambda i: (nc - 1 - i, 0, 0, 0)),
                  pl.BlockSpec((CHUNK, D_SSD), rev)],
        out_specs=[pl.BlockSpec((CHUNK, D_CONV), rev), pl.BlockSpec((CHUNK, LANES), rev),
                   pl.BlockSpec((8, LANES), lambda i: (0, 0))],
        out_shape=[jax.ShapeDtypeStruct((S, D_CONV), F32), jax.ShapeDtypeStruct((S, LANES), F32),
                   jax.ShapeDtypeStruct((8, LANES), F32)],
        scratch_shapes=[pltpu.VMEM((SSD_HEADS, SSD_P, SSD_N), F32)],
        compiler_params=_cp(("arbitrary",)),
    )(xact, proj, svec, states, dy)


def _ada_fwd(c_all, w_ada, b_sh, name):
    nb = 1536 // 512

    def body(c_ref, w_ref, b_ref, o_ref):
        ca = jax.nn.silu(c_ref[...]).astype(BF16)
        o_ref[0] = _dot(ca, w_ref[0].astype(BF16), ((1,), (0,))) + b_ref[0]

    return pl.pallas_call(
        body, name=name, grid=(DEPTH, nb),
        in_specs=[pl.BlockSpec((8, D_MODEL), lambda l, j: (0, 0)), pl.BlockSpec((1, D_MODEL, 512), lambda l, j: (l, 0, j)),
                  pl.BlockSpec((1, 1, 512), lambda l, j: (l, 0, j))],
        out_specs=pl.BlockSpec((1, 8, 512), lambda l, j: (l, 0, j)),
        out_shape=jax.ShapeDtypeStruct((DEPTH, 8, 1536), F32),
        compiler_params=_cp(("parallel", "parallel")),
    )(c_all, w_ada, b_sh)


def _ada_bwd(c_all_t, dmod_sh, name):
    nb = 1536 // 512

    def body(c_ref, d_ref, o_ref):
        ca = jax.nn.silu(c_ref[...])
        acc = ca[:, 0:1] * d_ref[0, 0:1, :]
        for b in range(1, 8):
            acc = acc + ca[:, b:b + 1] * d_ref[0, b:b + 1, :]
        o_ref[0] = acc

    return pl.pallas_call(
        body, name=name, grid=(DEPTH, nb),
        in_specs=[pl.BlockSpec((D_MODEL, 8), lambda l, j: (0, 0)), pl.BlockSpec((1, 8, 512), lambda l, j: (l, 0, j))],
        out_specs=pl.BlockSpec((1, D_MODEL, 512), lambda l, j: (l, 0, j)),
        out_shape=jax.ShapeDtypeStruct((DEPTH, D_MODEL, 1536), F32),
        compiler_params=_cp(("parallel", "parallel")),
    )(c_all_t, dmod_sh)


def _rows_tile(rows):
    return next(t for t in (512, 256, 128, 64, 32, 16, 8) if rows % t == 0)


SUM_BLOCKS = 4
ADAM_BLOCKS = 8


def _sum_sibling(gs, ls, ci, name):
    n = len(gs)

    def body(c_ref, *refs):
        for p in range(n):
            refs[2 * n + p][...] = refs[2 * p][...] + refs[2 * p + 1][...]

    in_specs, out_specs, out_shape = [], [], []
    for g in gs:
        _, _, rh, cw = g.shape
        rb = rh // SUM_BLOCKS
        in_specs += [pl.BlockSpec((None, None, rb, cw), lambda s, i, c: (s, c[0], i, 0)),
                     pl.BlockSpec((None, rb, cw), lambda s, i, c: (s, i, 0))]
        out_specs.append(pl.BlockSpec((None, rb, cw), lambda s, i, c: (s, i, 0)))
        out_shape.append(jax.ShapeDtypeStruct((4, rh, cw), F32))
    ops = [a for pair in zip(gs, ls) for a in pair]
    return pl.pallas_call(
        body, name=name,
        grid_spec=pltpu.PrefetchScalarGridSpec(num_scalar_prefetch=1, grid=(4, SUM_BLOCKS), in_specs=in_specs, out_specs=out_specs),
        out_shape=out_shape, compiler_params=_cp(("parallel", "parallel")),
    )(ci.reshape(1).astype(jnp.int32), *ops)


def _sum_chips(cs, lands, chip, name):
    n = len(cs)

    def body(c_ref, *refs):
        for p in range(n):
            a = refs[4 * p:4 * p + 4]
            refs[4 * n + p][...] = ((a[0][...] + a[1][...]) + a[2][...]) + a[3][...]

    in_specs, out_specs, out_shape = [], [], []
    for c in cs:
        _, rh, cw = c.shape
        rb = rh // SUM_BLOCKS
        in_specs.append(pl.BlockSpec((None, rb, cw), lambda i, ch: (ch[0], i, 0)))
        in_specs += [pl.BlockSpec((None, rb, cw), functools.partial(lambda i, ch, k: (k, i, 0), k=k)) for k in range(3)]
        out_specs.append(pl.BlockSpec((rb, cw), lambda i, ch: (i, 0)))
        out_shape.append(jax.ShapeDtypeStruct((rh, cw), F32))
    ops = [a for c, l in zip(cs, lands) for a in (c, l, l, l)]
    return pl.pallas_call(
        body, name=name,
        grid_spec=pltpu.PrefetchScalarGridSpec(num_scalar_prefetch=1, grid=(SUM_BLOCKS,), in_specs=in_specs, out_specs=out_specs),
        out_shape=out_shape, compiler_params=_cp(("parallel",)),
    )(chip.reshape(1).astype(jnp.int32), *ops)


def _adam_update(w, m, v, g):
    c1 = 1.0 / (1.0 - ADAM_B1 ** ADAM_STEP)
    c2 = 1.0 / (1.0 - ADAM_B2 ** ADAM_STEP)
    nm = ADAM_B1 * m + (1.0 - ADAM_B1) * g
    nv = ADAM_B2 * v + (1.0 - ADAM_B2) * (g * g)
    return -ADAM_LR * ((nm * c1) / (jnp.sqrt(nv * c2) + ADAM_EPS) + ADAM_WD * w), nm, nv


def _adam_multi(ws, ms, vs, gs, name):
    n = len(ws)

    def body(*refs):
        for p in range(n):
            d, nm, nv = _adam_update(*[refs[4 * p + k][...] for k in range(4)])
            refs[4 * n + 3 * p][...] = d
            refs[4 * n + 3 * p + 1][...] = nm
            refs[4 * n + 3 * p + 2][...] = nv

    in_specs, out_specs, out_shape = [], [], []
    for w in ws:
        _, r, cw = w.shape
        spec = pl.BlockSpec((None, r // ADAM_BLOCKS, cw), lambda l, i: (l, i, 0))
        in_specs += [spec] * 4
        out_specs += [spec] * 3
        out_shape += [jax.ShapeDtypeStruct(w.shape, F32)] * 3
    ops = [a for q in zip(ws, ms, vs, gs) for a in q]
    res = pl.pallas_call(
        body, name=name, grid=(DEPTH, ADAM_BLOCKS), in_specs=in_specs, out_specs=out_specs, out_shape=out_shape,
        compiler_params=_cp(("parallel", "parallel")),
    )(*ops)
    return res[0::3], res[1::3], res[2::3]


def _adam(w, m, v, parts, name):
    rows, width = w.shape
    bm = min(256, _rows_tile(rows))
    np_ = len(parts)
    c1 = 1.0 / (1.0 - ADAM_B1 ** ADAM_STEP)
    c2 = 1.0 / (1.0 - ADAM_B2 ** ADAM_STEP)

    def body(*refs):
        w_ref, m_ref, v_ref = refs[:3]
        g = refs[3][...]
        for r in refs[4:3 + np_]:
            g = g + r[...]
        g_ref, d_ref, nm_ref, nv_ref = refs[3 + np_:]
        nm = ADAM_B1 * m_ref[...] + (1.0 - ADAM_B1) * g
        nv = ADAM_B2 * v_ref[...] + (1.0 - ADAM_B2) * (g * g)
        g_ref[...] = g
        nm_ref[...] = nm
        nv_ref[...] = nv
        d_ref[...] = -ADAM_LR * ((nm * c1) / (jnp.sqrt(nv * c2) + ADAM_EPS) + ADAM_WD * w_ref[...])

    blk = pl.BlockSpec((bm, width), lambda i: (i, 0))
    return pl.pallas_call(
        body, name=name, grid=(rows // bm,),
        in_specs=[blk, blk, blk] + [pl.BlockSpec((bm, width), functools.partial(lambda i, o: (i + o, 0), o=off // bm))
                                    for (_, off) in parts],
        out_specs=[blk, blk, blk, blk],
        out_shape=[jax.ShapeDtypeStruct((rows, width), F32)] * 4,
        compiler_params=_cp(("parallel",)),
    )(w, m, v, *[p[0] for p in parts])


def _coords():
    return lax.axis_index("x"), lax.axis_index("y"), lax.axis_index("c")


def _other_chips(x, y):
    return [(1 - x, y), (x, 1 - y), (1 - x, 1 - y)]


def _ag8(blk, name):
    m_per, n = blk.shape

    def body(x_ref, out_ref, send_sems, recv_sems, local_sem):
        x, y, c = _coords()
        me, sibling = (x, y, c), (x, y, 1 - c)
        chips = _other_chips(x, y)

        def rows(px, py, pc):
            return out_ref.at[pl.ds((4 * px + 2 * py + pc) * m_per, m_per), :]

        def copy(k, block, to, src=None):
            return pltpu.make_async_remote_copy(
                src_ref=rows(*block) if src is None else src, dst_ref=rows(*block),
                send_sem=send_sems.at[k], recv_sem=recv_sems.at[k], device_id=to, device_id_type=MESH)

        mine = pltpu.make_async_copy(x_ref, rows(*me), local_sem)
        mine.start()
        first = [copy(0, me, sibling, src=x_ref)]
        first += [copy(1 + j, me, (*chip, c), src=x_ref) for j, chip in enumerate(chips)]
        for cp in first:
            cp.start()
        passed = [copy(4 + j, (*chip, c), sibling) for j, chip in enumerate(chips)]
        for j, chip in enumerate(chips):
            copy(1 + j, (*chip, c), me).wait_recv()
            passed[j].start()
        copy(0, sibling, me).wait_recv()
        for j, chip in enumerate(chips):
            copy(4 + j, (*chip, 1 - c), me).wait_recv()
        for cp in first + passed:
            cp.wait_send()
        mine.wait()

    return pl.pallas_call(
        body, name=name,
        out_shape=jax.ShapeDtypeStruct((8 * m_per, n), blk.dtype),
        in_specs=[pl.BlockSpec(memory_space=pltpu.VMEM)], out_specs=pl.BlockSpec(memory_space=pltpu.VMEM),
        scratch_shapes=[pltpu.SemaphoreType.DMA((7,)), pltpu.SemaphoreType.DMA((7,)), pltpu.SemaphoreType.DMA],
    )(blk)


HBM_SPEC = pl.BlockSpec(memory_space=pltpu.HBM)
SEM_SPEC = pl.BlockSpec(memory_space=pltpu.SEMAPHORE)
EFFECT = pltpu.SideEffectType.DATAFLOW_SIDE_EFFECTING


def _remote(src, dst, send_sem, recv_sem, to):
    return pltpu.make_async_remote_copy(src_ref=src, dst_ref=dst, send_sem=send_sem, recv_sem=recv_sem,
                                        device_id=to, device_id_type=MESH)


def _ag_list(shards, name):
    n = len(shards)

    def body(*refs):
        sh, out = refs[:n], refs[n:2 * n]
        send_sems, recv_sems = refs[2 * n:]
        x, y, c = _coords()
        sibling = (x, y, 1 - c)
        chips = _other_chips(x, y)
        first = [_remote(sh[p].at[c], out[p].at[2 * x + y, c], send_sems.at[6 * p + j], recv_sems.at[6 * p + j], (px, py, c))
                 for p in range(n) for j, (px, py) in enumerate(chips)]
        for cp in first:
            cp.start()
        passed = []
        for j, (px, py) in enumerate(chips):
            for p in range(n):
                got = out[p].at[2 * px + py, c]
                _remote(got, got, send_sems.at[6 * p + j], recv_sems.at[6 * p + j], (x, y, c)).wait_recv()
                cp = _remote(got, got, send_sems.at[6 * p + 3 + j], recv_sems.at[6 * p + 3 + j], sibling)
                cp.start()
                passed.append(cp)
        for j, (px, py) in enumerate(chips):
            for p in range(n):
                got = out[p].at[2 * px + py, 1 - c]
                _remote(got, got, send_sems.at[6 * p + 3 + j], recv_sems.at[6 * p + 3 + j], (x, y, c)).wait_recv()
        for cp in first + passed:
            cp.wait_send()

    return pl.pallas_call(
        body, name=name,
        out_shape=[jax.ShapeDtypeStruct((4,) + s.shape, s.dtype) for s in shards],
        in_specs=[pl.BlockSpec(memory_space=pl.ANY)] * n, out_specs=[pl.BlockSpec(memory_space=pl.ANY)] * n,
        scratch_shapes=[pltpu.SemaphoreType.DMA((6 * n,)), pltpu.SemaphoreType.DMA((6 * n,))],
    )(*shards)


def _ag_direct_copies(sh, land, send_sems, recv_sems, starting):
    x, y, c = _coords()
    return [_remote(sh[p], land[p].at[2 * x + y] if starting else land[p].at[2 * px + py],
                    send_sems.at[3 * p + j], recv_sems.at[3 * p + j], (px, py, c))
            for p in range(len(sh)) for j, (px, py) in enumerate(_other_chips(x, y))]


def _ag_direct_start(shards, name):
    n = len(shards)

    def body(*refs):
        for cp in _ag_direct_copies(refs[:n], refs[n:2 * n], refs[2 * n], refs[2 * n + 1], True):
            cp.start()
        token = refs[4 * n + 2]
        token[...] = jnp.zeros_like(token)

    lands = [pltpu.with_memory_space_constraint(lax.empty((4,) + s.shape, s.dtype), pltpu.HBM) for s in shards]
    res = pl.pallas_call(
        body, name=name,
        out_shape=(pltpu.SemaphoreType.DMA((3 * n,)), pltpu.SemaphoreType.DMA((3 * n,)))
        + tuple(pltpu.HBM(s.shape, s.dtype) for s in shards) + tuple(pltpu.HBM(l.shape, l.dtype) for l in lands)
        + (jax.ShapeDtypeStruct((8, LANES), F32),),
        in_specs=(HBM_SPEC,) * (2 * n), out_specs=(SEM_SPEC, SEM_SPEC) + (HBM_SPEC,) * (2 * n) + (pl.BlockSpec(memory_space=pltpu.VMEM),),
        input_output_aliases={i: 2 + i for i in range(2 * n)},
        compiler_params=pltpu.CompilerParams(has_side_effects=EFFECT),
    )(*[pltpu.with_memory_space_constraint(s, pltpu.HBM) for s in shards], *lands)
    return res[0], res[1], res[2:2 + n], res[2 + n:2 + 2 * n], res[2 + 2 * n]


def _ag_direct_wait(send_sems, recv_sems, sh_thru, land_thru, after, name):
    n = len(sh_thru)

    def body(*refs):
        sh, land = refs[:n], refs[n:2 * n]
        for cp in _ag_direct_copies(sh, land, refs[2 * n], refs[2 * n + 1], False):
            cp.wait_send()
            cp.wait_recv()

    res = pl.pallas_call(
        body, name=name,
        out_shape=tuple(pltpu.HBM(s.shape, s.dtype) for s in sh_thru) + tuple(pltpu.HBM(l.shape, l.dtype) for l in land_thru),
        in_specs=(HBM_SPEC,) * (2 * n) + (SEM_SPEC, SEM_SPEC, pl.BlockSpec(memory_space=pl.ANY)),
        out_specs=(HBM_SPEC,) * (2 * n), input_output_aliases={i: i for i in range(2 * n)},
        compiler_params=pltpu.CompilerParams(has_side_effects=EFFECT),
    )(*sh_thru, *land_thru, send_sems, recv_sems, after)
    return res[n:]


def _rs_sibling_list(gs, name):
    n = len(gs)

    def body(*refs):
        g, out, send_sems, recv_sems = refs[:n], refs[n:2 * n], refs[2 * n], refs[2 * n + 1]
        x, y, c = _coords()
        cps = [_remote(g[p].at[s, 1 - c], out[p].at[s], send_sems.at[4 * p + s], recv_sems.at[4 * p + s], (x, y, 1 - c))
               for p in range(n) for s in range(4)]
        for cp in cps:
            cp.start()
        for cp in cps:
            cp.wait_recv()
        for cp in cps:
            cp.wait_send()

    return pl.pallas_call(
        body, name=name,
        out_shape=[jax.ShapeDtypeStruct((4,) + g.shape[2:], g.dtype) for g in gs],
        in_specs=[pl.BlockSpec(memory_space=pl.ANY)] * n, out_specs=[pl.BlockSpec(memory_space=pl.ANY)] * n,
        scratch_shapes=[pltpu.SemaphoreType.DMA((4 * n,)), pltpu.SemaphoreType.DMA((4 * n,))],
    )(*gs)


def _rs_chips_copies(cs, land, send_sems, recv_sems):
    x, y, c = _coords()
    return [_remote(cs[p].at[2 * px + py], land[p].at[j], send_sems.at[3 * p + j], recv_sems.at[3 * p + j], (px, py, c))
            for p in range(len(cs)) for j, (px, py) in enumerate(_other_chips(x, y))]


def _rs_chips_start(cs, name):
    n = len(cs)

    def body(*refs):
        for cp in _rs_chips_copies(refs[:n], refs[n:2 * n], refs[2 * n], refs[2 * n + 1]):
            cp.start()
        token = refs[4 * n + 2]
        token[...] = jnp.zeros_like(token)

    lands = [pltpu.with_memory_space_constraint(lax.empty((3,) + c.shape[1:], c.dtype), pltpu.HBM) for c in cs]
    res = pl.pallas_call(
        body, name=name,
        out_shape=(pltpu.SemaphoreType.DMA((3 * n,)), pltpu.SemaphoreType.DMA((3 * n,)))
        + tuple(pltpu.HBM(c.shape, c.dtype) for c in cs) + tuple(pltpu.HBM(l.shape, l.dtype) for l in lands)
        + (jax.ShapeDtypeStruct((8, LANES), F32),),
        in_specs=(HBM_SPEC,) * (2 * n), out_specs=(SEM_SPEC, SEM_SPEC) + (HBM_SPEC,) * (2 * n) + (pl.BlockSpec(memory_space=pltpu.VMEM),),
        input_output_aliases={i: 2 + i for i in range(2 * n)},
        compiler_params=pltpu.CompilerParams(has_side_effects=EFFECT),
    )(*[pltpu.with_memory_space_constraint(c, pltpu.HBM) for c in cs], *lands)
    return res[0], res[1], res[2:2 + n], res[2 + n:2 + 2 * n], res[2 + 2 * n]


def _rs_chips_wait(send_sems, recv_sems, cs_thru, land_thru, after, name):
    n = len(cs_thru)

    def body(*refs):
        for cp in _rs_chips_copies(refs[:n], refs[n:2 * n], refs[2 * n], refs[2 * n + 1]):
            cp.wait_send()
            cp.wait_recv()

    res = pl.pallas_call(
        body, name=name,
        out_shape=tuple(pltpu.HBM(c.shape, c.dtype) for c in cs_thru) + tuple(pltpu.HBM(l.shape, l.dtype) for l in land_thru),
        in_specs=(HBM_SPEC,) * (2 * n) + (SEM_SPEC, SEM_SPEC, pl.BlockSpec(memory_space=pl.ANY)),
        out_specs=(HBM_SPEC,) * (2 * n), input_output_aliases={i: i for i in range(2 * n)},
        compiler_params=pltpu.CompilerParams(has_side_effects=EFFECT),
    )(*cs_thru, *land_thru, send_sems, recv_sems, after)
    return res[:n], res[n:]


def _swap_list(ghs, name):
    n = len(ghs)

    def body(*refs):
        g, out, send_sems, recv_sems = refs[:n], refs[n:2 * n], refs[2 * n], refs[2 * n + 1]
        x, y, c = _coords()
        cps = [_remote(g[p], out[p], send_sems.at[p], recv_sems.at[p], (x, y, 1 - c)) for p in range(n)]
        for cp in cps:
            cp.start()
        for cp in cps:
            cp.wait_recv()
        for cp in cps:
            cp.wait_send()

    return pl.pallas_call(
        body, name=name,
        out_shape=[jax.ShapeDtypeStruct(g.shape, g.dtype) for g in ghs],
        in_specs=[pl.BlockSpec(memory_space=pl.ANY)] * n, out_specs=[pl.BlockSpec(memory_space=pl.ANY)] * n,
        scratch_shapes=[pltpu.SemaphoreType.DMA((n,)), pltpu.SemaphoreType.DMA((n,))],
    )(*ghs)


def _pad_win(w):
    return jnp.concatenate([w[:, :416], jnp.zeros((w.shape[0], 96), w.dtype), w[:, 416:1952],
                            w[:, 1952:1960], jnp.zeros((w.shape[0], 120), w.dtype)], axis=1)


def _unpad_win(g):
    return jnp.concatenate([g[:, :416], g[:, 512:2048], g[:, 2048:2056]], axis=1)


def _pad_wq(w):
    return jnp.pad(w.reshape(Q_LORA, HEADS, QK_DIM), ((0, 0), (0, 0), (0, LANES - QK_DIM))).reshape(Q_LORA, HEADS * LANES)


def _unpad_wq(g):
    return g.reshape(Q_LORA, HEADS, LANES)[:, :, :QK_DIM].reshape(Q_LORA, HEADS * QK_DIM)


def _cols_to_shards(a):
    r, c4 = a.shape
    return a.reshape(r, 4, c4 // 4).transpose(1, 0, 2)


def _shards_to_cols(a):
    _, r, c = a.shape
    return a.transpose(1, 0, 2).reshape(r, 4 * c)


def _pack_small(tree):
    parts = []
    for l in range(DEPTH):
        for (n, k) in SMALL:
            parts.append(jnp.pad(tree[n][l].reshape(-1), (0, -k % LANES)))
    flat = jnp.concatenate(parts)
    return jnp.pad(flat, (0, SMALL_ROWS * LANES - flat.shape[0])).reshape(SMALL_ROWS, LANES)


def _unpack_small(buf):
    flat = buf.reshape(-1)
    out = {n: [] for (n, _) in SMALL}
    o = 0
    for l in range(DEPTH):
        for (n, k) in SMALL:
            out[n].append(flat[o:o + k])
            o += k + (-k % LANES)
    return {n: jnp.stack(v) for n, v in out.items()}


def _vec(v, width=LANES):
    return jnp.pad(v.reshape(1, -1), ((0, 0), (0, width - v.shape[-1])))


def kernel(x, c, positions, norm1_w, norm2_w, w_ada, b_ada, w_in, q_a_norm_w, w_q_up, kv_a_norm_w, w_kv_up, q_nope_norm_w, q_pe_norm_w, k_nope_norm_w, k_pe_norm_w, conv_w, conv_b, dt_bias, a_log, d_skip, ssd_norm_w, w_out, w_gate_up, w_down, loss_target, m_norm1_w, m_norm2_w, m_w_ada, m_b_ada, m_w_in, m_q_a_norm_w, m_w_q_up, m_kv_a_norm_w, m_w_kv_up, m_q_nope_norm_w, m_q_pe_norm_w, m_k_nope_norm_w, m_k_pe_norm_w, m_conv_w, m_conv_b, m_dt_bias, m_a_log, m_d_skip, m_ssd_norm_w, m_w_out, m_w_gate_up, m_w_down, v_norm1_w, v_norm2_w, v_w_ada, v_b_ada, v_w_in, v_q_a_norm_w, v_w_q_up, v_kv_a_norm_w, v_w_kv_up, v_q_nope_norm_w, v_q_pe_norm_w, v_k_nope_norm_w, v_k_pe_norm_w, v_conv_w, v_conv_b, v_dt_bias, v_a_log, v_d_skip, v_ssd_norm_w, v_w_out, v_w_gate_up, v_w_down):
    W = dict(zip(WEIGHTS, (norm1_w, norm2_w, w_ada, b_ada, w_in, q_a_norm_w, w_q_up, kv_a_norm_w, w_kv_up, q_nope_norm_w, q_pe_norm_w, k_nope_norm_w, k_pe_norm_w, conv_w, conv_b, dt_bias, a_log, d_skip, ssd_norm_w, w_out, w_gate_up, w_down)))
    M = dict(zip(WEIGHTS, (m_norm1_w, m_norm2_w, m_w_ada, m_b_ada, m_w_in, m_q_a_norm_w, m_w_q_up, m_kv_a_norm_w, m_w_kv_up, m_q_nope_norm_w, m_q_pe_norm_w, m_k_nope_norm_w, m_k_pe_norm_w, m_conv_w, m_conv_b, m_dt_bias, m_a_log, m_d_skip, m_ssd_norm_w, m_w_out, m_w_gate_up, m_w_down)))
    V = dict(zip(WEIGHTS, (v_norm1_w, v_norm2_w, v_w_ada, v_b_ada, v_w_in, v_q_a_norm_w, v_w_q_up, v_kv_a_norm_w, v_w_kv_up, v_q_nope_norm_w, v_q_pe_norm_w, v_k_nope_norm_w, v_k_pe_norm_w, v_conv_w, v_conv_b, v_dt_bias, v_a_log, v_d_skip, v_ssd_norm_w, v_w_out, v_w_gate_up, v_w_down)))
    S = x.shape[1]
    xi, yi, ci = _coords()
    chip = 2 * xi + yi
    dev = 2 * chip + ci
    x0 = x[0]
    tgt = loss_target[0]

    inv_freq = 1.0 / (ROPE_THETA ** (jnp.arange(0, ROPE, 2, dtype=F32) / ROPE))
    ang = positions[0].astype(F32)[:, None] * inv_freq
    cos, sin = jnp.cos(ang), jnp.sin(ang)
    z16, z32, z64 = jnp.zeros((S, 16), F32), jnp.zeros((S, 32), F32), jnp.zeros((S, 64), F32)
    tab_c = jnp.concatenate([jnp.ones((S, 64), F32), cos, cos, z32], axis=1)
    tab_s1 = jnp.concatenate([z64, z16, sin, z32], axis=1)
    tab_s2 = jnp.concatenate([z64, -sin, z16, z32], axis=1)

    blk0 = jnp.concatenate([c.reshape(-1), W['conv_w'].reshape(-1)]).reshape(24, LANES)
    g0 = _ag8(blk0, "ag_c_conv").reshape(8, 24 * LANES)
    c_all = g0[:, :D_MODEL]
    conv_full = g0[0::2, D_MODEL:].reshape(4, DEPTH, CONV_TAPS, 256).transpose(1, 2, 0, 3).reshape(DEPTH, CONV_TAPS, D_CONV)

    sh = [[W[n][l].astype(BF16) for n in BIG] for l in range(DEPTH)]
    got0 = _ag_list([a.reshape(2, a.shape[0] // 2, a.shape[1]) for a in sh[0]], "ag_w0")
    ag1 = _ag_direct_start(sh[1], "ag_w1_start")

    def layer_weights(gathered, own):
        st = {n: lax.dynamic_update_slice_in_dim(a.reshape(4, -1, a.shape[-1]), o[None], chip, axis=0)
              for n, a, o in zip(BIG, gathered, own)}
        return dict(w_in=_pad_win(_shards_to_cols(st['w_in'])), w_q_up=_pad_wq(_shards_to_cols(st['w_q_up'])),
                    w_kv_up=_shards_to_cols(st['w_kv_up']), w_out=st['w_out'].reshape(D_MODEL, D_MODEL),
                    w_gate_up=st['w_gate_up'], w_down=st['w_down'].reshape(D_FF, D_MODEL))

    LW = [layer_weights(got0, sh[0]), None]

    b_sh = lax.dynamic_slice_in_dim(W['b_ada'], chip * 1536, 1536, axis=1).reshape(DEPTH, 1, 1536)
    mod_sh = _ada_fwd(c_all, W['w_ada'], b_sh, "ada_fwd")
    g1 = _ag8(mod_sh.reshape(192, LANES), "ag_mod").reshape(8, DEPTH, 8, 1536)
    mod_all = g1[0::2].transpose(1, 2, 0, 3).reshape(DEPTH, 8, 6 * D_MODEL)
    mod = lax.dynamic_index_in_dim(mod_all, dev, axis=1, keepdims=False)

    def mvec(l, k):
        return mod[l, k * D_MODEL:(k + 1) * D_MODEL].reshape(1, D_MODEL)

    def small(name, l, width=None):
        v = W[name][l]
        return _vec(v, width or v.shape[-1])

    def wq_vec(l):
        return _vec(jnp.concatenate([W['q_nope_norm_w'][l], W['q_pe_norm_w'][l]]))

    def wk_vec(l):
        return _vec(jnp.concatenate([W['k_nope_norm_w'][l], W['k_pe_norm_w'][l]]))

    def conv_vec(l):
        return jnp.concatenate([conv_full[l], W['conv_b'][l].reshape(1, D_CONV), jnp.zeros((3, D_CONV), F32)], axis=0)

    def ssd_vec(l):
        return jnp.concatenate([_vec(W['dt_bias'][l]), _vec(W['a_log'][l]), _vec(W['d_skip'][l]), jnp.zeros((5, LANES), F32)], axis=0)

    sv = []
    xcur = x0
    h1 = _row_fwd(fn_norm_mod, "norm_mod_f", [(x0, 0, D_MODEL)], [small('norm1_w', 0) + ag1[4][0, 0], mvec(0, 1), mvec(0, 0)],
                  [(D_MODEL, BF16)])[0]
    fin = None
    for l in range(DEPTH):
        if l == 1:
            LW[1] = layer_weights(_ag_direct_wait(ag1[0], ag1[1], ag1[2], ag1[3], xcur, "ag_w1_wait"), sh[1])
        lw = LW[l]
        t = dict(xcur=xcur, h1=h1)
        t['proj'] = proj = _mm(h1, lw['w_in'], 'nn', f"mm_in_{l}")
        t['qa_n'], t['kva_n'] = _row_fwd(fn_lat_norm, f"lat_norm_f{l}", [(proj, 0, 256), (proj, 2, 128)],
                                         [small('q_a_norm_w', l), small('kv_a_norm_w', l)], [(256, BF16), (128, BF16)])
        t['q'] = _mm(t['qa_n'], lw['w_q_up'], 'nn', f"mm_q_{l}")
        t['kv'] = _mm(t['kva_n'], lw['w_kv_up'], 'nn', f"mm_kv_{l}")
        t['qf'], t['kf'], t['vv'] = _row_fwd(
            fn_qk_prep, f"qk_prep_f{l}",
            [(t['q'], 0, 1024), (t['kv'], 0, 1024), (proj, 3, 128), (tab_c, 0, 128), (tab_s1, 0, 128), (tab_s2, 0, 128)],
            [wq_vec(l), wk_vec(l)], [(1024, BF16), (1024, BF16), (1024, BF16)])
        t['ao'], t['lse'] = _attn_fwd(t['qf'], t['kf'], t['vv'], f"attn_f{l}")
        t['xact'] = _conv_fwd(proj, conv_vec(l), f"conv_f{l}")
        t['y'], t['states'] = _ssd_fwd(t['xact'], proj, ssd_vec(l), f"ssd_f{l}")
        t['mix'] = _row_fwd(fn_gated_mix, f"gated_f{l}", [(t['y'], 0, 512), (proj, 1, 512), (t['ao'], 0, 512)],
                            [small('ssd_norm_w', l)], [(1024, BF16)])[0]
        t['mo'] = _mm(t['mix'], lw['w_out'], 'nn', f"mm_out_{l}")
        t['x1'], t['h2'] = _row_fwd(fn_resid_norm, f"resid_mid_f{l}", [(xcur, 0, D_MODEL), (t['mo'], 0, D_MODEL)],
                                    [mvec(l, 2), small('norm2_w', l), mvec(l, 4), mvec(l, 3)],
                                    [(D_MODEL, F32), (D_MODEL, BF16)])
        t['gu'] = _mm(t['h2'], lw['w_gate_up'], 'nn', f"mm_gu_{l}", stack='b')
        t['act'] = _row_fwd(fn_swiglu, f"swiglu_f{l}", [(t['gu'], 0, 2 * D_FF)], [], [(D_FF, BF16)], tm=128)[0]
        t['ff'] = _mm(t['act'], lw['w_down'], 'nn', f"mm_down_{l}")
        if l + 1 < DEPTH:
            xcur, h1 = _row_fwd(fn_resid_norm, f"resid_end_f{l}", [(t['x1'], 0, D_MODEL), (t['ff'], 0, D_MODEL)],
                                [mvec(l, 5), small('norm1_w', l + 1), mvec(l + 1, 1), mvec(l + 1, 0)],
                                [(D_MODEL, F32), (D_MODEL, BF16)])
        else:
            fin = _final(t['x1'], t['ff'], tgt, mvec(l, 5), "final_loss")
        sv.append(t)

    dx1, dff, dg2_last, loss_acc = fin
    loss = lax.psum(loss_acc[0, 0], ("x", "y", "c"))
    gfull = {n: [None] * DEPTH for n in BIG}
    gsm = {n: [None] * DEPTH for (n, _) in SMALL}
    dmod = [[None] * 6 for _ in range(DEPTH)]
    dmod[DEPTH - 1][5] = dg2_last
    grad_x = None
    pending = []

    def rs_begin(l, names, tag):
        g4 = [gfull[n][l].reshape(4, 2, gfull[n][l].shape[1] // 2, gfull[n][l].shape[2]) for n in names]
        sib = _rs_sibling_list(g4, f"rs_sibling_{tag}")
        cs = _sum_sibling(g4, sib, ci, f"sum_sibling_{tag}")
        h = _rs_chips_start(cs, f"rs_chips_start_{tag}")
        pending.append((l, names, h))
        return h[4][0, 0]

    tie_l1 = tie_l0a = 0.0

    for l in reversed(range(DEPTH)):
        t = sv[l]
        lw = LW[l]
        proj = t['proj']
        dact = _mm(dff, lw['w_down'], 'nt', f"mm_down_dx{l}")
        gfull['w_down'][l] = _mm(t['act'], dff, 'tn', f"mm_down_dw{l}").reshape(4, D_FF // 4, D_MODEL)
        dgu = _row_bwd(fn_swiglu, f"swiglu_b{l}", [(t['gu'], 0, 2 * D_FF)], [], [(dact, 0, D_FF)], [0], [], tm=128)[0]
        dh2 = _mm(dgu, lw['w_gate_up'], 'nt', f"mm_gu_dx{l}", stack='b')
        gfull['w_gate_up'][l] = _mm(t['h2'], dgu, 'tn', f"mm_gu_dw{l}", stack='out')
        if l == 0:
            tie_l0a = rs_begin(0, EARLY, "l0a")
        dxc, dmo, dmod[l][2], gsm['norm2_w'][l], dmod[l][4], dmod[l][3] = _row_bwd(
            fn_resid_norm, f"resid_mid_b{l}", [(t['xcur'], 0, D_MODEL), (t['mo'], 0, D_MODEL)],
            [mvec(l, 2) + (tie_l1 if l == 0 else 0.0), small('norm2_w', l), mvec(l, 4), mvec(l, 3)],
            [(dx1, 0, D_MODEL), (dh2, 0, D_MODEL)], [0, 1], [0, 1, 2, 3])
        dmix = _mm(dmo, lw['w_out'], 'nt', f"mm_out_dx{l}")
        gfull['w_out'][l] = _mm(t['mix'], dmo, 'tn', f"mm_out_dw{l}").reshape(4, D_MODEL // 4, D_MODEL)
        dy, dz, gsm['ssd_norm_w'][l] = _row_bwd(fn_gated_norm, f"gated_b{l}", [(t['y'], 0, 512), (proj, 1, 512)],
                                                [small('ssd_norm_w', l)], [(dmix, 1, 512)], [0, 1], [0])
        dxact, ddt, dsv = _ssd_bwd(t['xact'], proj, ssd_vec(l) + (tie_l0a if l == 0 else 0.0), t['states'], dy, f"ssd_b{l}")
        gsm['dt_bias'][l], gsm['a_log'][l], gsm['d_skip'][l] = dsv[0, :8], dsv[1, :8], dsv[2, :8]
        dxbc, dcv = _conv_bwd(proj, conv_vec(l), dxact, f"conv_b{l}")
        gsm['conv_w'][l] = dcv[:CONV_TAPS]
        gsm['conv_b'][l] = dcv[CONV_TAPS]
        delta = _row_fwd(fn_attn_delta, f"attn_delta{l}", [(dmix, 0, 512), (t['ao'], 0, 512)], [], [(512, F32)])[0]
        lse_r = (t['lse'][:, ::64] * LOG2E).T.reshape(HEADS // 2, 2, S)
        delta_r = delta[:, ::64].T.reshape(HEADS // 2, 2, S)
        dqT, dkf, dvv = _attn_bwd(t['qf'], t['kf'], t['kf'].T, t['vv'], dmix, lse_r, delta_r, f"attn_b{l}")
        dqf = dqT.T
        dq, dkv, dkpe, dwq, dwk = _row_bwd(
            fn_qk_prep, f"qk_prep_b{l}",
            [(t['q'], 0, 1024), (t['kv'], 0, 1024), (proj, 3, 128), (tab_c, 0, 128), (tab_s1, 0, 128), (tab_s2, 0, 128)],
            [wq_vec(l), wk_vec(l)], [(dqf, 0, 1024), (dkf, 0, 1024), (dvv, 0, 1024)], [0, 1, 2], [0, 1])
        gsm['q_nope_norm_w'][l], gsm['q_pe_norm_w'][l] = dwq[0, :NOPE], dwq[0, NOPE:QK_DIM]
        gsm['k_nope_norm_w'][l], gsm['k_pe_norm_w'][l] = dwk[0, :NOPE], dwk[0, NOPE:QK_DIM]
        dqa_n = _mm(dq, lw['w_q_up'], 'nt', f"mm_q_dx{l}")
        gfull['w_q_up'][l] = _cols_to_shards(_unpad_wq(_mm(t['qa_n'], dq, 'tn', f"mm_q_dw{l}")))
        dkva_n = _mm(dkv, lw['w_kv_up'], 'nt', f"mm_kv_dx{l}")
        gfull['w_kv_up'][l] = _cols_to_shards(_mm(t['kva_n'], dkv, 'tn', f"mm_kv_dw{l}"))
        dqa, dkva, dqw, dkvw = _row_bwd(fn_lat_norm, f"lat_norm_b{l}", [(proj, 0, 256), (proj, 2, 128)],
                                        [small('q_a_norm_w', l), small('kv_a_norm_w', l)],
                                        [(dqa_n, 0, 256), (dkva_n, 0, 128)], [0, 1], [0, 1])
        gsm['q_a_norm_w'][l], gsm['kv_a_norm_w'][l] = dqw[0], dkvw[0]
        dproj = jnp.concatenate([dqa, dkva, dkpe, dz, dxbc, ddt], axis=1)
        dh1 = _mm(dproj, lw['w_in'], 'nt', f"mm_in_dx{l}")
        gfull['w_in'][l] = _cols_to_shards(_unpad_win(_mm(t['h1'], dproj, 'tn', f"mm_in_dw{l}")))
        if l > 0:
            p = sv[l - 1]
            dx1, dff, dmod[l - 1][5], gsm['norm1_w'][l], dmod[l][1], dmod[l][0] = _row_bwd(
                fn_resid_norm, f"resid_end_b{l - 1}", [(p['x1'], 0, D_MODEL), (p['ff'], 0, D_MODEL)],
                [mvec(l - 1, 5), small('norm1_w', l), mvec(l, 1), mvec(l, 0)], [(dxc, 0, D_MODEL), (dh1, 0, D_MODEL)],
                [0, 1], [0, 1, 2, 3])
            tie_l1 = rs_begin(l, BIG, f"l{l}")
        else:
            grad_x, gsm['norm1_w'][l], dmod[l][1], dmod[l][0] = _row_bwd(
                fn_norm_mod_pass, "norm_mod_b", [(x0, 0, D_MODEL)], [small('norm1_w', 0), mvec(0, 1), mvec(0, 0)],
                [(dxc, 0, D_MODEL), (dh1, 0, D_MODEL)], [0], [0, 1, 2])
        for n in ('norm1_w', 'norm2_w', 'ssd_norm_w'):
            gsm[n][l] = gsm[n][l][0]

    for l in range(DEPTH):
        gsm['b_ada'][l] = jnp.concatenate([d[0] for d in dmod[l]])
    sm_part = _pack_small({n: jnp.stack(v) for n, v in gsm.items()})
    sm_all = _ag8(sm_part + rs_begin(0, BIG[2:], "l0b"), "ag_small")

    def with_conv(tree):
        wide = lax.dynamic_update_slice_in_dim(jnp.zeros((DEPTH, CONV_TAPS, D_CONV), F32), tree['conv_w'], chip * 256, axis=2)
        return {**tree, 'conv_w': wide}

    g_sm, d_sm, m_sm, v_sm = _adam(_pack_small(with_conv(W)), _pack_small(with_conv(M)), _pack_small(with_conv(V)),
                                   [(sm_all, d * SMALL_ROWS) for d in range(8)], "adam_small")
    out_small = [_unpack_small(b) for b in (g_sm, d_sm, m_sm, v_sm)]
    for o in out_small:
        o['conv_w'] = lax.dynamic_slice_in_dim(o['conv_w'].reshape(DEPTH, CONV_TAPS, D_CONV), chip * 256, 256, axis=2)

    dmod_all = sm_all.reshape(8, SMALL_ROWS * LANES)
    per_layer = sum(k + (-k % LANES) for (_, k) in SMALL)
    dmod_sh = jnp.stack([lax.dynamic_slice_in_dim(dmod_all[:, l * per_layer:l * per_layer + 6 * D_MODEL], chip * 1536, 1536, axis=1)
                         for l in range(DEPTH)])
    g_ada = _ada_bwd(c_all.T, dmod_sh, "ada_bwd")
    ada = _adam(W['w_ada'].reshape(DEPTH * D_MODEL, 1536), M['w_ada'].reshape(DEPTH * D_MODEL, 1536),
                V['w_ada'].reshape(DEPTH * D_MODEL, 1536), [(g_ada.reshape(DEPTH * D_MODEL, 1536), 0)], "adam_ada")
    out_ada = [a.reshape(DEPTH, D_MODEL, 1536) for a in ada]

    keys, cs_all, land_all = [], [], []
    for (l, names, (send_sems, recv_sems, cs_thru, land_thru, _)) in pending:
        cs, lands = _rs_chips_wait(send_sems, recv_sems, cs_thru, land_thru, ada[3], f"rs_chips_wait_l{l}{len(names)}")
        keys += [(l, n) for n in names]
        cs_all += list(cs)
        land_all += list(lands)
    ghalf = _sum_chips(cs_all, land_all, chip, "sum_chips")
    gother = _swap_list(ghalf, "swap_halves")
    gshard = {k: jnp.where(ci == 0, jnp.concatenate([a, b]), jnp.concatenate([b, a])) for k, a, b in zip(keys, ghalf, gother)}
    g_big = [jnp.stack([gshard[(l, n)] for l in range(DEPTH)]) for n in BIG]
    d_big, m_big, v_big = _adam_multi([W[n] for n in BIG], [M[n] for n in BIG], [V[n] for n in BIG], g_big, "adam_big")
    out_big = [dict(zip(BIG, o)) for o in (g_big, d_big, m_big, v_big)]

    outs = [loss, grad_x[None]]
    for k in range(4):
        for n in WEIGHTS:
            if n == 'w_ada':
                outs.append(out_ada[k])
            elif n in BIG:
                outs.append(out_big[k][n])
            else:
                outs.append(out_small[k][n])
    return tuple(outs)
```

```python
import functools

import jax
import jax.numpy as jnp
from jax import lax
from jax.experimental import pallas as pl
from jax.experimental.pallas import tpu as pltpu

F32 = jnp.float32
BF16 = jnp.bfloat16
MESH = pl.DeviceIdType.MESH

D_MODEL = 1024
DEPTH = 2
HEADS = 8
NOPE = 64
ROPE = 32
QK_DIM = NOPE + ROPE
Q_LORA = 256
KV_LORA = 128
SSD_HEADS = 8
SSD_P = 64
SSD_N = 128
CHUNK = 128
CONV_TAPS = 4
D_SSD = 512
D_CONV = 1024
D_FF = 2816
D_IN = 1960
D_IN_PAD = 2176
EPS = 1e-6
ROPE_THETA = 10000.0
ATT_SCALE = QK_DIM ** -0.5
NEG = -1e30
LANES = 128
VMEM_LIMIT = 48 * 1024 * 1024

ADAM_LR, ADAM_B1, ADAM_B2, ADAM_EPS, ADAM_WD, ADAM_STEP = 0.001, 0.9, 0.999, 1e-08, 0.01, 10

WEIGHTS = ['norm1_w', 'norm2_w', 'w_ada', 'b_ada', 'w_in', 'q_a_norm_w', 'w_q_up', 'kv_a_norm_w', 'w_kv_up',
           'q_nope_norm_w', 'q_pe_norm_w', 'k_nope_norm_w', 'k_pe_norm_w', 'conv_w', 'conv_b', 'dt_bias', 'a_log',
           'd_skip', 'ssd_norm_w', 'w_out', 'w_gate_up', 'w_down']
BIG = ['w_down', 'w_gate_up', 'w_out', 'w_kv_up', 'w_q_up', 'w_in']
EARLY = BIG[:2]
SMALL = [('b_ada', 6144), ('conv_w', 4096), ('norm1_w', 1024), ('norm2_w', 1024), ('conv_b', 1024), ('ssd_norm_w', 512),
         ('q_a_norm_w', 256), ('kv_a_norm_w', 128), ('q_nope_norm_w', 64), ('q_pe_norm_w', 32),
         ('k_nope_norm_w', 64), ('k_pe_norm_w', 32), ('dt_bias', 8), ('a_log', 8), ('d_skip', 8)]
SMALL_ROWS = 240


def _cp(sem=None, **kw):
    return pltpu.CompilerParams(dimension_semantics=sem, vmem_limit_bytes=VMEM_LIMIT, **kw)


def _dot(a, b, dims, prec=None):
    return lax.dot_general(a, b, (dims, ((), ())), preferred_element_type=F32, precision=prec)


def _tile(dim, target):
    best = 0
    for t in range(LANES, min(dim, target) + 1, LANES):
        if dim % t == 0:
            best = t
    if best < 256 and dim <= 2304:
        return dim
    return best


def _mm(a, b, mode, name, out_dtype=F32, stack=None):
    ns = None
    if stack == 'b':
        ns = b.shape[2]
        if mode == 'nn':
            (M, K), N = a.shape, 4 * ns
        else:
            (M, K), N = a.shape, b.shape[1]
    elif mode == 'nn':
        (M, K), (_, N) = a.shape, b.shape
    elif mode == 'nt':
        (M, K), (N, _) = a.shape, b.shape
    else:
        (K, M), (_, N) = a.shape, b.shape
    if stack == 'out':
        ns = N // 4
    tm, tn, tk = _tile(M, 512), _tile(N, 1408), _tile(K, 1408)
    if stack == 'b' and mode == 'nt':
        tk = ns
    elif stack is not None:
        tn = ns
    nk = K // tk
    dims = {'nn': ((1,), (0,)), 'nt': ((1,), (1,)), 'tn': ((0,), (0,))}[mode]

    def body(a_ref, b_ref, o_ref, acc):
        k = pl.program_id(2)

        @pl.when(k == 0)
        def _():
            acc[...] = jnp.zeros_like(acc)

        acc[...] += _dot(a_ref[...].astype(BF16), b_ref[...].astype(BF16), dims)

        @pl.when(k == nk - 1)
        def _():
            o_ref[...] = acc[...].astype(o_ref.dtype)

    a_spec = pl.BlockSpec((tk, tm), lambda i, j, k: (k, i)) if mode == 'tn' else pl.BlockSpec((tm, tk), lambda i, j, k: (i, k))
    b_spec = pl.BlockSpec((tn, tk), lambda i, j, k: (j, k)) if mode == 'nt' else pl.BlockSpec((tk, tn), lambda i, j, k: (k, j))
    o_spec, o_shape = pl.BlockSpec((tm, tn), lambda i, j, k: (i, j)), (M, N)
    if stack == 'b':
        b_spec = (pl.BlockSpec((None, tn, ns), lambda i, j, k: (k, j, 0)) if mode == 'nt'
                  else pl.BlockSpec((None, tk, ns), lambda i, j, k: (j, k, 0)))
    if stack == 'out':
        o_spec, o_shape = pl.BlockSpec((None, tm, ns), lambda i, j, k: (j, i, 0)), (4, M, ns)
    return pl.pallas_call(
        body, name=name, grid=(M // tm, N // tn, nk),
        in_specs=[a_spec, b_spec], out_specs=o_spec,
        out_shape=jax.ShapeDtypeStruct(o_shape, out_dtype),
        scratch_shapes=[pltpu.VMEM((tm, tn), F32)],
        compiler_params=_cp(("parallel", "parallel", "arbitrary")),
    )(a, b)


def _rspec(tm, w, cb):
    return pl.BlockSpec((tm, w), lambda i: (i, cb))


def _vspec(shape):
    return pl.BlockSpec(shape, lambda i: (0,) * len(shape))


def _row_fwd(fn, name, rows, vecs, outs, tm=256):
    S = rows[0][0].shape[0]
    tm = min(tm, S)
    nin = len(rows) + len(vecs)

    def body(*refs):
        res = fn(*[r[...] for r in refs[:nin]])
        for o_ref, r in zip(refs[nin:], res):
            o_ref[...] = r.astype(o_ref.dtype)

    return pl.pallas_call(
        body, name=name, grid=(S // tm,),
        in_specs=[_rspec(tm, w, cb) for (_, cb, w) in rows] + [_vspec(v.shape) for v in vecs],
        out_specs=[_rspec(tm, w, 0) for (w, _) in outs],
        out_shape=[jax.ShapeDtypeStruct((S, w), dt) for (w, dt) in outs],
        compiler_params=_cp(("parallel",)),
    )(*[r[0] for r in rows], *vecs)


def _row_bwd(fn, name, rows, vecs, cts, drows, dvecs, tm=256):
    S = rows[0][0].shape[0]
    tm = min(tm, S)
    nr, nv, nc = len(rows), len(vecs), len(cts)
    didx = list(drows) + [nr + j for j in dvecs]

    def body(*refs):
        vals = [r[...] for r in refs[:nr + nv]]
        ct = tuple(r[...].astype(F32) for r in refs[nr + nv:nr + nv + nc])
        outs = refs[nr + nv + nc:]

        def g(*d):
            a = list(vals)
            for k, val in zip(didx, d):
                a[k] = val
            return tuple(fn(*a))

        _, vjp = jax.vjp(g, *[vals[k] for k in didx])
        grads = vjp(ct)
        for o, gr in zip(outs[:len(drows)], grads[:len(drows)]):
            o[...] = gr.astype(o.dtype)

        @pl.when(pl.program_id(0) == 0)
        def _():
            for o in outs[len(drows):]:
                o[...] = jnp.zeros_like(o)

        for o, gr in zip(outs[len(drows):], grads[len(drows):]):
            o[...] += gr

    return pl.pallas_call(
        body, name=name, grid=(S // tm,),
        in_specs=[_rspec(tm, w, cb) for (_, cb, w) in rows] + [_vspec(v.shape) for v in vecs]
        + [_rspec(tm, w, cb) for (_, cb, w) in cts],
        out_specs=[_rspec(tm, rows[k][2], 0) for k in drows] + [_vspec(vecs[j].shape) for j in dvecs],
        out_shape=[jax.ShapeDtypeStruct((S, rows[k][2]), F32) for k in drows]
        + [jax.ShapeDtypeStruct(vecs[j].shape, F32) for j in dvecs],
        compiler_params=_cp(("arbitrary",)),
    )(*[r[0] for r in rows], *vecs, *[c[0] for c in cts])


def _rms(x):
    return x * lax.rsqrt(jnp.mean(x * x, axis=-1, keepdims=True) + EPS)


def fn_norm_mod(x, nw, sc, sh):
    return (_rms(x) * nw * (1.0 + sc) + sh,)


def fn_norm_mod_pass(x, nw, sc, sh):
    return (x, _rms(x) * nw * (1.0 + sc) + sh)


def fn_resid_norm(x, d, g, nw, sc, sh):
    xn = x + g * d
    return (xn, _rms(xn) * nw * (1.0 + sc) + sh)


def fn_lat_norm(qa, kva, qw, kvw):
    return (_rms(qa) * qw, _rms(kva) * kvw)


@functools.partial(jax.custom_vjp, nondiff_argnums=(1,))
def _lroll(x, s):
    return pltpu.roll(x, s, 1)


def _lroll_fwd(x, s):
    return pltpu.roll(x, s, 1), None


def _lroll_bwd(s, _, g):
    return (pltpu.roll(g, (LANES - s) % LANES, 1),)


_lroll.defvjp(_lroll_fwd, _lroll_bwd)


def _lane_masks(shape):
    lane = lax.broadcasted_iota(jnp.int32, shape, 1)
    return (lane < NOPE).astype(F32), ((lane >= NOPE) & (lane < QK_DIM)).astype(F32)


def _rope(t, tc, ts1, ts2):
    return t * tc + _lroll(t, 16) * ts1 + _lroll(t, LANES - 16) * ts2


def fn_qk_prep(q, kv, kpe, tc, ts1, ts2, wq, wk):
    mn, mp = _lane_masks((1, LANES))
    mhi = 1.0 - mn

    def head_norm(t, w):
        rn = lax.rsqrt(jnp.sum(t * t * mn, axis=-1, keepdims=True) * (1.0 / NOPE) + EPS)
        rp = lax.rsqrt(jnp.sum(t * t * mp, axis=-1, keepdims=True) * (1.0 / ROPE) + EPS)
        return t * (rn * mn + rp * mp) * w

    kp = _rope(head_norm(_lroll(kpe, NOPE), wk) * mp, tc, ts1, ts2)
    qs, ks, vs = [], [], []
    for h in range(HEADS):
        qs.append(_rope(head_norm(q[:, h * LANES:(h + 1) * LANES], wq), tc, ts1, ts2))
        t = kv[:, h * LANES:(h + 1) * LANES]
        ks.append(head_norm(t, wk) * mn + kp)
        vs.append(_lroll(t, NOPE) * mn + mhi)
    return (jnp.concatenate(qs, axis=1), jnp.concatenate(ks, axis=1), jnp.concatenate(vs, axis=1))


def fn_attn_delta(do, o):
    mn, _ = _lane_masks((1, LANES))
    mhi = 1.0 - mn
    out = []
    for hp in range(HEADS // 2):
        y = do[:, hp * LANES:(hp + 1) * LANES] * o[:, hp * LANES:(hp + 1) * LANES]
        out.append(jnp.sum(y * mn, axis=-1, keepdims=True) * mn + jnp.sum(y * mhi, axis=-1, keepdims=True) * mhi)
    return (jnp.concatenate(out, axis=1),)


def fn_gated_norm(y, z, w):
    u = y * jax.nn.silu(z)
    half = D_SSD // 2
    return (jnp.concatenate([_rms(u[:, :half]), _rms(u[:, half:])], axis=1) * w,)


def fn_gated_mix(y, z, ao, w):
    return (jnp.concatenate([ao, fn_gated_norm(y, z, w)[0]], axis=1),)


def fn_swiglu(gu):
    return (jax.nn.silu(gu[:, :D_FF]) * gu[:, D_FF:],)


def _final(x1, ff, tgt, g2, name):
    S = x1.shape[0]
    tm = min(256, S)

    def body(x_ref, f_ref, t_ref, g_ref, dx_ref, df_ref, dg_ref, l_ref):
        @pl.when(pl.program_id(0) == 0)
        def _():
            dg_ref[...] = jnp.zeros_like(dg_ref)
            l_ref[...] = jnp.zeros_like(l_ref)

        f = f_ref[...]
        g = g_ref[...]
        e = x_ref[...] + g * f - t_ref[...]
        dx = e * (1.0 / D_MODEL)
        dx_ref[...] = dx
        df_ref[...] = g * dx
        dg_ref[...] += jnp.sum(dx * f, axis=0, keepdims=True)
        l_ref[...] += jnp.sum(e * e) * (0.5 / D_MODEL)

    r = _rspec(tm, D_MODEL, 0)
    return pl.pallas_call(
        body, name=name, grid=(S // tm,),
        in_specs=[r, r, r, _vspec((1, D_MODEL))],
        out_specs=[r, r, _vspec((1, D_MODEL)), _vspec((1, LANES))],
        out_shape=[jax.ShapeDtypeStruct((S, D_MODEL), F32), jax.ShapeDtypeStruct((S, D_MODEL), F32),
                   jax.ShapeDtypeStruct((1, D_MODEL), F32), jax.ShapeDtypeStruct((1, LANES), F32)],
        compiler_params=_cp(("arbitrary",)),
    )(x1, ff, tgt, g2)


def _causal_mask(t):
    r = lax.broadcasted_iota(jnp.int32, (t, t), 0)
    c = lax.broadcasted_iota(jnp.int32, (t, t), 1)
    return c <= r


LOG2E = 1.4426950408889634
EXP2_SCALE = ATT_SCALE * LOG2E
ATT_TQ, ATT_TK = 512, 1024
ATT_BQ, ATT_BK = 1024, 512


def _attn_fwd(qf, kf, va, name):
    S = qf.shape[0]
    T, TK = min(ATT_TQ, S), min(ATT_TK, S)
    nmask = max(1, T // TK)

    def body(q_ref, k_ref, v_ref, o_ref, l_ref):
        i = pl.program_id(1)
        r = lax.broadcasted_iota(jnp.int32, (T, TK), 0)
        c = lax.broadcasted_iota(jnp.int32, (T, TK), 1)
        qs = [q_ref[:, hh * LANES:(hh + 1) * LANES] for hh in range(2)]

        def blk(j, carry, masked):
            off = pl.multiple_of(j * TK, TK)
            out = []
            for hh in range(2):
                m, acc = carry[hh]
                s = _dot(qs[hh], k_ref[pl.ds(off, TK), hh * LANES:(hh + 1) * LANES], ((1,), (1,)))
                if masked:
                    s = jnp.where(c + j * TK <= r + i * T, s, NEG)
                mn = jnp.maximum(m, jnp.max(s, axis=1, keepdims=True))
                p = jnp.exp2((s - mn) * EXP2_SCALE)
                al = jnp.exp2((m - mn) * EXP2_SCALE)
                vj = v_ref[pl.ds(off, TK), hh * LANES:(hh + 1) * LANES]
                out.append((mn, al * acc + _dot(p.astype(BF16), vj, ((1,), (0,)))))
            return tuple(out)

        one = (jnp.full((T, 1), NEG, F32), jnp.zeros((T, LANES), F32))
        nfull = lax.div(i * T, TK)
        carry = lax.fori_loop(0, nfull, lambda j, cr: blk(j, cr, False), (one, one))
        for t in range(nmask):
            carry = blk(nfull + t, carry, True)
        for hh in range(2):
            m, acc = carry[hh]
            l = acc[:, 64:65]
            o_ref[:, hh * 64:(hh + 1) * 64] = (acc / l)[:, :64]
            l_ref[:, hh * 64:(hh + 1) * 64] = jnp.broadcast_to(m * ATT_SCALE + jnp.log(l), (T, 64))

    return pl.pallas_call(
        body, name=name, grid=(HEADS // 2, S // T),
        in_specs=[pl.BlockSpec((T, 256), lambda h, i: (i, h)), pl.BlockSpec((S, 256), lambda h, i: (0, h)),
                  pl.BlockSpec((S, 256), lambda h, i: (0, h))],
        out_specs=[pl.BlockSpec((T, LANES), lambda h, i: (i, h)), pl.BlockSpec((T, LANES), lambda h, i: (i, h))],
        out_shape=[jax.ShapeDtypeStruct((S, D_SSD), F32), jax.ShapeDtypeStruct((S, D_SSD), F32)],
        compiler_params=_cp(("parallel", "parallel")),
    )(qf, kf, va)


def _attn_bwd(qf, kf, kT, va, do, lse_r, delta_r, name):
    S = qf.shape[0]
    T, TK = min(ATT_BQ, S), min(ATT_BK, S)
    nq = S // T
    nmask = max(1, TK // T)

    def body(q_ref, k_ref, kT_ref, v_ref, do_ref, l_ref, d_ref, dqT_ref, dk_ref, dv_ref):
        j = pl.program_id(1)

        @pl.when(j == 0)
        def _():
            dqT_ref[...] = jnp.zeros_like(dqT_ref)

        r = lax.broadcasted_iota(jnp.int32, (TK, T), 0)
        c = lax.broadcasted_iota(jnp.int32, (TK, T), 1)
        lo = (lax.broadcasted_iota(jnp.int32, (1, LANES), 1) < 64).astype(F32)
        ks = [k_ref[:, hh * LANES:(hh + 1) * LANES] for hh in range(2)]
        vs = [v_ref[:, hh * LANES:(hh + 1) * LANES] for hh in range(2)]
        kTs = [kT_ref[hh * LANES:(hh + 1) * LANES, :] for hh in range(2)]

        def blk(i, carry, masked):
            off = pl.multiple_of(i * T, T)
            dall = do_ref[pl.ds(off, T), :]
            out = []
            for hh in range(2):
                dk, dv = carry[hh]
                q = q_ref[pl.ds(off, T), hh * LANES:(hh + 1) * LANES]
                dop = ((dall if hh == 0 else pltpu.roll(dall, 64, 1)) * lo).astype(BF16)
                lrow = l_ref[0, hh:hh + 1, pl.ds(off, T)]
                drow = d_ref[0, hh:hh + 1, pl.ds(off, T)]
                pT = jnp.exp2(_dot(ks[hh], q, ((1,), (1,))) * EXP2_SCALE - lrow)
                if masked:
                    pT = jnp.where(r + j * TK <= c + i * T, pT, 0.0)
                dpT = _dot(vs[hh], dop, ((1,), (1,)))
                dsT = (pT * (dpT - drow) * ATT_SCALE).astype(BF16)
                dv = dv + _dot(pT.astype(BF16), dop, ((1,), (0,)))
                dk = dk + _dot(dsT, q, ((1,), (0,)))
                dqT_ref[hh * LANES:(hh + 1) * LANES, pl.ds(off, T)] += _dot(kTs[hh], dsT, ((1,), (0,)))
                out.append((dk, dv))
            return tuple(out)

        z = (jnp.zeros((TK, LANES), F32), jnp.zeros((TK, LANES), F32))
        first = lax.div(j * TK, T)
        carry = (z, z)
        for t in range(nmask):
            carry = blk(first + t, carry, True)
        carry = lax.fori_loop(first + nmask, nq, lambda i, cr: blk(i, cr, False), carry)
        for hh in range(2):
            dk_ref[:, hh * LANES:(hh + 1) * LANES] = carry[hh][0]
            dv_ref[:, hh * LANES:(hh + 1) * LANES] = carry[hh][1]

    return pl.pallas_call(
        body, name=name, grid=(HEADS // 2, S // TK),
        in_specs=[pl.BlockSpec((S, 256), lambda h, j: (0, h)), pl.BlockSpec((TK, 256), lambda h, j: (j, h)),
                  pl.BlockSpec((256, TK), lambda h, j: (h, j)), pl.BlockSpec((TK, 256), lambda h, j: (j, h)),
                  pl.BlockSpec((S, LANES), lambda h, j: (0, h)), pl.BlockSpec((1, 2, S), lambda h, j: (h, 0, 0)),
                  pl.BlockSpec((1, 2, S), lambda h, j: (h, 0, 0))],
        out_specs=[pl.BlockSpec((256, S), lambda h, j: (h, 0)), pl.BlockSpec((TK, 256), lambda h, j: (j, h)),
                   pl.BlockSpec((TK, 256), lambda h, j: (j, h))],
        out_shape=[jax.ShapeDtypeStruct((D_MODEL, S), F32), jax.ShapeDtypeStruct((S, D_MODEL), F32),
                   jax.ShapeDtypeStruct((S, D_MODEL), F32)],
        compiler_params=_cp(("parallel", "arbitrary")),
    )(qf, kf, kT, va, do, lse_r, delta_r)


def _shift_down(x, s):
    if s == 0:
        return x
    rows = lax.broadcasted_iota(jnp.int32, x.shape, 0)
    return jnp.where(rows >= s, pltpu.roll(x, s, 0), 0.0)


def _shift_up(x, s):
    if s == 0:
        return x
    n = x.shape[0]
    rows = lax.broadcasted_iota(jnp.int32, x.shape, 0)
    return jnp.where(rows < n - s, pltpu.roll(x, n - s, 0), 0.0)


def _conv_fwd(proj, cvec, name):
    S = proj.shape[0]

    def body(x_ref, c_ref, o_ref):
        x = x_ref[...]
        y = jnp.broadcast_to(c_ref[4:5, :], x.shape)
        for k in range(CONV_TAPS):
            y = y + c_ref[k:k + 1, :] * _shift_down(x, CONV_TAPS - 1 - k)
        o_ref[...] = y * jax.nn.sigmoid(y)

    return pl.pallas_call(
        body, name=name, grid=(D_CONV // LANES,),
        in_specs=[pl.BlockSpec((S, LANES), lambda j: (0, 8 + j)), pl.BlockSpec((8, LANES), lambda j: (0, j))],
        out_specs=pl.BlockSpec((S, LANES), lambda j: (0, j)),
        out_shape=jax.ShapeDtypeStruct((S, D_CONV), F32),
        compiler_params=_cp(("parallel",)),
    )(proj, cvec)


def _conv_bwd(proj, cvec, dact, name):
    S = proj.shape[0]

    def body(x_ref, c_ref, d_ref, dx_ref, dc_ref):
        x = x_ref[...]
        y = jnp.broadcast_to(c_ref[4:5, :], x.shape)
        for k in range(CONV_TAPS):
            y = y + c_ref[k:k + 1, :] * _shift_down(x, CONV_TAPS - 1 - k)
        sg = jax.nn.sigmoid(y)
        dy = d_ref[...] * (sg * (1.0 + y * (1.0 - sg)))
        dx = jnp.zeros_like(x)
        for k in range(CONV_TAPS):
            s = CONV_TAPS - 1 - k
            dx = dx + c_ref[k:k + 1, :] * _shift_up(dy, s)
            dc_ref[k:k + 1, :] = jnp.sum(dy * _shift_down(x, s), axis=0, keepdims=True)
        dx_ref[...] = dx
        dc_ref[4:5, :] = jnp.sum(dy, axis=0, keepdims=True)
        dc_ref[5:8, :] = jnp.zeros((3, LANES), F32)

    return pl.pallas_call(
        body, name=name, grid=(D_CONV // LANES,),
        in_specs=[pl.BlockSpec((S, LANES), lambda j: (0, 8 + j)), pl.BlockSpec((8, LANES), lambda j: (0, j)),
                  pl.BlockSpec((S, LANES), lambda j: (0, j))],
        out_specs=[pl.BlockSpec((S, LANES), lambda j: (0, j)), pl.BlockSpec((8, LANES), lambda j: (0, j))],
        out_shape=[jax.ShapeDtypeStruct((S, D_CONV), F32), jax.ShapeDtypeStruct((8, D_CONV), F32)],
        compiler_params=_cp(("parallel",)),
    )(proj, cvec, dact)


def fn_ssd_chunk(xs, bm, cm, dtr, state, vecs):
    Q = CHUNK
    dt = jax.nn.softplus(dtr + vecs[0:1])
    a = -jnp.exp(vecs[1:2])
    d_skip = vecs[2:3]
    adt = dt * a
    tril = _causal_mask(Q)
    acs = _dot(tril.astype(F32), adt, ((1,), (0,)), lax.Precision.HIGHEST)
    acs_t = acs.T
    ys, new_states = [], []
    Bs = [bm[:, g * SSD_N:(g + 1) * SSD_N].astype(BF16) for g in range(2)]
    Cs = [cm[:, g * SSD_N:(g + 1) * SSD_N].astype(BF16) for g in range(2)]
    Gs = [_dot(Cs[g], Bs[g], ((1,), (1,))) for g in range(2)]
    for h in range(SSD_HEADS):
        g = h // (SSD_HEADS // 2)
        x = xs[:, h * SSD_P:(h + 1) * SSD_P]
        B, C = Bs[g], Cs[g]
        xdt = x * dt[:, h:h + 1]
        A = acs[:, h:h + 1]
        L = jnp.exp(jnp.where(tril, A - acs_t[h:h + 1, :], -jnp.inf))
        M = Gs[g] * L
        yd = _dot(M.astype(BF16), xdt.astype(BF16), ((1,), (0,)))
        st = state[h]
        yo = _dot(C, st.astype(BF16), ((1,), (1,))) * jnp.exp(A)
        alast = acs[Q - 1:Q, h:h + 1]
        U = xdt * jnp.exp(alast - A)
        new_states.append(jnp.exp(alast) * st + _dot(U.astype(BF16), B, ((0,), (0,))))
        ys.append(yd + yo + d_skip[:, h:h + 1] * x)
    return jnp.concatenate(ys, axis=1), jnp.stack(new_states)


def _ssd_fwd(xact, proj, svec, name):
    S = xact.shape[0]
    nc = S // CHUNK

    def body(x_ref, dt_ref, v_ref, y_ref, st_ref, state):
        @pl.when(pl.program_id(0) == 0)
        def _():
            state[...] = jnp.zeros_like(state)

        st_ref[0] = state[...]
        x = x_ref[...]
        y, sn = fn_ssd_chunk(x[:, 0:512], x[:, 512:768], x[:, 768:1024], dt_ref[...], state[...], v_ref[...])
        y_ref[...] = y
        state[...] = sn

    return pl.pallas_call(
        body, name=name, grid=(nc,),
        in_specs=[pl.BlockSpec((CHUNK, D_CONV), lambda i: (i, 0)), pl.BlockSpec((CHUNK, LANES), lambda i: (i, 16)),
                  pl.BlockSpec((8, LANES), lambda i: (0, 0))],
        out_specs=[pl.BlockSpec((CHUNK, D_SSD), lambda i: (i, 0)),
                   pl.BlockSpec((1, SSD_HEADS, SSD_P, SSD_N), lambda i: (i, 0, 0, 0))],
        out_shape=[jax.ShapeDtypeStruct((S, D_SSD), F32), jax.ShapeDtypeStruct((nc, SSD_HEADS, SSD_P, SSD_N), F32)],
        scratch_shapes=[pltpu.VMEM((SSD_HEADS, SSD_P, SSD_N), F32)],
        compiler_params=_cp(("arbitrary",)),
    )(xact, proj, svec)


def _ssd_bwd(xact, proj, svec, states, dy, name):
    S = xact.shape[0]
    nc = S // CHUNK

    def body(x_ref, dt_ref, v_ref, st_ref, dy_ref, dx_ref, ddt_ref, dv_ref, dstate):
        @pl.when(pl.program_id(0) == 0)
        def _():
            dstate[...] = jnp.zeros_like(dstate)
            dv_ref[...] = jnp.zeros_like(dv_ref)

        x = x_ref[...]
        _, vjp = jax.vjp(fn_ssd_chunk, x[:, 0:512], x[:, 512:768], x[:, 768:1024], dt_ref[...], st_ref[0], v_ref[...])
        dxs, dbm, dcm, ddt, dst, dvec = vjp((dy_ref[...], dstate[...]))
        dx_ref[:, 0:512] = dxs
        dx_ref[:, 512:768] = dbm
        dx_ref[:, 768:1024] = dcm
        ddt_ref[...] = ddt
        dstate[...] = dst
        dv_ref[...] += dvec

    rev = lambda i: (nc - 1 - i, 0)
    return pl.pallas_call(
        body, name=name, grid=(nc,),
        in_specs=[pl.BlockSpec((CHUNK, D_CONV), rev), pl.BlockSpec((CHUNK, LANES), lambda i: (nc - 1 - i, 16)),
                  pl.BlockSpec((8, LANES), lambda i: (0, 0)),
                  pl.BlockSpec((1, SSD_HEADS, SSD_P, SSD_N), lambda i: (nc - 1 - i, 0, 0, 0)),
                  pl.BlockSpec((CHUNK, D_SSD), rev)],
        out_specs=[pl.BlockSpec((CHUNK, D_CONV), rev), pl.BlockSpec((CHUNK, LANES), rev),
                   pl.BlockSpec((8, LANES), lambda i: (0, 0))],
        out_shape=[jax.ShapeDtypeStruct((S, D_CONV), F32), jax.ShapeDtypeStruct((S, LANES), F32),
                   jax.ShapeDtypeStruct((8, LANES), F32)],
        scratch_shapes=[pltpu.VMEM((SSD_HEADS, SSD_P, SSD_N), F32)],
        compiler_params=_cp(("arbitrary",)),
    )(xact, proj, svec, states, dy)


def _ada_fwd(c_all, w_ada, b_sh, name):
    nb = 1536 // 512

    def body(c_ref, w_ref, b_ref, o_ref):
        ca = jax.nn.silu(c_ref[...]).astype(BF16)
        o_ref[0] = _dot(ca, w_ref[0].astype(BF16), ((1,), (0,))) + b_ref[0]

    return pl.pallas_call(
        body, name=name, grid=(DEPTH, nb),
        in_specs=[pl.BlockSpec((8, D_MODEL), lambda l, j: (0, 0)), pl.BlockSpec((1, D_MODEL, 512), lambda l, j: (l, 0, j)),
                  pl.BlockSpec((1, 1, 512), lambda l, j: (l, 0, j))],
        out_specs=pl.BlockSpec((1, 8, 512), lambda l, j: (l, 0, j)),
        out_shape=jax.ShapeDtypeStruct((DEPTH, 8, 1536), F32),
        compiler_params=_cp(("parallel", "parallel")),
    )(c_all, w_ada, b_sh)


def _ada_bwd(c_all_t, dmod_sh, name):
    nb = 1536 // 512

    def body(c_ref, d_ref, o_ref):
        ca = jax.nn.silu(c_ref[...])
        acc = ca[:, 0:1] * d_ref[0, 0:1, :]
        for b in range(1, 8):
            acc = acc + ca[:, b:b + 1] * d_ref[0, b:b + 1, :]
        o_ref[0] = acc

    return pl.pallas_call(
        body, name=name, grid=(DEPTH, nb),
        in_specs=[pl.BlockSpec((D_MODEL, 8), lambda l, j: (0, 0)), pl.BlockSpec((1, 8, 512), lambda l, j: (l, 0, j))],
        out_specs=pl.BlockSpec((1, D_MODEL, 512), lambda l, j: (l, 0, j)),
        out_shape=jax.ShapeDtypeStruct((DEPTH, D_MODEL, 1536), F32),
        compiler_params=_cp(("parallel", "parallel")),
    )(c_all_t, dmod_sh)


def _rows_tile(rows):
    return next(t for t in (512, 256, 128, 64, 32, 16, 8) if rows % t == 0)


SUM_BLOCKS = 4
ADAM_BLOCKS = 8


def _sum_sibling(gs, ls, ci, name):
    n = len(gs)

    def body(c_ref, *refs):
        for p in range(n):
            refs[2 * n + p][...] = refs[2 * p][...] + refs[2 * p + 1][...]

    in_specs, out_specs, out_shape = [], [], []
    for g in gs:
        _, _, rh, cw = g.shape
        rb = rh // SUM_BLOCKS
        in_specs += [pl.BlockSpec((None, None, rb, cw), lambda s, i, c: (s, c[0], i, 0)),
                     pl.BlockSpec((None, rb, cw), lambda s, i, c: (s, i, 0))]
        out_specs.append(pl.BlockSpec((None, rb, cw), lambda s, i, c: (s, i, 0)))
        out_shape.append(jax.ShapeDtypeStruct((4, rh, cw), F32))
    ops = [a for pair in zip(gs, ls) for a in pair]
    return pl.pallas_call(
        body, name=name,
        grid_spec=pltpu.PrefetchScalarGridSpec(num_scalar_prefetch=1, grid=(4, SUM_BLOCKS), in_specs=in_specs, out_specs=out_specs),
        out_shape=out_shape, compiler_params=_cp(("parallel", "parallel")),
    )(ci.reshape(1).astype(jnp.int32), *ops)


def _sum_chips(cs, lands, chip, name):
    n = len(cs)

    def body(c_ref, *refs):
        for p in range(n):
            a = refs[4 * p:4 * p + 4]
            refs[4 * n + p][...] = ((a[0][...] + a[1][...]) + a[2][...]) + a[3][...]

    in_specs, out_specs, out_shape = [], [], []
    for c in cs:
        _, rh, cw = c.shape
        rb = rh // SUM_BLOCKS
        in_specs.append(pl.BlockSpec((None, rb, cw), lambda i, ch: (ch[0], i, 0)))
        in_specs += [pl.BlockSpec((None, rb, cw), functools.partial(lambda i, ch, k: (k, i, 0), k=k)) for k in range(3)]
        out_specs.append(pl.BlockSpec((rb, cw), lambda i, ch: (i, 0)))
        out_shape.append(jax.ShapeDtypeStruct((rh, cw), F32))
    ops = [a for c, l in zip(cs, lands) for a in (c, l, l, l)]
    return pl.pallas_call(
        body, name=name,
        grid_spec=pltpu.PrefetchScalarGridSpec(num_scalar_prefetch=1, grid=(SUM_BLOCKS,), in_specs=in_specs, out_specs=out_specs),
        out_shape=out_shape, compiler_params=_cp(("parallel",)),
    )(chip.reshape(1).astype(jnp.int32), *ops)


def _adam_update(w, m, v, g):
    c1 = 1.0 / (1.0 - ADAM_B1 ** ADAM_STEP)
    c2 = 1.0 / (1.0 - ADAM_B2 ** ADAM_STEP)
    nm = ADAM_B1 * m + (1.0 - ADAM_B1) * g
    nv = ADAM_B2 * v + (1.0 - ADAM_B2) * (g * g)
    return -ADAM_LR * ((nm * c1) / (jnp.sqrt(nv * c2) + ADAM_EPS) + ADAM_WD * w), nm, nv


def _adam_multi(ws, ms, vs, gs, name):
    n = len(ws)

    def body(*refs):
        for p in range(n):
            d, nm, nv = _adam_update(*[refs[4 * p + k][...] for k in range(4)])
            refs[4 * n + 3 * p][...] = d
            refs[4 * n + 3 * p + 1][...] = nm
            refs[4 * n + 3 * p + 2][...] = nv

    in_specs, out_specs, out_shape = [], [], []
    for w in ws:
        _, r, cw = w.shape
        spec = pl.BlockSpec((None, r // ADAM_BLOCKS, cw), lambda l, i: (l, i, 0))
        in_specs += [spec] * 4
        out_specs += [spec] * 3
        out_shape += [jax.ShapeDtypeStruct(w.shape, F32)] * 3
    ops = [a for q in zip(ws, ms, vs, gs) for a in q]
    res = pl.pallas_call(
        body, name=name, grid=(DEPTH, ADAM_BLOCKS), in_specs=in_specs, out_specs=out_specs, out_shape=out_shape,
        compiler_params=_cp(("parallel", "parallel")),
    )(*ops)
    return res[0::3], res[1::3], res[2::3]


def _adam(w, m, v, parts, name):
    rows, width = w.shape
    bm = min(256, _rows_tile(rows))
    np_ = len(parts)
    c1 = 1.0 / (1.0 - ADAM_B1 ** ADAM_STEP)
    c2 = 1.0 / (1.0 - ADAM_B2 ** ADAM_STEP)

    def body(*refs):
        w_ref, m_ref, v_ref = refs[:3]
        g = refs[3][...]
        for r in refs[4:3 + np_]:
            g = g + r[...]
        g_ref, d_ref, nm_ref, nv_ref = refs[3 + np_:]
        nm = ADAM_B1 * m_ref[...] + (1.0 - ADAM_B1) * g
        nv = ADAM_B2 * v_ref[...] + (1.0 - ADAM_B2) * (g * g)
        g_ref[...] = g
        nm_ref[...] = nm
        nv_ref[...] = nv
        d_ref[...] = -ADAM_LR * ((nm * c1) / (jnp.sqrt(nv * c2) + ADAM_EPS) + ADAM_WD * w_ref[...])

    blk = pl.BlockSpec((bm, width), lambda i: (i, 0))
    return pl.pallas_call(
        body, name=name, grid=(rows // bm,),
        in_specs=[blk, blk, blk] + [pl.BlockSpec((bm, width), functools.partial(lambda i, o: (i + o, 0), o=off // bm))
                                    for (_, off) in parts],
        out_specs=[blk, blk, blk, blk],
        out_shape=[jax.ShapeDtypeStruct((rows, width), F32)] * 4,
        compiler_params=_cp(("parallel",)),
    )(w, m, v, *[p[0] for p in parts])


def _coords():
    return lax.axis_index("x"), lax.axis_index("y"), lax.axis_index("c")


def _other_chips(x, y):
    return [(1 - x, y), (x, 1 - y), (1 - x, 1 - y)]


def _ag8(blk, name):
    m_per, n = blk.shape

    def body(x_ref, out_ref, send_sems, recv_sems, local_sem):
        x, y, c = _coords()
        me, sibling = (x, y, c), (x, y, 1 - c)
        chips = _other_chips(x, y)

        def rows(px, py, pc):
            return out_ref.at[pl.ds((4 * px + 2 * py + pc) * m_per, m_per), :]

        def copy(k, block, to, src=None):
            return pltpu.make_async_remote_copy(
                src_ref=rows(*block) if src is None else src, dst_ref=rows(*block),
                send_sem=send_sems.at[k], recv_sem=recv_sems.at[k], device_id=to, device_id_type=MESH)

        mine = pltpu.make_async_copy(x_ref, rows(*me), local_sem)
        mine.start()
        first = [copy(0, me, sibling, src=x_ref)]
        first += [copy(1 + j, me, (*chip, c), src=x_ref) for j, chip in enumerate(chips)]
        for cp in first:
            cp.start()
        passed = [copy(4 + j, (*chip, c), sibling) for j, chip in enumerate(chips)]
        for j, chip in enumerate(chips):
            copy(1 + j, (*chip, c), me).wait_recv()
            passed[j].start()
        copy(0, sibling, me).wait_recv()
        for j, chip in enumerate(chips):
            copy(4 + j, (*chip, 1 - c), me).wait_recv()
        for cp in first + passed:
            cp.wait_send()
        mine.wait()

    return pl.pallas_call(
        body, name=name,
        out_shape=jax.ShapeDtypeStruct((8 * m_per, n), blk.dtype),
        in_specs=[pl.BlockSpec(memory_space=pltpu.VMEM)], out_specs=pl.BlockSpec(memory_space=pltpu.VMEM),
        scratch_shapes=[pltpu.SemaphoreType.DMA((7,)), pltpu.SemaphoreType.DMA((7,)), pltpu.SemaphoreType.DMA],
    )(blk)


HBM_SPEC = pl.BlockSpec(memory_space=pltpu.HBM)
SEM_SPEC = pl.BlockSpec(memory_space=pltpu.SEMAPHORE)
EFFECT = pltpu.SideEffectType.DATAFLOW_SIDE_EFFECTING


def _remote(src, dst, send_sem, recv_sem, to):
    return pltpu.make_async_remote_copy(src_ref=src, dst_ref=dst, send_sem=send_sem, recv_sem=recv_sem,
                                        device_id=to, device_id_type=MESH)


def _ag_list(shards, name):
    n = len(shards)

    def body(*refs):
        sh, out = refs[:n], refs[n:2 * n]
        send_sems, recv_sems = refs[2 * n:]
        x, y, c = _coords()
        sibling = (x, y, 1 - c)
        chips = _other_chips(x, y)
        first = [_remote(sh[p].at[c], out[p].at[2 * x + y, c], send_sems.at[6 * p + j], recv_sems.at[6 * p + j], (px, py, c))
                 for p in range(n) for j, (px, py) in enumerate(chips)]
        for cp in first:
            cp.start()
        passed = []
        for j, (px, py) in enumerate(chips):
            for p in range(n):
                got = out[p].at[2 * px + py, c]
                _remote(got, got, send_sems.at[6 * p + j], recv_sems.at[6 * p + j], (x, y, c)).wait_recv()
                cp = _remote(got, got, send_sems.at[6 * p + 3 + j], recv_sems.at[6 * p + 3 + j], sibling)
                cp.start()
                passed.append(cp)
        for j, (px, py) in enumerate(chips):
            for p in range(n):
                got = out[p].at[2 * px + py, 1 - c]
                _remote(got, got, send_sems.at[6 * p + 3 + j], recv_sems.at[6 * p + 3 + j], (x, y, c)).wait_recv()
        for cp in first + passed:
            cp.wait_send()

    return pl.pallas_call(
        body, name=name,
        out_shape=[jax.ShapeDtypeStruct((4,) + s.shape, s.dtype) for s in shards],
        in_specs=[pl.BlockSpec(memory_space=pl.ANY)] * n, out_specs=[pl.BlockSpec(memory_space=pl.ANY)] * n,
        scratch_shapes=[pltpu.SemaphoreType.DMA((6 * n,)), pltpu.SemaphoreType.DMA((6 * n,))],
    )(*shards)


def _ag_direct_copies(sh, land, send_sems, recv_sems, starting):
    x, y, c = _coords()
    return [_remote(sh[p], land[p].at[2 * x + y] if starting else land[p].at[2 * px + py],
                    send_sems.at[3 * p + j], recv_sems.at[3 * p + j], (px, py, c))
            for p in range(len(sh)) for j, (px, py) in enumerate(_other_chips(x, y))]


def _ag_direct_start(shards, name):
    n = len(shards)

    def body(*refs):
        for cp in _ag_direct_copies(refs[:n], refs[n:2 * n], refs[2 * n], refs[2 * n + 1], True):
            cp.start()
        token = refs[4 * n + 2]
        token[...] = jnp.zeros_like(token)

    lands = [pltpu.with_memory_space_constraint(lax.empty((4,) + s.shape, s.dtype), pltpu.HBM) for s in shards]
    res = pl.pallas_call(
        body, name=name,
        out_shape=(pltpu.SemaphoreType.DMA((3 * n,)), pltpu.SemaphoreType.DMA((3 * n,)))
        + tuple(pltpu.HBM(s.shape, s.dtype) for s in shards) + tuple(pltpu.HBM(l.shape, l.dtype) for l in lands)
        + (jax.ShapeDtypeStruct((8, LANES), F32),),
        in_specs=(HBM_SPEC,) * (2 * n), out_specs=(SEM_SPEC, SEM_SPEC) + (HBM_SPEC,) * (2 * n) + (pl.BlockSpec(memory_space=pltpu.VMEM),),
        input_output_aliases={i: 2 + i for i in range(2 * n)},
        compiler_params=pltpu.CompilerParams(has_side_effects=EFFECT),
    )(*[pltpu.with_memory_space_constraint(s, pltpu.HBM) for s in shards], *lands)
    return res[0], res[1], res[2:2 + n], res[2 + n:2 + 2 * n], res[2 + 2 * n]


def _ag_direct_wait(send_sems, recv_sems, sh_thru, land_thru, after, name):
    n = len(sh_thru)

    def body(*refs):
        sh, land = refs[:n], refs[n:2 * n]
        for cp in _ag_direct_copies(sh, land, refs[2 * n], refs[2 * n + 1], False):
            cp.wait_send()
            cp.wait_recv()

    res = pl.pallas_call(
        body, name=name,
        out_shape=tuple(pltpu.HBM(s.shape, s.dtype) for s in sh_thru) + tuple(pltpu.HBM(l.shape, l.dtype) for l in land_thru),
        in_specs=(HBM_SPEC,) * (2 * n) + (SEM_SPEC, SEM_SPEC, pl.BlockSpec(memory_space=pl.ANY)),
        out_specs=(HBM_SPEC,) * (2 * n), input_output_aliases={i: i for i in range(2 * n)},
        compiler_params=pltpu.CompilerParams(has_side_effects=EFFECT),
    )(*sh_thru, *land_thru, send_sems, recv_sems, after)
    return res[n:]


def _rs_sibling_list(gs, name):
    n = len(gs)

    def body(*refs):
        g, out, send_sems, recv_sems = refs[:n], refs[n:2 * n], refs[2 * n], refs[2 * n + 1]
        x, y, c = _coords()
        cps = [_remote(g[p].at[s, 1 - c], out[p].at[s], send_sems.at[4 * p + s], recv_sems.at[4 * p + s], (x, y, 1 - c))
               for p in range(n) for s in range(4)]
        for cp in cps:
            cp.start()
        for cp in cps:
            cp.wait_recv()
        for cp in cps:
            cp.wait_send()

    return pl.pallas_call(
        body, name=name,
        out_shape=[jax.ShapeDtypeStruct((4,) + g.shape[2:], g.dtype) for g in gs],
        in_specs=[pl.BlockSpec(memory_space=pl.ANY)] * n, out_specs=[pl.BlockSpec(memory_space=pl.ANY)] * n,
        scratch_shapes=[pltpu.SemaphoreType.DMA((4 * n,)), pltpu.SemaphoreType.DMA((4 * n,))],
    )(*gs)


def _rs_chips_copies(cs, land, send_sems, recv_sems):
    x, y, c = _coords()
    return [_remote(cs[p].at[2 * px + py], land[p].at[j], send_sems.at[3 * p + j], recv_sems.at[3 * p + j], (px, py, c))
            for p in range(len(cs)) for j, (px, py) in enumerate(_other_chips(x, y))]


def _rs_chips_start(cs, name):
    n = len(cs)

    def body(*refs):
        for cp in _rs_chips_copies(refs[:n], refs[n:2 * n], refs[2 * n], refs[2 * n + 1]):
            cp.start()
        token = refs[4 * n + 2]
        token[...] = jnp.zeros_like(token)

    lands = [pltpu.with_memory_space_constraint(lax.empty((3,) + c.shape[1:], c.dtype), pltpu.HBM) for c in cs]
    res = pl.pallas_call(
        body, name=name,
        out_shape=(pltpu.SemaphoreType.DMA((3 * n,)), pltpu.SemaphoreType.DMA((3 * n,)))
        + tuple(pltpu.HBM(c.shape, c.dtype) for c in cs) + tuple(pltpu.HBM(l.shape, l.dtype) for l in lands)
        + (jax.ShapeDtypeStruct((8, LANES), F32),),
        in_specs=(HBM_SPEC,) * (2 * n), out_specs=(SEM_SPEC, SEM_SPEC) + (HBM_SPEC,) * (2 * n) + (pl.BlockSpec(memory_space=pltpu.VMEM),),
        input_output_aliases={i: 2 + i for i in range(2 * n)},
        compiler_params=pltpu.CompilerParams(has_side_effects=EFFECT),
    )(*[pltpu.with_memory_space_constraint(c, pltpu.HBM) for c in cs], *lands)
    return res[0], res[1], res[2:2 + n], res[2 + n:2 + 2 * n], res[2 + 2 * n]


def _rs_chips_wait(send_sems, recv_sems, cs_thru, land_thru, after, name):
    n = len(cs_thru)

    def body(*refs):
        for cp in _rs_chips_copies(refs[:n], refs[n:2 * n], refs[2 * n], refs[2 * n + 1]):
            cp.wait_send()
            cp.wait_recv()

    res = pl.pallas_call(
        body, name=name,
        out_shape=tuple(pltpu.HBM(c.shape, c.dtype) for c in cs_thru) + tuple(pltpu.HBM(l.shape, l.dtype) for l in land_thru),
        in_specs=(HBM_SPEC,) * (2 * n) + (SEM_SPEC, SEM_SPEC, pl.BlockSpec(memory_space=pl.ANY)),
        out_specs=(HBM_SPEC,) * (2 * n), input_output_aliases={i: i for i in range(2 * n)},
        compiler_params=pltpu.CompilerParams(has_side_effects=EFFECT),
    )(*cs_thru, *land_thru, send_sems, recv_sems, after)
    return res[:n], res[n:]


def _swap_list(ghs, name):
    n = len(ghs)

    def body(*refs):
        g, out, send_sems, recv_sems = refs[:n], refs[n:2 * n], refs[2 * n], refs[2 * n + 1]
        x, y, c = _coords()
        cps = [_remote(g[p], out[p], send_sems.at[p], recv_sems.at[p], (x, y, 1 - c)) for p in range(n)]
        for cp in cps:
            cp.start()
        for cp in cps:
            cp.wait_recv()
        for cp in cps:
            cp.wait_send()

    return pl.pallas_call(
        body, name=name,
        out_shape=[jax.ShapeDtypeStruct(g.shape, g.dtype) for g in ghs],
        in_specs=[pl.BlockSpec(memory_space=pl.ANY)] * n, out_specs=[pl.BlockSpec(memory_space=pl.ANY)] * n,
        scratch_shapes=[pltpu.SemaphoreType.DMA((n,)), pltpu.SemaphoreType.DMA((n,))],
    )(*ghs)


def _pad_win(w):
    return jnp.concatenate([w[:, :416], jnp.zeros((w.shape[0], 96), w.dtype), w[:, 416:1952],
                            w[:, 1952:1960], jnp.zeros((w.shape[0], 120), w.dtype)], axis=1)


def _unpad_win(g):
    return jnp.concatenate([g[:, :416], g[:, 512:2048], g[:, 2048:2056]], axis=1)


def _pad_wq(w):
    return jnp.pad(w.reshape(Q_LORA, HEADS, QK_DIM), ((0, 0), (0, 0), (0, LANES - QK_DIM))).reshape(Q_LORA, HEADS * LANES)


def _unpad_wq(g):
    return g.reshape(Q_LORA, HEADS, LANES)[:, :, :QK_DIM].reshape(Q_LORA, HEADS * QK_DIM)


def _cols_to_shards(a):
    r, c4 = a.shape
    return a.reshape(r, 4, c4 // 4).transpose(1, 0, 2)


def _shards_to_cols(a):
    _, r, c = a.shape
    return a.transpose(1, 0, 2).reshape(r, 4 * c)


def _pack_small(tree):
    parts = []
    for l in range(DEPTH):
        for (n, k) in SMALL:
            parts.append(jnp.pad(tree[n][l].reshape(-1), (0, -k % LANES)))
    flat = jnp.concatenate(parts)
    return jnp.pad(flat, (0, SMALL_ROWS * LANES - flat.shape[0])).reshape(SMALL_ROWS, LANES)


def _unpack_small(buf):
    flat = buf.reshape(-1)
    out = {n: [] for (n, _) in SMALL}
    o = 0
    for l in range(DEPTH):
        for (n, k) in SMALL:
            out[n].append(flat[o:o + k])
            o += k + (-k % LANES)
    return {n: jnp.stack(v) for n, v in out.items()}


def _vec(v, width=LANES):
    return jnp.pad(v.reshape(1, -1), ((0, 0), (0, width - v.shape[-1])))


def kernel(x, c, positions, norm1_w, norm2_w, w_ada, b_ada, w_in, q_a_norm_w, w_q_up, kv_a_norm_w, w_kv_up, q_nope_norm_w, q_pe_norm_w, k_nope_norm_w, k_pe_norm_w, conv_w, conv_b, dt_bias, a_log, d_skip, ssd_norm_w, w_out, w_gate_up, w_down, loss_target, m_norm1_w, m_norm2_w, m_w_ada, m_b_ada, m_w_in, m_q_a_norm_w, m_w_q_up, m_kv_a_norm_w, m_w_kv_up, m_q_nope_norm_w, m_q_pe_norm_w, m_k_nope_norm_w, m_k_pe_norm_w, m_conv_w, m_conv_b, m_dt_bias, m_a_log, m_d_skip, m_ssd_norm_w, m_w_out, m_w_gate_up, m_w_down, v_norm1_w, v_norm2_w, v_w_ada, v_b_ada, v_w_in, v_q_a_norm_w, v_w_q_up, v_kv_a_norm_w, v_w_kv_up, v_q_nope_norm_w, v_q_pe_norm_w, v_k_nope_norm_w, v_k_pe_norm_w, v_conv_w, v_conv_b, v_dt_bias, v_a_log, v_d_skip, v_ssd_norm_w, v_w_out, v_w_gate_up, v_w_down):
    W = dict(zip(WEIGHTS, (norm1_w, norm2_w, w_ada, b_ada, w_in, q_a_norm_w, w_q_up, kv_a_norm_w, w_kv_up, q_nope_norm_w, q_pe_norm_w, k_nope_norm_w, k_pe_norm_w, conv_w, conv_b, dt_bias, a_log, d_skip, ssd_norm_w, w_out, w_gate_up, w_down)))
    M = dict(zip(WEIGHTS, (m_norm1_w, m_norm2_w, m_w_ada, m_b_ada, m_w_in, m_q_a_norm_w, m_w_q_up, m_kv_a_norm_w, m_w_kv_up, m_q_nope_norm_w, m_q_pe_norm_w, m_k_nope_norm_w, m_k_pe_norm_w, m_conv_w, m_conv_b, m_dt_bias, m_a_log, m_d_skip, m_ssd_norm_w, m_w_out, m_w_gate_up, m_w_down)))
    V = dict(zip(WEIGHTS, (v_norm1_w, v_norm2_w, v_w_ada, v_b_ada, v_w_in, v_q_a_norm_w, v_w_q_up, v_kv_a_norm_w, v_w_kv_up, v_q_nope_norm_w, v_q_pe_norm_w, v_k_nope_norm_w, v_k_pe_norm_w, v_conv_w, v_conv_b, v_dt_bias, v_a_log, v_d_skip, v_ssd_norm_w, v_w_out, v_w_gate_up, v_w_down)))
    S = x.shape[1]
    xi, yi, ci = _coords()
    chip = 2 * xi + yi
    dev = 2 * chip + ci
    x0 = x[0]
    tgt = loss_target[0]

    inv_freq = 1.0 / (ROPE_THETA ** (jnp.arange(0, ROPE, 2, dtype=F32) / ROPE))
    ang = positions[0].astype(F32)[:, None] * inv_freq
    cos, sin = jnp.cos(ang), jnp.sin(ang)
    z16, z32, z64 = jnp.zeros((S, 16), F32), jnp.zeros((S, 32), F32), jnp.zeros((S, 64), F32)
    tab_c = jnp.concatenate([jnp.ones((S, 64), F32), cos, cos, z32], axis=1)
    tab_s1 = jnp.concatenate([z64, z16, sin, z32], axis=1)
    tab_s2 = jnp.concatenate([z64, -sin, z16, z32], axis=1)

    blk0 = jnp.concatenate([c.reshape(-1), W['conv_w'].reshape(-1)]).reshape(24, LANES)
    g0 = _ag8(blk0, "ag_c_conv").reshape(8, 24 * LANES)
    c_all = g0[:, :D_MODEL]
    conv_full = g0[0::2, D_MODEL:].reshape(4, DEPTH, CONV_TAPS, 256).transpose(1, 2, 0, 3).reshape(DEPTH, CONV_TAPS, D_CONV)

    sh = [[W[n][l].astype(BF16) for n in BIG] for l in range(DEPTH)]
    got0 = _ag_list([a.reshape(2, a.shape[0] // 2, a.shape[1]) for a in sh[0]], "ag_w0")

    def layer_weights(gathered, own):
        st = {n: lax.dynamic_update_slice_in_dim(a.reshape(4, -1, a.shape[-1]), o[None], chip, axis=0)
              for n, a, o in zip(BIG, gathered, own)}
        return dict(w_in=_pad_win(_shards_to_cols(st['w_in'])), w_q_up=_pad_wq(_shards_to_cols(st['w_q_up'])),
                    w_kv_up=_shards_to_cols(st['w_kv_up']), w_out=st['w_out'].reshape(D_MODEL, D_MODEL),
                    w_gate_up=st['w_gate_up'], w_down=st['w_down'].reshape(D_FF, D_MODEL))

    LW = [layer_weights(got0, sh[0]), None]

    b_sh = lax.dynamic_slice_in_dim(W['b_ada'], chip * 1536, 1536, axis=1).reshape(DEPTH, 1, 1536)
    mod_sh = _ada_fwd(c_all, W['w_ada'], b_sh, "ada_fwd")
    g1 = _ag8(mod_sh.reshape(192, LANES), "ag_mod").reshape(8, DEPTH, 8, 1536)
    mod_all = g1[0::2].transpose(1, 2, 0, 3).reshape(DEPTH, 8, 6 * D_MODEL)
    mod = lax.dynamic_index_in_dim(mod_all, dev, axis=1, keepdims=False)
    mod, sh1 = lax.optimization_barrier((mod, sh[1]))
    ag1 = _ag_direct_start(sh1, "ag_w1_start")

    def mvec(l, k):
        return mod[l, k * D_MODEL:(k + 1) * D_MODEL].reshape(1, D_MODEL)

    def small(name, l, width=None):
        v = W[name][l]
        return _vec(v, width or v.shape[-1])

    def wq_vec(l):
        return _vec(jnp.concatenate([W['q_nope_norm_w'][l], W['q_pe_norm_w'][l]]))

    def wk_vec(l):
        return _vec(jnp.concatenate([W['k_nope_norm_w'][l], W['k_pe_norm_w'][l]]))

    def conv_vec(l):
        return jnp.concatenate([conv_full[l], W['conv_b'][l].reshape(1, D_CONV), jnp.zeros((3, D_CONV), F32)], axis=0)

    def ssd_vec(l):
        return jnp.concatenate([_vec(W['dt_bias'][l]), _vec(W['a_log'][l]), _vec(W['d_skip'][l]), jnp.zeros((5, LANES), F32)], axis=0)

    sv = []
    xcur = x0
    h1 = _row_fwd(fn_norm_mod, "norm_mod_f", [(x0, 0, D_MODEL)], [small('norm1_w', 0) + ag1[4][0, 0], mvec(0, 1), mvec(0, 0)],
                  [(D_MODEL, BF16)])[0]
    fin = None
    for l in range(DEPTH):
        if l == 1:
            LW[1] = layer_weights(_ag_direct_wait(ag1[0], ag1[1], ag1[2], ag1[3], xcur, "ag_w1_wait"), sh[1])
        lw = LW[l]
        t = dict(xcur=xcur, h1=h1)
        t['proj'] = proj = _mm(h1, lw['w_in'], 'nn', f"mm_in_{l}")
        t['qa_n'], t['kva_n'] = _row_fwd(fn_lat_norm, f"lat_norm_f{l}", [(proj, 0, 256), (proj, 2, 128)],
                                         [small('q_a_norm_w', l), small('kv_a_norm_w', l)], [(256, BF16), (128, BF16)])
        t['q'] = _mm(t['qa_n'], lw['w_q_up'], 'nn', f"mm_q_{l}")
        t['kv'] = _mm(t['kva_n'], lw['w_kv_up'], 'nn', f"mm_kv_{l}")
        t['qf'], t['kf'], t['vv'] = _row_fwd(
            fn_qk_prep, f"qk_prep_f{l}",
            [(t['q'], 0, 1024), (t['kv'], 0, 1024), (proj, 3, 128), (tab_c, 0, 128), (tab_s1, 0, 128), (tab_s2, 0, 128)],
            [wq_vec(l), wk_vec(l)], [(1024, BF16), (1024, BF16), (1024, BF16)])
        t['ao'], t['lse'] = _attn_fwd(t['qf'], t['kf'], t['vv'], f"attn_f{l}")
        t['xact'] = _conv_fwd(proj, conv_vec(l), f"conv_f{l}")
        t['y'], t['states'] = _ssd_fwd(t['xact'], proj, ssd_vec(l), f"ssd_f{l}")
        t['mix'] = _row_fwd(fn_gated_mix, f"gated_f{l}", [(t['y'], 0, 512), (proj, 1, 512), (t['ao'], 0, 512)],
                            [small('ssd_norm_w', l)], [(1024, BF16)])[0]
        t['mo'] = _mm(t['mix'], lw['w_out'], 'nn', f"mm_out_{l}")
        t['x1'], t['h2'] = _row_fwd(fn_resid_norm, f"resid_mid_f{l}", [(xcur, 0, D_MODEL), (t['mo'], 0, D_MODEL)],
                                    [mvec(l, 2), small('norm2_w', l), mvec(l, 4), mvec(l, 3)],
                                    [(D_MODEL, F32), (D_MODEL, BF16)])
        t['gu'] = _mm(t['h2'], lw['w_gate_up'], 'nn', f"mm_gu_{l}", stack='b')
        t['act'] = _row_fwd(fn_swiglu, f"swiglu_f{l}", [(t['gu'], 0, 2 * D_FF)], [], [(D_FF, BF16)], tm=128)[0]
        t['ff'] = _mm(t['act'], lw['w_down'], 'nn', f"mm_down_{l}")
        if l + 1 < DEPTH:
            xcur, h1 = _row_fwd(fn_resid_norm, f"resid_end_f{l}", [(t['x1'], 0, D_MODEL), (t['ff'], 0, D_MODEL)],
                                [mvec(l, 5), small('norm1_w', l + 1), mvec(l + 1, 1), mvec(l + 1, 0)],
                                [(D_MODEL, F32), (D_MODEL, BF16)])
        else:
            fin = _final(t['x1'], t['ff'], tgt, mvec(l, 5), "final_loss")
        sv.append(t)

    dx1, dff, dg2_last, loss_acc = fin
    gfull = {n: [None] * DEPTH for n in BIG}
    gsm = {n: [None] * DEPTH for (n, _) in SMALL}
    dmod = [[None] * 6 for _ in range(DEPTH)]
    dmod[DEPTH - 1][5] = dg2_last
    grad_x = None
    pending = []

    def rs_begin(l, names, tag):
        g4 = [gfull[n][l].reshape(4, 2, gfull[n][l].shape[1] // 2, gfull[n][l].shape[2]) for n in names]
        sib = _rs_sibling_list(g4, f"rs_sibling_{tag}")
        cs = _sum_sibling(g4, sib, ci, f"sum_sibling_{tag}")
        h = _rs_chips_start(cs, f"rs_chips_start_{tag}")
        pending.append((l, names, h))
        return h[4][0, 0]

    tie_l1 = tie_l0a = 0.0

    for l in reversed(range(DEPTH)):
        t = sv[l]
        lw = LW[l]
        proj = t['proj']
        dact = _mm(dff, lw['w_down'], 'nt', f"mm_down_dx{l}")
        gfull['w_down'][l] = _mm(t['act'], dff, 'tn', f"mm_down_dw{l}").reshape(4, D_FF // 4, D_MODEL)
        dgu = _row_bwd(fn_swiglu, f"swiglu_b{l}", [(t['gu'], 0, 2 * D_FF)], [], [(dact, 0, D_FF)], [0], [], tm=128)[0]
        dh2 = _mm(dgu, lw['w_gate_up'], 'nt', f"mm_gu_dx{l}", stack='b')
        gfull['w_gate_up'][l] = _mm(t['h2'], dgu, 'tn', f"mm_gu_dw{l}", stack='out')
        if l == 0:
            tie_l0a = rs_begin(0, EARLY, "l0a")
        dxc, dmo, dmod[l][2], gsm['norm2_w'][l], dmod[l][4], dmod[l][3] = _row_bwd(
            fn_resid_norm, f"resid_mid_b{l}", [(t['xcur'], 0, D_MODEL), (t['mo'], 0, D_MODEL)],
            [mvec(l, 2) + (tie_l1 if l == 0 else 0.0), small('norm2_w', l), mvec(l, 4), mvec(l, 3)],
            [(dx1, 0, D_MODEL), (dh2, 0, D_MODEL)], [0, 1], [0, 1, 2, 3])
        dmix = _mm(dmo, lw['w_out'], 'nt', f"mm_out_dx{l}")
        gfull['w_out'][l] = _mm(t['mix'], dmo, 'tn', f"mm_out_dw{l}").reshape(4, D_MODEL // 4, D_MODEL)
        dy, dz, gsm['ssd_norm_w'][l] = _row_bwd(fn_gated_norm, f"gated_b{l}", [(t['y'], 0, 512), (proj, 1, 512)],
                                                [small('ssd_norm_w', l)], [(dmix, 1, 512)], [0, 1], [0])
        dxact, ddt, dsv = _ssd_bwd(t['xact'], proj, ssd_vec(l) + (tie_l0a if l == 0 else 0.0), t['states'], dy, f"ssd_b{l}")
        gsm['dt_bias'][l], gsm['a_log'][l], gsm['d_skip'][l] = dsv[0, :8], dsv[1, :8], dsv[2, :8]
        dxbc, dcv = _conv_bwd(proj, conv_vec(l), dxact, f"conv_b{l}")
        gsm['conv_w'][l] = dcv[:CONV_TAPS]
        gsm['conv_b'][l] = dcv[CONV_TAPS]
        delta = _row_fwd(fn_attn_delta, f"attn_delta{l}", [(dmix, 0, 512), (t['ao'], 0, 512)], [], [(512, F32)])[0]
        lse_r = (t['lse'][:, ::64] * LOG2E).T.reshape(HEADS // 2, 2, S)
        delta_r = delta[:, ::64].T.reshape(HEADS // 2, 2, S)
        dqT, dkf, dvv = _attn_bwd(t['qf'], t['kf'], t['kf'].T, t['vv'], dmix, lse_r, delta_r, f"attn_b{l}")
        dqf = dqT.T
        dq, dkv, dkpe, dwq, dwk = _row_bwd(
            fn_qk_prep, f"qk_prep_b{l}",
            [(t['q'], 0, 1024), (t['kv'], 0, 1024), (proj, 3, 128), (tab_c, 0, 128), (tab_s1, 0, 128), (tab_s2, 0, 128)],
            [wq_vec(l), wk_vec(l)], [(dqf, 0, 1024), (dkf, 0, 1024), (dvv, 0, 1024)], [0, 1, 2], [0, 1])
        gsm['q_nope_norm_w'][l], gsm['q_pe_norm_w'][l] = dwq[0, :NOPE], dwq[0, NOPE:QK_DIM]
        gsm['k_nope_norm_w'][l], gsm['k_pe_norm_w'][l] = dwk[0, :NOPE], dwk[0, NOPE:QK_DIM]
        dqa_n = _mm(dq, lw['w_q_up'], 'nt', f"mm_q_dx{l}")
        gfull['w_q_up'][l] = _cols_to_shards(_unpad_wq(_mm(t['qa_n'], dq, 'tn', f"mm_q_dw{l}")))
        dkva_n = _mm(dkv, lw['w_kv_up'], 'nt', f"mm_kv_dx{l}")
        gfull['w_kv_up'][l] = _cols_to_shards(_mm(t['kva_n'], dkv, 'tn', f"mm_kv_dw{l}"))
        dqa, dkva, dqw, dkvw = _row_bwd(fn_lat_norm, f"lat_norm_b{l}", [(proj, 0, 256), (proj, 2, 128)],
                                        [small('q_a_norm_w', l), small('kv_a_norm_w', l)],
                                        [(dqa_n, 0, 256), (dkva_n, 0, 128)], [0, 1], [0, 1])
        gsm['q_a_norm_w'][l], gsm['kv_a_norm_w'][l] = dqw[0], dkvw[0]
        dproj = jnp.concatenate([dqa, dkva, dkpe, dz, dxbc, ddt], axis=1)
        dh1 = _mm(dproj, lw['w_in'], 'nt', f"mm_in_dx{l}")
        gfull['w_in'][l] = _cols_to_shards(_unpad_win(_mm(t['h1'], dproj, 'tn', f"mm_in_dw{l}")))
        if l > 0:
            p = sv[l - 1]
            dx1, dff, dmod[l - 1][5], gsm['norm1_w'][l], dmod[l][1], dmod[l][0] = _row_bwd(
                fn_resid_norm, f"resid_end_b{l - 1}", [(p['x1'], 0, D_MODEL), (p['ff'], 0, D_MODEL)],
                [mvec(l - 1, 5), small('norm1_w', l), mvec(l, 1), mvec(l, 0)], [(dxc, 0, D_MODEL), (dh1, 0, D_MODEL)],
                [0, 1], [0, 1, 2, 3])
            tie_l1 = rs_begin(l, BIG, f"l{l}")
        else:
            grad_x, gsm['norm1_w'][l], dmod[l][1], dmod[l][0] = _row_bwd(
                fn_norm_mod_pass, "norm_mod_b", [(x0, 0, D_MODEL)], [small('norm1_w', 0), mvec(0, 1), mvec(0, 0)],
                [(dxc, 0, D_MODEL), (dh1, 0, D_MODEL)], [0], [0, 1, 2])
        for n in ('norm1_w', 'norm2_w', 'ssd_norm_w'):
            gsm[n][l] = gsm[n][l][0]

    for l in range(DEPTH):
        gsm['b_ada'][l] = jnp.concatenate([d[0] for d in dmod[l]])
    sm_part = _pack_small({n: jnp.stack(v) for n, v in gsm.items()}).at[SMALL_ROWS - 1, 0].set(loss_acc[0, 0])
    sm_all = _ag8(sm_part, "ag_small")
    loss = jnp.sum(sm_all.reshape(8, SMALL_ROWS, LANES)[:, SMALL_ROWS - 1, 0])
    sm_all, late = lax.optimization_barrier((sm_all, [gfull[n][0] for n in BIG[2:]]))
    for n, g in zip(BIG[2:], late):
        gfull[n][0] = g
    tie_l0b = rs_begin(0, BIG[2:], "l0b")

    def with_conv(tree):
        wide = lax.dynamic_update_slice_in_dim(jnp.zeros((DEPTH, CONV_TAPS, D_CONV), F32), tree['conv_w'], chip * 256, axis=2)
        return {**tree, 'conv_w': wide}

    g_sm, d_sm, m_sm, v_sm = _adam(_pack_small(with_conv(W)) + tie_l0b, _pack_small(with_conv(M)), _pack_small(with_conv(V)),
                                   [(sm_all, d * SMALL_ROWS) for d in range(8)], "adam_small")
    out_small = [_unpack_small(b) for b in (g_sm, d_sm, m_sm, v_sm)]
    for o in out_small:
        o['conv_w'] = lax.dynamic_slice_in_dim(o['conv_w'].reshape(DEPTH, CONV_TAPS, D_CONV), chip * 256, 256, axis=2)

    dmod_all = sm_all.reshape(8, SMALL_ROWS * LANES)
    per_layer = sum(k + (-k % LANES) for (_, k) in SMALL)
    dmod_sh = jnp.stack([lax.dynamic_slice_in_dim(dmod_all[:, l * per_layer:l * per_layer + 6 * D_MODEL], chip * 1536, 1536, axis=1)
                         for l in range(DEPTH)])
    g_ada = _ada_bwd(c_all.T, dmod_sh, "ada_bwd")
    ada = _adam(W['w_ada'].reshape(DEPTH * D_MODEL, 1536), M['w_ada'].reshape(DEPTH * D_MODEL, 1536),
                V['w_ada'].reshape(DEPTH * D_MODEL, 1536), [(g_ada.reshape(DEPTH * D_MODEL, 1536), 0)], "adam_ada")
    out_ada = [a.reshape(DEPTH, D_MODEL, 1536) for a in ada]

    keys, cs_all, land_all = [], [], []
    for (l, names, (send_sems, recv_sems, cs_thru, land_thru, _)) in pending:
        cs, lands = _rs_chips_wait(send_sems, recv_sems, cs_thru, land_thru, ada[3], f"rs_chips_wait_l{l}{len(names)}")
        keys += [(l, n) for n in names]
        cs_all += list(cs)
        land_all += list(lands)
    ghalf = _sum_chips(cs_all, land_all, chip, "sum_chips")
    gother = _swap_list(ghalf, "swap_halves")
    gshard = {k: jnp.where(ci == 0, jnp.concatenate([a, b]), jnp.concatenate([b, a])) for k, a, b in zip(keys, ghalf, gother)}
    g_big = [jnp.stack([gshard[(l, n)] for l in range(DEPTH)]) for n in BIG]
    d_big, m_big, v_big = _adam_multi([W[n] for n in BIG], [M[n] for n in BIG], [V[n] for n in BIG], g_big, "adam_big")
    out_big = [dict(zip(BIG, o)) for o in (g_big, d_big, m_big, v_big)]

    outs = [loss, grad_x[None]]
    for k in range(4):
        for n in WEIGHTS:
            if n == 'w_ada':
                outs.append(out_ada[k])
            elif n in BIG:
                outs.append(out_big[k][n])
            else:
                outs.append(out_small[k][n])
    return tuple(outs)
```

```python
import functools

import jax
import jax.numpy as jnp
from jax import lax
from jax.experimental import pallas as pl
from jax.experimental.pallas import tpu as pltpu

F32 = jnp.float32
BF16 = jnp.bfloat16
MESH = pl.DeviceIdType.MESH

D_MODEL = 1024
DEPTH = 2
HEADS = 8
NOPE = 64
ROPE = 32
QK_DIM = NOPE + ROPE
Q_LORA = 256
KV_LORA = 128
SSD_HEADS = 8
SSD_P = 64
SSD_N = 128
CHUNK = 128
CONV_TAPS = 4
D_SSD = 512
D_CONV = 1024
D_FF = 2816
D_IN = 1960
D_IN_PAD = 2176
EPS = 1e-6
ROPE_THETA = 10000.0
ATT_SCALE = QK_DIM ** -0.5
NEG = -1e30
LANES = 128
VMEM_LIMIT = 48 * 1024 * 1024

ADAM_LR, ADAM_B1, ADAM_B2, ADAM_EPS, ADAM_WD, ADAM_STEP = 0.001, 0.9, 0.999, 1e-08, 0.01, 10

WEIGHTS = ['norm1_w', 'norm2_w', 'w_ada', 'b_ada', 'w_in', 'q_a_norm_w', 'w_q_up', 'kv_a_norm_w', 'w_kv_up',
           'q_nope_norm_w', 'q_pe_norm_w', 'k_nope_norm_w', 'k_pe_norm_w', 'conv_w', 'conv_b', 'dt_bias', 'a_log',
           'd_skip', 'ssd_norm_w', 'w_out', 'w_gate_up', 'w_down']
BIG = ['w_down', 'w_gate_up', 'w_out', 'w_kv_up', 'w_q_up', 'w_in']
EARLY = BIG[:2]
SMALL = [('b_ada', 6144), ('conv_w', 4096), ('norm1_w', 1024), ('norm2_w', 1024), ('conv_b', 1024), ('ssd_norm_w', 512),
         ('q_a_norm_w', 256), ('kv_a_norm_w', 128), ('q_nope_norm_w', 64), ('q_pe_norm_w', 32),
         ('k_nope_norm_w', 64), ('k_pe_norm_w', 32), ('dt_bias', 8), ('a_log', 8), ('d_skip', 8)]
SMALL_ROWS = 240


def _cp(sem=None, **kw):
    return pltpu.CompilerParams(dimension_semantics=sem, vmem_limit_bytes=VMEM_LIMIT, **kw)


def _dot(a, b, dims, prec=None):
    return lax.dot_general(a, b, (dims, ((), ())), preferred_element_type=F32, precision=prec)


def _tile(dim, target):
    best = 0
    for t in range(LANES, min(dim, target) + 1, LANES):
        if dim % t == 0:
            best = t
    if best < 256 and dim <= 2304:
        return dim
    return best


def _mm(a, b, mode, name, out_dtype=F32, stack=None):
    ns = None
    if stack == 'b':
        ns = b.shape[2]
        if mode == 'nn':
            (M, K), N = a.shape, 4 * ns
        else:
            (M, K), N = a.shape, b.shape[1]
    elif mode == 'nn':
        (M, K), (_, N) = a.shape, b.shape
    elif mode == 'nt':
        (M, K), (N, _) = a.shape, b.shape
    else:
        (K, M), (_, N) = a.shape, b.shape
    if stack == 'out':
        ns = N // 4
    tm, tn, tk = _tile(M, 1408 if mode == 'tn' else 1024), _tile(N, 1408), _tile(K, 1408)
    if stack == 'b' and mode == 'nt':
        tk = ns
    elif stack is not None:
        tn = ns
    nk = K // tk
    dims = {'nn': ((1,), (0,)), 'nt': ((1,), (1,)), 'tn': ((0,), (0,))}[mode]

    def body(a_ref, b_ref, o_ref, *acc):
        part = _dot(a_ref[...].astype(BF16), b_ref[...].astype(BF16), dims)
        if nk == 1:
            o_ref[...] = part.astype(o_ref.dtype)
            return
        k = pl.program_id(2)

        @pl.when(k == 0)
        def _():
            acc[0][...] = part

        @pl.when(k > 0)
        def _():
            acc[0][...] += part

        @pl.when(k == nk - 1)
        def _():
            o_ref[...] = acc[0][...].astype(o_ref.dtype)

    a_spec = pl.BlockSpec((tk, tm), lambda i, j, k: (k, i)) if mode == 'tn' else pl.BlockSpec((tm, tk), lambda i, j, k: (i, k))
    b_spec = pl.BlockSpec((tn, tk), lambda i, j, k: (j, k)) if mode == 'nt' else pl.BlockSpec((tk, tn), lambda i, j, k: (k, j))
    o_spec, o_shape = pl.BlockSpec((tm, tn), lambda i, j, k: (i, j)), (M, N)
    if stack == 'b':
        b_spec = (pl.BlockSpec((None, tn, ns), lambda i, j, k: (k, j, 0)) if mode == 'nt'
                  else pl.BlockSpec((None, tk, ns), lambda i, j, k: (j, k, 0)))
    if stack == 'out':
        o_spec, o_shape = pl.BlockSpec((None, tm, ns), lambda i, j, k: (j, i, 0)), (4, M, ns)
    return pl.pallas_call(
        body, name=name, grid=(M // tm, N // tn, nk),
        in_specs=[a_spec, b_spec], out_specs=o_spec,
        out_shape=jax.ShapeDtypeStruct(o_shape, out_dtype),
        scratch_shapes=[pltpu.VMEM((tm, tn), F32)] if nk > 1 else [],
        compiler_params=_cp(("parallel", "parallel", "arbitrary")),
    )(a, b)


def _rspec(tm, w, cb):
    return pl.BlockSpec((tm, w), lambda i: (i, cb))


def _vspec(shape):
    return pl.BlockSpec(shape, lambda i: (0,) * len(shape))


def _row_fwd(fn, name, rows, vecs, outs, tm=256):
    S = rows[0][0].shape[0]
    tm = min(tm, S)
    nin = len(rows) + len(vecs)

    def body(*refs):
        res = fn(*[r[...] for r in refs[:nin]])
        for o_ref, r in zip(refs[nin:], res):
            o_ref[...] = r.astype(o_ref.dtype)

    return pl.pallas_call(
        body, name=name, grid=(S // tm,),
        in_specs=[_rspec(tm, w, cb) for (_, cb, w) in rows] + [_vspec(v.shape) for v in vecs],
        out_specs=[_rspec(tm, w, 0) for (w, _) in outs],
        out_shape=[jax.ShapeDtypeStruct((S, w), dt) for (w, dt) in outs],
        compiler_params=_cp(("parallel",)),
    )(*[r[0] for r in rows], *vecs)


def _row_bwd(fn, name, rows, vecs, cts, drows, dvecs, tm=256, ddtypes=None):
    S = rows[0][0].shape[0]
    ddtypes = ddtypes or [F32] * len(drows)
    tm = min(tm, S)
    nr, nv, nc = len(rows), len(vecs), len(cts)
    didx = list(drows) + [nr + j for j in dvecs]

    def body(*refs):
        vals = [r[...] for r in refs[:nr + nv]]
        ct = tuple(r[...].astype(F32) for r in refs[nr + nv:nr + nv + nc])
        outs = refs[nr + nv + nc:]

        def g(*d):
            a = list(vals)
            for k, val in zip(didx, d):
                a[k] = val
            return tuple(fn(*a))

        _, vjp = jax.vjp(g, *[vals[k] for k in didx])
        grads = vjp(ct)
        for o, gr in zip(outs[:len(drows)], grads[:len(drows)]):
            o[...] = gr.astype(o.dtype)

        @pl.when(pl.program_id(0) == 0)
        def _():
            for o in outs[len(drows):]:
                o[...] = jnp.zeros_like(o)

        for o, gr in zip(outs[len(drows):], grads[len(drows):]):
            o[...] += gr

    return pl.pallas_call(
        body, name=name, grid=(S // tm,),
        in_specs=[_rspec(tm, w, cb) for (_, cb, w) in rows] + [_vspec(v.shape) for v in vecs]
        + [_rspec(tm, w, cb) for (_, cb, w) in cts],
        out_specs=[_rspec(tm, rows[k][2], 0) for k in drows] + [_vspec(vecs[j].shape) for j in dvecs],
        out_shape=[jax.ShapeDtypeStruct((S, rows[k][2]), dt) for k, dt in zip(drows, ddtypes)]
        + [jax.ShapeDtypeStruct(vecs[j].shape, F32) for j in dvecs],
        compiler_params=_cp(("arbitrary",)),
    )(*[r[0] for r in rows], *vecs, *[c[0] for c in cts])


def _rms(x):
    return x * lax.rsqrt(jnp.mean(x * x, axis=-1, keepdims=True) + EPS)


def fn_norm_mod(x, nw, sc, sh):
    return (_rms(x) * nw * (1.0 + sc) + sh,)


def fn_norm_mod_pass(x, nw, sc, sh):
    return (x, _rms(x) * nw * (1.0 + sc) + sh)


def fn_resid_norm(x, d, g, nw, sc, sh):
    xn = x + g * d
    return (xn, _rms(xn) * nw * (1.0 + sc) + sh)


def fn_lat_norm(qa, kva, qw, kvw):
    return (_rms(qa) * qw, _rms(kva) * kvw)


@functools.partial(jax.custom_vjp, nondiff_argnums=(1,))
def _lroll(x, s):
    return pltpu.roll(x, s, 1)


def _lroll_fwd(x, s):
    return pltpu.roll(x, s, 1), None


def _lroll_bwd(s, _, g):
    return (pltpu.roll(g, (LANES - s) % LANES, 1),)


_lroll.defvjp(_lroll_fwd, _lroll_bwd)


def _lane_masks(shape):
    lane = lax.broadcasted_iota(jnp.int32, shape, 1)
    return (lane < NOPE).astype(F32), ((lane >= NOPE) & (lane < QK_DIM)).astype(F32)


def _rope(t, tc, ts1, ts2):
    return t * tc + _lroll(t, 16) * ts1 + _lroll(t, LANES - 16) * ts2


def fn_qk_prep(q, kv, kpe, tc, ts1, ts2, wq, wk):
    mn, mp = _lane_masks((1, LANES))
    mhi = 1.0 - mn

    def head_norm(t, w):
        rn = lax.rsqrt(jnp.sum(t * t * mn, axis=-1, keepdims=True) * (1.0 / NOPE) + EPS)
        rp = lax.rsqrt(jnp.sum(t * t * mp, axis=-1, keepdims=True) * (1.0 / ROPE) + EPS)
        return t * (rn * mn + rp * mp) * w

    kp = _rope(head_norm(_lroll(kpe, NOPE), wk) * mp, tc, ts1, ts2)
    qs, ks, vs = [], [], []
    for h in range(HEADS):
        qs.append(_rope(head_norm(q[:, h * LANES:(h + 1) * LANES], wq), tc, ts1, ts2))
        t = kv[:, h * LANES:(h + 1) * LANES]
        ks.append(head_norm(t, wk) * mn + kp)
        vs.append(_lroll(t, NOPE) * mn + mhi)
    return (jnp.concatenate(qs, axis=1), jnp.concatenate(ks, axis=1), jnp.concatenate(vs, axis=1))


def fn_gated_norm(y, z, w):
    u = y * jax.nn.silu(z)
    half = D_SSD // 2
    return (jnp.concatenate([_rms(u[:, :half]), _rms(u[:, half:])], axis=1) * w,)


def fn_gated_mix(y, z, ao, w):
    return (jnp.concatenate([ao, fn_gated_norm(y, z, w)[0]], axis=1),)


def fn_swiglu(gu):
    return (jax.nn.silu(gu[:, :D_FF]) * gu[:, D_FF:],)


def _final(x1, ff, tgt, g2, name):
    S = x1.shape[0]
    tm = min(256, S)

    def body(x_ref, f_ref, t_ref, g_ref, dx_ref, df_ref, dg_ref, l_ref):
        @pl.when(pl.program_id(0) == 0)
        def _():
            dg_ref[...] = jnp.zeros_like(dg_ref)
            l_ref[...] = jnp.zeros_like(l_ref)

        f = f_ref[...]
        g = g_ref[...]
        e = x_ref[...] + g * f - t_ref[...]
        dx = e * (1.0 / D_MODEL)
        dx_ref[...] = dx
        df_ref[...] = (g * dx).astype(df_ref.dtype)
        dg_ref[...] += jnp.sum(dx * f, axis=0, keepdims=True)
        l_ref[...] += jnp.sum(e * e) * (0.5 / D_MODEL)

    r = _rspec(tm, D_MODEL, 0)
    return pl.pallas_call(
        body, name=name, grid=(S // tm,),
        in_specs=[r, r, r, _vspec((1, D_MODEL))],
        out_specs=[r, r, _vspec((1, D_MODEL)), _vspec((1, LANES))],
        out_shape=[jax.ShapeDtypeStruct((S, D_MODEL), F32), jax.ShapeDtypeStruct((S, D_MODEL), BF16),
                   jax.ShapeDtypeStruct((1, D_MODEL), F32), jax.ShapeDtypeStruct((1, LANES), F32)],
        compiler_params=_cp(("arbitrary",)),
    )(x1, ff, tgt, g2)


def _causal_mask(t):
    r = lax.broadcasted_iota(jnp.int32, (t, t), 0)
    c = lax.broadcasted_iota(jnp.int32, (t, t), 1)
    return c <= r


LOG2E = 1.4426950408889634
EXP2_SCALE = ATT_SCALE * LOG2E
ATT_TQ, ATT_TK = 512, 1024
ATT_BQ, ATT_BK = 1024, 512


def _attn_fwd(qf, kf, va, name):
    S = qf.shape[0]
    T, TK = min(ATT_TQ, S), min(ATT_TK, S)
    nmask = max(1, T // TK)

    def body(q_ref, k_ref, v_ref, o_ref, l_ref):
        i = pl.program_id(1)
        r = lax.broadcasted_iota(jnp.int32, (T, TK), 0)
        c = lax.broadcasted_iota(jnp.int32, (T, TK), 1)
        qs = [q_ref[:, hh * LANES:(hh + 1) * LANES] for hh in range(2)]

        def blk(j, carry, masked):
            off = pl.multiple_of(j * TK, TK)
            out = []
            for hh in range(2):
                m, acc = carry[hh]
                s = _dot(qs[hh], k_ref[pl.ds(off, TK), hh * LANES:(hh + 1) * LANES], ((1,), (1,)))
                if masked:
                    s = jnp.where(c + j * TK <= r + i * T, s, NEG)
                mn = jnp.maximum(m, jnp.max(s, axis=1, keepdims=True))
                p = jnp.exp2((s - mn) * EXP2_SCALE)
                al = jnp.exp2((m - mn) * EXP2_SCALE)
                vj = v_ref[pl.ds(off, TK), hh * LANES:(hh + 1) * LANES]
                out.append((mn, al * acc + _dot(p.astype(BF16), vj, ((1,), (0,)))))
            return tuple(out)

        one = (jnp.full((T, 1), NEG, F32), jnp.zeros((T, LANES), F32))
        nfull = lax.div(i * T, TK)
        carry = lax.fori_loop(0, nfull, lambda j, cr: blk(j, cr, False), (one, one))
        for t in range(nmask):
            carry = blk(nfull + t, carry, True)
        lane = lax.broadcasted_iota(jnp.int32, (1, LANES), 1)
        z = jnp.zeros((T, LANES), F32)
        for hh in range(2):
            m, acc = carry[hh]
            l = acc[:, 64:65]
            o_ref[:, hh * 64:(hh + 1) * 64] = (acc / l)[:, :64]
            z = z + (m * EXP2_SCALE + jnp.log(l) * LOG2E) * (lane == hh).astype(F32)
        l_ref[0] = z.T[0:2, :]

    return pl.pallas_call(
        body, name=name, grid=(HEADS // 2, S // T),
        in_specs=[pl.BlockSpec((T, 256), lambda h, i: (i, h)), pl.BlockSpec((S, 256), lambda h, i: (0, h)),
                  pl.BlockSpec((S, 256), lambda h, i: (0, h))],
        out_specs=[pl.BlockSpec((T, LANES), lambda h, i: (i, h)), pl.BlockSpec((1, 2, T), lambda h, i: (h, 0, i))],
        out_shape=[jax.ShapeDtypeStruct((S, D_SSD), F32), jax.ShapeDtypeStruct((HEADS // 2, 2, S), F32)],
        compiler_params=_cp(("parallel", "parallel")),
    )(qf, kf, va)


def _attn_delta(dmix, ao, name):
    S = ao.shape[0]
    tm = min(512, S)

    def body(d_ref, o_ref, out_ref):
        lane = lax.broadcasted_iota(jnp.int32, (1, LANES), 1)
        lo = (lane < 64).astype(F32)
        for hp in range(HEADS // 2):
            y = d_ref[:, hp * LANES:(hp + 1) * LANES] * o_ref[:, hp * LANES:(hp + 1) * LANES]
            z = (jnp.sum(y * lo, axis=1, keepdims=True) * (lane == 0).astype(F32)
                 + jnp.sum(y * (1.0 - lo), axis=1, keepdims=True) * (lane == 1).astype(F32))
            out_ref[hp] = z.T[0:2, :]

    return pl.pallas_call(
        body, name=name, grid=(S // tm,),
        in_specs=[pl.BlockSpec((tm, D_SSD), lambda i: (i, 0)), pl.BlockSpec((tm, D_SSD), lambda i: (i, 0))],
        out_specs=pl.BlockSpec((HEADS // 2, 2, tm), lambda i: (0, 0, i)),
        out_shape=jax.ShapeDtypeStruct((HEADS // 2, 2, S), F32),
        compiler_params=_cp(("parallel",)),
    )(dmix, ao)


def _attn_bwd(qf, kf, kT, va, do, lse_r, delta_r, name):
    S = qf.shape[0]
    T, TK = min(ATT_BQ, S), min(ATT_BK, S)
    nq = S // T
    nmask = max(1, TK // T)

    def body(q_ref, k_ref, kT_ref, v_ref, do_ref, l_ref, d_ref, dqT_ref, dk_ref, dv_ref):
        j = pl.program_id(1)

        @pl.when(j == 0)
        def _():
            dqT_ref[...] = jnp.zeros_like(dqT_ref)

        r = lax.broadcasted_iota(jnp.int32, (TK, T), 0)
        c = lax.broadcasted_iota(jnp.int32, (TK, T), 1)
        lo = (lax.broadcasted_iota(jnp.int32, (1, LANES), 1) < 64).astype(F32)
        ks = [k_ref[:, hh * LANES:(hh + 1) * LANES] for hh in range(2)]
        vs = [v_ref[:, hh * LANES:(hh + 1) * LANES] for hh in range(2)]
        kTs = [kT_ref[hh * LANES:(hh + 1) * LANES, :] for hh in range(2)]

        def blk(i, carry, masked):
            off = pl.multiple_of(i * T, T)
            dall = do_ref[pl.ds(off, T), :]
            out = []
            for hh in range(2):
                dk, dv = carry[hh]
                q = q_ref[pl.ds(off, T), hh * LANES:(hh + 1) * LANES]
                dop = ((dall if hh == 0 else pltpu.roll(dall, 64, 1)) * lo).astype(BF16)
                lrow = l_ref[0, hh:hh + 1, pl.ds(off, T)]
                drow = d_ref[0, hh:hh + 1, pl.ds(off, T)]
                pT = jnp.exp2(_dot(ks[hh], q, ((1,), (1,))) * EXP2_SCALE - lrow)
                if masked:
                    pT = jnp.where(r + j * TK <= c + i * T, pT, 0.0)
                dpT = _dot(vs[hh], dop, ((1,), (1,)))
                dsT = (pT * (dpT - drow) * ATT_SCALE).astype(BF16)
                dv = dv + _dot(pT.astype(BF16), dop, ((1,), (0,)))
                dk = dk + _dot(dsT, q, ((1,), (0,)))
                dqT_ref[hh * LANES:(hh + 1) * LANES, pl.ds(off, T)] += _dot(kTs[hh], dsT, ((1,), (0,)))
                out.append((dk, dv))
            return tuple(out)

        z = (jnp.zeros((TK, LANES), F32), jnp.zeros((TK, LANES), F32))
        first = lax.div(j * TK, T)
        carry = (z, z)
        for t in range(nmask):
            carry = blk(first + t, carry, True)
        carry = lax.fori_loop(first + nmask, nq, lambda i, cr: blk(i, cr, False), carry)
        for hh in range(2):
            dk_ref[:, hh * LANES:(hh + 1) * LANES] = carry[hh][0]
            dv_ref[:, hh * LANES:(hh + 1) * LANES] = carry[hh][1]

    return pl.pallas_call(
        body, name=name, grid=(HEADS // 2, S // TK),
        in_specs=[pl.BlockSpec((S, 256), lambda h, j: (0, h)), pl.BlockSpec((TK, 256), lambda h, j: (j, h)),
                  pl.BlockSpec((256, TK), lambda h, j: (h, j)), pl.BlockSpec((TK, 256), lambda h, j: (j, h)),
                  pl.BlockSpec((S, LANES), lambda h, j: (0, h)), pl.BlockSpec((1, 2, S), lambda h, j: (h, 0, 0)),
                  pl.BlockSpec((1, 2, S), lambda h, j: (h, 0, 0))],
        out_specs=[pl.BlockSpec((256, S), lambda h, j: (h, 0)), pl.BlockSpec((TK, 256), lambda h, j: (j, h)),
                   pl.BlockSpec((TK, 256), lambda h, j: (j, h))],
        out_shape=[jax.ShapeDtypeStruct((D_MODEL, S), F32), jax.ShapeDtypeStruct((S, D_MODEL), F32),
                   jax.ShapeDtypeStruct((S, D_MODEL), F32)],
        compiler_params=_cp(("parallel", "arbitrary")),
    )(qf, kf, kT, va, do, lse_r, delta_r)


def _shift_down(x, s):
    if s == 0:
        return x
    rows = lax.broadcasted_iota(jnp.int32, x.shape, 0)
    return jnp.where(rows >= s, pltpu.roll(x, s, 0), 0.0)


def _shift_up(x, s):
    if s == 0:
        return x
    n = x.shape[0]
    rows = lax.broadcasted_iota(jnp.int32, x.shape, 0)
    return jnp.where(rows < n - s, pltpu.roll(x, n - s, 0), 0.0)


def _conv_fwd(proj, cvec, name):
    S = proj.shape[0]

    def body(x_ref, c_ref, o_ref):
        x = x_ref[...]
        y = jnp.broadcast_to(c_ref[4:5, :], x.shape)
        for k in range(CONV_TAPS):
            y = y + c_ref[k:k + 1, :] * _shift_down(x, CONV_TAPS - 1 - k)
        o_ref[...] = y * jax.nn.sigmoid(y)

    return pl.pallas_call(
        body, name=name, grid=(D_CONV // LANES,),
        in_specs=[pl.BlockSpec((S, LANES), lambda j: (0, 8 + j)), pl.BlockSpec((8, LANES), lambda j: (0, j))],
        out_specs=pl.BlockSpec((S, LANES), lambda j: (0, j)),
        out_shape=jax.ShapeDtypeStruct((S, D_CONV), F32),
        compiler_params=_cp(("parallel",)),
    )(proj, cvec)


def _conv_bwd(proj, cvec, dact, name):
    S = proj.shape[0]

    def body(x_ref, c_ref, d_ref, dx_ref, dc_ref):
        x = x_ref[...]
        y = jnp.broadcast_to(c_ref[4:5, :], x.shape)
        for k in range(CONV_TAPS):
            y = y + c_ref[k:k + 1, :] * _shift_down(x, CONV_TAPS - 1 - k)
        sg = jax.nn.sigmoid(y)
        dy = d_ref[...] * (sg * (1.0 + y * (1.0 - sg)))
        dx = jnp.zeros_like(x)
        for k in range(CONV_TAPS):
            s = CONV_TAPS - 1 - k
            dx = dx + c_ref[k:k + 1, :] * _shift_up(dy, s)
            dc_ref[k:k + 1, :] = jnp.sum(dy * _shift_down(x, s), axis=0, keepdims=True)
        dx_ref[...] = dx
        dc_ref[4:5, :] = jnp.sum(dy, axis=0, keepdims=True)
        dc_ref[5:8, :] = jnp.zeros((3, LANES), F32)

    return pl.pallas_call(
        body, name=name, grid=(D_CONV // LANES,),
        in_specs=[pl.BlockSpec((S, LANES), lambda j: (0, 8 + j)), pl.BlockSpec((8, LANES), lambda j: (0, j)),
                  pl.BlockSpec((S, LANES), lambda j: (0, j))],
        out_specs=[pl.BlockSpec((S, LANES), lambda j: (0, j)), pl.BlockSpec((8, LANES), lambda j: (0, j))],
        out_shape=[jax.ShapeDtypeStruct((S, D_CONV), F32), jax.ShapeDtypeStruct((8, D_CONV), F32)],
        compiler_params=_cp(("parallel",)),
    )(proj, cvec, dact)


def fn_ssd_chunk(xs, bm, cm, dtr, state, vecs):
    Q = CHUNK
    dt = jax.nn.softplus(dtr + vecs[0:1])
    a = -jnp.exp(vecs[1:2])
    d_skip = vecs[2:3]
    adt = dt * a
    tril = _causal_mask(Q)
    acs = _dot(tril.astype(F32), adt, ((1,), (0,)), lax.Precision.HIGHEST)
    acs_t = acs.T
    ys, new_states = [], []
    Bs = [bm[:, g * SSD_N:(g + 1) * SSD_N].astype(BF16) for g in range(2)]
    Cs = [cm[:, g * SSD_N:(g + 1) * SSD_N].astype(BF16) for g in range(2)]
    Gs = [_dot(Cs[g], Bs[g], ((1,), (1,))) for g in range(2)]
    for h in range(SSD_HEADS):
        g = h // (SSD_HEADS // 2)
        x = xs[:, h * SSD_P:(h + 1) * SSD_P]
        B, C = Bs[g], Cs[g]
        xdt = x * dt[:, h:h + 1]
        A = acs[:, h:h + 1]
        L = jnp.exp(jnp.where(tril, A - acs_t[h:h + 1, :], -jnp.inf))
        M = Gs[g] * L
        yd = _dot(M.astype(BF16), xdt.astype(BF16), ((1,), (0,)))
        st = state[h]
        yo = _dot(C, st.astype(BF16), ((1,), (1,))) * jnp.exp(A)
        alast = acs[Q - 1:Q, h:h + 1]
        U = xdt * jnp.exp(alast - A)
        new_states.append(jnp.exp(alast) * st + _dot(U.astype(BF16), B, ((0,), (0,))))
        ys.append(yd + yo + d_skip[:, h:h + 1] * x)
    return jnp.concatenate(ys, axis=1), jnp.stack(new_states)


def _ssd_fwd(xact, proj, svec, name):
    S = xact.shape[0]
    nc = S // CHUNK

    def body(x_ref, dt_ref, v_ref, y_ref, st_ref, state):
        @pl.when(pl.program_id(0) == 0)
        def _():
            state[...] = jnp.zeros_like(state)

        st_ref[0] = state[...]
        x = x_ref[...]
        y, sn = fn_ssd_chunk(x[:, 0:512], x[:, 512:768], x[:, 768:1024], dt_ref[...], state[...], v_ref[...])
        y_ref[...] = y
        state[...] = sn

    return pl.pallas_call(
        body, name=name, grid=(nc,),
        in_specs=[pl.BlockSpec((CHUNK, D_CONV), lambda i: (i, 0)), pl.BlockSpec((CHUNK, LANES), lambda i: (i, 16)),
                  pl.BlockSpec((8, LANES), lambda i: (0, 0))],
        out_specs=[pl.BlockSpec((CHUNK, D_SSD), lambda i: (i, 0)),
                   pl.BlockSpec((1, SSD_HEADS, SSD_P, SSD_N), lambda i: (i, 0, 0, 0))],
        out_shape=[jax.ShapeDtypeStruct((S, D_SSD), F32), jax.ShapeDtypeStruct((nc, SSD_HEADS, SSD_P, SSD_N), F32)],
        scratch_shapes=[pltpu.VMEM((SSD_HEADS, SSD_P, SSD_N), F32)],
        compiler_params=_cp(("arbitrary",)),
    )(xact, proj, svec)


def _ssd_bwd(xact, proj, svec, states, dy, name):
    S = xact.shape[0]
    nc = S // CHUNK

    def body(x_ref, dt_ref, v_ref, st_ref, dy_ref, dx_ref, ddt_ref, dv_ref, dstate):
        @pl.when(pl.program_id(0) == 0)
        def _():
            dstate[...] = jnp.zeros_like(dstate)
            dv_ref[...] = jnp.zeros_like(dv_ref)

        x = x_ref[...]
        _, vjp = jax.vjp(fn_ssd_chunk, x[:, 0:512], x[:, 512:768], x[:, 768:1024], dt_ref[...], st_ref[0], v_ref[...])
        dxs, dbm, dcm, ddt, dst, dvec = vjp((dy_ref[...], dstate[...]))
        dx_ref[:, 0:512] = dxs
        dx_ref[:, 512:768] = dbm
        dx_ref[:, 768:1024] = dcm
        ddt_ref[...] = ddt
        dstate[...] = dst
        dv_ref[...] += dvec

    rev = lambda i: (nc - 1 - i, 0)
    return pl.pallas_call(
        body, name=name, grid=(nc,),
        in_specs=[pl.BlockSpec((CHUNK, D_CONV), rev), pl.BlockSpec((CHUNK, LANES), lambda i: (nc - 1 - i, 16)),
                  pl.BlockSpec((8, LANES), lambda i: (0, 0)),
                  pl.BlockSpec((1, SSD_HEADS, SSD_P, SSD_N), lambda i: (nc - 1 - i, 0, 0, 0)),
                  pl.BlockSpec((CHUNK, D_SSD), rev)],
        out_specs=[pl.BlockSpec((CHUNK, D_CONV), rev), pl.BlockSpec((CHUNK, LANES), rev),
                   pl.BlockSpec((8, LANES), lambda i: (0, 0))],
        out_shape=[jax.ShapeDtypeStruct((S, D_CONV), F32), jax.ShapeDtypeStruct((S, LANES), F32),
                   jax.ShapeDtypeStruct((8, LANES), F32)],
        scratch_shapes=[pltpu.VMEM((SSD_HEADS, SSD_P, SSD_N), F32)],
        compiler_params=_cp(("arbitrary",)),
    )(xact, proj, svec, states, dy)


def _ada_fwd(c_all, w_ada, b_sh, name):
    nb = 1536 // 512

    def body(c_ref, w_ref, b_ref, o_ref):
        ca = jax.nn.silu(c_ref[...]).astype(BF16)
        o_ref[0] = _dot(ca, w_ref[0].astype(BF16), ((1,), (0,))) + b_ref[0]

    return pl.pallas_call(
        body, name=name, grid=(DEPTH, nb),
        in_specs=[pl.BlockSpec((8, D_MODEL), lambda l, j: (0, 0)), pl.BlockSpec((1, D_MODEL, 512), lambda l, j: (l, 0, j)),
                  pl.BlockSpec((1, 1, 512), lambda l, j: (l, 0, j))],
        out_specs=pl.BlockSpec((1, 8, 512), lambda l, j: (l, 0, j)),
        out_shape=jax.ShapeDtypeStruct((DEPTH, 8, 1536), F32),
        compiler_params=_cp(("parallel", "parallel")),
    )(c_all, w_ada, b_sh)


def _ada_bwd(c_all_t, dmod_sh, name):
    nb = 1536 // 512

    def body(c_ref, d_ref, o_ref):
        ca = jax.nn.silu(c_ref[...])
        acc = ca[:, 0:1] * d_ref[0, 0:1, :]
        for b in range(1, 8):
            acc = acc + ca[:, b:b + 1] * d_ref[0, b:b + 1, :]
        o_ref[0] = acc

    return pl.pallas_call(
        body, name=name, grid=(DEPTH, nb),
        in_specs=[pl.BlockSpec((D_MODEL, 8), lambda l, j: (0, 0)), pl.BlockSpec((1, 8, 512), lambda l, j: (l, 0, j))],
        out_specs=pl.BlockSpec((1, D_MODEL, 512), lambda l, j: (l, 0, j)),
        out_shape=jax.ShapeDtypeStruct((DEPTH, D_MODEL, 1536), F32),
        compiler_params=_cp(("parallel", "parallel")),
    )(c_all_t, dmod_sh)


def _rows_tile(rows):
    return next(t for t in (512, 256, 128, 64, 32, 16, 8) if rows % t == 0)


SUM_BLOCKS = 4
ADAM_BLOCKS = 8


def _sum_sibling(gs, ls, ci, name):
    n = len(gs)

    def body(c_ref, *refs):
        for p in range(n):
            refs[2 * n + p][...] = refs[2 * p][...] + refs[2 * p + 1][...]

    in_specs, out_specs, out_shape = [], [], []
    for g in gs:
        _, _, rh, cw = g.shape
        rb = rh // SUM_BLOCKS
        in_specs += [pl.BlockSpec((None, None, rb, cw), lambda s, i, c: (s, c[0], i, 0)),
                     pl.BlockSpec((None, rb, cw), lambda s, i, c: (s, i, 0))]
        out_specs.append(pl.BlockSpec((None, rb, cw), lambda s, i, c: (s, i, 0)))
        out_shape.append(jax.ShapeDtypeStruct((4, rh, cw), F32))
    ops = [a for pair in zip(gs, ls) for a in pair]
    return pl.pallas_call(
        body, name=name,
        grid_spec=pltpu.PrefetchScalarGridSpec(num_scalar_prefetch=1, grid=(4, SUM_BLOCKS), in_specs=in_specs, out_specs=out_specs),
        out_shape=out_shape, compiler_params=_cp(("parallel", "parallel")),
    )(ci.reshape(1).astype(jnp.int32), *ops)


def _sum_chips(cs, lands, chip, name):
    n = len(cs)

    def body(c_ref, *refs):
        for p in range(n):
            a = refs[4 * p:4 * p + 4]
            refs[4 * n + p][...] = ((a[0][...] + a[1][...]) + a[2][...]) + a[3][...]

    in_specs, out_specs, out_shape = [], [], []
    for c in cs:
        _, rh, cw = c.shape
        rb = rh // SUM_BLOCKS
        in_specs.append(pl.BlockSpec((None, rb, cw), lambda i, ch: (ch[0], i, 0)))
        in_specs += [pl.BlockSpec((None, rb, cw), functools.partial(lambda i, ch, k: (k, i, 0), k=k)) for k in range(3)]
        out_specs.append(pl.BlockSpec((rb, cw), lambda i, ch: (i, 0)))
        out_shape.append(jax.ShapeDtypeStruct((rh, cw), F32))
    ops = [a for c, l in zip(cs, lands) for a in (c, l, l, l)]
    return pl.pallas_call(
        body, name=name,
        grid_spec=pltpu.PrefetchScalarGridSpec(num_scalar_prefetch=1, grid=(SUM_BLOCKS,), in_specs=in_specs, out_specs=out_specs),
        out_shape=out_shape, compiler_params=_cp(("parallel",)),
    )(chip.reshape(1).astype(jnp.int32), *ops)


def _adam_update(w, m, v, g):
    c1 = 1.0 / (1.0 - ADAM_B1 ** ADAM_STEP)
    c2 = 1.0 / (1.0 - ADAM_B2 ** ADAM_STEP)
    nm = ADAM_B1 * m + (1.0 - ADAM_B1) * g
    nv = ADAM_B2 * v + (1.0 - ADAM_B2) * (g * g)
    return -ADAM_LR * ((nm * c1) / (jnp.sqrt(nv * c2) + ADAM_EPS) + ADAM_WD * w), nm, nv


def _adam_multi(ws, ms, vs, gs, name):
    n = len(ws)

    def body(*refs):
        for p in range(n):
            d, nm, nv = _adam_update(*[refs[4 * p + k][...] for k in range(4)])
            refs[4 * n + 3 * p][...] = d
            refs[4 * n + 3 * p + 1][...] = nm
            refs[4 * n + 3 * p + 2][...] = nv

    in_specs, out_specs, out_shape = [], [], []
    for w in ws:
        _, r, cw = w.shape
        spec = pl.BlockSpec((None, r // ADAM_BLOCKS, cw), lambda l, i: (l, i, 0))
        in_specs += [spec] * 4
        out_specs += [spec] * 3
        out_shape += [jax.ShapeDtypeStruct(w.shape, F32)] * 3
    ops = [a for q in zip(ws, ms, vs, gs) for a in q]
    res = pl.pallas_call(
        body, name=name, grid=(DEPTH, ADAM_BLOCKS), in_specs=in_specs, out_specs=out_specs, out_shape=out_shape,
        compiler_params=_cp(("parallel", "parallel")),
    )(*ops)
    return res[0::3], res[1::3], res[2::3]


def _adam(w, m, v, parts, name):
    rows, width = w.shape
    bm = min(256, _rows_tile(rows))
    np_ = len(parts)
    c1 = 1.0 / (1.0 - ADAM_B1 ** ADAM_STEP)
    c2 = 1.0 / (1.0 - ADAM_B2 ** ADAM_STEP)

    def body(*refs):
        w_ref, m_ref, v_ref = refs[:3]
        g = refs[3][...]
        for r in refs[4:3 + np_]:
            g = g + r[...]
        g_ref, d_ref, nm_ref, nv_ref = refs[3 + np_:]
        nm = ADAM_B1 * m_ref[...] + (1.0 - ADAM_B1) * g
        nv = ADAM_B2 * v_ref[...] + (1.0 - ADAM_B2) * (g * g)
        g_ref[...] = g
        nm_ref[...] = nm
        nv_ref[...] = nv
        d_ref[...] = -ADAM_LR * ((nm * c1) / (jnp.sqrt(nv * c2) + ADAM_EPS) + ADAM_WD * w_ref[...])

    blk = pl.BlockSpec((bm, width), lambda i: (i, 0))
    return pl.pallas_call(
        body, name=name, grid=(rows // bm,),
        in_specs=[blk, blk, blk] + [pl.BlockSpec((bm, width), functools.partial(lambda i, o: (i + o, 0), o=off // bm))
                                    for (_, off) in parts],
        out_specs=[blk, blk, blk, blk],
        out_shape=[jax.ShapeDtypeStruct((rows, width), F32)] * 4,
        compiler_params=_cp(("parallel",)),
    )(w, m, v, *[p[0] for p in parts])


def _coords():
    return lax.axis_index("x"), lax.axis_index("y"), lax.axis_index("c")


def _other_chips(x, y):
    return [(1 - x, y), (x, 1 - y), (1 - x, 1 - y)]


def _ag8(blk, name):
    m_per, n = blk.shape

    def body(x_ref, out_ref, send_sems, recv_sems, local_sem):
        x, y, c = _coords()
        me, sibling = (x, y, c), (x, y, 1 - c)
        chips = _other_chips(x, y)

        def rows(px, py, pc):
            return out_ref.at[pl.ds((4 * px + 2 * py + pc) * m_per, m_per), :]

        def copy(k, block, to, src=None):
            return pltpu.make_async_remote_copy(
                src_ref=rows(*block) if src is None else src, dst_ref=rows(*block),
                send_sem=send_sems.at[k], recv_sem=recv_sems.at[k], device_id=to, device_id_type=MESH)

        mine = pltpu.make_async_copy(x_ref, rows(*me), local_sem)
        mine.start()
        first = [copy(0, me, sibling, src=x_ref)]
        first += [copy(1 + j, me, (*chip, c), src=x_ref) for j, chip in enumerate(chips)]
        for cp in first:
            cp.start()
        passed = [copy(4 + j, (*chip, c), sibling) for j, chip in enumerate(chips)]
        for j, chip in enumerate(chips):
            copy(1 + j, (*chip, c), me).wait_recv()
            passed[j].start()
        copy(0, sibling, me).wait_recv()
        for j, chip in enumerate(chips):
            copy(4 + j, (*chip, 1 - c), me).wait_recv()
        for cp in first + passed:
            cp.wait_send()
        mine.wait()

    return pl.pallas_call(
        body, name=name,
        out_shape=jax.ShapeDtypeStruct((8 * m_per, n), blk.dtype),
        in_specs=[pl.BlockSpec(memory_space=pltpu.VMEM)], out_specs=pl.BlockSpec(memory_space=pltpu.VMEM),
        scratch_shapes=[pltpu.SemaphoreType.DMA((7,)), pltpu.SemaphoreType.DMA((7,)), pltpu.SemaphoreType.DMA],
    )(blk)


HBM_SPEC = pl.BlockSpec(memory_space=pltpu.HBM)
SEM_SPEC = pl.BlockSpec(memory_space=pltpu.SEMAPHORE)
EFFECT = pltpu.SideEffectType.DATAFLOW_SIDE_EFFECTING


def _remote(src, dst, send_sem, recv_sem, to):
    return pltpu.make_async_remote_copy(src_ref=src, dst_ref=dst, send_sem=send_sem, recv_sem=recv_sem,
                                        device_id=to, device_id_type=MESH)


def _ag_list(shards, name):
    n = len(shards)

    def body(*refs):
        sh, out = refs[:n], refs[n:2 * n]
        send_sems, recv_sems = refs[2 * n:]
        x, y, c = _coords()
        sibling = (x, y, 1 - c)
        chips = _other_chips(x, y)
        first = [_remote(sh[p].at[c], out[p].at[2 * x + y, c], send_sems.at[6 * p + j], recv_sems.at[6 * p + j], (px, py, c))
                 for p in range(n) for j, (px, py) in enumerate(chips)]
        for cp in first:
            cp.start()
        passed = []
        for j, (px, py) in enumerate(chips):
            for p in range(n):
                got = out[p].at[2 * px + py, c]
                _remote(got, got, send_sems.at[6 * p + j], recv_sems.at[6 * p + j], (x, y, c)).wait_recv()
                cp = _remote(got, got, send_sems.at[6 * p + 3 + j], recv_sems.at[6 * p + 3 + j], sibling)
                cp.start()
                passed.append(cp)
        for j, (px, py) in enumerate(chips):
            for p in range(n):
                got = out[p].at[2 * px + py, 1 - c]
                _remote(got, got, send_sems.at[6 * p + 3 + j], recv_sems.at[6 * p + 3 + j], (x, y, c)).wait_recv()
        for cp in first + passed:
            cp.wait_send()

    return pl.pallas_call(
        body, name=name,
        out_shape=[jax.ShapeDtypeStruct((4,) + s.shape, s.dtype) for s in shards],
        in_specs=[pl.BlockSpec(memory_space=pl.ANY)] * n, out_specs=[pl.BlockSpec(memory_space=pl.ANY)] * n,
        scratch_shapes=[pltpu.SemaphoreType.DMA((6 * n,)), pltpu.SemaphoreType.DMA((6 * n,))],
    )(*shards)


def _ag_direct_copies(sh, land, send_sems, recv_sems, starting):
    x, y, c = _coords()
    return [_remote(sh[p], land[p].at[2 * x + y] if starting else land[p].at[2 * px + py],
                    send_sems.at[3 * p + j], recv_sems.at[3 * p + j], (px, py, c))
            for p in range(len(sh)) for j, (px, py) in enumerate(_other_chips(x, y))]


def _ag_direct_start(shards, name):
    n = len(shards)

    def body(*refs):
        for cp in _ag_direct_copies(refs[:n], refs[n:2 * n], refs[2 * n], refs[2 * n + 1], True):
            cp.start()
        token = refs[4 * n + 2]
        token[...] = jnp.zeros_like(token)

    lands = [pltpu.with_memory_space_constraint(lax.empty((4,) + s.shape, s.dtype), pltpu.HBM) for s in shards]
    res = pl.pallas_call(
        body, name=name,
        out_shape=(pltpu.SemaphoreType.DMA((3 * n,)), pltpu.SemaphoreType.DMA((3 * n,)))
        + tuple(pltpu.HBM(s.shape, s.dtype) for s in shards) + tuple(pltpu.HBM(l.shape, l.dtype) for l in lands)
        + (jax.ShapeDtypeStruct((8, LANES), F32),),
        in_specs=(HBM_SPEC,) * (2 * n), out_specs=(SEM_SPEC, SEM_SPEC) + (HBM_SPEC,) * (2 * n) + (pl.BlockSpec(memory_space=pltpu.VMEM),),
        input_output_aliases={i: 2 + i for i in range(2 * n)},
        compiler_params=pltpu.CompilerParams(has_side_effects=EFFECT),
    )(*[pltpu.with_memory_space_constraint(s, pltpu.HBM) for s in shards], *lands)
    return res[0], res[1], res[2:2 + n], res[2 + n:2 + 2 * n], res[2 + 2 * n]


def _ag_direct_wait(send_sems, recv_sems, sh_thru, land_thru, after, name):
    n = len(sh_thru)

    def body(*refs):
        sh, land = refs[:n], refs[n:2 * n]
        for cp in _ag_direct_copies(sh, land, refs[2 * n], refs[2 * n + 1], False):
            cp.wait_send()
            cp.wait_recv()

    res = pl.pallas_call(
        body, name=name,
        out_shape=tuple(pltpu.HBM(s.shape, s.dtype) for s in sh_thru) + tuple(pltpu.HBM(l.shape, l.dtype) for l in land_thru),
        in_specs=(HBM_SPEC,) * (2 * n) + (SEM_SPEC, SEM_SPEC, pl.BlockSpec(memory_space=pl.ANY)),
        out_specs=(HBM_SPEC,) * (2 * n), input_output_aliases={i: i for i in range(2 * n)},
        compiler_params=pltpu.CompilerParams(has_side_effects=EFFECT),
    )(*sh_thru, *land_thru, send_sems, recv_sems, after)
    return res[n:]


def _rs_sibling_list(gs, name):
    n = len(gs)

    def body(*refs):
        g, out, send_sems, recv_sems = refs[:n], refs[n:2 * n], refs[2 * n], refs[2 * n + 1]
        x, y, c = _coords()
        cps = [_remote(g[p].at[s, 1 - c], out[p].at[s], send_sems.at[4 * p + s], recv_sems.at[4 * p + s], (x, y, 1 - c))
               for p in range(n) for s in range(4)]
        for cp in cps:
            cp.start()
        for cp in cps:
            cp.wait_recv()
        for cp in cps:
            cp.wait_send()

    return pl.pallas_call(
        body, name=name,
        out_shape=[jax.ShapeDtypeStruct((4,) + g.shape[2:], g.dtype) for g in gs],
        in_specs=[pl.BlockSpec(memory_space=pl.ANY)] * n, out_specs=[pl.BlockSpec(memory_space=pl.ANY)] * n,
        scratch_shapes=[pltpu.SemaphoreType.DMA((4 * n,)), pltpu.SemaphoreType.DMA((4 * n,))],
    )(*gs)


def _rs_chips_copies(cs, land, send_sems, recv_sems):
    x, y, c = _coords()
    return [_remote(cs[p].at[2 * px + py], land[p].at[j], send_sems.at[3 * p + j], recv_sems.at[3 * p + j], (px, py, c))
            for p in range(len(cs)) for j, (px, py) in enumerate(_other_chips(x, y))]


def _rs_chips_start(cs, name):
    n = len(cs)

    def body(*refs):
        for cp in _rs_chips_copies(refs[:n], refs[n:2 * n], refs[2 * n], refs[2 * n + 1]):
            cp.start()
        token = refs[4 * n + 2]
        token[...] = jnp.zeros_like(token)

    lands = [pltpu.with_memory_space_constraint(lax.empty((3,) + c.shape[1:], c.dtype), pltpu.HBM) for c in cs]
    res = pl.pallas_call(
        body, name=name,
        out_shape=(pltpu.SemaphoreType.DMA((3 * n,)), pltpu.SemaphoreType.DMA((3 * n,)))
        + tuple(pltpu.HBM(c.shape, c.dtype) for c in cs) + tuple(pltpu.HBM(l.shape, l.dtype) for l in lands)
        + (jax.ShapeDtypeStruct((8, LANES), F32),),
        in_specs=(HBM_SPEC,) * (2 * n), out_specs=(SEM_SPEC, SEM_SPEC) + (HBM_SPEC,) * (2 * n) + (pl.BlockSpec(memory_space=pltpu.VMEM),),
        input_output_aliases={i: 2 + i for i in range(2 * n)},
        compiler_params=pltpu.CompilerParams(has_side_effects=EFFECT),
    )(*[pltpu.with_memory_space_constraint(c, pltpu.HBM) for c in cs], *lands)
    return res[0], res[1], res[2:2 + n], res[2 + n:2 + 2 * n], res[2 + 2 * n]


def _rs_chips_wait(send_sems, recv_sems, cs_thru, land_thru, after, name):
    n = len(cs_thru)

    def body(*refs):
        for cp in _rs_chips_copies(refs[:n], refs[n:2 * n], refs[2 * n], refs[2 * n + 1]):
            cp.wait_send()
            cp.wait_recv()

    res = pl.pallas_call(
        body, name=name,
        out_shape=tuple(pltpu.HBM(c.shape, c.dtype) for c in cs_thru) + tuple(pltpu.HBM(l.shape, l.dtype) for l in land_thru),
        in_specs=(HBM_SPEC,) * (2 * n) + (SEM_SPEC, SEM_SPEC, pl.BlockSpec(memory_space=pl.ANY)),
        out_specs=(HBM_SPEC,) * (2 * n), input_output_aliases={i: i for i in range(2 * n)},
        compiler_params=pltpu.CompilerParams(has_side_effects=EFFECT),
    )(*cs_thru, *land_thru, send_sems, recv_sems, after)
    return res[:n], res[n:]


def _swap_list(ghs, name):
    n = len(ghs)

    def body(*refs):
        g, out, send_sems, recv_sems = refs[:n], refs[n:2 * n], refs[2 * n], refs[2 * n + 1]
        x, y, c = _coords()
        cps = [_remote(g[p], out[p], send_sems.at[p], recv_sems.at[p], (x, y, 1 - c)) for p in range(n)]
        for cp in cps:
            cp.start()
        for cp in cps:
            cp.wait_recv()
        for cp in cps:
            cp.wait_send()

    return pl.pallas_call(
        body, name=name,
        out_shape=[jax.ShapeDtypeStruct(g.shape, g.dtype) for g in ghs],
        in_specs=[pl.BlockSpec(memory_space=pl.ANY)] * n, out_specs=[pl.BlockSpec(memory_space=pl.ANY)] * n,
        scratch_shapes=[pltpu.SemaphoreType.DMA((n,)), pltpu.SemaphoreType.DMA((n,))],
    )(*ghs)


def _pad_win(w):
    return jnp.concatenate([w[:, :416], jnp.zeros((w.shape[0], 96), w.dtype), w[:, 416:1952],
                            w[:, 1952:1960], jnp.zeros((w.shape[0], 120), w.dtype)], axis=1)


def _unpad_win(g):
    return jnp.concatenate([g[:, :416], g[:, 512:2048], g[:, 2048:2056]], axis=1)


def _pad_wq(w):
    return jnp.pad(w.reshape(Q_LORA, HEADS, QK_DIM), ((0, 0), (0, 0), (0, LANES - QK_DIM))).reshape(Q_LORA, HEADS * LANES)


def _unpad_wq(g):
    return g.reshape(Q_LORA, HEADS, LANES)[:, :, :QK_DIM].reshape(Q_LORA, HEADS * QK_DIM)


def _cols_to_shards(a):
    r, c4 = a.shape
    return a.reshape(r, 4, c4 // 4).transpose(1, 0, 2)


def _shards_to_cols(a):
    _, r, c = a.shape
    return a.transpose(1, 0, 2).reshape(r, 4 * c)


def _pack_small(tree):
    parts = []
    for l in range(DEPTH):
        for (n, k) in SMALL:
            parts.append(jnp.pad(tree[n][l].reshape(-1), (0, -k % LANES)))
    flat = jnp.concatenate(parts)
    return jnp.pad(flat, (0, SMALL_ROWS * LANES - flat.shape[0])).reshape(SMALL_ROWS, LANES)


def _unpack_small(buf):
    flat = buf.reshape(-1)
    out = {n: [] for (n, _) in SMALL}
    o = 0
    for l in range(DEPTH):
        for (n, k) in SMALL:
            out[n].append(flat[o:o + k])
            o += k + (-k % LANES)
    return {n: jnp.stack(v) for n, v in out.items()}


def _vec(v, width=LANES):
    return jnp.pad(v.reshape(1, -1), ((0, 0), (0, width - v.shape[-1])))


def kernel(x, c, positions, norm1_w, norm2_w, w_ada, b_ada, w_in, q_a_norm_w, w_q_up, kv_a_norm_w, w_kv_up, q_nope_norm_w, q_pe_norm_w, k_nope_norm_w, k_pe_norm_w, conv_w, conv_b, dt_bias, a_log, d_skip, ssd_norm_w, w_out, w_gate_up, w_down, loss_target, m_norm1_w, m_norm2_w, m_w_ada, m_b_ada, m_w_in, m_q_a_norm_w, m_w_q_up, m_kv_a_norm_w, m_w_kv_up, m_q_nope_norm_w, m_q_pe_norm_w, m_k_nope_norm_w, m_k_pe_norm_w, m_conv_w, m_conv_b, m_dt_bias, m_a_log, m_d_skip, m_ssd_norm_w, m_w_out, m_w_gate_up, m_w_down, v_norm1_w, v_norm2_w, v_w_ada, v_b_ada, v_w_in, v_q_a_norm_w, v_w_q_up, v_kv_a_norm_w, v_w_kv_up, v_q_nope_norm_w, v_q_pe_norm_w, v_k_nope_norm_w, v_k_pe_norm_w, v_conv_w, v_conv_b, v_dt_bias, v_a_log, v_d_skip, v_ssd_norm_w, v_w_out, v_w_gate_up, v_w_down):
    W = dict(zip(WEIGHTS, (norm1_w, norm2_w, w_ada, b_ada, w_in, q_a_norm_w, w_q_up, kv_a_norm_w, w_kv_up, q_nope_norm_w, q_pe_norm_w, k_nope_norm_w, k_pe_norm_w, conv_w, conv_b, dt_bias, a_log, d_skip, ssd_norm_w, w_out, w_gate_up, w_down)))
    M = dict(zip(WEIGHTS, (m_norm1_w, m_norm2_w, m_w_ada, m_b_ada, m_w_in, m_q_a_norm_w, m_w_q_up, m_kv_a_norm_w, m_w_kv_up, m_q_nope_norm_w, m_q_pe_norm_w, m_k_nope_norm_w, m_k_pe_norm_w, m_conv_w, m_conv_b, m_dt_bias, m_a_log, m_d_skip, m_ssd_norm_w, m_w_out, m_w_gate_up, m_w_down)))
    V = dict(zip(WEIGHTS, (v_norm1_w, v_norm2_w, v_w_ada, v_b_ada, v_w_in, v_q_a_norm_w, v_w_q_up, v_kv_a_norm_w, v_w_kv_up, v_q_nope_norm_w, v_q_pe_norm_w, v_k_nope_norm_w, v_k_pe_norm_w, v_conv_w, v_conv_b, v_dt_bias, v_a_log, v_d_skip, v_ssd_norm_w, v_w_out, v_w_gate_up, v_w_down)))
    S = x.shape[1]
    xi, yi, ci = _coords()
    chip = 2 * xi + yi
    dev = 2 * chip + ci
    x0 = x[0]
    tgt = loss_target[0]

    inv_freq = 1.0 / (ROPE_THETA ** (jnp.arange(0, ROPE, 2, dtype=F32) / ROPE))
    ang = positions[0].astype(F32)[:, None] * inv_freq
    cos, sin = jnp.cos(ang), jnp.sin(ang)
    z16, z32, z64 = jnp.zeros((S, 16), F32), jnp.zeros((S, 32), F32), jnp.zeros((S, 64), F32)
    tab_c = jnp.concatenate([jnp.ones((S, 64), F32), cos, cos, z32], axis=1)
    tab_s1 = jnp.concatenate([z64, z16, sin, z32], axis=1)
    tab_s2 = jnp.concatenate([z64, -sin, z16, z32], axis=1)

    blk0 = jnp.concatenate([c.reshape(-1), W['conv_w'].reshape(-1)]).reshape(24, LANES)
    g0 = _ag8(blk0, "ag_c_conv").reshape(8, 24 * LANES)
    c_all = g0[:, :D_MODEL]
    conv_full = g0[0::2, D_MODEL:].reshape(4, DEPTH, CONV_TAPS, 256).transpose(1, 2, 0, 3).reshape(DEPTH, CONV_TAPS, D_CONV)

    sh = [[W[n][l].astype(BF16) for n in BIG] for l in range(DEPTH)]
    got0 = _ag_list([a.reshape(2, a.shape[0] // 2, a.shape[1]) for a in sh[0]], "ag_w0")

    def layer_weights(gathered, own):
        st = {n: lax.dynamic_update_slice_in_dim(a.reshape(4, -1, a.shape[-1]), o[None], chip, axis=0)
              for n, a, o in zip(BIG, gathered, own)}
        return dict(w_in=_pad_win(_shards_to_cols(st['w_in'])), w_q_up=_pad_wq(_shards_to_cols(st['w_q_up'])),
                    w_kv_up=_shards_to_cols(st['w_kv_up']), w_out=st['w_out'].reshape(D_MODEL, D_MODEL),
                    w_gate_up=st['w_gate_up'], w_down=st['w_down'].reshape(D_FF, D_MODEL))

    LW = [layer_weights(got0, sh[0]), None]

    b_sh = lax.dynamic_slice_in_dim(W['b_ada'], chip * 1536, 1536, axis=1).reshape(DEPTH, 1, 1536)
    mod_sh = _ada_fwd(c_all, W['w_ada'], b_sh, "ada_fwd")
    g1 = _ag8(mod_sh.reshape(192, LANES), "ag_mod").reshape(8, DEPTH, 8, 1536)
    mod_all = g1[0::2].transpose(1, 2, 0, 3).reshape(DEPTH, 8, 6 * D_MODEL)
    mod = lax.dynamic_index_in_dim(mod_all, dev, axis=1, keepdims=False)
    mod, sh1 = lax.optimization_barrier((mod, sh[1]))
    ag1 = _ag_direct_start(sh1, "ag_w1_start")

    def mvec(l, k):
        return mod[l, k * D_MODEL:(k + 1) * D_MODEL].reshape(1, D_MODEL)

    def small(name, l, width=None):
        v = W[name][l]
        return _vec(v, width or v.shape[-1])

    def wq_vec(l):
        return _vec(jnp.concatenate([W['q_nope_norm_w'][l], W['q_pe_norm_w'][l]]))

    def wk_vec(l):
        return _vec(jnp.concatenate([W['k_nope_norm_w'][l], W['k_pe_norm_w'][l]]))

    def conv_vec(l):
        return jnp.concatenate([conv_full[l], W['conv_b'][l].reshape(1, D_CONV), jnp.zeros((3, D_CONV), F32)], axis=0)

    def ssd_vec(l):
        return jnp.concatenate([_vec(W['dt_bias'][l]), _vec(W['a_log'][l]), _vec(W['d_skip'][l]), jnp.zeros((5, LANES), F32)], axis=0)

    sv = []
    xcur = x0
    h1 = _row_fwd(fn_norm_mod, "norm_mod_f", [(x0, 0, D_MODEL)], [small('norm1_w', 0) + ag1[4][0, 0], mvec(0, 1), mvec(0, 0)],
                  [(D_MODEL, BF16)])[0]
    fin = None
    for l in range(DEPTH):
        if l == 1:
            LW[1] = layer_weights(_ag_direct_wait(ag1[0], ag1[1], ag1[2], ag1[3], xcur, "ag_w1_wait"), sh[1])
        lw = LW[l]
        t = dict(xcur=xcur, h1=h1)
        t['proj'] = proj = _mm(h1, lw['w_in'], 'nn', f"mm_in_{l}")
        t['qa_n'], t['kva_n'] = _row_fwd(fn_lat_norm, f"lat_norm_f{l}", [(proj, 0, 256), (proj, 2, 128)],
                                         [small('q_a_norm_w', l), small('kv_a_norm_w', l)], [(256, BF16), (128, BF16)])
        t['q'] = _mm(t['qa_n'], lw['w_q_up'], 'nn', f"mm_q_{l}")
        t['kv'] = _mm(t['kva_n'], lw['w_kv_up'], 'nn', f"mm_kv_{l}")
        t['qf'], t['kf'], t['vv'] = _row_fwd(
            fn_qk_prep, f"qk_prep_f{l}",
            [(t['q'], 0, 1024), (t['kv'], 0, 1024), (proj, 3, 128), (tab_c, 0, 128), (tab_s1, 0, 128), (tab_s2, 0, 128)],
            [wq_vec(l), wk_vec(l)], [(1024, BF16), (1024, BF16), (1024, BF16)])
        t['ao'], t['lse'] = _attn_fwd(t['qf'], t['kf'], t['vv'], f"attn_f{l}")
        t['xact'] = _conv_fwd(proj, conv_vec(l), f"conv_f{l}")
        t['y'], t['states'] = _ssd_fwd(t['xact'], proj, ssd_vec(l), f"ssd_f{l}")
        t['mix'] = _row_fwd(fn_gated_mix, f"gated_f{l}", [(t['y'], 0, 512), (proj, 1, 512), (t['ao'], 0, 512)],
                            [small('ssd_norm_w', l)], [(1024, BF16)])[0]
        t['mo'] = _mm(t['mix'], lw['w_out'], 'nn', f"mm_out_{l}")
        t['x1'], t['h2'] = _row_fwd(fn_resid_norm, f"resid_mid_f{l}", [(xcur, 0, D_MODEL), (t['mo'], 0, D_MODEL)],
                                    [mvec(l, 2), small('norm2_w', l), mvec(l, 4), mvec(l, 3)],
                                    [(D_MODEL, F32), (D_MODEL, BF16)])
        t['gu'] = _mm(t['h2'], lw['w_gate_up'], 'nn', f"mm_gu_{l}", stack='b')
        t['act'] = _row_fwd(fn_swiglu, f"swiglu_f{l}", [(t['gu'], 0, 2 * D_FF)], [], [(D_FF, BF16)], tm=128)[0]
        t['ff'] = _mm(t['act'], lw['w_down'], 'nn', f"mm_down_{l}")
        if l + 1 < DEPTH:
            xcur, h1 = _row_fwd(fn_resid_norm, f"resid_end_f{l}", [(t['x1'], 0, D_MODEL), (t['ff'], 0, D_MODEL)],
                                [mvec(l, 5), small('norm1_w', l + 1), mvec(l + 1, 1), mvec(l + 1, 0)],
                                [(D_MODEL, F32), (D_MODEL, BF16)])
        else:
            fin = _final(t['x1'], t['ff'], tgt, mvec(l, 5), "final_loss")
        sv.append(t)

    dx1, dff, dg2_last, loss_acc = fin
    gfull = {n: [None] * DEPTH for n in BIG}
    gsm = {n: [None] * DEPTH for (n, _) in SMALL}
    dmod = [[None] * 6 for _ in range(DEPTH)]
    dmod[DEPTH - 1][5] = dg2_last
    grad_x = None
    pending = []

    def rs_begin(l, names, tag):
        g4 = [gfull[n][l].reshape(4, 2, gfull[n][l].shape[1] // 2, gfull[n][l].shape[2]) for n in names]
        sib = _rs_sibling_list(g4, f"rs_sibling_{tag}")
        cs = _sum_sibling(g4, sib, ci, f"sum_sibling_{tag}")
        h = _rs_chips_start(cs, f"rs_chips_start_{tag}")
        pending.append((l, names, h))
        return h[4][0, 0]

    tie_l1 = tie_l0a = 0.0

    for l in reversed(range(DEPTH)):
        t = sv[l]
        lw = LW[l]
        proj = t['proj']
        dact = _mm(dff, lw['w_down'], 'nt', f"mm_down_dx{l}")
        gfull['w_down'][l] = _mm(t['act'], dff, 'tn', f"mm_down_dw{l}").reshape(4, D_FF // 4, D_MODEL)
        dgu = _row_bwd(fn_swiglu, f"swiglu_b{l}", [(t['gu'], 0, 2 * D_FF)], [], [(dact, 0, D_FF)], [0], [], tm=128, ddtypes=[BF16])[0]
        dh2 = _mm(dgu, lw['w_gate_up'], 'nt', f"mm_gu_dx{l}", stack='b')
        gfull['w_gate_up'][l] = _mm(t['h2'], dgu, 'tn', f"mm_gu_dw{l}", stack='out')
        if l == 0:
            tie_l0a = rs_begin(0, EARLY, "l0a")
        dxc, dmo, dmod[l][2], gsm['norm2_w'][l], dmod[l][4], dmod[l][3] = _row_bwd(
            fn_resid_norm, f"resid_mid_b{l}", [(t['xcur'], 0, D_MODEL), (t['mo'], 0, D_MODEL)],
            [mvec(l, 2) + (tie_l1 if l == 0 else 0.0), small('norm2_w', l), mvec(l, 4), mvec(l, 3)],
            [(dx1, 0, D_MODEL), (dh2, 0, D_MODEL)], [0, 1], [0, 1, 2, 3], ddtypes=[F32, BF16])
        dmix = _mm(dmo, lw['w_out'], 'nt', f"mm_out_dx{l}")
        gfull['w_out'][l] = _mm(t['mix'], dmo, 'tn', f"mm_out_dw{l}").reshape(4, D_MODEL // 4, D_MODEL)
        dy, dz, gsm['ssd_norm_w'][l] = _row_bwd(fn_gated_norm, f"gated_b{l}", [(t['y'], 0, 512), (proj, 1, 512)],
                                                [small('ssd_norm_w', l)], [(dmix, 1, 512)], [0, 1], [0])
        dxact, ddt, dsv = _ssd_bwd(t['xact'], proj, ssd_vec(l) + (tie_l0a if l == 0 else 0.0), t['states'], dy, f"ssd_b{l}")
        gsm['dt_bias'][l], gsm['a_log'][l], gsm['d_skip'][l] = dsv[0, :8], dsv[1, :8], dsv[2, :8]
        dxbc, dcv = _conv_bwd(proj, conv_vec(l), dxact, f"conv_b{l}")
        gsm['conv_w'][l] = dcv[:CONV_TAPS]
        gsm['conv_b'][l] = dcv[CONV_TAPS]
        delta_r = _attn_delta(dmix, t['ao'], f"attn_delta{l}")
        dqT, dkf, dvv = _attn_bwd(t['qf'], t['kf'], t['kf'].T, t['vv'], dmix, t['lse'], delta_r, f"attn_b{l}")
        dqf = dqT.T
        dq, dkv, dkpe, dwq, dwk = _row_bwd(
            fn_qk_prep, f"qk_prep_b{l}",
            [(t['q'], 0, 1024), (t['kv'], 0, 1024), (proj, 3, 128), (tab_c, 0, 128), (tab_s1, 0, 128), (tab_s2, 0, 128)],
            [wq_vec(l), wk_vec(l)], [(dqf, 0, 1024), (dkf, 0, 1024), (dvv, 0, 1024)], [0, 1, 2], [0, 1],
            ddtypes=[BF16, BF16, F32])
        gsm['q_nope_norm_w'][l], gsm['q_pe_norm_w'][l] = dwq[0, :NOPE], dwq[0, NOPE:QK_DIM]
        gsm['k_nope_norm_w'][l], gsm['k_pe_norm_w'][l] = dwk[0, :NOPE], dwk[0, NOPE:QK_DIM]
        dqa_n = _mm(dq, lw['w_q_up'], 'nt', f"mm_q_dx{l}")
        gfull['w_q_up'][l] = _cols_to_shards(_unpad_wq(_mm(t['qa_n'], dq, 'tn', f"mm_q_dw{l}")))
        dkva_n = _mm(dkv, lw['w_kv_up'], 'nt', f"mm_kv_dx{l}")
        gfull['w_kv_up'][l] = _cols_to_shards(_mm(t['kva_n'], dkv, 'tn', f"mm_kv_dw{l}"))
        dqa, dkva, dqw, dkvw = _row_bwd(fn_lat_norm, f"lat_norm_b{l}", [(proj, 0, 256), (proj, 2, 128)],
                                        [small('q_a_norm_w', l), small('kv_a_norm_w', l)],
                                        [(dqa_n, 0, 256), (dkva_n, 0, 128)], [0, 1], [0, 1])
        gsm['q_a_norm_w'][l], gsm['kv_a_norm_w'][l] = dqw[0], dkvw[0]
        dproj = jnp.concatenate([dqa, dkva, dkpe, dz, dxbc, ddt], axis=1).astype(BF16)
        dh1 = _mm(dproj, lw['w_in'], 'nt', f"mm_in_dx{l}")
        gfull['w_in'][l] = _cols_to_shards(_unpad_win(_mm(t['h1'], dproj, 'tn', f"mm_in_dw{l}")))
        if l > 0:
            p = sv[l - 1]
            dx1, dff, dmod[l - 1][5], gsm['norm1_w'][l], dmod[l][1], dmod[l][0] = _row_bwd(
                fn_resid_norm, f"resid_end_b{l - 1}", [(p['x1'], 0, D_MODEL), (p['ff'], 0, D_MODEL)],
                [mvec(l - 1, 5), small('norm1_w', l), mvec(l, 1), mvec(l, 0)], [(dxc, 0, D_MODEL), (dh1, 0, D_MODEL)],
                [0, 1], [0, 1, 2, 3], ddtypes=[F32, BF16])
            tie_l1 = rs_begin(l, BIG, f"l{l}")
        else:
            grad_x, gsm['norm1_w'][l], dmod[l][1], dmod[l][0] = _row_bwd(
                fn_norm_mod_pass, "norm_mod_b", [(x0, 0, D_MODEL)], [small('norm1_w', 0), mvec(0, 1), mvec(0, 0)],
                [(dxc, 0, D_MODEL), (dh1, 0, D_MODEL)], [0], [0, 1, 2])
        for n in ('norm1_w', 'norm2_w', 'ssd_norm_w'):
            gsm[n][l] = gsm[n][l][0]

    for l in range(DEPTH):
        gsm['b_ada'][l] = jnp.concatenate([d[0] for d in dmod[l]])
    sm_part = _pack_small({n: jnp.stack(v) for n, v in gsm.items()}).at[SMALL_ROWS - 1, 0].set(loss_acc[0, 0])
    sm_all = _ag8(sm_part, "ag_small")
    loss = jnp.sum(sm_all.reshape(8, SMALL_ROWS, LANES)[:, SMALL_ROWS - 1, 0])
    sm_all, late = lax.optimization_barrier((sm_all, [gfull[n][0] for n in BIG[2:]]))
    for n, g in zip(BIG[2:], late):
        gfull[n][0] = g
    tie_l0b = rs_begin(0, BIG[2:], "l0b")

    def with_conv(tree):
        wide = lax.dynamic_update_slice_in_dim(jnp.zeros((DEPTH, CONV_TAPS, D_CONV), F32), tree['conv_w'], chip * 256, axis=2)
        return {**tree, 'conv_w': wide}

    g_sm, d_sm, m_sm, v_sm = _adam(_pack_small(with_conv(W)) + tie_l0b, _pack_small(with_conv(M)), _pack_small(with_conv(V)),
                                   [(sm_all, d * SMALL_ROWS) for d in range(8)], "adam_small")
    out_small = [_unpack_small(b) for b in (g_sm, d_sm, m_sm, v_sm)]
    for o in out_small:
        o['conv_w'] = lax.dynamic_slice_in_dim(o['conv_w'].reshape(DEPTH, CONV_TAPS, D_CONV), chip * 256, 256, axis=2)

    dmod_all = sm_all.reshape(8, SMALL_ROWS * LANES)
    per_layer = sum(k + (-k % LANES) for (_, k) in SMALL)
    dmod_sh = jnp.stack([lax.dynamic_slice_in_dim(dmod_all[:, l * per_layer:l * per_layer + 6 * D_MODEL], chip * 1536, 1536, axis=1)
                         for l in range(DEPTH)])
    g_ada = _ada_bwd(c_all.T, dmod_sh, "ada_bwd")
    ada = _adam(W['w_ada'].reshape(DEPTH * D_MODEL, 1536), M['w_ada'].reshape(DEPTH * D_MODEL, 1536),
                V['w_ada'].reshape(DEPTH * D_MODEL, 1536), [(g_ada.reshape(DEPTH * D_MODEL, 1536), 0)], "adam_ada")
    out_ada = [a.reshape(DEPTH, D_MODEL, 1536) for a in ada]

    keys, cs_all, land_all = [], [], []
    for (l, names, (send_sems, recv_sems, cs_thru, land_thru, _)) in pending:
        cs, lands = _rs_chips_wait(send_sems, recv_sems, cs_thru, land_thru, ada[3], f"rs_chips_wait_l{l}{len(names)}")
        keys += [(l, n) for n in names]
        cs_all += list(cs)
        land_all += list(lands)
    ghalf = _sum_chips(cs_all, land_all, chip, "sum_chips")
    gother = _swap_list(ghalf, "swap_halves")
    gshard = {k: jnp.where(ci == 0, jnp.concatenate([a, b]), jnp.concatenate([b, a])) for k, a, b in zip(keys, ghalf, gother)}
    g_big = [jnp.stack([gshard[(l, n)] for l in range(DEPTH)]) for n in BIG]
    d_big, m_big, v_big = _adam_multi([W[n] for n in BIG], [M[n] for n in BIG], [V[n] for n in BIG], g_big, "adam_big")
    out_big = [dict(zip(BIG, o)) for o in (g_big, d_big, m_big, v_big)]

    outs = [loss, grad_x[None]]
    for k in range(4):
        for n in WEIGHTS:
            if n == 'w_ada':
                outs.append(out_ada[k])
            elif n in BIG:
                outs.append(out_big[k][n])
            else:
                outs.append(out_small[k][n])
    return tuple(outs)
```

```python
import functools

import jax
import jax.numpy as jnp
from jax import lax
from jax.experimental import pallas as pl
from jax.experimental.pallas import tpu as pltpu

F32 = jnp.float32
BF16 = jnp.bfloat16
MESH = pl.DeviceIdType.MESH

D_MODEL = 1024
DEPTH = 2
HEADS = 8
NOPE = 64
ROPE = 32
QK_DIM = NOPE + ROPE
Q_LORA = 256
KV_LORA = 128
SSD_HEADS = 8
SSD_P = 64
SSD_N = 128
CHUNK = 128
CONV_TAPS = 4
D_SSD = 512
D_CONV = 1024
D_FF = 2816
D_IN = 1960
D_IN_PAD = 2176
EPS = 1e-6
ROPE_THETA = 10000.0
ATT_SCALE = QK_DIM ** -0.5
NEG = -1e30
LANES = 128
VMEM_LIMIT = 48 * 1024 * 1024

ADAM_LR, ADAM_B1, ADAM_B2, ADAM_EPS, ADAM_WD, ADAM_STEP = 0.001, 0.9, 0.999, 1e-08, 0.01, 10

WEIGHTS = ['norm1_w', 'norm2_w', 'w_ada', 'b_ada', 'w_in', 'q_a_norm_w', 'w_q_up', 'kv_a_norm_w', 'w_kv_up',
           'q_nope_norm_w', 'q_pe_norm_w', 'k_nope_norm_w', 'k_pe_norm_w', 'conv_w', 'conv_b', 'dt_bias', 'a_log',
           'd_skip', 'ssd_norm_w', 'w_out', 'w_gate_up', 'w_down']
BIG = ['w_down', 'w_gate_up', 'w_out', 'w_kv_up', 'w_q_up', 'w_in']
EARLY = BIG[:2]
SMALL = [('b_ada', 6144), ('conv_w', 4096), ('norm1_w', 1024), ('norm2_w', 1024), ('conv_b', 1024), ('ssd_norm_w', 512),
         ('q_a_norm_w', 256), ('kv_a_norm_w', 128), ('q_nope_norm_w', 64), ('q_pe_norm_w', 32),
         ('k_nope_norm_w', 64), ('k_pe_norm_w', 32), ('dt_bias', 8), ('a_log', 8), ('d_skip', 8)]
SMALL_ROWS = 240


def _cp(sem=None, **kw):
    return pltpu.CompilerParams(dimension_semantics=sem, vmem_limit_bytes=VMEM_LIMIT, **kw)


def _dot(a, b, dims, prec=None):
    return lax.dot_general(a, b, (dims, ((), ())), preferred_element_type=F32, precision=prec)


def _tile(dim, target):
    best = 0
    for t in range(LANES, min(dim, target) + 1, LANES):
        if dim % t == 0:
            best = t
    if best < 256 and dim <= 2304:
        return dim
    return best


def _mm(a, b, mode, name, out_dtype=F32, stack=None):
    ns = None
    if stack == 'b':
        ns = b.shape[2]
        if mode == 'nn':
            (M, K), N = a.shape, 4 * ns
        else:
            (M, K), N = a.shape, b.shape[1]
    elif mode == 'nn':
        (M, K), (_, N) = a.shape, b.shape
    elif mode == 'nt':
        (M, K), (N, _) = a.shape, b.shape
    else:
        (K, M), (_, N) = a.shape, b.shape
    if stack == 'out':
        ns = N // 4
    tm, tn, tk = _tile(M, 1408 if mode == 'tn' else 1024), _tile(N, 1408), _tile(K, 1408)
    if stack == 'b' and mode == 'nt':
        tk = ns
    elif stack is not None:
        tn = ns
    nk = K // tk
    dims = {'nn': ((1,), (0,)), 'nt': ((1,), (1,)), 'tn': ((0,), (0,))}[mode]

    def body(a_ref, b_ref, o_ref, *acc):
        part = _dot(a_ref[...].astype(BF16), b_ref[...].astype(BF16), dims)
        if nk == 1:
            o_ref[...] = part.astype(o_ref.dtype)
            return
        k = pl.program_id(2)

        @pl.when(k == 0)
        def _():
            acc[0][...] = part

        @pl.when(k > 0)
        def _():
            acc[0][...] += part

        @pl.when(k == nk - 1)
        def _():
            o_ref[...] = acc[0][...].astype(o_ref.dtype)

    a_spec = pl.BlockSpec((tk, tm), lambda i, j, k: (k, i)) if mode == 'tn' else pl.BlockSpec((tm, tk), lambda i, j, k: (i, k))
    b_spec = pl.BlockSpec((tn, tk), lambda i, j, k: (j, k)) if mode == 'nt' else pl.BlockSpec((tk, tn), lambda i, j, k: (k, j))
    o_spec, o_shape = pl.BlockSpec((tm, tn), lambda i, j, k: (i, j)), (M, N)
    if stack == 'b':
        b_spec = (pl.BlockSpec((None, tn, ns), lambda i, j, k: (k, j, 0)) if mode == 'nt'
                  else pl.BlockSpec((None, tk, ns), lambda i, j, k: (j, k, 0)))
    if stack == 'out':
        o_spec, o_shape = pl.BlockSpec((None, tm, ns), lambda i, j, k: (j, i, 0)), (4, M, ns)
    return pl.pallas_call(
        body, name=name, grid=(M // tm, N // tn, nk),
        in_specs=[a_spec, b_spec], out_specs=o_spec,
        out_shape=jax.ShapeDtypeStruct(o_shape, out_dtype),
        scratch_shapes=[pltpu.VMEM((tm, tn), F32)] if nk > 1 else [],
        compiler_params=_cp(("parallel", "parallel", "arbitrary")),
    )(a, b)


def _rspec(tm, w, cb):
    return pl.BlockSpec((tm, w), lambda i: (i, cb))


def _vspec(shape):
    return pl.BlockSpec(shape, lambda i: (0,) * len(shape))


def _row_fwd(fn, name, rows, vecs, outs, tm=256):
    S = rows[0][0].shape[0]
    tm = min(tm, S)
    nin = len(rows) + len(vecs)

    def body(*refs):
        res = fn(*[r[...] for r in refs[:nin]])
        for o_ref, r in zip(refs[nin:], res):
            o_ref[...] = r.astype(o_ref.dtype)

    return pl.pallas_call(
        body, name=name, grid=(S // tm,),
        in_specs=[_rspec(tm, w, cb) for (_, cb, w) in rows] + [_vspec(v.shape) for v in vecs],
        out_specs=[_rspec(tm, w, 0) for (w, _) in outs],
        out_shape=[jax.ShapeDtypeStruct((S, w), dt) for (w, dt) in outs],
        compiler_params=_cp(("parallel",)),
    )(*[r[0] for r in rows], *vecs)


def _row_bwd(fn, name, rows, vecs, cts, drows, dvecs, tm=256, ddtypes=None):
    S = rows[0][0].shape[0]
    ddtypes = ddtypes or [F32] * len(drows)
    tm = min(tm, S)
    nr, nv, nc = len(rows), len(vecs), len(cts)
    didx = list(drows) + [nr + j for j in dvecs]

    def body(*refs):
        vals = [r[...] for r in refs[:nr + nv]]
        ct = tuple(r[...].astype(F32) for r in refs[nr + nv:nr + nv + nc])
        outs = refs[nr + nv + nc:]

        def g(*d):
            a = list(vals)
            for k, val in zip(didx, d):
                a[k] = val
            return tuple(fn(*a))

        _, vjp = jax.vjp(g, *[vals[k] for k in didx])
        grads = vjp(ct)
        for o, gr in zip(outs[:len(drows)], grads[:len(drows)]):
            o[...] = gr.astype(o.dtype)

        @pl.when(pl.program_id(0) == 0)
        def _():
            for o in outs[len(drows):]:
                o[...] = jnp.zeros_like(o)

        for o, gr in zip(outs[len(drows):], grads[len(drows):]):
            o[...] += gr

    return pl.pallas_call(
        body, name=name, grid=(S // tm,),
        in_specs=[_rspec(tm, w, cb) for (_, cb, w) in rows] + [_vspec(v.shape) for v in vecs]
        + [_rspec(tm, w, cb) for (_, cb, w) in cts],
        out_specs=[_rspec(tm, rows[k][2], 0) for k in drows] + [_vspec(vecs[j].shape) for j in dvecs],
        out_shape=[jax.ShapeDtypeStruct((S, rows[k][2]), dt) for k, dt in zip(drows, ddtypes)]
        + [jax.ShapeDtypeStruct(vecs[j].shape, F32) for j in dvecs],
        compiler_params=_cp(("arbitrary",)),
    )(*[r[0] for r in rows], *vecs, *[c[0] for c in cts])


def _rms(x):
    return x * lax.rsqrt(jnp.mean(x * x, axis=-1, keepdims=True) + EPS)


def fn_norm_mod(x, nw, sc, sh):
    return (_rms(x) * nw * (1.0 + sc) + sh,)


def fn_norm_mod_pass(x, nw, sc, sh):
    return (x, _rms(x) * nw * (1.0 + sc) + sh)


def fn_resid_norm(x, d, g, nw, sc, sh):
    xn = x + g * d
    return (xn, _rms(xn) * nw * (1.0 + sc) + sh)


def fn_lat_norm(qa, kva, qw, kvw):
    return (_rms(qa) * qw, _rms(kva) * kvw)


@functools.partial(jax.custom_vjp, nondiff_argnums=(1,))
def _lroll(x, s):
    return pltpu.roll(x, s, 1)


def _lroll_fwd(x, s):
    return pltpu.roll(x, s, 1), None


def _lroll_bwd(s, _, g):
    return (pltpu.roll(g, (LANES - s) % LANES, 1),)


_lroll.defvjp(_lroll_fwd, _lroll_bwd)


def _lane_masks(shape):
    lane = lax.broadcasted_iota(jnp.int32, shape, 1)
    return (lane < NOPE).astype(F32), ((lane >= NOPE) & (lane < QK_DIM)).astype(F32)


def _rope(t, tc, ts1, ts2):
    return t * tc + _lroll(t, 16) * ts1 + _lroll(t, LANES - 16) * ts2


def fn_qk_prep(q, kv, kpe, tc, ts1, ts2, wq, wk):
    mn, mp = _lane_masks((1, LANES))
    mhi = 1.0 - mn

    def head_norm(t, w):
        rn = lax.rsqrt(jnp.sum(t * t * mn, axis=-1, keepdims=True) * (1.0 / NOPE) + EPS)
        rp = lax.rsqrt(jnp.sum(t * t * mp, axis=-1, keepdims=True) * (1.0 / ROPE) + EPS)
        return t * (rn * mn + rp * mp) * w

    kp = _rope(head_norm(_lroll(kpe, NOPE), wk) * mp, tc, ts1, ts2)
    qs, ks, vs = [], [], []
    for h in range(HEADS):
        qs.append(_rope(head_norm(q[:, h * LANES:(h + 1) * LANES], wq), tc, ts1, ts2))
        t = kv[:, h * LANES:(h + 1) * LANES]
        ks.append(head_norm(t, wk) * mn + kp)
        vs.append(_lroll(t, NOPE) * mn + mhi)
    return (jnp.concatenate(qs, axis=1), jnp.concatenate(ks, axis=1), jnp.concatenate(vs, axis=1))


def fn_gated_norm(y, z, w):
    u = y * jax.nn.silu(z)
    half = D_SSD // 2
    return (jnp.concatenate([_rms(u[:, :half]), _rms(u[:, half:])], axis=1) * w,)


def fn_gated_mix(y, z, ao, w):
    return (jnp.concatenate([ao, fn_gated_norm(y, z, w)[0]], axis=1),)


def fn_swiglu(gu):
    return (jax.nn.silu(gu[:, :D_FF]) * gu[:, D_FF:],)


def _final(x1, ff, tgt, g2, name):
    S = x1.shape[0]
    tm = min(256, S)

    def body(x_ref, f_ref, t_ref, g_ref, dx_ref, df_ref, dg_ref, l_ref):
        @pl.when(pl.program_id(0) == 0)
        def _():
            dg_ref[...] = jnp.zeros_like(dg_ref)
            l_ref[...] = jnp.zeros_like(l_ref)

        f = f_ref[...]
        g = g_ref[...]
        e = x_ref[...] + g * f - t_ref[...]
        dx = e * (1.0 / D_MODEL)
        dx_ref[...] = dx
        df_ref[...] = (g * dx).astype(df_ref.dtype)
        dg_ref[...] += jnp.sum(dx * f, axis=0, keepdims=True)
        l_ref[...] += jnp.sum(e * e) * (0.5 / D_MODEL)

    r = _rspec(tm, D_MODEL, 0)
    return pl.pallas_call(
        body, name=name, grid=(S // tm,),
        in_specs=[r, r, r, _vspec((1, D_MODEL))],
        out_specs=[r, r, _vspec((1, D_MODEL)), _vspec((1, LANES))],
        out_shape=[jax.ShapeDtypeStruct((S, D_MODEL), F32), jax.ShapeDtypeStruct((S, D_MODEL), BF16),
                   jax.ShapeDtypeStruct((1, D_MODEL), F32), jax.ShapeDtypeStruct((1, LANES), F32)],
        compiler_params=_cp(("arbitrary",)),
    )(x1, ff, tgt, g2)


def _causal_mask(t):
    r = lax.broadcasted_iota(jnp.int32, (t, t), 0)
    c = lax.broadcasted_iota(jnp.int32, (t, t), 1)
    return c <= r


LOG2E = 1.4426950408889634
EXP2_SCALE = ATT_SCALE * LOG2E
ATT_TQ, ATT_TK = 512, 1024
ATT_BQ, ATT_BK = 1024, 512


def _attn_fwd(qf, kf, va, name):
    S = qf.shape[0]
    T, TK = min(ATT_TQ, S), min(ATT_TK, S)
    nmask = max(1, T // TK)

    def body(q_ref, k_ref, v_ref, o_ref, l_ref):
        i = pl.program_id(1)
        r = lax.broadcasted_iota(jnp.int32, (T, TK), 0)
        c = lax.broadcasted_iota(jnp.int32, (T, TK), 1)
        qs = [q_ref[:, hh * LANES:(hh + 1) * LANES] for hh in range(2)]

        def blk(j, carry, masked):
            off = pl.multiple_of(j * TK, TK)
            out = []
            for hh in range(2):
                m, acc = carry[hh]
                s = _dot(qs[hh], k_ref[pl.ds(off, TK), hh * LANES:(hh + 1) * LANES], ((1,), (1,)))
                if masked:
                    s = jnp.where(c + j * TK <= r + i * T, s, NEG)
                mn = jnp.maximum(m, jnp.max(s, axis=1, keepdims=True))
                p = jnp.exp2((s - mn) * EXP2_SCALE)
                al = jnp.exp2((m - mn) * EXP2_SCALE)
                vj = v_ref[pl.ds(off, TK), hh * LANES:(hh + 1) * LANES]
                out.append((mn, al * acc + _dot(p.astype(BF16), vj, ((1,), (0,)))))
            return tuple(out)

        one = (jnp.full((T, 1), NEG, F32), jnp.zeros((T, LANES), F32))
        nfull = lax.div(i * T, TK)
        carry = lax.fori_loop(0, nfull, lambda j, cr: blk(j, cr, False), (one, one))
        for t in range(nmask):
            carry = blk(nfull + t, carry, True)
        lane = lax.broadcasted_iota(jnp.int32, (1, LANES), 1)
        z = jnp.zeros((T, LANES), F32)
        for hh in range(2):
            m, acc = carry[hh]
            l = acc[:, 64:65]
            o_ref[:, hh * 64:(hh + 1) * 64] = (acc / l)[:, :64]
            z = z + (m * EXP2_SCALE + jnp.log(l) * LOG2E) * (lane == hh).astype(F32)
        l_ref[0] = z.T[0:2, :]

    return pl.pallas_call(
        body, name=name, grid=(HEADS // 2, S // T),
        in_specs=[pl.BlockSpec((T, 256), lambda h, i: (i, h)), pl.BlockSpec((S, 256), lambda h, i: (0, h)),
                  pl.BlockSpec((S, 256), lambda h, i: (0, h))],
        out_specs=[pl.BlockSpec((T, LANES), lambda h, i: (i, h)), pl.BlockSpec((1, 2, T), lambda h, i: (h, 0, i))],
        out_shape=[jax.ShapeDtypeStruct((S, D_SSD), F32), jax.ShapeDtypeStruct((HEADS // 2, 2, S), F32)],
        compiler_params=_cp(("parallel", "parallel")),
    )(qf, kf, va)


def _attn_delta(dmix, ao, name):
    S = ao.shape[0]
    tm = min(512, S)

    def body(d_ref, o_ref, out_ref):
        lane = lax.broadcasted_iota(jnp.int32, (1, LANES), 1)
        lo = (lane < 64).astype(F32)
        for hp in range(HEADS // 2):
            y = d_ref[:, hp * LANES:(hp + 1) * LANES] * o_ref[:, hp * LANES:(hp + 1) * LANES]
            z = (jnp.sum(y * lo, axis=1, keepdims=True) * (lane == 0).astype(F32)
                 + jnp.sum(y * (1.0 - lo), axis=1, keepdims=True) * (lane == 1).astype(F32))
            out_ref[hp] = z.T[0:2, :]

    return pl.pallas_call(
        body, name=name, grid=(S // tm,),
        in_specs=[pl.BlockSpec((tm, D_SSD), lambda i: (i, 0)), pl.BlockSpec((tm, D_SSD), lambda i: (i, 0))],
        out_specs=pl.BlockSpec((HEADS // 2, 2, tm), lambda i: (0, 0, i)),
        out_shape=jax.ShapeDtypeStruct((HEADS // 2, 2, S), F32),
        compiler_params=_cp(("parallel",)),
    )(dmix, ao)


def _attn_bwd(qf, kf, kT, va, do, lse_r, delta_r, name):
    S = qf.shape[0]
    T, TK = min(ATT_BQ, S), min(ATT_BK, S)
    nq = S // T
    nmask = max(1, TK // T)

    def body(q_ref, k_ref, kT_ref, v_ref, do_ref, l_ref, d_ref, dqT_ref, dk_ref, dv_ref):
        j = pl.program_id(1)

        @pl.when(j == 0)
        def _():
            dqT_ref[...] = jnp.zeros_like(dqT_ref)

        r = lax.broadcasted_iota(jnp.int32, (TK, T), 0)
        c = lax.broadcasted_iota(jnp.int32, (TK, T), 1)
        lo = (lax.broadcasted_iota(jnp.int32, (1, LANES), 1) < 64).astype(F32)
        ks = [k_ref[:, hh * LANES:(hh + 1) * LANES] for hh in range(2)]
        vs = [v_ref[:, hh * LANES:(hh + 1) * LANES] for hh in range(2)]
        kTs = [kT_ref[hh * LANES:(hh + 1) * LANES, :] for hh in range(2)]

        def blk(i, carry, masked):
            off = pl.multiple_of(i * T, T)
            dall = do_ref[pl.ds(off, T), :]
            out = []
            for hh in range(2):
                dk, dv = carry[hh]
                q = q_ref[pl.ds(off, T), hh * LANES:(hh + 1) * LANES]
                dop = ((dall if hh == 0 else pltpu.roll(dall, 64, 1)) * lo).astype(BF16)
                lrow = l_ref[0, hh:hh + 1, pl.ds(off, T)]
                drow = d_ref[0, hh:hh + 1, pl.ds(off, T)]
                pT = jnp.exp2(_dot(ks[hh], q, ((1,), (1,))) * EXP2_SCALE - lrow)
                if masked:
                    pT = jnp.where(r + j * TK <= c + i * T, pT, 0.0)
                dpT = _dot(vs[hh], dop, ((1,), (1,)))
                dsT = (pT * (dpT - drow) * ATT_SCALE).astype(BF16)
                dv = dv + _dot(pT.astype(BF16), dop, ((1,), (0,)))
                dk = dk + _dot(dsT, q, ((1,), (0,)))
                dqT_ref[hh * LANES:(hh + 1) * LANES, pl.ds(off, T)] += _dot(kTs[hh], dsT, ((1,), (0,)))
                out.append((dk, dv))
            return tuple(out)

        z = (jnp.zeros((TK, LANES), F32), jnp.zeros((TK, LANES), F32))
        first = lax.div(j * TK, T)
        carry = (z, z)
        for t in range(nmask):
            carry = blk(first + t, carry, True)
        carry = lax.fori_loop(first + nmask, nq, lambda i, cr: blk(i, cr, False), carry)
        for hh in range(2):
            dk_ref[:, hh * LANES:(hh + 1) * LANES] = carry[hh][0]
            dv_ref[:, hh * LANES:(hh + 1) * LANES] = carry[hh][1]

    return pl.pallas_call(
        body, name=name, grid=(HEADS // 2, S // TK),
        in_specs=[pl.BlockSpec((S, 256), lambda h, j: (0, h)), pl.BlockSpec((TK, 256), lambda h, j: (j, h)),
                  pl.BlockSpec((256, TK), lambda h, j: (h, j)), pl.BlockSpec((TK, 256), lambda h, j: (j, h)),
                  pl.BlockSpec((S, LANES), lambda h, j: (0, h)), pl.BlockSpec((1, 2, S), lambda h, j: (h, 0, 0)),
                  pl.BlockSpec((1, 2, S), lambda h, j: (h, 0, 0))],
        out_specs=[pl.BlockSpec((256, S), lambda h, j: (h, 0)), pl.BlockSpec((TK, 256), lambda h, j: (j, h)),
                   pl.BlockSpec((TK, 256), lambda h, j: (j, h))],
        out_shape=[jax.ShapeDtypeStruct((D_MODEL, S), F32), jax.ShapeDtypeStruct((S, D_MODEL), F32),
                   jax.ShapeDtypeStruct((S, D_MODEL), F32)],
        compiler_params=_cp(("parallel", "arbitrary")),
    )(qf, kf, kT, va, do, lse_r, delta_r)


def _shift_down(x, s):
    if s == 0:
        return x
    rows = lax.broadcasted_iota(jnp.int32, x.shape, 0)
    return jnp.where(rows >= s, pltpu.roll(x, s, 0), 0.0)


def _shift_up(x, s):
    if s == 0:
        return x
    n = x.shape[0]
    rows = lax.broadcasted_iota(jnp.int32, x.shape, 0)
    return jnp.where(rows < n - s, pltpu.roll(x, n - s, 0), 0.0)


def _conv_fwd(proj, cvec, name):
    S = proj.shape[0]

    def body(x_ref, c_ref, o_ref):
        x = x_ref[...]
        y = jnp.broadcast_to(c_ref[4:5, :], x.shape)
        for k in range(CONV_TAPS):
            y = y + c_ref[k:k + 1, :] * _shift_down(x, CONV_TAPS - 1 - k)
        o_ref[...] = y * jax.nn.sigmoid(y)

    return pl.pallas_call(
        body, name=name, grid=(D_CONV // LANES,),
        in_specs=[pl.BlockSpec((S, LANES), lambda j: (0, 8 + j)), pl.BlockSpec((8, LANES), lambda j: (0, j))],
        out_specs=pl.BlockSpec((S, LANES), lambda j: (0, j)),
        out_shape=jax.ShapeDtypeStruct((S, D_CONV), F32),
        compiler_params=_cp(("parallel",)),
    )(proj, cvec)


def _conv_bwd(proj, cvec, dact, name):
    S = proj.shape[0]

    def body(x_ref, c_ref, d_ref, dx_ref, dc_ref):
        x = x_ref[...]
        y = jnp.broadcast_to(c_ref[4:5, :], x.shape)
        for k in range(CONV_TAPS):
            y = y + c_ref[k:k + 1, :] * _shift_down(x, CONV_TAPS - 1 - k)
        sg = jax.nn.sigmoid(y)
        dy = d_ref[...] * (sg * (1.0 + y * (1.0 - sg)))
        dx = jnp.zeros_like(x)
        for k in range(CONV_TAPS):
            s = CONV_TAPS - 1 - k
            dx = dx + c_ref[k:k + 1, :] * _shift_up(dy, s)
            dc_ref[k:k + 1, :] = jnp.sum(dy * _shift_down(x, s), axis=0, keepdims=True)
        dx_ref[...] = dx
        dc_ref[4:5, :] = jnp.sum(dy, axis=0, keepdims=True)
        dc_ref[5:8, :] = jnp.zeros((3, LANES), F32)

    return pl.pallas_call(
        body, name=name, grid=(D_CONV // LANES,),
        in_specs=[pl.BlockSpec((S, LANES), lambda j: (0, 8 + j)), pl.BlockSpec((8, LANES), lambda j: (0, j)),
                  pl.BlockSpec((S, LANES), lambda j: (0, j))],
        out_specs=[pl.BlockSpec((S, LANES), lambda j: (0, j)), pl.BlockSpec((8, LANES), lambda j: (0, j))],
        out_shape=[jax.ShapeDtypeStruct((S, D_CONV), F32), jax.ShapeDtypeStruct((8, D_CONV), F32)],
        compiler_params=_cp(("parallel",)),
    )(proj, cvec, dact)


def fn_ssd_chunk(xs, bm, cm, dtr, state, vecs):
    Q = CHUNK
    dt = jax.nn.softplus(dtr + vecs[0:1])
    a = -jnp.exp(vecs[1:2])
    adt = dt * a
    tril = _causal_mask(Q)
    acs = _dot(tril.astype(F32), adt, ((1,), (0,)), lax.Precision.HIGHEST)
    acs_t = acs.T
    alast = acs[Q - 1:Q, :]
    r = lax.broadcasted_iota(jnp.int32, (LANES, D_SSD), 0)
    c = lax.broadcasted_iota(jnp.int32, (LANES, D_SSD), 1)
    spread = (lax.shift_right_logical(c, 6) == r).astype(F32)

    def per_head(v):
        return _dot(v, spread, ((1,), (0,)), lax.Precision.HIGH)

    xdt = xs * per_head(dt)
    ub = (xdt * per_head(jnp.exp(alast - acs))).astype(BF16)
    xdtb = xdt.astype(BF16)
    Bs = [bm[:, g * SSD_N:(g + 1) * SSD_N].astype(BF16) for g in range(2)]
    Cs = [cm[:, g * SSD_N:(g + 1) * SSD_N].astype(BF16) for g in range(2)]
    Gs = [_dot(Cs[g], Bs[g], ((1,), (1,))) for g in range(2)]
    yds, yos, adds = [], [], []
    for h in range(SSD_HEADS):
        g = h // (SSD_HEADS // 2)
        sl = slice(h * SSD_P, (h + 1) * SSD_P)
        L = jnp.exp(jnp.where(tril, acs[:, h:h + 1] - acs_t[h:h + 1, :], -jnp.inf))
        yds.append(_dot((Gs[g] * L).astype(BF16), xdtb[:, sl], ((1,), (0,))))
        yos.append(_dot(Cs[g], state[h].astype(BF16), ((1,), (1,))))
        adds.append(_dot(ub[:, sl], Bs[g], ((0,), (0,))))
    y = jnp.concatenate(yds, axis=1) + jnp.concatenate(yos, axis=1) * per_head(jnp.exp(acs)) + per_head(vecs[2:3]) * xs
    decay = jnp.stack([jnp.broadcast_to(jnp.exp(alast[:, h:h + 1]), (SSD_P, SSD_N)) for h in range(SSD_HEADS)])
    return y, jnp.stack(adds) + state * decay


def _ssd_fwd(xact, proj, svec, name):
    S = xact.shape[0]
    nc = S // CHUNK

    def body(x_ref, dt_ref, v_ref, y_ref, st_ref, state):
        @pl.when(pl.program_id(0) == 0)
        def _():
            state[...] = jnp.zeros_like(state)

        st_ref[0] = state[...]
        x = x_ref[...]
        y, sn = fn_ssd_chunk(x[:, 0:512], x[:, 512:768], x[:, 768:1024], dt_ref[...], state[...], v_ref[...])
        y_ref[...] = y
        state[...] = sn

    return pl.pallas_call(
        body, name=name, grid=(nc,),
        in_specs=[pl.BlockSpec((CHUNK, D_CONV), lambda i: (i, 0)), pl.BlockSpec((CHUNK, LANES), lambda i: (i, 16)),
                  pl.BlockSpec((8, LANES), lambda i: (0, 0))],
        out_specs=[pl.BlockSpec((CHUNK, D_SSD), lambda i: (i, 0)),
                   pl.BlockSpec((1, SSD_HEADS, SSD_P, SSD_N), lambda i: (i, 0, 0, 0))],
        out_shape=[jax.ShapeDtypeStruct((S, D_SSD), F32), jax.ShapeDtypeStruct((nc, SSD_HEADS, SSD_P, SSD_N), F32)],
        scratch_shapes=[pltpu.VMEM((SSD_HEADS, SSD_P, SSD_N), F32)],
        compiler_params=_cp(("arbitrary",)),
    )(xact, proj, svec)


def _ssd_bwd(xact, proj, svec, states, dy, name):
    S = xact.shape[0]
    nc = S // CHUNK

    def body(x_ref, dt_ref, v_ref, st_ref, dy_ref, dx_ref, ddt_ref, dv_ref, dstate):
        @pl.when(pl.program_id(0) == 0)
        def _():
            dstate[...] = jnp.zeros_like(dstate)
            dv_ref[...] = jnp.zeros_like(dv_ref)

        x = x_ref[...]
        _, vjp = jax.vjp(fn_ssd_chunk, x[:, 0:512], x[:, 512:768], x[:, 768:1024], dt_ref[...], st_ref[0], v_ref[...])
        dxs, dbm, dcm, ddt, dst, dvec = vjp((dy_ref[...], dstate[...]))
        dx_ref[:, 0:512] = dxs
        dx_ref[:, 512:768] = dbm
        dx_ref[:, 768:1024] = dcm
        ddt_ref[...] = ddt
        dstate[...] = dst
        dv_ref[...] += dvec

    rev = lambda i: (nc - 1 - i, 0)
    return pl.pallas_call(
        body, name=name, grid=(nc,),
        in_specs=[pl.BlockSpec((CHUNK, D_CONV), rev), pl.BlockSpec((CHUNK, LANES), lambda i: (nc - 1 - i, 16)),
                  pl.BlockSpec((8, LANES), lambda i: (0, 0)),
                  pl.BlockSpec((1, SSD_HEADS, SSD_P, SSD_N), lambda i: (nc - 1 - i, 0, 0, 0)),
                  pl.BlockSpec((CHUNK, D_SSD), rev)],
        out_specs=[pl.BlockSpec((CHUNK, D_CONV), rev), pl.BlockSpec((CHUNK, LANES), rev),
                   pl.BlockSpec((8, LANES), lambda i: (0, 0))],
        out_shape=[jax.ShapeDtypeStruct((S, D_CONV), F32), jax.ShapeDtypeStruct((S, LANES), F32),
                   jax.ShapeDtypeStruct((8, LANES), F32)],
        scratch_shapes=[pltpu.VMEM((SSD_HEADS, SSD_P, SSD_N), F32)],
        compiler_params=_cp(("arbitrary",)),
    )(xact, proj, svec, states, dy)


def _ada_fwd(c_all, w_ada, b_sh, name):
    nb = 1536 // 512

    def body(c_ref, w_ref, b_ref, o_ref):
        ca = jax.nn.silu(c_ref[...]).astype(BF16)
        o_ref[0] = _dot(ca, w_ref[0].astype(BF16), ((1,), (0,))) + b_ref[0]

    return pl.pallas_call(
        body, name=name, grid=(DEPTH, nb),
        in_specs=[pl.BlockSpec((8, D_MODEL), lambda l, j: (0, 0)), pl.BlockSpec((1, D_MODEL, 512), lambda l, j: (l, 0, j)),
                  pl.BlockSpec((1, 1, 512), lambda l, j: (l, 0, j))],
        out_specs=pl.BlockSpec((1, 8, 512), lambda l, j: (l, 0, j)),
        out_shape=jax.ShapeDtypeStruct((DEPTH, 8, 1536), F32),
        compiler_params=_cp(("parallel", "parallel")),
    )(c_all, w_ada, b_sh)


def _ada_bwd(c_all_t, dmod_sh, name):
    nb = 1536 // 512

    def body(c_ref, d_ref, o_ref):
        ca = jax.nn.silu(c_ref[...])
        acc = ca[:, 0:1] * d_ref[0, 0:1, :]
        for b in range(1, 8):
            acc = acc + ca[:, b:b + 1] * d_ref[0, b:b + 1, :]
        o_ref[0] = acc

    return pl.pallas_call(
        body, name=name, grid=(DEPTH, nb),
        in_specs=[pl.BlockSpec((D_MODEL, 8), lambda l, j: (0, 0)), pl.BlockSpec((1, 8, 512), lambda l, j: (l, 0, j))],
        out_specs=pl.BlockSpec((1, D_MODEL, 512), lambda l, j: (l, 0, j)),
        out_shape=jax.ShapeDtypeStruct((DEPTH, D_MODEL, 1536), F32),
        compiler_params=_cp(("parallel", "parallel")),
    )(c_all_t, dmod_sh)


def _rows_tile(rows):
    return next(t for t in (512, 256, 128, 64, 32, 16, 8) if rows % t == 0)


SUM_BLOCKS = 4
ADAM_BLOCKS = 8


def _sum_sibling(gs, ls, ci, name):
    n = len(gs)

    def body(c_ref, *refs):
        for p in range(n):
            refs[2 * n + p][...] = refs[2 * p][...] + refs[2 * p + 1][...]

    in_specs, out_specs, out_shape = [], [], []
    for g in gs:
        _, _, rh, cw = g.shape
        rb = rh // SUM_BLOCKS
        in_specs += [pl.BlockSpec((None, None, rb, cw), lambda s, i, c: (s, c[0], i, 0)),
                     pl.BlockSpec((None, rb, cw), lambda s, i, c: (s, i, 0))]
        out_specs.append(pl.BlockSpec((None, rb, cw), lambda s, i, c: (s, i, 0)))
        out_shape.append(jax.ShapeDtypeStruct((4, rh, cw), F32))
    ops = [a for pair in zip(gs, ls) for a in pair]
    return pl.pallas_call(
        body, name=name,
        grid_spec=pltpu.PrefetchScalarGridSpec(num_scalar_prefetch=1, grid=(4, SUM_BLOCKS), in_specs=in_specs, out_specs=out_specs),
        out_shape=out_shape, compiler_params=_cp(("parallel", "parallel")),
    )(ci.reshape(1).astype(jnp.int32), *ops)


def _sum_chips(cs, lands, chip, name):
    n = len(cs)

    def body(c_ref, *refs):
        for p in range(n):
            a = refs[4 * p:4 * p + 4]
            refs[4 * n + p][...] = ((a[0][...] + a[1][...]) + a[2][...]) + a[3][...]

    in_specs, out_specs, out_shape = [], [], []
    for c in cs:
        _, rh, cw = c.shape
        rb = rh // SUM_BLOCKS
        in_specs.append(pl.BlockSpec((None, rb, cw), lambda i, ch: (ch[0], i, 0)))
        in_specs += [pl.BlockSpec((None, rb, cw), functools.partial(lambda i, ch, k: (k, i, 0), k=k)) for k in range(3)]
        out_specs.append(pl.BlockSpec((rb, cw), lambda i, ch: (i, 0)))
        out_shape.append(jax.ShapeDtypeStruct((rh, cw), F32))
    ops = [a for c, l in zip(cs, lands) for a in (c, l, l, l)]
    return pl.pallas_call(
        body, name=name,
        grid_spec=pltpu.PrefetchScalarGridSpec(num_scalar_prefetch=1, grid=(SUM_BLOCKS,), in_specs=in_specs, out_specs=out_specs),
        out_shape=out_shape, compiler_params=_cp(("parallel",)),
    )(chip.reshape(1).astype(jnp.int32), *ops)


def _adam_update(w, m, v, g):
    c1 = 1.0 / (1.0 - ADAM_B1 ** ADAM_STEP)
    c2 = 1.0 / (1.0 - ADAM_B2 ** ADAM_STEP)
    nm = ADAM_B1 * m + (1.0 - ADAM_B1) * g
    nv = ADAM_B2 * v + (1.0 - ADAM_B2) * (g * g)
    return -ADAM_LR * ((nm * c1) / (jnp.sqrt(nv * c2) + ADAM_EPS) + ADAM_WD * w), nm, nv


def _adam_multi(ws, ms, vs, gs, name):
    n = len(ws)

    def body(*refs):
        for p in range(n):
            d, nm, nv = _adam_update(*[refs[4 * p + k][...] for k in range(4)])
            refs[4 * n + 3 * p][...] = d
            refs[4 * n + 3 * p + 1][...] = nm
            refs[4 * n + 3 * p + 2][...] = nv

    in_specs, out_specs, out_shape = [], [], []
    for w in ws:
        _, r, cw = w.shape
        spec = pl.BlockSpec((None, r // ADAM_BLOCKS, cw), lambda l, i: (l, i, 0))
        in_specs += [spec] * 4
        out_specs += [spec] * 3
        out_shape += [jax.ShapeDtypeStruct(w.shape, F32)] * 3
    ops = [a for q in zip(ws, ms, vs, gs) for a in q]
    res = pl.pallas_call(
        body, name=name, grid=(DEPTH, ADAM_BLOCKS), in_specs=in_specs, out_specs=out_specs, out_shape=out_shape,
        compiler_params=_cp(("parallel", "parallel")),
    )(*ops)
    return res[0::3], res[1::3], res[2::3]


def _adam(w, m, v, parts, name):
    rows, width = w.shape
    bm = min(256, _rows_tile(rows))
    np_ = len(parts)
    c1 = 1.0 / (1.0 - ADAM_B1 ** ADAM_STEP)
    c2 = 1.0 / (1.0 - ADAM_B2 ** ADAM_STEP)

    def body(*refs):
        w_ref, m_ref, v_ref = refs[:3]
        g = refs[3][...]
        for r in refs[4:3 + np_]:
            g = g + r[...]
        g_ref, d_ref, nm_ref, nv_ref = refs[3 + np_:]
        nm = ADAM_B1 * m_ref[...] + (1.0 - ADAM_B1) * g
        nv = ADAM_B2 * v_ref[...] + (1.0 - ADAM_B2) * (g * g)
        g_ref[...] = g
        nm_ref[...] = nm
        nv_ref[...] = nv
        d_ref[...] = -ADAM_LR * ((nm * c1) / (jnp.sqrt(nv * c2) + ADAM_EPS) + ADAM_WD * w_ref[...])

    blk = pl.BlockSpec((bm, width), lambda i: (i, 0))
    return pl.pallas_call(
        body, name=name, grid=(rows // bm,),
        in_specs=[blk, blk, blk] + [pl.BlockSpec((bm, width), functools.partial(lambda i, o: (i + o, 0), o=off // bm))
                                    for (_, off) in parts],
        out_specs=[blk, blk, blk, blk],
        out_shape=[jax.ShapeDtypeStruct((rows, width), F32)] * 4,
        compiler_params=_cp(("parallel",)),
    )(w, m, v, *[p[0] for p in parts])


def _coords():
    return lax.axis_index("x"), lax.axis_index("y"), lax.axis_index("c")


def _other_chips(x, y):
    return [(1 - x, y), (x, 1 - y), (1 - x, 1 - y)]


def _ag8(blk, name):
    m_per, n = blk.shape

    def body(x_ref, out_ref, send_sems, recv_sems, local_sem):
        x, y, c = _coords()
        me, sibling = (x, y, c), (x, y, 1 - c)
        chips = _other_chips(x, y)

        def rows(px, py, pc):
            return out_ref.at[pl.ds((4 * px + 2 * py + pc) * m_per, m_per), :]

        def copy(k, block, to, src=None):
            return pltpu.make_async_remote_copy(
                src_ref=rows(*block) if src is None else src, dst_ref=rows(*block),
                send_sem=send_sems.at[k], recv_sem=recv_sems.at[k], device_id=to, device_id_type=MESH)

        mine = pltpu.make_async_copy(x_ref, rows(*me), local_sem)
        mine.start()
        first = [copy(0, me, sibling, src=x_ref)]
        first += [copy(1 + j, me, (*chip, c), src=x_ref) for j, chip in enumerate(chips)]
        for cp in first:
            cp.start()
        passed = [copy(4 + j, (*chip, c), sibling) for j, chip in enumerate(chips)]
        for j, chip in enumerate(chips):
            copy(1 + j, (*chip, c), me).wait_recv()
            passed[j].start()
        copy(0, sibling, me).wait_recv()
        for j, chip in enumerate(chips):
            copy(4 + j, (*chip, 1 - c), me).wait_recv()
        for cp in first + passed:
            cp.wait_send()
        mine.wait()

    return pl.pallas_call(
        body, name=name,
        out_shape=jax.ShapeDtypeStruct((8 * m_per, n), blk.dtype),
        in_specs=[pl.BlockSpec(memory_space=pltpu.VMEM)], out_specs=pl.BlockSpec(memory_space=pltpu.VMEM),
        scratch_shapes=[pltpu.SemaphoreType.DMA((7,)), pltpu.SemaphoreType.DMA((7,)), pltpu.SemaphoreType.DMA],
    )(blk)


HBM_SPEC = pl.BlockSpec(memory_space=pltpu.HBM)
SEM_SPEC = pl.BlockSpec(memory_space=pltpu.SEMAPHORE)
EFFECT = pltpu.SideEffectType.DATAFLOW_SIDE_EFFECTING


def _remote(src, dst, send_sem, recv_sem, to):
    return pltpu.make_async_remote_copy(src_ref=src, dst_ref=dst, send_sem=send_sem, recv_sem=recv_sem,
                                        device_id=to, device_id_type=MESH)


def _ag_list(shards, name):
    n = len(shards)

    def body(*refs):
        sh, out = refs[:n], refs[n:2 * n]
        send_sems, recv_sems = refs[2 * n:]
        x, y, c = _coords()
        sibling = (x, y, 1 - c)
        chips = _other_chips(x, y)
        first = [_remote(sh[p].at[c], out[p].at[2 * x + y, c], send_sems.at[6 * p + j], recv_sems.at[6 * p + j], (px, py, c))
                 for p in range(n) for j, (px, py) in enumerate(chips)]
        for cp in first:
            cp.start()
        passed = []
        for j, (px, py) in enumerate(chips):
            for p in range(n):
                got = out[p].at[2 * px + py, c]
                _remote(got, got, send_sems.at[6 * p + j], recv_sems.at[6 * p + j], (x, y, c)).wait_recv()
                cp = _remote(got, got, send_sems.at[6 * p + 3 + j], recv_sems.at[6 * p + 3 + j], sibling)
                cp.start()
                passed.append(cp)
        for j, (px, py) in enumerate(chips):
            for p in range(n):
                got = out[p].at[2 * px + py, 1 - c]
                _remote(got, got, send_sems.at[6 * p + 3 + j], recv_sems.at[6 * p + 3 + j], (x, y, c)).wait_recv()
        for cp in first + passed:
            cp.wait_send()

    return pl.pallas_call(
        body, name=name,
        out_shape=[jax.ShapeDtypeStruct((4,) + s.shape, s.dtype) for s in shards],
        in_specs=[pl.BlockSpec(memory_space=pl.ANY)] * n, out_specs=[pl.BlockSpec(memory_space=pl.ANY)] * n,
        scratch_shapes=[pltpu.SemaphoreType.DMA((6 * n,)), pltpu.SemaphoreType.DMA((6 * n,))],
    )(*shards)


def _ag_direct_copies(sh, land, send_sems, recv_sems, starting):
    x, y, c = _coords()
    return [_remote(sh[p], land[p].at[2 * x + y] if starting else land[p].at[2 * px + py],
                    send_sems.at[3 * p + j], recv_sems.at[3 * p + j], (px, py, c))
            for p in range(len(sh)) for j, (px, py) in enumerate(_other_chips(x, y))]


def _ag_direct_start(shards, name):
    n = len(shards)

    def body(*refs):
        for cp in _ag_direct_copies(refs[:n], refs[n:2 * n], refs[2 * n], refs[2 * n + 1], True):
            cp.start()
        token = refs[4 * n + 2]
        token[...] = jnp.zeros_like(token)

    lands = [pltpu.with_memory_space_constraint(lax.empty((4,) + s.shape, s.dtype), pltpu.HBM) for s in shards]
    res = pl.pallas_call(
        body, name=name,
        out_shape=(pltpu.SemaphoreType.DMA((3 * n,)), pltpu.SemaphoreType.DMA((3 * n,)))
        + tuple(pltpu.HBM(s.shape, s.dtype) for s in shards) + tuple(pltpu.HBM(l.shape, l.dtype) for l in lands)
        + (jax.ShapeDtypeStruct((8, LANES), F32),),
        in_specs=(HBM_SPEC,) * (2 * n), out_specs=(SEM_SPEC, SEM_SPEC) + (HBM_SPEC,) * (2 * n) + (pl.BlockSpec(memory_space=pltpu.VMEM),),
        input_output_aliases={i: 2 + i for i in range(2 * n)},
        compiler_params=pltpu.CompilerParams(has_side_effects=EFFECT),
    )(*[pltpu.with_memory_space_constraint(s, pltpu.HBM) for s in shards], *lands)
    return res[0], res[1], res[2:2 + n], res[2 + n:2 + 2 * n], res[2 + 2 * n]


def _ag_direct_wait(send_sems, recv_sems, sh_thru, land_thru, after, name):
    n = len(sh_thru)

    def body(*refs):
        sh, land = refs[:n], refs[n:2 * n]
        for cp in _ag_direct_copies(sh, land, refs[2 * n], refs[2 * n + 1], False):
            cp.wait_send()
            cp.wait_recv()

    res = pl.pallas_call(
        body, name=name,
        out_shape=tuple(pltpu.HBM(s.shape, s.dtype) for s in sh_thru) + tuple(pltpu.HBM(l.shape, l.dtype) for l in land_thru),
        in_specs=(HBM_SPEC,) * (2 * n) + (SEM_SPEC, SEM_SPEC, pl.BlockSpec(memory_space=pl.ANY)),
        out_specs=(HBM_SPEC,) * (2 * n), input_output_aliases={i: i for i in range(2 * n)},
        compiler_params=pltpu.CompilerParams(has_side_effects=EFFECT),
    )(*sh_thru, *land_thru, send_sems, recv_sems, after)
    return res[n:]


def _rs_sibling_list(gs, name):
    n = len(gs)

    def body(*refs):
        g, out, send_sems, recv_sems = refs[:n], refs[n:2 * n], refs[2 * n], refs[2 * n + 1]
        x, y, c = _coords()
        cps = [_remote(g[p].at[s, 1 - c], out[p].at[s], send_sems.at[4 * p + s], recv_sems.at[4 * p + s], (x, y, 1 - c))
               for p in range(n) for s in range(4)]
        for cp in cps:
            cp.start()
        for cp in cps:
            cp.wait_recv()
        for cp in cps:
            cp.wait_send()

    return pl.pallas_call(
        body, name=name,
        out_shape=[jax.ShapeDtypeStruct((4,) + g.shape[2:], g.dtype) for g in gs],
        in_specs=[pl.BlockSpec(memory_space=pl.ANY)] * n, out_specs=[pl.BlockSpec(memory_space=pl.ANY)] * n,
        scratch_shapes=[pltpu.SemaphoreType.DMA((4 * n,)), pltpu.SemaphoreType.DMA((4 * n,))],
    )(*gs)


def _rs_chips_copies(cs, land, send_sems, recv_sems):
    x, y, c = _coords()
    return [_remote(cs[p].at[2 * px + py], land[p].at[j], send_sems.at[3 * p + j], recv_sems.at[3 * p + j], (px, py, c))
            for p in range(len(cs)) for j, (px, py) in enumerate(_other_chips(x, y))]


def _rs_chips_start(cs, name):
    n = len(cs)

    def body(*refs):
        for cp in _rs_chips_copies(refs[:n], refs[n:2 * n], refs[2 * n], refs[2 * n + 1]):
            cp.start()
        token = refs[4 * n + 2]
        token[...] = jnp.zeros_like(token)

    lands = [pltpu.with_memory_space_constraint(lax.empty((3,) + c.shape[1:], c.dtype), pltpu.HBM) for c in cs]
    res = pl.pallas_call(
        body, name=name,
        out_shape=(pltpu.SemaphoreType.DMA((3 * n,)), pltpu.SemaphoreType.DMA((3 * n,)))
        + tuple(pltpu.HBM(c.shape, c.dtype) for c in cs) + tuple(pltpu.HBM(l.shape, l.dtype) for l in lands)
        + (jax.ShapeDtypeStruct((8, LANES), F32),),
        in_specs=(HBM_SPEC,) * (2 * n), out_specs=(SEM_SPEC, SEM_SPEC) + (HBM_SPEC,) * (2 * n) + (pl.BlockSpec(memory_space=pltpu.VMEM),),
        input_output_aliases={i: 2 + i for i in range(2 * n)},
        compiler_params=pltpu.CompilerParams(has_side_effects=EFFECT),
    )(*[pltpu.with_memory_space_constraint(c, pltpu.HBM) for c in cs], *lands)
    return res[0], res[1], res[2:2 + n], res[2 + n:2 + 2 * n], res[2 + 2 * n]


def _rs_chips_wait(send_sems, recv_sems, cs_thru, land_thru, after, name):
    n = len(cs_thru)

    def body(*refs):
        for cp in _rs_chips_copies(refs[:n], refs[n:2 * n], refs[2 * n], refs[2 * n + 1]):
            cp.wait_send()
            cp.wait_recv()

    res = pl.pallas_call(
        body, name=name,
        out_shape=tuple(pltpu.HBM(c.shape, c.dtype) for c in cs_thru) + tuple(pltpu.HBM(l.shape, l.dtype) for l in land_thru),
        in_specs=(HBM_SPEC,) * (2 * n) + (SEM_SPEC, SEM_SPEC, pl.BlockSpec(memory_space=pl.ANY)),
        out_specs=(HBM_SPEC,) * (2 * n), input_output_aliases={i: i for i in range(2 * n)},
        compiler_params=pltpu.CompilerParams(has_side_effects=EFFECT),
    )(*cs_thru, *land_thru, send_sems, recv_sems, after)
    return res[:n], res[n:]


def _swap_list(ghs, name):
    n = len(ghs)

    def body(*refs):
        g, out, send_sems, recv_sems = refs[:n], refs[n:2 * n], refs[2 * n], refs[2 * n + 1]
        x, y, c = _coords()
        cps = [_remote(g[p], out[p], send_sems.at[p], recv_sems.at[p], (x, y, 1 - c)) for p in range(n)]
        for cp in cps:
            cp.start()
        for cp in cps:
            cp.wait_recv()
        for cp in cps:
            cp.wait_send()

    return pl.pallas_call(
        body, name=name,
        out_shape=[jax.ShapeDtypeStruct(g.shape, g.dtype) for g in ghs],
        in_specs=[pl.BlockSpec(memory_space=pl.ANY)] * n, out_specs=[pl.BlockSpec(memory_space=pl.ANY)] * n,
        scratch_shapes=[pltpu.SemaphoreType.DMA((n,)), pltpu.SemaphoreType.DMA((n,))],
    )(*ghs)


def _pad_win(w):
    return jnp.concatenate([w[:, :416], jnp.zeros((w.shape[0], 96), w.dtype), w[:, 416:1952],
                            w[:, 1952:1960], jnp.zeros((w.shape[0], 120), w.dtype)], axis=1)


def _unpad_win(g):
    return jnp.concatenate([g[:, :416], g[:, 512:2048], g[:, 2048:2056]], axis=1)


def _pad_wq(w):
    return jnp.pad(w.reshape(Q_LORA, HEADS, QK_DIM), ((0, 0), (0, 0), (0, LANES - QK_DIM))).reshape(Q_LORA, HEADS * LANES)


def _unpad_wq(g):
    return g.reshape(Q_LORA, HEADS, LANES)[:, :, :QK_DIM].reshape(Q_LORA, HEADS * QK_DIM)


def _cols_to_shards(a):
    r, c4 = a.shape
    return a.reshape(r, 4, c4 // 4).transpose(1, 0, 2)


def _shards_to_cols(a):
    _, r, c = a.shape
    return a.transpose(1, 0, 2).reshape(r, 4 * c)


def _pack_small(tree):
    parts = []
    for l in range(DEPTH):
        for (n, k) in SMALL:
            parts.append(jnp.pad(tree[n][l].reshape(-1), (0, -k % LANES)))
    flat = jnp.concatenate(parts)
    return jnp.pad(flat, (0, SMALL_ROWS * LANES - flat.shape[0])).reshape(SMALL_ROWS, LANES)


def _unpack_small(buf):
    flat = buf.reshape(-1)
    out = {n: [] for (n, _) in SMALL}
    o = 0
    for l in range(DEPTH):
        for (n, k) in SMALL:
            out[n].append(flat[o:o + k])
            o += k + (-k % LANES)
    return {n: jnp.stack(v) for n, v in out.items()}


def _vec(v, width=LANES):
    return jnp.pad(v.reshape(1, -1), ((0, 0), (0, width - v.shape[-1])))


def kernel(x, c, positions, norm1_w, norm2_w, w_ada, b_ada, w_in, q_a_norm_w, w_q_up, kv_a_norm_w, w_kv_up, q_nope_norm_w, q_pe_norm_w, k_nope_norm_w, k_pe_norm_w, conv_w, conv_b, dt_bias, a_log, d_skip, ssd_norm_w, w_out, w_gate_up, w_down, loss_target, m_norm1_w, m_norm2_w, m_w_ada, m_b_ada, m_w_in, m_q_a_norm_w, m_w_q_up, m_kv_a_norm_w, m_w_kv_up, m_q_nope_norm_w, m_q_pe_norm_w, m_k_nope_norm_w, m_k_pe_norm_w, m_conv_w, m_conv_b, m_dt_bias, m_a_log, m_d_skip, m_ssd_norm_w, m_w_out, m_w_gate_up, m_w_down, v_norm1_w, v_norm2_w, v_w_ada, v_b_ada, v_w_in, v_q_a_norm_w, v_w_q_up, v_kv_a_norm_w, v_w_kv_up, v_q_nope_norm_w, v_q_pe_norm_w, v_k_nope_norm_w, v_k_pe_norm_w, v_conv_w, v_conv_b, v_dt_bias, v_a_log, v_d_skip, v_ssd_norm_w, v_w_out, v_w_gate_up, v_w_down):
    W = dict(zip(WEIGHTS, (norm1_w, norm2_w, w_ada, b_ada, w_in, q_a_norm_w, w_q_up, kv_a_norm_w, w_kv_up, q_nope_norm_w, q_pe_norm_w, k_nope_norm_w, k_pe_norm_w, conv_w, conv_b, dt_bias, a_log, d_skip, ssd_norm_w, w_out, w_gate_up, w_down)))
    M = dict(zip(WEIGHTS, (m_norm1_w, m_norm2_w, m_w_ada, m_b_ada, m_w_in, m_q_a_norm_w, m_w_q_up, m_kv_a_norm_w, m_w_kv_up, m_q_nope_norm_w, m_q_pe_norm_w, m_k_nope_norm_w, m_k_pe_norm_w, m_conv_w, m_conv_b, m_dt_bias, m_a_log, m_d_skip, m_ssd_norm_w, m_w_out, m_w_gate_up, m_w_down)))
    V = dict(zip(WEIGHTS, (v_norm1_w, v_norm2_w, v_w_ada, v_b_ada, v_w_in, v_q_a_norm_w, v_w_q_up, v_kv_a_norm_w, v_w_kv_up, v_q_nope_norm_w, v_q_pe_norm_w, v_k_nope_norm_w, v_k_pe_norm_w, v_conv_w, v_conv_b, v_dt_bias, v_a_log, v_d_skip, v_ssd_norm_w, v_w_out, v_w_gate_up, v_w_down)))
    S = x.shape[1]
    xi, yi, ci = _coords()
    chip = 2 * xi + yi
    dev = 2 * chip + ci
    x0 = x[0]
    tgt = loss_target[0]

    inv_freq = 1.0 / (ROPE_THETA ** (jnp.arange(0, ROPE, 2, dtype=F32) / ROPE))
    ang = positions[0].astype(F32)[:, None] * inv_freq
    cos, sin = jnp.cos(ang), jnp.sin(ang)
    z16, z32, z64 = jnp.zeros((S, 16), F32), jnp.zeros((S, 32), F32), jnp.zeros((S, 64), F32)
    tab_c = jnp.concatenate([jnp.ones((S, 64), F32), cos, cos, z32], axis=1)
    tab_s1 = jnp.concatenate([z64, z16, sin, z32], axis=1)
    tab_s2 = jnp.concatenate([z64, -sin, z16, z32], axis=1)

    blk0 = jnp.concatenate([c.reshape(-1), W['conv_w'].reshape(-1)]).reshape(24, LANES)
    g0 = _ag8(blk0, "ag_c_conv").reshape(8, 24 * LANES)
    c_all = g0[:, :D_MODEL]
    conv_full = g0[0::2, D_MODEL:].reshape(4, DEPTH, CONV_TAPS, 256).transpose(1, 2, 0, 3).reshape(DEPTH, CONV_TAPS, D_CONV)

    sh = [[W[n][l].astype(BF16) for n in BIG] for l in range(DEPTH)]
    got0 = _ag_list([a.reshape(2, a.shape[0] // 2, a.shape[1]) for a in sh[0]], "ag_w0")

    def layer_weights(gathered, own):
        st = {n: lax.dynamic_update_slice_in_dim(a.reshape(4, -1, a.shape[-1]), o[None], chip, axis=0)
              for n, a, o in zip(BIG, gathered, own)}
        return dict(w_in=_pad_win(_shards_to_cols(st['w_in'])), w_q_up=_pad_wq(_shards_to_cols(st['w_q_up'])),
                    w_kv_up=_shards_to_cols(st['w_kv_up']), w_out=st['w_out'].reshape(D_MODEL, D_MODEL),
                    w_gate_up=st['w_gate_up'], w_down=st['w_down'].reshape(D_FF, D_MODEL))

    LW = [layer_weights(got0, sh[0]), None]

    b_sh = lax.dynamic_slice_in_dim(W['b_ada'], chip * 1536, 1536, axis=1).reshape(DEPTH, 1, 1536)
    mod_sh = _ada_fwd(c_all, W['w_ada'], b_sh, "ada_fwd")
    g1 = _ag8(mod_sh.reshape(192, LANES), "ag_mod").reshape(8, DEPTH, 8, 1536)
    mod_all = g1[0::2].transpose(1, 2, 0, 3).reshape(DEPTH, 8, 6 * D_MODEL)
    mod = lax.dynamic_index_in_dim(mod_all, dev, axis=1, keepdims=False)
    mod, sh1 = lax.optimization_barrier((mod, sh[1]))
    ag1 = _ag_direct_start(sh1, "ag_w1_start")

    def mvec(l, k):
        return mod[l, k * D_MODEL:(k + 1) * D_MODEL].reshape(1, D_MODEL)

    def small(name, l, width=None):
        v = W[name][l]
        return _vec(v, width or v.shape[-1])

    def wq_vec(l):
        return _vec(jnp.concatenate([W['q_nope_norm_w'][l], W['q_pe_norm_w'][l]]))

    def wk_vec(l):
        return _vec(jnp.concatenate([W['k_nope_norm_w'][l], W['k_pe_norm_w'][l]]))

    def conv_vec(l):
        return jnp.concatenate([conv_full[l], W['conv_b'][l].reshape(1, D_CONV), jnp.zeros((3, D_CONV), F32)], axis=0)

    def ssd_vec(l):
        return jnp.concatenate([_vec(W['dt_bias'][l]), _vec(W['a_log'][l]), _vec(W['d_skip'][l]), jnp.zeros((5, LANES), F32)], axis=0)

    sv = []
    xcur = x0
    h1 = _row_fwd(fn_norm_mod, "norm_mod_f", [(x0, 0, D_MODEL)], [small('norm1_w', 0) + ag1[4][0, 0], mvec(0, 1), mvec(0, 0)],
                  [(D_MODEL, BF16)])[0]
    fin = None
    for l in range(DEPTH):
        if l == 1:
            LW[1] = layer_weights(_ag_direct_wait(ag1[0], ag1[1], ag1[2], ag1[3], xcur, "ag_w1_wait"), sh[1])
        lw = LW[l]
        t = dict(xcur=xcur, h1=h1)
        t['proj'] = proj = _mm(h1, lw['w_in'], 'nn', f"mm_in_{l}")
        t['qa_n'], t['kva_n'] = _row_fwd(fn_lat_norm, f"lat_norm_f{l}", [(proj, 0, 256), (proj, 2, 128)],
                                         [small('q_a_norm_w', l), small('kv_a_norm_w', l)], [(256, BF16), (128, BF16)])
        t['q'] = _mm(t['qa_n'], lw['w_q_up'], 'nn', f"mm_q_{l}")
        t['kv'] = _mm(t['kva_n'], lw['w_kv_up'], 'nn', f"mm_kv_{l}")
        t['qf'], t['kf'], t['vv'] = _row_fwd(
            fn_qk_prep, f"qk_prep_f{l}",
            [(t['q'], 0, 1024), (t['kv'], 0, 1024), (proj, 3, 128), (tab_c, 0, 128), (tab_s1, 0, 128), (tab_s2, 0, 128)],
            [wq_vec(l), wk_vec(l)], [(1024, BF16), (1024, BF16), (1024, BF16)])
        t['ao'], t['lse'] = _attn_fwd(t['qf'], t['kf'], t['vv'], f"attn_f{l}")
        t['xact'] = _conv_fwd(proj, conv_vec(l), f"conv_f{l}")
        t['y'], t['states'] = _ssd_fwd(t['xact'], proj, ssd_vec(l), f"ssd_f{l}")
        t['mix'] = _row_fwd(fn_gated_mix, f"gated_f{l}", [(t['y'], 0, 512), (proj, 1, 512), (t['ao'], 0, 512)],
                            [small('ssd_norm_w', l)], [(1024, BF16)])[0]
        t['mo'] = _mm(t['mix'], lw['w_out'], 'nn', f"mm_out_{l}")
        t['x1'], t['h2'] = _row_fwd(fn_resid_norm, f"resid_mid_f{l}", [(xcur, 0, D_MODEL), (t['mo'], 0, D_MODEL)],
                                    [mvec(l, 2), small('norm2_w', l), mvec(l, 4), mvec(l, 3)],
                                    [(D_MODEL, F32), (D_MODEL, BF16)])
        t['gu'] = _mm(t['h2'], lw['w_gate_up'], 'nn', f"mm_gu_{l}", stack='b')
        t['act'] = _row_fwd(fn_swiglu, f"swiglu_f{l}", [(t['gu'], 0, 2 * D_FF)], [], [(D_FF, BF16)], tm=128)[0]
        t['ff'] = _mm(t['act'], lw['w_down'], 'nn', f"mm_down_{l}")
        if l + 1 < DEPTH:
            xcur, h1 = _row_fwd(fn_resid_norm, f"resid_end_f{l}", [(t['x1'], 0, D_MODEL), (t['ff'], 0, D_MODEL)],
                                [mvec(l, 5), small('norm1_w', l + 1), mvec(l + 1, 1), mvec(l + 1, 0)],
                                [(D_MODEL, F32), (D_MODEL, BF16)])
        else:
            fin = _final(t['x1'], t['ff'], tgt, mvec(l, 5), "final_loss")
        sv.append(t)

    dx1, dff, dg2_last, loss_acc = fin
    gfull = {n: [None] * DEPTH for n in BIG}
    gsm = {n: [None] * DEPTH for (n, _) in SMALL}
    dmod = [[None] * 6 for _ in range(DEPTH)]
    dmod[DEPTH - 1][5] = dg2_last
    grad_x = None
    pending = []

    def rs_begin(l, names, tag):
        g4 = [gfull[n][l].reshape(4, 2, gfull[n][l].shape[1] // 2, gfull[n][l].shape[2]) for n in names]
        sib = _rs_sibling_list(g4, f"rs_sibling_{tag}")
        cs = _sum_sibling(g4, sib, ci, f"sum_sibling_{tag}")
        h = _rs_chips_start(cs, f"rs_chips_start_{tag}")
        pending.append((l, names, h))
        return h[4][0, 0]

    tie_l1 = tie_l0a = 0.0

    for l in reversed(range(DEPTH)):
        t = sv[l]
        lw = LW[l]
        proj = t['proj']
        dact = _mm(dff, lw['w_down'], 'nt', f"mm_down_dx{l}")
        gfull['w_down'][l] = _mm(t['act'], dff, 'tn', f"mm_down_dw{l}").reshape(4, D_FF // 4, D_MODEL)
        dgu = _row_bwd(fn_swiglu, f"swiglu_b{l}", [(t['gu'], 0, 2 * D_FF)], [], [(dact, 0, D_FF)], [0], [], tm=128, ddtypes=[BF16])[0]
        dh2 = _mm(dgu, lw['w_gate_up'], 'nt', f"mm_gu_dx{l}", stack='b')
        gfull['w_gate_up'][l] = _mm(t['h2'], dgu, 'tn', f"mm_gu_dw{l}", stack='out')
        if l == 0:
            tie_l0a = rs_begin(0, EARLY, "l0a")
        dxc, dmo, dmod[l][2], gsm['norm2_w'][l], dmod[l][4], dmod[l][3] = _row_bwd(
            fn_resid_norm, f"resid_mid_b{l}", [(t['xcur'], 0, D_MODEL), (t['mo'], 0, D_MODEL)],
            [mvec(l, 2) + (tie_l1 if l == 0 else 0.0), small('norm2_w', l), mvec(l, 4), mvec(l, 3)],
            [(dx1, 0, D_MODEL), (dh2, 0, D_MODEL)], [0, 1], [0, 1, 2, 3], ddtypes=[F32, BF16])
        dmix = _mm(dmo, lw['w_out'], 'nt', f"mm_out_dx{l}")
        gfull['w_out'][l] = _mm(t['mix'], dmo, 'tn', f"mm_out_dw{l}").reshape(4, D_MODEL // 4, D_MODEL)
        dy, dz, gsm['ssd_norm_w'][l] = _row_bwd(fn_gated_norm, f"gated_b{l}", [(t['y'], 0, 512), (proj, 1, 512)],
                                                [small('ssd_norm_w', l)], [(dmix, 1, 512)], [0, 1], [0])
        dxact, ddt, dsv = _ssd_bwd(t['xact'], proj, ssd_vec(l) + (tie_l0a if l == 0 else 0.0), t['states'], dy, f"ssd_b{l}")
        gsm['dt_bias'][l], gsm['a_log'][l], gsm['d_skip'][l] = dsv[0, :8], dsv[1, :8], dsv[2, :8]
        dxbc, dcv = _conv_bwd(proj, conv_vec(l), dxact, f"conv_b{l}")
        gsm['conv_w'][l] = dcv[:CONV_TAPS]
        gsm['conv_b'][l] = dcv[CONV_TAPS]
        delta_r = _attn_delta(dmix, t['ao'], f"attn_delta{l}")
        dqT, dkf, dvv = _attn_bwd(t['qf'], t['kf'], t['kf'].T, t['vv'], dmix, t['lse'], delta_r, f"attn_b{l}")
        dqf = dqT.T
        dq, dkv, dkpe, dwq, dwk = _row_bwd(
            fn_qk_prep, f"qk_prep_b{l}",
            [(t['q'], 0, 1024), (t['kv'], 0, 1024), (proj, 3, 128), (tab_c, 0, 128), (tab_s1, 0, 128), (tab_s2, 0, 128)],
            [wq_vec(l), wk_vec(l)], [(dqf, 0, 1024), (dkf, 0, 1024), (dvv, 0, 1024)], [0, 1, 2], [0, 1],
            ddtypes=[BF16, BF16, F32])
        gsm['q_nope_norm_w'][l], gsm['q_pe_norm_w'][l] = dwq[0, :NOPE], dwq[0, NOPE:QK_DIM]
        gsm['k_nope_norm_w'][l], gsm['k_pe_norm_w'][l] = dwk[0, :NOPE], dwk[0, NOPE:QK_DIM]
        dqa_n = _mm(dq, lw['w_q_up'], 'nt', f"mm_q_dx{l}")
        gfull['w_q_up'][l] = _cols_to_shards(_unpad_wq(_mm(t['qa_n'], dq, 'tn', f"mm_q_dw{l}")))
        dkva_n = _mm(dkv, lw['w_kv_up'], 'nt', f"mm_kv_dx{l}")
        gfull['w_kv_up'][l] = _cols_to_shards(_mm(t['kva_n'], dkv, 'tn', f"mm_kv_dw{l}"))
        dqa, dkva, dqw, dkvw = _row_bwd(fn_lat_norm, f"lat_norm_b{l}", [(proj, 0, 256), (proj, 2, 128)],
                                        [small('q_a_norm_w', l), small('kv_a_norm_w', l)],
                                        [(dqa_n, 0, 256), (dkva_n, 0, 128)], [0, 1], [0, 1])
        gsm['q_a_norm_w'][l], gsm['kv_a_norm_w'][l] = dqw[0], dkvw[0]
        dproj = jnp.concatenate([dqa, dkva, dkpe, dz, dxbc, ddt], axis=1).astype(BF16)
        dh1 = _mm(dproj, lw['w_in'], 'nt', f"mm_in_dx{l}")
        gfull['w_in'][l] = _cols_to_shards(_unpad_win(_mm(t['h1'], dproj, 'tn', f"mm_in_dw{l}")))
        if l > 0:
            p = sv[l - 1]
            dx1, dff, dmod[l - 1][5], gsm['norm1_w'][l], dmod[l][1], dmod[l][0] = _row_bwd(
                fn_resid_norm, f"resid_end_b{l - 1}", [(p['x1'], 0, D_MODEL), (p['ff'], 0, D_MODEL)],
                [mvec(l - 1, 5), small('norm1_w', l), mvec(l, 1), mvec(l, 0)], [(dxc, 0, D_MODEL), (dh1, 0, D_MODEL)],
                [0, 1], [0, 1, 2, 3], ddtypes=[F32, BF16])
            tie_l1 = rs_begin(l, BIG, f"l{l}")
        else:
            grad_x, gsm['norm1_w'][l], dmod[l][1], dmod[l][0] = _row_bwd(
                fn_norm_mod_pass, "norm_mod_b", [(x0, 0, D_MODEL)], [small('norm1_w', 0), mvec(0, 1), mvec(0, 0)],
                [(dxc, 0, D_MODEL), (dh1, 0, D_MODEL)], [0], [0, 1, 2])
        for n in ('norm1_w', 'norm2_w', 'ssd_norm_w'):
            gsm[n][l] = gsm[n][l][0]

    for l in range(DEPTH):
        gsm['b_ada'][l] = jnp.concatenate([d[0] for d in dmod[l]])
    sm_part = _pack_small({n: jnp.stack(v) for n, v in gsm.items()}).at[SMALL_ROWS - 1, 0].set(loss_acc[0, 0])
    sm_all = _ag8(sm_part, "ag_small")
    loss = jnp.sum(sm_all.reshape(8, SMALL_ROWS, LANES)[:, SMALL_ROWS - 1, 0])
    sm_all, late = lax.optimization_barrier((sm_all, [gfull[n][0] for n in BIG[2:]]))
    for n, g in zip(BIG[2:], late):
        gfull[n][0] = g
    tie_l0b = rs_begin(0, BIG[2:], "l0b")

    def with_conv(tree):
        wide = lax.dynamic_update_slice_in_dim(jnp.zeros((DEPTH, CONV_TAPS, D_CONV), F32), tree['conv_w'], chip * 256, axis=2)
        return {**tree, 'conv_w': wide}

    g_sm, d_sm, m_sm, v_sm = _adam(_pack_small(with_conv(W)) + tie_l0b, _pack_small(with_conv(M)), _pack_small(with_conv(V)),
                                   [(sm_all, d * SMALL_ROWS) for d in range(8)], "adam_small")
    out_small = [_unpack_small(b) for b in (g_sm, d_sm, m_sm, v_sm)]
    for o in out_small:
        o['conv_w'] = lax.dynamic_slice_in_dim(o['conv_w'].reshape(DEPTH, CONV_TAPS, D_CONV), chip * 256, 256, axis=2)

    dmod_all = sm_all.reshape(8, SMALL_ROWS * LANES)
    per_layer = sum(k + (-k % LANES) for (_, k) in SMALL)
    dmod_sh = jnp.stack([lax.dynamic_slice_in_dim(dmod_all[:, l * per_layer:l * per_layer + 6 * D_MODEL], chip * 1536, 1536, axis=1)
                         for l in range(DEPTH)])
    g_ada = _ada_bwd(c_all.T, dmod_sh, "ada_bwd")
    ada = _adam(W['w_ada'].reshape(DEPTH * D_MODEL, 1536), M['w_ada'].reshape(DEPTH * D_MODEL, 1536),
                V['w_ada'].reshape(DEPTH * D_MODEL, 1536), [(g_ada.reshape(DEPTH * D_MODEL, 1536), 0)], "adam_ada")
    out_ada = [a.reshape(DEPTH, D_MODEL, 1536) for a in ada]

    keys, cs_all, land_all = [], [], []
    for (l, names, (send_sems, recv_sems, cs_thru, land_thru, _)) in pending:
        cs, lands = _rs_chips_wait(send_sems, recv_sems, cs_thru, land_thru, ada[3], f"rs_chips_wait_l{l}{len(names)}")
        keys += [(l, n) for n in names]
        cs_all += list(cs)
        land_all += list(lands)
    ghalf = _sum_chips(cs_all, land_all, chip, "sum_chips")
    gother = _swap_list(ghalf, "swap_halves")
    gshard = {k: jnp.where(ci == 0, jnp.concatenate([a, b]), jnp.concatenate([b, a])) for k, a, b in zip(keys, ghalf, gother)}
    g_big = [jnp.stack([gshard[(l, n)] for l in range(DEPTH)]) for n in BIG]
    d_big, m_big, v_big = _adam_multi([W[n] for n in BIG], [M[n] for n in BIG], [V[n] for n in BIG], g_big, "adam_big")
    out_big = [dict(zip(BIG, o)) for o in (g_big, d_big, m_big, v_big)]

    outs = [loss, grad_x[None]]
    for k in range(4):
        for n in WEIGHTS:
            if n == 'w_ada':
                outs.append(out_ada[k])
            elif n in BIG:
                outs.append(out_big[k][n])
            else:
                outs.append(out_small[k][n])
    return tuple(outs)
```

```python
import functools

import jax
import jax.numpy as jnp
from jax import lax
from jax.experimental import pallas as pl
from jax.experimental.pallas import tpu as pltpu

F32 = jnp.float32
BF16 = jnp.bfloat16
MESH = pl.DeviceIdType.MESH

D_MODEL = 1024
DEPTH = 2
HEADS = 8
NOPE = 64
ROPE = 32
QK_DIM = NOPE + ROPE
Q_LORA = 256
KV_LORA = 128
SSD_HEADS = 8
SSD_P = 64
SSD_N = 128
CHUNK = 128
CONV_TAPS = 4
D_SSD = 512
D_CONV = 1024
D_FF = 2816
D_IN = 1960
D_IN_PAD = 2176
EPS = 1e-6
ROPE_THETA = 10000.0
ATT_SCALE = QK_DIM ** -0.5
NEG = -1e30
LANES = 128
VMEM_LIMIT = 48 * 1024 * 1024

ADAM_LR, ADAM_B1, ADAM_B2, ADAM_EPS, ADAM_WD, ADAM_STEP = 0.001, 0.9, 0.999, 1e-08, 0.01, 10

WEIGHTS = ['norm1_w', 'norm2_w', 'w_ada', 'b_ada', 'w_in', 'q_a_norm_w', 'w_q_up', 'kv_a_norm_w', 'w_kv_up',
           'q_nope_norm_w', 'q_pe_norm_w', 'k_nope_norm_w', 'k_pe_norm_w', 'conv_w', 'conv_b', 'dt_bias', 'a_log',
           'd_skip', 'ssd_norm_w', 'w_out', 'w_gate_up', 'w_down']
BIG = ['w_down', 'w_gate_up', 'w_out', 'w_kv_up', 'w_q_up', 'w_in']
EARLY = BIG[:2]
FIRST = BIG[3:]
REST = BIG[:3]
SMALL = [('b_ada', 6144), ('conv_w', 4096), ('norm1_w', 1024), ('norm2_w', 1024), ('conv_b', 1024), ('ssd_norm_w', 512),
         ('q_a_norm_w', 256), ('kv_a_norm_w', 128), ('q_nope_norm_w', 64), ('q_pe_norm_w', 32),
         ('k_nope_norm_w', 64), ('k_pe_norm_w', 32), ('dt_bias', 8), ('a_log', 8), ('d_skip', 8)]
SMALL_ROWS = 240


def _cp(sem=None, **kw):
    return pltpu.CompilerParams(dimension_semantics=sem, vmem_limit_bytes=VMEM_LIMIT, **kw)


def _dot(a, b, dims, prec=None):
    return lax.dot_general(a, b, (dims, ((), ())), preferred_element_type=F32, precision=prec)


def _tile(dim, target):
    best = 0
    for t in range(LANES, min(dim, target) + 1, LANES):
        if dim % t == 0:
            best = t
    if best < 256 and dim <= 2304:
        return dim
    return best


def _mm(a, b, mode, name, out_dtype=F32, stack=None):
    ns = None
    if stack == 'b':
        ns = b.shape[2]
        if mode == 'nn':
            (M, K), N = a.shape, 4 * ns
        else:
            (M, K), N = a.shape, b.shape[1]
    elif mode == 'nn':
        (M, K), (_, N) = a.shape, b.shape
    elif mode == 'nt':
        (M, K), (N, _) = a.shape, b.shape
    else:
        (K, M), (_, N) = a.shape, b.shape
    if stack == 'out':
        ns = N // 4
    tm, tn, tk = _tile(M, 1408 if mode == 'tn' else 1024), _tile(N, 1408), _tile(K, 1408)
    if stack == 'b' and mode == 'nt':
        tk = ns
    elif stack is not None:
        tn = ns
    nk = K // tk
    dims = {'nn': ((1,), (0,)), 'nt': ((1,), (1,)), 'tn': ((0,), (0,))}[mode]

    def body(a_ref, b_ref, o_ref, *acc):
        part = _dot(a_ref[...].astype(BF16), b_ref[...].astype(BF16), dims)
        if nk == 1:
            o_ref[...] = part.astype(o_ref.dtype)
            return
        k = pl.program_id(2)

        @pl.when(k == 0)
        def _():
            acc[0][...] = part

        @pl.when(k > 0)
        def _():
            acc[0][...] += part

        @pl.when(k == nk - 1)
        def _():
            o_ref[...] = acc[0][...].astype(o_ref.dtype)

    a_spec = pl.BlockSpec((tk, tm), lambda i, j, k: (k, i)) if mode == 'tn' else pl.BlockSpec((tm, tk), lambda i, j, k: (i, k))
    b_spec = pl.BlockSpec((tn, tk), lambda i, j, k: (j, k)) if mode == 'nt' else pl.BlockSpec((tk, tn), lambda i, j, k: (k, j))
    o_spec, o_shape = pl.BlockSpec((tm, tn), lambda i, j, k: (i, j)), (M, N)
    if stack == 'b':
        b_spec = (pl.BlockSpec((None, tn, ns), lambda i, j, k: (k, j, 0)) if mode == 'nt'
                  else pl.BlockSpec((None, tk, ns), lambda i, j, k: (j, k, 0)))
    if stack == 'out':
        o_spec, o_shape = pl.BlockSpec((None, tm, ns), lambda i, j, k: (j, i, 0)), (4, M, ns)
    return pl.pallas_call(
        body, name=name, grid=(M // tm, N // tn, nk),
        in_specs=[a_spec, b_spec], out_specs=o_spec,
        out_shape=jax.ShapeDtypeStruct(o_shape, out_dtype),
        scratch_shapes=[pltpu.VMEM((tm, tn), F32)] if nk > 1 else [],
        compiler_params=_cp(("parallel", "parallel", "arbitrary")),
    )(a, b)


def _rspec(tm, w, cb):
    return pl.BlockSpec((tm, w), lambda i: (i, cb))


def _vspec(shape):
    return pl.BlockSpec(shape, lambda i: (0,) * len(shape))


def _row_fwd(fn, name, rows, vecs, outs, tm=256):
    S = rows[0][0].shape[0]
    tm = min(tm, S)
    nin = len(rows) + len(vecs)

    def body(*refs):
        res = fn(*[r[...] for r in refs[:nin]])
        for o_ref, r in zip(refs[nin:], res):
            o_ref[...] = r.astype(o_ref.dtype)

    return pl.pallas_call(
        body, name=name, grid=(S // tm,),
        in_specs=[_rspec(tm, w, cb) for (_, cb, w) in rows] + [_vspec(v.shape) for v in vecs],
        out_specs=[_rspec(tm, w, 0) for (w, _) in outs],
        out_shape=[jax.ShapeDtypeStruct((S, w), dt) for (w, dt) in outs],
        compiler_params=_cp(("parallel",)),
    )(*[r[0] for r in rows], *vecs)


def _row_bwd(fn, name, rows, vecs, cts, drows, dvecs, tm=256, ddtypes=None):
    S = rows[0][0].shape[0]
    ddtypes = ddtypes or [F32] * len(drows)
    tm = min(tm, S)
    nr, nv, nc = len(rows), len(vecs), len(cts)
    didx = list(drows) + [nr + j for j in dvecs]

    def body(*refs):
        vals = [r[...] for r in refs[:nr + nv]]
        ct = tuple(r[...].astype(F32) for r in refs[nr + nv:nr + nv + nc])
        outs = refs[nr + nv + nc:]

        def g(*d):
            a = list(vals)
            for k, val in zip(didx, d):
                a[k] = val
            return tuple(fn(*a))

        _, vjp = jax.vjp(g, *[vals[k] for k in didx])
        grads = vjp(ct)
        for o, gr in zip(outs[:len(drows)], grads[:len(drows)]):
            o[...] = gr.astype(o.dtype)

        @pl.when(pl.program_id(0) == 0)
        def _():
            for o in outs[len(drows):]:
                o[...] = jnp.zeros_like(o)

        for o, gr in zip(outs[len(drows):], grads[len(drows):]):
            o[...] += gr

    return pl.pallas_call(
        body, name=name, grid=(S // tm,),
        in_specs=[_rspec(tm, w, cb) for (_, cb, w) in rows] + [_vspec(v.shape) for v in vecs]
        + [_rspec(tm, w, cb) for (_, cb, w) in cts],
        out_specs=[_rspec(tm, rows[k][2], 0) for k in drows] + [_vspec(vecs[j].shape) for j in dvecs],
        out_shape=[jax.ShapeDtypeStruct((S, rows[k][2]), dt) for k, dt in zip(drows, ddtypes)]
        + [jax.ShapeDtypeStruct(vecs[j].shape, F32) for j in dvecs],
        compiler_params=_cp(("arbitrary",)),
    )(*[r[0] for r in rows], *vecs, *[c[0] for c in cts])


def _rms(x):
    return x * lax.rsqrt(jnp.mean(x * x, axis=-1, keepdims=True) + EPS)


def fn_norm_mod(x, nw, sc, sh):
    return (_rms(x) * nw * (1.0 + sc) + sh,)


def fn_norm_mod_pass(x, nw, sc, sh):
    return (x, _rms(x) * nw * (1.0 + sc) + sh)


def fn_resid_norm(x, d, g, nw, sc, sh):
    xn = x + g * d
    return (xn, _rms(xn) * nw * (1.0 + sc) + sh)


def fn_lat_norm(qa, kva, qw, kvw):
    return (_rms(qa) * qw, _rms(kva) * kvw)


@functools.partial(jax.custom_vjp, nondiff_argnums=(1,))
def _lroll(x, s):
    return pltpu.roll(x, s, 1)


def _lroll_fwd(x, s):
    return pltpu.roll(x, s, 1), None


def _lroll_bwd(s, _, g):
    return (pltpu.roll(g, (LANES - s) % LANES, 1),)


_lroll.defvjp(_lroll_fwd, _lroll_bwd)


def _lane_masks(shape):
    lane = lax.broadcasted_iota(jnp.int32, shape, 1)
    return (lane < NOPE).astype(F32), ((lane >= NOPE) & (lane < QK_DIM)).astype(F32)


def _rope(t, tc, ts1, ts2):
    return t * tc + _lroll(t, 16) * ts1 + _lroll(t, LANES - 16) * ts2


def fn_qk_prep(q, kv, kpe, tc, ts1, ts2, wq, wk):
    mn, mp = _lane_masks((1, LANES))
    mhi = 1.0 - mn

    def head_norm(t, w):
        rn = lax.rsqrt(jnp.sum(t * t * mn, axis=-1, keepdims=True) * (1.0 / NOPE) + EPS)
        rp = lax.rsqrt(jnp.sum(t * t * mp, axis=-1, keepdims=True) * (1.0 / ROPE) + EPS)
        return t * (rn * mn + rp * mp) * w

    kp = _rope(head_norm(_lroll(kpe, NOPE), wk) * mp, tc, ts1, ts2)
    qs, ks, vs = [], [], []
    for h in range(HEADS):
        qs.append(_rope(head_norm(q[:, h * LANES:(h + 1) * LANES], wq), tc, ts1, ts2))
        t = kv[:, h * LANES:(h + 1) * LANES]
        ks.append(head_norm(t, wk) * mn + kp)
        vs.append(_lroll(t, NOPE) * mn + mhi)
    return (jnp.concatenate(qs, axis=1), jnp.concatenate(ks, axis=1), jnp.concatenate(vs, axis=1))


def fn_gated_norm(y, z, w):
    u = y * jax.nn.silu(z)
    half = D_SSD // 2
    return (jnp.concatenate([_rms(u[:, :half]), _rms(u[:, half:])], axis=1) * w,)


def fn_gated_mix(y, z, ao, w):
    return (jnp.concatenate([ao, fn_gated_norm(y, z, w)[0]], axis=1),)


def fn_swiglu(gu):
    return (jax.nn.silu(gu[:, :D_FF]) * gu[:, D_FF:],)


def _final(x1, ff, tgt, g2, name):
    S = x1.shape[0]
    tm = min(256, S)

    def body(x_ref, f_ref, t_ref, g_ref, dx_ref, df_ref, dg_ref, l_ref):
        @pl.when(pl.program_id(0) == 0)
        def _():
            dg_ref[...] = jnp.zeros_like(dg_ref)
            l_ref[...] = jnp.zeros_like(l_ref)

        f = f_ref[...]
        g = g_ref[...]
        e = x_ref[...] + g * f - t_ref[...]
        dx = e * (1.0 / D_MODEL)
        dx_ref[...] = dx
        df_ref[...] = (g * dx).astype(df_ref.dtype)
        dg_ref[...] += jnp.sum(dx * f, axis=0, keepdims=True)
        l_ref[...] += jnp.sum(e * e) * (0.5 / D_MODEL)

    r = _rspec(tm, D_MODEL, 0)
    return pl.pallas_call(
        body, name=name, grid=(S // tm,),
        in_specs=[r, r, r, _vspec((1, D_MODEL))],
        out_specs=[r, r, _vspec((1, D_MODEL)), _vspec((1, LANES))],
        out_shape=[jax.ShapeDtypeStruct((S, D_MODEL), F32), jax.ShapeDtypeStruct((S, D_MODEL), BF16),
                   jax.ShapeDtypeStruct((1, D_MODEL), F32), jax.ShapeDtypeStruct((1, LANES), F32)],
        compiler_params=_cp(("arbitrary",)),
    )(x1, ff, tgt, g2)


def _causal_mask(t):
    r = lax.broadcasted_iota(jnp.int32, (t, t), 0)
    c = lax.broadcasted_iota(jnp.int32, (t, t), 1)
    return c <= r


LOG2E = 1.4426950408889634
EXP2_SCALE = ATT_SCALE * LOG2E
ATT_TQ, ATT_TK = 512, 1024
ATT_BQ, ATT_BK = 1024, 512


def _attn_fwd(qf, kf, va, name):
    S = qf.shape[0]
    T, TK = min(ATT_TQ, S), min(ATT_TK, S)
    nmask = max(1, T // TK)

    def body(q_ref, k_ref, v_ref, o_ref, l_ref):
        i = pl.program_id(1)
        r = lax.broadcasted_iota(jnp.int32, (T, TK), 0)
        c = lax.broadcasted_iota(jnp.int32, (T, TK), 1)
        qs = [q_ref[:, hh * LANES:(hh + 1) * LANES] for hh in range(2)]

        def blk(j, carry, masked):
            off = pl.multiple_of(j * TK, TK)
            out = []
            for hh in range(2):
                m, acc = carry[hh]
                s = _dot(qs[hh], k_ref[pl.ds(off, TK), hh * LANES:(hh + 1) * LANES], ((1,), (1,)))
                if masked:
                    s = jnp.where(c + j * TK <= r + i * T, s, NEG)
                mn = jnp.maximum(m, jnp.max(s, axis=1, keepdims=True))
                p = jnp.exp2((s - mn) * EXP2_SCALE)
                al = jnp.exp2((m - mn) * EXP2_SCALE)
                vj = v_ref[pl.ds(off, TK), hh * LANES:(hh + 1) * LANES]
                out.append((mn, al * acc + _dot(p.astype(BF16), vj, ((1,), (0,)))))
            return tuple(out)

        one = (jnp.full((T, 1), NEG, F32), jnp.zeros((T, LANES), F32))
        nfull = lax.div(i * T, TK)
        carry = lax.fori_loop(0, nfull, lambda j, cr: blk(j, cr, False), (one, one))
        for t in range(nmask):
            carry = blk(nfull + t, carry, True)
        lane = lax.broadcasted_iota(jnp.int32, (1, LANES), 1)
        z = jnp.zeros((T, LANES), F32)
        for hh in range(2):
            m, acc = carry[hh]
            l = acc[:, 64:65]
            o_ref[:, hh * 64:(hh + 1) * 64] = (acc / l)[:, :64]
            z = z + (m * EXP2_SCALE + jnp.log(l) * LOG2E) * (lane == hh).astype(F32)
        l_ref[0] = z.T[0:2, :]

    return pl.pallas_call(
        body, name=name, grid=(HEADS // 2, S // T),
        in_specs=[pl.BlockSpec((T, 256), lambda h, i: (i, h)), pl.BlockSpec((S, 256), lambda h, i: (0, h)),
                  pl.BlockSpec((S, 256), lambda h, i: (0, h))],
        out_specs=[pl.BlockSpec((T, LANES), lambda h, i: (i, h)), pl.BlockSpec((1, 2, T), lambda h, i: (h, 0, i))],
        out_shape=[jax.ShapeDtypeStruct((S, D_SSD), F32), jax.ShapeDtypeStruct((HEADS // 2, 2, S), F32)],
        compiler_params=_cp(("parallel", "parallel")),
    )(qf, kf, va)


def _attn_delta(dmix, ao, name):
    S = ao.shape[0]
    tm = min(512, S)

    def body(d_ref, o_ref, out_ref):
        lane = lax.broadcasted_iota(jnp.int32, (1, LANES), 1)
        lo = (lane < 64).astype(F32)
        for hp in range(HEADS // 2):
            y = d_ref[:, hp * LANES:(hp + 1) * LANES] * o_ref[:, hp * LANES:(hp + 1) * LANES]
            z = (jnp.sum(y * lo, axis=1, keepdims=True) * (lane == 0).astype(F32)
                 + jnp.sum(y * (1.0 - lo), axis=1, keepdims=True) * (lane == 1).astype(F32))
            out_ref[hp] = z.T[0:2, :]

    return pl.pallas_call(
        body, name=name, grid=(S // tm,),
        in_specs=[pl.BlockSpec((tm, D_SSD), lambda i: (i, 0)), pl.BlockSpec((tm, D_SSD), lambda i: (i, 0))],
        out_specs=pl.BlockSpec((HEADS // 2, 2, tm), lambda i: (0, 0, i)),
        out_shape=jax.ShapeDtypeStruct((HEADS // 2, 2, S), F32),
        compiler_params=_cp(("parallel",)),
    )(dmix, ao)


def _attn_bwd(qf, kf, kT, va, do, lse_r, delta_r, name):
    S = qf.shape[0]
    T, TK = min(ATT_BQ, S), min(ATT_BK, S)
    nq = S // T
    nmask = max(1, TK // T)

    def body(q_ref, k_ref, kT_ref, v_ref, do_ref, l_ref, d_ref, dqT_ref, dk_ref, dv_ref):
        j = pl.program_id(1)

        @pl.when(j == 0)
        def _():
            dqT_ref[...] = jnp.zeros_like(dqT_ref)

        r = lax.broadcasted_iota(jnp.int32, (TK, T), 0)
        c = lax.broadcasted_iota(jnp.int32, (TK, T), 1)
        lo = (lax.broadcasted_iota(jnp.int32, (1, LANES), 1) < 64).astype(F32)
        ks = [k_ref[:, hh * LANES:(hh + 1) * LANES] for hh in range(2)]
        vs = [v_ref[:, hh * LANES:(hh + 1) * LANES] for hh in range(2)]
        kTs = [kT_ref[hh * LANES:(hh + 1) * LANES, :] for hh in range(2)]

        def blk(i, carry, masked):
            off = pl.multiple_of(i * T, T)
            dall = do_ref[pl.ds(off, T), :]
            out = []
            for hh in range(2):
                dk, dv = carry[hh]
                q = q_ref[pl.ds(off, T), hh * LANES:(hh + 1) * LANES]
                dop = ((dall if hh == 0 else pltpu.roll(dall, 64, 1)) * lo).astype(BF16)
                lrow = l_ref[0, hh:hh + 1, pl.ds(off, T)]
                drow = d_ref[0, hh:hh + 1, pl.ds(off, T)]
                pT = jnp.exp2(_dot(ks[hh], q, ((1,), (1,))) * EXP2_SCALE - lrow)
                if masked:
                    pT = jnp.where(r + j * TK <= c + i * T, pT, 0.0)
                dpT = _dot(vs[hh], dop, ((1,), (1,)))
                dsT = (pT * (dpT - drow) * ATT_SCALE).astype(BF16)
                dv = dv + _dot(pT.astype(BF16), dop, ((1,), (0,)))
                dk = dk + _dot(dsT, q, ((1,), (0,)))
                dqT_ref[hh * LANES:(hh + 1) * LANES, pl.ds(off, T)] += _dot(kTs[hh], dsT, ((1,), (0,)))
                out.append((dk, dv))
            return tuple(out)

        z = (jnp.zeros((TK, LANES), F32), jnp.zeros((TK, LANES), F32))
        first = lax.div(j * TK, T)
        carry = (z, z)
        for t in range(nmask):
            carry = blk(first + t, carry, True)
        carry = lax.fori_loop(first + nmask, nq, lambda i, cr: blk(i, cr, False), carry)
        for hh in range(2):
            dk_ref[:, hh * LANES:(hh + 1) * LANES] = carry[hh][0]
            dv_ref[:, hh * LANES:(hh + 1) * LANES] = carry[hh][1]

    return pl.pallas_call(
        body, name=name, grid=(HEADS // 2, S // TK),
        in_specs=[pl.BlockSpec((S, 256), lambda h, j: (0, h)), pl.BlockSpec((TK, 256), lambda h, j: (j, h)),
                  pl.BlockSpec((256, TK), lambda h, j: (h, j)), pl.BlockSpec((TK, 256), lambda h, j: (j, h)),
                  pl.BlockSpec((S, LANES), lambda h, j: (0, h)), pl.BlockSpec((1, 2, S), lambda h, j: (h, 0, 0)),
                  pl.BlockSpec((1, 2, S), lambda h, j: (h, 0, 0))],
        out_specs=[pl.BlockSpec((256, S), lambda h, j: (h, 0)), pl.BlockSpec((TK, 256), lambda h, j: (j, h)),
                   pl.BlockSpec((TK, 256), lambda h, j: (j, h))],
        out_shape=[jax.ShapeDtypeStruct((D_MODEL, S), F32), jax.ShapeDtypeStruct((S, D_MODEL), F32),
                   jax.ShapeDtypeStruct((S, D_MODEL), F32)],
        compiler_params=_cp(("parallel", "arbitrary")),
    )(qf, kf, kT, va, do, lse_r, delta_r)


def _shift_down(x, s):
    if s == 0:
        return x
    rows = lax.broadcasted_iota(jnp.int32, x.shape, 0)
    return jnp.where(rows >= s, pltpu.roll(x, s, 0), 0.0)


def _shift_up(x, s):
    if s == 0:
        return x
    n = x.shape[0]
    rows = lax.broadcasted_iota(jnp.int32, x.shape, 0)
    return jnp.where(rows < n - s, pltpu.roll(x, n - s, 0), 0.0)


def _conv_fwd(proj, cvec, name):
    S = proj.shape[0]

    def body(x_ref, c_ref, o_ref):
        x = x_ref[...]
        y = jnp.broadcast_to(c_ref[4:5, :], x.shape)
        for k in range(CONV_TAPS):
            y = y + c_ref[k:k + 1, :] * _shift_down(x, CONV_TAPS - 1 - k)
        o_ref[...] = y * jax.nn.sigmoid(y)

    return pl.pallas_call(
        body, name=name, grid=(D_CONV // LANES,),
        in_specs=[pl.BlockSpec((S, LANES), lambda j: (0, 8 + j)), pl.BlockSpec((8, LANES), lambda j: (0, j))],
        out_specs=pl.BlockSpec((S, LANES), lambda j: (0, j)),
        out_shape=jax.ShapeDtypeStruct((S, D_CONV), F32),
        compiler_params=_cp(("parallel",)),
    )(proj, cvec)


def _conv_bwd(proj, cvec, dact, name):
    S = proj.shape[0]

    def body(x_ref, c_ref, d_ref, dx_ref, dc_ref):
        x = x_ref[...]
        y = jnp.broadcast_to(c_ref[4:5, :], x.shape)
        for k in range(CONV_TAPS):
            y = y + c_ref[k:k + 1, :] * _shift_down(x, CONV_TAPS - 1 - k)
        sg = jax.nn.sigmoid(y)
        dy = d_ref[...] * (sg * (1.0 + y * (1.0 - sg)))
        dx = jnp.zeros_like(x)
        for k in range(CONV_TAPS):
            s = CONV_TAPS - 1 - k
            dx = dx + c_ref[k:k + 1, :] * _shift_up(dy, s)
            dc_ref[k:k + 1, :] = jnp.sum(dy * _shift_down(x, s), axis=0, keepdims=True)
        dx_ref[...] = dx
        dc_ref[4:5, :] = jnp.sum(dy, axis=0, keepdims=True)
        dc_ref[5:8, :] = jnp.zeros((3, LANES), F32)

    return pl.pallas_call(
        body, name=name, grid=(D_CONV // LANES,),
        in_specs=[pl.BlockSpec((S, LANES), lambda j: (0, 8 + j)), pl.BlockSpec((8, LANES), lambda j: (0, j)),
                  pl.BlockSpec((S, LANES), lambda j: (0, j))],
        out_specs=[pl.BlockSpec((S, LANES), lambda j: (0, j)), pl.BlockSpec((8, LANES), lambda j: (0, j))],
        out_shape=[jax.ShapeDtypeStruct((S, D_CONV), F32), jax.ShapeDtypeStruct((8, D_CONV), F32)],
        compiler_params=_cp(("parallel",)),
    )(proj, cvec, dact)


def fn_ssd_chunk(xs, bm, cm, dtr, state, vecs):
    Q = CHUNK
    dt = jax.nn.softplus(dtr + vecs[0:1])
    a = -jnp.exp(vecs[1:2])
    adt = dt * a
    tril = _causal_mask(Q)
    acs = _dot(tril.astype(F32), adt, ((1,), (0,)), lax.Precision.HIGHEST)
    acs_t = acs.T
    alast = acs[Q - 1:Q, :]
    r = lax.broadcasted_iota(jnp.int32, (LANES, D_SSD), 0)
    c = lax.broadcasted_iota(jnp.int32, (LANES, D_SSD), 1)
    spread = (lax.shift_right_logical(c, 6) == r).astype(F32)

    def per_head(v):
        return _dot(v, spread, ((1,), (0,)), lax.Precision.HIGH)

    xdt = xs * per_head(dt)
    ub = (xdt * per_head(jnp.exp(alast - acs))).astype(BF16)
    xdtb = xdt.astype(BF16)
    Bs = [bm[:, g * SSD_N:(g + 1) * SSD_N].astype(BF16) for g in range(2)]
    Cs = [cm[:, g * SSD_N:(g + 1) * SSD_N].astype(BF16) for g in range(2)]
    Gs = [_dot(Cs[g], Bs[g], ((1,), (1,))) for g in range(2)]
    yds, yos, adds = [], [], []
    for h in range(SSD_HEADS):
        g = h // (SSD_HEADS // 2)
        sl = slice(h * SSD_P, (h + 1) * SSD_P)
        L = jnp.exp(jnp.where(tril, acs[:, h:h + 1] - acs_t[h:h + 1, :], -jnp.inf))
        yds.append(_dot((Gs[g] * L).astype(BF16), xdtb[:, sl], ((1,), (0,))))
        yos.append(_dot(Cs[g], state[h].astype(BF16), ((1,), (1,))))
        adds.append(_dot(ub[:, sl], Bs[g], ((0,), (0,))))
    y = jnp.concatenate(yds, axis=1) + jnp.concatenate(yos, axis=1) * per_head(jnp.exp(acs)) + per_head(vecs[2:3]) * xs
    decay = jnp.stack([jnp.broadcast_to(jnp.exp(alast[:, h:h + 1]), (SSD_P, SSD_N)) for h in range(SSD_HEADS)])
    return y, jnp.stack(adds) + state * decay


def _ssd_fwd(xact, proj, svec, name):
    S = xact.shape[0]
    nc = S // CHUNK

    def body(x_ref, dt_ref, v_ref, y_ref, st_ref, state):
        @pl.when(pl.program_id(0) == 0)
        def _():
            state[...] = jnp.zeros_like(state)

        st_ref[0] = state[...]
        x = x_ref[...]
        y, sn = fn_ssd_chunk(x[:, 0:512], x[:, 512:768], x[:, 768:1024], dt_ref[...], state[...], v_ref[...])
        y_ref[...] = y
        state[...] = sn

    return pl.pallas_call(
        body, name=name, grid=(nc,),
        in_specs=[pl.BlockSpec((CHUNK, D_CONV), lambda i: (i, 0)), pl.BlockSpec((CHUNK, LANES), lambda i: (i, 16)),
                  pl.BlockSpec((8, LANES), lambda i: (0, 0))],
        out_specs=[pl.BlockSpec((CHUNK, D_SSD), lambda i: (i, 0)),
                   pl.BlockSpec((1, SSD_HEADS, SSD_P, SSD_N), lambda i: (i, 0, 0, 0))],
        out_shape=[jax.ShapeDtypeStruct((S, D_SSD), F32), jax.ShapeDtypeStruct((nc, SSD_HEADS, SSD_P, SSD_N), F32)],
        scratch_shapes=[pltpu.VMEM((SSD_HEADS, SSD_P, SSD_N), F32)],
        compiler_params=_cp(("arbitrary",)),
    )(xact, proj, svec)


def _ssd_bwd(xact, proj, svec, states, dy, name):
    S = xact.shape[0]
    nc = S // CHUNK

    def body(x_ref, dt_ref, v_ref, st_ref, dy_ref, dx_ref, ddt_ref, dv_ref, dstate):
        @pl.when(pl.program_id(0) == 0)
        def _():
            dstate[...] = jnp.zeros_like(dstate)
            dv_ref[...] = jnp.zeros_like(dv_ref)

        x = x_ref[...]
        _, vjp = jax.vjp(fn_ssd_chunk, x[:, 0:512], x[:, 512:768], x[:, 768:1024], dt_ref[...], st_ref[0], v_ref[...])
        dxs, dbm, dcm, ddt, dst, dvec = vjp((dy_ref[...], dstate[...]))
        dx_ref[:, 0:512] = dxs
        dx_ref[:, 512:768] = dbm
        dx_ref[:, 768:1024] = dcm
        ddt_ref[...] = ddt
        dstate[...] = dst
        dv_ref[...] += dvec

    rev = lambda i: (nc - 1 - i, 0)
    return pl.pallas_call(
        body, name=name, grid=(nc,),
        in_specs=[pl.BlockSpec((CHUNK, D_CONV), rev), pl.BlockSpec((CHUNK, LANES), lambda i: (nc - 1 - i, 16)),
                  pl.BlockSpec((8, LANES), lambda i: (0, 0)),
                  pl.BlockSpec((1, SSD_HEADS, SSD_P, SSD_N), lambda i: (nc - 1 - i, 0, 0, 0)),
                  pl.BlockSpec((CHUNK, D_SSD), rev)],
        out_specs=[pl.BlockSpec((CHUNK, D_CONV), rev), pl.BlockSpec((CHUNK, LANES), rev),
                   pl.BlockSpec((8, LANES), lambda i: (0, 0))],
        out_shape=[jax.ShapeDtypeStruct((S, D_CONV), F32), jax.ShapeDtypeStruct((S, LANES), F32),
                   jax.ShapeDtypeStruct((8, LANES), F32)],
        scratch_shapes=[pltpu.VMEM((SSD_HEADS, SSD_P, SSD_N), F32)],
        compiler_params=_cp(("arbitrary",)),
    )(xact, proj, svec, states, dy)


def _ada_fwd(c_all, w_ada, b_sh, name):
    nb = 1536 // 512

    def body(c_ref, w_ref, b_ref, o_ref):
        ca = jax.nn.silu(c_ref[...]).astype(BF16)
        o_ref[0] = _dot(ca, w_ref[0].astype(BF16), ((1,), (0,))) + b_ref[0]

    return pl.pallas_call(
        body, name=name, grid=(DEPTH, nb),
        in_specs=[pl.BlockSpec((8, D_MODEL), lambda l, j: (0, 0)), pl.BlockSpec((1, D_MODEL, 512), lambda l, j: (l, 0, j)),
                  pl.BlockSpec((1, 1, 512), lambda l, j: (l, 0, j))],
        out_specs=pl.BlockSpec((1, 8, 512), lambda l, j: (l, 0, j)),
        out_shape=jax.ShapeDtypeStruct((DEPTH, 8, 1536), F32),
        compiler_params=_cp(("parallel", "parallel")),
    )(c_all, w_ada, b_sh)


def _ada_bwd(c_all_t, dmod_sh, name):
    nb = 1536 // 512

    def body(c_ref, d_ref, o_ref):
        ca = jax.nn.silu(c_ref[...])
        acc = ca[:, 0:1] * d_ref[0, 0:1, :]
        for b in range(1, 8):
            acc = acc + ca[:, b:b + 1] * d_ref[0, b:b + 1, :]
        o_ref[0] = acc

    return pl.pallas_call(
        body, name=name, grid=(DEPTH, nb),
        in_specs=[pl.BlockSpec((D_MODEL, 8), lambda l, j: (0, 0)), pl.BlockSpec((1, 8, 512), lambda l, j: (l, 0, j))],
        out_specs=pl.BlockSpec((1, D_MODEL, 512), lambda l, j: (l, 0, j)),
        out_shape=jax.ShapeDtypeStruct((DEPTH, D_MODEL, 1536), F32),
        compiler_params=_cp(("parallel", "parallel")),
    )(c_all_t, dmod_sh)


def _rows_tile(rows):
    return next(t for t in (512, 256, 128, 64, 32, 16, 8) if rows % t == 0)


SUM_BLOCKS = 4
ADAM_BLOCKS = 8


def _sum_sibling(gs, ls, ci, name):
    n = len(gs)

    def body(c_ref, *refs):
        for p in range(n):
            refs[2 * n + p][...] = refs[2 * p][...] + refs[2 * p + 1][...]

    in_specs, out_specs, out_shape = [], [], []
    for g in gs:
        _, _, rh, cw = g.shape
        rb = rh // SUM_BLOCKS
        in_specs += [pl.BlockSpec((None, None, rb, cw), lambda s, i, c: (s, c[0], i, 0)),
                     pl.BlockSpec((None, rb, cw), lambda s, i, c: (s, i, 0))]
        out_specs.append(pl.BlockSpec((None, rb, cw), lambda s, i, c: (s, i, 0)))
        out_shape.append(jax.ShapeDtypeStruct((4, rh, cw), F32))
    ops = [a for pair in zip(gs, ls) for a in pair]
    return pl.pallas_call(
        body, name=name,
        grid_spec=pltpu.PrefetchScalarGridSpec(num_scalar_prefetch=1, grid=(4, SUM_BLOCKS), in_specs=in_specs, out_specs=out_specs),
        out_shape=out_shape, compiler_params=_cp(("parallel", "parallel")),
    )(ci.reshape(1).astype(jnp.int32), *ops)


def _sum_chips(cs, lands, chip, name):
    n = len(cs)

    def body(c_ref, *refs):
        for p in range(n):
            a = refs[4 * p:4 * p + 4]
            refs[4 * n + p][...] = ((a[0][...] + a[1][...]) + a[2][...]) + a[3][...]

    in_specs, out_specs, out_shape = [], [], []
    for c in cs:
        _, rh, cw = c.shape
        rb = rh // SUM_BLOCKS
        in_specs.append(pl.BlockSpec((None, rb, cw), lambda i, ch: (ch[0], i, 0)))
        in_specs += [pl.BlockSpec((None, rb, cw), functools.partial(lambda i, ch, k: (k, i, 0), k=k)) for k in range(3)]
        out_specs.append(pl.BlockSpec((rb, cw), lambda i, ch: (i, 0)))
        out_shape.append(jax.ShapeDtypeStruct((rh, cw), F32))
    ops = [a for c, l in zip(cs, lands) for a in (c, l, l, l)]
    return pl.pallas_call(
        body, name=name,
        grid_spec=pltpu.PrefetchScalarGridSpec(num_scalar_prefetch=1, grid=(SUM_BLOCKS,), in_specs=in_specs, out_specs=out_specs),
        out_shape=out_shape, compiler_params=_cp(("parallel",)),
    )(chip.reshape(1).astype(jnp.int32), *ops)


def _adam_update(w, m, v, g):
    c1 = 1.0 / (1.0 - ADAM_B1 ** ADAM_STEP)
    c2 = 1.0 / (1.0 - ADAM_B2 ** ADAM_STEP)
    nm = ADAM_B1 * m + (1.0 - ADAM_B1) * g
    nv = ADAM_B2 * v + (1.0 - ADAM_B2) * (g * g)
    return -ADAM_LR * ((nm * c1) / (jnp.sqrt(nv * c2) + ADAM_EPS) + ADAM_WD * w), nm, nv


def _adam_multi(ws, ms, vs, gs, name):
    n = len(ws)

    def body(*refs):
        for p in range(n):
            d, nm, nv = _adam_update(*[refs[4 * p + k][...] for k in range(4)])
            refs[4 * n + 3 * p][...] = d
            refs[4 * n + 3 * p + 1][...] = nm
            refs[4 * n + 3 * p + 2][...] = nv

    in_specs, out_specs, out_shape = [], [], []
    for w in ws:
        _, r, cw = w.shape
        spec = pl.BlockSpec((None, r // ADAM_BLOCKS, cw), lambda l, i: (l, i, 0))
        in_specs += [spec] * 4
        out_specs += [spec] * 3
        out_shape += [jax.ShapeDtypeStruct(w.shape, F32)] * 3
    ops = [a for q in zip(ws, ms, vs, gs) for a in q]
    res = pl.pallas_call(
        body, name=name, grid=(DEPTH, ADAM_BLOCKS), in_specs=in_specs, out_specs=out_specs, out_shape=out_shape,
        compiler_params=_cp(("parallel", "parallel")),
    )(*ops)
    return res[0::3], res[1::3], res[2::3]


def _adam(w, m, v, parts, name):
    rows, width = w.shape
    bm = min(256, _rows_tile(rows))
    np_ = len(parts)
    c1 = 1.0 / (1.0 - ADAM_B1 ** ADAM_STEP)
    c2 = 1.0 / (1.0 - ADAM_B2 ** ADAM_STEP)

    def body(*refs):
        w_ref, m_ref, v_ref = refs[:3]
        g = refs[3][...]
        for r in refs[4:3 + np_]:
            g = g + r[...]
        g_ref, d_ref, nm_ref, nv_ref = refs[3 + np_:]
        nm = ADAM_B1 * m_ref[...] + (1.0 - ADAM_B1) * g
        nv = ADAM_B2 * v_ref[...] + (1.0 - ADAM_B2) * (g * g)
        g_ref[...] = g
        nm_ref[...] = nm
        nv_ref[...] = nv
        d_ref[...] = -ADAM_LR * ((nm * c1) / (jnp.sqrt(nv * c2) + ADAM_EPS) + ADAM_WD * w_ref[...])

    blk = pl.BlockSpec((bm, width), lambda i: (i, 0))
    return pl.pallas_call(
        body, name=name, grid=(rows // bm,),
        in_specs=[blk, blk, blk] + [pl.BlockSpec((bm, width), functools.partial(lambda i, o: (i + o, 0), o=off // bm))
                                    for (_, off) in parts],
        out_specs=[blk, blk, blk, blk],
        out_shape=[jax.ShapeDtypeStruct((rows, width), F32)] * 4,
        compiler_params=_cp(("parallel",)),
    )(w, m, v, *[p[0] for p in parts])


def _coords():
    return lax.axis_index("x"), lax.axis_index("y"), lax.axis_index("c")


def _other_chips(x, y):
    return [(1 - x, y), (x, 1 - y), (1 - x, 1 - y)]


def _ag8(blk, name):
    m_per, n = blk.shape

    def body(x_ref, out_ref, send_sems, recv_sems, local_sem):
        x, y, c = _coords()
        me, sibling = (x, y, c), (x, y, 1 - c)
        chips = _other_chips(x, y)

        def rows(px, py, pc):
            return out_ref.at[pl.ds((4 * px + 2 * py + pc) * m_per, m_per), :]

        def copy(k, block, to, src=None):
            return pltpu.make_async_remote_copy(
                src_ref=rows(*block) if src is None else src, dst_ref=rows(*block),
                send_sem=send_sems.at[k], recv_sem=recv_sems.at[k], device_id=to, device_id_type=MESH)

        mine = pltpu.make_async_copy(x_ref, rows(*me), local_sem)
        mine.start()
        first = [copy(0, me, sibling, src=x_ref)]
        first += [copy(1 + j, me, (*chip, c), src=x_ref) for j, chip in enumerate(chips)]
        for cp in first:
            cp.start()
        passed = [copy(4 + j, (*chip, c), sibling) for j, chip in enumerate(chips)]
        for j, chip in enumerate(chips):
            copy(1 + j, (*chip, c), me).wait_recv()
            passed[j].start()
        copy(0, sibling, me).wait_recv()
        for j, chip in enumerate(chips):
            copy(4 + j, (*chip, 1 - c), me).wait_recv()
        for cp in first + passed:
            cp.wait_send()
        mine.wait()

    return pl.pallas_call(
        body, name=name,
        out_shape=jax.ShapeDtypeStruct((8 * m_per, n), blk.dtype),
        in_specs=[pl.BlockSpec(memory_space=pltpu.VMEM)], out_specs=pl.BlockSpec(memory_space=pltpu.VMEM),
        scratch_shapes=[pltpu.SemaphoreType.DMA((7,)), pltpu.SemaphoreType.DMA((7,)), pltpu.SemaphoreType.DMA],
    )(blk)


HBM_SPEC = pl.BlockSpec(memory_space=pltpu.HBM)
SEM_SPEC = pl.BlockSpec(memory_space=pltpu.SEMAPHORE)
EFFECT = pltpu.SideEffectType.DATAFLOW_SIDE_EFFECTING


def _remote(src, dst, send_sem, recv_sem, to):
    return pltpu.make_async_remote_copy(src_ref=src, dst_ref=dst, send_sem=send_sem, recv_sem=recv_sem,
                                        device_id=to, device_id_type=MESH)


def _ag_list(shards, name):
    n = len(shards)

    def body(*refs):
        sh, out = refs[:n], refs[n:2 * n]
        send_sems, recv_sems = refs[2 * n:]
        x, y, c = _coords()
        sibling = (x, y, 1 - c)
        chips = _other_chips(x, y)
        first = [_remote(sh[p].at[c], out[p].at[2 * x + y, c], send_sems.at[6 * p + j], recv_sems.at[6 * p + j], (px, py, c))
                 for p in range(n) for j, (px, py) in enumerate(chips)]
        for cp in first:
            cp.start()
        passed = []
        for j, (px, py) in enumerate(chips):
            for p in range(n):
                got = out[p].at[2 * px + py, c]
                _remote(got, got, send_sems.at[6 * p + j], recv_sems.at[6 * p + j], (x, y, c)).wait_recv()
                cp = _remote(got, got, send_sems.at[6 * p + 3 + j], recv_sems.at[6 * p + 3 + j], sibling)
                cp.start()
                passed.append(cp)
        for j, (px, py) in enumerate(chips):
            for p in range(n):
                got = out[p].at[2 * px + py, 1 - c]
                _remote(got, got, send_sems.at[6 * p + 3 + j], recv_sems.at[6 * p + 3 + j], (x, y, c)).wait_recv()
        for cp in first + passed:
            cp.wait_send()

    return pl.pallas_call(
        body, name=name,
        out_shape=[jax.ShapeDtypeStruct((4,) + s.shape, s.dtype) for s in shards],
        in_specs=[pl.BlockSpec(memory_space=pl.ANY)] * n, out_specs=[pl.BlockSpec(memory_space=pl.ANY)] * n,
        scratch_shapes=[pltpu.SemaphoreType.DMA((6 * n,)), pltpu.SemaphoreType.DMA((6 * n,))],
    )(*shards)


def _ag_direct_copies(sh, land, send_sems, recv_sems, starting):
    x, y, c = _coords()
    return [_remote(sh[p], land[p].at[2 * x + y] if starting else land[p].at[2 * px + py],
                    send_sems.at[3 * p + j], recv_sems.at[3 * p + j], (px, py, c))
            for p in range(len(sh)) for j, (px, py) in enumerate(_other_chips(x, y))]


def _ag_direct_start(shards, name):
    n = len(shards)

    def body(*refs):
        for cp in _ag_direct_copies(refs[:n], refs[n:2 * n], refs[2 * n], refs[2 * n + 1], True):
            cp.start()
        token = refs[4 * n + 2]
        token[...] = jnp.zeros_like(token)

    lands = [pltpu.with_memory_space_constraint(lax.empty((4,) + s.shape, s.dtype), pltpu.HBM) for s in shards]
    res = pl.pallas_call(
        body, name=name,
        out_shape=(pltpu.SemaphoreType.DMA((3 * n,)), pltpu.SemaphoreType.DMA((3 * n,)))
        + tuple(pltpu.HBM(s.shape, s.dtype) for s in shards) + tuple(pltpu.HBM(l.shape, l.dtype) for l in lands)
        + (jax.ShapeDtypeStruct((8, LANES), F32),),
        in_specs=(HBM_SPEC,) * (2 * n), out_specs=(SEM_SPEC, SEM_SPEC) + (HBM_SPEC,) * (2 * n) + (pl.BlockSpec(memory_space=pltpu.VMEM),),
        input_output_aliases={i: 2 + i for i in range(2 * n)},
        compiler_params=pltpu.CompilerParams(has_side_effects=EFFECT),
    )(*[pltpu.with_memory_space_constraint(s, pltpu.HBM) for s in shards], *lands)
    return res[0], res[1], res[2:2 + n], res[2 + n:2 + 2 * n], res[2 + 2 * n]


def _ag_direct_wait(send_sems, recv_sems, sh_thru, land_thru, after, name):
    n = len(sh_thru)

    def body(*refs):
        sh, land = refs[:n], refs[n:2 * n]
        for cp in _ag_direct_copies(sh, land, refs[2 * n], refs[2 * n + 1], False):
            cp.wait_send()
            cp.wait_recv()

    res = pl.pallas_call(
        body, name=name,
        out_shape=tuple(pltpu.HBM(s.shape, s.dtype) for s in sh_thru) + tuple(pltpu.HBM(l.shape, l.dtype) for l in land_thru),
        in_specs=(HBM_SPEC,) * (2 * n) + (SEM_SPEC, SEM_SPEC, pl.BlockSpec(memory_space=pl.ANY)),
        out_specs=(HBM_SPEC,) * (2 * n), input_output_aliases={i: i for i in range(2 * n)},
        compiler_params=pltpu.CompilerParams(has_side_effects=EFFECT),
    )(*sh_thru, *land_thru, send_sems, recv_sems, after)
    return res[n:]


def _rs_sibling_list(gs, name):
    n = len(gs)

    def body(*refs):
        g, out, send_sems, recv_sems = refs[:n], refs[n:2 * n], refs[2 * n], refs[2 * n + 1]
        x, y, c = _coords()
        cps = [_remote(g[p].at[s, 1 - c], out[p].at[s], send_sems.at[4 * p + s], recv_sems.at[4 * p + s], (x, y, 1 - c))
               for p in range(n) for s in range(4)]
        for cp in cps:
            cp.start()
        for cp in cps:
            cp.wait_recv()
        for cp in cps:
            cp.wait_send()

    return pl.pallas_call(
        body, name=name,
        out_shape=[jax.ShapeDtypeStruct((4,) + g.shape[2:], g.dtype) for g in gs],
        in_specs=[pl.BlockSpec(memory_space=pl.ANY)] * n, out_specs=[pl.BlockSpec(memory_space=pl.ANY)] * n,
        scratch_shapes=[pltpu.SemaphoreType.DMA((4 * n,)), pltpu.SemaphoreType.DMA((4 * n,))],
    )(*gs)


def _rs_chips_copies(cs, land, send_sems, recv_sems):
    x, y, c = _coords()
    return [_remote(cs[p].at[2 * px + py], land[p].at[j], send_sems.at[3 * p + j], recv_sems.at[3 * p + j], (px, py, c))
            for p in range(len(cs)) for j, (px, py) in enumerate(_other_chips(x, y))]


def _rs_chips_start(cs, name):
    n = len(cs)

    def body(*refs):
        for cp in _rs_chips_copies(refs[:n], refs[n:2 * n], refs[2 * n], refs[2 * n + 1]):
            cp.start()
        token = refs[4 * n + 2]
        token[...] = jnp.zeros_like(token)

    lands = [pltpu.with_memory_space_constraint(lax.empty((3,) + c.shape[1:], c.dtype), pltpu.HBM) for c in cs]
    res = pl.pallas_call(
        body, name=name,
        out_shape=(pltpu.SemaphoreType.DMA((3 * n,)), pltpu.SemaphoreType.DMA((3 * n,)))
        + tuple(pltpu.HBM(c.shape, c.dtype) for c in cs) + tuple(pltpu.HBM(l.shape, l.dtype) for l in lands)
        + (jax.ShapeDtypeStruct((8, LANES), F32),),
        in_specs=(HBM_SPEC,) * (2 * n), out_specs=(SEM_SPEC, SEM_SPEC) + (HBM_SPEC,) * (2 * n) + (pl.BlockSpec(memory_space=pltpu.VMEM),),
        input_output_aliases={i: 2 + i for i in range(2 * n)},
        compiler_params=pltpu.CompilerParams(has_side_effects=EFFECT),
    )(*[pltpu.with_memory_space_constraint(c, pltpu.HBM) for c in cs], *lands)
    return res[0], res[1], res[2:2 + n], res[2 + n:2 + 2 * n], res[2 + 2 * n]


def _rs_chips_wait(send_sems, recv_sems, cs_thru, land_thru, after, name):
    n = len(cs_thru)

    def body(*refs):
        for cp in _rs_chips_copies(refs[:n], refs[n:2 * n], refs[2 * n], refs[2 * n + 1]):
            cp.wait_send()
            cp.wait_recv()

    res = pl.pallas_call(
        body, name=name,
        out_shape=tuple(pltpu.HBM(c.shape, c.dtype) for c in cs_thru) + tuple(pltpu.HBM(l.shape, l.dtype) for l in land_thru),
        in_specs=(HBM_SPEC,) * (2 * n) + (SEM_SPEC, SEM_SPEC, pl.BlockSpec(memory_space=pl.ANY)),
        out_specs=(HBM_SPEC,) * (2 * n), input_output_aliases={i: i for i in range(2 * n)},
        compiler_params=pltpu.CompilerParams(has_side_effects=EFFECT),
    )(*cs_thru, *land_thru, send_sems, recv_sems, after)
    return res[:n], res[n:]


def _swap_list(ghs, name):
    n = len(ghs)

    def body(*refs):
        g, out, send_sems, recv_sems = refs[:n], refs[n:2 * n], refs[2 * n], refs[2 * n + 1]
        x, y, c = _coords()
        cps = [_remote(g[p], out[p], send_sems.at[p], recv_sems.at[p], (x, y, 1 - c)) for p in range(n)]
        for cp in cps:
            cp.start()
        for cp in cps:
            cp.wait_recv()
        for cp in cps:
            cp.wait_send()

    return pl.pallas_call(
        body, name=name,
        out_shape=[jax.ShapeDtypeStruct(g.shape, g.dtype) for g in ghs],
        in_specs=[pl.BlockSpec(memory_space=pl.ANY)] * n, out_specs=[pl.BlockSpec(memory_space=pl.ANY)] * n,
        scratch_shapes=[pltpu.SemaphoreType.DMA((n,)), pltpu.SemaphoreType.DMA((n,))],
    )(*ghs)


def _pad_win(w):
    return jnp.concatenate([w[:, :416], jnp.zeros((w.shape[0], 96), w.dtype), w[:, 416:1952],
                            w[:, 1952:1960], jnp.zeros((w.shape[0], 120), w.dtype)], axis=1)


def _unpad_win(g):
    return jnp.concatenate([g[:, :416], g[:, 512:2048], g[:, 2048:2056]], axis=1)


def _pad_wq(w):
    return jnp.pad(w.reshape(Q_LORA, HEADS, QK_DIM), ((0, 0), (0, 0), (0, LANES - QK_DIM))).reshape(Q_LORA, HEADS * LANES)


def _unpad_wq(g):
    return g.reshape(Q_LORA, HEADS, LANES)[:, :, :QK_DIM].reshape(Q_LORA, HEADS * QK_DIM)


def _cols_to_shards(a):
    r, c4 = a.shape
    return a.reshape(r, 4, c4 // 4).transpose(1, 0, 2)


def _shards_to_cols(a):
    _, r, c = a.shape
    return a.transpose(1, 0, 2).reshape(r, 4 * c)


def _pack_small(tree):
    parts = []
    for l in range(DEPTH):
        for (n, k) in SMALL:
            parts.append(jnp.pad(tree[n][l].reshape(-1), (0, -k % LANES)))
    flat = jnp.concatenate(parts)
    return jnp.pad(flat, (0, SMALL_ROWS * LANES - flat.shape[0])).reshape(SMALL_ROWS, LANES)


def _unpack_small(buf):
    flat = buf.reshape(-1)
    out = {n: [] for (n, _) in SMALL}
    o = 0
    for l in range(DEPTH):
        for (n, k) in SMALL:
            out[n].append(flat[o:o + k])
            o += k + (-k % LANES)
    return {n: jnp.stack(v) for n, v in out.items()}


def _vec(v, width=LANES):
    return jnp.pad(v.reshape(1, -1), ((0, 0), (0, width - v.shape[-1])))


def kernel(x, c, positions, norm1_w, norm2_w, w_ada, b_ada, w_in, q_a_norm_w, w_q_up, kv_a_norm_w, w_kv_up, q_nope_norm_w, q_pe_norm_w, k_nope_norm_w, k_pe_norm_w, conv_w, conv_b, dt_bias, a_log, d_skip, ssd_norm_w, w_out, w_gate_up, w_down, loss_target, m_norm1_w, m_norm2_w, m_w_ada, m_b_ada, m_w_in, m_q_a_norm_w, m_w_q_up, m_kv_a_norm_w, m_w_kv_up, m_q_nope_norm_w, m_q_pe_norm_w, m_k_nope_norm_w, m_k_pe_norm_w, m_conv_w, m_conv_b, m_dt_bias, m_a_log, m_d_skip, m_ssd_norm_w, m_w_out, m_w_gate_up, m_w_down, v_norm1_w, v_norm2_w, v_w_ada, v_b_ada, v_w_in, v_q_a_norm_w, v_w_q_up, v_kv_a_norm_w, v_w_kv_up, v_q_nope_norm_w, v_q_pe_norm_w, v_k_nope_norm_w, v_k_pe_norm_w, v_conv_w, v_conv_b, v_dt_bias, v_a_log, v_d_skip, v_ssd_norm_w, v_w_out, v_w_gate_up, v_w_down):
    W = dict(zip(WEIGHTS, (norm1_w, norm2_w, w_ada, b_ada, w_in, q_a_norm_w, w_q_up, kv_a_norm_w, w_kv_up, q_nope_norm_w, q_pe_norm_w, k_nope_norm_w, k_pe_norm_w, conv_w, conv_b, dt_bias, a_log, d_skip, ssd_norm_w, w_out, w_gate_up, w_down)))
    M = dict(zip(WEIGHTS, (m_norm1_w, m_norm2_w, m_w_ada, m_b_ada, m_w_in, m_q_a_norm_w, m_w_q_up, m_kv_a_norm_w, m_w_kv_up, m_q_nope_norm_w, m_q_pe_norm_w, m_k_nope_norm_w, m_k_pe_norm_w, m_conv_w, m_conv_b, m_dt_bias, m_a_log, m_d_skip, m_ssd_norm_w, m_w_out, m_w_gate_up, m_w_down)))
    V = dict(zip(WEIGHTS, (v_norm1_w, v_norm2_w, v_w_ada, v_b_ada, v_w_in, v_q_a_norm_w, v_w_q_up, v_kv_a_norm_w, v_w_kv_up, v_q_nope_norm_w, v_q_pe_norm_w, v_k_nope_norm_w, v_k_pe_norm_w, v_conv_w, v_conv_b, v_dt_bias, v_a_log, v_d_skip, v_ssd_norm_w, v_w_out, v_w_gate_up, v_w_down)))
    S = x.shape[1]
    xi, yi, ci = _coords()
    chip = 2 * xi + yi
    dev = 2 * chip + ci
    x0 = x[0]
    tgt = loss_target[0]

    inv_freq = 1.0 / (ROPE_THETA ** (jnp.arange(0, ROPE, 2, dtype=F32) / ROPE))
    ang = positions[0].astype(F32)[:, None] * inv_freq
    cos, sin = jnp.cos(ang), jnp.sin(ang)
    z16, z32, z64 = jnp.zeros((S, 16), F32), jnp.zeros((S, 32), F32), jnp.zeros((S, 64), F32)
    tab_c = jnp.concatenate([jnp.ones((S, 64), F32), cos, cos, z32], axis=1)
    tab_s1 = jnp.concatenate([z64, z16, sin, z32], axis=1)
    tab_s2 = jnp.concatenate([z64, -sin, z16, z32], axis=1)

    blk0 = jnp.concatenate([c.reshape(-1), W['conv_w'].reshape(-1)]).reshape(24, LANES)
    g0 = _ag8(blk0, "ag_c_conv").reshape(8, 24 * LANES)
    c_all = g0[:, :D_MODEL]
    conv_full = g0[0::2, D_MODEL:].reshape(4, DEPTH, CONV_TAPS, 256).transpose(1, 2, 0, 3).reshape(DEPTH, CONV_TAPS, D_CONV)

    sh = [{n: W[n][l].astype(BF16) for n in BIG} for l in range(DEPTH)]
    got_first = _ag_list([sh[0][n].reshape(2, sh[0][n].shape[0] // 2, sh[0][n].shape[1]) for n in FIRST], "ag_w0_first")
    to_operand = dict(w_in=lambda a: _pad_win(_shards_to_cols(a)), w_q_up=lambda a: _pad_wq(_shards_to_cols(a)),
                      w_kv_up=_shards_to_cols, w_out=lambda a: a.reshape(D_MODEL, D_MODEL), w_gate_up=lambda a: a,
                      w_down=lambda a: a.reshape(D_FF, D_MODEL))

    def layer_weights(names, gathered, own):
        return {n: to_operand[n](lax.dynamic_update_slice_in_dim(a.reshape(4, -1, a.shape[-1]), own[n][None], chip, axis=0))
                for n, a in zip(names, gathered)}

    LW = [layer_weights(FIRST, got_first, sh[0]), None]

    b_sh = lax.dynamic_slice_in_dim(W['b_ada'], chip * 1536, 1536, axis=1).reshape(DEPTH, 1, 1536)
    mod_sh = _ada_fwd(c_all, W['w_ada'], b_sh, "ada_fwd")
    g1 = _ag8(mod_sh.reshape(192, LANES), "ag_mod").reshape(8, DEPTH, 8, 1536)
    mod_all = g1[0::2].transpose(1, 2, 0, 3).reshape(DEPTH, 8, 6 * D_MODEL)
    mod = lax.dynamic_index_in_dim(mod_all, dev, axis=1, keepdims=False)
    mod, rest0 = lax.optimization_barrier((mod, [sh[0][n] for n in REST]))
    ag0 = _ag_direct_start(rest0, "ag_w0_rest_start")

    def mvec(l, k):
        return mod[l, k * D_MODEL:(k + 1) * D_MODEL].reshape(1, D_MODEL)

    def small(name, l, width=None):
        v = W[name][l]
        return _vec(v, width or v.shape[-1])

    def wq_vec(l):
        return _vec(jnp.concatenate([W['q_nope_norm_w'][l], W['q_pe_norm_w'][l]]))

    def wk_vec(l):
        return _vec(jnp.concatenate([W['k_nope_norm_w'][l], W['k_pe_norm_w'][l]]))

    def conv_vec(l):
        return jnp.concatenate([conv_full[l], W['conv_b'][l].reshape(1, D_CONV), jnp.zeros((3, D_CONV), F32)], axis=0)

    def ssd_vec(l):
        return jnp.concatenate([_vec(W['dt_bias'][l]), _vec(W['a_log'][l]), _vec(W['d_skip'][l]), jnp.zeros((5, LANES), F32)], axis=0)

    sv = []
    xcur = x0
    h1 = _row_fwd(fn_norm_mod, "norm_mod_f", [(x0, 0, D_MODEL)], [small('norm1_w', 0) + ag0[4][0, 0], mvec(0, 1), mvec(0, 0)],
                  [(D_MODEL, BF16)])[0]
    fin = None
    ag1 = None
    for l in range(DEPTH):
        if l == 1:
            LW[1] = layer_weights(BIG, _ag_direct_wait(ag1[0], ag1[1], ag1[2], ag1[3], xcur, "ag_w1_wait"), sh[1])
        lw = LW[l]
        t = dict(xcur=xcur, h1=h1)
        t['proj'] = proj = _mm(h1, lw['w_in'], 'nn', f"mm_in_{l}")
        t['qa_n'], t['kva_n'] = _row_fwd(fn_lat_norm, f"lat_norm_f{l}", [(proj, 0, 256), (proj, 2, 128)],
                                         [small('q_a_norm_w', l), small('kv_a_norm_w', l)], [(256, BF16), (128, BF16)])
        t['q'] = _mm(t['qa_n'], lw['w_q_up'], 'nn', f"mm_q_{l}")
        t['kv'] = _mm(t['kva_n'], lw['w_kv_up'], 'nn', f"mm_kv_{l}")
        t['qf'], t['kf'], t['vv'] = _row_fwd(
            fn_qk_prep, f"qk_prep_f{l}",
            [(t['q'], 0, 1024), (t['kv'], 0, 1024), (proj, 3, 128), (tab_c, 0, 128), (tab_s1, 0, 128), (tab_s2, 0, 128)],
            [wq_vec(l), wk_vec(l)], [(1024, BF16), (1024, BF16), (1024, BF16)])
        t['ao'], t['lse'] = _attn_fwd(t['qf'], t['kf'], t['vv'], f"attn_f{l}")
        t['xact'] = _conv_fwd(proj, conv_vec(l), f"conv_f{l}")
        t['y'], t['states'] = _ssd_fwd(t['xact'], proj, ssd_vec(l), f"ssd_f{l}")
        tie = 0.0
        if l == 0:
            rest = _ag_direct_wait(ag0[0], ag0[1], ag0[2], ag0[3], t['y'], "ag_w0_rest_wait")
            rest, sh1 = lax.optimization_barrier((list(rest), [sh[1][n] for n in BIG]))
            lw.update(layer_weights(REST, rest, sh[0]))
            ag1 = _ag_direct_start(sh1, "ag_w1_start")
            tie = ag1[4][0, 0]
        t['mix'] = _row_fwd(fn_gated_mix, f"gated_f{l}", [(t['y'], 0, 512), (proj, 1, 512), (t['ao'], 0, 512)],
                            [small('ssd_norm_w', l) + tie], [(1024, BF16)])[0]
        t['mo'] = _mm(t['mix'], lw['w_out'], 'nn', f"mm_out_{l}")
        t['x1'], t['h2'] = _row_fwd(fn_resid_norm, f"resid_mid_f{l}", [(xcur, 0, D_MODEL), (t['mo'], 0, D_MODEL)],
                                    [mvec(l, 2), small('norm2_w', l), mvec(l, 4), mvec(l, 3)],
                                    [(D_MODEL, F32), (D_MODEL, BF16)])
        t['gu'] = _mm(t['h2'], lw['w_gate_up'], 'nn', f"mm_gu_{l}", stack='b')
        t['act'] = _row_fwd(fn_swiglu, f"swiglu_f{l}", [(t['gu'], 0, 2 * D_FF)], [], [(D_FF, BF16)], tm=128)[0]
        t['ff'] = _mm(t['act'], lw['w_down'], 'nn', f"mm_down_{l}")
        if l + 1 < DEPTH:
            xcur, h1 = _row_fwd(fn_resid_norm, f"resid_end_f{l}", [(t['x1'], 0, D_MODEL), (t['ff'], 0, D_MODEL)],
                                [mvec(l, 5), small('norm1_w', l + 1), mvec(l + 1, 1), mvec(l + 1, 0)],
                                [(D_MODEL, F32), (D_MODEL, BF16)])
        else:
            fin = _final(t['x1'], t['ff'], tgt, mvec(l, 5), "final_loss")
        sv.append(t)

    dx1, dff, dg2_last, loss_acc = fin
    gfull = {n: [None] * DEPTH for n in BIG}
    gsm = {n: [None] * DEPTH for (n, _) in SMALL}
    dmod = [[None] * 6 for _ in range(DEPTH)]
    dmod[DEPTH - 1][5] = dg2_last
    grad_x = None
    pending = []

    def rs_begin(l, names, tag):
        g4 = [gfull[n][l].reshape(4, 2, gfull[n][l].shape[1] // 2, gfull[n][l].shape[2]) for n in names]
        sib = _rs_sibling_list(g4, f"rs_sibling_{tag}")
        cs = _sum_sibling(g4, sib, ci, f"sum_sibling_{tag}")
        h = _rs_chips_start(cs, f"rs_chips_start_{tag}")
        pending.append((l, names, h))
        return h[4][0, 0]

    tie_l1 = tie_l0a = 0.0

    for l in reversed(range(DEPTH)):
        t = sv[l]
        lw = LW[l]
        proj = t['proj']
        dact = _mm(dff, lw['w_down'], 'nt', f"mm_down_dx{l}")
        gfull['w_down'][l] = _mm(t['act'], dff, 'tn', f"mm_down_dw{l}").reshape(4, D_FF // 4, D_MODEL)
        dgu = _row_bwd(fn_swiglu, f"swiglu_b{l}", [(t['gu'], 0, 2 * D_FF)], [], [(dact, 0, D_FF)], [0], [], tm=128, ddtypes=[BF16])[0]
        dh2 = _mm(dgu, lw['w_gate_up'], 'nt', f"mm_gu_dx{l}", stack='b')
        gfull['w_gate_up'][l] = _mm(t['h2'], dgu, 'tn', f"mm_gu_dw{l}", stack='out')
        if l == 0:
            tie_l0a = rs_begin(0, EARLY, "l0a")
        dxc, dmo, dmod[l][2], gsm['norm2_w'][l], dmod[l][4], dmod[l][3] = _row_bwd(
            fn_resid_norm, f"resid_mid_b{l}", [(t['xcur'], 0, D_MODEL), (t['mo'], 0, D_MODEL)],
            [mvec(l, 2) + (tie_l1 if l == 0 else 0.0), small('norm2_w', l), mvec(l, 4), mvec(l, 3)],
            [(dx1, 0, D_MODEL), (dh2, 0, D_MODEL)], [0, 1], [0, 1, 2, 3], ddtypes=[F32, BF16])
        dmix = _mm(dmo, lw['w_out'], 'nt', f"mm_out_dx{l}")
        gfull['w_out'][l] = _mm(t['mix'], dmo, 'tn', f"mm_out_dw{l}").reshape(4, D_MODEL // 4, D_MODEL)
        dy, dz, gsm['ssd_norm_w'][l] = _row_bwd(fn_gated_norm, f"gated_b{l}", [(t['y'], 0, 512), (proj, 1, 512)],
                                                [small('ssd_norm_w', l)], [(dmix, 1, 512)], [0, 1], [0])
        dxact, ddt, dsv = _ssd_bwd(t['xact'], proj, ssd_vec(l) + (tie_l0a if l == 0 else 0.0), t['states'], dy, f"ssd_b{l}")
        gsm['dt_bias'][l], gsm['a_log'][l], gsm['d_skip'][l] = dsv[0, :8], dsv[1, :8], dsv[2, :8]
        dxbc, dcv = _conv_bwd(proj, conv_vec(l), dxact, f"conv_b{l}")
        gsm['conv_w'][l] = dcv[:CONV_TAPS]
        gsm['conv_b'][l] = dcv[CONV_TAPS]
        delta_r = _attn_delta(dmix, t['ao'], f"attn_delta{l}")
        dqT, dkf, dvv = _attn_bwd(t['qf'], t['kf'], t['kf'].T, t['vv'], dmix, t['lse'], delta_r, f"attn_b{l}")
        dqf = dqT.T
        dq, dkv, dkpe, dwq, dwk = _row_bwd(
            fn_qk_prep, f"qk_prep_b{l}",
            [(t['q'], 0, 1024), (t['kv'], 0, 1024), (proj, 3, 128), (tab_c, 0, 128), (tab_s1, 0, 128), (tab_s2, 0, 128)],
            [wq_vec(l), wk_vec(l)], [(dqf, 0, 1024), (dkf, 0, 1024), (dvv, 0, 1024)], [0, 1, 2], [0, 1],
            ddtypes=[BF16, BF16, F32])
        gsm['q_nope_norm_w'][l], gsm['q_pe_norm_w'][l] = dwq[0, :NOPE], dwq[0, NOPE:QK_DIM]
        gsm['k_nope_norm_w'][l], gsm['k_pe_norm_w'][l] = dwk[0, :NOPE], dwk[0, NOPE:QK_DIM]
        dqa_n = _mm(dq, lw['w_q_up'], 'nt', f"mm_q_dx{l}")
        gfull['w_q_up'][l] = _cols_to_shards(_unpad_wq(_mm(t['qa_n'], dq, 'tn', f"mm_q_dw{l}")))
        dkva_n = _mm(dkv, lw['w_kv_up'], 'nt', f"mm_kv_dx{l}")
        gfull['w_kv_up'][l] = _cols_to_shards(_mm(t['kva_n'], dkv, 'tn', f"mm_kv_dw{l}"))
        dqa, dkva, dqw, dkvw = _row_bwd(fn_lat_norm, f"lat_norm_b{l}", [(proj, 0, 256), (proj, 2, 128)],
                                        [small('q_a_norm_w', l), small('kv_a_norm_w', l)],
                                        [(dqa_n, 0, 256), (dkva_n, 0, 128)], [0, 1], [0, 1])
        gsm['q_a_norm_w'][l], gsm['kv_a_norm_w'][l] = dqw[0], dkvw[0]
        dproj = jnp.concatenate([dqa, dkva, dkpe, dz, dxbc, ddt], axis=1).astype(BF16)
        dh1 = _mm(dproj, lw['w_in'], 'nt', f"mm_in_dx{l}")
        gfull['w_in'][l] = _cols_to_shards(_unpad_win(_mm(t['h1'], dproj, 'tn', f"mm_in_dw{l}")))
        if l > 0:
            p = sv[l - 1]
            dx1, dff, dmod[l - 1][5], gsm['norm1_w'][l], dmod[l][1], dmod[l][0] = _row_bwd(
                fn_resid_norm, f"resid_end_b{l - 1}", [(p['x1'], 0, D_MODEL), (p['ff'], 0, D_MODEL)],
                [mvec(l - 1, 5), small('norm1_w', l), mvec(l, 1), mvec(l, 0)], [(dxc, 0, D_MODEL), (dh1, 0, D_MODEL)],
                [0, 1], [0, 1, 2, 3], ddtypes=[F32, BF16])
            tie_l1 = rs_begin(l, BIG, f"l{l}")
        else:
            grad_x, gsm['norm1_w'][l], dmod[l][1], dmod[l][0] = _row_bwd(
                fn_norm_mod_pass, "norm_mod_b", [(x0, 0, D_MODEL)], [small('norm1_w', 0), mvec(0, 1), mvec(0, 0)],
                [(dxc, 0, D_MODEL), (dh1, 0, D_MODEL)], [0], [0, 1, 2])
        for n in ('norm1_w', 'norm2_w', 'ssd_norm_w'):
            gsm[n][l] = gsm[n][l][0]

    for l in range(DEPTH):
        gsm['b_ada'][l] = jnp.concatenate([d[0] for d in dmod[l]])
    sm_part = _pack_small({n: jnp.stack(v) for n, v in gsm.items()}).at[SMALL_ROWS - 1, 0].set(loss_acc[0, 0])
    sm_all = _ag8(sm_part, "ag_small")
    loss = jnp.sum(sm_all.reshape(8, SMALL_ROWS, LANES)[:, SMALL_ROWS - 1, 0])
    sm_all, late = lax.optimization_barrier((sm_all, [gfull[n][0] for n in BIG[2:]]))
    for n, g in zip(BIG[2:], late):
        gfull[n][0] = g
    tie_l0b = rs_begin(0, BIG[2:], "l0b")

    def with_conv(tree):
        wide = lax.dynamic_update_slice_in_dim(jnp.zeros((DEPTH, CONV_TAPS, D_CONV), F32), tree['conv_w'], chip * 256, axis=2)
        return {**tree, 'conv_w': wide}

    g_sm, d_sm, m_sm, v_sm = _adam(_pack_small(with_conv(W)) + tie_l0b, _pack_small(with_conv(M)), _pack_small(with_conv(V)),
                                   [(sm_all, d * SMALL_ROWS) for d in range(8)], "adam_small")
    out_small = [_unpack_small(b) for b in (g_sm, d_sm, m_sm, v_sm)]
    for o in out_small:
        o['conv_w'] = lax.dynamic_slice_in_dim(o['conv_w'].reshape(DEPTH, CONV_TAPS, D_CONV), chip * 256, 256, axis=2)

    dmod_all = sm_all.reshape(8, SMALL_ROWS * LANES)
    per_layer = sum(k + (-k % LANES) for (_, k) in SMALL)
    dmod_sh = jnp.stack([lax.dynamic_slice_in_dim(dmod_all[:, l * per_layer:l * per_layer + 6 * D_MODEL], chip * 1536, 1536, axis=1)
                         for l in range(DEPTH)])
    g_ada = _ada_bwd(c_all.T, dmod_sh, "ada_bwd")
    ada = _adam(W['w_ada'].reshape(DEPTH * D_MODEL, 1536), M['w_ada'].reshape(DEPTH * D_MODEL, 1536),
                V['w_ada'].reshape(DEPTH * D_MODEL, 1536), [(g_ada.reshape(DEPTH * D_MODEL, 1536), 0)], "adam_ada")
    out_ada = [a.reshape(DEPTH, D_MODEL, 1536) for a in ada]

    keys, cs_all, land_all = [], [], []
    for (l, names, (send_sems, recv_sems, cs_thru, land_thru, _)) in pending:
        cs, lands = _rs_chips_wait(send_sems, recv_sems, cs_thru, land_thru, ada[3], f"rs_chips_wait_l{l}{len(names)}")
        keys += [(l, n) for n in names]
        cs_all += list(cs)
        land_all += list(lands)
    ghalf = _sum_chips(cs_all, land_all, chip, "sum_chips")
    gother = _swap_list(ghalf, "swap_halves")
    gshard = {k: jnp.where(ci == 0, jnp.concatenate([a, b]), jnp.concatenate([b, a])) for k, a, b in zip(keys, ghalf, gother)}
    g_big = [jnp.stack([gshard[(l, n)] for l in range(DEPTH)]) for n in BIG]
    d_big, m_big, v_big = _adam_multi([W[n] for n in BIG], [M[n] for n in BIG], [V[n] for n in BIG], g_big, "adam_big")
    out_big = [dict(zip(BIG, o)) for o in (g_big, d_big, m_big, v_big)]

    outs = [loss, grad_x[None]]
    for k in range(4):
        for n in WEIGHTS:
            if n == 'w_ada':
                outs.append(out_ada[k])
            elif n in BIG:
                outs.append(out_big[k][n])
            else:
                outs.append(out_small[k][n])
    return tuple(outs)
```

```python
import functools

import jax
import jax.numpy as jnp
from jax import lax
from jax.experimental import pallas as pl
from jax.experimental.pallas import tpu as pltpu

F32 = jnp.float32
BF16 = jnp.bfloat16
MESH = pl.DeviceIdType.MESH

D_MODEL = 1024
DEPTH = 2
HEADS = 8
NOPE = 64
ROPE = 32
QK_DIM = NOPE + ROPE
Q_LORA = 256
KV_LORA = 128
SSD_HEADS = 8
SSD_P = 64
SSD_N = 128
CHUNK = 128
CONV_TAPS = 4
D_SSD = 512
D_CONV = 1024
D_FF = 2816
D_IN = 1960
D_IN_PAD = 2176
EPS = 1e-6
ROPE_THETA = 10000.0
ATT_SCALE = QK_DIM ** -0.5
NEG = -1e30
LANES = 128
VMEM_LIMIT = 48 * 1024 * 1024

ADAM_LR, ADAM_B1, ADAM_B2, ADAM_EPS, ADAM_WD, ADAM_STEP = 0.001, 0.9, 0.999, 1e-08, 0.01, 10

WEIGHTS = ['norm1_w', 'norm2_w', 'w_ada', 'b_ada', 'w_in', 'q_a_norm_w', 'w_q_up', 'kv_a_norm_w', 'w_kv_up',
           'q_nope_norm_w', 'q_pe_norm_w', 'k_nope_norm_w', 'k_pe_norm_w', 'conv_w', 'conv_b', 'dt_bias', 'a_log',
           'd_skip', 'ssd_norm_w', 'w_out', 'w_gate_up', 'w_down']
BIG = ['w_down', 'w_gate_up', 'w_out', 'w_kv_up', 'w_q_up', 'w_in']
EARLY = BIG[:2]
FIRST = BIG[3:]
REST = BIG[:3]
SMALL = [('b_ada', 6144), ('conv_w', 4096), ('norm1_w', 1024), ('norm2_w', 1024), ('conv_b', 1024), ('ssd_norm_w', 512),
         ('q_a_norm_w', 256), ('kv_a_norm_w', 128), ('q_nope_norm_w', 64), ('q_pe_norm_w', 32),
         ('k_nope_norm_w', 64), ('k_pe_norm_w', 32), ('dt_bias', 8), ('a_log', 8), ('d_skip', 8)]
SMALL_ROWS = 240


def _cp(sem=None, **kw):
    return pltpu.CompilerParams(dimension_semantics=sem, vmem_limit_bytes=VMEM_LIMIT, **kw)


def _dot(a, b, dims, prec=None):
    return lax.dot_general(a, b, (dims, ((), ())), preferred_element_type=F32, precision=prec)


def _tile(dim, target):
    best = 0
    for t in range(LANES, min(dim, target) + 1, LANES):
        if dim % t == 0:
            best = t
    if best < 256 and dim <= 2304:
        return dim
    return best


def _mm(a, b, mode, name, out_dtype=F32, stack=None):
    ns = None
    halves = (a if mode == 'nt' else b).ndim == 3 and stack is not None and not (stack == 'b' and mode == 'nn')
    if stack == 'b':
        ns = b.shape[2]
        if mode == 'nn':
            (M, K), N = a.shape, 4 * ns
        else:
            M, K, N = a.shape[-2], 4 * ns, b.shape[1]
    elif mode == 'nn':
        (M, K), (_, N) = a.shape, b.shape
    elif mode == 'nt':
        (M, K), (N, _) = a.shape, b.shape
    else:
        (K, M), N = a.shape, (2 * b.shape[2] if halves else b.shape[1])
    if stack == 'out':
        ns = N // 4
    tm, tn, tk = _tile(M, 1408 if mode == 'tn' else 1024), _tile(N, 1408), _tile(K, 1408)
    if stack == 'b' and mode == 'nt':
        tk = ns
    elif stack is not None:
        tn = ns
    nk = K // tk
    dims = {'nn': ((1,), (0,)), 'nt': ((1,), (1,)), 'tn': ((0,), (0,))}[mode]

    def body(a_ref, b_ref, o_ref, *acc):
        part = _dot(a_ref[...].astype(BF16), b_ref[...].astype(BF16), dims)
        if nk == 1:
            o_ref[...] = part.astype(o_ref.dtype)
            return
        k = pl.program_id(2)

        @pl.when(k == 0)
        def _():
            acc[0][...] = part

        @pl.when(k > 0)
        def _():
            acc[0][...] += part

        @pl.when(k == nk - 1)
        def _():
            o_ref[...] = acc[0][...].astype(o_ref.dtype)

    a_spec = pl.BlockSpec((tk, tm), lambda i, j, k: (k, i)) if mode == 'tn' else pl.BlockSpec((tm, tk), lambda i, j, k: (i, k))
    b_spec = pl.BlockSpec((tn, tk), lambda i, j, k: (j, k)) if mode == 'nt' else pl.BlockSpec((tk, tn), lambda i, j, k: (k, j))
    o_spec, o_shape = pl.BlockSpec((tm, tn), lambda i, j, k: (i, j)), (M, N)
    if stack == 'b':
        b_spec = (pl.BlockSpec((None, tn, ns), lambda i, j, k: (k, j, 0)) if mode == 'nt'
                  else pl.BlockSpec((None, tk, ns), lambda i, j, k: (j, k, 0)))
    if stack == 'out':
        o_spec, o_shape = pl.BlockSpec((None, tm, ns), lambda i, j, k: (j, i, 0)), (4, M, ns)
    if halves and mode == 'nt':
        a_spec = pl.BlockSpec((None, tm, ns), lambda i, j, k: (lax.div(k, 2), i, lax.rem(k, 2)))
    if halves and mode == 'tn':
        b_spec = pl.BlockSpec((None, tk, ns), lambda i, j, k: (lax.div(j, 2), k, lax.rem(j, 2)))
    return pl.pallas_call(
        body, name=name, grid=(M // tm, N // tn, nk),
        in_specs=[a_spec, b_spec], out_specs=o_spec,
        out_shape=jax.ShapeDtypeStruct(o_shape, out_dtype),
        scratch_shapes=[pltpu.VMEM((tm, tn), F32)] if nk > 1 else [],
        compiler_params=_cp(("parallel", "parallel", "arbitrary")),
    )(a, b)


def _mm_gu_swiglu(h, wst, name):
    S, K = h.shape
    ns = wst.shape[2]
    tm = _tile(S, 512)

    def body(a_ref, bg_ref, bu_ref, gu_ref, act_ref):
        a = a_ref[...]
        g = _dot(a, bg_ref[...], ((1,), (0,)))
        u = _dot(a, bu_ref[...], ((1,), (0,)))
        gu_ref[0] = g
        gu_ref[1] = u
        act_ref[...] = (g * jax.nn.sigmoid(g) * u).astype(act_ref.dtype)

    return pl.pallas_call(
        body, name=name, grid=(S // tm, 2),
        in_specs=[pl.BlockSpec((tm, K), lambda i, j: (i, 0)), pl.BlockSpec((None, K, ns), lambda i, j: (j, 0, 0)),
                  pl.BlockSpec((None, K, ns), lambda i, j: (j + 2, 0, 0))],
        out_specs=[pl.BlockSpec((2, tm, ns), lambda i, j: (0, i, j)), pl.BlockSpec((tm, ns), lambda i, j: (i, j))],
        out_shape=[jax.ShapeDtypeStruct((2, S, 2 * ns), F32), jax.ShapeDtypeStruct((S, 2 * ns), BF16)],
        compiler_params=_cp(("parallel", "parallel")),
    )(h, wst, wst)


def _mm_down_dx_swiglu(dff, w_down, gu, name):
    S, K = dff.shape
    tm, tn = _tile(S, 512), _tile(D_FF, 1408)

    def body(a_ref, b_ref, g_ref, u_ref, o_ref):
        dact = _dot(a_ref[...].astype(BF16), b_ref[...], ((1,), (1,)))
        g, u = g_ref[...], u_ref[...]
        sg = jax.nn.sigmoid(g)
        o_ref[0] = (dact * u * (sg * (1.0 + g * (1.0 - sg)))).astype(o_ref.dtype)
        o_ref[1] = (dact * (g * sg)).astype(o_ref.dtype)

    return pl.pallas_call(
        body, name=name, grid=(S // tm, D_FF // tn),
        in_specs=[pl.BlockSpec((tm, K), lambda i, j: (i, 0)), pl.BlockSpec((tn, K), lambda i, j: (j, 0)),
                  pl.BlockSpec((None, tm, tn), lambda i, j: (0, i, j)), pl.BlockSpec((None, tm, tn), lambda i, j: (1, i, j))],
        out_specs=pl.BlockSpec((2, tm, tn), lambda i, j: (0, i, j)),
        out_shape=jax.ShapeDtypeStruct((2, S, D_FF), BF16),
        compiler_params=_cp(("parallel", "parallel")),
    )(dff, w_down, gu, gu)


def _rspec(tm, w, cb):
    return pl.BlockSpec((tm, w), lambda i: (i, cb))


def _vspec(shape):
    return pl.BlockSpec(shape, lambda i: (0,) * len(shape))


def _row_fwd(fn, name, rows, vecs, outs, tm=256):
    S = rows[0][0].shape[0]
    tm = min(tm, S)
    nin = len(rows) + len(vecs)

    def body(*refs):
        res = fn(*[r[...] for r in refs[:nin]])
        for o_ref, r in zip(refs[nin:], res):
            o_ref[...] = r.astype(o_ref.dtype)

    return pl.pallas_call(
        body, name=name, grid=(S // tm,),
        in_specs=[_rspec(tm, w, cb) for (_, cb, w) in rows] + [_vspec(v.shape) for v in vecs],
        out_specs=[_rspec(tm, w, 0) for (w, _) in outs],
        out_shape=[jax.ShapeDtypeStruct((S, w), dt) for (w, dt) in outs],
        compiler_params=_cp(("parallel",)),
    )(*[r[0] for r in rows], *vecs)


def _row_bwd(fn, name, rows, vecs, cts, drows, dvecs, tm=256, ddtypes=None):
    S = rows[0][0].shape[0]
    ddtypes = ddtypes or [F32] * len(drows)
    tm = min(tm, S)
    nr, nv, nc = len(rows), len(vecs), len(cts)
    didx = list(drows) + [nr + j for j in dvecs]

    def body(*refs):
        vals = [r[...] for r in refs[:nr + nv]]
        ct = tuple(r[...].astype(F32) for r in refs[nr + nv:nr + nv + nc])
        outs = refs[nr + nv + nc:]

        def g(*d):
            a = list(vals)
            for k, val in zip(didx, d):
                a[k] = val
            return tuple(fn(*a))

        _, vjp = jax.vjp(g, *[vals[k] for k in didx])
        grads = vjp(ct)
        for o, gr in zip(outs[:len(drows)], grads[:len(drows)]):
            o[...] = gr.astype(o.dtype)

        @pl.when(pl.program_id(0) == 0)
        def _():
            for o in outs[len(drows):]:
                o[...] = jnp.zeros_like(o)

        for o, gr in zip(outs[len(drows):], grads[len(drows):]):
            o[...] += gr

    return pl.pallas_call(
        body, name=name, grid=(S // tm,),
        in_specs=[_rspec(tm, w, cb) for (_, cb, w) in rows] + [_vspec(v.shape) for v in vecs]
        + [_rspec(tm, w, cb) for (_, cb, w) in cts],
        out_specs=[_rspec(tm, rows[k][2], 0) for k in drows] + [_vspec(vecs[j].shape) for j in dvecs],
        out_shape=[jax.ShapeDtypeStruct((S, rows[k][2]), dt) for k, dt in zip(drows, ddtypes)]
        + [jax.ShapeDtypeStruct(vecs[j].shape, F32) for j in dvecs],
        compiler_params=_cp(("arbitrary",)),
    )(*[r[0] for r in rows], *vecs, *[c[0] for c in cts])


def _rms(x):
    return x * lax.rsqrt(jnp.mean(x * x, axis=-1, keepdims=True) + EPS)


def fn_norm_mod(x, nw, sc, sh):
    return (_rms(x) * nw * (1.0 + sc) + sh,)


def fn_norm_mod_pass(x, nw, sc, sh):
    return (x, _rms(x) * nw * (1.0 + sc) + sh)


def fn_resid_norm(x, d, g, nw, sc, sh):
    xn = x + g * d
    return (xn, _rms(xn) * nw * (1.0 + sc) + sh)


def fn_lat_norm(qa, kva, qw, kvw):
    return (_rms(qa) * qw, _rms(kva) * kvw)


@functools.partial(jax.custom_vjp, nondiff_argnums=(1,))
def _lroll(x, s):
    return pltpu.roll(x, s, 1)


def _lroll_fwd(x, s):
    return pltpu.roll(x, s, 1), None


def _lroll_bwd(s, _, g):
    return (pltpu.roll(g, (LANES - s) % LANES, 1),)


_lroll.defvjp(_lroll_fwd, _lroll_bwd)


def _lane_masks(shape):
    lane = lax.broadcasted_iota(jnp.int32, shape, 1)
    return (lane < NOPE).astype(F32), ((lane >= NOPE) & (lane < QK_DIM)).astype(F32)


def _rope(t, tc, ts1, ts2):
    return t * tc + _lroll(t, 16) * ts1 + _lroll(t, LANES - 16) * ts2


def fn_qk_prep(q, kv, kpe, tc, ts1, ts2, wq, wk):
    mn, mp = _lane_masks((1, LANES))
    mhi = 1.0 - mn

    def head_norm(t, w):
        rn = lax.rsqrt(jnp.sum(t * t * mn, axis=-1, keepdims=True) * (1.0 / NOPE) + EPS)
        rp = lax.rsqrt(jnp.sum(t * t * mp, axis=-1, keepdims=True) * (1.0 / ROPE) + EPS)
        return t * (rn * mn + rp * mp) * w

    kp = _rope(head_norm(_lroll(kpe, NOPE), wk) * mp, tc, ts1, ts2)
    qs, ks, vs = [], [], []
    for h in range(HEADS):
        qs.append(_rope(head_norm(q[:, h * LANES:(h + 1) * LANES], wq), tc, ts1, ts2))
        t = kv[:, h * LANES:(h + 1) * LANES]
        ks.append(head_norm(t, wk) * mn + kp)
        vs.append(_lroll(t, NOPE) * mn + mhi)
    return (jnp.concatenate(qs, axis=1), jnp.concatenate(ks, axis=1), jnp.concatenate(vs, axis=1))


def fn_gated_norm(y, z, w):
    u = y * jax.nn.silu(z)
    half = D_SSD // 2
    return (jnp.concatenate([_rms(u[:, :half]), _rms(u[:, half:])], axis=1) * w,)


def fn_gated_mix(y, z, ao, w):
    return (jnp.concatenate([ao, fn_gated_norm(y, z, w)[0]], axis=1),)


def _final(x1, ff, tgt, g2, name):
    S = x1.shape[0]
    tm = min(256, S)

    def body(x_ref, f_ref, t_ref, g_ref, dx_ref, df_ref, dg_ref, l_ref):
        @pl.when(pl.program_id(0) == 0)
        def _():
            dg_ref[...] = jnp.zeros_like(dg_ref)
            l_ref[...] = jnp.zeros_like(l_ref)

        f = f_ref[...]
        g = g_ref[...]
        e = x_ref[...] + g * f - t_ref[...]
        dx = e * (1.0 / D_MODEL)
        dx_ref[...] = dx
        df_ref[...] = (g * dx).astype(df_ref.dtype)
        dg_ref[...] += jnp.sum(dx * f, axis=0, keepdims=True)
        l_ref[...] += jnp.sum(e * e) * (0.5 / D_MODEL)

    r = _rspec(tm, D_MODEL, 0)
    return pl.pallas_call(
        body, name=name, grid=(S // tm,),
        in_specs=[r, r, r, _vspec((1, D_MODEL))],
        out_specs=[r, r, _vspec((1, D_MODEL)), _vspec((1, LANES))],
        out_shape=[jax.ShapeDtypeStruct((S, D_MODEL), F32), jax.ShapeDtypeStruct((S, D_MODEL), BF16),
                   jax.ShapeDtypeStruct((1, D_MODEL), F32), jax.ShapeDtypeStruct((1, LANES), F32)],
        compiler_params=_cp(("arbitrary",)),
    )(x1, ff, tgt, g2)


def _causal_mask(t):
    r = lax.broadcasted_iota(jnp.int32, (t, t), 0)
    c = lax.broadcasted_iota(jnp.int32, (t, t), 1)
    return c <= r


LOG2E = 1.4426950408889634
EXP2_SCALE = ATT_SCALE * LOG2E
ATT_TQ, ATT_TK = 512, 1024
ATT_BQ, ATT_BK = 1024, 512


def _attn_fwd(qf, kf, va, name):
    S = qf.shape[0]
    T, TK = min(ATT_TQ, S), min(ATT_TK, S)
    nmask = max(1, T // TK)

    def body(q_ref, k_ref, v_ref, o_ref, l_ref):
        i = pl.program_id(1)
        r = lax.broadcasted_iota(jnp.int32, (T, TK), 0)
        c = lax.broadcasted_iota(jnp.int32, (T, TK), 1)
        qs = [q_ref[:, hh * LANES:(hh + 1) * LANES] for hh in range(2)]

        def blk(j, carry, masked):
            off = pl.multiple_of(j * TK, TK)
            out = []
            for hh in range(2):
                m, acc = carry[hh]
                s = _dot(qs[hh], k_ref[pl.ds(off, TK), hh * LANES:(hh + 1) * LANES], ((1,), (1,)))
                if masked:
                    s = jnp.where(c + j * TK <= r + i * T, s, NEG)
                mn = jnp.maximum(m, jnp.max(s, axis=1, keepdims=True))
                p = jnp.exp2((s - mn) * EXP2_SCALE)
                al = jnp.exp2((m - mn) * EXP2_SCALE)
                vj = v_ref[pl.ds(off, TK), hh * LANES:(hh + 1) * LANES]
                out.append((mn, al * acc + _dot(p.astype(BF16), vj, ((1,), (0,)))))
            return tuple(out)

        one = (jnp.full((T, 1), NEG, F32), jnp.zeros((T, LANES), F32))
        nfull = lax.div(i * T, TK)
        carry = lax.fori_loop(0, nfull, lambda j, cr: blk(j, cr, False), (one, one))
        for t in range(nmask):
            carry = blk(nfull + t, carry, True)
        lane = lax.broadcasted_iota(jnp.int32, (1, LANES), 1)
        z = jnp.zeros((T, LANES), F32)
        for hh in range(2):
            m, acc = carry[hh]
            l = acc[:, 64:65]
            o_ref[:, hh * 64:(hh + 1) * 64] = (acc / l)[:, :64]
            z = z + (m * EXP2_SCALE + jnp.log(l) * LOG2E) * (lane == hh).astype(F32)
        l_ref[0] = z.T[0:2, :]

    return pl.pallas_call(
        body, name=name, grid=(HEADS // 2, S // T),
        in_specs=[pl.BlockSpec((T, 256), lambda h, i: (i, h)), pl.BlockSpec((S, 256), lambda h, i: (0, h)),
                  pl.BlockSpec((S, 256), lambda h, i: (0, h))],
        out_specs=[pl.BlockSpec((T, LANES), lambda h, i: (i, h)), pl.BlockSpec((1, 2, T), lambda h, i: (h, 0, i))],
        out_shape=[jax.ShapeDtypeStruct((S, D_SSD), F32), jax.ShapeDtypeStruct((HEADS // 2, 2, S), F32)],
        compiler_params=_cp(("parallel", "parallel")),
    )(qf, kf, va)


def _attn_delta(dmix, ao, name):
    S = ao.shape[0]
    tm = min(512, S)

    def body(d_ref, o_ref, out_ref):
        lane = lax.broadcasted_iota(jnp.int32, (1, LANES), 1)
        lo = (lane < 64).astype(F32)
        for hp in range(HEADS // 2):
            y = d_ref[:, hp * LANES:(hp + 1) * LANES] * o_ref[:, hp * LANES:(hp + 1) * LANES]
            z = (jnp.sum(y * lo, axis=1, keepdims=True) * (lane == 0).astype(F32)
                 + jnp.sum(y * (1.0 - lo), axis=1, keepdims=True) * (lane == 1).astype(F32))
            out_ref[hp] = z.T[0:2, :]

    return pl.pallas_call(
        body, name=name, grid=(S // tm,),
        in_specs=[pl.BlockSpec((tm, D_SSD), lambda i: (i, 0)), pl.BlockSpec((tm, D_SSD), lambda i: (i, 0))],
        out_specs=pl.BlockSpec((HEADS // 2, 2, tm), lambda i: (0, 0, i)),
        out_shape=jax.ShapeDtypeStruct((HEADS // 2, 2, S), F32),
        compiler_params=_cp(("parallel",)),
    )(dmix, ao)


def _attn_bwd(qf, kf, kT, va, do, lse_r, delta_r, name):
    S = qf.shape[0]
    T, TK = min(ATT_BQ, S), min(ATT_BK, S)
    nq = S // T
    nmask = max(1, TK // T)

    def body(q_ref, k_ref, kT_ref, v_ref, do_ref, l_ref, d_ref, dqT_ref, dk_ref, dv_ref):
        j = pl.program_id(1)

        @pl.when(j == 0)
        def _():
            dqT_ref[...] = jnp.zeros_like(dqT_ref)

        r = lax.broadcasted_iota(jnp.int32, (TK, T), 0)
        c = lax.broadcasted_iota(jnp.int32, (TK, T), 1)
        lo = (lax.broadcasted_iota(jnp.int32, (1, LANES), 1) < 64).astype(F32)
        ks = [k_ref[:, hh * LANES:(hh + 1) * LANES] for hh in range(2)]
        vs = [v_ref[:, hh * LANES:(hh + 1) * LANES] for hh in range(2)]
        kTs = [kT_ref[hh * LANES:(hh + 1) * LANES, :] for hh in range(2)]

        def blk(i, carry, masked):
            off = pl.multiple_of(i * T, T)
            dall = do_ref[pl.ds(off, T), :]
            out = []
            for hh in range(2):
                dk, dv = carry[hh]
                q = q_ref[pl.ds(off, T), hh * LANES:(hh + 1) * LANES]
                dop = ((dall if hh == 0 else pltpu.roll(dall, 64, 1)) * lo).astype(BF16)
                lrow = l_ref[0, hh:hh + 1, pl.ds(off, T)]
                drow = d_ref[0, hh:hh + 1, pl.ds(off, T)]
                pT = jnp.exp2(_dot(ks[hh], q, ((1,), (1,))) * EXP2_SCALE - lrow)
                if masked:
                    pT = jnp.where(r + j * TK <= c + i * T, pT, 0.0)
                dpT = _dot(vs[hh], dop, ((1,), (1,)))
                dsT = (pT * (dpT - drow) * ATT_SCALE).astype(BF16)
                dv = dv + _dot(pT.astype(BF16), dop, ((1,), (0,)))
                dk = dk + _dot(dsT, q, ((1,), (0,)))
                dqT_ref[hh * LANES:(hh + 1) * LANES, pl.ds(off, T)] += _dot(kTs[hh], dsT, ((1,), (0,)))
                out.append((dk, dv))
            return tuple(out)

        z = (jnp.zeros((TK, LANES), F32), jnp.zeros((TK, LANES), F32))
        first = lax.div(j * TK, T)
        carry = (z, z)
        for t in range(nmask):
            carry = blk(first + t, carry, True)
        carry = lax.fori_loop(first + nmask, nq, lambda i, cr: blk(i, cr, False), carry)
        for hh in range(2):
            dk_ref[:, hh * LANES:(hh + 1) * LANES] = carry[hh][0]
            dv_ref[:, hh * LANES:(hh + 1) * LANES] = carry[hh][1]

    return pl.pallas_call(
        body, name=name, grid=(HEADS // 2, S // TK),
        in_specs=[pl.BlockSpec((S, 256), lambda h, j: (0, h)), pl.BlockSpec((TK, 256), lambda h, j: (j, h)),
                  pl.BlockSpec((256, TK), lambda h, j: (h, j)), pl.BlockSpec((TK, 256), lambda h, j: (j, h)),
                  pl.BlockSpec((S, LANES), lambda h, j: (0, h)), pl.BlockSpec((1, 2, S), lambda h, j: (h, 0, 0)),
                  pl.BlockSpec((1, 2, S), lambda h, j: (h, 0, 0))],
        out_specs=[pl.BlockSpec((256, S), lambda h, j: (h, 0)), pl.BlockSpec((TK, 256), lambda h, j: (j, h)),
                   pl.BlockSpec((TK, 256), lambda h, j: (j, h))],
        out_shape=[jax.ShapeDtypeStruct((D_MODEL, S), F32), jax.ShapeDtypeStruct((S, D_MODEL), F32),
                   jax.ShapeDtypeStruct((S, D_MODEL), F32)],
        compiler_params=_cp(("parallel", "arbitrary")),
    )(qf, kf, kT, va, do, lse_r, delta_r)


def _shift_down(x, s):
    if s == 0:
        return x
    rows = lax.broadcasted_iota(jnp.int32, x.shape, 0)
    return jnp.where(rows >= s, pltpu.roll(x, s, 0), 0.0)


def _shift_up(x, s):
    if s == 0:
        return x
    n = x.shape[0]
    rows = lax.broadcasted_iota(jnp.int32, x.shape, 0)
    return jnp.where(rows < n - s, pltpu.roll(x, n - s, 0), 0.0)


def _conv_fwd(proj, cvec, name):
    S = proj.shape[0]

    def body(x_ref, c_ref, o_ref):
        x = x_ref[...]
        y = jnp.broadcast_to(c_ref[4:5, :], x.shape)
        for k in range(CONV_TAPS):
            y = y + c_ref[k:k + 1, :] * _shift_down(x, CONV_TAPS - 1 - k)
        o_ref[...] = y * jax.nn.sigmoid(y)

    return pl.pallas_call(
        body, name=name, grid=(D_CONV // LANES,),
        in_specs=[pl.BlockSpec((S, LANES), lambda j: (0, 8 + j)), pl.BlockSpec((8, LANES), lambda j: (0, j))],
        out_specs=pl.BlockSpec((S, LANES), lambda j: (0, j)),
        out_shape=jax.ShapeDtypeStruct((S, D_CONV), F32),
        compiler_params=_cp(("parallel",)),
    )(proj, cvec)


def _conv_bwd(proj, cvec, dact, name):
    S = proj.shape[0]

    def body(x_ref, c_ref, d_ref, dx_ref, dc_ref):
        x = x_ref[...]
        y = jnp.broadcast_to(c_ref[4:5, :], x.shape)
        for k in range(CONV_TAPS):
            y = y + c_ref[k:k + 1, :] * _shift_down(x, CONV_TAPS - 1 - k)
        sg = jax.nn.sigmoid(y)
        dy = d_ref[...] * (sg * (1.0 + y * (1.0 - sg)))
        dx = jnp.zeros_like(x)
        for k in range(CONV_TAPS):
            s = CONV_TAPS - 1 - k
            dx = dx + c_ref[k:k + 1, :] * _shift_up(dy, s)
            dc_ref[k:k + 1, :] = jnp.sum(dy * _shift_down(x, s), axis=0, keepdims=True)
        dx_ref[...] = dx
        dc_ref[4:5, :] = jnp.sum(dy, axis=0, keepdims=True)
        dc_ref[5:8, :] = jnp.zeros((3, LANES), F32)

    return pl.pallas_call(
        body, name=name, grid=(D_CONV // LANES,),
        in_specs=[pl.BlockSpec((S, LANES), lambda j: (0, 8 + j)), pl.BlockSpec((8, LANES), lambda j: (0, j)),
                  pl.BlockSpec((S, LANES), lambda j: (0, j))],
        out_specs=[pl.BlockSpec((S, LANES), lambda j: (0, j)), pl.BlockSpec((8, LANES), lambda j: (0, j))],
        out_shape=[jax.ShapeDtypeStruct((S, D_CONV), F32), jax.ShapeDtypeStruct((8, D_CONV), F32)],
        compiler_params=_cp(("parallel",)),
    )(proj, cvec, dact)


def fn_ssd_chunk(xs, bm, cm, dtr, state, vecs):
    Q = CHUNK
    dt = jax.nn.softplus(dtr + vecs[0:1])
    a = -jnp.exp(vecs[1:2])
    adt = dt * a
    tril = _causal_mask(Q)
    acs = _dot(tril.astype(F32), adt, ((1,), (0,)), lax.Precision.HIGHEST)
    acs_t = acs.T
    alast = acs[Q - 1:Q, :]
    r = lax.broadcasted_iota(jnp.int32, (LANES, D_SSD), 0)
    c = lax.broadcasted_iota(jnp.int32, (LANES, D_SSD), 1)
    spread = (lax.shift_right_logical(c, 6) == r).astype(F32)

    def per_head(v):
        return _dot(v, spread, ((1,), (0,)), lax.Precision.HIGH)

    xdt = xs * per_head(dt)
    ub = (xdt * per_head(jnp.exp(alast - acs))).astype(BF16)
    xdtb = xdt.astype(BF16)
    Bs = [bm[:, g * SSD_N:(g + 1) * SSD_N].astype(BF16) for g in range(2)]
    Cs = [cm[:, g * SSD_N:(g + 1) * SSD_N].astype(BF16) for g in range(2)]
    Gs = [_dot(Cs[g], Bs[g], ((1,), (1,))) for g in range(2)]
    yds, yos, adds = [], [], []
    for h in range(SSD_HEADS):
        g = h // (SSD_HEADS // 2)
        sl = slice(h * SSD_P, (h + 1) * SSD_P)
        L = jnp.exp(jnp.where(tril, acs[:, h:h + 1] - acs_t[h:h + 1, :], -jnp.inf))
        yds.append(_dot((Gs[g] * L).astype(BF16), xdtb[:, sl], ((1,), (0,))))
        yos.append(_dot(Cs[g], state[h].astype(BF16), ((1,), (1,))))
        adds.append(_dot(ub[:, sl], Bs[g], ((0,), (0,))))
    y = jnp.concatenate(yds, axis=1) + jnp.concatenate(yos, axis=1) * per_head(jnp.exp(acs)) + per_head(vecs[2:3]) * xs
    decay = jnp.stack([jnp.broadcast_to(jnp.exp(alast[:, h:h + 1]), (SSD_P, SSD_N)) for h in range(SSD_HEADS)])
    return y, jnp.stack(adds) + state * decay


def _ssd_fwd(xact, proj, svec, name):
    S = xact.shape[0]
    nc = S // CHUNK

    def body(x_ref, dt_ref, v_ref, y_ref, st_ref, state):
        @pl.when(pl.program_id(0) == 0)
        def _():
            state[...] = jnp.zeros_like(state)

        st_ref[0] = state[...]
        x = x_ref[...]
        y, sn = fn_ssd_chunk(x[:, 0:512], x[:, 512:768], x[:, 768:1024], dt_ref[...], state[...], v_ref[...])
        y_ref[...] = y
        state[...] = sn

    return pl.pallas_call(
        body, name=name, grid=(nc,),
        in_specs=[pl.BlockSpec((CHUNK, D_CONV), lambda i: (i, 0)), pl.BlockSpec((CHUNK, LANES), lambda i: (i, 16)),
                  pl.BlockSpec((8, LANES), lambda i: (0, 0))],
        out_specs=[pl.BlockSpec((CHUNK, D_SSD), lambda i: (i, 0)),
                   pl.BlockSpec((1, SSD_HEADS, SSD_P, SSD_N), lambda i: (i, 0, 0, 0))],
        out_shape=[jax.ShapeDtypeStruct((S, D_SSD), F32), jax.ShapeDtypeStruct((nc, SSD_HEADS, SSD_P, SSD_N), F32)],
        scratch_shapes=[pltpu.VMEM((SSD_HEADS, SSD_P, SSD_N), F32)],
        compiler_params=_cp(("arbitrary",)),
    )(xact, proj, svec)


def _ssd_bwd(xact, proj, svec, states, dy, name):
    S = xact.shape[0]
    nc = S // CHUNK

    def body(x_ref, dt_ref, v_ref, st_ref, dy_ref, dx_ref, ddt_ref, dv_ref, dstate):
        @pl.when(pl.program_id(0) == 0)
        def _():
            dstate[...] = jnp.zeros_like(dstate)
            dv_ref[...] = jnp.zeros_like(dv_ref)

        x = x_ref[...]
        _, vjp = jax.vjp(fn_ssd_chunk, x[:, 0:512], x[:, 512:768], x[:, 768:1024], dt_ref[...], st_ref[0], v_ref[...])
        dxs, dbm, dcm, ddt, dst, dvec = vjp((dy_ref[...], dstate[...]))
        dx_ref[:, 0:512] = dxs
        dx_ref[:, 512:768] = dbm
        dx_ref[:, 768:1024] = dcm
        ddt_ref[...] = ddt
        dstate[...] = dst
        dv_ref[...] += dvec

    rev = lambda i: (nc - 1 - i, 0)
    return pl.pallas_call(
        body, name=name, grid=(nc,),
        in_specs=[pl.BlockSpec((CHUNK, D_CONV), rev), pl.BlockSpec((CHUNK, LANES), lambda i: (nc - 1 - i, 16)),
                  pl.BlockSpec((8, LANES), lambda i: (0, 0)),
                  pl.BlockSpec((1, SSD_HEADS, SSD_P, SSD_N), lambda i: (nc - 1 - i, 0, 0, 0)),
                  pl.BlockSpec((CHUNK, D_SSD), rev)],
        out_specs=[pl.BlockSpec((CHUNK, D_CONV), rev), pl.BlockSpec((CHUNK, LANES), rev),
                   pl.BlockSpec((8, LANES), lambda i: (0, 0))],
        out_shape=[jax.ShapeDtypeStruct((S, D_CONV), F32), jax.ShapeDtypeStruct((S, LANES), F32),
                   jax.ShapeDtypeStruct((8, LANES), F32)],
        scratch_shapes=[pltpu.VMEM((SSD_HEADS, SSD_P, SSD_N), F32)],
        compiler_params=_cp(("arbitrary",)),
    )(xact, proj, svec, states, dy)


def _ada_fwd(c_all, w_ada, b_sh, name):
    nb = 1536 // 512

    def body(c_ref, w_ref, b_ref, o_ref):
        ca = jax.nn.silu(c_ref[...]).astype(BF16)
        o_ref[0] = _dot(ca, w_ref[0].astype(BF16), ((1,), (0,))) + b_ref[0]

    return pl.pallas_call(
        body, name=name, grid=(DEPTH, nb),
        in_specs=[pl.BlockSpec((8, D_MODEL), lambda l, j: (0, 0)), pl.BlockSpec((1, D_MODEL, 512), lambda l, j: (l, 0, j)),
                  pl.BlockSpec((1, 1, 512), lambda l, j: (l, 0, j))],
        out_specs=pl.BlockSpec((1, 8, 512), lambda l, j: (l, 0, j)),
        out_shape=jax.ShapeDtypeStruct((DEPTH, 8, 1536), F32),
        compiler_params=_cp(("parallel", "parallel")),
    )(c_all, w_ada, b_sh)


def _ada_bwd(c_all_t, dmod_sh, name):
    nb = 1536 // 512

    def body(c_ref, d_ref, o_ref):
        ca = jax.nn.silu(c_ref[...])
        acc = ca[:, 0:1] * d_ref[0, 0:1, :]
        for b in range(1, 8):
            acc = acc + ca[:, b:b + 1] * d_ref[0, b:b + 1, :]
        o_ref[0] = acc

    return pl.pallas_call(
        body, name=name, grid=(DEPTH, nb),
        in_specs=[pl.BlockSpec((D_MODEL, 8), lambda l, j: (0, 0)), pl.BlockSpec((1, 8, 512), lambda l, j: (l, 0, j))],
        out_specs=pl.BlockSpec((1, D_MODEL, 512), lambda l, j: (l, 0, j)),
        out_shape=jax.ShapeDtypeStruct((DEPTH, D_MODEL, 1536), F32),
        compiler_params=_cp(("parallel", "parallel")),
    )(c_all_t, dmod_sh)


def _rows_tile(rows):
    return next(t for t in (512, 256, 128, 64, 32, 16, 8) if rows % t == 0)


SUM_BLOCKS = 4
ADAM_BLOCKS = 8


def _sum_sibling(gs, ls, ci, name):
    n = len(gs)

    def body(c_ref, *refs):
        for p in range(n):
            refs[2 * n + p][...] = refs[2 * p][...] + refs[2 * p + 1][...]

    in_specs, out_specs, out_shape = [], [], []
    for g in gs:
        _, _, rh, cw = g.shape
        rb = rh // SUM_BLOCKS
        in_specs += [pl.BlockSpec((None, None, rb, cw), lambda s, i, c: (s, c[0], i, 0)),
                     pl.BlockSpec((None, rb, cw), lambda s, i, c: (s, i, 0))]
        out_specs.append(pl.BlockSpec((None, rb, cw), lambda s, i, c: (s, i, 0)))
        out_shape.append(jax.ShapeDtypeStruct((4, rh, cw), F32))
    ops = [a for pair in zip(gs, ls) for a in pair]
    return pl.pallas_call(
        body, name=name,
        grid_spec=pltpu.PrefetchScalarGridSpec(num_scalar_prefetch=1, grid=(4, SUM_BLOCKS), in_specs=in_specs, out_specs=out_specs),
        out_shape=out_shape, compiler_params=_cp(("parallel", "parallel")),
    )(ci.reshape(1).astype(jnp.int32), *ops)


def _sum_chips(cs, lands, chip, name):
    n = len(cs)

    def body(c_ref, *refs):
        for p in range(n):
            a = refs[4 * p:4 * p + 4]
            refs[4 * n + p][...] = ((a[0][...] + a[1][...]) + a[2][...]) + a[3][...]

    in_specs, out_specs, out_shape = [], [], []
    for c in cs:
        _, rh, cw = c.shape
        rb = rh // SUM_BLOCKS
        in_specs.append(pl.BlockSpec((None, rb, cw), lambda i, ch: (ch[0], i, 0)))
        in_specs += [pl.BlockSpec((None, rb, cw), functools.partial(lambda i, ch, k: (k, i, 0), k=k)) for k in range(3)]
        out_specs.append(pl.BlockSpec((rb, cw), lambda i, ch: (i, 0)))
        out_shape.append(jax.ShapeDtypeStruct((rh, cw), F32))
    ops = [a for c, l in zip(cs, lands) for a in (c, l, l, l)]
    return pl.pallas_call(
        body, name=name,
        grid_spec=pltpu.PrefetchScalarGridSpec(num_scalar_prefetch=1, grid=(SUM_BLOCKS,), in_specs=in_specs, out_specs=out_specs),
        out_shape=out_shape, compiler_params=_cp(("parallel",)),
    )(chip.reshape(1).astype(jnp.int32), *ops)


def _adam_update(w, m, v, g):
    c1 = 1.0 / (1.0 - ADAM_B1 ** ADAM_STEP)
    c2 = 1.0 / (1.0 - ADAM_B2 ** ADAM_STEP)
    nm = ADAM_B1 * m + (1.0 - ADAM_B1) * g
    nv = ADAM_B2 * v + (1.0 - ADAM_B2) * (g * g)
    return -ADAM_LR * ((nm * c1) / (jnp.sqrt(nv * c2) + ADAM_EPS) + ADAM_WD * w), nm, nv


def _adam_multi(ws, ms, vs, gs, name):
    n = len(ws)

    def body(*refs):
        for p in range(n):
            d, nm, nv = _adam_update(*[refs[4 * p + k][...] for k in range(4)])
            refs[4 * n + 3 * p][...] = d
            refs[4 * n + 3 * p + 1][...] = nm
            refs[4 * n + 3 * p + 2][...] = nv

    in_specs, out_specs, out_shape = [], [], []
    for w in ws:
        _, r, cw = w.shape
        spec = pl.BlockSpec((None, r // ADAM_BLOCKS, cw), lambda l, i: (l, i, 0))
        in_specs += [spec] * 4
        out_specs += [spec] * 3
        out_shape += [jax.ShapeDtypeStruct(w.shape, F32)] * 3
    ops = [a for q in zip(ws, ms, vs, gs) for a in q]
    res = pl.pallas_call(
        body, name=name, grid=(DEPTH, ADAM_BLOCKS), in_specs=in_specs, out_specs=out_specs, out_shape=out_shape,
        compiler_params=_cp(("parallel", "parallel")),
    )(*ops)
    return res[0::3], res[1::3], res[2::3]


def _adam(w, m, v, parts, name):
    rows, width = w.shape
    bm = min(256, _rows_tile(rows))
    np_ = len(parts)
    c1 = 1.0 / (1.0 - ADAM_B1 ** ADAM_STEP)
    c2 = 1.0 / (1.0 - ADAM_B2 ** ADAM_STEP)

    def body(*refs):
        w_ref, m_ref, v_ref = refs[:3]
        g = refs[3][...]
        for r in refs[4:3 + np_]:
            g = g + r[...]
        g_ref, d_ref, nm_ref, nv_ref = refs[3 + np_:]
        nm = ADAM_B1 * m_ref[...] + (1.0 - ADAM_B1) * g
        nv = ADAM_B2 * v_ref[...] + (1.0 - ADAM_B2) * (g * g)
        g_ref[...] = g
        nm_ref[...] = nm
        nv_ref[...] = nv
        d_ref[...] = -ADAM_LR * ((nm * c1) / (jnp.sqrt(nv * c2) + ADAM_EPS) + ADAM_WD * w_ref[...])

    blk = pl.BlockSpec((bm, width), lambda i: (i, 0))
    return pl.pallas_call(
        body, name=name, grid=(rows // bm,),
        in_specs=[blk, blk, blk] + [pl.BlockSpec((bm, width), functools.partial(lambda i, o: (i + o, 0), o=off // bm))
                                    for (_, off) in parts],
        out_specs=[blk, blk, blk, blk],
        out_shape=[jax.ShapeDtypeStruct((rows, width), F32)] * 4,
        compiler_params=_cp(("parallel",)),
    )(w, m, v, *[p[0] for p in parts])


def _coords():
    return lax.axis_index("x"), lax.axis_index("y"), lax.axis_index("c")


def _other_chips(x, y):
    return [(1 - x, y), (x, 1 - y), (1 - x, 1 - y)]


def _ag8(blk, name):
    m_per, n = blk.shape

    def body(x_ref, out_ref, send_sems, recv_sems, local_sem):
        x, y, c = _coords()
        me, sibling = (x, y, c), (x, y, 1 - c)
        chips = _other_chips(x, y)

        def rows(px, py, pc):
            return out_ref.at[pl.ds((4 * px + 2 * py + pc) * m_per, m_per), :]

        def copy(k, block, to, src=None):
            return pltpu.make_async_remote_copy(
                src_ref=rows(*block) if src is None else src, dst_ref=rows(*block),
                send_sem=send_sems.at[k], recv_sem=recv_sems.at[k], device_id=to, device_id_type=MESH)

        mine = pltpu.make_async_copy(x_ref, rows(*me), local_sem)
        mine.start()
        first = [copy(0, me, sibling, src=x_ref)]
        first += [copy(1 + j, me, (*chip, c), src=x_ref) for j, chip in enumerate(chips)]
        for cp in first:
            cp.start()
        passed = [copy(4 + j, (*chip, c), sibling) for j, chip in enumerate(chips)]
        for j, chip in enumerate(chips):
            copy(1 + j, (*chip, c), me).wait_recv()
            passed[j].start()
        copy(0, sibling, me).wait_recv()
        for j, chip in enumerate(chips):
            copy(4 + j, (*chip, 1 - c), me).wait_recv()
        for cp in first + passed:
            cp.wait_send()
        mine.wait()

    return pl.pallas_call(
        body, name=name,
        out_shape=jax.ShapeDtypeStruct((8 * m_per, n), blk.dtype),
        in_specs=[pl.BlockSpec(memory_space=pltpu.VMEM)], out_specs=pl.BlockSpec(memory_space=pltpu.VMEM),
        scratch_shapes=[pltpu.SemaphoreType.DMA((7,)), pltpu.SemaphoreType.DMA((7,)), pltpu.SemaphoreType.DMA],
    )(blk)


HBM_SPEC = pl.BlockSpec(memory_space=pltpu.HBM)
SEM_SPEC = pl.BlockSpec(memory_space=pltpu.SEMAPHORE)
EFFECT = pltpu.SideEffectType.DATAFLOW_SIDE_EFFECTING


def _remote(src, dst, send_sem, recv_sem, to):
    return pltpu.make_async_remote_copy(src_ref=src, dst_ref=dst, send_sem=send_sem, recv_sem=recv_sem,
                                        device_id=to, device_id_type=MESH)


def _ag_list(shards, name):
    n = len(shards)

    def body(*refs):
        sh, out = refs[:n], refs[n:2 * n]
        send_sems, recv_sems = refs[2 * n:]
        x, y, c = _coords()
        sibling = (x, y, 1 - c)
        chips = _other_chips(x, y)
        first = [_remote(sh[p].at[c], out[p].at[2 * x + y, c], send_sems.at[6 * p + j], recv_sems.at[6 * p + j], (px, py, c))
                 for p in range(n) for j, (px, py) in enumerate(chips)]
        for cp in first:
            cp.start()
        passed = []
        for j, (px, py) in enumerate(chips):
            for p in range(n):
                got = out[p].at[2 * px + py, c]
                _remote(got, got, send_sems.at[6 * p + j], recv_sems.at[6 * p + j], (x, y, c)).wait_recv()
                cp = _remote(got, got, send_sems.at[6 * p + 3 + j], recv_sems.at[6 * p + 3 + j], sibling)
                cp.start()
                passed.append(cp)
        for j, (px, py) in enumerate(chips):
            for p in range(n):
                got = out[p].at[2 * px + py, 1 - c]
                _remote(got, got, send_sems.at[6 * p + 3 + j], recv_sems.at[6 * p + 3 + j], (x, y, c)).wait_recv()
        for cp in first + passed:
            cp.wait_send()

    return pl.pallas_call(
        body, name=name,
        out_shape=[jax.ShapeDtypeStruct((4,) + s.shape, s.dtype) for s in shards],
        in_specs=[pl.BlockSpec(memory_space=pl.ANY)] * n, out_specs=[pl.BlockSpec(memory_space=pl.ANY)] * n,
        scratch_shapes=[pltpu.SemaphoreType.DMA((6 * n,)), pltpu.SemaphoreType.DMA((6 * n,))],
    )(*shards)


def _ag_direct_copies(sh, land, send_sems, recv_sems, starting):
    x, y, c = _coords()
    return [_remote(sh[p], land[p].at[2 * x + y] if starting else land[p].at[2 * px + py],
                    send_sems.at[3 * p + j], recv_sems.at[3 * p + j], (px, py, c))
            for p in range(len(sh)) for j, (px, py) in enumerate(_other_chips(x, y))]


def _ag_direct_start(shards, name):
    n = len(shards)

    def body(*refs):
        for cp in _ag_direct_copies(refs[:n], refs[n:2 * n], refs[2 * n], refs[2 * n + 1], True):
            cp.start()
        token = refs[4 * n + 2]
        token[...] = jnp.zeros_like(token)

    lands = [pltpu.with_memory_space_constraint(lax.empty((4,) + s.shape, s.dtype), pltpu.HBM) for s in shards]
    res = pl.pallas_call(
        body, name=name,
        out_shape=(pltpu.SemaphoreType.DMA((3 * n,)), pltpu.SemaphoreType.DMA((3 * n,)))
        + tuple(pltpu.HBM(s.shape, s.dtype) for s in shards) + tuple(pltpu.HBM(l.shape, l.dtype) for l in lands)
        + (jax.ShapeDtypeStruct((8, LANES), F32),),
        in_specs=(HBM_SPEC,) * (2 * n), out_specs=(SEM_SPEC, SEM_SPEC) + (HBM_SPEC,) * (2 * n) + (pl.BlockSpec(memory_space=pltpu.VMEM),),
        input_output_aliases={i: 2 + i for i in range(2 * n)},
        compiler_params=pltpu.CompilerParams(has_side_effects=EFFECT),
    )(*[pltpu.with_memory_space_constraint(s, pltpu.HBM) for s in shards], *lands)
    return res[0], res[1], res[2:2 + n], res[2 + n:2 + 2 * n], res[2 + 2 * n]


def _ag_direct_wait(send_sems, recv_sems, sh_thru, land_thru, after, name):
    n = len(sh_thru)

    def body(*refs):
        sh, land = refs[:n], refs[n:2 * n]
        for cp in _ag_direct_copies(sh, land, refs[2 * n], refs[2 * n + 1], False):
            cp.wait_send()
            cp.wait_recv()

    res = pl.pallas_call(
        body, name=name,
        out_shape=tuple(pltpu.HBM(s.shape, s.dtype) for s in sh_thru) + tuple(pltpu.HBM(l.shape, l.dtype) for l in land_thru),
        in_specs=(HBM_SPEC,) * (2 * n) + (SEM_SPEC, SEM_SPEC, pl.BlockSpec(memory_space=pl.ANY)),
        out_specs=(HBM_SPEC,) * (2 * n), input_output_aliases={i: i for i in range(2 * n)},
        compiler_params=pltpu.CompilerParams(has_side_effects=EFFECT),
    )(*sh_thru, *land_thru, send_sems, recv_sems, after)
    return res[n:]


def _rs_sibling_list(gs, name):
    n = len(gs)

    def body(*refs):
        g, out, send_sems, recv_sems = refs[:n], refs[n:2 * n], refs[2 * n], refs[2 * n + 1]
        x, y, c = _coords()
        cps = [_remote(g[p].at[s, 1 - c], out[p].at[s], send_sems.at[4 * p + s], recv_sems.at[4 * p + s], (x, y, 1 - c))
               for p in range(n) for s in range(4)]
        for cp in cps:
            cp.start()
        for cp in cps:
            cp.wait_recv()
        for cp in cps:
            cp.wait_send()

    return pl.pallas_call(
        body, name=name,
        out_shape=[jax.ShapeDtypeStruct((4,) + g.shape[2:], g.dtype) for g in gs],
        in_specs=[pl.BlockSpec(memory_space=pl.ANY)] * n, out_specs=[pl.BlockSpec(memory_space=pl.ANY)] * n,
        scratch_shapes=[pltpu.SemaphoreType.DMA((4 * n,)), pltpu.SemaphoreType.DMA((4 * n,))],
    )(*gs)


def _rs_chips_copies(cs, land, send_sems, recv_sems):
    x, y, c = _coords()
    return [_remote(cs[p].at[2 * px + py], land[p].at[j], send_sems.at[3 * p + j], recv_sems.at[3 * p + j], (px, py, c))
            for p in range(len(cs)) for j, (px, py) in enumerate(_other_chips(x, y))]


def _rs_chips_start(cs, name):
    n = len(cs)

    def body(*refs):
        for cp in _rs_chips_copies(refs[:n], refs[n:2 * n], refs[2 * n], refs[2 * n + 1]):
            cp.start()
        token = refs[4 * n + 2]
        token[...] = jnp.zeros_like(token)

    lands = [pltpu.with_memory_space_constraint(lax.empty((3,) + c.shape[1:], c.dtype), pltpu.HBM) for c in cs]
    res = pl.pallas_call(
        body, name=name,
        out_shape=(pltpu.SemaphoreType.DMA((3 * n,)), pltpu.SemaphoreType.DMA((3 * n,)))
        + tuple(pltpu.HBM(c.shape, c.dtype) for c in cs) + tuple(pltpu.HBM(l.shape, l.dtype) for l in lands)
        + (jax.ShapeDtypeStruct((8, LANES), F32),),
        in_specs=(HBM_SPEC,) * (2 * n), out_specs=(SEM_SPEC, SEM_SPEC) + (HBM_SPEC,) * (2 * n) + (pl.BlockSpec(memory_space=pltpu.VMEM),),
        input_output_aliases={i: 2 + i for i in range(2 * n)},
        compiler_params=pltpu.CompilerParams(has_side_effects=EFFECT),
    )(*[pltpu.with_memory_space_constraint(c, pltpu.HBM) for c in cs], *lands)
    return res[0], res[1], res[2:2 + n], res[2 + n:2 + 2 * n], res[2 + 2 * n]


def _rs_chips_wait(send_sems, recv_sems, cs_thru, land_thru, after, name):
    n = len(cs_thru)

    def body(*refs):
        for cp in _rs_chips_copies(refs[:n], refs[n:2 * n], refs[2 * n], refs[2 * n + 1]):
            cp.wait_send()
            cp.wait_recv()

    res = pl.pallas_call(
        body, name=name,
        out_shape=tuple(pltpu.HBM(c.shape, c.dtype) for c in cs_thru) + tuple(pltpu.HBM(l.shape, l.dtype) for l in land_thru),
        in_specs=(HBM_SPEC,) * (2 * n) + (SEM_SPEC, SEM_SPEC, pl.BlockSpec(memory_space=pl.ANY)),
        out_specs=(HBM_SPEC,) * (2 * n), input_output_aliases={i: i for i in range(2 * n)},
        compiler_params=pltpu.CompilerParams(has_side_effects=EFFECT),
    )(*cs_thru, *land_thru, send_sems, recv_sems, after)
    return res[:n], res[n:]


def _swap_list(ghs, name):
    n = len(ghs)

    def body(*refs):
        g, out, send_sems, recv_sems = refs[:n], refs[n:2 * n], refs[2 * n], refs[2 * n + 1]
        x, y, c = _coords()
        cps = [_remote(g[p], out[p], send_sems.at[p], recv_sems.at[p], (x, y, 1 - c)) for p in range(n)]
        for cp in cps:
            cp.start()
        for cp in cps:
            cp.wait_recv()
        for cp in cps:
            cp.wait_send()

    return pl.pallas_call(
        body, name=name,
        out_shape=[jax.ShapeDtypeStruct(g.shape, g.dtype) for g in ghs],
        in_specs=[pl.BlockSpec(memory_space=pl.ANY)] * n, out_specs=[pl.BlockSpec(memory_space=pl.ANY)] * n,
        scratch_shapes=[pltpu.SemaphoreType.DMA((n,)), pltpu.SemaphoreType.DMA((n,))],
    )(*ghs)


def _pad_win(w):
    return jnp.concatenate([w[:, :416], jnp.zeros((w.shape[0], 96), w.dtype), w[:, 416:1952],
                            w[:, 1952:1960], jnp.zeros((w.shape[0], 120), w.dtype)], axis=1)


def _unpad_win(g):
    return jnp.concatenate([g[:, :416], g[:, 512:2048], g[:, 2048:2056]], axis=1)


def _pad_wq(w):
    return jnp.pad(w.reshape(Q_LORA, HEADS, QK_DIM), ((0, 0), (0, 0), (0, LANES - QK_DIM))).reshape(Q_LORA, HEADS * LANES)


def _unpad_wq(g):
    return g.reshape(Q_LORA, HEADS, LANES)[:, :, :QK_DIM].reshape(Q_LORA, HEADS * QK_DIM)


def _cols_to_shards(a):
    r, c4 = a.shape
    return a.reshape(r, 4, c4 // 4).transpose(1, 0, 2)


def _shards_to_cols(a):
    _, r, c = a.shape
    return a.transpose(1, 0, 2).reshape(r, 4 * c)


def _pack_small(tree):
    parts = []
    for l in range(DEPTH):
        for (n, k) in SMALL:
            parts.append(jnp.pad(tree[n][l].reshape(-1), (0, -k % LANES)))
    flat = jnp.concatenate(parts)
    return jnp.pad(flat, (0, SMALL_ROWS * LANES - flat.shape[0])).reshape(SMALL_ROWS, LANES)


def _unpack_small(buf):
    flat = buf.reshape(-1)
    out = {n: [] for (n, _) in SMALL}
    o = 0
    for l in range(DEPTH):
        for (n, k) in SMALL:
            out[n].append(flat[o:o + k])
            o += k + (-k % LANES)
    return {n: jnp.stack(v) for n, v in out.items()}


def _vec(v, width=LANES):
    return jnp.pad(v.reshape(1, -1), ((0, 0), (0, width - v.shape[-1])))


def kernel(x, c, positions, norm1_w, norm2_w, w_ada, b_ada, w_in, q_a_norm_w, w_q_up, kv_a_norm_w, w_kv_up, q_nope_norm_w, q_pe_norm_w, k_nope_norm_w, k_pe_norm_w, conv_w, conv_b, dt_bias, a_log, d_skip, ssd_norm_w, w_out, w_gate_up, w_down, loss_target, m_norm1_w, m_norm2_w, m_w_ada, m_b_ada, m_w_in, m_q_a_norm_w, m_w_q_up, m_kv_a_norm_w, m_w_kv_up, m_q_nope_norm_w, m_q_pe_norm_w, m_k_nope_norm_w, m_k_pe_norm_w, m_conv_w, m_conv_b, m_dt_bias, m_a_log, m_d_skip, m_ssd_norm_w, m_w_out, m_w_gate_up, m_w_down, v_norm1_w, v_norm2_w, v_w_ada, v_b_ada, v_w_in, v_q_a_norm_w, v_w_q_up, v_kv_a_norm_w, v_w_kv_up, v_q_nope_norm_w, v_q_pe_norm_w, v_k_nope_norm_w, v_k_pe_norm_w, v_conv_w, v_conv_b, v_dt_bias, v_a_log, v_d_skip, v_ssd_norm_w, v_w_out, v_w_gate_up, v_w_down):
    W = dict(zip(WEIGHTS, (norm1_w, norm2_w, w_ada, b_ada, w_in, q_a_norm_w, w_q_up, kv_a_norm_w, w_kv_up, q_nope_norm_w, q_pe_norm_w, k_nope_norm_w, k_pe_norm_w, conv_w, conv_b, dt_bias, a_log, d_skip, ssd_norm_w, w_out, w_gate_up, w_down)))
    M = dict(zip(WEIGHTS, (m_norm1_w, m_norm2_w, m_w_ada, m_b_ada, m_w_in, m_q_a_norm_w, m_w_q_up, m_kv_a_norm_w, m_w_kv_up, m_q_nope_norm_w, m_q_pe_norm_w, m_k_nope_norm_w, m_k_pe_norm_w, m_conv_w, m_conv_b, m_dt_bias, m_a_log, m_d_skip, m_ssd_norm_w, m_w_out, m_w_gate_up, m_w_down)))
    V = dict(zip(WEIGHTS, (v_norm1_w, v_norm2_w, v_w_ada, v_b_ada, v_w_in, v_q_a_norm_w, v_w_q_up, v_kv_a_norm_w, v_w_kv_up, v_q_nope_norm_w, v_q_pe_norm_w, v_k_nope_norm_w, v_k_pe_norm_w, v_conv_w, v_conv_b, v_dt_bias, v_a_log, v_d_skip, v_ssd_norm_w, v_w_out, v_w_gate_up, v_w_down)))
    S = x.shape[1]
    xi, yi, ci = _coords()
    chip = 2 * xi + yi
    dev = 2 * chip + ci
    x0 = x[0]
    tgt = loss_target[0]

    inv_freq = 1.0 / (ROPE_THETA ** (jnp.arange(0, ROPE, 2, dtype=F32) / ROPE))
    ang = positions[0].astype(F32)[:, None] * inv_freq
    cos, sin = jnp.cos(ang), jnp.sin(ang)
    z16, z32, z64 = jnp.zeros((S, 16), F32), jnp.zeros((S, 32), F32), jnp.zeros((S, 64), F32)
    tab_c = jnp.concatenate([jnp.ones((S, 64), F32), cos, cos, z32], axis=1)
    tab_s1 = jnp.concatenate([z64, z16, sin, z32], axis=1)
    tab_s2 = jnp.concatenate([z64, -sin, z16, z32], axis=1)

    blk0 = jnp.concatenate([c.reshape(-1), W['conv_w'].reshape(-1)]).reshape(24, LANES)
    g0 = _ag8(blk0, "ag_c_conv").reshape(8, 24 * LANES)
    c_all = g0[:, :D_MODEL]
    conv_full = g0[0::2, D_MODEL:].reshape(4, DEPTH, CONV_TAPS, 256).transpose(1, 2, 0, 3).reshape(DEPTH, CONV_TAPS, D_CONV)

    sh = [{n: W[n][l].astype(BF16) for n in BIG} for l in range(DEPTH)]
    got_first = _ag_list([sh[0][n].reshape(2, sh[0][n].shape[0] // 2, sh[0][n].shape[1]) for n in FIRST], "ag_w0_first")
    to_operand = dict(w_in=lambda a: _pad_win(_shards_to_cols(a)), w_q_up=lambda a: _pad_wq(_shards_to_cols(a)),
                      w_kv_up=_shards_to_cols, w_out=lambda a: a.reshape(D_MODEL, D_MODEL), w_gate_up=lambda a: a,
                      w_down=lambda a: a.reshape(D_FF, D_MODEL))

    def layer_weights(names, gathered, own):
        return {n: to_operand[n](lax.dynamic_update_slice_in_dim(a.reshape(4, -1, a.shape[-1]), own[n][None], chip, axis=0))
                for n, a in zip(names, gathered)}

    LW = [layer_weights(FIRST, got_first, sh[0]), None]

    b_sh = lax.dynamic_slice_in_dim(W['b_ada'], chip * 1536, 1536, axis=1).reshape(DEPTH, 1, 1536)
    mod_sh = _ada_fwd(c_all, W['w_ada'], b_sh, "ada_fwd")
    g1 = _ag8(mod_sh.reshape(192, LANES), "ag_mod").reshape(8, DEPTH, 8, 1536)
    mod_all = g1[0::2].transpose(1, 2, 0, 3).reshape(DEPTH, 8, 6 * D_MODEL)
    mod = lax.dynamic_index_in_dim(mod_all, dev, axis=1, keepdims=False)
    mod, rest0 = lax.optimization_barrier((mod, [sh[0][n] for n in REST]))
    ag0 = _ag_direct_start(rest0, "ag_w0_rest_start")

    def mvec(l, k):
        return mod[l, k * D_MODEL:(k + 1) * D_MODEL].reshape(1, D_MODEL)

    def small(name, l, width=None):
        v = W[name][l]
        return _vec(v, width or v.shape[-1])

    def wq_vec(l):
        return _vec(jnp.concatenate([W['q_nope_norm_w'][l], W['q_pe_norm_w'][l]]))

    def wk_vec(l):
        return _vec(jnp.concatenate([W['k_nope_norm_w'][l], W['k_pe_norm_w'][l]]))

    def conv_vec(l):
        return jnp.concatenate([conv_full[l], W['conv_b'][l].reshape(1, D_CONV), jnp.zeros((3, D_CONV), F32)], axis=0)

    def ssd_vec(l):
        return jnp.concatenate([_vec(W['dt_bias'][l]), _vec(W['a_log'][l]), _vec(W['d_skip'][l]), jnp.zeros((5, LANES), F32)], axis=0)

    sv = []
    xcur = x0
    h1 = _row_fwd(fn_norm_mod, "norm_mod_f", [(x0, 0, D_MODEL)], [small('norm1_w', 0) + ag0[4][0, 0], mvec(0, 1), mvec(0, 0)],
                  [(D_MODEL, BF16)])[0]
    fin = None
    ag1 = None
    for l in range(DEPTH):
        if l == 1:
            LW[1] = layer_weights(BIG, _ag_direct_wait(ag1[0], ag1[1], ag1[2], ag1[3], xcur, "ag_w1_wait"), sh[1])
        lw = LW[l]
        t = dict(xcur=xcur, h1=h1)
        t['proj'] = proj = _mm(h1, lw['w_in'], 'nn', f"mm_in_{l}")
        t['qa_n'], t['kva_n'] = _row_fwd(fn_lat_norm, f"lat_norm_f{l}", [(proj, 0, 256), (proj, 2, 128)],
                                         [small('q_a_norm_w', l), small('kv_a_norm_w', l)], [(256, BF16), (128, BF16)])
        t['q'] = _mm(t['qa_n'], lw['w_q_up'], 'nn', f"mm_q_{l}")
        t['kv'] = _mm(t['kva_n'], lw['w_kv_up'], 'nn', f"mm_kv_{l}")
        t['qf'], t['kf'], t['vv'] = _row_fwd(
            fn_qk_prep, f"qk_prep_f{l}",
            [(t['q'], 0, 1024), (t['kv'], 0, 1024), (proj, 3, 128), (tab_c, 0, 128), (tab_s1, 0, 128), (tab_s2, 0, 128)],
            [wq_vec(l), wk_vec(l)], [(1024, BF16), (1024, BF16), (1024, BF16)])
        t['ao'], t['lse'] = _attn_fwd(t['qf'], t['kf'], t['vv'], f"attn_f{l}")
        t['xact'] = _conv_fwd(proj, conv_vec(l), f"conv_f{l}")
        t['y'], t['states'] = _ssd_fwd(t['xact'], proj, ssd_vec(l), f"ssd_f{l}")
        tie = 0.0
        if l == 0:
            t['ao'], t['y'] = lax.optimization_barrier((t['ao'], t['y']))
            rest = _ag_direct_wait(ag0[0], ag0[1], ag0[2], ag0[3], t['y'], "ag_w0_rest_wait")
            rest, sh1 = lax.optimization_barrier((list(rest), [sh[1][n] for n in BIG]))
            lw.update(layer_weights(REST, rest, sh[0]))
            ag1 = _ag_direct_start(sh1, "ag_w1_start")
            tie = ag1[4][0, 0]
        t['mix'] = _row_fwd(fn_gated_mix, f"gated_f{l}", [(t['y'], 0, 512), (proj, 1, 512), (t['ao'], 0, 512)],
                            [small('ssd_norm_w', l) + tie], [(1024, BF16)])[0]
        t['mo'] = _mm(t['mix'], lw['w_out'], 'nn', f"mm_out_{l}")
        t['x1'], t['h2'] = _row_fwd(fn_resid_norm, f"resid_mid_f{l}", [(xcur, 0, D_MODEL), (t['mo'], 0, D_MODEL)],
                                    [mvec(l, 2), small('norm2_w', l), mvec(l, 4), mvec(l, 3)],
                                    [(D_MODEL, F32), (D_MODEL, BF16)])
        t['gu'], t['act'] = _mm_gu_swiglu(t['h2'], lw['w_gate_up'], f"mm_gu_{l}")
        t['ff'] = _mm(t['act'], lw['w_down'], 'nn', f"mm_down_{l}")
        if l + 1 < DEPTH:
            xcur, h1 = _row_fwd(fn_resid_norm, f"resid_end_f{l}", [(t['x1'], 0, D_MODEL), (t['ff'], 0, D_MODEL)],
                                [mvec(l, 5), small('norm1_w', l + 1), mvec(l + 1, 1), mvec(l + 1, 0)],
                                [(D_MODEL, F32), (D_MODEL, BF16)])
        else:
            fin = _final(t['x1'], t['ff'], tgt, mvec(l, 5), "final_loss")
        sv.append(t)

    dx1, dff, dg2_last, loss_acc = fin
    gfull = {n: [None] * DEPTH for n in BIG}
    gsm = {n: [None] * DEPTH for (n, _) in SMALL}
    dmod = [[None] * 6 for _ in range(DEPTH)]
    dmod[DEPTH - 1][5] = dg2_last
    grad_x = None
    pending = []

    def rs_begin(l, names, tag):
        g4 = [gfull[n][l].reshape(4, 2, gfull[n][l].shape[1] // 2, gfull[n][l].shape[2]) for n in names]
        sib = _rs_sibling_list(g4, f"rs_sibling_{tag}")
        cs = _sum_sibling(g4, sib, ci, f"sum_sibling_{tag}")
        h = _rs_chips_start(cs, f"rs_chips_start_{tag}")
        pending.append((l, names, h))
        return h[4][0, 0]

    tie_l1 = tie_l0a = 0.0

    for l in reversed(range(DEPTH)):
        t = sv[l]
        lw = LW[l]
        proj = t['proj']
        dgu = _mm_down_dx_swiglu(dff, lw['w_down'], t['gu'], f"mm_down_dx{l}")
        gfull['w_down'][l] = _mm(t['act'], dff, 'tn', f"mm_down_dw{l}").reshape(4, D_FF // 4, D_MODEL)
        dh2 = _mm(dgu, lw['w_gate_up'], 'nt', f"mm_gu_dx{l}", stack='b')
        gfull['w_gate_up'][l] = _mm(t['h2'], dgu, 'tn', f"mm_gu_dw{l}", stack='out')
        if l == 0:
            tie_l0a = rs_begin(0, EARLY, "l0a")
        dxc, dmo, dmod[l][2], gsm['norm2_w'][l], dmod[l][4], dmod[l][3] = _row_bwd(
            fn_resid_norm, f"resid_mid_b{l}", [(t['xcur'], 0, D_MODEL), (t['mo'], 0, D_MODEL)],
            [mvec(l, 2) + (tie_l1 if l == 0 else 0.0), small('norm2_w', l), mvec(l, 4), mvec(l, 3)],
            [(dx1, 0, D_MODEL), (dh2, 0, D_MODEL)], [0, 1], [0, 1, 2, 3], ddtypes=[F32, BF16])
        dmix = _mm(dmo, lw['w_out'], 'nt', f"mm_out_dx{l}")
        gfull['w_out'][l] = _mm(t['mix'], dmo, 'tn', f"mm_out_dw{l}").reshape(4, D_MODEL // 4, D_MODEL)
        dy, dz, gsm['ssd_norm_w'][l] = _row_bwd(fn_gated_norm, f"gated_b{l}", [(t['y'], 0, 512), (proj, 1, 512)],
                                                [small('ssd_norm_w', l)], [(dmix, 1, 512)], [0, 1], [0])
        dxact, ddt, dsv = _ssd_bwd(t['xact'], proj, ssd_vec(l) + (tie_l0a if l == 0 else 0.0), t['states'], dy, f"ssd_b{l}")
        gsm['dt_bias'][l], gsm['a_log'][l], gsm['d_skip'][l] = dsv[0, :8], dsv[1, :8], dsv[2, :8]
        dxbc, dcv = _conv_bwd(proj, conv_vec(l), dxact, f"conv_b{l}")
        gsm['conv_w'][l] = dcv[:CONV_TAPS]
        gsm['conv_b'][l] = dcv[CONV_TAPS]
        delta_r = _attn_delta(dmix, t['ao'], f"attn_delta{l}")
        dqT, dkf, dvv = _attn_bwd(t['qf'], t['kf'], t['kf'].T, t['vv'], dmix, t['lse'], delta_r, f"attn_b{l}")
        dqf = dqT.T
        dq, dkv, dkpe, dwq, dwk = _row_bwd(
            fn_qk_prep, f"qk_prep_b{l}",
            [(t['q'], 0, 1024), (t['kv'], 0, 1024), (proj, 3, 128), (tab_c, 0, 128), (tab_s1, 0, 128), (tab_s2, 0, 128)],
            [wq_vec(l), wk_vec(l)], [(dqf, 0, 1024), (dkf, 0, 1024), (dvv, 0, 1024)], [0, 1, 2], [0, 1],
            ddtypes=[BF16, BF16, F32])
        gsm['q_nope_norm_w'][l], gsm['q_pe_norm_w'][l] = dwq[0, :NOPE], dwq[0, NOPE:QK_DIM]
        gsm['k_nope_norm_w'][l], gsm['k_pe_norm_w'][l] = dwk[0, :NOPE], dwk[0, NOPE:QK_DIM]
        dqa_n = _mm(dq, lw['w_q_up'], 'nt', f"mm_q_dx{l}")
        gfull['w_q_up'][l] = _cols_to_shards(_unpad_wq(_mm(t['qa_n'], dq, 'tn', f"mm_q_dw{l}")))
        dkva_n = _mm(dkv, lw['w_kv_up'], 'nt', f"mm_kv_dx{l}")
        gfull['w_kv_up'][l] = _cols_to_shards(_mm(t['kva_n'], dkv, 'tn', f"mm_kv_dw{l}"))
        dqa, dkva, dqw, dkvw = _row_bwd(fn_lat_norm, f"lat_norm_b{l}", [(proj, 0, 256), (proj, 2, 128)],
                                        [small('q_a_norm_w', l), small('kv_a_norm_w', l)],
                                        [(dqa_n, 0, 256), (dkva_n, 0, 128)], [0, 1], [0, 1])
        gsm['q_a_norm_w'][l], gsm['kv_a_norm_w'][l] = dqw[0], dkvw[0]
        dproj = jnp.concatenate([dqa, dkva, dkpe, dz, dxbc, ddt], axis=1).astype(BF16)
        dh1 = _mm(dproj, lw['w_in'], 'nt', f"mm_in_dx{l}")
        gfull['w_in'][l] = _cols_to_shards(_unpad_win(_mm(t['h1'], dproj, 'tn', f"mm_in_dw{l}")))
        if l > 0:
            p = sv[l - 1]
            dx1, dff, dmod[l - 1][5], gsm['norm1_w'][l], dmod[l][1], dmod[l][0] = _row_bwd(
                fn_resid_norm, f"resid_end_b{l - 1}", [(p['x1'], 0, D_MODEL), (p['ff'], 0, D_MODEL)],
                [mvec(l - 1, 5), small('norm1_w', l), mvec(l, 1), mvec(l, 0)], [(dxc, 0, D_MODEL), (dh1, 0, D_MODEL)],
                [0, 1], [0, 1, 2, 3], ddtypes=[F32, BF16])
            tie_l1 = rs_begin(l, BIG, f"l{l}")
        else:
            grad_x, gsm['norm1_w'][l], dmod[l][1], dmod[l][0] = _row_bwd(
                fn_norm_mod_pass, "norm_mod_b", [(x0, 0, D_MODEL)], [small('norm1_w', 0), mvec(0, 1), mvec(0, 0)],
                [(dxc, 0, D_MODEL), (dh1, 0, D_MODEL)], [0], [0, 1, 2])
        for n in ('norm1_w', 'norm2_w', 'ssd_norm_w'):
            gsm[n][l] = gsm[n][l][0]

    for l in range(DEPTH):
        gsm['b_ada'][l] = jnp.concatenate([d[0] for d in dmod[l]])
    sm_part = _pack_small({n: jnp.stack(v) for n, v in gsm.items()}).at[SMALL_ROWS - 1, 0].set(loss_acc[0, 0])
    sm_all = _ag8(sm_part, "ag_small")
    loss = jnp.sum(sm_all.reshape(8, SMALL_ROWS, LANES)[:, SMALL_ROWS - 1, 0])
    sm_all, late = lax.optimization_barrier((sm_all, [gfull[n][0] for n in BIG[2:]]))
    for n, g in zip(BIG[2:], late):
        gfull[n][0] = g
    tie_l0b = rs_begin(0, BIG[2:], "l0b")

    def with_conv(tree):
        wide = lax.dynamic_update_slice_in_dim(jnp.zeros((DEPTH, CONV_TAPS, D_CONV), F32), tree['conv_w'], chip * 256, axis=2)
        return {**tree, 'conv_w': wide}

    g_sm, d_sm, m_sm, v_sm = _adam(_pack_small(with_conv(W)) + tie_l0b, _pack_small(with_conv(M)), _pack_small(with_conv(V)),
                                   [(sm_all, d * SMALL_ROWS) for d in range(8)], "adam_small")
    out_small = [_unpack_small(b) for b in (g_sm, d_sm, m_sm, v_sm)]
    for o in out_small:
        o['conv_w'] = lax.dynamic_slice_in_dim(o['conv_w'].reshape(DEPTH, CONV_TAPS, D_CONV), chip * 256, 256, axis=2)

    dmod_all = sm_all.reshape(8, SMALL_ROWS * LANES)
    per_layer = sum(k + (-k % LANES) for (_, k) in SMALL)
    dmod_sh = jnp.stack([lax.dynamic_slice_in_dim(dmod_all[:, l * per_layer:l * per_layer + 6 * D_MODEL], chip * 1536, 1536, axis=1)
                         for l in range(DEPTH)])
    g_ada = _ada_bwd(c_all.T, dmod_sh, "ada_bwd")
    ada = _adam(W['w_ada'].reshape(DEPTH * D_MODEL, 1536), M['w_ada'].reshape(DEPTH * D_MODEL, 1536),
                V['w_ada'].reshape(DEPTH * D_MODEL, 1536), [(g_ada.reshape(DEPTH * D_MODEL, 1536), 0)], "adam_ada")
    out_ada = [a.reshape(DEPTH, D_MODEL, 1536) for a in ada]

    keys, cs_all, land_all = [], [], []
    for (l, names, (send_sems, recv_sems, cs_thru, land_thru, _)) in pending:
        cs, lands = _rs_chips_wait(send_sems, recv_sems, cs_thru, land_thru, ada[3], f"rs_chips_wait_l{l}{len(names)}")
        keys += [(l, n) for n in names]
        cs_all += list(cs)
        land_all += list(lands)
    ghalf = _sum_chips(cs_all, land_all, chip, "sum_chips")
    gother = _swap_list(ghalf, "swap_halves")
    gshard = {k: jnp.where(ci == 0, jnp.concatenate([a, b]), jnp.concatenate([b, a])) for k, a, b in zip(keys, ghalf, gother)}
    g_big = [jnp.stack([gshard[(l, n)] for l in range(DEPTH)]) for n in BIG]
    d_big, m_big, v_big = _adam_multi([W[n] for n in BIG], [M[n] for n in BIG], [V[n] for n in BIG], g_big, "adam_big")
    out_big = [dict(zip(BIG, o)) for o in (g_big, d_big, m_big, v_big)]

    outs = [loss, grad_x[None]]
    for k in range(4):
        for n in WEIGHTS:
            if n == 'w_ada':
                outs.append(out_ada[k])
            elif n in BIG:
                outs.append(out_big[k][n])
            else:
                outs.append(out_small[k][n])
    return tuple(outs)
```

```python
import functools

import jax
import jax.numpy as jnp
from jax import lax
from jax.experimental import pallas as pl
from jax.experimental.pallas import tpu as pltpu

F32 = jnp.float32
BF16 = jnp.bfloat16
MESH = pl.DeviceIdType.MESH

D_MODEL = 1024
DEPTH = 2
HEADS = 8
NOPE = 64
ROPE = 32
QK_DIM = NOPE + ROPE
Q_LORA = 256
KV_LORA = 128
SSD_HEADS = 8
SSD_P = 64
SSD_N = 128
CHUNK = 128
CONV_TAPS = 4
D_SSD = 512
D_CONV = 1024
D_FF = 2816
D_IN = 1960
D_IN_PAD = 2176
EPS = 1e-6
ROPE_THETA = 10000.0
ATT_SCALE = QK_DIM ** -0.5
NEG = -1e30
LANES = 128
VMEM_LIMIT = 48 * 1024 * 1024

ADAM_LR, ADAM_B1, ADAM_B2, ADAM_EPS, ADAM_WD, ADAM_STEP = 0.001, 0.9, 0.999, 1e-08, 0.01, 10

WEIGHTS = ['norm1_w', 'norm2_w', 'w_ada', 'b_ada', 'w_in', 'q_a_norm_w', 'w_q_up', 'kv_a_norm_w', 'w_kv_up',
           'q_nope_norm_w', 'q_pe_norm_w', 'k_nope_norm_w', 'k_pe_norm_w', 'conv_w', 'conv_b', 'dt_bias', 'a_log',
           'd_skip', 'ssd_norm_w', 'w_out', 'w_gate_up', 'w_down']
BIG = ['w_down', 'w_gate_up', 'w_out', 'w_kv_up', 'w_q_up', 'w_in']
EARLY = BIG[:2]
FIRST = BIG[3:]
REST = BIG[:3]
SMALL = [('b_ada', 6144), ('conv_w', 4096), ('norm1_w', 1024), ('norm2_w', 1024), ('conv_b', 1024), ('ssd_norm_w', 512),
         ('q_a_norm_w', 256), ('kv_a_norm_w', 128), ('q_nope_norm_w', 64), ('q_pe_norm_w', 32),
         ('k_nope_norm_w', 64), ('k_pe_norm_w', 32), ('dt_bias', 8), ('a_log', 8), ('d_skip', 8)]
SMALL_ROWS = 240


def _cp(sem=None, **kw):
    return pltpu.CompilerParams(dimension_semantics=sem, vmem_limit_bytes=VMEM_LIMIT, **kw)


def _dot(a, b, dims, prec=None):
    return lax.dot_general(a, b, (dims, ((), ())), preferred_element_type=F32, precision=prec)


def _tile(dim, target):
    best = 0
    for t in range(LANES, min(dim, target) + 1, LANES):
        if dim % t == 0:
            best = t
    if best < 256 and dim <= 2304:
        return dim
    return best


def _mm(a, b, mode, name, out_dtype=F32, stack=None):
    ns = None
    halves = (a if mode == 'nt' else b).ndim == 3 and stack is not None and not (stack == 'b' and mode == 'nn')
    if stack == 'b':
        ns = b.shape[2]
        if mode == 'nn':
            (M, K), N = a.shape, 4 * ns
        else:
            M, K, N = a.shape[-2], 4 * ns, b.shape[1]
    elif mode == 'nn':
        (M, K), (_, N) = a.shape, b.shape
    elif mode == 'nt':
        (M, K), (N, _) = a.shape, b.shape
    else:
        (K, M), N = a.shape, (2 * b.shape[2] if halves else b.shape[1])
    if stack == 'out':
        ns = N // 4
    tm, tn, tk = _tile(M, 1408 if mode == 'tn' else 1024), _tile(N, 1408), _tile(K, 1408)
    if stack == 'b' and mode == 'nt':
        tk = ns
    elif stack is not None:
        tn = ns
    nk = K // tk
    dims = {'nn': ((1,), (0,)), 'nt': ((1,), (1,)), 'tn': ((0,), (0,))}[mode]

    def body(a_ref, b_ref, o_ref, *acc):
        part = _dot(a_ref[...].astype(BF16), b_ref[...].astype(BF16), dims)
        if nk == 1:
            o_ref[...] = part.astype(o_ref.dtype)
            return
        k = pl.program_id(2)

        @pl.when(k == 0)
        def _():
            acc[0][...] = part

        @pl.when(k > 0)
        def _():
            acc[0][...] += part

        @pl.when(k == nk - 1)
        def _():
            o_ref[...] = acc[0][...].astype(o_ref.dtype)

    a_spec = pl.BlockSpec((tk, tm), lambda i, j, k: (k, i)) if mode == 'tn' else pl.BlockSpec((tm, tk), lambda i, j, k: (i, k))
    b_spec = pl.BlockSpec((tn, tk), lambda i, j, k: (j, k)) if mode == 'nt' else pl.BlockSpec((tk, tn), lambda i, j, k: (k, j))
    o_spec, o_shape = pl.BlockSpec((tm, tn), lambda i, j, k: (i, j)), (M, N)
    if stack == 'b':
        b_spec = (pl.BlockSpec((None, tn, ns), lambda i, j, k: (k, j, 0)) if mode == 'nt'
                  else pl.BlockSpec((None, tk, ns), lambda i, j, k: (j, k, 0)))
    if stack == 'out':
        o_spec, o_shape = pl.BlockSpec((None, tm, ns), lambda i, j, k: (j, i, 0)), (4, M, ns)
    if halves and mode == 'nt':
        a_spec = pl.BlockSpec((None, tm, ns), lambda i, j, k: (lax.div(k, 2), i, lax.rem(k, 2)))
    if halves and mode == 'tn':
        b_spec = pl.BlockSpec((None, tk, ns), lambda i, j, k: (lax.div(j, 2), k, lax.rem(j, 2)))
    return pl.pallas_call(
        body, name=name, grid=(M // tm, N // tn, nk),
        in_specs=[a_spec, b_spec], out_specs=o_spec,
        out_shape=jax.ShapeDtypeStruct(o_shape, out_dtype),
        scratch_shapes=[pltpu.VMEM((tm, tn), F32)] if nk > 1 else [],
        compiler_params=_cp(("parallel", "parallel", "arbitrary")),
    )(a, b)


def _mm_gu_swiglu(h, wst, name):
    S, K = h.shape
    ns = wst.shape[2]
    tm = _tile(S, 512)

    def body(a_ref, bg_ref, bu_ref, gu_ref, act_ref):
        a = a_ref[...]
        g = _dot(a, bg_ref[...], ((1,), (0,)))
        u = _dot(a, bu_ref[...], ((1,), (0,)))
        gu_ref[0] = g
        gu_ref[1] = u
        act_ref[...] = (g * jax.nn.sigmoid(g) * u).astype(act_ref.dtype)

    return pl.pallas_call(
        body, name=name, grid=(S // tm, 2),
        in_specs=[pl.BlockSpec((tm, K), lambda i, j: (i, 0)), pl.BlockSpec((None, K, ns), lambda i, j: (j, 0, 0)),
                  pl.BlockSpec((None, K, ns), lambda i, j: (j + 2, 0, 0))],
        out_specs=[pl.BlockSpec((2, tm, ns), lambda i, j: (0, i, j)), pl.BlockSpec((tm, ns), lambda i, j: (i, j))],
        out_shape=[jax.ShapeDtypeStruct((2, S, 2 * ns), F32), jax.ShapeDtypeStruct((S, 2 * ns), BF16)],
        compiler_params=_cp(("parallel", "parallel")),
    )(h, wst, wst)


def _mm_down_dx_swiglu(dff, w_down, gu, name):
    S, K = dff.shape
    tm, tn = _tile(S, 512), _tile(D_FF, 1408)

    def body(a_ref, b_ref, g_ref, u_ref, o_ref):
        dact = _dot(a_ref[...].astype(BF16), b_ref[...], ((1,), (1,)))
        g, u = g_ref[...], u_ref[...]
        sg = jax.nn.sigmoid(g)
        o_ref[0] = (dact * u * (sg * (1.0 + g * (1.0 - sg)))).astype(o_ref.dtype)
        o_ref[1] = (dact * (g * sg)).astype(o_ref.dtype)

    return pl.pallas_call(
        body, name=name, grid=(S // tm, D_FF // tn),
        in_specs=[pl.BlockSpec((tm, K), lambda i, j: (i, 0)), pl.BlockSpec((tn, K), lambda i, j: (j, 0)),
                  pl.BlockSpec((None, tm, tn), lambda i, j: (0, i, j)), pl.BlockSpec((None, tm, tn), lambda i, j: (1, i, j))],
        out_specs=pl.BlockSpec((2, tm, tn), lambda i, j: (0, i, j)),
        out_shape=jax.ShapeDtypeStruct((2, S, D_FF), BF16),
        compiler_params=_cp(("parallel", "parallel")),
    )(dff, w_down, gu, gu)


def _rspec(tm, w, cb):
    return pl.BlockSpec((tm, w), lambda i: (i, cb))


def _vspec(shape):
    return pl.BlockSpec(shape, lambda i: (0,) * len(shape))


def _row_fwd(fn, name, rows, vecs, outs, tm=256, transposed=()):
    S = rows[0][0].shape[0]
    tm = min(tm, S)
    nin = len(rows) + len(vecs)

    def body(*refs):
        res = fn(*[r[...] for r in refs[:nin]])
        for k, (o_ref, r) in enumerate(zip(refs[nin:], res)):
            o_ref[...] = (r.T if k in transposed else r).astype(o_ref.dtype)

    return pl.pallas_call(
        body, name=name, grid=(S // tm,),
        in_specs=[_rspec(tm, w, cb) for (_, cb, w) in rows] + [_vspec(v.shape) for v in vecs],
        out_specs=[pl.BlockSpec((w, tm), lambda i: (0, i)) if k in transposed else _rspec(tm, w, 0)
                   for k, (w, _) in enumerate(outs)],
        out_shape=[jax.ShapeDtypeStruct((w, S) if k in transposed else (S, w), dt) for k, (w, dt) in enumerate(outs)],
        compiler_params=_cp(("parallel",)),
    )(*[r[0] for r in rows], *vecs)


def _row_bwd(fn, name, rows, vecs, cts, drows, dvecs, tm=256, ddtypes=None, transposed=()):
    S = rows[0][0].shape[0]
    ddtypes = ddtypes or [F32] * len(drows)
    tm = min(tm, S)
    nr, nv, nc = len(rows), len(vecs), len(cts)
    didx = list(drows) + [nr + j for j in dvecs]

    def body(*refs):
        vals = [r[...] for r in refs[:nr + nv]]
        ct = tuple((r[...].T if k in transposed else r[...]).astype(F32)
                   for k, r in enumerate(refs[nr + nv:nr + nv + nc]))
        outs = refs[nr + nv + nc:]

        def g(*d):
            a = list(vals)
            for k, val in zip(didx, d):
                a[k] = val
            return tuple(fn(*a))

        _, vjp = jax.vjp(g, *[vals[k] for k in didx])
        grads = vjp(ct)
        for o, gr in zip(outs[:len(drows)], grads[:len(drows)]):
            o[...] = gr.astype(o.dtype)

        @pl.when(pl.program_id(0) == 0)
        def _():
            for o in outs[len(drows):]:
                o[...] = jnp.zeros_like(o)

        for o, gr in zip(outs[len(drows):], grads[len(drows):]):
            o[...] += gr

    return pl.pallas_call(
        body, name=name, grid=(S // tm,),
        in_specs=[_rspec(tm, w, cb) for (_, cb, w) in rows] + [_vspec(v.shape) for v in vecs]
        + [pl.BlockSpec((w, tm), lambda i: (0, i)) if k in transposed else _rspec(tm, w, cb) for k, (_, cb, w) in enumerate(cts)],
        out_specs=[_rspec(tm, rows[k][2], 0) for k in drows] + [_vspec(vecs[j].shape) for j in dvecs],
        out_shape=[jax.ShapeDtypeStruct((S, rows[k][2]), dt) for k, dt in zip(drows, ddtypes)]
        + [jax.ShapeDtypeStruct(vecs[j].shape, F32) for j in dvecs],
        compiler_params=_cp(("arbitrary",)),
    )(*[r[0] for r in rows], *vecs, *[c[0] for c in cts])


def _rms(x):
    return x * lax.rsqrt(jnp.mean(x * x, axis=-1, keepdims=True) + EPS)


def fn_norm_mod(x, nw, sc, sh):
    return (_rms(x) * nw * (1.0 + sc) + sh,)


def fn_norm_mod_pass(x, nw, sc, sh):
    return (x, _rms(x) * nw * (1.0 + sc) + sh)


def fn_resid_norm(x, d, g, nw, sc, sh):
    xn = x + g * d
    return (xn, _rms(xn) * nw * (1.0 + sc) + sh)


def fn_lat_norm(qa, kva, qw, kvw):
    return (_rms(qa) * qw, _rms(kva) * kvw)


@functools.partial(jax.custom_vjp, nondiff_argnums=(1,))
def _lroll(x, s):
    return pltpu.roll(x, s, 1)


def _lroll_fwd(x, s):
    return pltpu.roll(x, s, 1), None


def _lroll_bwd(s, _, g):
    return (pltpu.roll(g, (LANES - s) % LANES, 1),)


_lroll.defvjp(_lroll_fwd, _lroll_bwd)


def _lane_masks(shape):
    lane = lax.broadcasted_iota(jnp.int32, shape, 1)
    return (lane < NOPE).astype(F32), ((lane >= NOPE) & (lane < QK_DIM)).astype(F32)


def _rope(t, tc, ts1, ts2):
    return t * tc + _lroll(t, 16) * ts1 + _lroll(t, LANES - 16) * ts2


def fn_qk_prep(q, kv, kpe, tc, ts1, ts2, wq, wk):
    mn, mp = _lane_masks((1, LANES))
    mhi = 1.0 - mn

    def head_norm(t, w):
        rn = lax.rsqrt(jnp.sum(t * t * mn, axis=-1, keepdims=True) * (1.0 / NOPE) + EPS)
        rp = lax.rsqrt(jnp.sum(t * t * mp, axis=-1, keepdims=True) * (1.0 / ROPE) + EPS)
        return t * (rn * mn + rp * mp) * w

    kp = _rope(head_norm(_lroll(kpe, NOPE), wk) * mp, tc, ts1, ts2)
    qs, ks, vs = [], [], []
    for h in range(HEADS):
        qs.append(_rope(head_norm(q[:, h * LANES:(h + 1) * LANES], wq), tc, ts1, ts2))
        t = kv[:, h * LANES:(h + 1) * LANES]
        ks.append(head_norm(t, wk) * mn + kp)
        vs.append(_lroll(t, NOPE) * mn + mhi)
    return (jnp.concatenate(qs, axis=1), jnp.concatenate(ks, axis=1), jnp.concatenate(vs, axis=1))


def fn_qk_prep_kt(*args):
    qf, kf, va = fn_qk_prep(*args)
    return (qf, kf, va, kf)


def fn_gated_norm(y, z, w):
    u = y * jax.nn.silu(z)
    half = D_SSD // 2
    return (jnp.concatenate([_rms(u[:, :half]), _rms(u[:, half:])], axis=1) * w,)


def fn_gated_mix(y, z, ao, w):
    return (jnp.concatenate([ao, fn_gated_norm(y, z, w)[0]], axis=1),)


def _final(x1, ff, tgt, g2, name):
    S = x1.shape[0]
    tm = min(256, S)

    def body(x_ref, f_ref, t_ref, g_ref, dx_ref, df_ref, dg_ref, l_ref):
        @pl.when(pl.program_id(0) == 0)
        def _():
            dg_ref[...] = jnp.zeros_like(dg_ref)
            l_ref[...] = jnp.zeros_like(l_ref)

        f = f_ref[...]
        g = g_ref[...]
        e = x_ref[...] + g * f - t_ref[...]
        dx = e * (1.0 / D_MODEL)
        dx_ref[...] = dx
        df_ref[...] = (g * dx).astype(df_ref.dtype)
        dg_ref[...] += jnp.sum(dx * f, axis=0, keepdims=True)
        l_ref[...] += jnp.sum(e * e) * (0.5 / D_MODEL)

    r = _rspec(tm, D_MODEL, 0)
    return pl.pallas_call(
        body, name=name, grid=(S // tm,),
        in_specs=[r, r, r, _vspec((1, D_MODEL))],
        out_specs=[r, r, _vspec((1, D_MODEL)), _vspec((1, LANES))],
        out_shape=[jax.ShapeDtypeStruct((S, D_MODEL), F32), jax.ShapeDtypeStruct((S, D_MODEL), BF16),
                   jax.ShapeDtypeStruct((1, D_MODEL), F32), jax.ShapeDtypeStruct((1, LANES), F32)],
        compiler_params=_cp(("arbitrary",)),
    )(x1, ff, tgt, g2)


def _causal_mask(t):
    r = lax.broadcasted_iota(jnp.int32, (t, t), 0)
    c = lax.broadcasted_iota(jnp.int32, (t, t), 1)
    return c <= r


LOG2E = 1.4426950408889634
EXP2_SCALE = ATT_SCALE * LOG2E
ATT_TQ, ATT_TK = 512, 1024
ATT_BQ, ATT_BK = 1024, 512


def _attn_fwd(qf, kf, va, name):
    S = qf.shape[0]
    T, TK = min(ATT_TQ, S), min(ATT_TK, S)
    nmask = max(1, T // TK)

    def body(q_ref, k_ref, v_ref, o_ref, l_ref):
        i = pl.program_id(1)
        r = lax.broadcasted_iota(jnp.int32, (T, TK), 0)
        c = lax.broadcasted_iota(jnp.int32, (T, TK), 1)
        qs = [q_ref[:, hh * LANES:(hh + 1) * LANES] for hh in range(2)]

        def blk(j, carry, masked):
            off = pl.multiple_of(j * TK, TK)
            out = []
            for hh in range(2):
                m, acc = carry[hh]
                s = _dot(qs[hh], k_ref[pl.ds(off, TK), hh * LANES:(hh + 1) * LANES], ((1,), (1,)))
                if masked:
                    s = jnp.where(c + j * TK <= r + i * T, s, NEG)
                mn = jnp.maximum(m, jnp.max(s, axis=1, keepdims=True))
                p = jnp.exp2((s - mn) * EXP2_SCALE)
                al = jnp.exp2((m - mn) * EXP2_SCALE)
                vj = v_ref[pl.ds(off, TK), hh * LANES:(hh + 1) * LANES]
                out.append((mn, al * acc + _dot(p.astype(BF16), vj, ((1,), (0,)))))
            return tuple(out)

        one = (jnp.full((T, 1), NEG, F32), jnp.zeros((T, LANES), F32))
        nfull = lax.div(i * T, TK)
        carry = lax.fori_loop(0, nfull, lambda j, cr: blk(j, cr, False), (one, one))
        for t in range(nmask):
            carry = blk(nfull + t, carry, True)
        lane = lax.broadcasted_iota(jnp.int32, (1, LANES), 1)
        z = jnp.zeros((T, LANES), F32)
        for hh in range(2):
            m, acc = carry[hh]
            l = acc[:, 64:65]
            o_ref[:, hh * 64:(hh + 1) * 64] = (acc / l)[:, :64]
            z = z + (m * EXP2_SCALE + jnp.log(l) * LOG2E) * (lane == hh).astype(F32)
        l_ref[0] = z.T[0:2, :]

    return pl.pallas_call(
        body, name=name, grid=(HEADS // 2, S // T),
        in_specs=[pl.BlockSpec((T, 256), lambda h, i: (i, h)), pl.BlockSpec((S, 256), lambda h, i: (0, h)),
                  pl.BlockSpec((S, 256), lambda h, i: (0, h))],
        out_specs=[pl.BlockSpec((T, LANES), lambda h, i: (i, h)), pl.BlockSpec((1, 2, T), lambda h, i: (h, 0, i))],
        out_shape=[jax.ShapeDtypeStruct((S, D_SSD), F32), jax.ShapeDtypeStruct((HEADS // 2, 2, S), F32)],
        compiler_params=_cp(("parallel", "parallel")),
    )(qf, kf, va)


def _attn_delta(dmix, ao, name):
    S = ao.shape[0]
    tm = min(512, S)

    def body(d_ref, o_ref, out_ref):
        lane = lax.broadcasted_iota(jnp.int32, (1, LANES), 1)
        lo = (lane < 64).astype(F32)
        for hp in range(HEADS // 2):
            y = d_ref[:, hp * LANES:(hp + 1) * LANES] * o_ref[:, hp * LANES:(hp + 1) * LANES]
            z = (jnp.sum(y * lo, axis=1, keepdims=True) * (lane == 0).astype(F32)
                 + jnp.sum(y * (1.0 - lo), axis=1, keepdims=True) * (lane == 1).astype(F32))
            out_ref[hp] = z.T[0:2, :]

    return pl.pallas_call(
        body, name=name, grid=(S // tm,),
        in_specs=[pl.BlockSpec((tm, D_SSD), lambda i: (i, 0)), pl.BlockSpec((tm, D_SSD), lambda i: (i, 0))],
        out_specs=pl.BlockSpec((HEADS // 2, 2, tm), lambda i: (0, 0, i)),
        out_shape=jax.ShapeDtypeStruct((HEADS // 2, 2, S), F32),
        compiler_params=_cp(("parallel",)),
    )(dmix, ao)


def _attn_bwd(qf, kf, kT, va, do, lse_r, delta_r, name):
    S = qf.shape[0]
    T, TK = min(ATT_BQ, S), min(ATT_BK, S)
    nq = S // T
    nmask = max(1, TK // T)

    def body(q_ref, k_ref, kT_ref, v_ref, do_ref, l_ref, d_ref, dqT_ref, dk_ref, dv_ref):
        j = pl.program_id(1)

        @pl.when(j == 0)
        def _():
            dqT_ref[...] = jnp.zeros_like(dqT_ref)

        r = lax.broadcasted_iota(jnp.int32, (TK, T), 0)
        c = lax.broadcasted_iota(jnp.int32, (TK, T), 1)
        lo = (lax.broadcasted_iota(jnp.int32, (1, LANES), 1) < 64).astype(F32)
        ks = [k_ref[:, hh * LANES:(hh + 1) * LANES] for hh in range(2)]
        vs = [v_ref[:, hh * LANES:(hh + 1) * LANES] for hh in range(2)]
        kTs = [kT_ref[hh * LANES:(hh + 1) * LANES, :] for hh in range(2)]

        def blk(i, carry, masked):
            off = pl.multiple_of(i * T, T)
            dall = do_ref[pl.ds(off, T), :]
            out = []
            for hh in range(2):
                dk, dv = carry[hh]
                q = q_ref[pl.ds(off, T), hh * LANES:(hh + 1) * LANES]
                dop = ((dall if hh == 0 else pltpu.roll(dall, 64, 1)) * lo).astype(BF16)
                lrow = l_ref[0, hh:hh + 1, pl.ds(off, T)]
                drow = d_ref[0, hh:hh + 1, pl.ds(off, T)]
                pT = jnp.exp2(_dot(ks[hh], q, ((1,), (1,))) * EXP2_SCALE - lrow)
                if masked:
                    pT = jnp.where(r + j * TK <= c + i * T, pT, 0.0)
                dpT = _dot(vs[hh], dop, ((1,), (1,)))
                dsT = (pT * (dpT - drow) * ATT_SCALE).astype(BF16)
                dv = dv + _dot(pT.astype(BF16), dop, ((1,), (0,)))
                dk = dk + _dot(dsT, q, ((1,), (0,)))
                dqT_ref[hh * LANES:(hh + 1) * LANES, pl.ds(off, T)] += _dot(kTs[hh], dsT, ((1,), (0,)))
                out.append((dk, dv))
            return tuple(out)

        z = (jnp.zeros((TK, LANES), F32), jnp.zeros((TK, LANES), F32))
        first = lax.div(j * TK, T)
        carry = (z, z)
        for t in range(nmask):
            carry = blk(first + t, carry, True)
        carry = lax.fori_loop(first + nmask, nq, lambda i, cr: blk(i, cr, False), carry)
        for hh in range(2):
            dk_ref[:, hh * LANES:(hh + 1) * LANES] = carry[hh][0]
            dv_ref[:, hh * LANES:(hh + 1) * LANES] = carry[hh][1]

    return pl.pallas_call(
        body, name=name, grid=(HEADS // 2, S // TK),
        in_specs=[pl.BlockSpec((S, 256), lambda h, j: (0, h)), pl.BlockSpec((TK, 256), lambda h, j: (j, h)),
                  pl.BlockSpec((256, TK), lambda h, j: (h, j)), pl.BlockSpec((TK, 256), lambda h, j: (j, h)),
                  pl.BlockSpec((S, LANES), lambda h, j: (0, h)), pl.BlockSpec((1, 2, S), lambda h, j: (h, 0, 0)),
                  pl.BlockSpec((1, 2, S), lambda h, j: (h, 0, 0))],
        out_specs=[pl.BlockSpec((256, S), lambda h, j: (h, 0)), pl.BlockSpec((TK, 256), lambda h, j: (j, h)),
                   pl.BlockSpec((TK, 256), lambda h, j: (j, h))],
        out_shape=[jax.ShapeDtypeStruct((D_MODEL, S), F32), jax.ShapeDtypeStruct((S, D_MODEL), F32),
                   jax.ShapeDtypeStruct((S, D_MODEL), F32)],
        compiler_params=_cp(("parallel", "arbitrary")),
    )(qf, kf, kT, va, do, lse_r, delta_r)


def _shift_down(x, s):
    if s == 0:
        return x
    rows = lax.broadcasted_iota(jnp.int32, x.shape, 0)
    return jnp.where(rows >= s, pltpu.roll(x, s, 0), 0.0)


def _shift_up(x, s):
    if s == 0:
        return x
    n = x.shape[0]
    rows = lax.broadcasted_iota(jnp.int32, x.shape, 0)
    return jnp.where(rows < n - s, pltpu.roll(x, n - s, 0), 0.0)


def _conv_fwd(proj, cvec, name):
    S = proj.shape[0]

    def body(x_ref, c_ref, o_ref):
        x = x_ref[...]
        y = jnp.broadcast_to(c_ref[4:5, :], x.shape)
        for k in range(CONV_TAPS):
            y = y + c_ref[k:k + 1, :] * _shift_down(x, CONV_TAPS - 1 - k)
        o_ref[...] = y * jax.nn.sigmoid(y)

    return pl.pallas_call(
        body, name=name, grid=(D_CONV // LANES,),
        in_specs=[pl.BlockSpec((S, LANES), lambda j: (0, 8 + j)), pl.BlockSpec((8, LANES), lambda j: (0, j))],
        out_specs=pl.BlockSpec((S, LANES), lambda j: (0, j)),
        out_shape=jax.ShapeDtypeStruct((S, D_CONV), F32),
        compiler_params=_cp(("parallel",)),
    )(proj, cvec)


def _conv_bwd(proj, cvec, dact, name):
    S = proj.shape[0]

    def body(x_ref, c_ref, d_ref, dx_ref, dc_ref):
        x = x_ref[...]
        y = jnp.broadcast_to(c_ref[4:5, :], x.shape)
        for k in range(CONV_TAPS):
            y = y + c_ref[k:k + 1, :] * _shift_down(x, CONV_TAPS - 1 - k)
        sg = jax.nn.sigmoid(y)
        dy = d_ref[...] * (sg * (1.0 + y * (1.0 - sg)))
        dx = jnp.zeros_like(x)
        for k in range(CONV_TAPS):
            s = CONV_TAPS - 1 - k
            dx = dx + c_ref[k:k + 1, :] * _shift_up(dy, s)
            dc_ref[k:k + 1, :] = jnp.sum(dy * _shift_down(x, s), axis=0, keepdims=True)
        dx_ref[...] = dx
        dc_ref[4:5, :] = jnp.sum(dy, axis=0, keepdims=True)
        dc_ref[5:8, :] = jnp.zeros((3, LANES), F32)

    return pl.pallas_call(
        body, name=name, grid=(D_CONV // LANES,),
        in_specs=[pl.BlockSpec((S, LANES), lambda j: (0, 8 + j)), pl.BlockSpec((8, LANES), lambda j: (0, j)),
                  pl.BlockSpec((S, LANES), lambda j: (0, j))],
        out_specs=[pl.BlockSpec((S, LANES), lambda j: (0, j)), pl.BlockSpec((8, LANES), lambda j: (0, j))],
        out_shape=[jax.ShapeDtypeStruct((S, D_CONV), F32), jax.ShapeDtypeStruct((8, D_CONV), F32)],
        compiler_params=_cp(("parallel",)),
    )(proj, cvec, dact)


def fn_ssd_chunk(xs, bm, cm, dtr, state, vecs):
    Q = CHUNK
    dt = jax.nn.softplus(dtr + vecs[0:1])
    a = -jnp.exp(vecs[1:2])
    adt = dt * a
    tril = _causal_mask(Q)
    acs = _dot(tril.astype(F32), adt, ((1,), (0,)), lax.Precision.HIGHEST)
    acs_t = acs.T
    alast = acs[Q - 1:Q, :]
    r = lax.broadcasted_iota(jnp.int32, (LANES, D_SSD), 0)
    c = lax.broadcasted_iota(jnp.int32, (LANES, D_SSD), 1)
    spread = (lax.shift_right_logical(c, 6) == r).astype(F32)

    def per_head(v):
        return _dot(v, spread, ((1,), (0,)), lax.Precision.HIGH)

    xdt = xs * per_head(dt)
    ub = (xdt * per_head(jnp.exp(alast - acs))).astype(BF16)
    xdtb = xdt.astype(BF16)
    Bs = [bm[:, g * SSD_N:(g + 1) * SSD_N].astype(BF16) for g in range(2)]
    Cs = [cm[:, g * SSD_N:(g + 1) * SSD_N].astype(BF16) for g in range(2)]
    Gs = [_dot(Cs[g], Bs[g], ((1,), (1,))) for g in range(2)]
    yds, yos, adds = [], [], []
    for h in range(SSD_HEADS):
        g = h // (SSD_HEADS // 2)
        sl = slice(h * SSD_P, (h + 1) * SSD_P)
        L = jnp.exp(jnp.where(tril, acs[:, h:h + 1] - acs_t[h:h + 1, :], -jnp.inf))
        yds.append(_dot((Gs[g] * L).astype(BF16), xdtb[:, sl], ((1,), (0,))))
        yos.append(_dot(Cs[g], state[h].astype(BF16), ((1,), (1,))))
        adds.append(_dot(ub[:, sl], Bs[g], ((0,), (0,))))
    y = jnp.concatenate(yds, axis=1) + jnp.concatenate(yos, axis=1) * per_head(jnp.exp(acs)) + per_head(vecs[2:3]) * xs
    decay = jnp.stack([jnp.broadcast_to(jnp.exp(alast[:, h:h + 1]), (SSD_P, SSD_N)) for h in range(SSD_HEADS)])
    return y, jnp.stack(adds) + state * decay


def _ssd_fwd(xact, proj, svec, name):
    S = xact.shape[0]
    nc = S // CHUNK

    def body(x_ref, dt_ref, v_ref, y_ref, st_ref, state):
        @pl.when(pl.program_id(0) == 0)
        def _():
            state[...] = jnp.zeros_like(state)

        st_ref[0] = state[...]
        x = x_ref[...]
        y, sn = fn_ssd_chunk(x[:, 0:512], x[:, 512:768], x[:, 768:1024], dt_ref[...], state[...], v_ref[...])
        y_ref[...] = y
        state[...] = sn

    return pl.pallas_call(
        body, name=name, grid=(nc,),
        in_specs=[pl.BlockSpec((CHUNK, D_CONV), lambda i: (i, 0)), pl.BlockSpec((CHUNK, LANES), lambda i: (i, 16)),
                  pl.BlockSpec((8, LANES), lambda i: (0, 0))],
        out_specs=[pl.BlockSpec((CHUNK, D_SSD), lambda i: (i, 0)),
                   pl.BlockSpec((1, SSD_HEADS, SSD_P, SSD_N), lambda i: (i, 0, 0, 0))],
        out_shape=[jax.ShapeDtypeStruct((S, D_SSD), F32), jax.ShapeDtypeStruct((nc, SSD_HEADS, SSD_P, SSD_N), F32)],
        scratch_shapes=[pltpu.VMEM((SSD_HEADS, SSD_P, SSD_N), F32)],
        compiler_params=_cp(("arbitrary",)),
    )(xact, proj, svec)


def _ssd_bwd(xact, proj, svec, states, dy, name):
    S = xact.shape[0]
    nc = S // CHUNK

    def body(x_ref, dt_ref, v_ref, st_ref, dy_ref, dx_ref, ddt_ref, dv_ref, dstate):
        @pl.when(pl.program_id(0) == 0)
        def _():
            dstate[...] = jnp.zeros_like(dstate)
            dv_ref[...] = jnp.zeros_like(dv_ref)

        x = x_ref[...]
        _, vjp = jax.vjp(fn_ssd_chunk, x[:, 0:512], x[:, 512:768], x[:, 768:1024], dt_ref[...], st_ref[0], v_ref[...])
        dxs, dbm, dcm, ddt, dst, dvec = vjp((dy_ref[...], dstate[...]))
        dx_ref[:, 0:512] = dxs
        dx_ref[:, 512:768] = dbm
        dx_ref[:, 768:1024] = dcm
        ddt_ref[...] = ddt
        dstate[...] = dst
        dv_ref[...] += dvec

    rev = lambda i: (nc - 1 - i, 0)
    return pl.pallas_call(
        body, name=name, grid=(nc,),
        in_specs=[pl.BlockSpec((CHUNK, D_CONV), rev), pl.BlockSpec((CHUNK, LANES), lambda i: (nc - 1 - i, 16)),
                  pl.BlockSpec((8, LANES), lambda i: (0, 0)),
                  pl.BlockSpec((1, SSD_HEADS, SSD_P, SSD_N), lambda i: (nc - 1 - i, 0, 0, 0)),
                  pl.BlockSpec((CHUNK, D_SSD), rev)],
        out_specs=[pl.BlockSpec((CHUNK, D_CONV), rev), pl.BlockSpec((CHUNK, LANES), rev),
                   pl.BlockSpec((8, LANES), lambda i: (0, 0))],
        out_shape=[jax.ShapeDtypeStruct((S, D_CONV), F32), jax.ShapeDtypeStruct((S, LANES), F32),
                   jax.ShapeDtypeStruct((8, LANES), F32)],
        scratch_shapes=[pltpu.VMEM((SSD_HEADS, SSD_P, SSD_N), F32)],
        compiler_params=_cp(("arbitrary",)),
    )(xact, proj, svec, states, dy)


def _ada_fwd(c_all, w_ada, b_sh, name):
    nb = 1536 // 512

    def body(c_ref, w_ref, b_ref, o_ref):
        ca = jax.nn.silu(c_ref[...]).astype(BF16)
        o_ref[0] = _dot(ca, w_ref[0].astype(BF16), ((1,), (0,))) + b_ref[0]

    return pl.pallas_call(
        body, name=name, grid=(DEPTH, nb),
        in_specs=[pl.BlockSpec((8, D_MODEL), lambda l, j: (0, 0)), pl.BlockSpec((1, D_MODEL, 512), lambda l, j: (l, 0, j)),
                  pl.BlockSpec((1, 1, 512), lambda l, j: (l, 0, j))],
        out_specs=pl.BlockSpec((1, 8, 512), lambda l, j: (l, 0, j)),
        out_shape=jax.ShapeDtypeStruct((DEPTH, 8, 1536), F32),
        compiler_params=_cp(("parallel", "parallel")),
    )(c_all, w_ada, b_sh)


def _ada_bwd(c_all_t, dmod_sh, name):
    nb = 1536 // 512

    def body(c_ref, d_ref, o_ref):
        ca = jax.nn.silu(c_ref[...])
        acc = ca[:, 0:1] * d_ref[0, 0:1, :]
        for b in range(1, 8):
            acc = acc + ca[:, b:b + 1] * d_ref[0, b:b + 1, :]
        o_ref[0] = acc

    return pl.pallas_call(
        body, name=name, grid=(DEPTH, nb),
        in_specs=[pl.BlockSpec((D_MODEL, 8), lambda l, j: (0, 0)), pl.BlockSpec((1, 8, 512), lambda l, j: (l, 0, j))],
        out_specs=pl.BlockSpec((1, D_MODEL, 512), lambda l, j: (l, 0, j)),
        out_shape=jax.ShapeDtypeStruct((DEPTH, D_MODEL, 1536), F32),
        compiler_params=_cp(("parallel", "parallel")),
    )(c_all_t, dmod_sh)


def _rows_tile(rows):
    return next(t for t in (512, 256, 128, 64, 32, 16, 8) if rows % t == 0)


SUM_BLOCKS = 4
ADAM_BLOCKS = 8


def _sum_sibling(gs, ls, ci, name):
    n = len(gs)

    def body(c_ref, *refs):
        for p in range(n):
            refs[2 * n + p][...] = refs[2 * p][...] + refs[2 * p + 1][...]

    in_specs, out_specs, out_shape = [], [], []
    for g in gs:
        _, _, rh, cw = g.shape
        rb = rh // SUM_BLOCKS
        in_specs += [pl.BlockSpec((None, None, rb, cw), lambda s, i, c: (s, c[0], i, 0)),
                     pl.BlockSpec((None, rb, cw), lambda s, i, c: (s, i, 0))]
        out_specs.append(pl.BlockSpec((None, rb, cw), lambda s, i, c: (s, i, 0)))
        out_shape.append(jax.ShapeDtypeStruct((4, rh, cw), F32))
    ops = [a for pair in zip(gs, ls) for a in pair]
    return pl.pallas_call(
        body, name=name,
        grid_spec=pltpu.PrefetchScalarGridSpec(num_scalar_prefetch=1, grid=(4, SUM_BLOCKS), in_specs=in_specs, out_specs=out_specs),
        out_shape=out_shape, compiler_params=_cp(("parallel", "parallel")),
    )(ci.reshape(1).astype(jnp.int32), *ops)


def _sum_chips(cs, lands, chip, ci, name):
    n = len(cs)

    def body(c_ref, *refs):
        for p in range(n):
            a = refs[4 * p:4 * p + 4]
            refs[4 * n + p][...] = ((a[0][...] + a[1][...]) + a[2][...]) + a[3][...]

    in_specs, out_specs, out_shape = [], [], []
    for c in cs:
        _, rh, cw = c.shape
        rb = rh // SUM_BLOCKS
        in_specs.append(pl.BlockSpec((None, rb, cw), lambda i, ch: (ch[0], i, 0)))
        in_specs += [pl.BlockSpec((None, rb, cw), functools.partial(lambda i, ch, k: (k, i, 0), k=k)) for k in range(3)]
        out_specs.append(pl.BlockSpec((None, rb, cw), lambda i, ch: (ch[1], i, 0)))
        out_shape.append(jax.ShapeDtypeStruct((2, rh, cw), F32))
    ops = [a for c, l in zip(cs, lands) for a in (c, l, l, l)]
    return pl.pallas_call(
        body, name=name,
        grid_spec=pltpu.PrefetchScalarGridSpec(num_scalar_prefetch=1, grid=(SUM_BLOCKS,), in_specs=in_specs, out_specs=out_specs),
        out_shape=out_shape, compiler_params=_cp(("parallel",)),
    )(jnp.stack([chip, ci]).astype(jnp.int32), *ops)


def _adam_update(w, m, v, g):
    c1 = 1.0 / (1.0 - ADAM_B1 ** ADAM_STEP)
    c2 = 1.0 / (1.0 - ADAM_B2 ** ADAM_STEP)
    nm = ADAM_B1 * m + (1.0 - ADAM_B1) * g
    nv = ADAM_B2 * v + (1.0 - ADAM_B2) * (g * g)
    return -ADAM_LR * ((nm * c1) / (jnp.sqrt(nv * c2) + ADAM_EPS) + ADAM_WD * w), nm, nv


def _adam_multi(ws, ms, vs, gs, name):
    n = len(ws)

    def body(*refs):
        for p in range(n):
            d, nm, nv = _adam_update(*[refs[4 * p + k][...] for k in range(4)])
            refs[4 * n + 3 * p][...] = d
            refs[4 * n + 3 * p + 1][...] = nm
            refs[4 * n + 3 * p + 2][...] = nv

    in_specs, out_specs, out_shape = [], [], []
    for w in ws:
        _, r, cw = w.shape
        spec = pl.BlockSpec((None, r // ADAM_BLOCKS, cw), lambda l, i: (l, i, 0))
        in_specs += [spec] * 4
        out_specs += [spec] * 3
        out_shape += [jax.ShapeDtypeStruct(w.shape, F32)] * 3
    ops = [a for q in zip(ws, ms, vs, gs) for a in q]
    res = pl.pallas_call(
        body, name=name, grid=(DEPTH, ADAM_BLOCKS), in_specs=in_specs, out_specs=out_specs, out_shape=out_shape,
        compiler_params=_cp(("parallel", "parallel")),
    )(*ops)
    return res[0::3], res[1::3], res[2::3]


def _adam(w, m, v, parts, name):
    rows, width = w.shape
    bm = min(256, _rows_tile(rows))
    np_ = len(parts)
    c1 = 1.0 / (1.0 - ADAM_B1 ** ADAM_STEP)
    c2 = 1.0 / (1.0 - ADAM_B2 ** ADAM_STEP)

    def body(*refs):
        w_ref, m_ref, v_ref = refs[:3]
        g = refs[3][...]
        for r in refs[4:3 + np_]:
            g = g + r[...]
        g_ref, d_ref, nm_ref, nv_ref = refs[3 + np_:]
        nm = ADAM_B1 * m_ref[...] + (1.0 - ADAM_B1) * g
        nv = ADAM_B2 * v_ref[...] + (1.0 - ADAM_B2) * (g * g)
        g_ref[...] = g
        nm_ref[...] = nm
        nv_ref[...] = nv
        d_ref[...] = -ADAM_LR * ((nm * c1) / (jnp.sqrt(nv * c2) + ADAM_EPS) + ADAM_WD * w_ref[...])

    blk = pl.BlockSpec((bm, width), lambda i: (i, 0))
    return pl.pallas_call(
        body, name=name, grid=(rows // bm,),
        in_specs=[blk, blk, blk] + [pl.BlockSpec((bm, width), functools.partial(lambda i, o: (i + o, 0), o=off // bm))
                                    for (_, off) in parts],
        out_specs=[blk, blk, blk, blk],
        out_shape=[jax.ShapeDtypeStruct((rows, width), F32)] * 4,
        compiler_params=_cp(("parallel",)),
    )(w, m, v, *[p[0] for p in parts])


def _coords():
    return lax.axis_index("x"), lax.axis_index("y"), lax.axis_index("c")


def _other_chips(x, y):
    return [(1 - x, y), (x, 1 - y), (1 - x, 1 - y)]


def _ag8(blk, name):
    m_per, n = blk.shape

    def body(x_ref, out_ref, send_sems, recv_sems, local_sem):
        x, y, c = _coords()
        me, sibling = (x, y, c), (x, y, 1 - c)
        chips = _other_chips(x, y)

        def rows(px, py, pc):
            return out_ref.at[pl.ds((4 * px + 2 * py + pc) * m_per, m_per), :]

        def copy(k, block, to, src=None):
            return pltpu.make_async_remote_copy(
                src_ref=rows(*block) if src is None else src, dst_ref=rows(*block),
                send_sem=send_sems.at[k], recv_sem=recv_sems.at[k], device_id=to, device_id_type=MESH)

        mine = pltpu.make_async_copy(x_ref, rows(*me), local_sem)
        mine.start()
        first = [copy(0, me, sibling, src=x_ref)]
        first += [copy(1 + j, me, (*chip, c), src=x_ref) for j, chip in enumerate(chips)]
        for cp in first:
            cp.start()
        passed = [copy(4 + j, (*chip, c), sibling) for j, chip in enumerate(chips)]
        for j, chip in enumerate(chips):
            copy(1 + j, (*chip, c), me).wait_recv()
            passed[j].start()
        copy(0, sibling, me).wait_recv()
        for j, chip in enumerate(chips):
            copy(4 + j, (*chip, 1 - c), me).wait_recv()
        for cp in first + passed:
            cp.wait_send()
        mine.wait()

    return pl.pallas_call(
        body, name=name,
        out_shape=jax.ShapeDtypeStruct((8 * m_per, n), blk.dtype),
        in_specs=[pl.BlockSpec(memory_space=pltpu.VMEM)], out_specs=pl.BlockSpec(memory_space=pltpu.VMEM),
        scratch_shapes=[pltpu.SemaphoreType.DMA((7,)), pltpu.SemaphoreType.DMA((7,)), pltpu.SemaphoreType.DMA],
    )(blk)


HBM_SPEC = pl.BlockSpec(memory_space=pltpu.HBM)
SEM_SPEC = pl.BlockSpec(memory_space=pltpu.SEMAPHORE)
EFFECT = pltpu.SideEffectType.DATAFLOW_SIDE_EFFECTING


def _remote(src, dst, send_sem, recv_sem, to):
    return pltpu.make_async_remote_copy(src_ref=src, dst_ref=dst, send_sem=send_sem, recv_sem=recv_sem,
                                        device_id=to, device_id_type=MESH)


def _ag_list(shards, name):
    n = len(shards)

    def body(*refs):
        sh, out = refs[:n], refs[n:2 * n]
        send_sems, recv_sems = refs[2 * n:]
        x, y, c = _coords()
        sibling = (x, y, 1 - c)
        chips = _other_chips(x, y)
        first = [_remote(sh[p].at[c], out[p].at[2 * x + y, c], send_sems.at[6 * p + j], recv_sems.at[6 * p + j], (px, py, c))
                 for p in range(n) for j, (px, py) in enumerate(chips)]
        for cp in first:
            cp.start()
        passed = []
        for j, (px, py) in enumerate(chips):
            for p in range(n):
                got = out[p].at[2 * px + py, c]
                _remote(got, got, send_sems.at[6 * p + j], recv_sems.at[6 * p + j], (x, y, c)).wait_recv()
                cp = _remote(got, got, send_sems.at[6 * p + 3 + j], recv_sems.at[6 * p + 3 + j], sibling)
                cp.start()
                passed.append(cp)
        for j, (px, py) in enumerate(chips):
            for p in range(n):
                got = out[p].at[2 * px + py, 1 - c]
                _remote(got, got, send_sems.at[6 * p + 3 + j], recv_sems.at[6 * p + 3 + j], (x, y, c)).wait_recv()
        for cp in first + passed:
            cp.wait_send()

    return pl.pallas_call(
        body, name=name,
        out_shape=[jax.ShapeDtypeStruct((4,) + s.shape, s.dtype) for s in shards],
        in_specs=[pl.BlockSpec(memory_space=pl.ANY)] * n, out_specs=[pl.BlockSpec(memory_space=pl.ANY)] * n,
        scratch_shapes=[pltpu.SemaphoreType.DMA((6 * n,)), pltpu.SemaphoreType.DMA((6 * n,))],
    )(*shards)


def _ag_direct_copies(sh, land, send_sems, recv_sems, starting):
    x, y, c = _coords()
    return [_remote(sh[p], land[p].at[2 * x + y] if starting else land[p].at[2 * px + py],
                    send_sems.at[3 * p + j], recv_sems.at[3 * p + j], (px, py, c))
            for p in range(len(sh)) for j, (px, py) in enumerate(_other_chips(x, y))]


def _ag_direct_start(shards, name):
    n = len(shards)

    def body(*refs):
        for cp in _ag_direct_copies(refs[:n], refs[n:2 * n], refs[2 * n], refs[2 * n + 1], True):
            cp.start()
        token = refs[4 * n + 2]
        token[...] = jnp.zeros_like(token)

    lands = [pltpu.with_memory_space_constraint(lax.empty((4,) + s.shape, s.dtype), pltpu.HBM) for s in shards]
    res = pl.pallas_call(
        body, name=name,
        out_shape=(pltpu.SemaphoreType.DMA((3 * n,)), pltpu.SemaphoreType.DMA((3 * n,)))
        + tuple(pltpu.HBM(s.shape, s.dtype) for s in shards) + tuple(pltpu.HBM(l.shape, l.dtype) for l in lands)
        + (jax.ShapeDtypeStruct((8, LANES), F32),),
        in_specs=(HBM_SPEC,) * (2 * n), out_specs=(SEM_SPEC, SEM_SPEC) + (HBM_SPEC,) * (2 * n) + (pl.BlockSpec(memory_space=pltpu.VMEM),),
        input_output_aliases={i: 2 + i for i in range(2 * n)},
        compiler_params=pltpu.CompilerParams(has_side_effects=EFFECT),
    )(*[pltpu.with_memory_space_constraint(s, pltpu.HBM) for s in shards], *lands)
    return res[0], res[1], res[2:2 + n], res[2 + n:2 + 2 * n], res[2 + 2 * n]


def _ag_direct_wait(send_sems, recv_sems, sh_thru, land_thru, after, name):
    n = len(sh_thru)

    def body(*refs):
        sh, land = refs[:n], refs[n:2 * n]
        for cp in _ag_direct_copies(sh, land, refs[2 * n], refs[2 * n + 1], False):
            cp.wait_send()
            cp.wait_recv()

    res = pl.pallas_call(
        body, name=name,
        out_shape=tuple(pltpu.HBM(s.shape, s.dtype) for s in sh_thru) + tuple(pltpu.HBM(l.shape, l.dtype) for l in land_thru),
        in_specs=(HBM_SPEC,) * (2 * n) + (SEM_SPEC, SEM_SPEC, pl.BlockSpec(memory_space=pl.ANY)),
        out_specs=(HBM_SPEC,) * (2 * n), input_output_aliases={i: i for i in range(2 * n)},
        compiler_params=pltpu.CompilerParams(has_side_effects=EFFECT),
    )(*sh_thru, *land_thru, send_sems, recv_sems, after)
    return res[n:]


def _rs_sibling_list(gs, name):
    n = len(gs)

    def body(*refs):
        g, out, send_sems, recv_sems = refs[:n], refs[n:2 * n], refs[2 * n], refs[2 * n + 1]
        x, y, c = _coords()
        cps = [_remote(g[p].at[s, 1 - c], out[p].at[s], send_sems.at[4 * p + s], recv_sems.at[4 * p + s], (x, y, 1 - c))
               for p in range(n) for s in range(4)]
        for cp in cps:
            cp.start()
        for cp in cps:
            cp.wait_recv()
        for cp in cps:
            cp.wait_send()

    return pl.pallas_call(
        body, name=name,
        out_shape=[jax.ShapeDtypeStruct((4,) + g.shape[2:], g.dtype) for g in gs],
        in_specs=[pl.BlockSpec(memory_space=pl.ANY)] * n, out_specs=[pl.BlockSpec(memory_space=pl.ANY)] * n,
        scratch_shapes=[pltpu.SemaphoreType.DMA((4 * n,)), pltpu.SemaphoreType.DMA((4 * n,))],
    )(*gs)


def _rs_chips_copies(cs, land, send_sems, recv_sems):
    x, y, c = _coords()
    return [_remote(cs[p].at[2 * px + py], land[p].at[j], send_sems.at[3 * p + j], recv_sems.at[3 * p + j], (px, py, c))
            for p in range(len(cs)) for j, (px, py) in enumerate(_other_chips(x, y))]


def _rs_chips_start(cs, name):
    n = len(cs)

    def body(*refs):
        for cp in _rs_chips_copies(refs[:n], refs[n:2 * n], refs[2 * n], refs[2 * n + 1]):
            cp.start()
        token = refs[4 * n + 2]
        token[...] = jnp.zeros_like(token)

    lands = [pltpu.with_memory_space_constraint(lax.empty((3,) + c.shape[1:], c.dtype), pltpu.HBM) for c in cs]
    res = pl.pallas_call(
        body, name=name,
        out_shape=(pltpu.SemaphoreType.DMA((3 * n,)), pltpu.SemaphoreType.DMA((3 * n,)))
        + tuple(pltpu.HBM(c.shape, c.dtype) for c in cs) + tuple(pltpu.HBM(l.shape, l.dtype) for l in lands)
        + (jax.ShapeDtypeStruct((8, LANES), F32),),
        in_specs=(HBM_SPEC,) * (2 * n), out_specs=(SEM_SPEC, SEM_SPEC) + (HBM_SPEC,) * (2 * n) + (pl.BlockSpec(memory_space=pltpu.VMEM),),
        input_output_aliases={i: 2 + i for i in range(2 * n)},
        compiler_params=pltpu.CompilerParams(has_side_effects=EFFECT),
    )(*[pltpu.with_memory_space_constraint(c, pltpu.HBM) for c in cs], *lands)
    return res[0], res[1], res[2:2 + n], res[2 + n:2 + 2 * n], res[2 + 2 * n]


def _rs_chips_wait(send_sems, recv_sems, cs_thru, land_thru, after, name):
    n = len(cs_thru)

    def body(*refs):
        for cp in _rs_chips_copies(refs[:n], refs[n:2 * n], refs[2 * n], refs[2 * n + 1]):
            cp.wait_send()
            cp.wait_recv()

    res = pl.pallas_call(
        body, name=name,
        out_shape=tuple(pltpu.HBM(c.shape, c.dtype) for c in cs_thru) + tuple(pltpu.HBM(l.shape, l.dtype) for l in land_thru),
        in_specs=(HBM_SPEC,) * (2 * n) + (SEM_SPEC, SEM_SPEC, pl.BlockSpec(memory_space=pl.ANY)),
        out_specs=(HBM_SPEC,) * (2 * n), input_output_aliases={i: i for i in range(2 * n)},
        compiler_params=pltpu.CompilerParams(has_side_effects=EFFECT),
    )(*cs_thru, *land_thru, send_sems, recv_sems, after)
    return res[:n], res[n:]


def _swap_list(ghs, name):
    n = len(ghs)

    def body(*refs):
        g, out, send_sems, recv_sems = refs[:n], refs[n:2 * n], refs[2 * n], refs[2 * n + 1]
        x, y, c = _coords()
        cps = [_remote(g[p].at[c], out[p].at[c], send_sems.at[p], recv_sems.at[p], (x, y, 1 - c)) for p in range(n)]
        for cp in cps:
            cp.start()
        for p in range(n):
            _remote(g[p].at[c], out[p].at[1 - c], send_sems.at[p], recv_sems.at[p], (x, y, 1 - c)).wait_recv()
        for cp in cps:
            cp.wait_send()

    return pl.pallas_call(
        body, name=name,
        out_shape=[jax.ShapeDtypeStruct(g.shape, g.dtype) for g in ghs],
        in_specs=[pl.BlockSpec(memory_space=pl.ANY)] * n, out_specs=[pl.BlockSpec(memory_space=pl.ANY)] * n,
        input_output_aliases={p: p for p in range(n)},
        scratch_shapes=[pltpu.SemaphoreType.DMA((n,)), pltpu.SemaphoreType.DMA((n,))],
    )(*ghs)


def _pad_win(w):
    return jnp.concatenate([w[:, :416], jnp.zeros((w.shape[0], 96), w.dtype), w[:, 416:1952],
                            w[:, 1952:1960], jnp.zeros((w.shape[0], 120), w.dtype)], axis=1)


def _unpad_win(g):
    return jnp.concatenate([g[:, :416], g[:, 512:2048], g[:, 2048:2056]], axis=1)


def _pad_wq(w):
    return jnp.pad(w.reshape(Q_LORA, HEADS, QK_DIM), ((0, 0), (0, 0), (0, LANES - QK_DIM))).reshape(Q_LORA, HEADS * LANES)


def _unpad_wq(g):
    return g.reshape(Q_LORA, HEADS, LANES)[:, :, :QK_DIM].reshape(Q_LORA, HEADS * QK_DIM)


def _cols_to_shards(a):
    r, c4 = a.shape
    return a.reshape(r, 4, c4 // 4).transpose(1, 0, 2)


def _shards_to_cols(a):
    _, r, c = a.shape
    return a.transpose(1, 0, 2).reshape(r, 4 * c)


def _pack_small(tree):
    parts = []
    for l in range(DEPTH):
        for (n, k) in SMALL:
            parts.append(jnp.pad(tree[n][l].reshape(-1), (0, -k % LANES)))
    flat = jnp.concatenate(parts)
    return jnp.pad(flat, (0, SMALL_ROWS * LANES - flat.shape[0])).reshape(SMALL_ROWS, LANES)


def _unpack_small(buf):
    flat = buf.reshape(-1)
    out = {n: [] for (n, _) in SMALL}
    o = 0
    for l in range(DEPTH):
        for (n, k) in SMALL:
            out[n].append(flat[o:o + k])
            o += k + (-k % LANES)
    return {n: jnp.stack(v) for n, v in out.items()}


def _vec(v, width=LANES):
    return jnp.pad(v.reshape(1, -1), ((0, 0), (0, width - v.shape[-1])))


def kernel(x, c, positions, norm1_w, norm2_w, w_ada, b_ada, w_in, q_a_norm_w, w_q_up, kv_a_norm_w, w_kv_up, q_nope_norm_w, q_pe_norm_w, k_nope_norm_w, k_pe_norm_w, conv_w, conv_b, dt_bias, a_log, d_skip, ssd_norm_w, w_out, w_gate_up, w_down, loss_target, m_norm1_w, m_norm2_w, m_w_ada, m_b_ada, m_w_in, m_q_a_norm_w, m_w_q_up, m_kv_a_norm_w, m_w_kv_up, m_q_nope_norm_w, m_q_pe_norm_w, m_k_nope_norm_w, m_k_pe_norm_w, m_conv_w, m_conv_b, m_dt_bias, m_a_log, m_d_skip, m_ssd_norm_w, m_w_out, m_w_gate_up, m_w_down, v_norm1_w, v_norm2_w, v_w_ada, v_b_ada, v_w_in, v_q_a_norm_w, v_w_q_up, v_kv_a_norm_w, v_w_kv_up, v_q_nope_norm_w, v_q_pe_norm_w, v_k_nope_norm_w, v_k_pe_norm_w, v_conv_w, v_conv_b, v_dt_bias, v_a_log, v_d_skip, v_ssd_norm_w, v_w_out, v_w_gate_up, v_w_down):
    W = dict(zip(WEIGHTS, (norm1_w, norm2_w, w_ada, b_ada, w_in, q_a_norm_w, w_q_up, kv_a_norm_w, w_kv_up, q_nope_norm_w, q_pe_norm_w, k_nope_norm_w, k_pe_norm_w, conv_w, conv_b, dt_bias, a_log, d_skip, ssd_norm_w, w_out, w_gate_up, w_down)))
    M = dict(zip(WEIGHTS, (m_norm1_w, m_norm2_w, m_w_ada, m_b_ada, m_w_in, m_q_a_norm_w, m_w_q_up, m_kv_a_norm_w, m_w_kv_up, m_q_nope_norm_w, m_q_pe_norm_w, m_k_nope_norm_w, m_k_pe_norm_w, m_conv_w, m_conv_b, m_dt_bias, m_a_log, m_d_skip, m_ssd_norm_w, m_w_out, m_w_gate_up, m_w_down)))
    V = dict(zip(WEIGHTS, (v_norm1_w, v_norm2_w, v_w_ada, v_b_ada, v_w_in, v_q_a_norm_w, v_w_q_up, v_kv_a_norm_w, v_w_kv_up, v_q_nope_norm_w, v_q_pe_norm_w, v_k_nope_norm_w, v_k_pe_norm_w, v_conv_w, v_conv_b, v_dt_bias, v_a_log, v_d_skip, v_ssd_norm_w, v_w_out, v_w_gate_up, v_w_down)))
    S = x.shape[1]
    xi, yi, ci = _coords()
    chip = 2 * xi + yi
    dev = 2 * chip + ci
    x0 = x[0]
    tgt = loss_target[0]

    inv_freq = 1.0 / (ROPE_THETA ** (jnp.arange(0, ROPE, 2, dtype=F32) / ROPE))
    ang = positions[0].astype(F32)[:, None] * inv_freq
    cos, sin = jnp.cos(ang), jnp.sin(ang)
    z16, z32, z64 = jnp.zeros((S, 16), F32), jnp.zeros((S, 32), F32), jnp.zeros((S, 64), F32)
    tab_c = jnp.concatenate([jnp.ones((S, 64), F32), cos, cos, z32], axis=1)
    tab_s1 = jnp.concatenate([z64, z16, sin, z32], axis=1)
    tab_s2 = jnp.concatenate([z64, -sin, z16, z32], axis=1)

    blk0 = jnp.concatenate([c.reshape(-1), W['conv_w'].reshape(-1)]).reshape(24, LANES)
    g0 = _ag8(blk0, "ag_c_conv").reshape(8, 24 * LANES)
    c_all = g0[:, :D_MODEL]
    conv_full = g0[0::2, D_MODEL:].reshape(4, DEPTH, CONV_TAPS, 256).transpose(1, 2, 0, 3).reshape(DEPTH, CONV_TAPS, D_CONV)

    sh = [{n: W[n][l].astype(BF16) for n in BIG} for l in range(DEPTH)]
    got_first = _ag_list([sh[0][n].reshape(2, sh[0][n].shape[0] // 2, sh[0][n].shape[1]) for n in FIRST], "ag_w0_first")
    to_operand = dict(w_in=lambda a: _pad_win(_shards_to_cols(a)), w_q_up=lambda a: _pad_wq(_shards_to_cols(a)),
                      w_kv_up=_shards_to_cols, w_out=lambda a: a.reshape(D_MODEL, D_MODEL), w_gate_up=lambda a: a,
                      w_down=lambda a: a.reshape(D_FF, D_MODEL))

    def layer_weights(names, gathered, own):
        return {n: to_operand[n](lax.dynamic_update_slice_in_dim(a.reshape(4, -1, a.shape[-1]), own[n][None], chip, axis=0))
                for n, a in zip(names, gathered)}

    LW = [layer_weights(FIRST, got_first, sh[0]), None]

    b_sh = lax.dynamic_slice_in_dim(W['b_ada'], chip * 1536, 1536, axis=1).reshape(DEPTH, 1, 1536)
    mod_sh = _ada_fwd(c_all, W['w_ada'], b_sh, "ada_fwd")
    g1 = _ag8(mod_sh.reshape(192, LANES), "ag_mod").reshape(8, DEPTH, 8, 1536)
    mod_all = g1[0::2].transpose(1, 2, 0, 3).reshape(DEPTH, 8, 6 * D_MODEL)
    mod = lax.dynamic_index_in_dim(mod_all, dev, axis=1, keepdims=False)
    mod, rest0 = lax.optimization_barrier((mod, [sh[0][n] for n in REST]))
    ag0 = _ag_direct_start(rest0, "ag_w0_rest_start")

    def mvec(l, k):
        return mod[l, k * D_MODEL:(k + 1) * D_MODEL].reshape(1, D_MODEL)

    def small(name, l, width=None):
        v = W[name][l]
        return _vec(v, width or v.shape[-1])

    def wq_vec(l):
        return _vec(jnp.concatenate([W['q_nope_norm_w'][l], W['q_pe_norm_w'][l]]))

    def wk_vec(l):
        return _vec(jnp.concatenate([W['k_nope_norm_w'][l], W['k_pe_norm_w'][l]]))

    def conv_vec(l):
        return jnp.concatenate([conv_full[l], W['conv_b'][l].reshape(1, D_CONV), jnp.zeros((3, D_CONV), F32)], axis=0)

    def ssd_vec(l):
        return jnp.concatenate([_vec(W['dt_bias'][l]), _vec(W['a_log'][l]), _vec(W['d_skip'][l]), jnp.zeros((5, LANES), F32)], axis=0)

    sv = []
    xcur = x0
    h1 = _row_fwd(fn_norm_mod, "norm_mod_f", [(x0, 0, D_MODEL)], [small('norm1_w', 0) + ag0[4][0, 0], mvec(0, 1), mvec(0, 0)],
                  [(D_MODEL, BF16)])[0]
    fin = None
    ag_first = None
    ag_rest = ag0
    for l in range(DEPTH):
        if l == 1:
            LW[1] = layer_weights(FIRST, _ag_direct_wait(*ag_first[:4], xcur, "ag_w1_first_wait"), sh[1])
        lw = LW[l]
        t = dict(xcur=xcur, h1=h1)
        t['proj'] = proj = _mm(h1, lw['w_in'], 'nn', f"mm_in_{l}")
        t['qa_n'], t['kva_n'] = _row_fwd(fn_lat_norm, f"lat_norm_f{l}", [(proj, 0, 256), (proj, 2, 128)],
                                         [small('q_a_norm_w', l), small('kv_a_norm_w', l)], [(256, BF16), (128, BF16)])
        t['q'] = _mm(t['qa_n'], lw['w_q_up'], 'nn', f"mm_q_{l}")
        t['kv'] = _mm(t['kva_n'], lw['w_kv_up'], 'nn', f"mm_kv_{l}")
        t['qf'], t['kf'], t['vv'], t['kT'] = _row_fwd(
            fn_qk_prep_kt, f"qk_prep_f{l}",
            [(t['q'], 0, 1024), (t['kv'], 0, 1024), (proj, 3, 128), (tab_c, 0, 128), (tab_s1, 0, 128), (tab_s2, 0, 128)],
            [wq_vec(l), wk_vec(l)], [(1024, BF16), (1024, BF16), (1024, BF16), (1024, BF16)], transposed=(3,))
        t['ao'], t['lse'] = _attn_fwd(t['qf'], t['kf'], t['vv'], f"attn_f{l}")
        t['xact'] = _conv_fwd(proj, conv_vec(l), f"conv_f{l}")
        t['y'], t['states'] = _ssd_fwd(t['xact'], proj, ssd_vec(l), f"ssd_f{l}")
        tie = 0.0
        t['ao'], t['y'] = lax.optimization_barrier((t['ao'], t['y']))
        rest = list(_ag_direct_wait(*ag_rest[:4], t['y'], f"ag_w{l}_rest_wait"))
        if l == 0:
            rest, sh1f, sh1r = lax.optimization_barrier((rest, [sh[1][n] for n in FIRST], [sh[1][n] for n in REST]))
            ag_first = _ag_direct_start(sh1f, "ag_w1_first_start")
            ag_rest = _ag_direct_start(sh1r, "ag_w1_rest_start")
            tie = ag_first[4][0, 0] + ag_rest[4][0, 0]
        lw.update(layer_weights(REST, rest, sh[l]))
        t['mix'] = _row_fwd(fn_gated_mix, f"gated_f{l}", [(t['y'], 0, 512), (proj, 1, 512), (t['ao'], 0, 512)],
                            [small('ssd_norm_w', l) + tie], [(1024, BF16)])[0]
        t['mo'] = _mm(t['mix'], lw['w_out'], 'nn', f"mm_out_{l}")
        t['x1'], t['h2'] = _row_fwd(fn_resid_norm, f"resid_mid_f{l}", [(xcur, 0, D_MODEL), (t['mo'], 0, D_MODEL)],
                                    [mvec(l, 2), small('norm2_w', l), mvec(l, 4), mvec(l, 3)],
                                    [(D_MODEL, F32), (D_MODEL, BF16)])
        t['gu'], t['act'] = _mm_gu_swiglu(t['h2'], lw['w_gate_up'], f"mm_gu_{l}")
        t['ff'] = _mm(t['act'], lw['w_down'], 'nn', f"mm_down_{l}")
        if l + 1 < DEPTH:
            xcur, h1 = _row_fwd(fn_resid_norm, f"resid_end_f{l}", [(t['x1'], 0, D_MODEL), (t['ff'], 0, D_MODEL)],
                                [mvec(l, 5), small('norm1_w', l + 1), mvec(l + 1, 1), mvec(l + 1, 0)],
                                [(D_MODEL, F32), (D_MODEL, BF16)])
        else:
            fin = _final(t['x1'], t['ff'], tgt, mvec(l, 5), "final_loss")
        sv.append(t)

    dx1, dff, dg2_last, loss_acc = fin
    gfull = {n: [None] * DEPTH for n in BIG}
    gsm = {n: [None] * DEPTH for (n, _) in SMALL}
    dmod = [[None] * 6 for _ in range(DEPTH)]
    dmod[DEPTH - 1][5] = dg2_last
    grad_x = None
    pending = []

    def rs_begin(l, names, tag):
        g4 = [gfull[n][l].reshape(4, 2, gfull[n][l].shape[1] // 2, gfull[n][l].shape[2]) for n in names]
        sib = _rs_sibling_list(g4, f"rs_sibling_{tag}")
        cs = _sum_sibling(g4, sib, ci, f"sum_sibling_{tag}")
        h = _rs_chips_start(cs, f"rs_chips_start_{tag}")
        pending.append((l, names, h))
        return h[4][0, 0]

    tie_l1 = tie_l0a = 0.0

    for l in reversed(range(DEPTH)):
        t = sv[l]
        lw = LW[l]
        proj = t['proj']
        dgu = _mm_down_dx_swiglu(dff, lw['w_down'], t['gu'], f"mm_down_dx{l}")
        gfull['w_down'][l] = _mm(t['act'], dff, 'tn', f"mm_down_dw{l}").reshape(4, D_FF // 4, D_MODEL)
        dh2 = _mm(dgu, lw['w_gate_up'], 'nt', f"mm_gu_dx{l}", stack='b')
        gfull['w_gate_up'][l] = _mm(t['h2'], dgu, 'tn', f"mm_gu_dw{l}", stack='out')
        if l == 0:
            tie_l0a = rs_begin(0, EARLY, "l0a")
        dxc, dmo, dmod[l][2], gsm['norm2_w'][l], dmod[l][4], dmod[l][3] = _row_bwd(
            fn_resid_norm, f"resid_mid_b{l}", [(t['xcur'], 0, D_MODEL), (t['mo'], 0, D_MODEL)],
            [mvec(l, 2) + (tie_l1 if l == 0 else 0.0), small('norm2_w', l), mvec(l, 4), mvec(l, 3)],
            [(dx1, 0, D_MODEL), (dh2, 0, D_MODEL)], [0, 1], [0, 1, 2, 3], ddtypes=[F32, BF16])
        dmix = _mm(dmo, lw['w_out'], 'nt', f"mm_out_dx{l}")
        gfull['w_out'][l] = _mm(t['mix'], dmo, 'tn', f"mm_out_dw{l}").reshape(4, D_MODEL // 4, D_MODEL)
        dy, dz, gsm['ssd_norm_w'][l] = _row_bwd(fn_gated_norm, f"gated_b{l}", [(t['y'], 0, 512), (proj, 1, 512)],
                                                [small('ssd_norm_w', l)], [(dmix, 1, 512)], [0, 1], [0])
        dxact, ddt, dsv = _ssd_bwd(t['xact'], proj, ssd_vec(l) + (tie_l0a if l == 0 else 0.0), t['states'], dy, f"ssd_b{l}")
        gsm['dt_bias'][l], gsm['a_log'][l], gsm['d_skip'][l] = dsv[0, :8], dsv[1, :8], dsv[2, :8]
        dxbc, dcv = _conv_bwd(proj, conv_vec(l), dxact, f"conv_b{l}")
        gsm['conv_w'][l] = dcv[:CONV_TAPS]
        gsm['conv_b'][l] = dcv[CONV_TAPS]
        delta_r = _attn_delta(dmix, t['ao'], f"attn_delta{l}")
        dqT, dkf, dvv = _attn_bwd(t['qf'], t['kf'], t['kT'], t['vv'], dmix, t['lse'], delta_r, f"attn_b{l}")
        dq, dkv, dkpe, dwq, dwk = _row_bwd(
            fn_qk_prep, f"qk_prep_b{l}",
            [(t['q'], 0, 1024), (t['kv'], 0, 1024), (proj, 3, 128), (tab_c, 0, 128), (tab_s1, 0, 128), (tab_s2, 0, 128)],
            [wq_vec(l), wk_vec(l)], [(dqT, 0, 1024), (dkf, 0, 1024), (dvv, 0, 1024)], [0, 1, 2], [0, 1],
            ddtypes=[BF16, BF16, F32], transposed=(0,))
        gsm['q_nope_norm_w'][l], gsm['q_pe_norm_w'][l] = dwq[0, :NOPE], dwq[0, NOPE:QK_DIM]
        gsm['k_nope_norm_w'][l], gsm['k_pe_norm_w'][l] = dwk[0, :NOPE], dwk[0, NOPE:QK_DIM]
        dqa_n = _mm(dq, lw['w_q_up'], 'nt', f"mm_q_dx{l}")
        gfull['w_q_up'][l] = _cols_to_shards(_unpad_wq(_mm(t['qa_n'], dq, 'tn', f"mm_q_dw{l}")))
        dkva_n = _mm(dkv, lw['w_kv_up'], 'nt', f"mm_kv_dx{l}")
        gfull['w_kv_up'][l] = _cols_to_shards(_mm(t['kva_n'], dkv, 'tn', f"mm_kv_dw{l}"))
        dqa, dkva, dqw, dkvw = _row_bwd(fn_lat_norm, f"lat_norm_b{l}", [(proj, 0, 256), (proj, 2, 128)],
                                        [small('q_a_norm_w', l), small('kv_a_norm_w', l)],
                                        [(dqa_n, 0, 256), (dkva_n, 0, 128)], [0, 1], [0, 1])
        gsm['q_a_norm_w'][l], gsm['kv_a_norm_w'][l] = dqw[0], dkvw[0]
        dproj = jnp.concatenate([dqa, dkva, dkpe, dz, dxbc, ddt], axis=1).astype(BF16)
        dh1 = _mm(dproj, lw['w_in'], 'nt', f"mm_in_dx{l}")
        gfull['w_in'][l] = _cols_to_shards(_unpad_win(_mm(t['h1'], dproj, 'tn', f"mm_in_dw{l}")))
        if l > 0:
            p = sv[l - 1]
            dx1, dff, dmod[l - 1][5], gsm['norm1_w'][l], dmod[l][1], dmod[l][0] = _row_bwd(
                fn_resid_norm, f"resid_end_b{l - 1}", [(p['x1'], 0, D_MODEL), (p['ff'], 0, D_MODEL)],
                [mvec(l - 1, 5), small('norm1_w', l), mvec(l, 1), mvec(l, 0)], [(dxc, 0, D_MODEL), (dh1, 0, D_MODEL)],
                [0, 1], [0, 1, 2, 3], ddtypes=[F32, BF16])
            tie_l1 = rs_begin(l, BIG, f"l{l}")
        else:
            grad_x, gsm['norm1_w'][l], dmod[l][1], dmod[l][0] = _row_bwd(
                fn_norm_mod_pass, "norm_mod_b", [(x0, 0, D_MODEL)], [small('norm1_w', 0), mvec(0, 1), mvec(0, 0)],
                [(dxc, 0, D_MODEL), (dh1, 0, D_MODEL)], [0], [0, 1, 2])
        for n in ('norm1_w', 'norm2_w', 'ssd_norm_w'):
            gsm[n][l] = gsm[n][l][0]

    for l in range(DEPTH):
        gsm['b_ada'][l] = jnp.concatenate([d[0] for d in dmod[l]])
    sm_part = _pack_small({n: jnp.stack(v) for n, v in gsm.items()}).at[SMALL_ROWS - 1, 0].set(loss_acc[0, 0])
    sm_all = _ag8(sm_part, "ag_small")
    loss = jnp.sum(sm_all.reshape(8, SMALL_ROWS, LANES)[:, SMALL_ROWS - 1, 0])
    sm_all, late = lax.optimization_barrier((sm_all, [gfull[n][0] for n in BIG[2:]]))
    for n, g in zip(BIG[2:], late):
        gfull[n][0] = g
    tie_l0b = rs_begin(0, BIG[2:], "l0b")

    def with_conv(tree):
        wide = lax.dynamic_update_slice_in_dim(jnp.zeros((DEPTH, CONV_TAPS, D_CONV), F32), tree['conv_w'], chip * 256, axis=2)
        return {**tree, 'conv_w': wide}

    g_sm, d_sm, m_sm, v_sm = _adam(_pack_small(with_conv(W)) + tie_l0b, _pack_small(with_conv(M)), _pack_small(with_conv(V)),
                                   [(sm_all, d * SMALL_ROWS) for d in range(8)], "adam_small")
    out_small = [_unpack_small(b) for b in (g_sm, d_sm, m_sm, v_sm)]
    for o in out_small:
        o['conv_w'] = lax.dynamic_slice_in_dim(o['conv_w'].reshape(DEPTH, CONV_TAPS, D_CONV), chip * 256, 256, axis=2)

    dmod_all = sm_all.reshape(8, SMALL_ROWS * LANES)
    per_layer = sum(k + (-k % LANES) for (_, k) in SMALL)
    dmod_sh = jnp.stack([lax.dynamic_slice_in_dim(dmod_all[:, l * per_layer:l * per_layer + 6 * D_MODEL], chip * 1536, 1536, axis=1)
                         for l in range(DEPTH)])
    g_ada = _ada_bwd(c_all.T, dmod_sh, "ada_bwd")
    ada = _adam(W['w_ada'].reshape(DEPTH * D_MODEL, 1536), M['w_ada'].reshape(DEPTH * D_MODEL, 1536),
                V['w_ada'].reshape(DEPTH * D_MODEL, 1536), [(g_ada.reshape(DEPTH * D_MODEL, 1536), 0)], "adam_ada")
    out_ada = [a.reshape(DEPTH, D_MODEL, 1536) for a in ada]

    keys, cs_all, land_all = [], [], []
    for (l, names, (send_sems, recv_sems, cs_thru, land_thru, _)) in pending:
        cs, lands = _rs_chips_wait(send_sems, recv_sems, cs_thru, land_thru, ada[3], f"rs_chips_wait_l{l}{len(names)}")
        keys += [(l, n) for n in names]
        cs_all += list(cs)
        land_all += list(lands)
    gboth = _swap_list(_sum_chips(cs_all, land_all, chip, ci, "sum_chips"), "swap_halves")
    gshard = {k: g.reshape(2 * g.shape[1], g.shape[2]) for k, g in zip(keys, gboth)}
    g_big = [jnp.stack([gshard[(l, n)] for l in range(DEPTH)]) for n in BIG]
    d_big, m_big, v_big = _adam_multi([W[n] for n in BIG], [M[n] for n in BIG], [V[n] for n in BIG], g_big, "adam_big")
    out_big = [dict(zip(BIG, o)) for o in (g_big, d_big, m_big, v_big)]

    outs = [loss, grad_x[None]]
    for k in range(4):
        for n in WEIGHTS:
            if n == 'w_ada':
                outs.append(out_ada[k])
            elif n in BIG:
                outs.append(out_big[k][n])
            else:
                outs.append(out_small[k][n])
    return tuple(outs)
```

```python
import functools

import jax
import jax.numpy as jnp
from jax import lax
from jax.experimental import pallas as pl
from jax.experimental.pallas import tpu as pltpu

F32 = jnp.float32
BF16 = jnp.bfloat16
MESH = pl.DeviceIdType.MESH

D_MODEL = 1024
DEPTH = 2
HEADS = 8
NOPE = 64
ROPE = 32
QK_DIM = NOPE + ROPE
Q_LORA = 256
KV_LORA = 128
SSD_HEADS = 8
SSD_P = 64
SSD_N = 128
CHUNK = 128
CONV_TAPS = 4
D_SSD = 512
D_CONV = 1024
D_FF = 2816
D_IN = 1960
D_IN_PAD = 2176
EPS = 1e-6
ROPE_THETA = 10000.0
ATT_SCALE = QK_DIM ** -0.5
NEG = -1e30
LANES = 128
VMEM_LIMIT = 48 * 1024 * 1024

ADAM_LR, ADAM_B1, ADAM_B2, ADAM_EPS, ADAM_WD, ADAM_STEP = 0.001, 0.9, 0.999, 1e-08, 0.01, 10

WEIGHTS = ['norm1_w', 'norm2_w', 'w_ada', 'b_ada', 'w_in', 'q_a_norm_w', 'w_q_up', 'kv_a_norm_w', 'w_kv_up',
           'q_nope_norm_w', 'q_pe_norm_w', 'k_nope_norm_w', 'k_pe_norm_w', 'conv_w', 'conv_b', 'dt_bias', 'a_log',
           'd_skip', 'ssd_norm_w', 'w_out', 'w_gate_up', 'w_down']
BIG = ['w_down', 'w_gate_up', 'w_out', 'w_kv_up', 'w_q_up', 'w_in']
EARLY = BIG[:2]
FIRST = BIG[3:]
REST = BIG[:3]
SMALL = [('b_ada', 6144), ('conv_w', 4096), ('norm1_w', 1024), ('norm2_w', 1024), ('conv_b', 1024), ('ssd_norm_w', 512),
         ('q_a_norm_w', 256), ('kv_a_norm_w', 128), ('q_nope_norm_w', 64), ('q_pe_norm_w', 32),
         ('k_nope_norm_w', 64), ('k_pe_norm_w', 32), ('dt_bias', 8), ('a_log', 8), ('d_skip', 8)]
SMALL_ROWS = 240


def _cp(sem=None, **kw):
    return pltpu.CompilerParams(dimension_semantics=sem, vmem_limit_bytes=VMEM_LIMIT, **kw)


def _dot(a, b, dims, prec=None):
    return lax.dot_general(a, b, (dims, ((), ())), preferred_element_type=F32, precision=prec)


def _tile(dim, target):
    best = 0
    for t in range(LANES, min(dim, target) + 1, LANES):
        if dim % t == 0:
            best = t
    if best < 256 and dim <= 2304:
        return dim
    return best


def _mm(a, b, mode, name, out_dtype=F32, stack=None):
    ns = None
    halves = (a if mode == 'nt' else b).ndim == 3 and stack is not None and not (stack == 'b' and mode == 'nn')
    if stack == 'b':
        ns = b.shape[2]
        if mode == 'nn':
            (M, K), N = a.shape, 4 * ns
        else:
            M, K, N = a.shape[-2], 4 * ns, b.shape[1]
    elif mode == 'nn':
        (M, K), (_, N) = a.shape, b.shape
    elif mode == 'nt':
        (M, K), (N, _) = a.shape, b.shape
    else:
        (K, M), N = a.shape, (2 * b.shape[2] if halves else b.shape[1])
    if stack == 'out':
        ns = N // 4
    tm, tn, tk = _tile(M, 1408 if mode == 'tn' else 1024), _tile(N, 1408), _tile(K, 1408)
    if stack == 'b' and mode == 'nt':
        tk = ns
    elif stack is not None:
        tn = ns
    nk = K // tk
    dims = {'nn': ((1,), (0,)), 'nt': ((1,), (1,)), 'tn': ((0,), (0,))}[mode]

    def body(a_ref, b_ref, o_ref, *acc):
        part = _dot(a_ref[...].astype(BF16), b_ref[...].astype(BF16), dims)
        if nk == 1:
            o_ref[...] = part.astype(o_ref.dtype)
            return
        k = pl.program_id(2)

        @pl.when(k == 0)
        def _():
            acc[0][...] = part

        @pl.when(k > 0)
        def _():
            acc[0][...] += part

        @pl.when(k == nk - 1)
        def _():
            o_ref[...] = acc[0][...].astype(o_ref.dtype)

    a_spec = pl.BlockSpec((tk, tm), lambda i, j, k: (k, i)) if mode == 'tn' else pl.BlockSpec((tm, tk), lambda i, j, k: (i, k))
    b_spec = pl.BlockSpec((tn, tk), lambda i, j, k: (j, k)) if mode == 'nt' else pl.BlockSpec((tk, tn), lambda i, j, k: (k, j))
    o_spec, o_shape = pl.BlockSpec((tm, tn), lambda i, j, k: (i, j)), (M, N)
    if stack == 'b':
        b_spec = (pl.BlockSpec((None, tn, ns), lambda i, j, k: (k, j, 0)) if mode == 'nt'
                  else pl.BlockSpec((None, tk, ns), lambda i, j, k: (j, k, 0)))
    if stack == 'out':
        o_spec, o_shape = pl.BlockSpec((None, tm, ns), lambda i, j, k: (j, i, 0)), (4, M, ns)
    if halves and mode == 'nt':
        a_spec = pl.BlockSpec((None, tm, ns), lambda i, j, k: (lax.div(k, 2), i, lax.rem(k, 2)))
    if halves and mode == 'tn':
        b_spec = pl.BlockSpec((None, tk, ns), lambda i, j, k: (lax.div(j, 2), k, lax.rem(j, 2)))
    return pl.pallas_call(
        body, name=name, grid=(M // tm, N // tn, nk),
        in_specs=[a_spec, b_spec], out_specs=o_spec,
        out_shape=jax.ShapeDtypeStruct(o_shape, out_dtype),
        scratch_shapes=[pltpu.VMEM((tm, tn), F32)] if nk > 1 else [],
        compiler_params=_cp(("parallel", "parallel", "arbitrary")),
    )(a, b)


def _mm_gu_swiglu(h, wst, name):
    S, K = h.shape
    ns = wst.shape[2]
    tm = _tile(S, 512)

    def body(a_ref, bg_ref, bu_ref, gu_ref, act_ref):
        a = a_ref[...]
        g = _dot(a, bg_ref[...], ((1,), (0,)))
        u = _dot(a, bu_ref[...], ((1,), (0,)))
        gu_ref[0] = g
        gu_ref[1] = u
        act_ref[...] = (g * jax.nn.sigmoid(g) * u).astype(act_ref.dtype)

    return pl.pallas_call(
        body, name=name, grid=(S // tm, 2),
        in_specs=[pl.BlockSpec((tm, K), lambda i, j: (i, 0)), pl.BlockSpec((None, K, ns), lambda i, j: (j, 0, 0)),
                  pl.BlockSpec((None, K, ns), lambda i, j: (j + 2, 0, 0))],
        out_specs=[pl.BlockSpec((2, tm, ns), lambda i, j: (0, i, j)), pl.BlockSpec((tm, ns), lambda i, j: (i, j))],
        out_shape=[jax.ShapeDtypeStruct((2, S, 2 * ns), F32), jax.ShapeDtypeStruct((S, 2 * ns), BF16)],
        compiler_params=_cp(("parallel", "parallel")),
    )(h, wst, wst)


def _mm_down_dx_swiglu(dff, w_down, gu, name):
    S, K = dff.shape
    tm, tn = _tile(S, 512), _tile(D_FF, 1408)

    def body(a_ref, b_ref, g_ref, u_ref, o_ref):
        dact = _dot(a_ref[...].astype(BF16), b_ref[...], ((1,), (1,)))
        g, u = g_ref[...], u_ref[...]
        sg = jax.nn.sigmoid(g)
        o_ref[0] = (dact * u * (sg * (1.0 + g * (1.0 - sg)))).astype(o_ref.dtype)
        o_ref[1] = (dact * (g * sg)).astype(o_ref.dtype)

    return pl.pallas_call(
        body, name=name, grid=(S // tm, D_FF // tn),
        in_specs=[pl.BlockSpec((tm, K), lambda i, j: (i, 0)), pl.BlockSpec((tn, K), lambda i, j: (j, 0)),
                  pl.BlockSpec((None, tm, tn), lambda i, j: (0, i, j)), pl.BlockSpec((None, tm, tn), lambda i, j: (1, i, j))],
        out_specs=pl.BlockSpec((2, tm, tn), lambda i, j: (0, i, j)),
        out_shape=jax.ShapeDtypeStruct((2, S, D_FF), BF16),
        compiler_params=_cp(("parallel", "parallel")),
    )(dff, w_down, gu, gu)


def _rspec(tm, w, cb):
    return pl.BlockSpec((tm, w), lambda i: (i, cb))


def _vspec(shape):
    return pl.BlockSpec(shape, lambda i: (0,) * len(shape))


def _row_fwd(fn, name, rows, vecs, outs, tm=256, transposed=()):
    S = rows[0][0].shape[0]
    tm = min(tm, S)
    nin = len(rows) + len(vecs)

    def body(*refs):
        res = fn(*[r[...] for r in refs[:nin]])
        for k, (o_ref, r) in enumerate(zip(refs[nin:], res)):
            o_ref[...] = (r.T if k in transposed else r).astype(o_ref.dtype)

    return pl.pallas_call(
        body, name=name, grid=(S // tm,),
        in_specs=[_rspec(tm, w, cb) for (_, cb, w) in rows] + [_vspec(v.shape) for v in vecs],
        out_specs=[pl.BlockSpec((w, tm), lambda i: (0, i)) if k in transposed else _rspec(tm, w, 0)
                   for k, (w, _) in enumerate(outs)],
        out_shape=[jax.ShapeDtypeStruct((w, S) if k in transposed else (S, w), dt) for k, (w, dt) in enumerate(outs)],
        compiler_params=_cp(("parallel",)),
    )(*[r[0] for r in rows], *vecs)


def _row_bwd(fn, name, rows, vecs, cts, drows, dvecs, tm=256, ddtypes=None, transposed=()):
    S = rows[0][0].shape[0]
    ddtypes = ddtypes or [F32] * len(drows)
    tm = min(tm, S)
    nr, nv, nc = len(rows), len(vecs), len(cts)
    didx = list(drows) + [nr + j for j in dvecs]

    def body(*refs):
        vals = [r[...] for r in refs[:nr + nv]]
        ct = tuple((r[...].T if k in transposed else r[...]).astype(F32)
                   for k, r in enumerate(refs[nr + nv:nr + nv + nc]))
        outs = refs[nr + nv + nc:]

        def g(*d):
            a = list(vals)
            for k, val in zip(didx, d):
                a[k] = val
            return tuple(fn(*a))

        _, vjp = jax.vjp(g, *[vals[k] for k in didx])
        grads = vjp(ct)
        for o, gr in zip(outs[:len(drows)], grads[:len(drows)]):
            o[...] = gr.astype(o.dtype)

        @pl.when(pl.program_id(0) == 0)
        def _():
            for o in outs[len(drows):]:
                o[...] = jnp.zeros_like(o)

        for o, gr in zip(outs[len(drows):], grads[len(drows):]):
            o[...] += gr

    return pl.pallas_call(
        body, name=name, grid=(S // tm,),
        in_specs=[_rspec(tm, w, cb) for (_, cb, w) in rows] + [_vspec(v.shape) for v in vecs]
        + [pl.BlockSpec((w, tm), lambda i: (0, i)) if k in transposed else _rspec(tm, w, cb) for k, (_, cb, w) in enumerate(cts)],
        out_specs=[_rspec(tm, rows[k][2], 0) for k in drows] + [_vspec(vecs[j].shape) for j in dvecs],
        out_shape=[jax.ShapeDtypeStruct((S, rows[k][2]), dt) for k, dt in zip(drows, ddtypes)]
        + [jax.ShapeDtypeStruct(vecs[j].shape, F32) for j in dvecs],
        compiler_params=_cp(("arbitrary",)),
    )(*[r[0] for r in rows], *vecs, *[c[0] for c in cts])


def _rms(x):
    return x * lax.rsqrt(jnp.mean(x * x, axis=-1, keepdims=True) + EPS)


def fn_norm_mod(x, nw, sc, sh):
    return (_rms(x) * nw * (1.0 + sc) + sh,)


def fn_norm_mod_pass(x, nw, sc, sh):
    return (x, _rms(x) * nw * (1.0 + sc) + sh)


def fn_resid_norm(x, d, g, nw, sc, sh):
    xn = x + g * d
    return (xn, _rms(xn) * nw * (1.0 + sc) + sh)


def fn_lat_norm(qa, kva, qw, kvw):
    return (_rms(qa) * qw, _rms(kva) * kvw)


@functools.partial(jax.custom_vjp, nondiff_argnums=(1,))
def _lroll(x, s):
    return pltpu.roll(x, s, 1)


def _lroll_fwd(x, s):
    return pltpu.roll(x, s, 1), None


def _lroll_bwd(s, _, g):
    return (pltpu.roll(g, (LANES - s) % LANES, 1),)


_lroll.defvjp(_lroll_fwd, _lroll_bwd)


def _lane_masks(shape):
    lane = lax.broadcasted_iota(jnp.int32, shape, 1)
    return (lane < NOPE).astype(F32), ((lane >= NOPE) & (lane < QK_DIM)).astype(F32)


def _rope(t, tc, ts1, ts2):
    return t * tc + _lroll(t, 16) * ts1 + _lroll(t, LANES - 16) * ts2


def fn_qk_prep(q, kv, kpe, tc, ts1, ts2, wq, wk):
    mn, mp = _lane_masks((1, LANES))
    mhi = 1.0 - mn

    def head_norm(t, w):
        rn = lax.rsqrt(jnp.sum(t * t * mn, axis=-1, keepdims=True) * (1.0 / NOPE) + EPS)
        rp = lax.rsqrt(jnp.sum(t * t * mp, axis=-1, keepdims=True) * (1.0 / ROPE) + EPS)
        return t * (rn * mn + rp * mp) * w

    kp = _rope(head_norm(_lroll(kpe, NOPE), wk) * mp, tc, ts1, ts2)
    qs, ks, vs = [], [], []
    for h in range(HEADS):
        qs.append(_rope(head_norm(q[:, h * LANES:(h + 1) * LANES], wq), tc, ts1, ts2))
        t = kv[:, h * LANES:(h + 1) * LANES]
        ks.append(head_norm(t, wk) * mn + kp)
        vs.append(_lroll(t, NOPE) * mn + mhi)
    return (jnp.concatenate(qs, axis=1), jnp.concatenate(ks, axis=1), jnp.concatenate(vs, axis=1))


def fn_qk_prep_kt(*args):
    qf, kf, va = fn_qk_prep(*args)
    return (qf, kf, va, kf)


def fn_gated_norm(y, z, w):
    u = y * jax.nn.silu(z)
    half = D_SSD // 2
    return (jnp.concatenate([_rms(u[:, :half]), _rms(u[:, half:])], axis=1) * w,)


def fn_gated_mix(y, z, ao, w):
    return (jnp.concatenate([ao, fn_gated_norm(y, z, w)[0]], axis=1),)


def _final(x1, ff, tgt, g2, name):
    S = x1.shape[0]
    tm = min(256, S)

    def body(x_ref, f_ref, t_ref, g_ref, dx_ref, df_ref, dg_ref, l_ref):
        @pl.when(pl.program_id(0) == 0)
        def _():
            dg_ref[...] = jnp.zeros_like(dg_ref)
            l_ref[...] = jnp.zeros_like(l_ref)

        f = f_ref[...]
        g = g_ref[...]
        e = x_ref[...] + g * f - t_ref[...]
        dx = e * (1.0 / D_MODEL)
        dx_ref[...] = dx
        df_ref[...] = (g * dx).astype(df_ref.dtype)
        dg_ref[...] += jnp.sum(dx * f, axis=0, keepdims=True)
        l_ref[...] += jnp.sum(e * e) * (0.5 / D_MODEL)

    r = _rspec(tm, D_MODEL, 0)
    return pl.pallas_call(
        body, name=name, grid=(S // tm,),
        in_specs=[r, r, r, _vspec((1, D_MODEL))],
        out_specs=[r, r, _vspec((1, D_MODEL)), _vspec((1, LANES))],
        out_shape=[jax.ShapeDtypeStruct((S, D_MODEL), F32), jax.ShapeDtypeStruct((S, D_MODEL), BF16),
                   jax.ShapeDtypeStruct((1, D_MODEL), F32), jax.ShapeDtypeStruct((1, LANES), F32)],
        compiler_params=_cp(("arbitrary",)),
    )(x1, ff, tgt, g2)


def _causal_mask(t):
    r = lax.broadcasted_iota(jnp.int32, (t, t), 0)
    c = lax.broadcasted_iota(jnp.int32, (t, t), 1)
    return c <= r


LOG2E = 1.4426950408889634
EXP2_SCALE = ATT_SCALE * LOG2E
ATT_TQ, ATT_TK = 512, 1024
ATT_BQ, ATT_BK = 1024, 512


def _attn_fwd(qf, kf, va, name):
    S = qf.shape[0]
    T, TK = min(ATT_TQ, S), min(ATT_TK, S)
    nmask = max(1, T // TK)

    def body(q_ref, k_ref, v_ref, o_ref, l_ref):
        i = pl.program_id(1)
        r = lax.broadcasted_iota(jnp.int32, (T, TK), 0)
        c = lax.broadcasted_iota(jnp.int32, (T, TK), 1)
        qs = [q_ref[:, hh * LANES:(hh + 1) * LANES] for hh in range(2)]

        def blk(j, carry, masked):
            off = pl.multiple_of(j * TK, TK)
            out = []
            for hh in range(2):
                m, acc = carry[hh]
                s = _dot(qs[hh], k_ref[pl.ds(off, TK), hh * LANES:(hh + 1) * LANES], ((1,), (1,)))
                if masked:
                    s = jnp.where(c + j * TK <= r + i * T, s, NEG)
                mn = jnp.maximum(m, jnp.max(s, axis=1, keepdims=True))
                p = jnp.exp2((s - mn) * EXP2_SCALE)
                al = jnp.exp2((m - mn) * EXP2_SCALE)
                vj = v_ref[pl.ds(off, TK), hh * LANES:(hh + 1) * LANES]
                out.append((mn, al * acc + _dot(p.astype(BF16), vj, ((1,), (0,)))))
            return tuple(out)

        one = (jnp.full((T, 1), NEG, F32), jnp.zeros((T, LANES), F32))
        nfull = lax.div(i * T, TK)
        carry = lax.fori_loop(0, nfull, lambda j, cr: blk(j, cr, False), (one, one))
        for t in range(nmask):
            carry = blk(nfull + t, carry, True)
        lane = lax.broadcasted_iota(jnp.int32, (1, LANES), 1)
        z = jnp.zeros((T, LANES), F32)
        for hh in range(2):
            m, acc = carry[hh]
            l = acc[:, 64:65]
            o_ref[:, hh * 64:(hh + 1) * 64] = (acc / l)[:, :64]
            z = z + (m * EXP2_SCALE + jnp.log(l) * LOG2E) * (lane == hh).astype(F32)
        l_ref[0] = z.T[0:2, :]

    return pl.pallas_call(
        body, name=name, grid=(HEADS // 2, S // T),
        in_specs=[pl.BlockSpec((T, 256), lambda h, i: (i, h)), pl.BlockSpec((S, 256), lambda h, i: (0, h)),
                  pl.BlockSpec((S, 256), lambda h, i: (0, h))],
        out_specs=[pl.BlockSpec((T, LANES), lambda h, i: (i, h)), pl.BlockSpec((1, 2, T), lambda h, i: (h, 0, i))],
        out_shape=[jax.ShapeDtypeStruct((S, D_SSD), F32), jax.ShapeDtypeStruct((HEADS // 2, 2, S), F32)],
        compiler_params=_cp(("parallel", "parallel")),
    )(qf, kf, va)


def _attn_delta(dmix, ao, name):
    S = ao.shape[0]
    tm = min(512, S)

    def body(d_ref, o_ref, out_ref):
        lane = lax.broadcasted_iota(jnp.int32, (1, LANES), 1)
        lo = (lane < 64).astype(F32)
        for hp in range(HEADS // 2):
            y = d_ref[:, hp * LANES:(hp + 1) * LANES] * o_ref[:, hp * LANES:(hp + 1) * LANES]
            z = (jnp.sum(y * lo, axis=1, keepdims=True) * (lane == 0).astype(F32)
                 + jnp.sum(y * (1.0 - lo), axis=1, keepdims=True) * (lane == 1).astype(F32))
            out_ref[hp] = z.T[0:2, :]

    return pl.pallas_call(
        body, name=name, grid=(S // tm,),
        in_specs=[pl.BlockSpec((tm, D_SSD), lambda i: (i, 0)), pl.BlockSpec((tm, D_SSD), lambda i: (i, 0))],
        out_specs=pl.BlockSpec((HEADS // 2, 2, tm), lambda i: (0, 0, i)),
        out_shape=jax.ShapeDtypeStruct((HEADS // 2, 2, S), F32),
        compiler_params=_cp(("parallel",)),
    )(dmix, ao)


def _attn_bwd(qf, kf, kT, va, do, lse_r, delta_r, name):
    S = qf.shape[0]
    T, TK = min(ATT_BQ, S), min(ATT_BK, S)
    nq = S // T
    nmask = max(1, TK // T)

    def body(q_ref, k_ref, kT_ref, v_ref, do_ref, l_ref, d_ref, dqT_ref, dk_ref, dv_ref):
        j = pl.program_id(1)

        @pl.when(j == 0)
        def _():
            dqT_ref[...] = jnp.zeros_like(dqT_ref)

        r = lax.broadcasted_iota(jnp.int32, (TK, T), 0)
        c = lax.broadcasted_iota(jnp.int32, (TK, T), 1)
        lo = (lax.broadcasted_iota(jnp.int32, (1, LANES), 1) < 64).astype(F32)
        ks = [k_ref[:, hh * LANES:(hh + 1) * LANES] for hh in range(2)]
        vs = [v_ref[:, hh * LANES:(hh + 1) * LANES] for hh in range(2)]
        kTs = [kT_ref[hh * LANES:(hh + 1) * LANES, :] for hh in range(2)]

        def blk(i, carry, masked):
            off = pl.multiple_of(i * T, T)
            dall = do_ref[pl.ds(off, T), :]
            out = []
            for hh in range(2):
                dk, dv = carry[hh]
                q = q_ref[pl.ds(off, T), hh * LANES:(hh + 1) * LANES]
                dop = ((dall if hh == 0 else pltpu.roll(dall, 64, 1)) * lo).astype(BF16)
                lrow = l_ref[0, hh:hh + 1, pl.ds(off, T)]
                drow = d_ref[0, hh:hh + 1, pl.ds(off, T)]
                pT = jnp.exp2(_dot(ks[hh], q, ((1,), (1,))) * EXP2_SCALE - lrow)
                if masked:
                    pT = jnp.where(r + j * TK <= c + i * T, pT, 0.0)
                dpT = _dot(vs[hh], dop, ((1,), (1,)))
                dsT = (pT * (dpT - drow) * ATT_SCALE).astype(BF16)
                dv = dv + _dot(pT.astype(BF16), dop, ((1,), (0,)))
                dk = dk + _dot(dsT, q, ((1,), (0,)))
                dqT_ref[hh * LANES:(hh + 1) * LANES, pl.ds(off, T)] += _dot(kTs[hh], dsT, ((1,), (0,)))
                out.append((dk, dv))
            return tuple(out)

        z = (jnp.zeros((TK, LANES), F32), jnp.zeros((TK, LANES), F32))
        first = lax.div(j * TK, T)
        carry = (z, z)
        for t in range(nmask):
            carry = blk(first + t, carry, True)
        carry = lax.fori_loop(first + nmask, nq, lambda i, cr: blk(i, cr, False), carry)
        for hh in range(2):
            dk_ref[:, hh * LANES:(hh + 1) * LANES] = carry[hh][0]
            dv_ref[:, hh * LANES:(hh + 1) * LANES] = carry[hh][1]

    return pl.pallas_call(
        body, name=name, grid=(HEADS // 2, S // TK),
        in_specs=[pl.BlockSpec((S, 256), lambda h, j: (0, h)), pl.BlockSpec((TK, 256), lambda h, j: (j, h)),
                  pl.BlockSpec((256, TK), lambda h, j: (h, j)), pl.BlockSpec((TK, 256), lambda h, j: (j, h)),
                  pl.BlockSpec((S, LANES), lambda h, j: (0, h)), pl.BlockSpec((1, 2, S), lambda h, j: (h, 0, 0)),
                  pl.BlockSpec((1, 2, S), lambda h, j: (h, 0, 0))],
        out_specs=[pl.BlockSpec((256, S), lambda h, j: (h, 0)), pl.BlockSpec((TK, 256), lambda h, j: (j, h)),
                   pl.BlockSpec((TK, 256), lambda h, j: (j, h))],
        out_shape=[jax.ShapeDtypeStruct((D_MODEL, S), F32), jax.ShapeDtypeStruct((S, D_MODEL), F32),
                   jax.ShapeDtypeStruct((S, D_MODEL), F32)],
        compiler_params=_cp(("parallel", "arbitrary")),
    )(qf, kf, kT, va, do, lse_r, delta_r)


def _shift_down(x, s):
    if s == 0:
        return x
    rows = lax.broadcasted_iota(jnp.int32, x.shape, 0)
    return jnp.where(rows >= s, pltpu.roll(x, s, 0), 0.0)


def _shift_up(x, s):
    if s == 0:
        return x
    n = x.shape[0]
    rows = lax.broadcasted_iota(jnp.int32, x.shape, 0)
    return jnp.where(rows < n - s, pltpu.roll(x, n - s, 0), 0.0)


def _conv_fwd(proj, cvec, name):
    S = proj.shape[0]

    def body(x_ref, c_ref, o_ref):
        x = x_ref[...]
        y = jnp.broadcast_to(c_ref[4:5, :], x.shape)
        for k in range(CONV_TAPS):
            y = y + c_ref[k:k + 1, :] * _shift_down(x, CONV_TAPS - 1 - k)
        o_ref[...] = y * jax.nn.sigmoid(y)

    return pl.pallas_call(
        body, name=name, grid=(D_CONV // LANES,),
        in_specs=[pl.BlockSpec((S, LANES), lambda j: (0, 8 + j)), pl.BlockSpec((8, LANES), lambda j: (0, j))],
        out_specs=pl.BlockSpec((S, LANES), lambda j: (0, j)),
        out_shape=jax.ShapeDtypeStruct((S, D_CONV), F32),
        compiler_params=_cp(("parallel",)),
    )(proj, cvec)


def _conv_bwd(proj, cvec, dact, name):
    S = proj.shape[0]

    def body(x_ref, c_ref, d_ref, dx_ref, dc_ref):
        x = x_ref[...]
        y = jnp.broadcast_to(c_ref[4:5, :], x.shape)
        for k in range(CONV_TAPS):
            y = y + c_ref[k:k + 1, :] * _shift_down(x, CONV_TAPS - 1 - k)
        sg = jax.nn.sigmoid(y)
        dy = d_ref[...] * (sg * (1.0 + y * (1.0 - sg)))
        dx = jnp.zeros_like(x)
        for k in range(CONV_TAPS):
            s = CONV_TAPS - 1 - k
            dx = dx + c_ref[k:k + 1, :] * _shift_up(dy, s)
            dc_ref[k:k + 1, :] = jnp.sum(dy * _shift_down(x, s), axis=0, keepdims=True)
        dx_ref[...] = dx
        dc_ref[4:5, :] = jnp.sum(dy, axis=0, keepdims=True)
        dc_ref[5:8, :] = jnp.zeros((3, LANES), F32)

    return pl.pallas_call(
        body, name=name, grid=(D_CONV // LANES,),
        in_specs=[pl.BlockSpec((S, LANES), lambda j: (0, 8 + j)), pl.BlockSpec((8, LANES), lambda j: (0, j)),
                  pl.BlockSpec((S, LANES), lambda j: (0, j))],
        out_specs=[pl.BlockSpec((S, LANES), lambda j: (0, j)), pl.BlockSpec((8, LANES), lambda j: (0, j))],
        out_shape=[jax.ShapeDtypeStruct((S, D_CONV), F32), jax.ShapeDtypeStruct((8, D_CONV), F32)],
        compiler_params=_cp(("parallel",)),
    )(proj, cvec, dact)


def fn_ssd_chunk(xs, bm, cm, dtr, state, vecs):
    Q = CHUNK
    dt = jax.nn.softplus(dtr + vecs[0:1])
    a = -jnp.exp(vecs[1:2])
    adt = dt * a
    tril = _causal_mask(Q)
    acs = _dot(tril.astype(F32), adt, ((1,), (0,)), lax.Precision.HIGHEST)
    acs_t = acs.T
    alast = acs[Q - 1:Q, :]
    r = lax.broadcasted_iota(jnp.int32, (LANES, D_SSD), 0)
    c = lax.broadcasted_iota(jnp.int32, (LANES, D_SSD), 1)
    spread = (lax.shift_right_logical(c, 6) == r).astype(F32)

    def per_head(v):
        return _dot(v, spread, ((1,), (0,)), lax.Precision.HIGH)

    xdt = xs * per_head(dt)
    ub = (xdt * per_head(jnp.exp(alast - acs))).astype(BF16)
    xdtb = xdt.astype(BF16)
    Bs = [bm[:, g * SSD_N:(g + 1) * SSD_N].astype(BF16) for g in range(2)]
    Cs = [cm[:, g * SSD_N:(g + 1) * SSD_N].astype(BF16) for g in range(2)]
    Gs = [_dot(Cs[g], Bs[g], ((1,), (1,))) for g in range(2)]
    yds, yos, adds = [], [], []
    for h in range(SSD_HEADS):
        g = h // (SSD_HEADS // 2)
        sl = slice(h * SSD_P, (h + 1) * SSD_P)
        L = jnp.exp(jnp.where(tril, acs[:, h:h + 1] - acs_t[h:h + 1, :], -jnp.inf))
        yds.append(_dot((Gs[g] * L).astype(BF16), xdtb[:, sl], ((1,), (0,))))
        yos.append(_dot(Cs[g], state[h].astype(BF16), ((1,), (1,))))
        adds.append(_dot(ub[:, sl], Bs[g], ((0,), (0,))))
    y = jnp.concatenate(yds, axis=1) + jnp.concatenate(yos, axis=1) * per_head(jnp.exp(acs)) + per_head(vecs[2:3]) * xs
    decay = jnp.stack([jnp.broadcast_to(jnp.exp(alast[:, h:h + 1]), (SSD_P, SSD_N)) for h in range(SSD_HEADS)])
    return y, jnp.stack(adds) + state * decay


def _ssd_fwd(xact, proj, svec, name):
    S = xact.shape[0]
    nc = S // CHUNK

    def body(x_ref, dt_ref, v_ref, y_ref, st_ref, state):
        @pl.when(pl.program_id(0) == 0)
        def _():
            state[...] = jnp.zeros_like(state)

        st_ref[0] = state[...]
        x = x_ref[...]
        y, sn = fn_ssd_chunk(x[:, 0:512], x[:, 512:768], x[:, 768:1024], dt_ref[...], state[...], v_ref[...])
        y_ref[...] = y
        state[...] = sn

    return pl.pallas_call(
        body, name=name, grid=(nc,),
        in_specs=[pl.BlockSpec((CHUNK, D_CONV), lambda i: (i, 0)), pl.BlockSpec((CHUNK, LANES), lambda i: (i, 16)),
                  pl.BlockSpec((8, LANES), lambda i: (0, 0))],
        out_specs=[pl.BlockSpec((CHUNK, D_SSD), lambda i: (i, 0)),
                   pl.BlockSpec((1, SSD_HEADS, SSD_P, SSD_N), lambda i: (i, 0, 0, 0))],
        out_shape=[jax.ShapeDtypeStruct((S, D_SSD), F32), jax.ShapeDtypeStruct((nc, SSD_HEADS, SSD_P, SSD_N), F32)],
        scratch_shapes=[pltpu.VMEM((SSD_HEADS, SSD_P, SSD_N), F32)],
        compiler_params=_cp(("arbitrary",)),
    )(xact, proj, svec)


def _ssd_bwd(xact, proj, svec, states, dy, name):
    S = xact.shape[0]
    nc = S // CHUNK

    def body(x_ref, dt_ref, v_ref, st_ref, dy_ref, dx_ref, ddt_ref, dv_ref, dstate):
        @pl.when(pl.program_id(0) == 0)
        def _():
            dstate[...] = jnp.zeros_like(dstate)
            dv_ref[...] = jnp.zeros_like(dv_ref)

        x = x_ref[...]
        _, vjp = jax.vjp(fn_ssd_chunk, x[:, 0:512], x[:, 512:768], x[:, 768:1024], dt_ref[...], st_ref[0], v_ref[...])
        dxs, dbm, dcm, ddt, dst, dvec = vjp((dy_ref[...], dstate[...]))
        dx_ref[:, 0:512] = dxs
        dx_ref[:, 512:768] = dbm
        dx_ref[:, 768:1024] = dcm
        ddt_ref[...] = ddt
        dstate[...] = dst
        dv_ref[...] += dvec

    rev = lambda i: (nc - 1 - i, 0)
    return pl.pallas_call(
        body, name=name, grid=(nc,),
        in_specs=[pl.BlockSpec((CHUNK, D_CONV), rev), pl.BlockSpec((CHUNK, LANES), lambda i: (nc - 1 - i, 16)),
                  pl.BlockSpec((8, LANES), lambda i: (0, 0)),
                  pl.BlockSpec((1, SSD_HEADS, SSD_P, SSD_N), lambda i: (nc - 1 - i, 0, 0, 0)),
                  pl.BlockSpec((CHUNK, D_SSD), rev)],
        out_specs=[pl.BlockSpec((CHUNK, D_CONV), rev), pl.BlockSpec((CHUNK, LANES), rev),
                   pl.BlockSpec((8, LANES), lambda i: (0, 0))],
        out_shape=[jax.ShapeDtypeStruct((S, D_CONV), F32), jax.ShapeDtypeStruct((S, LANES), F32),
                   jax.ShapeDtypeStruct((8, LANES), F32)],
        scratch_shapes=[pltpu.VMEM((SSD_HEADS, SSD_P, SSD_N), F32)],
        compiler_params=_cp(("arbitrary",)),
    )(xact, proj, svec, states, dy)


def _ada_fwd(c_all, w_ada, b_sh, name):
    nb = 1536 // 512

    def body(c_ref, w_ref, b_ref, o_ref):
        ca = jax.nn.silu(c_ref[...]).astype(BF16)
        o_ref[0] = _dot(ca, w_ref[0].astype(BF16), ((1,), (0,))) + b_ref[0]

    return pl.pallas_call(
        body, name=name, grid=(DEPTH, nb),
        in_specs=[pl.BlockSpec((8, D_MODEL), lambda l, j: (0, 0)), pl.BlockSpec((1, D_MODEL, 512), lambda l, j: (l, 0, j)),
                  pl.BlockSpec((1, 1, 512), lambda l, j: (l, 0, j))],
        out_specs=pl.BlockSpec((1, 8, 512), lambda l, j: (l, 0, j)),
        out_shape=jax.ShapeDtypeStruct((DEPTH, 8, 1536), F32),
        compiler_params=_cp(("parallel", "parallel")),
    )(c_all, w_ada, b_sh)


def _ada_bwd(c_all_t, dmod_sh, name):
    nb = 1536 // 512

    def body(c_ref, d_ref, o_ref):
        ca = jax.nn.silu(c_ref[...])
        acc = ca[:, 0:1] * d_ref[0, 0:1, :]
        for b in range(1, 8):
            acc = acc + ca[:, b:b + 1] * d_ref[0, b:b + 1, :]
        o_ref[0] = acc

    return pl.pallas_call(
        body, name=name, grid=(DEPTH, nb),
        in_specs=[pl.BlockSpec((D_MODEL, 8), lambda l, j: (0, 0)), pl.BlockSpec((1, 8, 512), lambda l, j: (l, 0, j))],
        out_specs=pl.BlockSpec((1, D_MODEL, 512), lambda l, j: (l, 0, j)),
        out_shape=jax.ShapeDtypeStruct((DEPTH, D_MODEL, 1536), F32),
        compiler_params=_cp(("parallel", "parallel")),
    )(c_all_t, dmod_sh)


def _rows_tile(rows):
    return next(t for t in (512, 256, 128, 64, 32, 16, 8) if rows % t == 0)


SUM_BLOCKS = 4
ADAM_BLOCKS = 8


def _sum_sibling(gs, ls, ci, name):
    n = len(gs)

    def body(c_ref, *refs):
        for p in range(n):
            refs[2 * n + p][...] = refs[2 * p][...] + refs[2 * p + 1][...]

    in_specs, out_specs, out_shape = [], [], []
    for g in gs:
        _, _, rh, cw = g.shape
        rb = rh // SUM_BLOCKS
        in_specs += [pl.BlockSpec((None, None, rb, cw), lambda s, i, c: (s, c[0], i, 0)),
                     pl.BlockSpec((None, rb, cw), lambda s, i, c: (s, i, 0))]
        out_specs.append(pl.BlockSpec((None, rb, cw), lambda s, i, c: (s, i, 0)))
        out_shape.append(jax.ShapeDtypeStruct((4, rh, cw), F32))
    ops = [a for pair in zip(gs, ls) for a in pair]
    return pl.pallas_call(
        body, name=name,
        grid_spec=pltpu.PrefetchScalarGridSpec(num_scalar_prefetch=1, grid=(4, SUM_BLOCKS), in_specs=in_specs, out_specs=out_specs),
        out_shape=out_shape, compiler_params=_cp(("parallel", "parallel")),
    )(ci.reshape(1).astype(jnp.int32), *ops)


def _sum_chips(cs, lands, chip, ci, name):
    n = len(cs)

    def body(c_ref, *refs):
        for p in range(n):
            a = refs[4 * p:4 * p + 4]
            refs[4 * n + p][...] = ((a[0][...] + a[1][...]) + a[2][...]) + a[3][...]

    in_specs, out_specs, out_shape = [], [], []
    for c in cs:
        _, rh, cw = c.shape
        rb = rh // SUM_BLOCKS
        in_specs.append(pl.BlockSpec((None, rb, cw), lambda i, ch: (ch[0], i, 0)))
        in_specs += [pl.BlockSpec((None, rb, cw), functools.partial(lambda i, ch, k: (k, i, 0), k=k)) for k in range(3)]
        out_specs.append(pl.BlockSpec((None, rb, cw), lambda i, ch: (ch[1], i, 0)))
        out_shape.append(jax.ShapeDtypeStruct((2, rh, cw), F32))
    ops = [a for c, l in zip(cs, lands) for a in (c, l, l, l)]
    return pl.pallas_call(
        body, name=name,
        grid_spec=pltpu.PrefetchScalarGridSpec(num_scalar_prefetch=1, grid=(SUM_BLOCKS,), in_specs=in_specs, out_specs=out_specs),
        out_shape=out_shape, compiler_params=_cp(("parallel",)),
    )(jnp.stack([chip, ci]).astype(jnp.int32), *ops)


def _adam_update(w, m, v, g):
    c1 = 1.0 / (1.0 - ADAM_B1 ** ADAM_STEP)
    c2 = 1.0 / (1.0 - ADAM_B2 ** ADAM_STEP)
    nm = ADAM_B1 * m + (1.0 - ADAM_B1) * g
    nv = ADAM_B2 * v + (1.0 - ADAM_B2) * (g * g)
    return -ADAM_LR * ((nm * c1) / (jnp.sqrt(nv * c2) + ADAM_EPS) + ADAM_WD * w), nm, nv


def _adam_multi(ws, ms, vs, gs, name):
    n = len(ws)
    per = 3 + DEPTH

    def body(*refs):
        layer = pl.program_id(0)
        for p in range(n):
            w, m, v = [refs[per * p + k][...] for k in range(3)]
            g = refs[per * p + 3][...]
            for l in range(1, DEPTH):
                g = jnp.where(layer == l, refs[per * p + 3 + l][...], g)
            d, nm, nv = _adam_update(w, m, v, g)
            for k, val in enumerate((g, d, nm, nv)):
                refs[per * n + 4 * p + k][...] = val

    in_specs, out_specs, out_shape = [], [], []
    for w in ws:
        _, r, cw = w.shape
        if r % (8 * ADAM_BLOCKS) == 0:
            spec = pl.BlockSpec((None, r // ADAM_BLOCKS, cw), lambda l, i: (l, i, 0))
            gspec = pl.BlockSpec((r // ADAM_BLOCKS, cw), lambda l, i: (i, 0))
        else:
            spec = pl.BlockSpec((None, r, cw // ADAM_BLOCKS), lambda l, i: (l, 0, i))
            gspec = pl.BlockSpec((r, cw // ADAM_BLOCKS), lambda l, i: (0, i))
        in_specs += [spec] * 3 + [gspec] * DEPTH
        out_specs += [spec] * 4
        out_shape += [jax.ShapeDtypeStruct(w.shape, F32)] * 4
    ops = [a for w, m, v, g in zip(ws, ms, vs, gs) for a in (w, m, v, *g)]
    res = pl.pallas_call(
        body, name=name, grid=(DEPTH, ADAM_BLOCKS), in_specs=in_specs, out_specs=out_specs, out_shape=out_shape,
        compiler_params=_cp(("parallel", "parallel")),
    )(*ops)
    return res[0::4], res[1::4], res[2::4], res[3::4]


def _adam(w, m, v, parts, name):
    rows, width = w.shape
    bm = min(256, _rows_tile(rows))
    np_ = len(parts)
    c1 = 1.0 / (1.0 - ADAM_B1 ** ADAM_STEP)
    c2 = 1.0 / (1.0 - ADAM_B2 ** ADAM_STEP)

    def body(*refs):
        w_ref, m_ref, v_ref = refs[:3]
        g = refs[3][...]
        for r in refs[4:3 + np_]:
            g = g + r[...]
        g_ref, d_ref, nm_ref, nv_ref = refs[3 + np_:]
        nm = ADAM_B1 * m_ref[...] + (1.0 - ADAM_B1) * g
        nv = ADAM_B2 * v_ref[...] + (1.0 - ADAM_B2) * (g * g)
        g_ref[...] = g
        nm_ref[...] = nm
        nv_ref[...] = nv
        d_ref[...] = -ADAM_LR * ((nm * c1) / (jnp.sqrt(nv * c2) + ADAM_EPS) + ADAM_WD * w_ref[...])

    blk = pl.BlockSpec((bm, width), lambda i: (i, 0))
    return pl.pallas_call(
        body, name=name, grid=(rows // bm,),
        in_specs=[blk, blk, blk] + [pl.BlockSpec((bm, width), functools.partial(lambda i, o: (i + o, 0), o=off // bm))
                                    for (_, off) in parts],
        out_specs=[blk, blk, blk, blk],
        out_shape=[jax.ShapeDtypeStruct((rows, width), F32)] * 4,
        compiler_params=_cp(("parallel",)),
    )(w, m, v, *[p[0] for p in parts])


def _coords():
    return lax.axis_index("x"), lax.axis_index("y"), lax.axis_index("c")


def _other_chips(x, y):
    return [(1 - x, y), (x, 1 - y), (1 - x, 1 - y)]


def _ag8(blk, name):
    m_per, n = blk.shape

    def body(x_ref, out_ref, send_sems, recv_sems, local_sem):
        x, y, c = _coords()
        me, sibling = (x, y, c), (x, y, 1 - c)
        chips = _other_chips(x, y)

        def rows(px, py, pc):
            return out_ref.at[pl.ds((4 * px + 2 * py + pc) * m_per, m_per), :]

        def copy(k, block, to, src=None):
            return pltpu.make_async_remote_copy(
                src_ref=rows(*block) if src is None else src, dst_ref=rows(*block),
                send_sem=send_sems.at[k], recv_sem=recv_sems.at[k], device_id=to, device_id_type=MESH)

        mine = pltpu.make_async_copy(x_ref, rows(*me), local_sem)
        mine.start()
        first = [copy(0, me, sibling, src=x_ref)]
        first += [copy(1 + j, me, (*chip, c), src=x_ref) for j, chip in enumerate(chips)]
        for cp in first:
            cp.start()
        passed = [copy(4 + j, (*chip, c), sibling) for j, chip in enumerate(chips)]
        for j, chip in enumerate(chips):
            copy(1 + j, (*chip, c), me).wait_recv()
            passed[j].start()
        copy(0, sibling, me).wait_recv()
        for j, chip in enumerate(chips):
            copy(4 + j, (*chip, 1 - c), me).wait_recv()
        for cp in first + passed:
            cp.wait_send()
        mine.wait()

    return pl.pallas_call(
        body, name=name,
        out_shape=jax.ShapeDtypeStruct((8 * m_per, n), blk.dtype),
        in_specs=[pl.BlockSpec(memory_space=pltpu.VMEM)], out_specs=pl.BlockSpec(memory_space=pltpu.VMEM),
        scratch_shapes=[pltpu.SemaphoreType.DMA((7,)), pltpu.SemaphoreType.DMA((7,)), pltpu.SemaphoreType.DMA],
    )(blk)


HBM_SPEC = pl.BlockSpec(memory_space=pltpu.HBM)
SEM_SPEC = pl.BlockSpec(memory_space=pltpu.SEMAPHORE)
EFFECT = pltpu.SideEffectType.DATAFLOW_SIDE_EFFECTING


def _remote(src, dst, send_sem, recv_sem, to):
    return pltpu.make_async_remote_copy(src_ref=src, dst_ref=dst, send_sem=send_sem, recv_sem=recv_sem,
                                        device_id=to, device_id_type=MESH)


def _ag_list(shards, name):
    n = len(shards)

    def body(*refs):
        sh, out = refs[:n], refs[n:2 * n]
        send_sems, recv_sems = refs[2 * n:]
        x, y, c = _coords()
        sibling = (x, y, 1 - c)
        chips = _other_chips(x, y)
        first = [_remote(sh[p].at[c], out[p].at[2 * x + y, c], send_sems.at[6 * p + j], recv_sems.at[6 * p + j], (px, py, c))
                 for p in range(n) for j, (px, py) in enumerate(chips)]
        for cp in first:
            cp.start()
        passed = []
        for j, (px, py) in enumerate(chips):
            for p in range(n):
                got = out[p].at[2 * px + py, c]
                _remote(got, got, send_sems.at[6 * p + j], recv_sems.at[6 * p + j], (x, y, c)).wait_recv()
                cp = _remote(got, got, send_sems.at[6 * p + 3 + j], recv_sems.at[6 * p + 3 + j], sibling)
                cp.start()
                passed.append(cp)
        for j, (px, py) in enumerate(chips):
            for p in range(n):
                got = out[p].at[2 * px + py, 1 - c]
                _remote(got, got, send_sems.at[6 * p + 3 + j], recv_sems.at[6 * p + 3 + j], (x, y, c)).wait_recv()
        for cp in first + passed:
            cp.wait_send()

    return pl.pallas_call(
        body, name=name,
        out_shape=[jax.ShapeDtypeStruct((4,) + s.shape, s.dtype) for s in shards],
        in_specs=[pl.BlockSpec(memory_space=pl.ANY)] * n, out_specs=[pl.BlockSpec(memory_space=pl.ANY)] * n,
        scratch_shapes=[pltpu.SemaphoreType.DMA((6 * n,)), pltpu.SemaphoreType.DMA((6 * n,))],
    )(*shards)


def _ag_direct_copies(sh, land, send_sems, recv_sems, starting):
    x, y, c = _coords()
    return [_remote(sh[p], land[p].at[2 * x + y] if starting else land[p].at[2 * px + py],
                    send_sems.at[3 * p + j], recv_sems.at[3 * p + j], (px, py, c))
            for p in range(len(sh)) for j, (px, py) in enumerate(_other_chips(x, y))]


def _ag_direct_start(shards, name):
    n = len(shards)

    def body(*refs):
        for cp in _ag_direct_copies(refs[:n], refs[n:2 * n], refs[2 * n], refs[2 * n + 1], True):
            cp.start()
        token = refs[4 * n + 2]
        token[...] = jnp.zeros_like(token)

    lands = [pltpu.with_memory_space_constraint(lax.empty((4,) + s.shape, s.dtype), pltpu.HBM) for s in shards]
    res = pl.pallas_call(
        body, name=name,
        out_shape=(pltpu.SemaphoreType.DMA((3 * n,)), pltpu.SemaphoreType.DMA((3 * n,)))
        + tuple(pltpu.HBM(s.shape, s.dtype) for s in shards) + tuple(pltpu.HBM(l.shape, l.dtype) for l in lands)
        + (jax.ShapeDtypeStruct((8, LANES), F32),),
        in_specs=(HBM_SPEC,) * (2 * n), out_specs=(SEM_SPEC, SEM_SPEC) + (HBM_SPEC,) * (2 * n) + (pl.BlockSpec(memory_space=pltpu.VMEM),),
        input_output_aliases={i: 2 + i for i in range(2 * n)},
        compiler_params=pltpu.CompilerParams(has_side_effects=EFFECT),
    )(*[pltpu.with_memory_space_constraint(s, pltpu.HBM) for s in shards], *lands)
    return res[0], res[1], res[2:2 + n], res[2 + n:2 + 2 * n], res[2 + 2 * n]


def _ag_direct_wait(send_sems, recv_sems, sh_thru, land_thru, after, name):
    n = len(sh_thru)

    def body(*refs):
        sh, land = refs[:n], refs[n:2 * n]
        for cp in _ag_direct_copies(sh, land, refs[2 * n], refs[2 * n + 1], False):
            cp.wait_send()
            cp.wait_recv()

    res = pl.pallas_call(
        body, name=name,
        out_shape=tuple(pltpu.HBM(s.shape, s.dtype) for s in sh_thru) + tuple(pltpu.HBM(l.shape, l.dtype) for l in land_thru),
        in_specs=(HBM_SPEC,) * (2 * n) + (SEM_SPEC, SEM_SPEC, pl.BlockSpec(memory_space=pl.ANY)),
        out_specs=(HBM_SPEC,) * (2 * n), input_output_aliases={i: i for i in range(2 * n)},
        compiler_params=pltpu.CompilerParams(has_side_effects=EFFECT),
    )(*sh_thru, *land_thru, send_sems, recv_sems, after)
    return res[n:]


def _rs_sibling_list(gs, name):
    n = len(gs)

    def body(*refs):
        g, out, send_sems, recv_sems = refs[:n], refs[n:2 * n], refs[2 * n], refs[2 * n + 1]
        x, y, c = _coords()
        cps = [_remote(g[p].at[s, 1 - c], out[p].at[s], send_sems.at[4 * p + s], recv_sems.at[4 * p + s], (x, y, 1 - c))
               for p in range(n) for s in range(4)]
        for cp in cps:
            cp.start()
        for cp in cps:
            cp.wait_recv()
        for cp in cps:
            cp.wait_send()

    return pl.pallas_call(
        body, name=name,
        out_shape=[jax.ShapeDtypeStruct((4,) + g.shape[2:], g.dtype) for g in gs],
        in_specs=[pl.BlockSpec(memory_space=pl.ANY)] * n, out_specs=[pl.BlockSpec(memory_space=pl.ANY)] * n,
        scratch_shapes=[pltpu.SemaphoreType.DMA((4 * n,)), pltpu.SemaphoreType.DMA((4 * n,))],
    )(*gs)


def _rs_chips_copies(cs, land, send_sems, recv_sems):
    x, y, c = _coords()
    return [_remote(cs[p].at[2 * px + py], land[p].at[j], send_sems.at[3 * p + j], recv_sems.at[3 * p + j], (px, py, c))
            for p in range(len(cs)) for j, (px, py) in enumerate(_other_chips(x, y))]


def _rs_chips_start(cs, name):
    n = len(cs)

    def body(*refs):
        for cp in _rs_chips_copies(refs[:n], refs[n:2 * n], refs[2 * n], refs[2 * n + 1]):
            cp.start()
        token = refs[4 * n + 2]
        token[...] = jnp.zeros_like(token)

    lands = [pltpu.with_memory_space_constraint(lax.empty((3,) + c.shape[1:], c.dtype), pltpu.HBM) for c in cs]
    res = pl.pallas_call(
        body, name=name,
        out_shape=(pltpu.SemaphoreType.DMA((3 * n,)), pltpu.SemaphoreType.DMA((3 * n,)))
        + tuple(pltpu.HBM(c.shape, c.dtype) for c in cs) + tuple(pltpu.HBM(l.shape, l.dtype) for l in lands)
        + (jax.ShapeDtypeStruct((8, LANES), F32),),
        in_specs=(HBM_SPEC,) * (2 * n), out_specs=(SEM_SPEC, SEM_SPEC) + (HBM_SPEC,) * (2 * n) + (pl.BlockSpec(memory_space=pltpu.VMEM),),
        input_output_aliases={i: 2 + i for i in range(2 * n)},
        compiler_params=pltpu.CompilerParams(has_side_effects=EFFECT),
    )(*[pltpu.with_memory_space_constraint(c, pltpu.HBM) for c in cs], *lands)
    return res[0], res[1], res[2:2 + n], res[2 + n:2 + 2 * n], res[2 + 2 * n]


def _rs_chips_wait(send_sems, recv_sems, cs_thru, land_thru, after, name):
    n = len(cs_thru)

    def body(*refs):
        for cp in _rs_chips_copies(refs[:n], refs[n:2 * n], refs[2 * n], refs[2 * n + 1]):
            cp.wait_send()
            cp.wait_recv()

    res = pl.pallas_call(
        body, name=name,
        out_shape=tuple(pltpu.HBM(c.shape, c.dtype) for c in cs_thru) + tuple(pltpu.HBM(l.shape, l.dtype) for l in land_thru),
        in_specs=(HBM_SPEC,) * (2 * n) + (SEM_SPEC, SEM_SPEC, pl.BlockSpec(memory_space=pl.ANY)),
        out_specs=(HBM_SPEC,) * (2 * n), input_output_aliases={i: i for i in range(2 * n)},
        compiler_params=pltpu.CompilerParams(has_side_effects=EFFECT),
    )(*cs_thru, *land_thru, send_sems, recv_sems, after)
    return res[:n], res[n:]


def _swap_list(ghs, name):
    n = len(ghs)

    def body(*refs):
        g, out, send_sems, recv_sems = refs[:n], refs[n:2 * n], refs[2 * n], refs[2 * n + 1]
        x, y, c = _coords()
        cps = [_remote(g[p].at[c], out[p].at[c], send_sems.at[p], recv_sems.at[p], (x, y, 1 - c)) for p in range(n)]
        for cp in cps:
            cp.start()
        for p in range(n):
            _remote(g[p].at[c], out[p].at[1 - c], send_sems.at[p], recv_sems.at[p], (x, y, 1 - c)).wait_recv()
        for cp in cps:
            cp.wait_send()

    return pl.pallas_call(
        body, name=name,
        out_shape=[jax.ShapeDtypeStruct(g.shape, g.dtype) for g in ghs],
        in_specs=[pl.BlockSpec(memory_space=pl.ANY)] * n, out_specs=[pl.BlockSpec(memory_space=pl.ANY)] * n,
        input_output_aliases={p: p for p in range(n)},
        scratch_shapes=[pltpu.SemaphoreType.DMA((n,)), pltpu.SemaphoreType.DMA((n,))],
    )(*ghs)


def _pad_win(w):
    return jnp.concatenate([w[:, :416], jnp.zeros((w.shape[0], 96), w.dtype), w[:, 416:1952],
                            w[:, 1952:1960], jnp.zeros((w.shape[0], 120), w.dtype)], axis=1)


def _unpad_win(g):
    return jnp.concatenate([g[:, :416], g[:, 512:2048], g[:, 2048:2056]], axis=1)


def _pad_wq(w):
    return jnp.pad(w.reshape(Q_LORA, HEADS, QK_DIM), ((0, 0), (0, 0), (0, LANES - QK_DIM))).reshape(Q_LORA, HEADS * LANES)


def _unpad_wq(g):
    return g.reshape(Q_LORA, HEADS, LANES)[:, :, :QK_DIM].reshape(Q_LORA, HEADS * QK_DIM)


def _cols_to_shards(a):
    r, c4 = a.shape
    return a.reshape(r, 4, c4 // 4).transpose(1, 0, 2)


def _shards_to_cols(a):
    _, r, c = a.shape
    return a.transpose(1, 0, 2).reshape(r, 4 * c)


def _pack_small(tree):
    parts = []
    for l in range(DEPTH):
        for (n, k) in SMALL:
            parts.append(jnp.pad(tree[n][l].reshape(-1), (0, -k % LANES)))
    flat = jnp.concatenate(parts)
    return jnp.pad(flat, (0, SMALL_ROWS * LANES - flat.shape[0])).reshape(SMALL_ROWS, LANES)


def _unpack_small(buf):
    flat = buf.reshape(-1)
    out = {n: [] for (n, _) in SMALL}
    o = 0
    for l in range(DEPTH):
        for (n, k) in SMALL:
            out[n].append(flat[o:o + k])
            o += k + (-k % LANES)
    return {n: jnp.stack(v) for n, v in out.items()}


def _vec(v, width=LANES):
    return jnp.pad(v.reshape(1, -1), ((0, 0), (0, width - v.shape[-1])))


def kernel(x, c, positions, norm1_w, norm2_w, w_ada, b_ada, w_in, q_a_norm_w, w_q_up, kv_a_norm_w, w_kv_up, q_nope_norm_w, q_pe_norm_w, k_nope_norm_w, k_pe_norm_w, conv_w, conv_b, dt_bias, a_log, d_skip, ssd_norm_w, w_out, w_gate_up, w_down, loss_target, m_norm1_w, m_norm2_w, m_w_ada, m_b_ada, m_w_in, m_q_a_norm_w, m_w_q_up, m_kv_a_norm_w, m_w_kv_up, m_q_nope_norm_w, m_q_pe_norm_w, m_k_nope_norm_w, m_k_pe_norm_w, m_conv_w, m_conv_b, m_dt_bias, m_a_log, m_d_skip, m_ssd_norm_w, m_w_out, m_w_gate_up, m_w_down, v_norm1_w, v_norm2_w, v_w_ada, v_b_ada, v_w_in, v_q_a_norm_w, v_w_q_up, v_kv_a_norm_w, v_w_kv_up, v_q_nope_norm_w, v_q_pe_norm_w, v_k_nope_norm_w, v_k_pe_norm_w, v_conv_w, v_conv_b, v_dt_bias, v_a_log, v_d_skip, v_ssd_norm_w, v_w_out, v_w_gate_up, v_w_down):
    W = dict(zip(WEIGHTS, (norm1_w, norm2_w, w_ada, b_ada, w_in, q_a_norm_w, w_q_up, kv_a_norm_w, w_kv_up, q_nope_norm_w, q_pe_norm_w, k_nope_norm_w, k_pe_norm_w, conv_w, conv_b, dt_bias, a_log, d_skip, ssd_norm_w, w_out, w_gate_up, w_down)))
    M = dict(zip(WEIGHTS, (m_norm1_w, m_norm2_w, m_w_ada, m_b_ada, m_w_in, m_q_a_norm_w, m_w_q_up, m_kv_a_norm_w, m_w_kv_up, m_q_nope_norm_w, m_q_pe_norm_w, m_k_nope_norm_w, m_k_pe_norm_w, m_conv_w, m_conv_b, m_dt_bias, m_a_log, m_d_skip, m_ssd_norm_w, m_w_out, m_w_gate_up, m_w_down)))
    V = dict(zip(WEIGHTS, (v_norm1_w, v_norm2_w, v_w_ada, v_b_ada, v_w_in, v_q_a_norm_w, v_w_q_up, v_kv_a_norm_w, v_w_kv_up, v_q_nope_norm_w, v_q_pe_norm_w, v_k_nope_norm_w, v_k_pe_norm_w, v_conv_w, v_conv_b, v_dt_bias, v_a_log, v_d_skip, v_ssd_norm_w, v_w_out, v_w_gate_up, v_w_down)))
    S = x.shape[1]
    xi, yi, ci = _coords()
    chip = 2 * xi + yi
    dev = 2 * chip + ci
    x0 = x[0]
    tgt = loss_target[0]

    inv_freq = 1.0 / (ROPE_THETA ** (jnp.arange(0, ROPE, 2, dtype=F32) / ROPE))
    ang = positions[0].astype(F32)[:, None] * inv_freq
    cos, sin = jnp.cos(ang), jnp.sin(ang)
    z16, z32, z64 = jnp.zeros((S, 16), F32), jnp.zeros((S, 32), F32), jnp.zeros((S, 64), F32)
    tab_c = jnp.concatenate([jnp.ones((S, 64), F32), cos, cos, z32], axis=1)
    tab_s1 = jnp.concatenate([z64, z16, sin, z32], axis=1)
    tab_s2 = jnp.concatenate([z64, -sin, z16, z32], axis=1)

    blk0 = jnp.concatenate([c.reshape(-1), W['conv_w'].reshape(-1)]).reshape(24, LANES)
    g0 = _ag8(blk0, "ag_c_conv").reshape(8, 24 * LANES)
    c_all = g0[:, :D_MODEL]
    conv_full = g0[0::2, D_MODEL:].reshape(4, DEPTH, CONV_TAPS, 256).transpose(1, 2, 0, 3).reshape(DEPTH, CONV_TAPS, D_CONV)

    sh = [{n: W[n][l].astype(BF16) for n in BIG} for l in range(DEPTH)]
    got_first = _ag_list([sh[0][n].reshape(2, sh[0][n].shape[0] // 2, sh[0][n].shape[1]) for n in FIRST], "ag_w0_first")
    to_operand = dict(w_in=lambda a: _pad_win(_shards_to_cols(a)), w_q_up=lambda a: _pad_wq(_shards_to_cols(a)),
                      w_kv_up=_shards_to_cols, w_out=lambda a: a.reshape(D_MODEL, D_MODEL), w_gate_up=lambda a: a,
                      w_down=lambda a: a.reshape(D_FF, D_MODEL))

    def layer_weights(names, gathered, own):
        return {n: to_operand[n](lax.dynamic_update_slice_in_dim(a.reshape(4, -1, a.shape[-1]), own[n][None], chip, axis=0))
                for n, a in zip(names, gathered)}

    LW = [layer_weights(FIRST, got_first, sh[0]), None]

    b_sh = lax.dynamic_slice_in_dim(W['b_ada'], chip * 1536, 1536, axis=1).reshape(DEPTH, 1, 1536)
    mod_sh = _ada_fwd(c_all, W['w_ada'], b_sh, "ada_fwd")
    g1 = _ag8(mod_sh.reshape(192, LANES), "ag_mod").reshape(8, DEPTH, 8, 1536)
    mod_all = g1[0::2].transpose(1, 2, 0, 3).reshape(DEPTH, 8, 6 * D_MODEL)
    mod = lax.dynamic_index_in_dim(mod_all, dev, axis=1, keepdims=False)
    mod, rest0 = lax.optimization_barrier((mod, [sh[0][n] for n in REST]))
    ag0 = _ag_direct_start(rest0, "ag_w0_rest_start")

    def mvec(l, k):
        return mod[l, k * D_MODEL:(k + 1) * D_MODEL].reshape(1, D_MODEL)

    def small(name, l, width=None):
        v = W[name][l]
        return _vec(v, width or v.shape[-1])

    def wq_vec(l):
        return _vec(jnp.concatenate([W['q_nope_norm_w'][l], W['q_pe_norm_w'][l]]))

    def wk_vec(l):
        return _vec(jnp.concatenate([W['k_nope_norm_w'][l], W['k_pe_norm_w'][l]]))

    def conv_vec(l):
        return jnp.concatenate([conv_full[l], W['conv_b'][l].reshape(1, D_CONV), jnp.zeros((3, D_CONV), F32)], axis=0)

    def ssd_vec(l):
        return jnp.concatenate([_vec(W['dt_bias'][l]), _vec(W['a_log'][l]), _vec(W['d_skip'][l]), jnp.zeros((5, LANES), F32)], axis=0)

    sv = []
    xcur = x0
    h1 = _row_fwd(fn_norm_mod, "norm_mod_f", [(x0, 0, D_MODEL)], [small('norm1_w', 0) + ag0[4][0, 0], mvec(0, 1), mvec(0, 0)],
                  [(D_MODEL, BF16)])[0]
    fin = None
    ag_first = None
    ag_rest = ag0
    for l in range(DEPTH):
        if l == 1:
            LW[1] = layer_weights(FIRST, _ag_direct_wait(*ag_first[:4], xcur, "ag_w1_first_wait"), sh[1])
        lw = LW[l]
        t = dict(xcur=xcur, h1=h1)
        t['proj'] = proj = _mm(h1, lw['w_in'], 'nn', f"mm_in_{l}")
        t['qa_n'], t['kva_n'] = _row_fwd(fn_lat_norm, f"lat_norm_f{l}", [(proj, 0, 256), (proj, 2, 128)],
                                         [small('q_a_norm_w', l), small('kv_a_norm_w', l)], [(256, BF16), (128, BF16)])
        t['q'] = _mm(t['qa_n'], lw['w_q_up'], 'nn', f"mm_q_{l}")
        t['kv'] = _mm(t['kva_n'], lw['w_kv_up'], 'nn', f"mm_kv_{l}")
        t['qf'], t['kf'], t['vv'], t['kT'] = _row_fwd(
            fn_qk_prep_kt, f"qk_prep_f{l}",
            [(t['q'], 0, 1024), (t['kv'], 0, 1024), (proj, 3, 128), (tab_c, 0, 128), (tab_s1, 0, 128), (tab_s2, 0, 128)],
            [wq_vec(l), wk_vec(l)], [(1024, BF16), (1024, BF16), (1024, BF16), (1024, BF16)], transposed=(3,))
        t['ao'], t['lse'] = _attn_fwd(t['qf'], t['kf'], t['vv'], f"attn_f{l}")
        t['xact'] = _conv_fwd(proj, conv_vec(l), f"conv_f{l}")
        t['y'], t['states'] = _ssd_fwd(t['xact'], proj, ssd_vec(l), f"ssd_f{l}")
        tie = 0.0
        t['ao'], t['y'] = lax.optimization_barrier((t['ao'], t['y']))
        rest = list(_ag_direct_wait(*ag_rest[:4], t['y'], f"ag_w{l}_rest_wait"))
        if l == 0:
            rest, sh1f, sh1r = lax.optimization_barrier((rest, [sh[1][n] for n in FIRST], [sh[1][n] for n in REST]))
            ag_first = _ag_direct_start(sh1f, "ag_w1_first_start")
            ag_rest = _ag_direct_start(sh1r, "ag_w1_rest_start")
            tie = ag_first[4][0, 0] + ag_rest[4][0, 0]
        lw.update(layer_weights(REST, rest, sh[l]))
        t['mix'] = _row_fwd(fn_gated_mix, f"gated_f{l}", [(t['y'], 0, 512), (proj, 1, 512), (t['ao'], 0, 512)],
                            [small('ssd_norm_w', l) + tie], [(1024, BF16)])[0]
        t['mo'] = _mm(t['mix'], lw['w_out'], 'nn', f"mm_out_{l}")
        t['x1'], t['h2'] = _row_fwd(fn_resid_norm, f"resid_mid_f{l}", [(xcur, 0, D_MODEL), (t['mo'], 0, D_MODEL)],
                                    [mvec(l, 2), small('norm2_w', l), mvec(l, 4), mvec(l, 3)],
                                    [(D_MODEL, F32), (D_MODEL, BF16)])
        t['gu'], t['act'] = _mm_gu_swiglu(t['h2'], lw['w_gate_up'], f"mm_gu_{l}")
        t['ff'] = _mm(t['act'], lw['w_down'], 'nn', f"mm_down_{l}")
        if l + 1 < DEPTH:
            xcur, h1 = _row_fwd(fn_resid_norm, f"resid_end_f{l}", [(t['x1'], 0, D_MODEL), (t['ff'], 0, D_MODEL)],
                                [mvec(l, 5), small('norm1_w', l + 1), mvec(l + 1, 1), mvec(l + 1, 0)],
                                [(D_MODEL, F32), (D_MODEL, BF16)])
        else:
            fin = _final(t['x1'], t['ff'], tgt, mvec(l, 5), "final_loss")
        sv.append(t)

    dx1, dff, dg2_last, loss_acc = fin
    gfull = {n: [None] * DEPTH for n in BIG}
    gsm = {n: [None] * DEPTH for (n, _) in SMALL}
    dmod = [[None] * 6 for _ in range(DEPTH)]
    dmod[DEPTH - 1][5] = dg2_last
    grad_x = None
    pending = []

    def rs_begin(l, names, tag):
        g4 = [gfull[n][l].reshape(4, 2, gfull[n][l].shape[1] // 2, gfull[n][l].shape[2]) for n in names]
        sib = _rs_sibling_list(g4, f"rs_sibling_{tag}")
        cs = _sum_sibling(g4, sib, ci, f"sum_sibling_{tag}")
        h = _rs_chips_start(cs, f"rs_chips_start_{tag}")
        pending.append((l, names, h))
        return h[4][0, 0]

    tie_l1 = tie_l0a = 0.0

    for l in reversed(range(DEPTH)):
        t = sv[l]
        lw = LW[l]
        proj = t['proj']
        dgu = _mm_down_dx_swiglu(dff, lw['w_down'], t['gu'], f"mm_down_dx{l}")
        gfull['w_down'][l] = _mm(t['act'], dff, 'tn', f"mm_down_dw{l}").reshape(4, D_FF // 4, D_MODEL)
        dh2 = _mm(dgu, lw['w_gate_up'], 'nt', f"mm_gu_dx{l}", stack='b')
        gfull['w_gate_up'][l] = _mm(t['h2'], dgu, 'tn', f"mm_gu_dw{l}", stack='out')
        if l == 0:
            tie_l0a = rs_begin(0, EARLY, "l0a")
        dxc, dmo, dmod[l][2], gsm['norm2_w'][l], dmod[l][4], dmod[l][3] = _row_bwd(
            fn_resid_norm, f"resid_mid_b{l}", [(t['xcur'], 0, D_MODEL), (t['mo'], 0, D_MODEL)],
            [mvec(l, 2) + (tie_l1 if l == 0 else 0.0), small('norm2_w', l), mvec(l, 4), mvec(l, 3)],
            [(dx1, 0, D_MODEL), (dh2, 0, D_MODEL)], [0, 1], [0, 1, 2, 3], ddtypes=[F32, BF16])
        dmix = _mm(dmo, lw['w_out'], 'nt', f"mm_out_dx{l}")
        gfull['w_out'][l] = _mm(t['mix'], dmo, 'tn', f"mm_out_dw{l}").reshape(4, D_MODEL // 4, D_MODEL)
        dy, dz, gsm['ssd_norm_w'][l] = _row_bwd(fn_gated_norm, f"gated_b{l}", [(t['y'], 0, 512), (proj, 1, 512)],
                                                [small('ssd_norm_w', l)], [(dmix, 1, 512)], [0, 1], [0])
        dxact, ddt, dsv = _ssd_bwd(t['xact'], proj, ssd_vec(l) + (tie_l0a if l == 0 else 0.0), t['states'], dy, f"ssd_b{l}")
        gsm['dt_bias'][l], gsm['a_log'][l], gsm['d_skip'][l] = dsv[0, :8], dsv[1, :8], dsv[2, :8]
        dxbc, dcv = _conv_bwd(proj, conv_vec(l), dxact, f"conv_b{l}")
        gsm['conv_w'][l] = dcv[:CONV_TAPS]
        gsm['conv_b'][l] = dcv[CONV_TAPS]
        delta_r = _attn_delta(dmix, t['ao'], f"attn_delta{l}")
        dqT, dkf, dvv = _attn_bwd(t['qf'], t['kf'], t['kT'], t['vv'], dmix, t['lse'], delta_r, f"attn_b{l}")
        dq, dkv, dkpe, dwq, dwk = _row_bwd(
            fn_qk_prep, f"qk_prep_b{l}",
            [(t['q'], 0, 1024), (t['kv'], 0, 1024), (proj, 3, 128), (tab_c, 0, 128), (tab_s1, 0, 128), (tab_s2, 0, 128)],
            [wq_vec(l), wk_vec(l)], [(dqT, 0, 1024), (dkf, 0, 1024), (dvv, 0, 1024)], [0, 1, 2], [0, 1],
            ddtypes=[BF16, BF16, F32], transposed=(0,))
        gsm['q_nope_norm_w'][l], gsm['q_pe_norm_w'][l] = dwq[0, :NOPE], dwq[0, NOPE:QK_DIM]
        gsm['k_nope_norm_w'][l], gsm['k_pe_norm_w'][l] = dwk[0, :NOPE], dwk[0, NOPE:QK_DIM]
        dqa_n = _mm(dq, lw['w_q_up'], 'nt', f"mm_q_dx{l}")
        gfull['w_q_up'][l] = _cols_to_shards(_unpad_wq(_mm(t['qa_n'], dq, 'tn', f"mm_q_dw{l}")))
        dkva_n = _mm(dkv, lw['w_kv_up'], 'nt', f"mm_kv_dx{l}")
        gfull['w_kv_up'][l] = _cols_to_shards(_mm(t['kva_n'], dkv, 'tn', f"mm_kv_dw{l}"))
        dqa, dkva, dqw, dkvw = _row_bwd(fn_lat_norm, f"lat_norm_b{l}", [(proj, 0, 256), (proj, 2, 128)],
                                        [small('q_a_norm_w', l), small('kv_a_norm_w', l)],
                                        [(dqa_n, 0, 256), (dkva_n, 0, 128)], [0, 1], [0, 1])
        gsm['q_a_norm_w'][l], gsm['kv_a_norm_w'][l] = dqw[0], dkvw[0]
        dproj = jnp.concatenate([dqa, dkva, dkpe, dz, dxbc, ddt], axis=1).astype(BF16)
        dh1 = _mm(dproj, lw['w_in'], 'nt', f"mm_in_dx{l}")
        gfull['w_in'][l] = _cols_to_shards(_unpad_win(_mm(t['h1'], dproj, 'tn', f"mm_in_dw{l}")))
        if l > 0:
            p = sv[l - 1]
            dx1, dff, dmod[l - 1][5], gsm['norm1_w'][l], dmod[l][1], dmod[l][0] = _row_bwd(
                fn_resid_norm, f"resid_end_b{l - 1}", [(p['x1'], 0, D_MODEL), (p['ff'], 0, D_MODEL)],
                [mvec(l - 1, 5), small('norm1_w', l), mvec(l, 1), mvec(l, 0)], [(dxc, 0, D_MODEL), (dh1, 0, D_MODEL)],
                [0, 1], [0, 1, 2, 3], ddtypes=[F32, BF16])
            tie_l1 = rs_begin(l, BIG, f"l{l}")
        else:
            grad_x, gsm['norm1_w'][l], dmod[l][1], dmod[l][0] = _row_bwd(
                fn_norm_mod_pass, "norm_mod_b", [(x0, 0, D_MODEL)], [small('norm1_w', 0), mvec(0, 1), mvec(0, 0)],
                [(dxc, 0, D_MODEL), (dh1, 0, D_MODEL)], [0], [0, 1, 2])
        for n in ('norm1_w', 'norm2_w', 'ssd_norm_w'):
            gsm[n][l] = gsm[n][l][0]

    for l in range(DEPTH):
        gsm['b_ada'][l] = jnp.concatenate([d[0] for d in dmod[l]])
    sm_part = _pack_small({n: jnp.stack(v) for n, v in gsm.items()}).at[SMALL_ROWS - 1, 0].set(loss_acc[0, 0])
    sm_all = _ag8(sm_part, "ag_small")
    loss = jnp.sum(sm_all.reshape(8, SMALL_ROWS, LANES)[:, SMALL_ROWS - 1, 0])
    sm_all, late = lax.optimization_barrier((sm_all, [gfull[n][0] for n in BIG[2:]]))
    for n, g in zip(BIG[2:], late):
        gfull[n][0] = g
    tie_l0b = rs_begin(0, BIG[2:], "l0b")

    def with_conv(tree):
        wide = lax.dynamic_update_slice_in_dim(jnp.zeros((DEPTH, CONV_TAPS, D_CONV), F32), tree['conv_w'], chip * 256, axis=2)
        return {**tree, 'conv_w': wide}

    g_sm, d_sm, m_sm, v_sm = _adam(_pack_small(with_conv(W)) + tie_l0b, _pack_small(with_conv(M)), _pack_small(with_conv(V)),
                                   [(sm_all, d * SMALL_ROWS) for d in range(8)], "adam_small")
    out_small = [_unpack_small(b) for b in (g_sm, d_sm, m_sm, v_sm)]
    for o in out_small:
        o['conv_w'] = lax.dynamic_slice_in_dim(o['conv_w'].reshape(DEPTH, CONV_TAPS, D_CONV), chip * 256, 256, axis=2)

    dmod_all = sm_all.reshape(8, SMALL_ROWS * LANES)
    per_layer = sum(k + (-k % LANES) for (_, k) in SMALL)
    dmod_sh = jnp.stack([lax.dynamic_slice_in_dim(dmod_all[:, l * per_layer:l * per_layer + 6 * D_MODEL], chip * 1536, 1536, axis=1)
                         for l in range(DEPTH)])
    g_ada = _ada_bwd(c_all.T, dmod_sh, "ada_bwd")
    ada = _adam(W['w_ada'].reshape(DEPTH * D_MODEL, 1536), M['w_ada'].reshape(DEPTH * D_MODEL, 1536),
                V['w_ada'].reshape(DEPTH * D_MODEL, 1536), [(g_ada.reshape(DEPTH * D_MODEL, 1536), 0)], "adam_ada")
    out_ada = [a.reshape(DEPTH, D_MODEL, 1536) for a in ada]

    keys, cs_all, land_all = [], [], []
    for (l, names, (send_sems, recv_sems, cs_thru, land_thru, _)) in pending:
        cs, lands = _rs_chips_wait(send_sems, recv_sems, cs_thru, land_thru, ada[3], f"rs_chips_wait_l{l}{len(names)}")
        keys += [(l, n) for n in names]
        cs_all += list(cs)
        land_all += list(lands)
    gboth = _swap_list(_sum_chips(cs_all, land_all, chip, ci, "sum_chips"), "swap_halves")
    gshard = {k: g.reshape(2 * g.shape[1], g.shape[2]) for k, g in zip(keys, gboth)}

    def natural(n, a):
        return jnp.swapaxes(a, -1, -2) if n == 'w_in' else a

    res = _adam_multi([natural(n, W[n]) for n in BIG], [natural(n, M[n]) for n in BIG], [natural(n, V[n]) for n in BIG],
                      [[natural(n, gshard[(l, n)]) for l in range(DEPTH)] for n in BIG], "adam_big")
    out_big = [{n: natural(n, a) for n, a in zip(BIG, o)} for o in res]

    outs = [loss, grad_x[None]]
    for k in range(4):
        for n in WEIGHTS:
            if n == 'w_ada':
                outs.append(out_ada[k])
            elif n in BIG:
                outs.append(out_big[k][n])
            else:
                outs.append(out_small[k][n])
    return tuple(outs)
```

```python
import functools

import jax
import jax.numpy as jnp
from jax import lax
from jax.experimental import pallas as pl
from jax.experimental.pallas import tpu as pltpu

F32 = jnp.float32
BF16 = jnp.bfloat16
MESH = pl.DeviceIdType.MESH

D_MODEL = 1024
DEPTH = 2
HEADS = 8
NOPE = 64
ROPE = 32
QK_DIM = NOPE + ROPE
Q_LORA = 256
KV_LORA = 128
SSD_HEADS = 8
SSD_P = 64
SSD_N = 128
CHUNK = 128
CONV_TAPS = 4
D_SSD = 512
D_CONV = 1024
D_FF = 2816
D_IN = 1960
D_IN_PAD = 2176
EPS = 1e-6
ROPE_THETA = 10000.0
ATT_SCALE = QK_DIM ** -0.5
NEG = -1e30
LANES = 128
VMEM_LIMIT = 48 * 1024 * 1024

ADAM_LR, ADAM_B1, ADAM_B2, ADAM_EPS, ADAM_WD, ADAM_STEP = 0.001, 0.9, 0.999, 1e-08, 0.01, 10

WEIGHTS = ['norm1_w', 'norm2_w', 'w_ada', 'b_ada', 'w_in', 'q_a_norm_w', 'w_q_up', 'kv_a_norm_w', 'w_kv_up',
           'q_nope_norm_w', 'q_pe_norm_w', 'k_nope_norm_w', 'k_pe_norm_w', 'conv_w', 'conv_b', 'dt_bias', 'a_log',
           'd_skip', 'ssd_norm_w', 'w_out', 'w_gate_up', 'w_down']
BIG = ['w_down', 'w_gate_up', 'w_out', 'w_kv_up', 'w_q_up', 'w_in']
EARLY = BIG[:2]
FIRST = BIG[3:]
REST = BIG[:3]
SMALL = [('b_ada', 6144), ('conv_w', 4096), ('norm1_w', 1024), ('norm2_w', 1024), ('conv_b', 1024), ('ssd_norm_w', 512),
         ('q_a_norm_w', 256), ('kv_a_norm_w', 128), ('q_nope_norm_w', 64), ('q_pe_norm_w', 32),
         ('k_nope_norm_w', 64), ('k_pe_norm_w', 32), ('dt_bias', 8), ('a_log', 8), ('d_skip', 8)]
SMALL_ROWS = 240


def _cp(sem=None, **kw):
    return pltpu.CompilerParams(dimension_semantics=sem, vmem_limit_bytes=VMEM_LIMIT, **kw)


def _dot(a, b, dims, prec=None):
    return lax.dot_general(a, b, (dims, ((), ())), preferred_element_type=F32, precision=prec)


def _tile(dim, target):
    best = 0
    for t in range(LANES, min(dim, target) + 1, LANES):
        if dim % t == 0:
            best = t
    if best < 256 and dim <= 2304:
        return dim
    return best


def _mm(a, b, mode, name, out_dtype=F32, stack=None):
    ns = None
    halves = (a if mode == 'nt' else b).ndim == 3 and stack is not None and not (stack == 'b' and mode == 'nn')
    if stack == 'b':
        ns = b.shape[2]
        if mode == 'nn':
            (M, K), N = a.shape, 4 * ns
        else:
            M, K, N = a.shape[-2], 4 * ns, b.shape[1]
    elif mode == 'nn':
        (M, K), (_, N) = a.shape, b.shape
    elif mode == 'nt':
        (M, K), (N, _) = a.shape, b.shape
    else:
        (K, M), N = a.shape, (2 * b.shape[2] if halves else b.shape[1])
    if stack == 'out':
        ns = N // 4
    tm, tn, tk = _tile(M, 1408 if mode == 'tn' else 1024), _tile(N, 1408), _tile(K, 1408)
    if stack == 'b' and mode == 'nt':
        tk = ns
    elif stack is not None:
        tn = ns
    nk = K // tk
    dims = {'nn': ((1,), (0,)), 'nt': ((1,), (1,)), 'tn': ((0,), (0,))}[mode]

    def body(a_ref, b_ref, o_ref, *acc):
        part = _dot(a_ref[...].astype(BF16), b_ref[...].astype(BF16), dims)
        if nk == 1:
            o_ref[...] = part.astype(o_ref.dtype)
            return
        k = pl.program_id(2)

        @pl.when(k == 0)
        def _():
            acc[0][...] = part

        @pl.when(k > 0)
        def _():
            acc[0][...] += part

        @pl.when(k == nk - 1)
        def _():
            o_ref[...] = acc[0][...].astype(o_ref.dtype)

    a_spec = pl.BlockSpec((tk, tm), lambda i, j, k: (k, i)) if mode == 'tn' else pl.BlockSpec((tm, tk), lambda i, j, k: (i, k))
    b_spec = pl.BlockSpec((tn, tk), lambda i, j, k: (j, k)) if mode == 'nt' else pl.BlockSpec((tk, tn), lambda i, j, k: (k, j))
    o_spec, o_shape = pl.BlockSpec((tm, tn), lambda i, j, k: (i, j)), (M, N)
    if stack == 'b':
        b_spec = (pl.BlockSpec((None, tn, ns), lambda i, j, k: (k, j, 0)) if mode == 'nt'
                  else pl.BlockSpec((None, tk, ns), lambda i, j, k: (j, k, 0)))
    if stack == 'out':
        o_spec, o_shape = pl.BlockSpec((None, tm, ns), lambda i, j, k: (j, i, 0)), (4, M, ns)
    if halves and mode == 'nt':
        a_spec = pl.BlockSpec((None, tm, ns), lambda i, j, k: (lax.div(k, 2), i, lax.rem(k, 2)))
    if halves and mode == 'tn':
        b_spec = pl.BlockSpec((None, tk, ns), lambda i, j, k: (lax.div(j, 2), k, lax.rem(j, 2)))
    return pl.pallas_call(
        body, name=name, grid=(M // tm, N // tn, nk),
        in_specs=[a_spec, b_spec], out_specs=o_spec,
        out_shape=jax.ShapeDtypeStruct(o_shape, out_dtype),
        scratch_shapes=[pltpu.VMEM((tm, tn), F32)] if nk > 1 else [],
        compiler_params=_cp(("parallel", "parallel", "arbitrary")),
    )(a, b)


def _mm_gu_swiglu(h, wst, name):
    S, K = h.shape
    ns = wst.shape[2]
    tm = _tile(S, 512)

    def body(a_ref, bg_ref, bu_ref, gu_ref, act_ref):
        a = a_ref[...]
        g = _dot(a, bg_ref[...], ((1,), (0,)))
        u = _dot(a, bu_ref[...], ((1,), (0,)))
        gu_ref[0] = g
        gu_ref[1] = u
        act_ref[...] = (g * jax.nn.sigmoid(g) * u).astype(act_ref.dtype)

    return pl.pallas_call(
        body, name=name, grid=(S // tm, 2),
        in_specs=[pl.BlockSpec((tm, K), lambda i, j: (i, 0)), pl.BlockSpec((None, K, ns), lambda i, j: (j, 0, 0)),
                  pl.BlockSpec((None, K, ns), lambda i, j: (j + 2, 0, 0))],
        out_specs=[pl.BlockSpec((2, tm, ns), lambda i, j: (0, i, j)), pl.BlockSpec((tm, ns), lambda i, j: (i, j))],
        out_shape=[jax.ShapeDtypeStruct((2, S, 2 * ns), F32), jax.ShapeDtypeStruct((S, 2 * ns), BF16)],
        compiler_params=_cp(("parallel", "parallel")),
    )(h, wst, wst)


def _mm_down_dx_swiglu(dff, w_down, gu, name):
    S, K = dff.shape
    tm, tn = _tile(S, 512), _tile(D_FF, 1408)

    def body(a_ref, b_ref, g_ref, u_ref, o_ref):
        dact = _dot(a_ref[...].astype(BF16), b_ref[...], ((1,), (1,)))
        g, u = g_ref[...], u_ref[...]
        sg = jax.nn.sigmoid(g)
        o_ref[0] = (dact * u * (sg * (1.0 + g * (1.0 - sg)))).astype(o_ref.dtype)
        o_ref[1] = (dact * (g * sg)).astype(o_ref.dtype)

    return pl.pallas_call(
        body, name=name, grid=(S // tm, D_FF // tn),
        in_specs=[pl.BlockSpec((tm, K), lambda i, j: (i, 0)), pl.BlockSpec((tn, K), lambda i, j: (j, 0)),
                  pl.BlockSpec((None, tm, tn), lambda i, j: (0, i, j)), pl.BlockSpec((None, tm, tn), lambda i, j: (1, i, j))],
        out_specs=pl.BlockSpec((2, tm, tn), lambda i, j: (0, i, j)),
        out_shape=jax.ShapeDtypeStruct((2, S, D_FF), BF16),
        compiler_params=_cp(("parallel", "parallel")),
    )(dff, w_down, gu, gu)


def _rspec(tm, w, cb):
    return pl.BlockSpec((tm, w), lambda i: (i, cb))


def _vspec(shape):
    return pl.BlockSpec(shape, lambda i: (0,) * len(shape))


def _row_fwd(fn, name, rows, vecs, outs, tm=256, transposed=()):
    S = rows[0][0].shape[0]
    tm = min(tm, S)
    nin = len(rows) + len(vecs)

    def body(*refs):
        res = fn(*[r[...] for r in refs[:nin]])
        for k, (o_ref, r) in enumerate(zip(refs[nin:], res)):
            o_ref[...] = (r.T if k in transposed else r).astype(o_ref.dtype)

    return pl.pallas_call(
        body, name=name, grid=(S // tm,),
        in_specs=[_rspec(tm, w, cb) for (_, cb, w) in rows] + [_vspec(v.shape) for v in vecs],
        out_specs=[pl.BlockSpec((w, tm), lambda i: (0, i)) if k in transposed else _rspec(tm, w, 0)
                   for k, (w, _) in enumerate(outs)],
        out_shape=[jax.ShapeDtypeStruct((w, S) if k in transposed else (S, w), dt) for k, (w, dt) in enumerate(outs)],
        compiler_params=_cp(("parallel",)),
    )(*[r[0] for r in rows], *vecs)


def _row_bwd(fn, name, rows, vecs, cts, drows, dvecs, tm=256, ddtypes=None, transposed=()):
    S = rows[0][0].shape[0]
    ddtypes = ddtypes or [F32] * len(drows)
    tm = min(tm, S)
    nr, nv, nc = len(rows), len(vecs), len(cts)
    didx = list(drows) + [nr + j for j in dvecs]

    def body(*refs):
        vals = [r[...] for r in refs[:nr + nv]]
        ct = tuple((r[...].T if k in transposed else r[...]).astype(F32)
                   for k, r in enumerate(refs[nr + nv:nr + nv + nc]))
        outs = refs[nr + nv + nc:]

        def g(*d):
            a = list(vals)
            for k, val in zip(didx, d):
                a[k] = val
            return tuple(fn(*a))

        _, vjp = jax.vjp(g, *[vals[k] for k in didx])
        grads = vjp(ct)
        for o, gr in zip(outs[:len(drows)], grads[:len(drows)]):
            o[...] = gr.astype(o.dtype)

        @pl.when(pl.program_id(0) == 0)
        def _():
            for o in outs[len(drows):]:
                o[...] = jnp.zeros_like(o)

        for o, gr in zip(outs[len(drows):], grads[len(drows):]):
            o[...] += gr

    return pl.pallas_call(
        body, name=name, grid=(S // tm,),
        in_specs=[_rspec(tm, w, cb) for (_, cb, w) in rows] + [_vspec(v.shape) for v in vecs]
        + [pl.BlockSpec((w, tm), lambda i: (0, i)) if k in transposed else _rspec(tm, w, cb) for k, (_, cb, w) in enumerate(cts)],
        out_specs=[_rspec(tm, rows[k][2], 0) for k in drows] + [_vspec(vecs[j].shape) for j in dvecs],
        out_shape=[jax.ShapeDtypeStruct((S, rows[k][2]), dt) for k, dt in zip(drows, ddtypes)]
        + [jax.ShapeDtypeStruct(vecs[j].shape, F32) for j in dvecs],
        compiler_params=_cp(("arbitrary",)),
    )(*[r[0] for r in rows], *vecs, *[c[0] for c in cts])


def _rms(x):
    return x * lax.rsqrt(jnp.mean(x * x, axis=-1, keepdims=True) + EPS)


def fn_norm_mod(x, nw, sc, sh):
    return (_rms(x) * nw * (1.0 + sc) + sh,)


def fn_norm_mod_pass(x, nw, sc, sh):
    return (x, _rms(x) * nw * (1.0 + sc) + sh)


def fn_resid_norm(x, d, g, nw, sc, sh):
    xn = x + g * d
    return (xn, _rms(xn) * nw * (1.0 + sc) + sh)


def fn_lat_norm(qa, kva, qw, kvw):
    return (_rms(qa) * qw, _rms(kva) * kvw)


@functools.partial(jax.custom_vjp, nondiff_argnums=(1,))
def _lroll(x, s):
    return pltpu.roll(x, s, 1)


def _lroll_fwd(x, s):
    return pltpu.roll(x, s, 1), None


def _lroll_bwd(s, _, g):
    return (pltpu.roll(g, (LANES - s) % LANES, 1),)


_lroll.defvjp(_lroll_fwd, _lroll_bwd)


def _lane_masks(shape):
    lane = lax.broadcasted_iota(jnp.int32, shape, 1)
    return (lane < NOPE).astype(F32), ((lane >= NOPE) & (lane < QK_DIM)).astype(F32)


def _rope(t, tc, ts1, ts2):
    return t * tc + _lroll(t, 16) * ts1 + _lroll(t, LANES - 16) * ts2


def fn_qk_prep(q, kv, kpe, tc, ts1, ts2, wq, wk):
    mn, mp = _lane_masks((1, LANES))
    mhi = 1.0 - mn

    def head_norm(t, w):
        rn = lax.rsqrt(jnp.sum(t * t * mn, axis=-1, keepdims=True) * (1.0 / NOPE) + EPS)
        rp = lax.rsqrt(jnp.sum(t * t * mp, axis=-1, keepdims=True) * (1.0 / ROPE) + EPS)
        return t * (rn * mn + rp * mp) * w

    kp = _rope(head_norm(_lroll(kpe, NOPE), wk) * mp, tc, ts1, ts2)
    qs, ks, vs = [], [], []
    for h in range(HEADS):
        qs.append(_rope(head_norm(q[:, h * LANES:(h + 1) * LANES], wq), tc, ts1, ts2))
        t = kv[:, h * LANES:(h + 1) * LANES]
        ks.append(head_norm(t, wk) * mn + kp)
        vs.append(_lroll(t, NOPE) * mn + mhi)
    return (jnp.concatenate(qs, axis=1), jnp.concatenate(ks, axis=1), jnp.concatenate(vs, axis=1))


def fn_qk_prep_kt(*args):
    qf, kf, va = fn_qk_prep(*args)
    return (qf, kf, va, kf)


def fn_gated_norm(y, z, w):
    u = y * jax.nn.silu(z)
    half = D_SSD // 2
    return (jnp.concatenate([_rms(u[:, :half]), _rms(u[:, half:])], axis=1) * w,)


def fn_gated_mix(y, z, ao, w):
    return (jnp.concatenate([ao, fn_gated_norm(y, z, w)[0]], axis=1),)


def _final(x1, ff, tgt, g2, name):
    S = x1.shape[0]
    tm = min(256, S)

    def body(x_ref, f_ref, t_ref, g_ref, dx_ref, df_ref, dg_ref, l_ref):
        @pl.when(pl.program_id(0) == 0)
        def _():
            dg_ref[...] = jnp.zeros_like(dg_ref)
            l_ref[...] = jnp.zeros_like(l_ref)

        f = f_ref[...]
        g = g_ref[...]
        e = x_ref[...] + g * f - t_ref[...]
        dx = e * (1.0 / D_MODEL)
        dx_ref[...] = dx
        df_ref[...] = (g * dx).astype(df_ref.dtype)
        dg_ref[...] += jnp.sum(dx * f, axis=0, keepdims=True)
        l_ref[...] += jnp.sum(e * e) * (0.5 / D_MODEL)

    r = _rspec(tm, D_MODEL, 0)
    return pl.pallas_call(
        body, name=name, grid=(S // tm,),
        in_specs=[r, r, r, _vspec((1, D_MODEL))],
        out_specs=[r, r, _vspec((1, D_MODEL)), _vspec((1, LANES))],
        out_shape=[jax.ShapeDtypeStruct((S, D_MODEL), F32), jax.ShapeDtypeStruct((S, D_MODEL), BF16),
                   jax.ShapeDtypeStruct((1, D_MODEL), F32), jax.ShapeDtypeStruct((1, LANES), F32)],
        compiler_params=_cp(("arbitrary",)),
    )(x1, ff, tgt, g2)


def _causal_mask(t):
    r = lax.broadcasted_iota(jnp.int32, (t, t), 0)
    c = lax.broadcasted_iota(jnp.int32, (t, t), 1)
    return c <= r


LOG2E = 1.4426950408889634
EXP2_SCALE = ATT_SCALE * LOG2E
ATT_TQ, ATT_TK = 512, 1024
ATT_BQ, ATT_BK = 1024, 512


def _attn_fwd(qf, kf, va, name):
    S = qf.shape[0]
    T, TK = min(ATT_TQ, S), min(ATT_TK, S)
    nmask = max(1, T // TK)

    def body(q_ref, k_ref, v_ref, o_ref, l_ref):
        i = pl.program_id(1)
        r = lax.broadcasted_iota(jnp.int32, (T, TK), 0)
        c = lax.broadcasted_iota(jnp.int32, (T, TK), 1)
        qs = [q_ref[:, hh * LANES:(hh + 1) * LANES] for hh in range(2)]

        def blk(j, carry, masked):
            off = pl.multiple_of(j * TK, TK)
            out = []
            for hh in range(2):
                m, acc = carry[hh]
                s = _dot(qs[hh], k_ref[pl.ds(off, TK), hh * LANES:(hh + 1) * LANES], ((1,), (1,)))
                if masked:
                    s = jnp.where(c + j * TK <= r + i * T, s, NEG)
                mn = jnp.maximum(m, jnp.max(s, axis=1, keepdims=True))
                p = jnp.exp2((s - mn) * EXP2_SCALE)
                al = jnp.exp2((m - mn) * EXP2_SCALE)
                vj = v_ref[pl.ds(off, TK), hh * LANES:(hh + 1) * LANES]
                out.append((mn, al * acc + _dot(p.astype(BF16), vj, ((1,), (0,)))))
            return tuple(out)

        one = (jnp.full((T, 1), NEG, F32), jnp.zeros((T, LANES), F32))
        nfull = lax.div(i * T, TK)
        carry = lax.fori_loop(0, nfull, lambda j, cr: blk(j, cr, False), (one, one))
        for t in range(nmask):
            carry = blk(nfull + t, carry, True)
        lane = lax.broadcasted_iota(jnp.int32, (1, LANES), 1)
        z = jnp.zeros((T, LANES), F32)
        for hh in range(2):
            m, acc = carry[hh]
            l = acc[:, 64:65]
            o_ref[:, hh * 64:(hh + 1) * 64] = (acc / l)[:, :64]
            z = z + (m * EXP2_SCALE + jnp.log(l) * LOG2E) * (lane == hh).astype(F32)
        l_ref[0] = z.T[0:2, :]

    return pl.pallas_call(
        body, name=name, grid=(HEADS // 2, S // T),
        in_specs=[pl.BlockSpec((T, 256), lambda h, i: (i, h)), pl.BlockSpec((S, 256), lambda h, i: (0, h)),
                  pl.BlockSpec((S, 256), lambda h, i: (0, h))],
        out_specs=[pl.BlockSpec((T, LANES), lambda h, i: (i, h)), pl.BlockSpec((1, 2, T), lambda h, i: (h, 0, i))],
        out_shape=[jax.ShapeDtypeStruct((S, D_SSD), F32), jax.ShapeDtypeStruct((HEADS // 2, 2, S), F32)],
        compiler_params=_cp(("parallel", "parallel")),
    )(qf, kf, va)


def _attn_delta(dmix, ao, name):
    S = ao.shape[0]
    tm = min(512, S)

    def body(d_ref, o_ref, out_ref):
        lane = lax.broadcasted_iota(jnp.int32, (1, LANES), 1)
        lo = (lane < 64).astype(F32)
        for hp in range(HEADS // 2):
            y = d_ref[:, hp * LANES:(hp + 1) * LANES] * o_ref[:, hp * LANES:(hp + 1) * LANES]
            z = (jnp.sum(y * lo, axis=1, keepdims=True) * (lane == 0).astype(F32)
                 + jnp.sum(y * (1.0 - lo), axis=1, keepdims=True) * (lane == 1).astype(F32))
            out_ref[hp] = z.T[0:2, :]

    return pl.pallas_call(
        body, name=name, grid=(S // tm,),
        in_specs=[pl.BlockSpec((tm, D_SSD), lambda i: (i, 0)), pl.BlockSpec((tm, D_SSD), lambda i: (i, 0))],
        out_specs=pl.BlockSpec((HEADS // 2, 2, tm), lambda i: (0, 0, i)),
        out_shape=jax.ShapeDtypeStruct((HEADS // 2, 2, S), F32),
        compiler_params=_cp(("parallel",)),
    )(dmix, ao)


def _attn_bwd(qf, kf, kT, va, do, lse_r, delta_r, name):
    S = qf.shape[0]
    T, TK = min(ATT_BQ, S), min(ATT_BK, S)
    nq = S // T
    nmask = max(1, TK // T)

    def body(q_ref, k_ref, kT_ref, v_ref, do_ref, l_ref, d_ref, dqT_ref, dk_ref, dv_ref):
        j = pl.program_id(1)

        @pl.when(j == 0)
        def _():
            dqT_ref[...] = jnp.zeros_like(dqT_ref)

        r = lax.broadcasted_iota(jnp.int32, (TK, T), 0)
        c = lax.broadcasted_iota(jnp.int32, (TK, T), 1)
        lo = (lax.broadcasted_iota(jnp.int32, (1, LANES), 1) < 64).astype(F32)
        ks = [k_ref[:, hh * LANES:(hh + 1) * LANES] for hh in range(2)]
        vs = [v_ref[:, hh * LANES:(hh + 1) * LANES] for hh in range(2)]
        kTs = [kT_ref[hh * LANES:(hh + 1) * LANES, :] for hh in range(2)]

        def blk(i, carry, masked):
            off = pl.multiple_of(i * T, T)
            dall = do_ref[pl.ds(off, T), :]
            out = []
            for hh in range(2):
                dk, dv = carry[hh]
                q = q_ref[pl.ds(off, T), hh * LANES:(hh + 1) * LANES]
                dop = ((dall if hh == 0 else pltpu.roll(dall, 64, 1)) * lo).astype(BF16)
                lrow = l_ref[0, hh:hh + 1, pl.ds(off, T)]
                drow = d_ref[0, hh:hh + 1, pl.ds(off, T)]
                pT = jnp.exp2(_dot(ks[hh], q, ((1,), (1,))) * EXP2_SCALE - lrow)
                if masked:
                    pT = jnp.where(r + j * TK <= c + i * T, pT, 0.0)
                dpT = _dot(vs[hh], dop, ((1,), (1,)))
                dsT = (pT * (dpT - drow) * ATT_SCALE).astype(BF16)
                dv = dv + _dot(pT.astype(BF16), dop, ((1,), (0,)))
                dk = dk + _dot(dsT, q, ((1,), (0,)))
                dqT_ref[hh * LANES:(hh + 1) * LANES, pl.ds(off, T)] += _dot(kTs[hh], dsT, ((1,), (0,)))
                out.append((dk, dv))
            return tuple(out)

        z = (jnp.zeros((TK, LANES), F32), jnp.zeros((TK, LANES), F32))
        first = lax.div(j * TK, T)
        carry = (z, z)
        for t in range(nmask):
            carry = blk(first + t, carry, True)
        carry = lax.fori_loop(first + nmask, nq, lambda i, cr: blk(i, cr, False), carry)
        for hh in range(2):
            dk_ref[:, hh * LANES:(hh + 1) * LANES] = carry[hh][0]
            dv_ref[:, hh * LANES:(hh + 1) * LANES] = carry[hh][1]

    return pl.pallas_call(
        body, name=name, grid=(HEADS // 2, S // TK),
        in_specs=[pl.BlockSpec((S, 256), lambda h, j: (0, h)), pl.BlockSpec((TK, 256), lambda h, j: (j, h)),
                  pl.BlockSpec((256, TK), lambda h, j: (h, j)), pl.BlockSpec((TK, 256), lambda h, j: (j, h)),
                  pl.BlockSpec((S, LANES), lambda h, j: (0, h)), pl.BlockSpec((1, 2, S), lambda h, j: (h, 0, 0)),
                  pl.BlockSpec((1, 2, S), lambda h, j: (h, 0, 0))],
        out_specs=[pl.BlockSpec((256, S), lambda h, j: (h, 0)), pl.BlockSpec((TK, 256), lambda h, j: (j, h)),
                   pl.BlockSpec((TK, 256), lambda h, j: (j, h))],
        out_shape=[jax.ShapeDtypeStruct((D_MODEL, S), F32), jax.ShapeDtypeStruct((S, D_MODEL), F32),
                   jax.ShapeDtypeStruct((S, D_MODEL), F32)],
        compiler_params=_cp(("parallel", "arbitrary")),
    )(qf, kf, kT, va, do, lse_r, delta_r)


def _shift_down(x, s):
    if s == 0:
        return x
    rows = lax.broadcasted_iota(jnp.int32, x.shape, 0)
    return jnp.where(rows >= s, pltpu.roll(x, s, 0), 0.0)


def _shift_up(x, s):
    if s == 0:
        return x
    n = x.shape[0]
    rows = lax.broadcasted_iota(jnp.int32, x.shape, 0)
    return jnp.where(rows < n - s, pltpu.roll(x, n - s, 0), 0.0)


def _conv_fwd(proj, cvec, name):
    S = proj.shape[0]

    def body(x_ref, c_ref, o_ref):
        x = x_ref[...]
        y = jnp.broadcast_to(c_ref[4:5, :], x.shape)
        for k in range(CONV_TAPS):
            y = y + c_ref[k:k + 1, :] * _shift_down(x, CONV_TAPS - 1 - k)
        o_ref[...] = y * jax.nn.sigmoid(y)

    return pl.pallas_call(
        body, name=name, grid=(D_CONV // LANES,),
        in_specs=[pl.BlockSpec((S, LANES), lambda j: (0, 8 + j)), pl.BlockSpec((8, LANES), lambda j: (0, j))],
        out_specs=pl.BlockSpec((S, LANES), lambda j: (0, j)),
        out_shape=jax.ShapeDtypeStruct((S, D_CONV), F32),
        compiler_params=_cp(("parallel",)),
    )(proj, cvec)


def _conv_bwd(proj, cvec, dact, name):
    S = proj.shape[0]

    def body(x_ref, c_ref, d_ref, dx_ref, dc_ref):
        x = x_ref[...]
        y = jnp.broadcast_to(c_ref[4:5, :], x.shape)
        for k in range(CONV_TAPS):
            y = y + c_ref[k:k + 1, :] * _shift_down(x, CONV_TAPS - 1 - k)
        sg = jax.nn.sigmoid(y)
        dy = d_ref[...] * (sg * (1.0 + y * (1.0 - sg)))
        dx = jnp.zeros_like(x)
        for k in range(CONV_TAPS):
            s = CONV_TAPS - 1 - k
            dx = dx + c_ref[k:k + 1, :] * _shift_up(dy, s)
            dc_ref[k:k + 1, :] = jnp.sum(dy * _shift_down(x, s), axis=0, keepdims=True)
        dx_ref[...] = dx
        dc_ref[4:5, :] = jnp.sum(dy, axis=0, keepdims=True)
        dc_ref[5:8, :] = jnp.zeros((3, LANES), F32)

    return pl.pallas_call(
        body, name=name, grid=(D_CONV // LANES,),
        in_specs=[pl.BlockSpec((S, LANES), lambda j: (0, 8 + j)), pl.BlockSpec((8, LANES), lambda j: (0, j)),
                  pl.BlockSpec((S, LANES), lambda j: (0, j))],
        out_specs=[pl.BlockSpec((S, LANES), lambda j: (0, j)), pl.BlockSpec((8, LANES), lambda j: (0, j))],
        out_shape=[jax.ShapeDtypeStruct((S, D_CONV), F32), jax.ShapeDtypeStruct((8, D_CONV), F32)],
        compiler_params=_cp(("parallel",)),
    )(proj, cvec, dact)


def fn_ssd_chunk(xs, bm, cm, dtr, state, vecs):
    Q = CHUNK
    dt = jax.nn.softplus(dtr + vecs[0:1])
    a = -jnp.exp(vecs[1:2])
    adt = dt * a
    tril = _causal_mask(Q)
    acs = _dot(tril.astype(F32), adt, ((1,), (0,)), lax.Precision.HIGHEST)
    acs_t = acs.T
    alast = acs[Q - 1:Q, :]
    r = lax.broadcasted_iota(jnp.int32, (LANES, D_SSD), 0)
    c = lax.broadcasted_iota(jnp.int32, (LANES, D_SSD), 1)
    spread = (lax.shift_right_logical(c, 6) == r).astype(F32)

    def per_head(v):
        return _dot(v, spread, ((1,), (0,)), lax.Precision.HIGH)

    xdt = xs * per_head(dt)
    ub = (xdt * per_head(jnp.exp(alast - acs))).astype(BF16)
    xdtb = xdt.astype(BF16)
    Bs = [bm[:, g * SSD_N:(g + 1) * SSD_N].astype(BF16) for g in range(2)]
    Cs = [cm[:, g * SSD_N:(g + 1) * SSD_N].astype(BF16) for g in range(2)]
    Gs = [_dot(Cs[g], Bs[g], ((1,), (1,))) for g in range(2)]
    yds, yos, adds = [], [], []
    for h in range(SSD_HEADS):
        g = h // (SSD_HEADS // 2)
        sl = slice(h * SSD_P, (h + 1) * SSD_P)
        L = jnp.exp(jnp.where(tril, acs[:, h:h + 1] - acs_t[h:h + 1, :], -jnp.inf))
        yds.append(_dot((Gs[g] * L).astype(BF16), xdtb[:, sl], ((1,), (0,))))
        yos.append(_dot(Cs[g], state[h].astype(BF16), ((1,), (1,))))
        adds.append(_dot(ub[:, sl], Bs[g], ((0,), (0,))))
    y = jnp.concatenate(yds, axis=1) + jnp.concatenate(yos, axis=1) * per_head(jnp.exp(acs)) + per_head(vecs[2:3]) * xs
    decay = jnp.stack([jnp.broadcast_to(jnp.exp(alast[:, h:h + 1]), (SSD_P, SSD_N)) for h in range(SSD_HEADS)])
    return y, jnp.stack(adds) + state * decay


def _ssd_fwd(xact, proj, svec, name):
    S = xact.shape[0]
    nc = S // CHUNK

    def body(x_ref, dt_ref, v_ref, y_ref, st_ref, state):
        @pl.when(pl.program_id(0) == 0)
        def _():
            state[...] = jnp.zeros_like(state)

        st_ref[0] = state[...]
        x = x_ref[...]
        y, sn = fn_ssd_chunk(x[:, 0:512], x[:, 512:768], x[:, 768:1024], dt_ref[...], state[...], v_ref[...])
        y_ref[...] = y
        state[...] = sn

    return pl.pallas_call(
        body, name=name, grid=(nc,),
        in_specs=[pl.BlockSpec((CHUNK, D_CONV), lambda i: (i, 0)), pl.BlockSpec((CHUNK, LANES), lambda i: (i, 16)),
                  pl.BlockSpec((8, LANES), lambda i: (0, 0))],
        out_specs=[pl.BlockSpec((CHUNK, D_SSD), lambda i: (i, 0)),
                   pl.BlockSpec((1, SSD_HEADS, SSD_P, SSD_N), lambda i: (i, 0, 0, 0))],
        out_shape=[jax.ShapeDtypeStruct((S, D_SSD), F32), jax.ShapeDtypeStruct((nc, SSD_HEADS, SSD_P, SSD_N), F32)],
        scratch_shapes=[pltpu.VMEM((SSD_HEADS, SSD_P, SSD_N), F32)],
        compiler_params=_cp(("arbitrary",)),
    )(xact, proj, svec)


def _ssd_bwd(xact, proj, svec, states, dy, name):
    S = xact.shape[0]
    nc = S // CHUNK

    def body(x_ref, dt_ref, v_ref, st_ref, dy_ref, dx_ref, ddt_ref, dv_ref, dstate):
        @pl.when(pl.program_id(0) == 0)
        def _():
            dstate[...] = jnp.zeros_like(dstate)
            dv_ref[...] = jnp.zeros_like(dv_ref)

        x = x_ref[...]
        _, vjp = jax.vjp(fn_ssd_chunk, x[:, 0:512], x[:, 512:768], x[:, 768:1024], dt_ref[...], st_ref[0], v_ref[...])
        dxs, dbm, dcm, ddt, dst, dvec = vjp((dy_ref[...], dstate[...]))
        dx_ref[:, 0:512] = dxs
        dx_ref[:, 512:768] = dbm
        dx_ref[:, 768:1024] = dcm
        ddt_ref[...] = ddt
        dstate[...] = dst
        dv_ref[...] += dvec

    rev = lambda i: (nc - 1 - i, 0)
    return pl.pallas_call(
        body, name=name, grid=(nc,),
        in_specs=[pl.BlockSpec((CHUNK, D_CONV), rev), pl.BlockSpec((CHUNK, LANES), lambda i: (nc - 1 - i, 16)),
                  pl.BlockSpec((8, LANES), lambda i: (0, 0)),
                  pl.BlockSpec((1, SSD_HEADS, SSD_P, SSD_N), lambda i: (nc - 1 - i, 0, 0, 0)),
                  pl.BlockSpec((CHUNK, D_SSD), rev)],
        out_specs=[pl.BlockSpec((CHUNK, D_CONV), rev), pl.BlockSpec((CHUNK, LANES), rev),
                   pl.BlockSpec((8, LANES), lambda i: (0, 0))],
        out_shape=[jax.ShapeDtypeStruct((S, D_CONV), F32), jax.ShapeDtypeStruct((S, LANES), F32),
                   jax.ShapeDtypeStruct((8, LANES), F32)],
        scratch_shapes=[pltpu.VMEM((SSD_HEADS, SSD_P, SSD_N), F32)],
        compiler_params=_cp(("arbitrary",)),
    )(xact, proj, svec, states, dy)


def _ada_fwd(c_all, w_ada, b_sh, name):
    nb = 1536 // 512

    def body(c_ref, w_ref, b_ref, o_ref):
        ca = jax.nn.silu(c_ref[...]).astype(BF16)
        o_ref[0] = _dot(ca, w_ref[0].astype(BF16), ((1,), (0,))) + b_ref[0]

    return pl.pallas_call(
        body, name=name, grid=(DEPTH, nb),
        in_specs=[pl.BlockSpec((8, D_MODEL), lambda l, j: (0, 0)), pl.BlockSpec((1, D_MODEL, 512), lambda l, j: (l, 0, j)),
                  pl.BlockSpec((1, 1, 512), lambda l, j: (l, 0, j))],
        out_specs=pl.BlockSpec((1, 8, 512), lambda l, j: (l, 0, j)),
        out_shape=jax.ShapeDtypeStruct((DEPTH, 8, 1536), F32),
        compiler_params=_cp(("parallel", "parallel")),
    )(c_all, w_ada, b_sh)


def _ada_bwd(c_all_t, dmod_sh, name):
    nb = 1536 // 512

    def body(c_ref, d_ref, o_ref):
        ca = jax.nn.silu(c_ref[...])
        acc = ca[:, 0:1] * d_ref[0, 0:1, :]
        for b in range(1, 8):
            acc = acc + ca[:, b:b + 1] * d_ref[0, b:b + 1, :]
        o_ref[0] = acc

    return pl.pallas_call(
        body, name=name, grid=(DEPTH, nb),
        in_specs=[pl.BlockSpec((D_MODEL, 8), lambda l, j: (0, 0)), pl.BlockSpec((1, 8, 512), lambda l, j: (l, 0, j))],
        out_specs=pl.BlockSpec((1, D_MODEL, 512), lambda l, j: (l, 0, j)),
        out_shape=jax.ShapeDtypeStruct((DEPTH, D_MODEL, 1536), F32),
        compiler_params=_cp(("parallel", "parallel")),
    )(c_all_t, dmod_sh)


def _rows_tile(rows):
    return next(t for t in (512, 256, 128, 64, 32, 16, 8) if rows % t == 0)


SUM_BLOCKS = 4
ADAM_BLOCKS = 8


def _sum_sibling(gs, ls, ci, name):
    n = len(gs)

    def body(c_ref, *refs):
        for p in range(n):
            refs[2 * n + p][...] = refs[2 * p][...] + refs[2 * p + 1][...]

    in_specs, out_specs, out_shape = [], [], []
    for g in gs:
        _, _, rh, cw = g.shape
        rb = rh // SUM_BLOCKS
        in_specs += [pl.BlockSpec((None, None, rb, cw), lambda s, i, c: (s, c[0], i, 0)),
                     pl.BlockSpec((None, rb, cw), lambda s, i, c: (s, i, 0))]
        out_specs.append(pl.BlockSpec((None, rb, cw), lambda s, i, c: (s, i, 0)))
        out_shape.append(jax.ShapeDtypeStruct((4, rh, cw), F32))
    ops = [a for pair in zip(gs, ls) for a in pair]
    return pl.pallas_call(
        body, name=name,
        grid_spec=pltpu.PrefetchScalarGridSpec(num_scalar_prefetch=1, grid=(4, SUM_BLOCKS), in_specs=in_specs, out_specs=out_specs),
        out_shape=out_shape, compiler_params=_cp(("parallel", "parallel")),
    )(ci.reshape(1).astype(jnp.int32), *ops)


def _sum_chips(cs, lands, chip, ci, name):
    n = len(cs)

    def body(c_ref, *refs):
        for p in range(n):
            a = refs[4 * p:4 * p + 4]
            refs[4 * n + p][...] = ((a[0][...] + a[1][...]) + a[2][...]) + a[3][...]

    in_specs, out_specs, out_shape = [], [], []
    for c in cs:
        _, rh, cw = c.shape
        rb = rh // SUM_BLOCKS
        in_specs.append(pl.BlockSpec((None, rb, cw), lambda i, ch: (ch[0], i, 0)))
        in_specs += [pl.BlockSpec((None, rb, cw), functools.partial(lambda i, ch, k: (k, i, 0), k=k)) for k in range(3)]
        out_specs.append(pl.BlockSpec((None, rb, cw), lambda i, ch: (ch[1], i, 0)))
        out_shape.append(jax.ShapeDtypeStruct((2, rh, cw), F32))
    ops = [a for c, l in zip(cs, lands) for a in (c, l, l, l)]
    return pl.pallas_call(
        body, name=name,
        grid_spec=pltpu.PrefetchScalarGridSpec(num_scalar_prefetch=1, grid=(SUM_BLOCKS,), in_specs=in_specs, out_specs=out_specs),
        out_shape=out_shape, compiler_params=_cp(("parallel",)),
    )(jnp.stack([chip, ci]).astype(jnp.int32), *ops)


def _adam_update(w, m, v, g):
    c1 = 1.0 / (1.0 - ADAM_B1 ** ADAM_STEP)
    c2 = 1.0 / (1.0 - ADAM_B2 ** ADAM_STEP)
    nm = ADAM_B1 * m + (1.0 - ADAM_B1) * g
    nv = ADAM_B2 * v + (1.0 - ADAM_B2) * (g * g)
    return -ADAM_LR * ((nm * c1) / (jnp.sqrt(nv * c2) + ADAM_EPS) + ADAM_WD * w), nm, nv


def _adam_multi(ws, ms, vs, gs, name):
    n = len(ws)
    per = 3 + DEPTH

    def body(*refs):
        layer = pl.program_id(0)
        for p in range(n):
            w, m, v = [refs[per * p + k][...] for k in range(3)]
            g = refs[per * p + 3][...]
            for l in range(1, DEPTH):
                g = jnp.where(layer == l, refs[per * p + 3 + l][...], g)
            d, nm, nv = _adam_update(w, m, v, g)
            for k, val in enumerate((g, d, nm, nv)):
                refs[per * n + 4 * p + k][...] = val

    in_specs, out_specs, out_shape = [], [], []
    for w in ws:
        _, r, cw = w.shape
        if r % (8 * ADAM_BLOCKS) == 0:
            spec = pl.BlockSpec((None, r // ADAM_BLOCKS, cw), lambda l, i: (l, i, 0))
            gspec = pl.BlockSpec((r // ADAM_BLOCKS, cw), lambda l, i: (i, 0))
        else:
            spec = pl.BlockSpec((None, r, cw // ADAM_BLOCKS), lambda l, i: (l, 0, i))
            gspec = pl.BlockSpec((r, cw // ADAM_BLOCKS), lambda l, i: (0, i))
        in_specs += [spec] * 3 + [gspec] * DEPTH
        out_specs += [spec] * 4
        out_shape += [jax.ShapeDtypeStruct(w.shape, F32)] * 4
    ops = [a for w, m, v, g in zip(ws, ms, vs, gs) for a in (w, m, v, *g)]
    res = pl.pallas_call(
        body, name=name, grid=(DEPTH, ADAM_BLOCKS), in_specs=in_specs, out_specs=out_specs, out_shape=out_shape,
        compiler_params=_cp(("parallel", "parallel")),
    )(*ops)
    return res[0::4], res[1::4], res[2::4], res[3::4]


def _adam(w, m, v, parts, name):
    rows, width = w.shape
    bm = min(256, _rows_tile(rows))
    np_ = len(parts)
    c1 = 1.0 / (1.0 - ADAM_B1 ** ADAM_STEP)
    c2 = 1.0 / (1.0 - ADAM_B2 ** ADAM_STEP)

    def body(*refs):
        w_ref, m_ref, v_ref = refs[:3]
        g = refs[3][...]
        for r in refs[4:3 + np_]:
            g = g + r[...]
        g_ref, d_ref, nm_ref, nv_ref = refs[3 + np_:]
        nm = ADAM_B1 * m_ref[...] + (1.0 - ADAM_B1) * g
        nv = ADAM_B2 * v_ref[...] + (1.0 - ADAM_B2) * (g * g)
        g_ref[...] = g
        nm_ref[...] = nm
        nv_ref[...] = nv
        d_ref[...] = -ADAM_LR * ((nm * c1) / (jnp.sqrt(nv * c2) + ADAM_EPS) + ADAM_WD * w_ref[...])

    blk = pl.BlockSpec((bm, width), lambda i: (i, 0))
    return pl.pallas_call(
        body, name=name, grid=(rows // bm,),
        in_specs=[blk, blk, blk] + [pl.BlockSpec((bm, width), functools.partial(lambda i, o: (i + o, 0), o=off // bm))
                                    for (_, off) in parts],
        out_specs=[blk, blk, blk, blk],
        out_shape=[jax.ShapeDtypeStruct((rows, width), F32)] * 4,
        compiler_params=_cp(("parallel",)),
    )(w, m, v, *[p[0] for p in parts])


def _coords():
    return lax.axis_index("x"), lax.axis_index("y"), lax.axis_index("c")


def _other_chips(x, y):
    return [(1 - x, y), (x, 1 - y), (1 - x, 1 - y)]


def _ag8(blk, name):
    m_per, n = blk.shape

    def body(x_ref, out_ref, send_sems, recv_sems, local_sem):
        x, y, c = _coords()
        me, sibling = (x, y, c), (x, y, 1 - c)
        chips = _other_chips(x, y)

        def rows(px, py, pc):
            return out_ref.at[pl.ds((4 * px + 2 * py + pc) * m_per, m_per), :]

        def copy(k, block, to, src=None):
            return pltpu.make_async_remote_copy(
                src_ref=rows(*block) if src is None else src, dst_ref=rows(*block),
                send_sem=send_sems.at[k], recv_sem=recv_sems.at[k], device_id=to, device_id_type=MESH)

        mine = pltpu.make_async_copy(x_ref, rows(*me), local_sem)
        mine.start()
        first = [copy(0, me, sibling, src=x_ref)]
        first += [copy(1 + j, me, (*chip, c), src=x_ref) for j, chip in enumerate(chips)]
        for cp in first:
            cp.start()
        passed = [copy(4 + j, (*chip, c), sibling) for j, chip in enumerate(chips)]
        for j, chip in enumerate(chips):
            copy(1 + j, (*chip, c), me).wait_recv()
            passed[j].start()
        copy(0, sibling, me).wait_recv()
        for j, chip in enumerate(chips):
            copy(4 + j, (*chip, 1 - c), me).wait_recv()
        for cp in first + passed:
            cp.wait_send()
        mine.wait()

    return pl.pallas_call(
        body, name=name,
        out_shape=jax.ShapeDtypeStruct((8 * m_per, n), blk.dtype),
        in_specs=[pl.BlockSpec(memory_space=pltpu.VMEM)], out_specs=pl.BlockSpec(memory_space=pltpu.VMEM),
        scratch_shapes=[pltpu.SemaphoreType.DMA((7,)), pltpu.SemaphoreType.DMA((7,)), pltpu.SemaphoreType.DMA],
    )(blk)


HBM_SPEC = pl.BlockSpec(memory_space=pltpu.HBM)
SEM_SPEC = pl.BlockSpec(memory_space=pltpu.SEMAPHORE)
EFFECT = pltpu.SideEffectType.DATAFLOW_SIDE_EFFECTING


def _remote(src, dst, send_sem, recv_sem, to):
    return pltpu.make_async_remote_copy(src_ref=src, dst_ref=dst, send_sem=send_sem, recv_sem=recv_sem,
                                        device_id=to, device_id_type=MESH)


def _ag_list(shards, name):
    n = len(shards)

    def body(*refs):
        sh, out = refs[:n], refs[n:2 * n]
        send_sems, recv_sems = refs[2 * n:]
        x, y, c = _coords()
        sibling = (x, y, 1 - c)
        chips = _other_chips(x, y)
        first = [_remote(sh[p].at[c], out[p].at[2 * x + y, c], send_sems.at[6 * p + j], recv_sems.at[6 * p + j], (px, py, c))
                 for p in range(n) for j, (px, py) in enumerate(chips)]
        for cp in first:
            cp.start()
        passed = []
        for j, (px, py) in enumerate(chips):
            for p in range(n):
                got = out[p].at[2 * px + py, c]
                _remote(got, got, send_sems.at[6 * p + j], recv_sems.at[6 * p + j], (x, y, c)).wait_recv()
                cp = _remote(got, got, send_sems.at[6 * p + 3 + j], recv_sems.at[6 * p + 3 + j], sibling)
                cp.start()
                passed.append(cp)
        for j, (px, py) in enumerate(chips):
            for p in range(n):
                got = out[p].at[2 * px + py, 1 - c]
                _remote(got, got, send_sems.at[6 * p + 3 + j], recv_sems.at[6 * p + 3 + j], (x, y, c)).wait_recv()
        for cp in first + passed:
            cp.wait_send()

    return pl.pallas_call(
        body, name=name,
        out_shape=[jax.ShapeDtypeStruct((4,) + s.shape, s.dtype) for s in shards],
        in_specs=[pl.BlockSpec(memory_space=pl.ANY)] * n, out_specs=[pl.BlockSpec(memory_space=pl.ANY)] * n,
        scratch_shapes=[pltpu.SemaphoreType.DMA((6 * n,)), pltpu.SemaphoreType.DMA((6 * n,))],
    )(*shards)


def _ag_direct_copies(sh, land, send_sems, recv_sems, starting):
    x, y, c = _coords()
    return [_remote(sh[p], land[p].at[2 * x + y] if starting else land[p].at[2 * px + py],
                    send_sems.at[3 * p + j], recv_sems.at[3 * p + j], (px, py, c))
            for p in range(len(sh)) for j, (px, py) in enumerate(_other_chips(x, y))]


def _rs_sibling_copies(g, land, send_sems, recv_sems, starting):
    x, y, c = _coords()
    return [_remote(g[p].at[s, 1 - c], land[p].at[s], send_sems.at[4 * p + s], recv_sems.at[4 * p + s], (x, y, 1 - c))
            for p in range(len(g)) for s in range(4)]


def _rs_chips_copies(cs, land, send_sems, recv_sems, starting):
    x, y, c = _coords()
    return [_remote(cs[p].at[2 * px + py], land[p].at[j], send_sems.at[3 * p + j], recv_sems.at[3 * p + j], (px, py, c))
            for p in range(len(cs)) for j, (px, py) in enumerate(_other_chips(x, y))]


def _split_start(copies, srcs, land_shapes, per, name):
    n = len(srcs)

    def body(*refs):
        for cp in copies(refs[:n], refs[n:2 * n], refs[2 * n], refs[2 * n + 1], True):
            cp.start()
        token = refs[4 * n + 2]
        token[...] = jnp.zeros_like(token)

    lands = [pltpu.with_memory_space_constraint(lax.empty(shp, s.dtype), pltpu.HBM) for shp, s in zip(land_shapes, srcs)]
    res = pl.pallas_call(
        body, name=name,
        out_shape=(pltpu.SemaphoreType.DMA((per * n,)), pltpu.SemaphoreType.DMA((per * n,)))
        + tuple(pltpu.HBM(s.shape, s.dtype) for s in srcs) + tuple(pltpu.HBM(l.shape, l.dtype) for l in lands)
        + (jax.ShapeDtypeStruct((8, LANES), F32),),
        in_specs=(HBM_SPEC,) * (2 * n), out_specs=(SEM_SPEC, SEM_SPEC) + (HBM_SPEC,) * (2 * n) + (pl.BlockSpec(memory_space=pltpu.VMEM),),
        input_output_aliases={i: 2 + i for i in range(2 * n)},
        compiler_params=pltpu.CompilerParams(has_side_effects=EFFECT),
    )(*[pltpu.with_memory_space_constraint(s, pltpu.HBM) for s in srcs], *lands)
    return res[0], res[1], res[2:2 + n], res[2 + n:2 + 2 * n], res[2 + 2 * n]


def _split_wait(copies, send_sems, recv_sems, src_thru, land_thru, after, name):
    n = len(src_thru)

    def body(*refs):
        for cp in copies(refs[:n], refs[n:2 * n], refs[2 * n], refs[2 * n + 1], False):
            cp.wait_send()
            cp.wait_recv()

    res = pl.pallas_call(
        body, name=name,
        out_shape=tuple(pltpu.HBM(s.shape, s.dtype) for s in src_thru) + tuple(pltpu.HBM(l.shape, l.dtype) for l in land_thru),
        in_specs=(HBM_SPEC,) * (2 * n) + (SEM_SPEC, SEM_SPEC, pl.BlockSpec(memory_space=pl.ANY)),
        out_specs=(HBM_SPEC,) * (2 * n), input_output_aliases={i: i for i in range(2 * n)},
        compiler_params=pltpu.CompilerParams(has_side_effects=EFFECT),
    )(*src_thru, *land_thru, send_sems, recv_sems, after)
    return res[:n], res[n:]


def _ag_direct_start(shards, name):
    return _split_start(_ag_direct_copies, shards, [(4,) + s.shape for s in shards], 3, name)


def _ag_direct_wait(send_sems, recv_sems, sh_thru, land_thru, after, name):
    return _split_wait(_ag_direct_copies, send_sems, recv_sems, sh_thru, land_thru, after, name)[1]


def _rs_sibling_start(gs, name):
    return _split_start(_rs_sibling_copies, gs, [(4,) + g.shape[2:] for g in gs], 4, name)


def _rs_sibling_wait(send_sems, recv_sems, g_thru, land_thru, after, name):
    return _split_wait(_rs_sibling_copies, send_sems, recv_sems, g_thru, land_thru, after, name)


def _rs_sibling_list(gs, name):
    n = len(gs)

    def body(*refs):
        cps = _rs_sibling_copies(refs[:n], refs[n:2 * n], refs[2 * n], refs[2 * n + 1], True)
        for cp in cps:
            cp.start()
        for cp in cps:
            cp.wait_recv()
        for cp in cps:
            cp.wait_send()

    return pl.pallas_call(
        body, name=name,
        out_shape=[jax.ShapeDtypeStruct((4,) + g.shape[2:], g.dtype) for g in gs],
        in_specs=[pl.BlockSpec(memory_space=pl.ANY)] * n, out_specs=[pl.BlockSpec(memory_space=pl.ANY)] * n,
        scratch_shapes=[pltpu.SemaphoreType.DMA((4 * n,)), pltpu.SemaphoreType.DMA((4 * n,))],
    )(*gs)


def _rs_chips_start(cs, name):
    return _split_start(_rs_chips_copies, cs, [(3,) + c.shape[1:] for c in cs], 3, name)


def _rs_chips_wait(send_sems, recv_sems, cs_thru, land_thru, after, name):
    return _split_wait(_rs_chips_copies, send_sems, recv_sems, cs_thru, land_thru, after, name)


def _swap_list(ghs, name):
    n = len(ghs)

    def body(*refs):
        g, out, send_sems, recv_sems = refs[:n], refs[n:2 * n], refs[2 * n], refs[2 * n + 1]
        x, y, c = _coords()
        cps = [_remote(g[p].at[c], out[p].at[c], send_sems.at[p], recv_sems.at[p], (x, y, 1 - c)) for p in range(n)]
        for cp in cps:
            cp.start()
        for p in range(n):
            _remote(g[p].at[c], out[p].at[1 - c], send_sems.at[p], recv_sems.at[p], (x, y, 1 - c)).wait_recv()
        for cp in cps:
            cp.wait_send()

    return pl.pallas_call(
        body, name=name,
        out_shape=[jax.ShapeDtypeStruct(g.shape, g.dtype) for g in ghs],
        in_specs=[pl.BlockSpec(memory_space=pl.ANY)] * n, out_specs=[pl.BlockSpec(memory_space=pl.ANY)] * n,
        input_output_aliases={p: p for p in range(n)},
        scratch_shapes=[pltpu.SemaphoreType.DMA((n,)), pltpu.SemaphoreType.DMA((n,))],
    )(*ghs)


def _pad_win(w):
    return jnp.concatenate([w[:, :416], jnp.zeros((w.shape[0], 96), w.dtype), w[:, 416:1952],
                            w[:, 1952:1960], jnp.zeros((w.shape[0], 120), w.dtype)], axis=1)


def _unpad_win(g):
    return jnp.concatenate([g[:, :416], g[:, 512:2048], g[:, 2048:2056]], axis=1)


def _pad_wq(w):
    return jnp.pad(w.reshape(Q_LORA, HEADS, QK_DIM), ((0, 0), (0, 0), (0, LANES - QK_DIM))).reshape(Q_LORA, HEADS * LANES)


def _unpad_wq(g):
    return g.reshape(Q_LORA, HEADS, LANES)[:, :, :QK_DIM].reshape(Q_LORA, HEADS * QK_DIM)


def _cols_to_shards(a):
    r, c4 = a.shape
    return a.reshape(r, 4, c4 // 4).transpose(1, 0, 2)


def _shards_to_cols(a):
    _, r, c = a.shape
    return a.transpose(1, 0, 2).reshape(r, 4 * c)


def _pack_small(tree):
    parts = []
    for l in range(DEPTH):
        for (n, k) in SMALL:
            parts.append(jnp.pad(tree[n][l].reshape(-1), (0, -k % LANES)))
    flat = jnp.concatenate(parts)
    return jnp.pad(flat, (0, SMALL_ROWS * LANES - flat.shape[0])).reshape(SMALL_ROWS, LANES)


def _unpack_small(buf):
    flat = buf.reshape(-1)
    out = {n: [] for (n, _) in SMALL}
    o = 0
    for l in range(DEPTH):
        for (n, k) in SMALL:
            out[n].append(flat[o:o + k])
            o += k + (-k % LANES)
    return {n: jnp.stack(v) for n, v in out.items()}


def _vec(v, width=LANES):
    return jnp.pad(v.reshape(1, -1), ((0, 0), (0, width - v.shape[-1])))


def kernel(x, c, positions, norm1_w, norm2_w, w_ada, b_ada, w_in, q_a_norm_w, w_q_up, kv_a_norm_w, w_kv_up, q_nope_norm_w, q_pe_norm_w, k_nope_norm_w, k_pe_norm_w, conv_w, conv_b, dt_bias, a_log, d_skip, ssd_norm_w, w_out, w_gate_up, w_down, loss_target, m_norm1_w, m_norm2_w, m_w_ada, m_b_ada, m_w_in, m_q_a_norm_w, m_w_q_up, m_kv_a_norm_w, m_w_kv_up, m_q_nope_norm_w, m_q_pe_norm_w, m_k_nope_norm_w, m_k_pe_norm_w, m_conv_w, m_conv_b, m_dt_bias, m_a_log, m_d_skip, m_ssd_norm_w, m_w_out, m_w_gate_up, m_w_down, v_norm1_w, v_norm2_w, v_w_ada, v_b_ada, v_w_in, v_q_a_norm_w, v_w_q_up, v_kv_a_norm_w, v_w_kv_up, v_q_nope_norm_w, v_q_pe_norm_w, v_k_nope_norm_w, v_k_pe_norm_w, v_conv_w, v_conv_b, v_dt_bias, v_a_log, v_d_skip, v_ssd_norm_w, v_w_out, v_w_gate_up, v_w_down):
    W = dict(zip(WEIGHTS, (norm1_w, norm2_w, w_ada, b_ada, w_in, q_a_norm_w, w_q_up, kv_a_norm_w, w_kv_up, q_nope_norm_w, q_pe_norm_w, k_nope_norm_w, k_pe_norm_w, conv_w, conv_b, dt_bias, a_log, d_skip, ssd_norm_w, w_out, w_gate_up, w_down)))
    M = dict(zip(WEIGHTS, (m_norm1_w, m_norm2_w, m_w_ada, m_b_ada, m_w_in, m_q_a_norm_w, m_w_q_up, m_kv_a_norm_w, m_w_kv_up, m_q_nope_norm_w, m_q_pe_norm_w, m_k_nope_norm_w, m_k_pe_norm_w, m_conv_w, m_conv_b, m_dt_bias, m_a_log, m_d_skip, m_ssd_norm_w, m_w_out, m_w_gate_up, m_w_down)))
    V = dict(zip(WEIGHTS, (v_norm1_w, v_norm2_w, v_w_ada, v_b_ada, v_w_in, v_q_a_norm_w, v_w_q_up, v_kv_a_norm_w, v_w_kv_up, v_q_nope_norm_w, v_q_pe_norm_w, v_k_nope_norm_w, v_k_pe_norm_w, v_conv_w, v_conv_b, v_dt_bias, v_a_log, v_d_skip, v_ssd_norm_w, v_w_out, v_w_gate_up, v_w_down)))
    S = x.shape[1]
    xi, yi, ci = _coords()
    chip = 2 * xi + yi
    dev = 2 * chip + ci
    x0 = x[0]
    tgt = loss_target[0]

    inv_freq = 1.0 / (ROPE_THETA ** (jnp.arange(0, ROPE, 2, dtype=F32) / ROPE))
    ang = positions[0].astype(F32)[:, None] * inv_freq
    cos, sin = jnp.cos(ang), jnp.sin(ang)
    z16, z32, z64 = jnp.zeros((S, 16), F32), jnp.zeros((S, 32), F32), jnp.zeros((S, 64), F32)
    tab_c = jnp.concatenate([jnp.ones((S, 64), F32), cos, cos, z32], axis=1)
    tab_s1 = jnp.concatenate([z64, z16, sin, z32], axis=1)
    tab_s2 = jnp.concatenate([z64, -sin, z16, z32], axis=1)

    blk0 = jnp.concatenate([c.reshape(-1), W['conv_w'].reshape(-1)]).reshape(24, LANES)
    g0 = _ag8(blk0, "ag_c_conv").reshape(8, 24 * LANES)
    c_all = g0[:, :D_MODEL]
    conv_full = g0[0::2, D_MODEL:].reshape(4, DEPTH, CONV_TAPS, 256).transpose(1, 2, 0, 3).reshape(DEPTH, CONV_TAPS, D_CONV)

    sh = [{n: W[n][l].astype(BF16) for n in BIG} for l in range(DEPTH)]
    got_first = _ag_list([sh[0][n].reshape(2, sh[0][n].shape[0] // 2, sh[0][n].shape[1]) for n in FIRST], "ag_w0_first")
    to_operand = dict(w_in=lambda a: _pad_win(_shards_to_cols(a)), w_q_up=lambda a: _pad_wq(_shards_to_cols(a)),
                      w_kv_up=_shards_to_cols, w_out=lambda a: a.reshape(D_MODEL, D_MODEL), w_gate_up=lambda a: a,
                      w_down=lambda a: a.reshape(D_FF, D_MODEL))

    def layer_weights(names, gathered, own):
        return {n: to_operand[n](lax.dynamic_update_slice_in_dim(a.reshape(4, -1, a.shape[-1]), own[n][None], chip, axis=0))
                for n, a in zip(names, gathered)}

    LW = [layer_weights(FIRST, got_first, sh[0]), None]

    b_sh = lax.dynamic_slice_in_dim(W['b_ada'], chip * 1536, 1536, axis=1).reshape(DEPTH, 1, 1536)
    mod_sh = _ada_fwd(c_all, W['w_ada'], b_sh, "ada_fwd")
    g1 = _ag8(mod_sh.reshape(192, LANES), "ag_mod").reshape(8, DEPTH, 8, 1536)
    mod_all = g1[0::2].transpose(1, 2, 0, 3).reshape(DEPTH, 8, 6 * D_MODEL)
    mod = lax.dynamic_index_in_dim(mod_all, dev, axis=1, keepdims=False)
    mod, rest0 = lax.optimization_barrier((mod, [sh[0][n] for n in REST]))
    ag0 = _ag_direct_start(rest0, "ag_w0_rest_start")

    def mvec(l, k):
        return mod[l, k * D_MODEL:(k + 1) * D_MODEL].reshape(1, D_MODEL)

    def small(name, l, width=None):
        v = W[name][l]
        return _vec(v, width or v.shape[-1])

    def wq_vec(l):
        return _vec(jnp.concatenate([W['q_nope_norm_w'][l], W['q_pe_norm_w'][l]]))

    def wk_vec(l):
        return _vec(jnp.concatenate([W['k_nope_norm_w'][l], W['k_pe_norm_w'][l]]))

    def conv_vec(l):
        return jnp.concatenate([conv_full[l], W['conv_b'][l].reshape(1, D_CONV), jnp.zeros((3, D_CONV), F32)], axis=0)

    def ssd_vec(l):
        return jnp.concatenate([_vec(W['dt_bias'][l]), _vec(W['a_log'][l]), _vec(W['d_skip'][l]), jnp.zeros((5, LANES), F32)], axis=0)

    sv = []
    xcur = x0
    h1 = _row_fwd(fn_norm_mod, "norm_mod_f", [(x0, 0, D_MODEL)], [small('norm1_w', 0) + ag0[4][0, 0], mvec(0, 1), mvec(0, 0)],
                  [(D_MODEL, BF16)])[0]
    fin = None
    ag_first = None
    ag_rest = ag0
    for l in range(DEPTH):
        if l == 1:
            LW[1] = layer_weights(FIRST, _ag_direct_wait(*ag_first[:4], xcur, "ag_w1_first_wait"), sh[1])
        lw = LW[l]
        t = dict(xcur=xcur, h1=h1)
        t['proj'] = proj = _mm(h1, lw['w_in'], 'nn', f"mm_in_{l}")
        t['qa_n'], t['kva_n'] = _row_fwd(fn_lat_norm, f"lat_norm_f{l}", [(proj, 0, 256), (proj, 2, 128)],
                                         [small('q_a_norm_w', l), small('kv_a_norm_w', l)], [(256, BF16), (128, BF16)])
        t['q'] = _mm(t['qa_n'], lw['w_q_up'], 'nn', f"mm_q_{l}")
        t['kv'] = _mm(t['kva_n'], lw['w_kv_up'], 'nn', f"mm_kv_{l}")
        t['qf'], t['kf'], t['vv'], t['kT'] = _row_fwd(
            fn_qk_prep_kt, f"qk_prep_f{l}",
            [(t['q'], 0, 1024), (t['kv'], 0, 1024), (proj, 3, 128), (tab_c, 0, 128), (tab_s1, 0, 128), (tab_s2, 0, 128)],
            [wq_vec(l), wk_vec(l)], [(1024, BF16), (1024, BF16), (1024, BF16), (1024, BF16)], transposed=(3,))
        t['ao'], t['lse'] = _attn_fwd(t['qf'], t['kf'], t['vv'], f"attn_f{l}")
        t['xact'] = _conv_fwd(proj, conv_vec(l), f"conv_f{l}")
        t['y'], t['states'] = _ssd_fwd(t['xact'], proj, ssd_vec(l), f"ssd_f{l}")
        tie = 0.0
        t['ao'], t['y'] = lax.optimization_barrier((t['ao'], t['y']))
        rest = list(_ag_direct_wait(*ag_rest[:4], t['y'], f"ag_w{l}_rest_wait"))
        if l == 0:
            rest, sh1f, sh1r = lax.optimization_barrier((rest, [sh[1][n] for n in FIRST], [sh[1][n] for n in REST]))
            ag_first = _ag_direct_start(sh1f, "ag_w1_first_start")
            ag_rest = _ag_direct_start(sh1r, "ag_w1_rest_start")
            tie = ag_first[4][0, 0] + ag_rest[4][0, 0]
        lw.update(layer_weights(REST, rest, sh[l]))
        t['mix'] = _row_fwd(fn_gated_mix, f"gated_f{l}", [(t['y'], 0, 512), (proj, 1, 512), (t['ao'], 0, 512)],
                            [small('ssd_norm_w', l) + tie], [(1024, BF16)])[0]
        t['mo'] = _mm(t['mix'], lw['w_out'], 'nn', f"mm_out_{l}")
        t['x1'], t['h2'] = _row_fwd(fn_resid_norm, f"resid_mid_f{l}", [(xcur, 0, D_MODEL), (t['mo'], 0, D_MODEL)],
                                    [mvec(l, 2), small('norm2_w', l), mvec(l, 4), mvec(l, 3)],
                                    [(D_MODEL, F32), (D_MODEL, BF16)])
        t['gu'], t['act'] = _mm_gu_swiglu(t['h2'], lw['w_gate_up'], f"mm_gu_{l}")
        t['ff'] = _mm(t['act'], lw['w_down'], 'nn', f"mm_down_{l}")
        if l + 1 < DEPTH:
            xcur, h1 = _row_fwd(fn_resid_norm, f"resid_end_f{l}", [(t['x1'], 0, D_MODEL), (t['ff'], 0, D_MODEL)],
                                [mvec(l, 5), small('norm1_w', l + 1), mvec(l + 1, 1), mvec(l + 1, 0)],
                                [(D_MODEL, F32), (D_MODEL, BF16)])
        else:
            fin = _final(t['x1'], t['ff'], tgt, mvec(l, 5), "final_loss")
        sv.append(t)

    dx1, dff, dg2_last, loss_acc = fin
    gfull = {n: [None] * DEPTH for n in BIG}
    gsm = {n: [None] * DEPTH for (n, _) in SMALL}
    dmod = [[None] * 6 for _ in range(DEPTH)]
    dmod[DEPTH - 1][5] = dg2_last
    grad_x = None
    pending = []

    def halves_of(l, names):
        return [gfull[n][l].reshape(4, 2, gfull[n][l].shape[1] // 2, gfull[n][l].shape[2]) for n in names]

    def rs_finish(l, names, tag, g4, sib):
        h = _rs_chips_start(_sum_sibling(g4, sib, ci, f"sum_sibling_{tag}"), f"rs_chips_start_{tag}")
        pending.append((l, names, h))
        return h[4][0, 0]

    tie_l1 = tie_l0a = tie_sib = 0.0
    sib_l1 = sib_l0a = None

    for l in reversed(range(DEPTH)):
        t = sv[l]
        lw = LW[l]
        proj = t['proj']
        dgu = _mm_down_dx_swiglu(dff, lw['w_down'], t['gu'], f"mm_down_dx{l}")
        gfull['w_down'][l] = _mm(t['act'], dff, 'tn', f"mm_down_dw{l}").reshape(4, D_FF // 4, D_MODEL)
        dh2 = _mm(dgu, lw['w_gate_up'], 'nt', f"mm_gu_dx{l}", stack='b')
        gfull['w_gate_up'][l] = _mm(t['h2'], dgu, 'tn', f"mm_gu_dw{l}", stack='out')
        if l == 0:
            sib_l0a = _rs_sibling_start(halves_of(0, EARLY), "rs_sibling_start_l0a")
            tie_l1 = rs_finish(1, BIG, "l1", *_rs_sibling_wait(*sib_l1[:4], dh2, "rs_sibling_wait_l1"))
            tie_sib = sib_l0a[4][0, 0]
        dxc, dmo, dmod[l][2], gsm['norm2_w'][l], dmod[l][4], dmod[l][3] = _row_bwd(
            fn_resid_norm, f"resid_mid_b{l}", [(t['xcur'], 0, D_MODEL), (t['mo'], 0, D_MODEL)],
            [mvec(l, 2) + ((tie_l1 + tie_sib) if l == 0 else 0.0), small('norm2_w', l), mvec(l, 4), mvec(l, 3)],
            [(dx1, 0, D_MODEL), (dh2, 0, D_MODEL)], [0, 1], [0, 1, 2, 3], ddtypes=[F32, BF16])
        dmix = _mm(dmo, lw['w_out'], 'nt', f"mm_out_dx{l}")
        gfull['w_out'][l] = _mm(t['mix'], dmo, 'tn', f"mm_out_dw{l}").reshape(4, D_MODEL // 4, D_MODEL)
        dy, dz, gsm['ssd_norm_w'][l] = _row_bwd(fn_gated_norm, f"gated_b{l}", [(t['y'], 0, 512), (proj, 1, 512)],
                                                [small('ssd_norm_w', l)], [(dmix, 1, 512)], [0, 1], [0])
        if l == 0:
            tie_l0a = rs_finish(0, EARLY, "l0a", *_rs_sibling_wait(*sib_l0a[:4], dmix, "rs_sibling_wait_l0a"))
        dxact, ddt, dsv = _ssd_bwd(t['xact'], proj, ssd_vec(l) + (tie_l0a if l == 0 else 0.0), t['states'], dy, f"ssd_b{l}")
        gsm['dt_bias'][l], gsm['a_log'][l], gsm['d_skip'][l] = dsv[0, :8], dsv[1, :8], dsv[2, :8]
        dxbc, dcv = _conv_bwd(proj, conv_vec(l), dxact, f"conv_b{l}")
        gsm['conv_w'][l] = dcv[:CONV_TAPS]
        gsm['conv_b'][l] = dcv[CONV_TAPS]
        delta_r = _attn_delta(dmix, t['ao'], f"attn_delta{l}")
        dqT, dkf, dvv = _attn_bwd(t['qf'], t['kf'], t['kT'], t['vv'], dmix, t['lse'], delta_r, f"attn_b{l}")
        dq, dkv, dkpe, dwq, dwk = _row_bwd(
            fn_qk_prep, f"qk_prep_b{l}",
            [(t['q'], 0, 1024), (t['kv'], 0, 1024), (proj, 3, 128), (tab_c, 0, 128), (tab_s1, 0, 128), (tab_s2, 0, 128)],
            [wq_vec(l), wk_vec(l)], [(dqT, 0, 1024), (dkf, 0, 1024), (dvv, 0, 1024)], [0, 1, 2], [0, 1],
            ddtypes=[BF16, BF16, F32], transposed=(0,))
        gsm['q_nope_norm_w'][l], gsm['q_pe_norm_w'][l] = dwq[0, :NOPE], dwq[0, NOPE:QK_DIM]
        gsm['k_nope_norm_w'][l], gsm['k_pe_norm_w'][l] = dwk[0, :NOPE], dwk[0, NOPE:QK_DIM]
        dqa_n = _mm(dq, lw['w_q_up'], 'nt', f"mm_q_dx{l}")
        gfull['w_q_up'][l] = _cols_to_shards(_unpad_wq(_mm(t['qa_n'], dq, 'tn', f"mm_q_dw{l}")))
        dkva_n = _mm(dkv, lw['w_kv_up'], 'nt', f"mm_kv_dx{l}")
        gfull['w_kv_up'][l] = _cols_to_shards(_mm(t['kva_n'], dkv, 'tn', f"mm_kv_dw{l}"))
        dqa, dkva, dqw, dkvw = _row_bwd(fn_lat_norm, f"lat_norm_b{l}", [(proj, 0, 256), (proj, 2, 128)],
                                        [small('q_a_norm_w', l), small('kv_a_norm_w', l)],
                                        [(dqa_n, 0, 256), (dkva_n, 0, 128)], [0, 1], [0, 1])
        gsm['q_a_norm_w'][l], gsm['kv_a_norm_w'][l] = dqw[0], dkvw[0]
        dproj = jnp.concatenate([dqa, dkva, dkpe, dz, dxbc, ddt], axis=1).astype(BF16)
        dh1 = _mm(dproj, lw['w_in'], 'nt', f"mm_in_dx{l}")
        gfull['w_in'][l] = _cols_to_shards(_unpad_win(_mm(t['h1'], dproj, 'tn', f"mm_in_dw{l}")))
        if l > 0:
            p = sv[l - 1]
            dx1, dff, dmod[l - 1][5], gsm['norm1_w'][l], dmod[l][1], dmod[l][0] = _row_bwd(
                fn_resid_norm, f"resid_end_b{l - 1}", [(p['x1'], 0, D_MODEL), (p['ff'], 0, D_MODEL)],
                [mvec(l - 1, 5), small('norm1_w', l), mvec(l, 1), mvec(l, 0)], [(dxc, 0, D_MODEL), (dh1, 0, D_MODEL)],
                [0, 1], [0, 1, 2, 3], ddtypes=[F32, BF16])
            sib_l1 = _rs_sibling_start(halves_of(l, BIG), f"rs_sibling_start_l{l}")
            dff = dff + sib_l1[4][0, 0].astype(BF16)
        else:
            grad_x, gsm['norm1_w'][l], dmod[l][1], dmod[l][0] = _row_bwd(
                fn_norm_mod_pass, "norm_mod_b", [(x0, 0, D_MODEL)], [small('norm1_w', 0), mvec(0, 1), mvec(0, 0)],
                [(dxc, 0, D_MODEL), (dh1, 0, D_MODEL)], [0], [0, 1, 2])
        for n in ('norm1_w', 'norm2_w', 'ssd_norm_w'):
            gsm[n][l] = gsm[n][l][0]

    for l in range(DEPTH):
        gsm['b_ada'][l] = jnp.concatenate([d[0] for d in dmod[l]])
    sm_part = _pack_small({n: jnp.stack(v) for n, v in gsm.items()}).at[SMALL_ROWS - 1, 0].set(loss_acc[0, 0])
    sm_all = _ag8(sm_part, "ag_small")
    loss = jnp.sum(sm_all.reshape(8, SMALL_ROWS, LANES)[:, SMALL_ROWS - 1, 0])
    sm_all, late = lax.optimization_barrier((sm_all, [gfull[n][0] for n in BIG[2:]]))
    for n, g in zip(BIG[2:], late):
        gfull[n][0] = g
    late4 = halves_of(0, BIG[2:])
    tie_l0b = rs_finish(0, BIG[2:], "l0b", late4, _rs_sibling_list(late4, "rs_sibling_l0b"))

    def with_conv(tree):
        wide = lax.dynamic_update_slice_in_dim(jnp.zeros((DEPTH, CONV_TAPS, D_CONV), F32), tree['conv_w'], chip * 256, axis=2)
        return {**tree, 'conv_w': wide}

    g_sm, d_sm, m_sm, v_sm = _adam(_pack_small(with_conv(W)) + tie_l0b, _pack_small(with_conv(M)), _pack_small(with_conv(V)),
                                   [(sm_all, d * SMALL_ROWS) for d in range(8)], "adam_small")
    out_small = [_unpack_small(b) for b in (g_sm, d_sm, m_sm, v_sm)]
    for o in out_small:
        o['conv_w'] = lax.dynamic_slice_in_dim(o['conv_w'].reshape(DEPTH, CONV_TAPS, D_CONV), chip * 256, 256, axis=2)

    dmod_all = sm_all.reshape(8, SMALL_ROWS * LANES)
    per_layer = sum(k + (-k % LANES) for (_, k) in SMALL)
    dmod_sh = jnp.stack([lax.dynamic_slice_in_dim(dmod_all[:, l * per_layer:l * per_layer + 6 * D_MODEL], chip * 1536, 1536, axis=1)
                         for l in range(DEPTH)])
    g_ada = _ada_bwd(c_all.T, dmod_sh, "ada_bwd")
    ada = _adam(W['w_ada'].reshape(DEPTH * D_MODEL, 1536), M['w_ada'].reshape(DEPTH * D_MODEL, 1536),
                V['w_ada'].reshape(DEPTH * D_MODEL, 1536), [(g_ada.reshape(DEPTH * D_MODEL, 1536), 0)], "adam_ada")
    out_ada = [a.reshape(DEPTH, D_MODEL, 1536) for a in ada]

    keys, cs_all, land_all = [], [], []
    for (l, names, (send_sems, recv_sems, cs_thru, land_thru, _)) in pending:
        cs, lands = _rs_chips_wait(send_sems, recv_sems, cs_thru, land_thru, ada[3], f"rs_chips_wait_l{l}{len(names)}")
        keys += [(l, n) for n in names]
        cs_all += list(cs)
        land_all += list(lands)
    gboth = _swap_list(_sum_chips(cs_all, land_all, chip, ci, "sum_chips"), "swap_halves")
    gshard = {k: g.reshape(2 * g.shape[1], g.shape[2]) for k, g in zip(keys, gboth)}

    def natural(n, a):
        return jnp.swapaxes(a, -1, -2) if n == 'w_in' else a

    res = _adam_multi([natural(n, W[n]) for n in BIG], [natural(n, M[n]) for n in BIG], [natural(n, V[n]) for n in BIG],
                      [[natural(n, gshard[(l, n)]) for l in range(DEPTH)] for n in BIG], "adam_big")
    out_big = [{n: natural(n, a) for n, a in zip(BIG, o)} for o in res]

    outs = [loss, grad_x[None]]
    for k in range(4):
        for n in WEIGHTS:
            if n == 'w_ada':
                outs.append(out_ada[k])
            elif n in BIG:
                outs.append(out_big[k][n])
            else:
                outs.append(out_small[k][n])
    return tuple(outs)
```

```python
import functools

import jax
import jax.numpy as jnp
from jax import lax
from jax.experimental import pallas as pl
from jax.experimental.pallas import tpu as pltpu

F32 = jnp.float32
BF16 = jnp.bfloat16
MESH = pl.DeviceIdType.MESH

D_MODEL = 1024
DEPTH = 2
HEADS = 8
NOPE = 64
ROPE = 32
QK_DIM = NOPE + ROPE
Q_LORA = 256
KV_LORA = 128
SSD_HEADS = 8
SSD_P = 64
SSD_N = 128
CHUNK = 128
CONV_TAPS = 4
D_SSD = 512
D_CONV = 1024
D_FF = 2816
D_IN = 1960
D_IN_PAD = 2176
EPS = 1e-6
ROPE_THETA = 10000.0
ATT_SCALE = QK_DIM ** -0.5
NEG = -1e30
LANES = 128
VMEM_LIMIT = 48 * 1024 * 1024

ADAM_LR, ADAM_B1, ADAM_B2, ADAM_EPS, ADAM_WD, ADAM_STEP = 0.001, 0.9, 0.999, 1e-08, 0.01, 10

WEIGHTS = ['norm1_w', 'norm2_w', 'w_ada', 'b_ada', 'w_in', 'q_a_norm_w', 'w_q_up', 'kv_a_norm_w', 'w_kv_up',
           'q_nope_norm_w', 'q_pe_norm_w', 'k_nope_norm_w', 'k_pe_norm_w', 'conv_w', 'conv_b', 'dt_bias', 'a_log',
           'd_skip', 'ssd_norm_w', 'w_out', 'w_gate_up', 'w_down']
BIG = ['w_down', 'w_gate_up', 'w_out', 'w_kv_up', 'w_q_up', 'w_in']
EARLY = BIG[:2]
FIRST = BIG[3:]
REST = BIG[:3]
SMALL = [('b_ada', 6144), ('conv_w', 4096), ('norm1_w', 1024), ('norm2_w', 1024), ('conv_b', 1024), ('ssd_norm_w', 512),
         ('q_a_norm_w', 256), ('kv_a_norm_w', 128), ('q_nope_norm_w', 64), ('q_pe_norm_w', 32),
         ('k_nope_norm_w', 64), ('k_pe_norm_w', 32), ('dt_bias', 8), ('a_log', 8), ('d_skip', 8)]
SMALL_ROWS = 240


def _cp(sem=None, **kw):
    return pltpu.CompilerParams(dimension_semantics=sem, vmem_limit_bytes=VMEM_LIMIT, **kw)


def _dot(a, b, dims, prec=None):
    return lax.dot_general(a, b, (dims, ((), ())), preferred_element_type=F32, precision=prec)


def _tile(dim, target):
    best = 0
    for t in range(LANES, min(dim, target) + 1, LANES):
        if dim % t == 0:
            best = t
    if best < 256 and dim <= 2304:
        return dim
    return best


def _mm(a, b, mode, name, out_dtype=F32, stack=None):
    ns = None
    halves = (a if mode == 'nt' else b).ndim == 3 and stack is not None and not (stack == 'b' and mode == 'nn')
    if stack == 'b':
        ns = b.shape[2]
        if mode == 'nn':
            (M, K), N = a.shape, 4 * ns
        else:
            M, K, N = a.shape[-2], 4 * ns, b.shape[1]
    elif mode == 'nn':
        (M, K), (_, N) = a.shape, b.shape
    elif mode == 'nt':
        (M, K), (N, _) = a.shape, b.shape
    else:
        (K, M), N = a.shape, (2 * b.shape[2] if halves else b.shape[1])
    if stack == 'out':
        ns = N // 4
    tm, tn, tk = _tile(M, 1408 if mode == 'tn' else 1024), _tile(N, 1408), _tile(K, 1408)
    if stack == 'b' and mode == 'nt':
        tk = ns
    elif stack is not None:
        tn = ns
    nk = K // tk
    dims = {'nn': ((1,), (0,)), 'nt': ((1,), (1,)), 'tn': ((0,), (0,))}[mode]

    def body(a_ref, b_ref, o_ref, *acc):
        part = _dot(a_ref[...].astype(BF16), b_ref[...].astype(BF16), dims)
        if nk == 1:
            o_ref[...] = part.astype(o_ref.dtype)
            return
        k = pl.program_id(2)

        @pl.when(k == 0)
        def _():
            acc[0][...] = part

        @pl.when(k > 0)
        def _():
            acc[0][...] += part

        @pl.when(k == nk - 1)
        def _():
            o_ref[...] = acc[0][...].astype(o_ref.dtype)

    a_spec = pl.BlockSpec((tk, tm), lambda i, j, k: (k, i)) if mode == 'tn' else pl.BlockSpec((tm, tk), lambda i, j, k: (i, k))
    b_spec = pl.BlockSpec((tn, tk), lambda i, j, k: (j, k)) if mode == 'nt' else pl.BlockSpec((tk, tn), lambda i, j, k: (k, j))
    o_spec, o_shape = pl.BlockSpec((tm, tn), lambda i, j, k: (i, j)), (M, N)
    if stack == 'b':
        b_spec = (pl.BlockSpec((None, tn, ns), lambda i, j, k: (k, j, 0)) if mode == 'nt'
                  else pl.BlockSpec((None, tk, ns), lambda i, j, k: (j, k, 0)))
    if stack == 'out':
        o_spec, o_shape = pl.BlockSpec((None, tm, ns), lambda i, j, k: (j, i, 0)), (4, M, ns)
    if halves and mode == 'nt':
        a_spec = pl.BlockSpec((None, tm, ns), lambda i, j, k: (lax.div(k, 2), i, lax.rem(k, 2)))
    if halves and mode == 'tn':
        b_spec = pl.BlockSpec((None, tk, ns), lambda i, j, k: (lax.div(j, 2), k, lax.rem(j, 2)))
    return pl.pallas_call(
        body, name=name, grid=(M // tm, N // tn, nk),
        in_specs=[a_spec, b_spec], out_specs=o_spec,
        out_shape=jax.ShapeDtypeStruct(o_shape, out_dtype),
        scratch_shapes=[pltpu.VMEM((tm, tn), F32)] if nk > 1 else [],
        compiler_params=_cp(("parallel", "parallel", "arbitrary")),
    )(a, b)


def _mm_gu_swiglu(h, wst, name):
    S, K = h.shape
    ns = wst.shape[2]
    tm = _tile(S, 512)

    def body(a_ref, bg_ref, bu_ref, gu_ref, act_ref):
        a = a_ref[...]
        g = _dot(a, bg_ref[...], ((1,), (0,)))
        u = _dot(a, bu_ref[...], ((1,), (0,)))
        gu_ref[0] = g
        gu_ref[1] = u
        act_ref[...] = (g * jax.nn.sigmoid(g) * u).astype(act_ref.dtype)

    return pl.pallas_call(
        body, name=name, grid=(S // tm, 2),
        in_specs=[pl.BlockSpec((tm, K), lambda i, j: (i, 0)), pl.BlockSpec((None, K, ns), lambda i, j: (j, 0, 0)),
                  pl.BlockSpec((None, K, ns), lambda i, j: (j + 2, 0, 0))],
        out_specs=[pl.BlockSpec((2, tm, ns), lambda i, j: (0, i, j)), pl.BlockSpec((tm, ns), lambda i, j: (i, j))],
        out_shape=[jax.ShapeDtypeStruct((2, S, 2 * ns), F32), jax.ShapeDtypeStruct((S, 2 * ns), BF16)],
        compiler_params=_cp(("parallel", "parallel")),
    )(h, wst, wst)


def _mm_down_dx_swiglu(dff, w_down, gu, name):
    S, K = dff.shape
    tm, tn = _tile(S, 512), _tile(D_FF, 1408)

    def body(a_ref, b_ref, g_ref, u_ref, o_ref):
        dact = _dot(a_ref[...].astype(BF16), b_ref[...], ((1,), (1,)))
        g, u = g_ref[...], u_ref[...]
        sg = jax.nn.sigmoid(g)
        o_ref[0] = (dact * u * (sg * (1.0 + g * (1.0 - sg)))).astype(o_ref.dtype)
        o_ref[1] = (dact * (g * sg)).astype(o_ref.dtype)

    return pl.pallas_call(
        body, name=name, grid=(S // tm, D_FF // tn),
        in_specs=[pl.BlockSpec((tm, K), lambda i, j: (i, 0)), pl.BlockSpec((tn, K), lambda i, j: (j, 0)),
                  pl.BlockSpec((None, tm, tn), lambda i, j: (0, i, j)), pl.BlockSpec((None, tm, tn), lambda i, j: (1, i, j))],
        out_specs=pl.BlockSpec((2, tm, tn), lambda i, j: (0, i, j)),
        out_shape=jax.ShapeDtypeStruct((2, S, D_FF), BF16),
        compiler_params=_cp(("parallel", "parallel")),
    )(dff, w_down, gu, gu)


def _rspec(tm, w, cb):
    return pl.BlockSpec((tm, w), lambda i: (i, cb))


def _vspec(shape):
    return pl.BlockSpec(shape, lambda i: (0,) * len(shape))


def _row_fwd(fn, name, rows, vecs, outs, tm=256, transposed=()):
    S = rows[0][0].shape[0]
    tm = min(tm, S)
    nin = len(rows) + len(vecs)

    def body(*refs):
        res = fn(*[r[...] for r in refs[:nin]])
        for k, (o_ref, r) in enumerate(zip(refs[nin:], res)):
            o_ref[...] = (r.T if k in transposed else r).astype(o_ref.dtype)

    return pl.pallas_call(
        body, name=name, grid=(S // tm,),
        in_specs=[_rspec(tm, w, cb) for (_, cb, w) in rows] + [_vspec(v.shape) for v in vecs],
        out_specs=[pl.BlockSpec((w, tm), lambda i: (0, i)) if k in transposed else _rspec(tm, w, 0)
                   for k, (w, _) in enumerate(outs)],
        out_shape=[jax.ShapeDtypeStruct((w, S) if k in transposed else (S, w), dt) for k, (w, dt) in enumerate(outs)],
        compiler_params=_cp(("parallel",)),
    )(*[r[0] for r in rows], *vecs)


def _row_bwd(fn, name, rows, vecs, cts, drows, dvecs, tm=256, ddtypes=None, transposed=()):
    S = rows[0][0].shape[0]
    ddtypes = ddtypes or [F32] * len(drows)
    tm = min(tm, S)
    nr, nv, nc = len(rows), len(vecs), len(cts)
    didx = list(drows) + [nr + j for j in dvecs]

    def body(*refs):
        vals = [r[...] for r in refs[:nr + nv]]
        ct = tuple((r[...].T if k in transposed else r[...]).astype(F32)
                   for k, r in enumerate(refs[nr + nv:nr + nv + nc]))
        outs = refs[nr + nv + nc:]

        def g(*d):
            a = list(vals)
            for k, val in zip(didx, d):
                a[k] = val
            return tuple(fn(*a))

        _, vjp = jax.vjp(g, *[vals[k] for k in didx])
        grads = vjp(ct)
        for o, gr in zip(outs[:len(drows)], grads[:len(drows)]):
            o[...] = gr.astype(o.dtype)

        @pl.when(pl.program_id(0) == 0)
        def _():
            for o in outs[len(drows):]:
                o[...] = jnp.zeros_like(o)

        for o, gr in zip(outs[len(drows):], grads[len(drows):]):
            o[...] += gr

    return pl.pallas_call(
        body, name=name, grid=(S // tm,),
        in_specs=[_rspec(tm, w, cb) for (_, cb, w) in rows] + [_vspec(v.shape) for v in vecs]
        + [pl.BlockSpec((w, tm), lambda i: (0, i)) if k in transposed else _rspec(tm, w, cb) for k, (_, cb, w) in enumerate(cts)],
        out_specs=[_rspec(tm, rows[k][2], 0) for k in drows] + [_vspec(vecs[j].shape) for j in dvecs],
        out_shape=[jax.ShapeDtypeStruct((S, rows[k][2]), dt) for k, dt in zip(drows, ddtypes)]
        + [jax.ShapeDtypeStruct(vecs[j].shape, F32) for j in dvecs],
        compiler_params=_cp(("arbitrary",)),
    )(*[r[0] for r in rows], *vecs, *[c[0] for c in cts])


def _rms(x):
    return x * lax.rsqrt(jnp.mean(x * x, axis=-1, keepdims=True) + EPS)


def fn_norm_mod(x, nw, sc, sh):
    return (_rms(x) * nw * (1.0 + sc) + sh,)


def fn_norm_mod_pass(x, nw, sc, sh):
    return (x, _rms(x) * nw * (1.0 + sc) + sh)


def fn_resid_norm(x, d, g, nw, sc, sh):
    xn = x + g * d
    return (xn, _rms(xn) * nw * (1.0 + sc) + sh)


def fn_lat_norm(qa, kva, qw, kvw):
    return (_rms(qa) * qw, _rms(kva) * kvw)


@functools.partial(jax.custom_vjp, nondiff_argnums=(1,))
def _lroll(x, s):
    return pltpu.roll(x, s, 1)


def _lroll_fwd(x, s):
    return pltpu.roll(x, s, 1), None


def _lroll_bwd(s, _, g):
    return (pltpu.roll(g, (LANES - s) % LANES, 1),)


_lroll.defvjp(_lroll_fwd, _lroll_bwd)


def _lane_masks(shape):
    lane = lax.broadcasted_iota(jnp.int32, shape, 1)
    return (lane < NOPE).astype(F32), ((lane >= NOPE) & (lane < QK_DIM)).astype(F32)


def _rope(t, tc, ts1, ts2):
    return t * tc + _lroll(t, 16) * ts1 + _lroll(t, LANES - 16) * ts2


def fn_qk_prep(q, kv, kpe, tc, ts1, ts2, wq, wk):
    mn, mp = _lane_masks((1, LANES))
    mhi = 1.0 - mn

    def head_norm(t, w):
        rn = lax.rsqrt(jnp.sum(t * t * mn, axis=-1, keepdims=True) * (1.0 / NOPE) + EPS)
        rp = lax.rsqrt(jnp.sum(t * t * mp, axis=-1, keepdims=True) * (1.0 / ROPE) + EPS)
        return t * (rn * mn + rp * mp) * w

    kp = _rope(head_norm(_lroll(kpe, NOPE), wk) * mp, tc, ts1, ts2)
    qs, ks, vs = [], [], []
    for h in range(HEADS):
        qs.append(_rope(head_norm(q[:, h * LANES:(h + 1) * LANES], wq), tc, ts1, ts2))
        t = kv[:, h * LANES:(h + 1) * LANES]
        ks.append(head_norm(t, wk) * mn + kp)
        vs.append(_lroll(t, NOPE) * mn + mhi)
    return (jnp.concatenate(qs, axis=1), jnp.concatenate(ks, axis=1), jnp.concatenate(vs, axis=1))


def fn_qk_prep_kt(*args):
    qf, kf, va = fn_qk_prep(*args)
    return (qf, kf, va, kf)


def fn_gated_norm(y, z, w):
    u = y * jax.nn.silu(z)
    half = D_SSD // 2
    return (jnp.concatenate([_rms(u[:, :half]), _rms(u[:, half:])], axis=1) * w,)


def fn_gated_mix(y, z, ao, w):
    return (jnp.concatenate([ao, fn_gated_norm(y, z, w)[0]], axis=1),)


def _final(x1, ff, tgt, g2, name):
    S = x1.shape[0]
    tm = min(256, S)

    def body(x_ref, f_ref, t_ref, g_ref, dx_ref, df_ref, dg_ref, l_ref):
        @pl.when(pl.program_id(0) == 0)
        def _():
            dg_ref[...] = jnp.zeros_like(dg_ref)
            l_ref[...] = jnp.zeros_like(l_ref)

        f = f_ref[...]
        g = g_ref[...]
        e = x_ref[...] + g * f - t_ref[...]
        dx = e * (1.0 / D_MODEL)
        dx_ref[...] = dx
        df_ref[...] = (g * dx).astype(df_ref.dtype)
        dg_ref[...] += jnp.sum(dx * f, axis=0, keepdims=True)
        l_ref[...] += jnp.sum(e * e) * (0.5 / D_MODEL)

    r = _rspec(tm, D_MODEL, 0)
    return pl.pallas_call(
        body, name=name, grid=(S // tm,),
        in_specs=[r, r, r, _vspec((1, D_MODEL))],
        out_specs=[r, r, _vspec((1, D_MODEL)), _vspec((1, LANES))],
        out_shape=[jax.ShapeDtypeStruct((S, D_MODEL), F32), jax.ShapeDtypeStruct((S, D_MODEL), BF16),
                   jax.ShapeDtypeStruct((1, D_MODEL), F32), jax.ShapeDtypeStruct((1, LANES), F32)],
        compiler_params=_cp(("arbitrary",)),
    )(x1, ff, tgt, g2)


def _causal_mask(t):
    r = lax.broadcasted_iota(jnp.int32, (t, t), 0)
    c = lax.broadcasted_iota(jnp.int32, (t, t), 1)
    return c <= r


LOG2E = 1.4426950408889634
EXP2_SCALE = ATT_SCALE * LOG2E
ATT_TQ, ATT_TK = 512, 1024
ATT_BQ, ATT_BK = 1024, 512


def _attn_fwd(qf, kf, va, name):
    S = qf.shape[0]
    T, TK = min(ATT_TQ, S), min(ATT_TK, S)
    nmask = max(1, T // TK)

    def body(q_ref, k_ref, v_ref, o_ref, l_ref):
        i = pl.program_id(1)
        qs = [q_ref[:, hh * LANES:(hh + 1) * LANES] for hh in range(2)]

        def blk(j, carry, masked, tk=TK):
            off = pl.multiple_of(j * tk, tk)
            out = []
            for hh in range(2):
                m, acc = carry[hh]
                s = _dot(qs[hh], k_ref[pl.ds(off, tk), hh * LANES:(hh + 1) * LANES], ((1,), (1,)))
                if masked:
                    r = lax.broadcasted_iota(jnp.int32, (T, tk), 0)
                    c = lax.broadcasted_iota(jnp.int32, (T, tk), 1)
                    s = jnp.where(c + j * tk <= r + i * T, s, NEG)
                mn = jnp.maximum(m, jnp.max(s, axis=1, keepdims=True))
                p = jnp.exp2((s - mn) * EXP2_SCALE)
                al = jnp.exp2((m - mn) * EXP2_SCALE)
                vj = v_ref[pl.ds(off, tk), hh * LANES:(hh + 1) * LANES]
                out.append((mn, al * acc + _dot(p.astype(BF16), vj, ((1,), (0,)))))
            return tuple(out)

        one = (jnp.full((T, 1), NEG, F32), jnp.zeros((T, LANES), F32))
        nfull = lax.div(i * T, TK)
        carry = lax.fori_loop(0, nfull, lambda j, cr: blk(j, cr, False), (one, one))
        if TK == 2 * T:
            carry = lax.cond(lax.rem(i, 2) == 1, lambda cr: blk(i - 1, cr, False, T), lambda cr: cr, carry)
            carry = blk(i, carry, True, T)
        else:
            for t in range(nmask):
                carry = blk(nfull + t, carry, True)
        lane = lax.broadcasted_iota(jnp.int32, (1, LANES), 1)
        z = jnp.zeros((T, LANES), F32)
        for hh in range(2):
            m, acc = carry[hh]
            l = acc[:, 64:65]
            o_ref[:, hh * 64:(hh + 1) * 64] = (acc / l)[:, :64]
            z = z + (m * EXP2_SCALE + jnp.log(l) * LOG2E) * (lane == hh).astype(F32)
        l_ref[0] = z.T[0:2, :]

    return pl.pallas_call(
        body, name=name, grid=(HEADS // 2, S // T),
        in_specs=[pl.BlockSpec((T, 256), lambda h, i: (i, h)), pl.BlockSpec((S, 256), lambda h, i: (0, h)),
                  pl.BlockSpec((S, 256), lambda h, i: (0, h))],
        out_specs=[pl.BlockSpec((T, LANES), lambda h, i: (i, h)), pl.BlockSpec((1, 2, T), lambda h, i: (h, 0, i))],
        out_shape=[jax.ShapeDtypeStruct((S, D_SSD), F32), jax.ShapeDtypeStruct((HEADS // 2, 2, S), F32)],
        compiler_params=_cp(("parallel", "parallel")),
    )(qf, kf, va)


def _attn_delta(dmix, ao, name):
    S = ao.shape[0]
    tm = min(512, S)

    def body(d_ref, o_ref, out_ref):
        lane = lax.broadcasted_iota(jnp.int32, (1, LANES), 1)
        lo = (lane < 64).astype(F32)
        for hp in range(HEADS // 2):
            y = d_ref[:, hp * LANES:(hp + 1) * LANES] * o_ref[:, hp * LANES:(hp + 1) * LANES]
            z = (jnp.sum(y * lo, axis=1, keepdims=True) * (lane == 0).astype(F32)
                 + jnp.sum(y * (1.0 - lo), axis=1, keepdims=True) * (lane == 1).astype(F32))
            out_ref[hp] = z.T[0:2, :]

    return pl.pallas_call(
        body, name=name, grid=(S // tm,),
        in_specs=[pl.BlockSpec((tm, D_SSD), lambda i: (i, 0)), pl.BlockSpec((tm, D_SSD), lambda i: (i, 0))],
        out_specs=pl.BlockSpec((HEADS // 2, 2, tm), lambda i: (0, 0, i)),
        out_shape=jax.ShapeDtypeStruct((HEADS // 2, 2, S), F32),
        compiler_params=_cp(("parallel",)),
    )(dmix, ao)


def _attn_bwd(qf, kf, kT, va, do, lse_r, delta_r, name):
    S = qf.shape[0]
    T, TK = min(ATT_BQ, S), min(ATT_BK, S)
    nq = S // T
    nmask = max(1, TK // T)

    def body(q_ref, k_ref, kT_ref, v_ref, do_ref, l_ref, d_ref, dqT_ref, dk_ref, dv_ref):
        j = pl.program_id(1)

        @pl.when(j == 0)
        def _():
            dqT_ref[...] = jnp.zeros_like(dqT_ref)

        lo = (lax.broadcasted_iota(jnp.int32, (1, LANES), 1) < 64).astype(F32)
        ks = [k_ref[:, hh * LANES:(hh + 1) * LANES] for hh in range(2)]
        vs = [v_ref[:, hh * LANES:(hh + 1) * LANES] for hh in range(2)]
        kTs = [kT_ref[hh * LANES:(hh + 1) * LANES, :] for hh in range(2)]

        def blk(i, carry, masked, tq=T):
            off = pl.multiple_of(i * tq, tq)
            dall = do_ref[pl.ds(off, tq), :]
            out = []
            for hh in range(2):
                dk, dv = carry[hh]
                q = q_ref[pl.ds(off, tq), hh * LANES:(hh + 1) * LANES]
                dop = ((dall if hh == 0 else pltpu.roll(dall, 64, 1)) * lo).astype(BF16)
                lrow = l_ref[0, hh:hh + 1, pl.ds(off, tq)]
                drow = d_ref[0, hh:hh + 1, pl.ds(off, tq)]
                pT = jnp.exp2(_dot(ks[hh], q, ((1,), (1,))) * EXP2_SCALE - lrow)
                if masked:
                    r = lax.broadcasted_iota(jnp.int32, (TK, tq), 0)
                    c = lax.broadcasted_iota(jnp.int32, (TK, tq), 1)
                    pT = jnp.where(r + j * TK <= c + i * tq, pT, 0.0)
                dpT = _dot(vs[hh], dop, ((1,), (1,)))
                dsT = (pT * (dpT - drow) * ATT_SCALE).astype(BF16)
                dv = dv + _dot(pT.astype(BF16), dop, ((1,), (0,)))
                dk = dk + _dot(dsT, q, ((1,), (0,)))
                dqT_ref[hh * LANES:(hh + 1) * LANES, pl.ds(off, tq)] += _dot(kTs[hh], dsT, ((1,), (0,)))
                out.append((dk, dv))
            return tuple(out)

        z = (jnp.zeros((TK, LANES), F32), jnp.zeros((TK, LANES), F32))
        first = lax.div(j * TK, T)
        carry = (z, z)
        if T == 2 * TK:
            carry = blk(j, carry, True, TK)
            carry = lax.cond(lax.rem(j, 2) == 0, lambda cr: blk(j + 1, cr, False, TK), lambda cr: cr, carry)
            nmask_blocks = 1
        else:
            for t in range(nmask):
                carry = blk(first + t, carry, True)
            nmask_blocks = nmask
        carry = lax.fori_loop(first + nmask_blocks, nq, lambda i, cr: blk(i, cr, False), carry)
        for hh in range(2):
            dk_ref[:, hh * LANES:(hh + 1) * LANES] = carry[hh][0]
            dv_ref[:, hh * LANES:(hh + 1) * LANES] = carry[hh][1]

    return pl.pallas_call(
        body, name=name, grid=(HEADS // 2, S // TK),
        in_specs=[pl.BlockSpec((S, 256), lambda h, j: (0, h)), pl.BlockSpec((TK, 256), lambda h, j: (j, h)),
                  pl.BlockSpec((256, TK), lambda h, j: (h, j)), pl.BlockSpec((TK, 256), lambda h, j: (j, h)),
                  pl.BlockSpec((S, LANES), lambda h, j: (0, h)), pl.BlockSpec((1, 2, S), lambda h, j: (h, 0, 0)),
                  pl.BlockSpec((1, 2, S), lambda h, j: (h, 0, 0))],
        out_specs=[pl.BlockSpec((256, S), lambda h, j: (h, 0)), pl.BlockSpec((TK, 256), lambda h, j: (j, h)),
                   pl.BlockSpec((TK, 256), lambda h, j: (j, h))],
        out_shape=[jax.ShapeDtypeStruct((D_MODEL, S), F32), jax.ShapeDtypeStruct((S, D_MODEL), F32),
                   jax.ShapeDtypeStruct((S, D_MODEL), F32)],
        compiler_params=_cp(("parallel", "arbitrary")),
    )(qf, kf, kT, va, do, lse_r, delta_r)


def _shift_down(x, s):
    if s == 0:
        return x
    rows = lax.broadcasted_iota(jnp.int32, x.shape, 0)
    return jnp.where(rows >= s, pltpu.roll(x, s, 0), 0.0)


def _shift_up(x, s):
    if s == 0:
        return x
    n = x.shape[0]
    rows = lax.broadcasted_iota(jnp.int32, x.shape, 0)
    return jnp.where(rows < n - s, pltpu.roll(x, n - s, 0), 0.0)


def _conv_fwd(proj, cvec, name):
    S = proj.shape[0]

    def body(x_ref, c_ref, o_ref):
        x = x_ref[...]
        y = jnp.broadcast_to(c_ref[4:5, :], x.shape)
        for k in range(CONV_TAPS):
            y = y + c_ref[k:k + 1, :] * _shift_down(x, CONV_TAPS - 1 - k)
        o_ref[...] = y * jax.nn.sigmoid(y)

    return pl.pallas_call(
        body, name=name, grid=(D_CONV // LANES,),
        in_specs=[pl.BlockSpec((S, LANES), lambda j: (0, 8 + j)), pl.BlockSpec((8, LANES), lambda j: (0, j))],
        out_specs=pl.BlockSpec((S, LANES), lambda j: (0, j)),
        out_shape=jax.ShapeDtypeStruct((S, D_CONV), F32),
        compiler_params=_cp(("parallel",)),
    )(proj, cvec)


def _conv_bwd(proj, cvec, dact, name):
    S = proj.shape[0]

    def body(x_ref, c_ref, d_ref, dx_ref, dc_ref):
        x = x_ref[...]
        y = jnp.broadcast_to(c_ref[4:5, :], x.shape)
        for k in range(CONV_TAPS):
            y = y + c_ref[k:k + 1, :] * _shift_down(x, CONV_TAPS - 1 - k)
        sg = jax.nn.sigmoid(y)
        dy = d_ref[...] * (sg * (1.0 + y * (1.0 - sg)))
        dx = jnp.zeros_like(x)
        for k in range(CONV_TAPS):
            s = CONV_TAPS - 1 - k
            dx = dx + c_ref[k:k + 1, :] * _shift_up(dy, s)
            dc_ref[k:k + 1, :] = jnp.sum(dy * _shift_down(x, s), axis=0, keepdims=True)
        dx_ref[...] = dx
        dc_ref[4:5, :] = jnp.sum(dy, axis=0, keepdims=True)
        dc_ref[5:8, :] = jnp.zeros((3, LANES), F32)

    return pl.pallas_call(
        body, name=name, grid=(D_CONV // LANES,),
        in_specs=[pl.BlockSpec((S, LANES), lambda j: (0, 8 + j)), pl.BlockSpec((8, LANES), lambda j: (0, j)),
                  pl.BlockSpec((S, LANES), lambda j: (0, j))],
        out_specs=[pl.BlockSpec((S, LANES), lambda j: (0, j)), pl.BlockSpec((8, LANES), lambda j: (0, j))],
        out_shape=[jax.ShapeDtypeStruct((S, D_CONV), F32), jax.ShapeDtypeStruct((8, D_CONV), F32)],
        compiler_params=_cp(("parallel",)),
    )(proj, cvec, dact)


def fn_ssd_chunk(xs, bm, cm, dtr, state, vecs):
    Q = CHUNK
    dt = jax.nn.softplus(dtr + vecs[0:1])
    a = -jnp.exp(vecs[1:2])
    adt = dt * a
    tril = _causal_mask(Q)
    acs = _dot(tril.astype(F32), adt, ((1,), (0,)), lax.Precision.HIGHEST)
    acs_t = acs.T
    alast = acs[Q - 1:Q, :]
    r = lax.broadcasted_iota(jnp.int32, (LANES, D_SSD), 0)
    c = lax.broadcasted_iota(jnp.int32, (LANES, D_SSD), 1)
    spread = (lax.shift_right_logical(c, 6) == r).astype(F32)

    def per_head(v):
        return _dot(v, spread, ((1,), (0,)), lax.Precision.HIGH)

    xdt = xs * per_head(dt)
    ub = (xdt * per_head(jnp.exp(alast - acs))).astype(BF16)
    xdtb = xdt.astype(BF16)
    Bs = [bm[:, g * SSD_N:(g + 1) * SSD_N].astype(BF16) for g in range(2)]
    Cs = [cm[:, g * SSD_N:(g + 1) * SSD_N].astype(BF16) for g in range(2)]
    Gs = [_dot(Cs[g], Bs[g], ((1,), (1,))) for g in range(2)]
    yds, yos, adds = [], [], []
    for h in range(SSD_HEADS):
        g = h // (SSD_HEADS // 2)
        sl = slice(h * SSD_P, (h + 1) * SSD_P)
        L = jnp.exp(jnp.where(tril, acs[:, h:h + 1] - acs_t[h:h + 1, :], -jnp.inf))
        yds.append(_dot((Gs[g] * L).astype(BF16), xdtb[:, sl], ((1,), (0,))))
        yos.append(_dot(Cs[g], state[h].astype(BF16), ((1,), (1,))))
        adds.append(_dot(ub[:, sl], Bs[g], ((0,), (0,))))
    y = jnp.concatenate(yds, axis=1) + jnp.concatenate(yos, axis=1) * per_head(jnp.exp(acs)) + per_head(vecs[2:3]) * xs
    decay = jnp.stack([jnp.broadcast_to(jnp.exp(alast[:, h:h + 1]), (SSD_P, SSD_N)) for h in range(SSD_HEADS)])
    return y, jnp.stack(adds) + state * decay


def _ssd_fwd(xact, proj, svec, name):
    S = xact.shape[0]
    nc = S // CHUNK

    def body(x_ref, dt_ref, v_ref, y_ref, st_ref, state):
        @pl.when(pl.program_id(0) == 0)
        def _():
            state[...] = jnp.zeros_like(state)

        st_ref[0] = state[...]
        x = x_ref[...]
        y, sn = fn_ssd_chunk(x[:, 0:512], x[:, 512:768], x[:, 768:1024], dt_ref[...], state[...], v_ref[...])
        y_ref[...] = y
        state[...] = sn

    return pl.pallas_call(
        body, name=name, grid=(nc,),
        in_specs=[pl.BlockSpec((CHUNK, D_CONV), lambda i: (i, 0)), pl.BlockSpec((CHUNK, LANES), lambda i: (i, 16)),
                  pl.BlockSpec((8, LANES), lambda i: (0, 0))],
        out_specs=[pl.BlockSpec((CHUNK, D_SSD), lambda i: (i, 0)),
                   pl.BlockSpec((1, SSD_HEADS, SSD_P, SSD_N), lambda i: (i, 0, 0, 0))],
        out_shape=[jax.ShapeDtypeStruct((S, D_SSD), F32), jax.ShapeDtypeStruct((nc, SSD_HEADS, SSD_P, SSD_N), F32)],
        scratch_shapes=[pltpu.VMEM((SSD_HEADS, SSD_P, SSD_N), F32)],
        compiler_params=_cp(("arbitrary",)),
    )(xact, proj, svec)


def _ssd_bwd(xact, proj, svec, states, dy, name):
    S = xact.shape[0]
    nc = S // CHUNK

    def body(x_ref, dt_ref, v_ref, st_ref, dy_ref, dx_ref, ddt_ref, dv_ref, dstate):
        @pl.when(pl.program_id(0) == 0)
        def _():
            dstate[...] = jnp.zeros_like(dstate)
            dv_ref[...] = jnp.zeros_like(dv_ref)

        x = x_ref[...]
        _, vjp = jax.vjp(fn_ssd_chunk, x[:, 0:512], x[:, 512:768], x[:, 768:1024], dt_ref[...], st_ref[0], v_ref[...])
        dxs, dbm, dcm, ddt, dst, dvec = vjp((dy_ref[...], dstate[...]))
        dx_ref[:, 0:512] = dxs
        dx_ref[:, 512:768] = dbm
        dx_ref[:, 768:1024] = dcm
        ddt_ref[...] = ddt
        dstate[...] = dst
        dv_ref[...] += dvec

    rev = lambda i: (nc - 1 - i, 0)
    return pl.pallas_call(
        body, name=name, grid=(nc,),
        in_specs=[pl.BlockSpec((CHUNK, D_CONV), rev), pl.BlockSpec((CHUNK, LANES), lambda i: (nc - 1 - i, 16)),
                  pl.BlockSpec((8, LANES), lambda i: (0, 0)),
                  pl.BlockSpec((1, SSD_HEADS, SSD_P, SSD_N), lambda i: (nc - 1 - i, 0, 0, 0)),
                  pl.BlockSpec((CHUNK, D_SSD), rev)],
        out_specs=[pl.BlockSpec((CHUNK, D_CONV), rev), pl.BlockSpec((CHUNK, LANES), rev),
                   pl.BlockSpec((8, LANES), lambda i: (0, 0))],
        out_shape=[jax.ShapeDtypeStruct((S, D_CONV), F32), jax.ShapeDtypeStruct((S, LANES), F32),
                   jax.ShapeDtypeStruct((8, LANES), F32)],
        scratch_shapes=[pltpu.VMEM((SSD_HEADS, SSD_P, SSD_N), F32)],
        compiler_params=_cp(("arbitrary",)),
    )(xact, proj, svec, states, dy)


def _ada_fwd(c_all, w_ada, b_sh, name):
    nb = 1536 // 512

    def body(c_ref, w_ref, b_ref, o_ref):
        ca = jax.nn.silu(c_ref[...]).astype(BF16)
        o_ref[0] = _dot(ca, w_ref[0].astype(BF16), ((1,), (0,))) + b_ref[0]

    return pl.pallas_call(
        body, name=name, grid=(DEPTH, nb),
        in_specs=[pl.BlockSpec((8, D_MODEL), lambda l, j: (0, 0)), pl.BlockSpec((1, D_MODEL, 512), lambda l, j: (l, 0, j)),
                  pl.BlockSpec((1, 1, 512), lambda l, j: (l, 0, j))],
        out_specs=pl.BlockSpec((1, 8, 512), lambda l, j: (l, 0, j)),
        out_shape=jax.ShapeDtypeStruct((DEPTH, 8, 1536), F32),
        compiler_params=_cp(("parallel", "parallel")),
    )(c_all, w_ada, b_sh)


def _ada_bwd(c_all_t, dmod_sh, name):
    nb = 1536 // 512

    def body(c_ref, d_ref, o_ref):
        ca = jax.nn.silu(c_ref[...])
        acc = ca[:, 0:1] * d_ref[0, 0:1, :]
        for b in range(1, 8):
            acc = acc + ca[:, b:b + 1] * d_ref[0, b:b + 1, :]
        o_ref[0] = acc

    return pl.pallas_call(
        body, name=name, grid=(DEPTH, nb),
        in_specs=[pl.BlockSpec((D_MODEL, 8), lambda l, j: (0, 0)), pl.BlockSpec((1, 8, 512), lambda l, j: (l, 0, j))],
        out_specs=pl.BlockSpec((1, D_MODEL, 512), lambda l, j: (l, 0, j)),
        out_shape=jax.ShapeDtypeStruct((DEPTH, D_MODEL, 1536), F32),
        compiler_params=_cp(("parallel", "parallel")),
    )(c_all_t, dmod_sh)


def _rows_tile(rows):
    return next(t for t in (512, 256, 128, 64, 32, 16, 8) if rows % t == 0)


SUM_BLOCKS = 4
ADAM_BLOCKS = 8


def _sum_sibling(gs, ls, ci, name):
    n = len(gs)

    def body(c_ref, *refs):
        for p in range(n):
            refs[2 * n + p][...] = refs[2 * p][...] + refs[2 * p + 1][...]

    in_specs, out_specs, out_shape = [], [], []
    for g in gs:
        _, _, rh, cw = g.shape
        rb = rh // SUM_BLOCKS
        in_specs += [pl.BlockSpec((None, None, rb, cw), lambda s, i, c: (s, c[0], i, 0)),
                     pl.BlockSpec((None, rb, cw), lambda s, i, c: (s, i, 0))]
        out_specs.append(pl.BlockSpec((None, rb, cw), lambda s, i, c: (s, i, 0)))
        out_shape.append(jax.ShapeDtypeStruct((4, rh, cw), F32))
    ops = [a for pair in zip(gs, ls) for a in pair]
    return pl.pallas_call(
        body, name=name,
        grid_spec=pltpu.PrefetchScalarGridSpec(num_scalar_prefetch=1, grid=(4, SUM_BLOCKS), in_specs=in_specs, out_specs=out_specs),
        out_shape=out_shape, compiler_params=_cp(("parallel", "parallel")),
    )(ci.reshape(1).astype(jnp.int32), *ops)


def _sum_chips(cs, lands, chip, ci, name):
    n = len(cs)

    def body(c_ref, *refs):
        for p in range(n):
            a = refs[4 * p:4 * p + 4]
            refs[4 * n + p][...] = ((a[0][...] + a[1][...]) + a[2][...]) + a[3][...]

    in_specs, out_specs, out_shape = [], [], []
    for c in cs:
        _, rh, cw = c.shape
        rb = rh // SUM_BLOCKS
        in_specs.append(pl.BlockSpec((None, rb, cw), lambda i, ch: (ch[0], i, 0)))
        in_specs += [pl.BlockSpec((None, rb, cw), functools.partial(lambda i, ch, k: (k, i, 0), k=k)) for k in range(3)]
        out_specs.append(pl.BlockSpec((None, rb, cw), lambda i, ch: (ch[1], i, 0)))
        out_shape.append(jax.ShapeDtypeStruct((2, rh, cw), F32))
    ops = [a for c, l in zip(cs, lands) for a in (c, l, l, l)]
    return pl.pallas_call(
        body, name=name,
        grid_spec=pltpu.PrefetchScalarGridSpec(num_scalar_prefetch=1, grid=(SUM_BLOCKS,), in_specs=in_specs, out_specs=out_specs),
        out_shape=out_shape, compiler_params=_cp(("parallel",)),
    )(jnp.stack([chip, ci]).astype(jnp.int32), *ops)


def _adam_update(w, m, v, g):
    c1 = 1.0 / (1.0 - ADAM_B1 ** ADAM_STEP)
    c2 = 1.0 / (1.0 - ADAM_B2 ** ADAM_STEP)
    nm = ADAM_B1 * m + (1.0 - ADAM_B1) * g
    nv = ADAM_B2 * v + (1.0 - ADAM_B2) * (g * g)
    return -ADAM_LR * ((nm * c1) / (jnp.sqrt(nv * c2) + ADAM_EPS) + ADAM_WD * w), nm, nv


def _adam_multi(ws, ms, vs, gs, name):
    n = len(ws)
    per = 3 + DEPTH

    def body(*refs):
        layer = pl.program_id(0)
        for p in range(n):
            w, m, v = [refs[per * p + k][...] for k in range(3)]
            g = refs[per * p + 3][...]
            for l in range(1, DEPTH):
                g = jnp.where(layer == l, refs[per * p + 3 + l][...], g)
            d, nm, nv = _adam_update(w, m, v, g)
            for k, val in enumerate((g, d, nm, nv)):
                refs[per * n + 4 * p + k][...] = val

    in_specs, out_specs, out_shape = [], [], []
    for w in ws:
        _, r, cw = w.shape
        if r % (8 * ADAM_BLOCKS) == 0:
            spec = pl.BlockSpec((None, r // ADAM_BLOCKS, cw), lambda l, i: (l, i, 0))
            gspec = pl.BlockSpec((r // ADAM_BLOCKS, cw), lambda l, i: (i, 0))
        else:
            spec = pl.BlockSpec((None, r, cw // ADAM_BLOCKS), lambda l, i: (l, 0, i))
            gspec = pl.BlockSpec((r, cw // ADAM_BLOCKS), lambda l, i: (0, i))
        in_specs += [spec] * 3 + [gspec] * DEPTH
        out_specs += [spec] * 4
        out_shape += [jax.ShapeDtypeStruct(w.shape, F32)] * 4
    ops = [a for w, m, v, g in zip(ws, ms, vs, gs) for a in (w, m, v, *g)]
    res = pl.pallas_call(
        body, name=name, grid=(DEPTH, ADAM_BLOCKS), in_specs=in_specs, out_specs=out_specs, out_shape=out_shape,
        compiler_params=_cp(("parallel", "parallel")),
    )(*ops)
    return res[0::4], res[1::4], res[2::4], res[3::4]


def _adam(w, m, v, parts, name):
    rows, width = w.shape
    bm = min(256, _rows_tile(rows))
    np_ = len(parts)
    c1 = 1.0 / (1.0 - ADAM_B1 ** ADAM_STEP)
    c2 = 1.0 / (1.0 - ADAM_B2 ** ADAM_STEP)

    def body(*refs):
        w_ref, m_ref, v_ref = refs[:3]
        g = refs[3][...]
        for r in refs[4:3 + np_]:
            g = g + r[...]
        g_ref, d_ref, nm_ref, nv_ref = refs[3 + np_:]
        nm = ADAM_B1 * m_ref[...] + (1.0 - ADAM_B1) * g
        nv = ADAM_B2 * v_ref[...] + (1.0 - ADAM_B2) * (g * g)
        g_ref[...] = g
        nm_ref[...] = nm
        nv_ref[...] = nv
        d_ref[...] = -ADAM_LR * ((nm * c1) / (jnp.sqrt(nv * c2) + ADAM_EPS) + ADAM_WD * w_ref[...])

    blk = pl.BlockSpec((bm, width), lambda i: (i, 0))
    return pl.pallas_call(
        body, name=name, grid=(rows // bm,),
        in_specs=[blk, blk, blk] + [pl.BlockSpec((bm, width), functools.partial(lambda i, o: (i + o, 0), o=off // bm))
                                    for (_, off) in parts],
        out_specs=[blk, blk, blk, blk],
        out_shape=[jax.ShapeDtypeStruct((rows, width), F32)] * 4,
        compiler_params=_cp(("parallel",)),
    )(w, m, v, *[p[0] for p in parts])


def _coords():
    return lax.axis_index("x"), lax.axis_index("y"), lax.axis_index("c")


def _other_chips(x, y):
    return [(1 - x, y), (x, 1 - y), (1 - x, 1 - y)]


def _ag8(blk, name):
    m_per, n = blk.shape

    def body(x_ref, out_ref, send_sems, recv_sems, local_sem):
        x, y, c = _coords()
        me, sibling = (x, y, c), (x, y, 1 - c)
        chips = _other_chips(x, y)

        def rows(px, py, pc):
            return out_ref.at[pl.ds((4 * px + 2 * py + pc) * m_per, m_per), :]

        def copy(k, block, to, src=None):
            return pltpu.make_async_remote_copy(
                src_ref=rows(*block) if src is None else src, dst_ref=rows(*block),
                send_sem=send_sems.at[k], recv_sem=recv_sems.at[k], device_id=to, device_id_type=MESH)

        mine = pltpu.make_async_copy(x_ref, rows(*me), local_sem)
        mine.start()
        first = [copy(0, me, sibling, src=x_ref)]
        first += [copy(1 + j, me, (*chip, c), src=x_ref) for j, chip in enumerate(chips)]
        for cp in first:
            cp.start()
        passed = [copy(4 + j, (*chip, c), sibling) for j, chip in enumerate(chips)]
        for j, chip in enumerate(chips):
            copy(1 + j, (*chip, c), me).wait_recv()
            passed[j].start()
        copy(0, sibling, me).wait_recv()
        for j, chip in enumerate(chips):
            copy(4 + j, (*chip, 1 - c), me).wait_recv()
        for cp in first + passed:
            cp.wait_send()
        mine.wait()

    return pl.pallas_call(
        body, name=name,
        out_shape=jax.ShapeDtypeStruct((8 * m_per, n), blk.dtype),
        in_specs=[pl.BlockSpec(memory_space=pltpu.VMEM)], out_specs=pl.BlockSpec(memory_space=pltpu.VMEM),
        scratch_shapes=[pltpu.SemaphoreType.DMA((7,)), pltpu.SemaphoreType.DMA((7,)), pltpu.SemaphoreType.DMA],
    )(blk)


HBM_SPEC = pl.BlockSpec(memory_space=pltpu.HBM)
SEM_SPEC = pl.BlockSpec(memory_space=pltpu.SEMAPHORE)
EFFECT = pltpu.SideEffectType.DATAFLOW_SIDE_EFFECTING


def _remote(src, dst, send_sem, recv_sem, to):
    return pltpu.make_async_remote_copy(src_ref=src, dst_ref=dst, send_sem=send_sem, recv_sem=recv_sem,
                                        device_id=to, device_id_type=MESH)


def _ag_list(shards, name):
    n = len(shards)

    def body(*refs):
        sh, out = refs[:n], refs[n:2 * n]
        send_sems, recv_sems = refs[2 * n:]
        x, y, c = _coords()
        sibling = (x, y, 1 - c)
        chips = _other_chips(x, y)
        first = [_remote(sh[p].at[c], out[p].at[2 * x + y, c], send_sems.at[6 * p + j], recv_sems.at[6 * p + j], (px, py, c))
                 for p in range(n) for j, (px, py) in enumerate(chips)]
        for cp in first:
            cp.start()
        passed = []
        for j, (px, py) in enumerate(chips):
            for p in range(n):
                got = out[p].at[2 * px + py, c]
                _remote(got, got, send_sems.at[6 * p + j], recv_sems.at[6 * p + j], (x, y, c)).wait_recv()
                cp = _remote(got, got, send_sems.at[6 * p + 3 + j], recv_sems.at[6 * p + 3 + j], sibling)
                cp.start()
                passed.append(cp)
        for j, (px, py) in enumerate(chips):
            for p in range(n):
                got = out[p].at[2 * px + py, 1 - c]
                _remote(got, got, send_sems.at[6 * p + 3 + j], recv_sems.at[6 * p + 3 + j], (x, y, c)).wait_recv()
        for cp in first + passed:
            cp.wait_send()

    return pl.pallas_call(
        body, name=name,
        out_shape=[jax.ShapeDtypeStruct((4,) + s.shape, s.dtype) for s in shards],
        in_specs=[pl.BlockSpec(memory_space=pl.ANY)] * n, out_specs=[pl.BlockSpec(memory_space=pl.ANY)] * n,
        scratch_shapes=[pltpu.SemaphoreType.DMA((6 * n,)), pltpu.SemaphoreType.DMA((6 * n,))],
    )(*shards)


def _ag_direct_copies(sh, land, send_sems, recv_sems, starting):
    x, y, c = _coords()
    return [_remote(sh[p], land[p].at[2 * x + y] if starting else land[p].at[2 * px + py],
                    send_sems.at[3 * p + j], recv_sems.at[3 * p + j], (px, py, c))
            for p in range(len(sh)) for j, (px, py) in enumerate(_other_chips(x, y))]


def _rs_sibling_copies(g, land, send_sems, recv_sems, starting):
    x, y, c = _coords()
    return [_remote(g[p].at[s, 1 - c], land[p].at[s], send_sems.at[4 * p + s], recv_sems.at[4 * p + s], (x, y, 1 - c))
            for p in range(len(g)) for s in range(4)]


def _rs_chips_copies(cs, land, send_sems, recv_sems, starting):
    x, y, c = _coords()
    return [_remote(cs[p].at[2 * px + py], land[p].at[j], send_sems.at[3 * p + j], recv_sems.at[3 * p + j], (px, py, c))
            for p in range(len(cs)) for j, (px, py) in enumerate(_other_chips(x, y))]


def _split_start(copies, srcs, land_shapes, per, name):
    n = len(srcs)

    def body(*refs):
        for cp in copies(refs[:n], refs[n:2 * n], refs[2 * n], refs[2 * n + 1], True):
            cp.start()
        token = refs[4 * n + 2]
        token[...] = jnp.zeros_like(token)

    lands = [pltpu.with_memory_space_constraint(lax.empty(shp, s.dtype), pltpu.HBM) for shp, s in zip(land_shapes, srcs)]
    res = pl.pallas_call(
        body, name=name,
        out_shape=(pltpu.SemaphoreType.DMA((per * n,)), pltpu.SemaphoreType.DMA((per * n,)))
        + tuple(pltpu.HBM(s.shape, s.dtype) for s in srcs) + tuple(pltpu.HBM(l.shape, l.dtype) for l in lands)
        + (jax.ShapeDtypeStruct((8, LANES), F32),),
        in_specs=(HBM_SPEC,) * (2 * n), out_specs=(SEM_SPEC, SEM_SPEC) + (HBM_SPEC,) * (2 * n) + (pl.BlockSpec(memory_space=pltpu.VMEM),),
        input_output_aliases={i: 2 + i for i in range(2 * n)},
        compiler_params=pltpu.CompilerParams(has_side_effects=EFFECT),
    )(*[pltpu.with_memory_space_constraint(s, pltpu.HBM) for s in srcs], *lands)
    return res[0], res[1], res[2:2 + n], res[2 + n:2 + 2 * n], res[2 + 2 * n]


def _split_wait(copies, send_sems, recv_sems, src_thru, land_thru, after, name):
    n = len(src_thru)

    def body(*refs):
        for cp in copies(refs[:n], refs[n:2 * n], refs[2 * n], refs[2 * n + 1], False):
            cp.wait_send()
            cp.wait_recv()

    res = pl.pallas_call(
        body, name=name,
        out_shape=tuple(pltpu.HBM(s.shape, s.dtype) for s in src_thru) + tuple(pltpu.HBM(l.shape, l.dtype) for l in land_thru),
        in_specs=(HBM_SPEC,) * (2 * n) + (SEM_SPEC, SEM_SPEC, pl.BlockSpec(memory_space=pl.ANY)),
        out_specs=(HBM_SPEC,) * (2 * n), input_output_aliases={i: i for i in range(2 * n)},
        compiler_params=pltpu.CompilerParams(has_side_effects=EFFECT),
    )(*src_thru, *land_thru, send_sems, recv_sems, after)
    return res[:n], res[n:]


def _ag_direct_start(shards, name):
    return _split_start(_ag_direct_copies, shards, [(4,) + s.shape for s in shards], 3, name)


def _ag_direct_wait(send_sems, recv_sems, sh_thru, land_thru, after, name):
    return _split_wait(_ag_direct_copies, send_sems, recv_sems, sh_thru, land_thru, after, name)[1]


def _rs_sibling_start(gs, name):
    return _split_start(_rs_sibling_copies, gs, [(4,) + g.shape[2:] for g in gs], 4, name)


def _rs_sibling_wait(send_sems, recv_sems, g_thru, land_thru, after, name):
    return _split_wait(_rs_sibling_copies, send_sems, recv_sems, g_thru, land_thru, after, name)


def _rs_sibling_list(gs, name):
    n = len(gs)

    def body(*refs):
        cps = _rs_sibling_copies(refs[:n], refs[n:2 * n], refs[2 * n], refs[2 * n + 1], True)
        for cp in cps:
            cp.start()
        for cp in cps:
            cp.wait_recv()
        for cp in cps:
            cp.wait_send()

    return pl.pallas_call(
        body, name=name,
        out_shape=[jax.ShapeDtypeStruct((4,) + g.shape[2:], g.dtype) for g in gs],
        in_specs=[pl.BlockSpec(memory_space=pl.ANY)] * n, out_specs=[pl.BlockSpec(memory_space=pl.ANY)] * n,
        scratch_shapes=[pltpu.SemaphoreType.DMA((4 * n,)), pltpu.SemaphoreType.DMA((4 * n,))],
    )(*gs)


def _rs_chips_start(cs, name):
    return _split_start(_rs_chips_copies, cs, [(3,) + c.shape[1:] for c in cs], 3, name)


def _rs_chips_wait(send_sems, recv_sems, cs_thru, land_thru, after, name):
    return _split_wait(_rs_chips_copies, send_sems, recv_sems, cs_thru, land_thru, after, name)


def _swap_list(ghs, name):
    n = len(ghs)

    def body(*refs):
        g, out, send_sems, recv_sems = refs[:n], refs[n:2 * n], refs[2 * n], refs[2 * n + 1]
        x, y, c = _coords()
        cps = [_remote(g[p].at[c], out[p].at[c], send_sems.at[p], recv_sems.at[p], (x, y, 1 - c)) for p in range(n)]
        for cp in cps:
            cp.start()
        for p in range(n):
            _remote(g[p].at[c], out[p].at[1 - c], send_sems.at[p], recv_sems.at[p], (x, y, 1 - c)).wait_recv()
        for cp in cps:
            cp.wait_send()

    return pl.pallas_call(
        body, name=name,
        out_shape=[jax.ShapeDtypeStruct(g.shape, g.dtype) for g in ghs],
        in_specs=[pl.BlockSpec(memory_space=pl.ANY)] * n, out_specs=[pl.BlockSpec(memory_space=pl.ANY)] * n,
        input_output_aliases={p: p for p in range(n)},
        scratch_shapes=[pltpu.SemaphoreType.DMA((n,)), pltpu.SemaphoreType.DMA((n,))],
    )(*ghs)


def _pad_win(w):
    return jnp.concatenate([w[:, :416], jnp.zeros((w.shape[0], 96), w.dtype), w[:, 416:1952],
                            w[:, 1952:1960], jnp.zeros((w.shape[0], 120), w.dtype)], axis=1)


def _unpad_win(g):
    return jnp.concatenate([g[:, :416], g[:, 512:2048], g[:, 2048:2056]], axis=1)


def _pad_wq(w):
    return jnp.pad(w.reshape(Q_LORA, HEADS, QK_DIM), ((0, 0), (0, 0), (0, LANES - QK_DIM))).reshape(Q_LORA, HEADS * LANES)


def _unpad_wq(g):
    return g.reshape(Q_LORA, HEADS, LANES)[:, :, :QK_DIM].reshape(Q_LORA, HEADS * QK_DIM)


def _cols_to_shards(a):
    r, c4 = a.shape
    return a.reshape(r, 4, c4 // 4).transpose(1, 0, 2)


def _shards_to_cols(a):
    _, r, c = a.shape
    return a.transpose(1, 0, 2).reshape(r, 4 * c)


def _pack_small(tree):
    parts = []
    for l in range(DEPTH):
        for (n, k) in SMALL:
            parts.append(jnp.pad(tree[n][l].reshape(-1), (0, -k % LANES)))
    flat = jnp.concatenate(parts)
    return jnp.pad(flat, (0, SMALL_ROWS * LANES - flat.shape[0])).reshape(SMALL_ROWS, LANES)


def _unpack_small(buf):
    flat = buf.reshape(-1)
    out = {n: [] for (n, _) in SMALL}
    o = 0
    for l in range(DEPTH):
        for (n, k) in SMALL:
            out[n].append(flat[o:o + k])
            o += k + (-k % LANES)
    return {n: jnp.stack(v) for n, v in out.items()}


def _vec(v, width=LANES):
    return jnp.pad(v.reshape(1, -1), ((0, 0), (0, width - v.shape[-1])))


def kernel(x, c, positions, norm1_w, norm2_w, w_ada, b_ada, w_in, q_a_norm_w, w_q_up, kv_a_norm_w, w_kv_up, q_nope_norm_w, q_pe_norm_w, k_nope_norm_w, k_pe_norm_w, conv_w, conv_b, dt_bias, a_log, d_skip, ssd_norm_w, w_out, w_gate_up, w_down, loss_target, m_norm1_w, m_norm2_w, m_w_ada, m_b_ada, m_w_in, m_q_a_norm_w, m_w_q_up, m_kv_a_norm_w, m_w_kv_up, m_q_nope_norm_w, m_q_pe_norm_w, m_k_nope_norm_w, m_k_pe_norm_w, m_conv_w, m_conv_b, m_dt_bias, m_a_log, m_d_skip, m_ssd_norm_w, m_w_out, m_w_gate_up, m_w_down, v_norm1_w, v_norm2_w, v_w_ada, v_b_ada, v_w_in, v_q_a_norm_w, v_w_q_up, v_kv_a_norm_w, v_w_kv_up, v_q_nope_norm_w, v_q_pe_norm_w, v_k_nope_norm_w, v_k_pe_norm_w, v_conv_w, v_conv_b, v_dt_bias, v_a_log, v_d_skip, v_ssd_norm_w, v_w_out, v_w_gate_up, v_w_down):
    W = dict(zip(WEIGHTS, (norm1_w, norm2_w, w_ada, b_ada, w_in, q_a_norm_w, w_q_up, kv_a_norm_w, w_kv_up, q_nope_norm_w, q_pe_norm_w, k_nope_norm_w, k_pe_norm_w, conv_w, conv_b, dt_bias, a_log, d_skip, ssd_norm_w, w_out, w_gate_up, w_down)))
    M = dict(zip(WEIGHTS, (m_norm1_w, m_norm2_w, m_w_ada, m_b_ada, m_w_in, m_q_a_norm_w, m_w_q_up, m_kv_a_norm_w, m_w_kv_up, m_q_nope_norm_w, m_q_pe_norm_w, m_k_nope_norm_w, m_k_pe_norm_w, m_conv_w, m_conv_b, m_dt_bias, m_a_log, m_d_skip, m_ssd_norm_w, m_w_out, m_w_gate_up, m_w_down)))
    V = dict(zip(WEIGHTS, (v_norm1_w, v_norm2_w, v_w_ada, v_b_ada, v_w_in, v_q_a_norm_w, v_w_q_up, v_kv_a_norm_w, v_w_kv_up, v_q_nope_norm_w, v_q_pe_norm_w, v_k_nope_norm_w, v_k_pe_norm_w, v_conv_w, v_conv_b, v_dt_bias, v_a_log, v_d_skip, v_ssd_norm_w, v_w_out, v_w_gate_up, v_w_down)))
    S = x.shape[1]
    xi, yi, ci = _coords()
    chip = 2 * xi + yi
    dev = 2 * chip + ci
    x0 = x[0]
    tgt = loss_target[0]

    inv_freq = 1.0 / (ROPE_THETA ** (jnp.arange(0, ROPE, 2, dtype=F32) / ROPE))
    ang = positions[0].astype(F32)[:, None] * inv_freq
    cos, sin = jnp.cos(ang), jnp.sin(ang)
    z16, z32, z64 = jnp.zeros((S, 16), F32), jnp.zeros((S, 32), F32), jnp.zeros((S, 64), F32)
    tab_c = jnp.concatenate([jnp.ones((S, 64), F32), cos, cos, z32], axis=1)
    tab_s1 = jnp.concatenate([z64, z16, sin, z32], axis=1)
    tab_s2 = jnp.concatenate([z64, -sin, z16, z32], axis=1)

    blk0 = jnp.concatenate([c.reshape(-1), W['conv_w'].reshape(-1)]).reshape(24, LANES)
    g0 = _ag8(blk0, "ag_c_conv").reshape(8, 24 * LANES)
    c_all = g0[:, :D_MODEL]
    conv_full = g0[0::2, D_MODEL:].reshape(4, DEPTH, CONV_TAPS, 256).transpose(1, 2, 0, 3).reshape(DEPTH, CONV_TAPS, D_CONV)

    sh = [{n: W[n][l].astype(BF16) for n in BIG} for l in range(DEPTH)]
    got_first = _ag_list([sh[0][n].reshape(2, sh[0][n].shape[0] // 2, sh[0][n].shape[1]) for n in FIRST], "ag_w0_first")
    to_operand = dict(w_in=lambda a: _pad_win(_shards_to_cols(a)), w_q_up=lambda a: _pad_wq(_shards_to_cols(a)),
                      w_kv_up=_shards_to_cols, w_out=lambda a: a.reshape(D_MODEL, D_MODEL), w_gate_up=lambda a: a,
                      w_down=lambda a: a.reshape(D_FF, D_MODEL))

    def layer_weights(names, gathered, own):
        return {n: to_operand[n](lax.dynamic_update_slice_in_dim(a.reshape(4, -1, a.shape[-1]), own[n][None], chip, axis=0))
                for n, a in zip(names, gathered)}

    LW = [layer_weights(FIRST, got_first, sh[0]), None]

    b_sh = lax.dynamic_slice_in_dim(W['b_ada'], chip * 1536, 1536, axis=1).reshape(DEPTH, 1, 1536)
    mod_sh = _ada_fwd(c_all, W['w_ada'], b_sh, "ada_fwd")
    g1 = _ag8(mod_sh.reshape(192, LANES), "ag_mod").reshape(8, DEPTH, 8, 1536)
    mod_all = g1[0::2].transpose(1, 2, 0, 3).reshape(DEPTH, 8, 6 * D_MODEL)
    mod = lax.dynamic_index_in_dim(mod_all, dev, axis=1, keepdims=False)
    mod, rest0 = lax.optimization_barrier((mod, [sh[0][n] for n in REST]))
    ag0 = _ag_direct_start(rest0, "ag_w0_rest_start")

    def mvec(l, k):
        return mod[l, k * D_MODEL:(k + 1) * D_MODEL].reshape(1, D_MODEL)

    def small(name, l, width=None):
        v = W[name][l]
        return _vec(v, width or v.shape[-1])

    def wq_vec(l):
        return _vec(jnp.concatenate([W['q_nope_norm_w'][l], W['q_pe_norm_w'][l]]))

    def wk_vec(l):
        return _vec(jnp.concatenate([W['k_nope_norm_w'][l], W['k_pe_norm_w'][l]]))

    def conv_vec(l):
        return jnp.concatenate([conv_full[l], W['conv_b'][l].reshape(1, D_CONV), jnp.zeros((3, D_CONV), F32)], axis=0)

    def ssd_vec(l):
        return jnp.concatenate([_vec(W['dt_bias'][l]), _vec(W['a_log'][l]), _vec(W['d_skip'][l]), jnp.zeros((5, LANES), F32)], axis=0)

    sv = []
    xcur = x0
    h1 = _row_fwd(fn_norm_mod, "norm_mod_f", [(x0, 0, D_MODEL)], [small('norm1_w', 0) + ag0[4][0, 0], mvec(0, 1), mvec(0, 0)],
                  [(D_MODEL, BF16)])[0]
    fin = None
    ag_first = None
    ag_rest = ag0
    for l in range(DEPTH):
        if l == 1:
            LW[1] = layer_weights(FIRST, _ag_direct_wait(*ag_first[:4], xcur, "ag_w1_first_wait"), sh[1])
        lw = LW[l]
        t = dict(xcur=xcur, h1=h1)
        t['proj'] = proj = _mm(h1, lw['w_in'], 'nn', f"mm_in_{l}")
        t['qa_n'], t['kva_n'] = _row_fwd(fn_lat_norm, f"lat_norm_f{l}", [(proj, 0, 256), (proj, 2, 128)],
                                         [small('q_a_norm_w', l), small('kv_a_norm_w', l)], [(256, BF16), (128, BF16)])
        t['q'] = _mm(t['qa_n'], lw['w_q_up'], 'nn', f"mm_q_{l}")
        t['kv'] = _mm(t['kva_n'], lw['w_kv_up'], 'nn', f"mm_kv_{l}")
        t['qf'], t['kf'], t['vv'], t['kT'] = _row_fwd(
            fn_qk_prep_kt, f"qk_prep_f{l}",
            [(t['q'], 0, 1024), (t['kv'], 0, 1024), (proj, 3, 128), (tab_c, 0, 128), (tab_s1, 0, 128), (tab_s2, 0, 128)],
            [wq_vec(l), wk_vec(l)], [(1024, BF16), (1024, BF16), (1024, BF16), (1024, BF16)], transposed=(3,))
        t['ao'], t['lse'] = _attn_fwd(t['qf'], t['kf'], t['vv'], f"attn_f{l}")
        t['xact'] = _conv_fwd(proj, conv_vec(l), f"conv_f{l}")
        t['y'], t['states'] = _ssd_fwd(t['xact'], proj, ssd_vec(l), f"ssd_f{l}")
        tie = 0.0
        t['ao'], t['y'] = lax.optimization_barrier((t['ao'], t['y']))
        rest = list(_ag_direct_wait(*ag_rest[:4], t['y'], f"ag_w{l}_rest_wait"))
        if l == 0:
            rest, sh1f, sh1r = lax.optimization_barrier((rest, [sh[1][n] for n in FIRST], [sh[1][n] for n in REST]))
            ag_first = _ag_direct_start(sh1f, "ag_w1_first_start")
            ag_rest = _ag_direct_start(sh1r, "ag_w1_rest_start")
            tie = ag_first[4][0, 0] + ag_rest[4][0, 0]
        lw.update(layer_weights(REST, rest, sh[l]))
        t['mix'] = _row_fwd(fn_gated_mix, f"gated_f{l}", [(t['y'], 0, 512), (proj, 1, 512), (t['ao'], 0, 512)],
                            [small('ssd_norm_w', l) + tie], [(1024, BF16)])[0]
        t['mo'] = _mm(t['mix'], lw['w_out'], 'nn', f"mm_out_{l}")
        t['x1'], t['h2'] = _row_fwd(fn_resid_norm, f"resid_mid_f{l}", [(xcur, 0, D_MODEL), (t['mo'], 0, D_MODEL)],
                                    [mvec(l, 2), small('norm2_w', l), mvec(l, 4), mvec(l, 3)],
                                    [(D_MODEL, F32), (D_MODEL, BF16)])
        t['gu'], t['act'] = _mm_gu_swiglu(t['h2'], lw['w_gate_up'], f"mm_gu_{l}")
        t['ff'] = _mm(t['act'], lw['w_down'], 'nn', f"mm_down_{l}")
        if l + 1 < DEPTH:
            xcur, h1 = _row_fwd(fn_resid_norm, f"resid_end_f{l}", [(t['x1'], 0, D_MODEL), (t['ff'], 0, D_MODEL)],
                                [mvec(l, 5), small('norm1_w', l + 1), mvec(l + 1, 1), mvec(l + 1, 0)],
                                [(D_MODEL, F32), (D_MODEL, BF16)])
        else:
            fin = _final(t['x1'], t['ff'], tgt, mvec(l, 5), "final_loss")
        sv.append(t)

    dx1, dff, dg2_last, loss_acc = fin
    gfull = {n: [None] * DEPTH for n in BIG}
    gsm = {n: [None] * DEPTH for (n, _) in SMALL}
    dmod = [[None] * 6 for _ in range(DEPTH)]
    dmod[DEPTH - 1][5] = dg2_last
    grad_x = None
    pending = []

    def halves_of(l, names):
        return [gfull[n][l].reshape(4, 2, gfull[n][l].shape[1] // 2, gfull[n][l].shape[2]) for n in names]

    def rs_finish(l, names, tag, g4, sib):
        h = _rs_chips_start(_sum_sibling(g4, sib, ci, f"sum_sibling_{tag}"), f"rs_chips_start_{tag}")
        pending.append((l, names, h))
        return h[4][0, 0]

    tie_l1 = tie_l0a = tie_sib = 0.0
    sib_l1 = sib_l0a = None

    for l in reversed(range(DEPTH)):
        t = sv[l]
        lw = LW[l]
        proj = t['proj']
        dgu = _mm_down_dx_swiglu(dff, lw['w_down'], t['gu'], f"mm_down_dx{l}")
        gfull['w_down'][l] = _mm(t['act'], dff, 'tn', f"mm_down_dw{l}").reshape(4, D_FF // 4, D_MODEL)
        dh2 = _mm(dgu, lw['w_gate_up'], 'nt', f"mm_gu_dx{l}", stack='b')
        gfull['w_gate_up'][l] = _mm(t['h2'], dgu, 'tn', f"mm_gu_dw{l}", stack='out')
        if l == 0:
            sib_l0a = _rs_sibling_start(halves_of(0, EARLY), "rs_sibling_start_l0a")
            tie_l1 = rs_finish(1, BIG, "l1", *_rs_sibling_wait(*sib_l1[:4], dh2, "rs_sibling_wait_l1"))
            tie_sib = sib_l0a[4][0, 0]
        dxc, dmo, dmod[l][2], gsm['norm2_w'][l], dmod[l][4], dmod[l][3] = _row_bwd(
            fn_resid_norm, f"resid_mid_b{l}", [(t['xcur'], 0, D_MODEL), (t['mo'], 0, D_MODEL)],
            [mvec(l, 2) + ((tie_l1 + tie_sib) if l == 0 else 0.0), small('norm2_w', l), mvec(l, 4), mvec(l, 3)],
            [(dx1, 0, D_MODEL), (dh2, 0, D_MODEL)], [0, 1], [0, 1, 2, 3], ddtypes=[F32, BF16])
        dmix = _mm(dmo, lw['w_out'], 'nt', f"mm_out_dx{l}")
        gfull['w_out'][l] = _mm(t['mix'], dmo, 'tn', f"mm_out_dw{l}").reshape(4, D_MODEL // 4, D_MODEL)
        dy, dz, gsm['ssd_norm_w'][l] = _row_bwd(fn_gated_norm, f"gated_b{l}", [(t['y'], 0, 512), (proj, 1, 512)],
                                                [small('ssd_norm_w', l)], [(dmix, 1, 512)], [0, 1], [0])
        if l == 0:
            tie_l0a = rs_finish(0, EARLY, "l0a", *_rs_sibling_wait(*sib_l0a[:4], dmix, "rs_sibling_wait_l0a"))
        dxact, ddt, dsv = _ssd_bwd(t['xact'], proj, ssd_vec(l) + (tie_l0a if l == 0 else 0.0), t['states'], dy, f"ssd_b{l}")
        gsm['dt_bias'][l], gsm['a_log'][l], gsm['d_skip'][l] = dsv[0, :8], dsv[1, :8], dsv[2, :8]
        dxbc, dcv = _conv_bwd(proj, conv_vec(l), dxact, f"conv_b{l}")
        gsm['conv_w'][l] = dcv[:CONV_TAPS]
        gsm['conv_b'][l] = dcv[CONV_TAPS]
        delta_r = _attn_delta(dmix, t['ao'], f"attn_delta{l}")
        dqT, dkf, dvv = _attn_bwd(t['qf'], t['kf'], t['kT'], t['vv'], dmix, t['lse'], delta_r, f"attn_b{l}")
        dq, dkv, dkpe, dwq, dwk = _row_bwd(
            fn_qk_prep, f"qk_prep_b{l}",
            [(t['q'], 0, 1024), (t['kv'], 0, 1024), (proj, 3, 128), (tab_c, 0, 128), (tab_s1, 0, 128), (tab_s2, 0, 128)],
            [wq_vec(l), wk_vec(l)], [(dqT, 0, 1024), (dkf, 0, 1024), (dvv, 0, 1024)], [0, 1, 2], [0, 1],
            ddtypes=[BF16, BF16, F32], transposed=(0,))
        gsm['q_nope_norm_w'][l], gsm['q_pe_norm_w'][l] = dwq[0, :NOPE], dwq[0, NOPE:QK_DIM]
        gsm['k_nope_norm_w'][l], gsm['k_pe_norm_w'][l] = dwk[0, :NOPE], dwk[0, NOPE:QK_DIM]
        dqa_n = _mm(dq, lw['w_q_up'], 'nt', f"mm_q_dx{l}")
        gfull['w_q_up'][l] = _cols_to_shards(_unpad_wq(_mm(t['qa_n'], dq, 'tn', f"mm_q_dw{l}")))
        dkva_n = _mm(dkv, lw['w_kv_up'], 'nt', f"mm_kv_dx{l}")
        gfull['w_kv_up'][l] = _cols_to_shards(_mm(t['kva_n'], dkv, 'tn', f"mm_kv_dw{l}"))
        dqa, dkva, dqw, dkvw = _row_bwd(fn_lat_norm, f"lat_norm_b{l}", [(proj, 0, 256), (proj, 2, 128)],
                                        [small('q_a_norm_w', l), small('kv_a_norm_w', l)],
                                        [(dqa_n, 0, 256), (dkva_n, 0, 128)], [0, 1], [0, 1])
        gsm['q_a_norm_w'][l], gsm['kv_a_norm_w'][l] = dqw[0], dkvw[0]
        dproj = jnp.concatenate([dqa, dkva, dkpe, dz, dxbc, ddt], axis=1).astype(BF16)
        dh1 = _mm(dproj, lw['w_in'], 'nt', f"mm_in_dx{l}")
        gfull['w_in'][l] = _cols_to_shards(_unpad_win(_mm(t['h1'], dproj, 'tn', f"mm_in_dw{l}")))
        if l > 0:
            p = sv[l - 1]
            dx1, dff, dmod[l - 1][5], gsm['norm1_w'][l], dmod[l][1], dmod[l][0] = _row_bwd(
                fn_resid_norm, f"resid_end_b{l - 1}", [(p['x1'], 0, D_MODEL), (p['ff'], 0, D_MODEL)],
                [mvec(l - 1, 5), small('norm1_w', l), mvec(l, 1), mvec(l, 0)], [(dxc, 0, D_MODEL), (dh1, 0, D_MODEL)],
                [0, 1], [0, 1, 2, 3], ddtypes=[F32, BF16])
            sib_l1 = _rs_sibling_start(halves_of(l, BIG), f"rs_sibling_start_l{l}")
            dff = dff + sib_l1[4][0, 0].astype(BF16)
        else:
            grad_x, gsm['norm1_w'][l], dmod[l][1], dmod[l][0] = _row_bwd(
                fn_norm_mod_pass, "norm_mod_b", [(x0, 0, D_MODEL)], [small('norm1_w', 0), mvec(0, 1), mvec(0, 0)],
                [(dxc, 0, D_MODEL), (dh1, 0, D_MODEL)], [0], [0, 1, 2])
        for n in ('norm1_w', 'norm2_w', 'ssd_norm_w'):
            gsm[n][l] = gsm[n][l][0]

    for l in range(DEPTH):
        gsm['b_ada'][l] = jnp.concatenate([d[0] for d in dmod[l]])
    sm_part = _pack_small({n: jnp.stack(v) for n, v in gsm.items()}).at[SMALL_ROWS - 1, 0].set(loss_acc[0, 0])
    sm_all = _ag8(sm_part, "ag_small")
    loss = jnp.sum(sm_all.reshape(8, SMALL_ROWS, LANES)[:, SMALL_ROWS - 1, 0])
    sm_all, late = lax.optimization_barrier((sm_all, [gfull[n][0] for n in BIG[2:]]))
    for n, g in zip(BIG[2:], late):
        gfull[n][0] = g
    late4 = halves_of(0, BIG[2:])
    tie_l0b = rs_finish(0, BIG[2:], "l0b", late4, _rs_sibling_list(late4, "rs_sibling_l0b"))

    def with_conv(tree):
        wide = lax.dynamic_update_slice_in_dim(jnp.zeros((DEPTH, CONV_TAPS, D_CONV), F32), tree['conv_w'], chip * 256, axis=2)
        return {**tree, 'conv_w': wide}

    g_sm, d_sm, m_sm, v_sm = _adam(_pack_small(with_conv(W)) + tie_l0b, _pack_small(with_conv(M)), _pack_small(with_conv(V)),
                                   [(sm_all, d * SMALL_ROWS) for d in range(8)], "adam_small")
    out_small = [_unpack_small(b) for b in (g_sm, d_sm, m_sm, v_sm)]
    for o in out_small:
        o['conv_w'] = lax.dynamic_slice_in_dim(o['conv_w'].reshape(DEPTH, CONV_TAPS, D_CONV), chip * 256, 256, axis=2)

    dmod_all = sm_all.reshape(8, SMALL_ROWS * LANES)
    per_layer = sum(k + (-k % LANES) for (_, k) in SMALL)
    dmod_sh = jnp.stack([lax.dynamic_slice_in_dim(dmod_all[:, l * per_layer:l * per_layer + 6 * D_MODEL], chip * 1536, 1536, axis=1)
                         for l in range(DEPTH)])
    g_ada = _ada_bwd(c_all.T, dmod_sh, "ada_bwd")
    ada = _adam(W['w_ada'].reshape(DEPTH * D_MODEL, 1536), M['w_ada'].reshape(DEPTH * D_MODEL, 1536),
                V['w_ada'].reshape(DEPTH * D_MODEL, 1536), [(g_ada.reshape(DEPTH * D_MODEL, 1536), 0)], "adam_ada")
    out_ada = [a.reshape(DEPTH, D_MODEL, 1536) for a in ada]

    keys, cs_all, land_all = [], [], []
    for (l, names, (send_sems, recv_sems, cs_thru, land_thru, _)) in pending:
        cs, lands = _rs_chips_wait(send_sems, recv_sems, cs_thru, land_thru, ada[3], f"rs_chips_wait_l{l}{len(names)}")
        keys += [(l, n) for n in names]
        cs_all += list(cs)
        land_all += list(lands)
    gboth = _swap_list(_sum_chips(cs_all, land_all, chip, ci, "sum_chips"), "swap_halves")
    gshard = {k: g.reshape(2 * g.shape[1], g.shape[2]) for k, g in zip(keys, gboth)}

    def natural(n, a):
        return jnp.swapaxes(a, -1, -2) if n == 'w_in' else a

    res = _adam_multi([natural(n, W[n]) for n in BIG], [natural(n, M[n]) for n in BIG], [natural(n, V[n]) for n in BIG],
                      [[natural(n, gshard[(l, n)]) for l in range(DEPTH)] for n in BIG], "adam_big")
    out_big = [{n: natural(n, a) for n, a in zip(BIG, o)} for o in res]

    outs = [loss, grad_x[None]]
    for k in range(4):
        for n in WEIGHTS:
            if n == 'w_ada':
                outs.append(out_ada[k])
            elif n in BIG:
                outs.append(out_big[k][n])
            else:
                outs.append(out_small[k][n])
    return tuple(outs)
```

```python
import functools

import jax
import jax.numpy as jnp
from jax import lax
from jax.experimental import pallas as pl
from jax.experimental.pallas import tpu as pltpu

F32 = jnp.float32
BF16 = jnp.bfloat16
MESH = pl.DeviceIdType.MESH

D_MODEL = 1024
DEPTH = 2
HEADS = 8
NOPE = 64
ROPE = 32
QK_DIM = NOPE + ROPE
Q_LORA = 256
KV_LORA = 128
SSD_HEADS = 8
SSD_P = 64
SSD_N = 128
CHUNK = 128
CONV_TAPS = 4
D_SSD = 512
D_CONV = 1024
D_FF = 2816
D_IN = 1960
D_IN_PAD = 2176
EPS = 1e-6
ROPE_THETA = 10000.0
ATT_SCALE = QK_DIM ** -0.5
NEG = -1e30
LANES = 128
VMEM_LIMIT = 48 * 1024 * 1024

ADAM_LR, ADAM_B1, ADAM_B2, ADAM_EPS, ADAM_WD, ADAM_STEP = 0.001, 0.9, 0.999, 1e-08, 0.01, 10

WEIGHTS = ['norm1_w', 'norm2_w', 'w_ada', 'b_ada', 'w_in', 'q_a_norm_w', 'w_q_up', 'kv_a_norm_w', 'w_kv_up',
           'q_nope_norm_w', 'q_pe_norm_w', 'k_nope_norm_w', 'k_pe_norm_w', 'conv_w', 'conv_b', 'dt_bias', 'a_log',
           'd_skip', 'ssd_norm_w', 'w_out', 'w_gate_up', 'w_down']
BIG = ['w_down', 'w_gate_up', 'w_out', 'w_kv_up', 'w_q_up', 'w_in']
EARLY = BIG[:2]
FIRST = BIG[3:]
REST = BIG[:3]
SMALL = [('b_ada', 6144), ('conv_w', 4096), ('norm1_w', 1024), ('norm2_w', 1024), ('conv_b', 1024), ('ssd_norm_w', 512),
         ('q_a_norm_w', 256), ('kv_a_norm_w', 128), ('q_nope_norm_w', 64), ('q_pe_norm_w', 32),
         ('k_nope_norm_w', 64), ('k_pe_norm_w', 32), ('dt_bias', 8), ('a_log', 8), ('d_skip', 8)]
SMALL_ROWS = 240


def _cp(sem=None, **kw):
    return pltpu.CompilerParams(dimension_semantics=sem, vmem_limit_bytes=VMEM_LIMIT, **kw)


def _dot(a, b, dims, prec=None):
    return lax.dot_general(a, b, (dims, ((), ())), preferred_element_type=F32, precision=prec)


def _tile(dim, target):
    best = 0
    for t in range(LANES, min(dim, target) + 1, LANES):
        if dim % t == 0:
            best = t
    if best < 256 and dim <= 2304:
        return dim
    return best


def _mm(a, b, mode, name, out_dtype=F32, stack=None):
    ns = None
    halves = (a if mode == 'nt' else b).ndim == 3 and stack is not None and not (stack == 'b' and mode == 'nn')
    if stack == 'b':
        ns = b.shape[2]
        if mode == 'nn':
            (M, K), N = a.shape, 4 * ns
        else:
            M, K, N = a.shape[-2], 4 * ns, b.shape[1]
    elif mode == 'nn':
        (M, K), (_, N) = a.shape, b.shape
    elif mode == 'nt':
        (M, K), (N, _) = a.shape, b.shape
    else:
        (K, M), N = a.shape, (2 * b.shape[2] if halves else b.shape[1])
    if stack == 'out':
        ns = N // 4
    tm, tn, tk = _tile(M, 1408 if mode == 'tn' else 1024), _tile(N, 1408), _tile(K, 1408)
    if stack == 'b' and mode == 'nt':
        tk = ns
    elif stack is not None:
        tn = ns
    nk = K // tk
    dims = {'nn': ((1,), (0,)), 'nt': ((1,), (1,)), 'tn': ((0,), (0,))}[mode]

    def body(a_ref, b_ref, o_ref, *acc):
        part = _dot(a_ref[...].astype(BF16), b_ref[...].astype(BF16), dims)
        if nk == 1:
            o_ref[...] = part.astype(o_ref.dtype)
            return
        k = pl.program_id(2)

        @pl.when(k == 0)
        def _():
            acc[0][...] = part

        @pl.when(k > 0)
        def _():
            acc[0][...] += part

        @pl.when(k == nk - 1)
        def _():
            o_ref[...] = acc[0][...].astype(o_ref.dtype)

    a_spec = pl.BlockSpec((tk, tm), lambda i, j, k: (k, i)) if mode == 'tn' else pl.BlockSpec((tm, tk), lambda i, j, k: (i, k))
    b_spec = pl.BlockSpec((tn, tk), lambda i, j, k: (j, k)) if mode == 'nt' else pl.BlockSpec((tk, tn), lambda i, j, k: (k, j))
    o_spec, o_shape = pl.BlockSpec((tm, tn), lambda i, j, k: (i, j)), (M, N)
    if stack == 'b':
        b_spec = (pl.BlockSpec((None, tn, ns), lambda i, j, k: (k, j, 0)) if mode == 'nt'
                  else pl.BlockSpec((None, tk, ns), lambda i, j, k: (j, k, 0)))
    if stack == 'out':
        o_spec, o_shape = pl.BlockSpec((None, tm, ns), lambda i, j, k: (j, i, 0)), (4, M, ns)
    if halves and mode == 'nt':
        a_spec = pl.BlockSpec((None, tm, ns), lambda i, j, k: (lax.div(k, 2), i, lax.rem(k, 2)))
    if halves and mode == 'tn':
        b_spec = pl.BlockSpec((None, tk, ns), lambda i, j, k: (lax.div(j, 2), k, lax.rem(j, 2)))
    return pl.pallas_call(
        body, name=name, grid=(M // tm, N // tn, nk),
        in_specs=[a_spec, b_spec], out_specs=o_spec,
        out_shape=jax.ShapeDtypeStruct(o_shape, out_dtype),
        scratch_shapes=[pltpu.VMEM((tm, tn), F32)] if nk > 1 else [],
        compiler_params=_cp(("parallel", "parallel", "arbitrary")),
    )(a, b)


def _mm_gu_swiglu(h, wst, name):
    S, K = h.shape
    ns = wst.shape[2]
    tm = _tile(S, 512)

    def body(a_ref, bg_ref, bu_ref, gu_ref, act_ref):
        a = a_ref[...]
        g = _dot(a, bg_ref[...], ((1,), (0,)))
        u = _dot(a, bu_ref[...], ((1,), (0,)))
        gu_ref[0] = g
        gu_ref[1] = u
        act_ref[...] = (g * jax.nn.sigmoid(g) * u).astype(act_ref.dtype)

    return pl.pallas_call(
        body, name=name, grid=(S // tm, 2),
        in_specs=[pl.BlockSpec((tm, K), lambda i, j: (i, 0)), pl.BlockSpec((None, K, ns), lambda i, j: (j, 0, 0)),
                  pl.BlockSpec((None, K, ns), lambda i, j: (j + 2, 0, 0))],
        out_specs=[pl.BlockSpec((2, tm, ns), lambda i, j: (0, i, j)), pl.BlockSpec((tm, ns), lambda i, j: (i, j))],
        out_shape=[jax.ShapeDtypeStruct((2, S, 2 * ns), F32), jax.ShapeDtypeStruct((S, 2 * ns), BF16)],
        compiler_params=_cp(("parallel", "parallel")),
    )(h, wst, wst)


def _mm_down_dx_swiglu(dff, w_down, gu, name):
    S, K = dff.shape
    tm, tn = _tile(S, 512), _tile(D_FF, 1408)

    def body(a_ref, b_ref, g_ref, u_ref, o_ref):
        dact = _dot(a_ref[...].astype(BF16), b_ref[...], ((1,), (1,)))
        g, u = g_ref[...], u_ref[...]
        sg = jax.nn.sigmoid(g)
        o_ref[0] = (dact * u * (sg * (1.0 + g * (1.0 - sg)))).astype(o_ref.dtype)
        o_ref[1] = (dact * (g * sg)).astype(o_ref.dtype)

    return pl.pallas_call(
        body, name=name, grid=(S // tm, D_FF // tn),
        in_specs=[pl.BlockSpec((tm, K), lambda i, j: (i, 0)), pl.BlockSpec((tn, K), lambda i, j: (j, 0)),
                  pl.BlockSpec((None, tm, tn), lambda i, j: (0, i, j)), pl.BlockSpec((None, tm, tn), lambda i, j: (1, i, j))],
        out_specs=pl.BlockSpec((2, tm, tn), lambda i, j: (0, i, j)),
        out_shape=jax.ShapeDtypeStruct((2, S, D_FF), BF16),
        compiler_params=_cp(("parallel", "parallel")),
    )(dff, w_down, gu, gu)


def _rspec(tm, w, cb):
    return pl.BlockSpec((tm, w), lambda i: (i, cb))


def _vspec(shape):
    return pl.BlockSpec(shape, lambda i: (0,) * len(shape))


def _row_fwd(fn, name, rows, vecs, outs, tm=256, transposed=()):
    S = rows[0][0].shape[0]
    tm = min(tm, S)
    nin = len(rows) + len(vecs)

    def body(*refs):
        res = fn(*[r[...] for r in refs[:nin]])
        for k, (o_ref, r) in enumerate(zip(refs[nin:], res)):
            o_ref[...] = (r.T if k in transposed else r).astype(o_ref.dtype)

    return pl.pallas_call(
        body, name=name, grid=(S // tm,),
        in_specs=[_rspec(tm, w, cb) for (_, cb, w) in rows] + [_vspec(v.shape) for v in vecs],
        out_specs=[pl.BlockSpec((w, tm), lambda i: (0, i)) if k in transposed else _rspec(tm, w, 0)
                   for k, (w, _) in enumerate(outs)],
        out_shape=[jax.ShapeDtypeStruct((w, S) if k in transposed else (S, w), dt) for k, (w, dt) in enumerate(outs)],
        compiler_params=_cp(("parallel",)),
    )(*[r[0] for r in rows], *vecs)


def _row_bwd(fn, name, rows, vecs, cts, drows, dvecs, tm=256, ddtypes=None, transposed=()):
    S = rows[0][0].shape[0]
    ddtypes = ddtypes or [F32] * len(drows)
    tm = min(tm, S)
    nr, nv, nc = len(rows), len(vecs), len(cts)
    didx = list(drows) + [nr + j for j in dvecs]

    def body(*refs):
        vals = [r[...] for r in refs[:nr + nv]]
        ct = tuple((r[...].T if k in transposed else r[...]).astype(F32)
                   for k, r in enumerate(refs[nr + nv:nr + nv + nc]))
        outs = refs[nr + nv + nc:]

        def g(*d):
            a = list(vals)
            for k, val in zip(didx, d):
                a[k] = val
            return tuple(fn(*a))

        _, vjp = jax.vjp(g, *[vals[k] for k in didx])
        grads = vjp(ct)
        for o, gr in zip(outs[:len(drows)], grads[:len(drows)]):
            o[...] = gr.astype(o.dtype)

        @pl.when(pl.program_id(0) == 0)
        def _():
            for o in outs[len(drows):]:
                o[...] = jnp.zeros_like(o)

        for o, gr in zip(outs[len(drows):], grads[len(drows):]):
            o[...] += gr

    return pl.pallas_call(
        body, name=name, grid=(S // tm,),
        in_specs=[_rspec(tm, w, cb) for (_, cb, w) in rows] + [_vspec(v.shape) for v in vecs]
        + [pl.BlockSpec((w, tm), lambda i: (0, i)) if k in transposed else _rspec(tm, w, cb) for k, (_, cb, w) in enumerate(cts)],
        out_specs=[_rspec(tm, rows[k][2], 0) for k in drows] + [_vspec(vecs[j].shape) for j in dvecs],
        out_shape=[jax.ShapeDtypeStruct((S, rows[k][2]), dt) for k, dt in zip(drows, ddtypes)]
        + [jax.ShapeDtypeStruct(vecs[j].shape, F32) for j in dvecs],
        compiler_params=_cp(("arbitrary",)),
    )(*[r[0] for r in rows], *vecs, *[c[0] for c in cts])


def _rms(x):
    return x * lax.rsqrt(jnp.mean(x * x, axis=-1, keepdims=True) + EPS)


def fn_norm_mod(x, nw, sc, sh):
    return (_rms(x) * nw * (1.0 + sc) + sh,)


def fn_norm_mod_pass(x, nw, sc, sh):
    return (x, _rms(x) * nw * (1.0 + sc) + sh)


def fn_resid_norm(x, d, g, nw, sc, sh):
    xn = x + g * d
    return (xn, _rms(xn) * nw * (1.0 + sc) + sh)


def fn_lat_norm(qa, kva, qw, kvw):
    return (_rms(qa) * qw, _rms(kva) * kvw)


@functools.partial(jax.custom_vjp, nondiff_argnums=(1,))
def _lroll(x, s):
    return pltpu.roll(x, s, 1)


def _lroll_fwd(x, s):
    return pltpu.roll(x, s, 1), None


def _lroll_bwd(s, _, g):
    return (pltpu.roll(g, (LANES - s) % LANES, 1),)


_lroll.defvjp(_lroll_fwd, _lroll_bwd)


def _lane_masks(shape):
    lane = lax.broadcasted_iota(jnp.int32, shape, 1)
    return (lane < NOPE).astype(F32), ((lane >= NOPE) & (lane < QK_DIM)).astype(F32)


def _rope(t, tc, ts1, ts2):
    return t * tc + _lroll(t, 16) * ts1 + _lroll(t, LANES - 16) * ts2


def fn_qk_prep(q, kv, kpe, tc, ts1, ts2, wq, wk):
    mn, mp = _lane_masks((1, LANES))
    mhi = 1.0 - mn

    def head_norm(t, w):
        rn = lax.rsqrt(jnp.sum(t * t * mn, axis=-1, keepdims=True) * (1.0 / NOPE) + EPS)
        rp = lax.rsqrt(jnp.sum(t * t * mp, axis=-1, keepdims=True) * (1.0 / ROPE) + EPS)
        return t * (rn * mn + rp * mp) * w

    kp = _rope(head_norm(_lroll(kpe, NOPE), wk) * mp, tc, ts1, ts2)
    qs, ks, vs = [], [], []
    for h in range(HEADS):
        qs.append(_rope(head_norm(q[:, h * LANES:(h + 1) * LANES], wq), tc, ts1, ts2))
        t = kv[:, h * LANES:(h + 1) * LANES]
        ks.append(head_norm(t, wk) * mn + kp)
        vs.append(_lroll(t, NOPE) * mn + mhi)
    return (jnp.concatenate(qs, axis=1), jnp.concatenate(ks, axis=1), jnp.concatenate(vs, axis=1))


def fn_qk_prep_kt(*args):
    qf, kf, va = fn_qk_prep(*args)
    return (qf, kf, va, kf)


def fn_gated_norm(y, z, w):
    u = y * jax.nn.silu(z)
    half = D_SSD // 2
    return (jnp.concatenate([_rms(u[:, :half]), _rms(u[:, half:])], axis=1) * w,)


def fn_gated_mix(y, z, ao, w):
    return (jnp.concatenate([ao, fn_gated_norm(y, z, w)[0]], axis=1),)


def _final(x1, ff, tgt, g2, name):
    S = x1.shape[0]
    tm = min(256, S)

    def body(x_ref, f_ref, t_ref, g_ref, dx_ref, df_ref, dg_ref, l_ref):
        @pl.when(pl.program_id(0) == 0)
        def _():
            dg_ref[...] = jnp.zeros_like(dg_ref)
            l_ref[...] = jnp.zeros_like(l_ref)

        f = f_ref[...]
        g = g_ref[...]
        e = x_ref[...] + g * f - t_ref[...]
        dx = e * (1.0 / D_MODEL)
        dx_ref[...] = dx
        df_ref[...] = (g * dx).astype(df_ref.dtype)
        dg_ref[...] += jnp.sum(dx * f, axis=0, keepdims=True)
        l_ref[...] += jnp.sum(e * e) * (0.5 / D_MODEL)

    r = _rspec(tm, D_MODEL, 0)
    return pl.pallas_call(
        body, name=name, grid=(S // tm,),
        in_specs=[r, r, r, _vspec((1, D_MODEL))],
        out_specs=[r, r, _vspec((1, D_MODEL)), _vspec((1, LANES))],
        out_shape=[jax.ShapeDtypeStruct((S, D_MODEL), F32), jax.ShapeDtypeStruct((S, D_MODEL), BF16),
                   jax.ShapeDtypeStruct((1, D_MODEL), F32), jax.ShapeDtypeStruct((1, LANES), F32)],
        compiler_params=_cp(("arbitrary",)),
    )(x1, ff, tgt, g2)


def _causal_mask(t):
    r = lax.broadcasted_iota(jnp.int32, (t, t), 0)
    c = lax.broadcasted_iota(jnp.int32, (t, t), 1)
    return c <= r


LOG2E = 1.4426950408889634
EXP2_SCALE = ATT_SCALE * LOG2E
ATT_TQ, ATT_TK = 512, 1024
ATT_BQ, ATT_BK = 1024, 512


def _attn_fwd(qf, kf, va, name):
    S = qf.shape[0]
    T, TK = min(ATT_TQ, S), min(ATT_TK, S)
    nmask = max(1, T // TK)

    def body(q_ref, k_ref, v_ref, o_ref, l_ref):
        i = pl.program_id(1)
        r = lax.broadcasted_iota(jnp.int32, (T, TK), 0)
        c = lax.broadcasted_iota(jnp.int32, (T, TK), 1)
        qs = [q_ref[:, hh * LANES:(hh + 1) * LANES] for hh in range(2)]

        def blk(j, carry, masked):
            off = pl.multiple_of(j * TK, TK)
            out = []
            for hh in range(2):
                m, acc = carry[hh]
                s = _dot(qs[hh], k_ref[pl.ds(off, TK), hh * LANES:(hh + 1) * LANES], ((1,), (1,)))
                if masked:
                    s = jnp.where(c + j * TK <= r + i * T, s, NEG)
                mn = jnp.maximum(m, jnp.max(s, axis=1, keepdims=True))
                p = jnp.exp2((s - mn) * EXP2_SCALE)
                al = jnp.exp2((m - mn) * EXP2_SCALE)
                vj = v_ref[pl.ds(off, TK), hh * LANES:(hh + 1) * LANES]
                out.append((mn, al * acc + _dot(p.astype(BF16), vj, ((1,), (0,)))))
            return tuple(out)

        one = (jnp.full((T, 1), NEG, F32), jnp.zeros((T, LANES), F32))
        nfull = lax.div(i * T, TK)
        carry = lax.fori_loop(0, nfull, lambda j, cr: blk(j, cr, False), (one, one))
        for t in range(nmask):
            carry = blk(nfull + t, carry, True)
        lane = lax.broadcasted_iota(jnp.int32, (1, LANES), 1)
        z = jnp.zeros((T, LANES), F32)
        for hh in range(2):
            m, acc = carry[hh]
            l = acc[:, 64:65]
            o_ref[:, hh * 64:(hh + 1) * 64] = (acc / l)[:, :64]
            z = z + (m * EXP2_SCALE + jnp.log(l) * LOG2E) * (lane == hh).astype(F32)
        l_ref[0] = z.T[0:2, :]

    return pl.pallas_call(
        body, name=name, grid=(HEADS // 2, S // T),
        in_specs=[pl.BlockSpec((T, 256), lambda h, i: (i, h)), pl.BlockSpec((S, 256), lambda h, i: (0, h)),
                  pl.BlockSpec((S, 256), lambda h, i: (0, h))],
        out_specs=[pl.BlockSpec((T, LANES), lambda h, i: (i, h)), pl.BlockSpec((1, 2, T), lambda h, i: (h, 0, i))],
        out_shape=[jax.ShapeDtypeStruct((S, D_SSD), F32), jax.ShapeDtypeStruct((HEADS // 2, 2, S), F32)],
        compiler_params=_cp(("parallel", "parallel")),
    )(qf, kf, va)


def _attn_delta(dmix, ao, name):
    S = ao.shape[0]
    tm = min(512, S)

    def body(d_ref, o_ref, out_ref):
        lane = lax.broadcasted_iota(jnp.int32, (1, LANES), 1)
        lo = (lane < 64).astype(F32)
        for hp in range(HEADS // 2):
            y = d_ref[:, hp * LANES:(hp + 1) * LANES] * o_ref[:, hp * LANES:(hp + 1) * LANES]
            z = (jnp.sum(y * lo, axis=1, keepdims=True) * (lane == 0).astype(F32)
                 + jnp.sum(y * (1.0 - lo), axis=1, keepdims=True) * (lane == 1).astype(F32))
            out_ref[hp] = z.T[0:2, :]

    return pl.pallas_call(
        body, name=name, grid=(S // tm,),
        in_specs=[pl.BlockSpec((tm, D_SSD), lambda i: (i, 0)), pl.BlockSpec((tm, D_SSD), lambda i: (i, 0))],
        out_specs=pl.BlockSpec((HEADS // 2, 2, tm), lambda i: (0, 0, i)),
        out_shape=jax.ShapeDtypeStruct((HEADS // 2, 2, S), F32),
        compiler_params=_cp(("parallel",)),
    )(dmix, ao)


def _attn_bwd(qf, kf, kT, va, do, lse_r, delta_r, name):
    S = qf.shape[0]
    T, TK = min(ATT_BQ, S), min(ATT_BK, S)
    nq = S // T
    nmask = max(1, TK // T)

    def body(q_ref, k_ref, kT_ref, v_ref, do_ref, l_ref, d_ref, dqT_ref, dk_ref, dv_ref):
        j = pl.program_id(1)

        @pl.when(j == 0)
        def _():
            dqT_ref[...] = jnp.zeros_like(dqT_ref)

        r = lax.broadcasted_iota(jnp.int32, (TK, T), 0)
        c = lax.broadcasted_iota(jnp.int32, (TK, T), 1)
        lo = (lax.broadcasted_iota(jnp.int32, (1, LANES), 1) < 64).astype(F32)
        ks = [k_ref[:, hh * LANES:(hh + 1) * LANES] for hh in range(2)]
        vs = [v_ref[:, hh * LANES:(hh + 1) * LANES] for hh in range(2)]
        kTs = [kT_ref[hh * LANES:(hh + 1) * LANES, :] for hh in range(2)]

        def blk(i, carry, masked):
            off = pl.multiple_of(i * T, T)
            dall = do_ref[pl.ds(off, T), :]
            out = []
            for hh in range(2):
                dk, dv = carry[hh]
                q = q_ref[pl.ds(off, T), hh * LANES:(hh + 1) * LANES]
                dop = ((dall if hh == 0 else pltpu.roll(dall, 64, 1)) * lo).astype(BF16)
                lrow = l_ref[0, hh:hh + 1, pl.ds(off, T)]
                drow = d_ref[0, hh:hh + 1, pl.ds(off, T)]
                pT = jnp.exp2(_dot(ks[hh], q, ((1,), (1,))) * EXP2_SCALE - lrow)
                if masked:
                    pT = jnp.where(r + j * TK <= c + i * T, pT, 0.0)
                dpT = _dot(vs[hh], dop, ((1,), (1,)))
                dsT = (pT * (dpT - drow) * ATT_SCALE).astype(BF16)
                dv = dv + _dot(pT.astype(BF16), dop, ((1,), (0,)))
                dk = dk + _dot(dsT, q, ((1,), (0,)))
                dqT_ref[hh * LANES:(hh + 1) * LANES, pl.ds(off, T)] += _dot(kTs[hh], dsT, ((1,), (0,)))
                out.append((dk, dv))
            return tuple(out)

        z = (jnp.zeros((TK, LANES), F32), jnp.zeros((TK, LANES), F32))
        first = lax.div(j * TK, T)
        carry = (z, z)
        for t in range(nmask):
            carry = blk(first + t, carry, True)
        carry = lax.fori_loop(first + nmask, nq, lambda i, cr: blk(i, cr, False), carry)
        for hh in range(2):
            dk_ref[:, hh * LANES:(hh + 1) * LANES] = carry[hh][0]
            dv_ref[:, hh * LANES:(hh + 1) * LANES] = carry[hh][1]

    return pl.pallas_call(
        body, name=name, grid=(HEADS // 2, S // TK),
        in_specs=[pl.BlockSpec((S, 256), lambda h, j: (0, h)), pl.BlockSpec((TK, 256), lambda h, j: (j, h)),
                  pl.BlockSpec((256, TK), lambda h, j: (h, j)), pl.BlockSpec((TK, 256), lambda h, j: (j, h)),
                  pl.BlockSpec((S, LANES), lambda h, j: (0, h)), pl.BlockSpec((1, 2, S), lambda h, j: (h, 0, 0)),
                  pl.BlockSpec((1, 2, S), lambda h, j: (h, 0, 0))],
        out_specs=[pl.BlockSpec((256, S), lambda h, j: (h, 0)), pl.BlockSpec((TK, 256), lambda h, j: (j, h)),
                   pl.BlockSpec((TK, 256), lambda h, j: (j, h))],
        out_shape=[jax.ShapeDtypeStruct((D_MODEL, S), F32), jax.ShapeDtypeStruct((S, D_MODEL), F32),
                   jax.ShapeDtypeStruct((S, D_MODEL), F32)],
        compiler_params=_cp(("parallel", "arbitrary")),
    )(qf, kf, kT, va, do, lse_r, delta_r)


def _shift_down(x, s):
    if s == 0:
        return x
    rows = lax.broadcasted_iota(jnp.int32, x.shape, 0)
    return jnp.where(rows >= s, pltpu.roll(x, s, 0), 0.0)


def _shift_up(x, s):
    if s == 0:
        return x
    n = x.shape[0]
    rows = lax.broadcasted_iota(jnp.int32, x.shape, 0)
    return jnp.where(rows < n - s, pltpu.roll(x, n - s, 0), 0.0)


def _conv_fwd(proj, cvec, name):
    S = proj.shape[0]

    def body(x_ref, c_ref, o_ref):
        x = x_ref[...]
        y = jnp.broadcast_to(c_ref[4:5, :], x.shape)
        for k in range(CONV_TAPS):
            y = y + c_ref[k:k + 1, :] * _shift_down(x, CONV_TAPS - 1 - k)
        o_ref[...] = y * jax.nn.sigmoid(y)

    return pl.pallas_call(
        body, name=name, grid=(D_CONV // LANES,),
        in_specs=[pl.BlockSpec((S, LANES), lambda j: (0, 8 + j)), pl.BlockSpec((8, LANES), lambda j: (0, j))],
        out_specs=pl.BlockSpec((S, LANES), lambda j: (0, j)),
        out_shape=jax.ShapeDtypeStruct((S, D_CONV), F32),
        compiler_params=_cp(("parallel",)),
    )(proj, cvec)


def _conv_bwd(proj, cvec, dact, name):
    S = proj.shape[0]

    def body(x_ref, c_ref, d_ref, dx_ref, dc_ref):
        x = x_ref[...]
        y = jnp.broadcast_to(c_ref[4:5, :], x.shape)
        for k in range(CONV_TAPS):
            y = y + c_ref[k:k + 1, :] * _shift_down(x, CONV_TAPS - 1 - k)
        sg = jax.nn.sigmoid(y)
        dy = d_ref[...] * (sg * (1.0 + y * (1.0 - sg)))
        dx = jnp.zeros_like(x)
        for k in range(CONV_TAPS):
            s = CONV_TAPS - 1 - k
            dx = dx + c_ref[k:k + 1, :] * _shift_up(dy, s)
            dc_ref[k:k + 1, :] = jnp.sum(dy * _shift_down(x, s), axis=0, keepdims=True)
        dx_ref[...] = dx
        dc_ref[4:5, :] = jnp.sum(dy, axis=0, keepdims=True)
        dc_ref[5:8, :] = jnp.zeros((3, LANES), F32)

    return pl.pallas_call(
        body, name=name, grid=(D_CONV // LANES,),
        in_specs=[pl.BlockSpec((S, LANES), lambda j: (0, 8 + j)), pl.BlockSpec((8, LANES), lambda j: (0, j)),
                  pl.BlockSpec((S, LANES), lambda j: (0, j))],
        out_specs=[pl.BlockSpec((S, LANES), lambda j: (0, j)), pl.BlockSpec((8, LANES), lambda j: (0, j))],
        out_shape=[jax.ShapeDtypeStruct((S, D_CONV), F32), jax.ShapeDtypeStruct((8, D_CONV), F32)],
        compiler_params=_cp(("parallel",)),
    )(proj, cvec, dact)


def fn_ssd_chunk(xs, bm, cm, dtr, state, vecs):
    Q = CHUNK
    dt = jax.nn.softplus(dtr + vecs[0:1])
    a = -jnp.exp(vecs[1:2])
    adt = dt * a
    tril = _causal_mask(Q)
    acs = _dot(tril.astype(F32), adt, ((1,), (0,)), lax.Precision.HIGHEST)
    acs_t = acs.T
    alast = acs[Q - 1:Q, :]
    r = lax.broadcasted_iota(jnp.int32, (LANES, D_SSD), 0)
    c = lax.broadcasted_iota(jnp.int32, (LANES, D_SSD), 1)
    spread = (lax.shift_right_logical(c, 6) == r).astype(F32)

    def per_head(v):
        return _dot(v, spread, ((1,), (0,)), lax.Precision.HIGH)

    xdt = xs * per_head(dt)
    ub = (xdt * per_head(jnp.exp(alast - acs))).astype(BF16)
    xdtb = xdt.astype(BF16)
    Bs = [bm[:, g * SSD_N:(g + 1) * SSD_N].astype(BF16) for g in range(2)]
    Cs = [cm[:, g * SSD_N:(g + 1) * SSD_N].astype(BF16) for g in range(2)]
    Gs = [_dot(Cs[g], Bs[g], ((1,), (1,))) for g in range(2)]
    yds, yos, adds = [], [], []
    for h in range(SSD_HEADS):
        g = h // (SSD_HEADS // 2)
        sl = slice(h * SSD_P, (h + 1) * SSD_P)
        L = jnp.exp(jnp.where(tril, acs[:, h:h + 1] - acs_t[h:h + 1, :], -jnp.inf))
        yds.append(_dot((Gs[g] * L).astype(BF16), xdtb[:, sl], ((1,), (0,))))
        yos.append(_dot(Cs[g], state[h].astype(BF16), ((1,), (1,))))
        adds.append(_dot(ub[:, sl], Bs[g], ((0,), (0,))))
    y = jnp.concatenate(yds, axis=1) + jnp.concatenate(yos, axis=1) * per_head(jnp.exp(acs)) + per_head(vecs[2:3]) * xs
    decay = jnp.stack([jnp.broadcast_to(jnp.exp(alast[:, h:h + 1]), (SSD_P, SSD_N)) for h in range(SSD_HEADS)])
    return y, jnp.stack(adds) + state * decay


def _ssd_fwd(xact, proj, svec, name):
    S = xact.shape[0]
    nc = S // CHUNK

    def body(x_ref, dt_ref, v_ref, y_ref, st_ref, state):
        @pl.when(pl.program_id(0) == 0)
        def _():
            state[...] = jnp.zeros_like(state)

        st_ref[0] = state[...]
        x = x_ref[...]
        y, sn = fn_ssd_chunk(x[:, 0:512], x[:, 512:768], x[:, 768:1024], dt_ref[...], state[...], v_ref[...])
        y_ref[...] = y
        state[...] = sn

    return pl.pallas_call(
        body, name=name, grid=(nc,),
        in_specs=[pl.BlockSpec((CHUNK, D_CONV), lambda i: (i, 0)), pl.BlockSpec((CHUNK, LANES), lambda i: (i, 16)),
                  pl.BlockSpec((8, LANES), lambda i: (0, 0))],
        out_specs=[pl.BlockSpec((CHUNK, D_SSD), lambda i: (i, 0)),
                   pl.BlockSpec((1, SSD_HEADS, SSD_P, SSD_N), lambda i: (i, 0, 0, 0))],
        out_shape=[jax.ShapeDtypeStruct((S, D_SSD), F32), jax.ShapeDtypeStruct((nc, SSD_HEADS, SSD_P, SSD_N), F32)],
        scratch_shapes=[pltpu.VMEM((SSD_HEADS, SSD_P, SSD_N), F32)],
        compiler_params=_cp(("arbitrary",)),
    )(xact, proj, svec)


def _ssd_bwd(xact, proj, svec, states, dy, name):
    S = xact.shape[0]
    nc = S // CHUNK

    def body(x_ref, dt_ref, v_ref, st_ref, dy_ref, dx_ref, ddt_ref, dv_ref, dstate):
        @pl.when(pl.program_id(0) == 0)
        def _():
            dstate[...] = jnp.zeros_like(dstate)
            dv_ref[...] = jnp.zeros_like(dv_ref)

        x = x_ref[...]
        _, vjp = jax.vjp(fn_ssd_chunk, x[:, 0:512], x[:, 512:768], x[:, 768:1024], dt_ref[...], st_ref[0], v_ref[...])
        dxs, dbm, dcm, ddt, dst, dvec = vjp((dy_ref[...], dstate[...]))
        dx_ref[:, 0:512] = dxs
        dx_ref[:, 512:768] = dbm
        dx_ref[:, 768:1024] = dcm
        ddt_ref[...] = ddt
        dstate[...] = dst
        dv_ref[...] += dvec

    rev = lambda i: (nc - 1 - i, 0)
    return pl.pallas_call(
        body, name=name, grid=(nc,),
        in_specs=[pl.BlockSpec((CHUNK, D_CONV), rev), pl.BlockSpec((CHUNK, LANES), lambda i: (nc - 1 - i, 16)),
                  pl.BlockSpec((8, LANES), lambda i: (0, 0)),
                  pl.BlockSpec((1, SSD_HEADS, SSD_P, SSD_N), lambda i: (nc - 1 - i, 0, 0, 0)),
                  pl.BlockSpec((CHUNK, D_SSD), rev)],
        out_specs=[pl.BlockSpec((CHUNK, D_CONV), rev), pl.BlockSpec((CHUNK, LANES), rev),
                   pl.BlockSpec((8, LANES), lambda i: (0, 0))],
        out_shape=[jax.ShapeDtypeStruct((S, D_CONV), F32), jax.ShapeDtypeStruct((S, LANES), F32),
                   jax.ShapeDtypeStruct((8, LANES), F32)],
        scratch_shapes=[pltpu.VMEM((SSD_HEADS, SSD_P, SSD_N), F32)],
        compiler_params=_cp(("arbitrary",)),
    )(xact, proj, svec, states, dy)


def _ada_fwd(c_all, w_ada, b_sh, name):
    nb = 1536 // 512

    def body(c_ref, w_ref, b_ref, o_ref):
        ca = jax.nn.silu(c_ref[...]).astype(BF16)
        o_ref[0] = _dot(ca, w_ref[0].astype(BF16), ((1,), (0,))) + b_ref[0]

    return pl.pallas_call(
        body, name=name, grid=(DEPTH, nb),
        in_specs=[pl.BlockSpec((8, D_MODEL), lambda l, j: (0, 0)), pl.BlockSpec((1, D_MODEL, 512), lambda l, j: (l, 0, j)),
                  pl.BlockSpec((1, 1, 512), lambda l, j: (l, 0, j))],
        out_specs=pl.BlockSpec((1, 8, 512), lambda l, j: (l, 0, j)),
        out_shape=jax.ShapeDtypeStruct((DEPTH, 8, 1536), F32),
        compiler_params=_cp(("parallel", "parallel")),
    )(c_all, w_ada, b_sh)


def _ada_bwd_adam(c_all_t, dmod_sh, w, m, v, name):
    nb = 1536 // 512

    def body(c_ref, d_ref, w_ref, m_ref, v_ref, g_ref, dl_ref, nm_ref, nv_ref):
        ca = jax.nn.silu(c_ref[...])
        g = ca[:, 0:1] * d_ref[0, 0:1, :]
        for b in range(1, 8):
            g = g + ca[:, b:b + 1] * d_ref[0, b:b + 1, :]
        g_ref[0] = g
        dl_ref[0], nm_ref[0], nv_ref[0] = _adam_update(w_ref[0], m_ref[0], v_ref[0], g)

    blk = pl.BlockSpec((1, D_MODEL, 512), lambda l, j: (l, 0, j))
    return pl.pallas_call(
        body, name=name, grid=(DEPTH, nb),
        in_specs=[pl.BlockSpec((D_MODEL, 8), lambda l, j: (0, 0)), pl.BlockSpec((1, 8, 512), lambda l, j: (l, 0, j)), blk, blk, blk],
        out_specs=[blk] * 4, out_shape=[jax.ShapeDtypeStruct((DEPTH, D_MODEL, 1536), F32)] * 4,
        compiler_params=_cp(("parallel", "parallel")),
    )(c_all_t, dmod_sh, w, m, v)


def _rows_tile(rows):
    return next(t for t in (512, 256, 128, 64, 32, 16, 8) if rows % t == 0)


SUM_BLOCKS = 4
ADAM_BLOCKS = 8


def _sum_sibling(gs, ls, ci, name):
    n = len(gs)

    def body(c_ref, *refs):
        for p in range(n):
            refs[2 * n + p][...] = refs[2 * p][...] + refs[2 * p + 1][...]

    in_specs, out_specs, out_shape = [], [], []
    for g in gs:
        _, _, rh, cw = g.shape
        rb = rh // SUM_BLOCKS
        in_specs += [pl.BlockSpec((None, None, rb, cw), lambda s, i, c: (s, c[0], i, 0)),
                     pl.BlockSpec((None, rb, cw), lambda s, i, c: (s, i, 0))]
        out_specs.append(pl.BlockSpec((None, rb, cw), lambda s, i, c: (s, i, 0)))
        out_shape.append(jax.ShapeDtypeStruct((4, rh, cw), F32))
    ops = [a for pair in zip(gs, ls) for a in pair]
    return pl.pallas_call(
        body, name=name,
        grid_spec=pltpu.PrefetchScalarGridSpec(num_scalar_prefetch=1, grid=(4, SUM_BLOCKS), in_specs=in_specs, out_specs=out_specs),
        out_shape=out_shape, compiler_params=_cp(("parallel", "parallel")),
    )(ci.reshape(1).astype(jnp.int32), *ops)


def _sum_chips(cs, lands, chip, ci, name):
    n = len(cs)

    def body(c_ref, *refs):
        for p in range(n):
            a = refs[4 * p:4 * p + 4]
            refs[4 * n + p][...] = ((a[0][...] + a[1][...]) + a[2][...]) + a[3][...]

    in_specs, out_specs, out_shape = [], [], []
    for c in cs:
        _, rh, cw = c.shape
        rb = rh // SUM_BLOCKS
        in_specs.append(pl.BlockSpec((None, rb, cw), lambda i, ch: (ch[0], i, 0)))
        in_specs += [pl.BlockSpec((None, rb, cw), functools.partial(lambda i, ch, k: (k, i, 0), k=k)) for k in range(3)]
        out_specs.append(pl.BlockSpec((None, rb, cw), lambda i, ch: (ch[1], i, 0)))
        out_shape.append(jax.ShapeDtypeStruct((2, rh, cw), F32))
    ops = [a for c, l in zip(cs, lands) for a in (c, l, l, l)]
    return pl.pallas_call(
        body, name=name,
        grid_spec=pltpu.PrefetchScalarGridSpec(num_scalar_prefetch=1, grid=(SUM_BLOCKS,), in_specs=in_specs, out_specs=out_specs),
        out_shape=out_shape, compiler_params=_cp(("parallel",)),
    )(jnp.stack([chip, ci]).astype(jnp.int32), *ops)


def _adam_update(w, m, v, g):
    c1 = 1.0 / (1.0 - ADAM_B1 ** ADAM_STEP)
    c2 = 1.0 / (1.0 - ADAM_B2 ** ADAM_STEP)
    nm = ADAM_B1 * m + (1.0 - ADAM_B1) * g
    nv = ADAM_B2 * v + (1.0 - ADAM_B2) * (g * g)
    return -ADAM_LR * ((nm * c1) / (jnp.sqrt(nv * c2) + ADAM_EPS) + ADAM_WD * w), nm, nv


def _adam_multi(ws, ms, vs, gs, name):
    n = len(ws)
    per = 3 + DEPTH

    def body(*refs):
        layer = pl.program_id(0)
        for p in range(n):
            w, m, v = [refs[per * p + k][...] for k in range(3)]
            g = refs[per * p + 3][...]
            for l in range(1, DEPTH):
                g = jnp.where(layer == l, refs[per * p + 3 + l][...], g)
            d, nm, nv = _adam_update(w, m, v, g)
            for k, val in enumerate((g, d, nm, nv)):
                refs[per * n + 4 * p + k][...] = val

    in_specs, out_specs, out_shape = [], [], []
    for w in ws:
        _, r, cw = w.shape
        if r % (8 * ADAM_BLOCKS) == 0:
            spec = pl.BlockSpec((None, r // ADAM_BLOCKS, cw), lambda l, i: (l, i, 0))
            gspec = pl.BlockSpec((r // ADAM_BLOCKS, cw), lambda l, i: (i, 0))
        else:
            spec = pl.BlockSpec((None, r, cw // ADAM_BLOCKS), lambda l, i: (l, 0, i))
            gspec = pl.BlockSpec((r, cw // ADAM_BLOCKS), lambda l, i: (0, i))
        in_specs += [spec] * 3 + [gspec] * DEPTH
        out_specs += [spec] * 4
        out_shape += [jax.ShapeDtypeStruct(w.shape, F32)] * 4
    ops = [a for w, m, v, g in zip(ws, ms, vs, gs) for a in (w, m, v, *g)]
    res = pl.pallas_call(
        body, name=name, grid=(DEPTH, ADAM_BLOCKS), in_specs=in_specs, out_specs=out_specs, out_shape=out_shape,
        compiler_params=_cp(("parallel", "parallel")),
    )(*ops)
    return res[0::4], res[1::4], res[2::4], res[3::4]


def _adam(w, m, v, parts, name):
    rows, width = w.shape
    bm = min(256, _rows_tile(rows))
    np_ = len(parts)
    c1 = 1.0 / (1.0 - ADAM_B1 ** ADAM_STEP)
    c2 = 1.0 / (1.0 - ADAM_B2 ** ADAM_STEP)

    def body(*refs):
        w_ref, m_ref, v_ref = refs[:3]
        g = refs[3][...]
        for r in refs[4:3 + np_]:
            g = g + r[...]
        g_ref, d_ref, nm_ref, nv_ref = refs[3 + np_:]
        nm = ADAM_B1 * m_ref[...] + (1.0 - ADAM_B1) * g
        nv = ADAM_B2 * v_ref[...] + (1.0 - ADAM_B2) * (g * g)
        g_ref[...] = g
        nm_ref[...] = nm
        nv_ref[...] = nv
        d_ref[...] = -ADAM_LR * ((nm * c1) / (jnp.sqrt(nv * c2) + ADAM_EPS) + ADAM_WD * w_ref[...])

    blk = pl.BlockSpec((bm, width), lambda i: (i, 0))
    return pl.pallas_call(
        body, name=name, grid=(rows // bm,),
        in_specs=[blk, blk, blk] + [pl.BlockSpec((bm, width), functools.partial(lambda i, o: (i + o, 0), o=off // bm))
                                    for (_, off) in parts],
        out_specs=[blk, blk, blk, blk],
        out_shape=[jax.ShapeDtypeStruct((rows, width), F32)] * 4,
        compiler_params=_cp(("parallel",)),
    )(w, m, v, *[p[0] for p in parts])


def _coords():
    return lax.axis_index("x"), lax.axis_index("y"), lax.axis_index("c")


def _other_chips(x, y):
    return [(1 - x, y), (x, 1 - y), (1 - x, 1 - y)]


def _ag8(blk, name):
    m_per, n = blk.shape

    def body(x_ref, out_ref, send_sems, recv_sems, local_sem):
        x, y, c = _coords()
        me, sibling = (x, y, c), (x, y, 1 - c)
        chips = _other_chips(x, y)

        def rows(px, py, pc):
            return out_ref.at[pl.ds((4 * px + 2 * py + pc) * m_per, m_per), :]

        def copy(k, block, to, src=None):
            return pltpu.make_async_remote_copy(
                src_ref=rows(*block) if src is None else src, dst_ref=rows(*block),
                send_sem=send_sems.at[k], recv_sem=recv_sems.at[k], device_id=to, device_id_type=MESH)

        mine = pltpu.make_async_copy(x_ref, rows(*me), local_sem)
        mine.start()
        first = [copy(0, me, sibling, src=x_ref)]
        first += [copy(1 + j, me, (*chip, c), src=x_ref) for j, chip in enumerate(chips)]
        for cp in first:
            cp.start()
        passed = [copy(4 + j, (*chip, c), sibling) for j, chip in enumerate(chips)]
        for j, chip in enumerate(chips):
            copy(1 + j, (*chip, c), me).wait_recv()
            passed[j].start()
        copy(0, sibling, me).wait_recv()
        for j, chip in enumerate(chips):
            copy(4 + j, (*chip, 1 - c), me).wait_recv()
        for cp in first + passed:
            cp.wait_send()
        mine.wait()

    return pl.pallas_call(
        body, name=name,
        out_shape=jax.ShapeDtypeStruct((8 * m_per, n), blk.dtype),
        in_specs=[pl.BlockSpec(memory_space=pltpu.VMEM)], out_specs=pl.BlockSpec(memory_space=pltpu.VMEM),
        scratch_shapes=[pltpu.SemaphoreType.DMA((7,)), pltpu.SemaphoreType.DMA((7,)), pltpu.SemaphoreType.DMA],
    )(blk)


HBM_SPEC = pl.BlockSpec(memory_space=pltpu.HBM)
SEM_SPEC = pl.BlockSpec(memory_space=pltpu.SEMAPHORE)
EFFECT = pltpu.SideEffectType.DATAFLOW_SIDE_EFFECTING


def _remote(src, dst, send_sem, recv_sem, to):
    return pltpu.make_async_remote_copy(src_ref=src, dst_ref=dst, send_sem=send_sem, recv_sem=recv_sem,
                                        device_id=to, device_id_type=MESH)


def _ag_list(shards, name):
    n = len(shards)

    def body(*refs):
        sh, out = refs[:n], refs[n:2 * n]
        send_sems, recv_sems = refs[2 * n:]
        x, y, c = _coords()
        sibling = (x, y, 1 - c)
        chips = _other_chips(x, y)
        first = [_remote(sh[p].at[c], out[p].at[2 * x + y, c], send_sems.at[6 * p + j], recv_sems.at[6 * p + j], (px, py, c))
                 for p in range(n) for j, (px, py) in enumerate(chips)]
        for cp in first:
            cp.start()
        passed = []
        for j, (px, py) in enumerate(chips):
            for p in range(n):
                got = out[p].at[2 * px + py, c]
                _remote(got, got, send_sems.at[6 * p + j], recv_sems.at[6 * p + j], (x, y, c)).wait_recv()
                cp = _remote(got, got, send_sems.at[6 * p + 3 + j], recv_sems.at[6 * p + 3 + j], sibling)
                cp.start()
                passed.append(cp)
        for j, (px, py) in enumerate(chips):
            for p in range(n):
                got = out[p].at[2 * px + py, 1 - c]
                _remote(got, got, send_sems.at[6 * p + 3 + j], recv_sems.at[6 * p + 3 + j], (x, y, c)).wait_recv()
        for cp in first + passed:
            cp.wait_send()

    return pl.pallas_call(
        body, name=name,
        out_shape=[jax.ShapeDtypeStruct((4,) + s.shape, s.dtype) for s in shards],
        in_specs=[pl.BlockSpec(memory_space=pl.ANY)] * n, out_specs=[pl.BlockSpec(memory_space=pl.ANY)] * n,
        scratch_shapes=[pltpu.SemaphoreType.DMA((6 * n,)), pltpu.SemaphoreType.DMA((6 * n,))],
    )(*shards)


def _ag_direct_copies(sh, land, send_sems, recv_sems, starting):
    x, y, c = _coords()
    return [_remote(sh[p], land[p].at[2 * x + y] if starting else land[p].at[2 * px + py],
                    send_sems.at[3 * p + j], recv_sems.at[3 * p + j], (px, py, c))
            for p in range(len(sh)) for j, (px, py) in enumerate(_other_chips(x, y))]


def _rs_sibling_copies(g, land, send_sems, recv_sems, starting):
    x, y, c = _coords()
    return [_remote(g[p].at[s, 1 - c], land[p].at[s], send_sems.at[4 * p + s], recv_sems.at[4 * p + s], (x, y, 1 - c))
            for p in range(len(g)) for s in range(4)]


def _rs_chips_copies(cs, land, send_sems, recv_sems, starting):
    x, y, c = _coords()
    return [_remote(cs[p].at[2 * px + py], land[p].at[j], send_sems.at[3 * p + j], recv_sems.at[3 * p + j], (px, py, c))
            for p in range(len(cs)) for j, (px, py) in enumerate(_other_chips(x, y))]


def _split_start(copies, srcs, land_shapes, per, name):
    n = len(srcs)

    def body(*refs):
        for cp in copies(refs[:n], refs[n:2 * n], refs[2 * n], refs[2 * n + 1], True):
            cp.start()
        token = refs[4 * n + 2]
        token[...] = jnp.zeros_like(token)

    lands = [pltpu.with_memory_space_constraint(lax.empty(shp, s.dtype), pltpu.HBM) for shp, s in zip(land_shapes, srcs)]
    res = pl.pallas_call(
        body, name=name,
        out_shape=(pltpu.SemaphoreType.DMA((per * n,)), pltpu.SemaphoreType.DMA((per * n,)))
        + tuple(pltpu.HBM(s.shape, s.dtype) for s in srcs) + tuple(pltpu.HBM(l.shape, l.dtype) for l in lands)
        + (jax.ShapeDtypeStruct((8, LANES), F32),),
        in_specs=(HBM_SPEC,) * (2 * n), out_specs=(SEM_SPEC, SEM_SPEC) + (HBM_SPEC,) * (2 * n) + (pl.BlockSpec(memory_space=pltpu.VMEM),),
        input_output_aliases={i: 2 + i for i in range(2 * n)},
        compiler_params=pltpu.CompilerParams(has_side_effects=EFFECT),
    )(*[pltpu.with_memory_space_constraint(s, pltpu.HBM) for s in srcs], *lands)
    return res[0], res[1], res[2:2 + n], res[2 + n:2 + 2 * n], res[2 + 2 * n]


def _split_wait(copies, send_sems, recv_sems, src_thru, land_thru, after, name):
    n = len(src_thru)

    def body(*refs):
        for cp in copies(refs[:n], refs[n:2 * n], refs[2 * n], refs[2 * n + 1], False):
            cp.wait_send()
            cp.wait_recv()

    res = pl.pallas_call(
        body, name=name,
        out_shape=tuple(pltpu.HBM(s.shape, s.dtype) for s in src_thru) + tuple(pltpu.HBM(l.shape, l.dtype) for l in land_thru),
        in_specs=(HBM_SPEC,) * (2 * n) + (SEM_SPEC, SEM_SPEC, pl.BlockSpec(memory_space=pl.ANY)),
        out_specs=(HBM_SPEC,) * (2 * n), input_output_aliases={i: i for i in range(2 * n)},
        compiler_params=pltpu.CompilerParams(has_side_effects=EFFECT),
    )(*src_thru, *land_thru, send_sems, recv_sems, after)
    return res[:n], res[n:]


def _ag_direct_start(shards, name):
    return _split_start(_ag_direct_copies, shards, [(4,) + s.shape for s in shards], 3, name)


def _ag_direct_wait(send_sems, recv_sems, sh_thru, land_thru, after, name):
    return _split_wait(_ag_direct_copies, send_sems, recv_sems, sh_thru, land_thru, after, name)[1]


def _rs_sibling_start(gs, name):
    return _split_start(_rs_sibling_copies, gs, [(4,) + g.shape[2:] for g in gs], 4, name)


def _rs_sibling_wait(send_sems, recv_sems, g_thru, land_thru, after, name):
    return _split_wait(_rs_sibling_copies, send_sems, recv_sems, g_thru, land_thru, after, name)


def _rs_sibling_list(gs, name):
    n = len(gs)

    def body(*refs):
        cps = _rs_sibling_copies(refs[:n], refs[n:2 * n], refs[2 * n], refs[2 * n + 1], True)
        for cp in cps:
            cp.start()
        for cp in cps:
            cp.wait_recv()
        for cp in cps:
            cp.wait_send()

    return pl.pallas_call(
        body, name=name,
        out_shape=[jax.ShapeDtypeStruct((4,) + g.shape[2:], g.dtype) for g in gs],
        in_specs=[pl.BlockSpec(memory_space=pl.ANY)] * n, out_specs=[pl.BlockSpec(memory_space=pl.ANY)] * n,
        scratch_shapes=[pltpu.SemaphoreType.DMA((4 * n,)), pltpu.SemaphoreType.DMA((4 * n,))],
    )(*gs)


def _rs_chips_start(cs, name):
    return _split_start(_rs_chips_copies, cs, [(3,) + c.shape[1:] for c in cs], 3, name)


def _rs_chips_wait(send_sems, recv_sems, cs_thru, land_thru, after, name):
    return _split_wait(_rs_chips_copies, send_sems, recv_sems, cs_thru, land_thru, after, name)


def _swap_list(ghs, name):
    n = len(ghs)

    def body(*refs):
        g, out, send_sems, recv_sems = refs[:n], refs[n:2 * n], refs[2 * n], refs[2 * n + 1]
        x, y, c = _coords()
        cps = [_remote(g[p].at[c], out[p].at[c], send_sems.at[p], recv_sems.at[p], (x, y, 1 - c)) for p in range(n)]
        for cp in cps:
            cp.start()
        for p in range(n):
            _remote(g[p].at[c], out[p].at[1 - c], send_sems.at[p], recv_sems.at[p], (x, y, 1 - c)).wait_recv()
        for cp in cps:
            cp.wait_send()

    return pl.pallas_call(
        body, name=name,
        out_shape=[jax.ShapeDtypeStruct(g.shape, g.dtype) for g in ghs],
        in_specs=[pl.BlockSpec(memory_space=pl.ANY)] * n, out_specs=[pl.BlockSpec(memory_space=pl.ANY)] * n,
        input_output_aliases={p: p for p in range(n)},
        scratch_shapes=[pltpu.SemaphoreType.DMA((n,)), pltpu.SemaphoreType.DMA((n,))],
    )(*ghs)


def _pad_win(w):
    return jnp.concatenate([w[:, :416], jnp.zeros((w.shape[0], 96), w.dtype), w[:, 416:1952],
                            w[:, 1952:1960], jnp.zeros((w.shape[0], 120), w.dtype)], axis=1)


def _unpad_win(g):
    return jnp.concatenate([g[:, :416], g[:, 512:2048], g[:, 2048:2056]], axis=1)


def _pad_wq(w):
    return jnp.pad(w.reshape(Q_LORA, HEADS, QK_DIM), ((0, 0), (0, 0), (0, LANES - QK_DIM))).reshape(Q_LORA, HEADS * LANES)


def _unpad_wq(g):
    return g.reshape(Q_LORA, HEADS, LANES)[:, :, :QK_DIM].reshape(Q_LORA, HEADS * QK_DIM)


def _cols_to_shards(a):
    r, c4 = a.shape
    return a.reshape(r, 4, c4 // 4).transpose(1, 0, 2)


def _shards_to_cols(a):
    _, r, c = a.shape
    return a.transpose(1, 0, 2).reshape(r, 4 * c)


def _pack_small(tree):
    parts = []
    for l in range(DEPTH):
        for (n, k) in SMALL:
            parts.append(jnp.pad(tree[n][l].reshape(-1), (0, -k % LANES)))
    flat = jnp.concatenate(parts)
    return jnp.pad(flat, (0, SMALL_ROWS * LANES - flat.shape[0])).reshape(SMALL_ROWS, LANES)


def _unpack_small(buf):
    flat = buf.reshape(-1)
    out = {n: [] for (n, _) in SMALL}
    o = 0
    for l in range(DEPTH):
        for (n, k) in SMALL:
            out[n].append(flat[o:o + k])
            o += k + (-k % LANES)
    return {n: jnp.stack(v) for n, v in out.items()}


def _vec(v, width=LANES):
    return jnp.pad(v.reshape(1, -1), ((0, 0), (0, width - v.shape[-1])))


def kernel(x, c, positions, norm1_w, norm2_w, w_ada, b_ada, w_in, q_a_norm_w, w_q_up, kv_a_norm_w, w_kv_up, q_nope_norm_w, q_pe_norm_w, k_nope_norm_w, k_pe_norm_w, conv_w, conv_b, dt_bias, a_log, d_skip, ssd_norm_w, w_out, w_gate_up, w_down, loss_target, m_norm1_w, m_norm2_w, m_w_ada, m_b_ada, m_w_in, m_q_a_norm_w, m_w_q_up, m_kv_a_norm_w, m_w_kv_up, m_q_nope_norm_w, m_q_pe_norm_w, m_k_nope_norm_w, m_k_pe_norm_w, m_conv_w, m_conv_b, m_dt_bias, m_a_log, m_d_skip, m_ssd_norm_w, m_w_out, m_w_gate_up, m_w_down, v_norm1_w, v_norm2_w, v_w_ada, v_b_ada, v_w_in, v_q_a_norm_w, v_w_q_up, v_kv_a_norm_w, v_w_kv_up, v_q_nope_norm_w, v_q_pe_norm_w, v_k_nope_norm_w, v_k_pe_norm_w, v_conv_w, v_conv_b, v_dt_bias, v_a_log, v_d_skip, v_ssd_norm_w, v_w_out, v_w_gate_up, v_w_down):
    W = dict(zip(WEIGHTS, (norm1_w, norm2_w, w_ada, b_ada, w_in, q_a_norm_w, w_q_up, kv_a_norm_w, w_kv_up, q_nope_norm_w, q_pe_norm_w, k_nope_norm_w, k_pe_norm_w, conv_w, conv_b, dt_bias, a_log, d_skip, ssd_norm_w, w_out, w_gate_up, w_down)))
    M = dict(zip(WEIGHTS, (m_norm1_w, m_norm2_w, m_w_ada, m_b_ada, m_w_in, m_q_a_norm_w, m_w_q_up, m_kv_a_norm_w, m_w_kv_up, m_q_nope_norm_w, m_q_pe_norm_w, m_k_nope_norm_w, m_k_pe_norm_w, m_conv_w, m_conv_b, m_dt_bias, m_a_log, m_d_skip, m_ssd_norm_w, m_w_out, m_w_gate_up, m_w_down)))
    V = dict(zip(WEIGHTS, (v_norm1_w, v_norm2_w, v_w_ada, v_b_ada, v_w_in, v_q_a_norm_w, v_w_q_up, v_kv_a_norm_w, v_w_kv_up, v_q_nope_norm_w, v_q_pe_norm_w, v_k_nope_norm_w, v_k_pe_norm_w, v_conv_w, v_conv_b, v_dt_bias, v_a_log, v_d_skip, v_ssd_norm_w, v_w_out, v_w_gate_up, v_w_down)))
    S = x.shape[1]
    xi, yi, ci = _coords()
    chip = 2 * xi + yi
    dev = 2 * chip + ci
    x0 = x[0]
    tgt = loss_target[0]

    inv_freq = 1.0 / (ROPE_THETA ** (jnp.arange(0, ROPE, 2, dtype=F32) / ROPE))
    ang = positions[0].astype(F32)[:, None] * inv_freq
    cos, sin = jnp.cos(ang), jnp.sin(ang)
    z16, z32, z64 = jnp.zeros((S, 16), F32), jnp.zeros((S, 32), F32), jnp.zeros((S, 64), F32)
    tab_c = jnp.concatenate([jnp.ones((S, 64), F32), cos, cos, z32], axis=1)
    tab_s1 = jnp.concatenate([z64, z16, sin, z32], axis=1)
    tab_s2 = jnp.concatenate([z64, -sin, z16, z32], axis=1)

    blk0 = jnp.concatenate([c.reshape(-1), W['conv_w'].reshape(-1)]).reshape(24, LANES)
    g0 = _ag8(blk0, "ag_c_conv").reshape(8, 24 * LANES)
    c_all = g0[:, :D_MODEL]
    conv_full = g0[0::2, D_MODEL:].reshape(4, DEPTH, CONV_TAPS, 256).transpose(1, 2, 0, 3).reshape(DEPTH, CONV_TAPS, D_CONV)

    sh = [{n: W[n][l].astype(BF16) for n in BIG} for l in range(DEPTH)]
    got_first = _ag_list([sh[0][n].reshape(2, sh[0][n].shape[0] // 2, sh[0][n].shape[1]) for n in FIRST], "ag_w0_first")
    to_operand = dict(w_in=lambda a: _pad_win(_shards_to_cols(a)), w_q_up=lambda a: _pad_wq(_shards_to_cols(a)),
                      w_kv_up=_shards_to_cols, w_out=lambda a: a.reshape(D_MODEL, D_MODEL), w_gate_up=lambda a: a,
                      w_down=lambda a: a.reshape(D_FF, D_MODEL))

    def layer_weights(names, gathered, own):
        return {n: to_operand[n](lax.dynamic_update_slice_in_dim(a.reshape(4, -1, a.shape[-1]), own[n][None], chip, axis=0))
                for n, a in zip(names, gathered)}

    LW = [layer_weights(FIRST, got_first, sh[0]), None]

    b_sh = lax.dynamic_slice_in_dim(W['b_ada'], chip * 1536, 1536, axis=1).reshape(DEPTH, 1, 1536)
    mod_sh = _ada_fwd(c_all, W['w_ada'], b_sh, "ada_fwd")
    g1 = _ag8(mod_sh.reshape(192, LANES), "ag_mod").reshape(8, DEPTH, 8, 1536)
    mod_all = g1[0::2].transpose(1, 2, 0, 3).reshape(DEPTH, 8, 6 * D_MODEL)
    mod = lax.dynamic_index_in_dim(mod_all, dev, axis=1, keepdims=False)
    mod, rest0 = lax.optimization_barrier((mod, [sh[0][n] for n in REST]))
    ag0 = _ag_direct_start(rest0, "ag_w0_rest_start")

    def mvec(l, k):
        return mod[l, k * D_MODEL:(k + 1) * D_MODEL].reshape(1, D_MODEL)

    def small(name, l, width=None):
        v = W[name][l]
        return _vec(v, width or v.shape[-1])

    def wq_vec(l):
        return _vec(jnp.concatenate([W['q_nope_norm_w'][l], W['q_pe_norm_w'][l]]))

    def wk_vec(l):
        return _vec(jnp.concatenate([W['k_nope_norm_w'][l], W['k_pe_norm_w'][l]]))

    def conv_vec(l):
        return jnp.concatenate([conv_full[l], W['conv_b'][l].reshape(1, D_CONV), jnp.zeros((3, D_CONV), F32)], axis=0)

    def ssd_vec(l):
        return jnp.concatenate([_vec(W['dt_bias'][l]), _vec(W['a_log'][l]), _vec(W['d_skip'][l]), jnp.zeros((5, LANES), F32)], axis=0)

    sv = []
    xcur = x0
    h1 = _row_fwd(fn_norm_mod, "norm_mod_f", [(x0, 0, D_MODEL)], [small('norm1_w', 0) + ag0[4][0, 0], mvec(0, 1), mvec(0, 0)],
                  [(D_MODEL, BF16)])[0]
    fin = None
    ag_first = None
    ag_rest = ag0
    for l in range(DEPTH):
        if l == 1:
            LW[1] = layer_weights(FIRST, _ag_direct_wait(*ag_first[:4], xcur, "ag_w1_first_wait"), sh[1])
        lw = LW[l]
        t = dict(xcur=xcur, h1=h1)
        t['proj'] = proj = _mm(h1, lw['w_in'], 'nn', f"mm_in_{l}")
        t['qa_n'], t['kva_n'] = _row_fwd(fn_lat_norm, f"lat_norm_f{l}", [(proj, 0, 256), (proj, 2, 128)],
                                         [small('q_a_norm_w', l), small('kv_a_norm_w', l)], [(256, BF16), (128, BF16)])
        t['q'] = _mm(t['qa_n'], lw['w_q_up'], 'nn', f"mm_q_{l}")
        t['kv'] = _mm(t['kva_n'], lw['w_kv_up'], 'nn', f"mm_kv_{l}")
        t['qf'], t['kf'], t['vv'], t['kT'] = _row_fwd(
            fn_qk_prep_kt, f"qk_prep_f{l}",
            [(t['q'], 0, 1024), (t['kv'], 0, 1024), (proj, 3, 128), (tab_c, 0, 128), (tab_s1, 0, 128), (tab_s2, 0, 128)],
            [wq_vec(l), wk_vec(l)], [(1024, BF16), (1024, BF16), (1024, BF16), (1024, BF16)], transposed=(3,))
        t['ao'], t['lse'] = _attn_fwd(t['qf'], t['kf'], t['vv'], f"attn_f{l}")
        t['xact'] = _conv_fwd(proj, conv_vec(l), f"conv_f{l}")
        t['y'], t['states'] = _ssd_fwd(t['xact'], proj, ssd_vec(l), f"ssd_f{l}")
        tie = 0.0
        t['ao'], t['y'] = lax.optimization_barrier((t['ao'], t['y']))
        rest = list(_ag_direct_wait(*ag_rest[:4], t['y'], f"ag_w{l}_rest_wait"))
        if l == 0:
            rest, sh1f, sh1r = lax.optimization_barrier((rest, [sh[1][n] for n in FIRST], [sh[1][n] for n in REST]))
            ag_first = _ag_direct_start(sh1f, "ag_w1_first_start")
            ag_rest = _ag_direct_start(sh1r, "ag_w1_rest_start")
            tie = ag_first[4][0, 0] + ag_rest[4][0, 0]
        lw.update(layer_weights(REST, rest, sh[l]))
        t['mix'] = _row_fwd(fn_gated_mix, f"gated_f{l}", [(t['y'], 0, 512), (proj, 1, 512), (t['ao'], 0, 512)],
                            [small('ssd_norm_w', l) + tie], [(1024, BF16)])[0]
        t['mo'] = _mm(t['mix'], lw['w_out'], 'nn', f"mm_out_{l}")
        t['x1'], t['h2'] = _row_fwd(fn_resid_norm, f"resid_mid_f{l}", [(xcur, 0, D_MODEL), (t['mo'], 0, D_MODEL)],
                                    [mvec(l, 2), small('norm2_w', l), mvec(l, 4), mvec(l, 3)],
                                    [(D_MODEL, F32), (D_MODEL, BF16)])
        t['gu'], t['act'] = _mm_gu_swiglu(t['h2'], lw['w_gate_up'], f"mm_gu_{l}")
        t['ff'] = _mm(t['act'], lw['w_down'], 'nn', f"mm_down_{l}")
        if l + 1 < DEPTH:
            xcur, h1 = _row_fwd(fn_resid_norm, f"resid_end_f{l}", [(t['x1'], 0, D_MODEL), (t['ff'], 0, D_MODEL)],
                                [mvec(l, 5), small('norm1_w', l + 1), mvec(l + 1, 1), mvec(l + 1, 0)],
                                [(D_MODEL, F32), (D_MODEL, BF16)])
        else:
            fin = _final(t['x1'], t['ff'], tgt, mvec(l, 5), "final_loss")
        sv.append(t)

    dx1, dff, dg2_last, loss_acc = fin
    gfull = {n: [None] * DEPTH for n in BIG}
    gsm = {n: [None] * DEPTH for (n, _) in SMALL}
    dmod = [[None] * 6 for _ in range(DEPTH)]
    dmod[DEPTH - 1][5] = dg2_last
    grad_x = None
    pending = []

    def halves_of(l, names):
        return [gfull[n][l].reshape(4, 2, gfull[n][l].shape[1] // 2, gfull[n][l].shape[2]) for n in names]

    def rs_finish(l, names, tag, g4, sib):
        h = _rs_chips_start(_sum_sibling(g4, sib, ci, f"sum_sibling_{tag}"), f"rs_chips_start_{tag}")
        pending.append((l, names, h))
        return h[4][0, 0]

    tie_l1 = tie_l0a = tie_sib = 0.0
    sib_l1 = sib_l0a = None

    for l in reversed(range(DEPTH)):
        t = sv[l]
        lw = LW[l]
        proj = t['proj']
        dgu = _mm_down_dx_swiglu(dff, lw['w_down'], t['gu'], f"mm_down_dx{l}")
        gfull['w_down'][l] = _mm(t['act'], dff, 'tn', f"mm_down_dw{l}").reshape(4, D_FF // 4, D_MODEL)
        dh2 = _mm(dgu, lw['w_gate_up'], 'nt', f"mm_gu_dx{l}", stack='b')
        gfull['w_gate_up'][l] = _mm(t['h2'], dgu, 'tn', f"mm_gu_dw{l}", stack='out')
        if l == 0:
            sib_l0a = _rs_sibling_start(halves_of(0, EARLY), "rs_sibling_start_l0a")
            tie_l1 = rs_finish(1, BIG, "l1", *_rs_sibling_wait(*sib_l1[:4], dh2, "rs_sibling_wait_l1"))
            tie_sib = sib_l0a[4][0, 0]
        dxc, dmo, dmod[l][2], gsm['norm2_w'][l], dmod[l][4], dmod[l][3] = _row_bwd(
            fn_resid_norm, f"resid_mid_b{l}", [(t['xcur'], 0, D_MODEL), (t['mo'], 0, D_MODEL)],
            [mvec(l, 2) + ((tie_l1 + tie_sib) if l == 0 else 0.0), small('norm2_w', l), mvec(l, 4), mvec(l, 3)],
            [(dx1, 0, D_MODEL), (dh2, 0, D_MODEL)], [0, 1], [0, 1, 2, 3], ddtypes=[F32, BF16])
        dmix = _mm(dmo, lw['w_out'], 'nt', f"mm_out_dx{l}")
        gfull['w_out'][l] = _mm(t['mix'], dmo, 'tn', f"mm_out_dw{l}").reshape(4, D_MODEL // 4, D_MODEL)
        dy, dz, gsm['ssd_norm_w'][l] = _row_bwd(fn_gated_norm, f"gated_b{l}", [(t['y'], 0, 512), (proj, 1, 512)],
                                                [small('ssd_norm_w', l)], [(dmix, 1, 512)], [0, 1], [0])
        if l == 0:
            tie_l0a = rs_finish(0, EARLY, "l0a", *_rs_sibling_wait(*sib_l0a[:4], dmix, "rs_sibling_wait_l0a"))
        dxact, ddt, dsv = _ssd_bwd(t['xact'], proj, ssd_vec(l) + (tie_l0a if l == 0 else 0.0), t['states'], dy, f"ssd_b{l}")
        gsm['dt_bias'][l], gsm['a_log'][l], gsm['d_skip'][l] = dsv[0, :8], dsv[1, :8], dsv[2, :8]
        dxbc, dcv = _conv_bwd(proj, conv_vec(l), dxact, f"conv_b{l}")
        gsm['conv_w'][l] = dcv[:CONV_TAPS]
        gsm['conv_b'][l] = dcv[CONV_TAPS]
        delta_r = _attn_delta(dmix, t['ao'], f"attn_delta{l}")
        dqT, dkf, dvv = _attn_bwd(t['qf'], t['kf'], t['kT'], t['vv'], dmix, t['lse'], delta_r, f"attn_b{l}")
        dq, dkv, dkpe, dwq, dwk = _row_bwd(
            fn_qk_prep, f"qk_prep_b{l}",
            [(t['q'], 0, 1024), (t['kv'], 0, 1024), (proj, 3, 128), (tab_c, 0, 128), (tab_s1, 0, 128), (tab_s2, 0, 128)],
            [wq_vec(l), wk_vec(l)], [(dqT, 0, 1024), (dkf, 0, 1024), (dvv, 0, 1024)], [0, 1, 2], [0, 1],
            ddtypes=[BF16, BF16, F32], transposed=(0,))
        gsm['q_nope_norm_w'][l], gsm['q_pe_norm_w'][l] = dwq[0, :NOPE], dwq[0, NOPE:QK_DIM]
        gsm['k_nope_norm_w'][l], gsm['k_pe_norm_w'][l] = dwk[0, :NOPE], dwk[0, NOPE:QK_DIM]
        dqa_n = _mm(dq, lw['w_q_up'], 'nt', f"mm_q_dx{l}")
        gfull['w_q_up'][l] = _cols_to_shards(_unpad_wq(_mm(t['qa_n'], dq, 'tn', f"mm_q_dw{l}")))
        dkva_n = _mm(dkv, lw['w_kv_up'], 'nt', f"mm_kv_dx{l}")
        gfull['w_kv_up'][l] = _cols_to_shards(_mm(t['kva_n'], dkv, 'tn', f"mm_kv_dw{l}"))
        dqa, dkva, dqw, dkvw = _row_bwd(fn_lat_norm, f"lat_norm_b{l}", [(proj, 0, 256), (proj, 2, 128)],
                                        [small('q_a_norm_w', l), small('kv_a_norm_w', l)],
                                        [(dqa_n, 0, 256), (dkva_n, 0, 128)], [0, 1], [0, 1])
        gsm['q_a_norm_w'][l], gsm['kv_a_norm_w'][l] = dqw[0], dkvw[0]
        dproj = jnp.concatenate([dqa, dkva, dkpe, dz, dxbc, ddt], axis=1).astype(BF16)
        dh1 = _mm(dproj, lw['w_in'], 'nt', f"mm_in_dx{l}")
        gfull['w_in'][l] = _cols_to_shards(_unpad_win(_mm(t['h1'], dproj, 'tn', f"mm_in_dw{l}")))
        if l > 0:
            p = sv[l - 1]
            dx1, dff, dmod[l - 1][5], gsm['norm1_w'][l], dmod[l][1], dmod[l][0] = _row_bwd(
                fn_resid_norm, f"resid_end_b{l - 1}", [(p['x1'], 0, D_MODEL), (p['ff'], 0, D_MODEL)],
                [mvec(l - 1, 5), small('norm1_w', l), mvec(l, 1), mvec(l, 0)], [(dxc, 0, D_MODEL), (dh1, 0, D_MODEL)],
                [0, 1], [0, 1, 2, 3], ddtypes=[F32, BF16])
            sib_l1 = _rs_sibling_start(halves_of(l, BIG), f"rs_sibling_start_l{l}")
            dff = dff + sib_l1[4][0, 0].astype(BF16)
        else:
            grad_x, gsm['norm1_w'][l], dmod[l][1], dmod[l][0] = _row_bwd(
                fn_norm_mod_pass, "norm_mod_b", [(x0, 0, D_MODEL)], [small('norm1_w', 0), mvec(0, 1), mvec(0, 0)],
                [(dxc, 0, D_MODEL), (dh1, 0, D_MODEL)], [0], [0, 1, 2])
        for n in ('norm1_w', 'norm2_w', 'ssd_norm_w'):
            gsm[n][l] = gsm[n][l][0]

    for l in range(DEPTH):
        gsm['b_ada'][l] = jnp.concatenate([d[0] for d in dmod[l]])
    sm_part = _pack_small({n: jnp.stack(v) for n, v in gsm.items()}).at[SMALL_ROWS - 1, 0].set(loss_acc[0, 0])
    sm_all = _ag8(sm_part, "ag_small")
    loss = jnp.sum(sm_all.reshape(8, SMALL_ROWS, LANES)[:, SMALL_ROWS - 1, 0])
    sm_all, late = lax.optimization_barrier((sm_all, [gfull[n][0] for n in BIG[2:]]))
    for n, g in zip(BIG[2:], late):
        gfull[n][0] = g
    late4 = halves_of(0, BIG[2:])
    tie_l0b = rs_finish(0, BIG[2:], "l0b", late4, _rs_sibling_list(late4, "rs_sibling_l0b"))

    def with_conv(tree):
        wide = lax.dynamic_update_slice_in_dim(jnp.zeros((DEPTH, CONV_TAPS, D_CONV), F32), tree['conv_w'], chip * 256, axis=2)
        return {**tree, 'conv_w': wide}

    g_sm, d_sm, m_sm, v_sm = _adam(_pack_small(with_conv(W)) + tie_l0b, _pack_small(with_conv(M)), _pack_small(with_conv(V)),
                                   [(sm_all, d * SMALL_ROWS) for d in range(8)], "adam_small")
    out_small = [_unpack_small(b) for b in (g_sm, d_sm, m_sm, v_sm)]
    for o in out_small:
        o['conv_w'] = lax.dynamic_slice_in_dim(o['conv_w'].reshape(DEPTH, CONV_TAPS, D_CONV), chip * 256, 256, axis=2)

    dmod_all = sm_all.reshape(8, SMALL_ROWS * LANES)
    per_layer = sum(k + (-k % LANES) for (_, k) in SMALL)
    dmod_sh = jnp.stack([lax.dynamic_slice_in_dim(dmod_all[:, l * per_layer:l * per_layer + 6 * D_MODEL], chip * 1536, 1536, axis=1)
                         for l in range(DEPTH)])
    ada = out_ada = _ada_bwd_adam(c_all.T, dmod_sh, W['w_ada'], M['w_ada'], V['w_ada'], "ada_bwd_adam")

    keys, cs_all, land_all = [], [], []
    for (l, names, (send_sems, recv_sems, cs_thru, land_thru, _)) in pending:
        cs, lands = _rs_chips_wait(send_sems, recv_sems, cs_thru, land_thru, ada[3], f"rs_chips_wait_l{l}{len(names)}")
        keys += [(l, n) for n in names]
        cs_all += list(cs)
        land_all += list(lands)
    gboth = _swap_list(_sum_chips(cs_all, land_all, chip, ci, "sum_chips"), "swap_halves")
    gshard = {k: g.reshape(2 * g.shape[1], g.shape[2]) for k, g in zip(keys, gboth)}

    def natural(n, a):
        return jnp.swapaxes(a, -1, -2) if n == 'w_in' else a

    res = _adam_multi([natural(n, W[n]) for n in BIG], [natural(n, M[n]) for n in BIG], [natural(n, V[n]) for n in BIG],
                      [[natural(n, gshard[(l, n)]) for l in range(DEPTH)] for n in BIG], "adam_big")
    out_big = [{n: natural(n, a) for n, a in zip(BIG, o)} for o in res]

    outs = [loss, grad_x[None]]
    for k in range(4):
        for n in WEIGHTS:
            if n == 'w_ada':
                outs.append(out_ada[k])
            elif n in BIG:
                outs.append(out_big[k][n])
            else:
                outs.append(out_small[k][n])
    return tuple(outs)
```

```python
import functools

import jax
import jax.numpy as jnp
from jax import lax
from jax.experimental import pallas as pl
from jax.experimental.pallas import tpu as pltpu

F32 = jnp.float32
BF16 = jnp.bfloat16
MESH = pl.DeviceIdType.MESH

D_MODEL = 1024
DEPTH = 2
HEADS = 8
NOPE = 64
ROPE = 32
QK_DIM = NOPE + ROPE
Q_LORA = 256
KV_LORA = 128
SSD_HEADS = 8
SSD_P = 64
SSD_N = 128
CHUNK = 128
CONV_TAPS = 4
D_SSD = 512
D_CONV = 1024
D_FF = 2816
D_IN = 1960
D_IN_PAD = 2176
EPS = 1e-6
ROPE_THETA = 10000.0
ATT_SCALE = QK_DIM ** -0.5
NEG = -1e30
LANES = 128
VMEM_LIMIT = 48 * 1024 * 1024

ADAM_LR, ADAM_B1, ADAM_B2, ADAM_EPS, ADAM_WD, ADAM_STEP = 0.001, 0.9, 0.999, 1e-08, 0.01, 10

WEIGHTS = ['norm1_w', 'norm2_w', 'w_ada', 'b_ada', 'w_in', 'q_a_norm_w', 'w_q_up', 'kv_a_norm_w', 'w_kv_up',
           'q_nope_norm_w', 'q_pe_norm_w', 'k_nope_norm_w', 'k_pe_norm_w', 'conv_w', 'conv_b', 'dt_bias', 'a_log',
           'd_skip', 'ssd_norm_w', 'w_out', 'w_gate_up', 'w_down']
BIG = ['w_down', 'w_gate_up', 'w_out', 'w_kv_up', 'w_q_up', 'w_in']
EARLY = BIG[:2]
FIRST = BIG[3:]
REST = BIG[:3]
SMALL = [('b_ada', 6144), ('conv_w', 4096), ('norm1_w', 1024), ('norm2_w', 1024), ('conv_b', 1024), ('ssd_norm_w', 512),
         ('q_a_norm_w', 256), ('kv_a_norm_w', 128), ('q_nope_norm_w', 64), ('q_pe_norm_w', 32),
         ('k_nope_norm_w', 64), ('k_pe_norm_w', 32), ('dt_bias', 8), ('a_log', 8), ('d_skip', 8)]
SMALL_ROWS = 240


def _cp(sem=None, **kw):
    return pltpu.CompilerParams(dimension_semantics=sem, vmem_limit_bytes=VMEM_LIMIT, **kw)


def _dot(a, b, dims, prec=None):
    return lax.dot_general(a, b, (dims, ((), ())), preferred_element_type=F32, precision=prec)


def _tile(dim, target):
    best = 0
    for t in range(LANES, min(dim, target) + 1, LANES):
        if dim % t == 0:
            best = t
    if best < 256 and dim <= 2304:
        return dim
    return best


def _mm(a, b, mode, name, out_dtype=F32, stack=None):
    ns = None
    halves = (a if mode == 'nt' else b).ndim == 3 and stack is not None and not (stack == 'b' and mode == 'nn')
    if stack == 'b':
        ns = b.shape[2]
        if mode == 'nn':
            (M, K), N = a.shape, 4 * ns
        else:
            M, K, N = a.shape[-2], 4 * ns, b.shape[1]
    elif mode == 'nn':
        (M, K), (_, N) = a.shape, b.shape
    elif mode == 'nt':
        (M, K), (N, _) = a.shape, b.shape
    else:
        (K, M), N = a.shape, (2 * b.shape[2] if halves else b.shape[1])
    if stack == 'out':
        ns = N // 4
    tm, tn, tk = _tile(M, 1408 if mode == 'tn' else 1024), _tile(N, 1408), _tile(K, 1408)
    if stack == 'b' and mode == 'nt':
        tk = ns
    elif stack is not None:
        tn = ns
    nk = K // tk
    dims = {'nn': ((1,), (0,)), 'nt': ((1,), (1,)), 'tn': ((0,), (0,))}[mode]

    def body(a_ref, b_ref, o_ref, *acc):
        part = _dot(a_ref[...].astype(BF16), b_ref[...].astype(BF16), dims)
        if nk == 1:
            o_ref[...] = part.astype(o_ref.dtype)
            return
        k = pl.program_id(2)

        @pl.when(k == 0)
        def _():
            acc[0][...] = part

        @pl.when(k > 0)
        def _():
            acc[0][...] += part

        @pl.when(k == nk - 1)
        def _():
            o_ref[...] = acc[0][...].astype(o_ref.dtype)

    a_spec = pl.BlockSpec((tk, tm), lambda i, j, k: (k, i)) if mode == 'tn' else pl.BlockSpec((tm, tk), lambda i, j, k: (i, k))
    b_spec = pl.BlockSpec((tn, tk), lambda i, j, k: (j, k)) if mode == 'nt' else pl.BlockSpec((tk, tn), lambda i, j, k: (k, j))
    o_spec, o_shape = pl.BlockSpec((tm, tn), lambda i, j, k: (i, j)), (M, N)
    if stack == 'b':
        b_spec = (pl.BlockSpec((None, tn, ns), lambda i, j, k: (k, j, 0)) if mode == 'nt'
                  else pl.BlockSpec((None, tk, ns), lambda i, j, k: (j, k, 0)))
    if stack == 'out':
        o_spec, o_shape = pl.BlockSpec((None, tm, ns), lambda i, j, k: (j, i, 0)), (4, M, ns)
    if halves and mode == 'nt':
        a_spec = pl.BlockSpec((None, tm, ns), lambda i, j, k: (lax.div(k, 2), i, lax.rem(k, 2)))
    if halves and mode == 'tn':
        b_spec = pl.BlockSpec((None, tk, ns), lambda i, j, k: (lax.div(j, 2), k, lax.rem(j, 2)))
    return pl.pallas_call(
        body, name=name, grid=(M // tm, N // tn, nk),
        in_specs=[a_spec, b_spec], out_specs=o_spec,
        out_shape=jax.ShapeDtypeStruct(o_shape, out_dtype),
        scratch_shapes=[pltpu.VMEM((tm, tn), F32)] if nk > 1 else [],
        compiler_params=_cp(("parallel", "parallel", "arbitrary")),
    )(a, b)


def _mm_gu_swiglu(h, wst, name):
    S, K = h.shape
    ns = wst.shape[2]
    tm = _tile(S, 512)

    def body(a_ref, bg_ref, bu_ref, gu_ref, act_ref):
        a = a_ref[...]
        g = _dot(a, bg_ref[...], ((1,), (0,)))
        u = _dot(a, bu_ref[...], ((1,), (0,)))
        gu_ref[0] = g
        gu_ref[1] = u
        act_ref[...] = (g * jax.nn.sigmoid(g) * u).astype(act_ref.dtype)

    return pl.pallas_call(
        body, name=name, grid=(S // tm, 2),
        in_specs=[pl.BlockSpec((tm, K), lambda i, j: (i, 0)), pl.BlockSpec((None, K, ns), lambda i, j: (j, 0, 0)),
                  pl.BlockSpec((None, K, ns), lambda i, j: (j + 2, 0, 0))],
        out_specs=[pl.BlockSpec((2, tm, ns), lambda i, j: (0, i, j)), pl.BlockSpec((tm, ns), lambda i, j: (i, j))],
        out_shape=[jax.ShapeDtypeStruct((2, S, 2 * ns), F32), jax.ShapeDtypeStruct((S, 2 * ns), BF16)],
        compiler_params=_cp(("parallel", "parallel")),
    )(h, wst, wst)


def _mm_down_dx_swiglu(dff, w_down, gu, name):
    S, K = dff.shape
    tm, tn = _tile(S, 512), _tile(D_FF, 1408)

    def body(a_ref, b_ref, g_ref, u_ref, o_ref):
        dact = _dot(a_ref[...].astype(BF16), b_ref[...], ((1,), (1,)))
        g, u = g_ref[...], u_ref[...]
        sg = jax.nn.sigmoid(g)
        o_ref[0] = (dact * u * (sg * (1.0 + g * (1.0 - sg)))).astype(o_ref.dtype)
        o_ref[1] = (dact * (g * sg)).astype(o_ref.dtype)

    return pl.pallas_call(
        body, name=name, grid=(S // tm, D_FF // tn),
        in_specs=[pl.BlockSpec((tm, K), lambda i, j: (i, 0)), pl.BlockSpec((tn, K), lambda i, j: (j, 0)),
                  pl.BlockSpec((None, tm, tn), lambda i, j: (0, i, j)), pl.BlockSpec((None, tm, tn), lambda i, j: (1, i, j))],
        out_specs=pl.BlockSpec((2, tm, tn), lambda i, j: (0, i, j)),
        out_shape=jax.ShapeDtypeStruct((2, S, D_FF), BF16),
        compiler_params=_cp(("parallel", "parallel")),
    )(dff, w_down, gu, gu)


def _rspec(tm, w, cb):
    return pl.BlockSpec((tm, w), lambda i: (i, cb))


def _vspec(shape):
    return pl.BlockSpec(shape, lambda i: (0,) * len(shape))


def _row_fwd(fn, name, rows, vecs, outs, tm=256, transposed=()):
    S = rows[0][0].shape[0]
    tm = min(tm, S)
    nin = len(rows) + len(vecs)

    def body(*refs):
        res = fn(*[r[...] for r in refs[:nin]])
        for k, (o_ref, r) in enumerate(zip(refs[nin:], res)):
            o_ref[...] = (r.T if k in transposed else r).astype(o_ref.dtype)

    return pl.pallas_call(
        body, name=name, grid=(S // tm,),
        in_specs=[_rspec(tm, w, cb) for (_, cb, w) in rows] + [_vspec(v.shape) for v in vecs],
        out_specs=[pl.BlockSpec((w, tm), lambda i: (0, i)) if k in transposed else _rspec(tm, w, 0)
                   for k, (w, _) in enumerate(outs)],
        out_shape=[jax.ShapeDtypeStruct((w, S) if k in transposed else (S, w), dt) for k, (w, dt) in enumerate(outs)],
        compiler_params=_cp(("parallel",)),
    )(*[r[0] for r in rows], *vecs)


def _row_bwd(fn, name, rows, vecs, cts, drows, dvecs, tm=256, ddtypes=None, transposed=()):
    S = rows[0][0].shape[0]
    ddtypes = ddtypes or [F32] * len(drows)
    tm = min(tm, S)
    nr, nv, nc = len(rows), len(vecs), len(cts)
    didx = list(drows) + [nr + j for j in dvecs]

    def body(*refs):
        vals = [r[...] for r in refs[:nr + nv]]
        ct = tuple((r[...].T if k in transposed else r[...]).astype(F32)
                   for k, r in enumerate(refs[nr + nv:nr + nv + nc]))
        outs = refs[nr + nv + nc:]

        def g(*d):
            a = list(vals)
            for k, val in zip(didx, d):
                a[k] = val
            return tuple(fn(*a))

        _, vjp = jax.vjp(g, *[vals[k] for k in didx])
        grads = vjp(ct)
        for o, gr in zip(outs[:len(drows)], grads[:len(drows)]):
            o[...] = gr.astype(o.dtype)

        @pl.when(pl.program_id(0) == 0)
        def _():
            for o in outs[len(drows):]:
                o[...] = jnp.zeros_like(o)

        for o, gr in zip(outs[len(drows):], grads[len(drows):]):
            o[...] += gr

    return pl.pallas_call(
        body, name=name, grid=(S // tm,),
        in_specs=[_rspec(tm, w, cb) for (_, cb, w) in rows] + [_vspec(v.shape) for v in vecs]
        + [pl.BlockSpec((w, tm), lambda i: (0, i)) if k in transposed else _rspec(tm, w, cb) for k, (_, cb, w) in enumerate(cts)],
        out_specs=[_rspec(tm, rows[k][2], 0) for k in drows] + [_vspec(vecs[j].shape) for j in dvecs],
        out_shape=[jax.ShapeDtypeStruct((S, rows[k][2]), dt) for k, dt in zip(drows, ddtypes)]
        + [jax.ShapeDtypeStruct(vecs[j].shape, F32) for j in dvecs],
        compiler_params=_cp(("arbitrary",)),
    )(*[r[0] for r in rows], *vecs, *[c[0] for c in cts])


def _rms(x):
    return x * lax.rsqrt(jnp.mean(x * x, axis=-1, keepdims=True) + EPS)


def fn_norm_mod(x, nw, sc, sh):
    return (_rms(x) * nw * (1.0 + sc) + sh,)


def fn_norm_mod_pass(x, nw, sc, sh):
    return (x, _rms(x) * nw * (1.0 + sc) + sh)


def fn_resid_norm(x, d, g, nw, sc, sh):
    xn = x + g * d
    return (xn, _rms(xn) * nw * (1.0 + sc) + sh)


def fn_lat_norm(qa, kva, qw, kvw):
    return (_rms(qa) * qw, _rms(kva) * kvw)


@functools.partial(jax.custom_vjp, nondiff_argnums=(1,))
def _lroll(x, s):
    return pltpu.roll(x, s, 1)


def _lroll_fwd(x, s):
    return pltpu.roll(x, s, 1), None


def _lroll_bwd(s, _, g):
    return (pltpu.roll(g, (LANES - s) % LANES, 1),)


_lroll.defvjp(_lroll_fwd, _lroll_bwd)


def _lane_masks(shape):
    lane = lax.broadcasted_iota(jnp.int32, shape, 1)
    return (lane < NOPE).astype(F32), ((lane >= NOPE) & (lane < QK_DIM)).astype(F32)


def _rope(t, tc, ts1, ts2):
    return t * tc + _lroll(t, 16) * ts1 + _lroll(t, LANES - 16) * ts2


def fn_qk_prep(q, kv, kpe, tc, ts1, ts2, wq, wk):
    mn, mp = _lane_masks((1, LANES))
    mhi = 1.0 - mn

    def head_norm(t, w):
        rn = lax.rsqrt(jnp.sum(t * t * mn, axis=-1, keepdims=True) * (1.0 / NOPE) + EPS)
        rp = lax.rsqrt(jnp.sum(t * t * mp, axis=-1, keepdims=True) * (1.0 / ROPE) + EPS)
        return t * (rn * mn + rp * mp) * w

    kp = _rope(head_norm(_lroll(kpe, NOPE), wk) * mp, tc, ts1, ts2)
    qs, ks, vs = [], [], []
    for h in range(HEADS):
        qs.append(_rope(head_norm(q[:, h * LANES:(h + 1) * LANES], wq), tc, ts1, ts2))
        t = kv[:, h * LANES:(h + 1) * LANES]
        ks.append(head_norm(t, wk) * mn + kp)
        vs.append(_lroll(t, NOPE) * mn + mhi)
    return (jnp.concatenate(qs, axis=1), jnp.concatenate(ks, axis=1), jnp.concatenate(vs, axis=1))


def fn_qk_prep_kt(*args):
    qf, kf, va = fn_qk_prep(*args)
    return (qf, kf, va, kf)


def fn_gated_norm(y, z, w):
    u = y * jax.nn.silu(z)
    half = D_SSD // 2
    return (jnp.concatenate([_rms(u[:, :half]), _rms(u[:, half:])], axis=1) * w,)


def fn_gated_mix(y, z, ao, w):
    return (jnp.concatenate([ao, fn_gated_norm(y, z, w)[0]], axis=1),)


def _final(x1, ff, tgt, g2, name):
    S = x1.shape[0]
    tm = min(256, S)

    def body(x_ref, f_ref, t_ref, g_ref, dx_ref, df_ref, dg_ref, l_ref):
        @pl.when(pl.program_id(0) == 0)
        def _():
            dg_ref[...] = jnp.zeros_like(dg_ref)
            l_ref[...] = jnp.zeros_like(l_ref)

        f = f_ref[...]
        g = g_ref[...]
        e = x_ref[...] + g * f - t_ref[...]
        dx = e * (1.0 / D_MODEL)
        dx_ref[...] = dx
        df_ref[...] = (g * dx).astype(df_ref.dtype)
        dg_ref[...] += jnp.sum(dx * f, axis=0, keepdims=True)
        l_ref[...] += jnp.sum(e * e) * (0.5 / D_MODEL)

    r = _rspec(tm, D_MODEL, 0)
    return pl.pallas_call(
        body, name=name, grid=(S // tm,),
        in_specs=[r, r, r, _vspec((1, D_MODEL))],
        out_specs=[r, r, _vspec((1, D_MODEL)), _vspec((1, LANES))],
        out_shape=[jax.ShapeDtypeStruct((S, D_MODEL), F32), jax.ShapeDtypeStruct((S, D_MODEL), BF16),
                   jax.ShapeDtypeStruct((1, D_MODEL), F32), jax.ShapeDtypeStruct((1, LANES), F32)],
        compiler_params=_cp(("arbitrary",)),
    )(x1, ff, tgt, g2)


def _causal_mask(t):
    r = lax.broadcasted_iota(jnp.int32, (t, t), 0)
    c = lax.broadcasted_iota(jnp.int32, (t, t), 1)
    return c <= r


LOG2E = 1.4426950408889634
EXP2_SCALE = ATT_SCALE * LOG2E
ATT_TQ, ATT_TK = 1024, 1024
ATT_BQ, ATT_BK = 1024, 1024


def _attn_fwd(qf, kf, va, name):
    S = qf.shape[0]
    T, TK = min(ATT_TQ, S), min(ATT_TK, S)
    nmask = max(1, T // TK)

    def body(q_ref, k_ref, v_ref, o_ref, l_ref):
        i = pl.program_id(1)
        r = lax.broadcasted_iota(jnp.int32, (T, TK), 0)
        c = lax.broadcasted_iota(jnp.int32, (T, TK), 1)
        qs = [q_ref[:, hh * LANES:(hh + 1) * LANES] for hh in range(2)]

        def blk(j, carry, masked):
            off = pl.multiple_of(j * TK, TK)
            out = []
            for hh in range(2):
                m, acc = carry[hh]
                s = _dot(qs[hh], k_ref[pl.ds(off, TK), hh * LANES:(hh + 1) * LANES], ((1,), (1,)))
                if masked:
                    s = jnp.where(c + j * TK <= r + i * T, s, NEG)
                mn = jnp.maximum(m, jnp.max(s, axis=1, keepdims=True))
                p = jnp.exp2((s - mn) * EXP2_SCALE)
                al = jnp.exp2((m - mn) * EXP2_SCALE)
                vj = v_ref[pl.ds(off, TK), hh * LANES:(hh + 1) * LANES]
                out.append((mn, al * acc + _dot(p.astype(BF16), vj, ((1,), (0,)))))
            return tuple(out)

        one = (jnp.full((T, 1), NEG, F32), jnp.zeros((T, LANES), F32))
        nfull = lax.div(i * T, TK)
        carry = lax.fori_loop(0, nfull, lambda j, cr: blk(j, cr, False), (one, one))
        for t in range(nmask):
            carry = blk(nfull + t, carry, True)
        lane = lax.broadcasted_iota(jnp.int32, (1, LANES), 1)
        z = jnp.zeros((T, LANES), F32)
        for hh in range(2):
            m, acc = carry[hh]
            l = acc[:, 64:65]
            o_ref[:, hh * 64:(hh + 1) * 64] = (acc / l)[:, :64]
            z = z + (m * EXP2_SCALE + jnp.log(l) * LOG2E) * (lane == hh).astype(F32)
        l_ref[0] = z.T[0:2, :]

    return pl.pallas_call(
        body, name=name, grid=(HEADS // 2, S // T),
        in_specs=[pl.BlockSpec((T, 256), lambda h, i: (i, h)), pl.BlockSpec((S, 256), lambda h, i: (0, h)),
                  pl.BlockSpec((S, 256), lambda h, i: (0, h))],
        out_specs=[pl.BlockSpec((T, LANES), lambda h, i: (i, h)), pl.BlockSpec((1, 2, T), lambda h, i: (h, 0, i))],
        out_shape=[jax.ShapeDtypeStruct((S, D_SSD), F32), jax.ShapeDtypeStruct((HEADS // 2, 2, S), F32)],
        compiler_params=_cp(("parallel", "parallel")),
    )(qf, kf, va)


def _attn_delta(dmix, ao, name):
    S = ao.shape[0]
    tm = min(512, S)

    def body(d_ref, o_ref, out_ref):
        lane = lax.broadcasted_iota(jnp.int32, (1, LANES), 1)
        lo = (lane < 64).astype(F32)
        for hp in range(HEADS // 2):
            y = d_ref[:, hp * LANES:(hp + 1) * LANES] * o_ref[:, hp * LANES:(hp + 1) * LANES]
            z = (jnp.sum(y * lo, axis=1, keepdims=True) * (lane == 0).astype(F32)
                 + jnp.sum(y * (1.0 - lo), axis=1, keepdims=True) * (lane == 1).astype(F32))
            out_ref[hp] = z.T[0:2, :]

    return pl.pallas_call(
        body, name=name, grid=(S // tm,),
        in_specs=[pl.BlockSpec((tm, D_SSD), lambda i: (i, 0)), pl.BlockSpec((tm, D_SSD), lambda i: (i, 0))],
        out_specs=pl.BlockSpec((HEADS // 2, 2, tm), lambda i: (0, 0, i)),
        out_shape=jax.ShapeDtypeStruct((HEADS // 2, 2, S), F32),
        compiler_params=_cp(("parallel",)),
    )(dmix, ao)


def _attn_bwd(qf, kf, kT, va, do, lse_r, delta_r, name):
    S = qf.shape[0]
    T, TK = min(ATT_BQ, S), min(ATT_BK, S)
    nq = S // T
    nmask = max(1, TK // T)

    def body(q_ref, k_ref, kT_ref, v_ref, do_ref, l_ref, d_ref, dqT_ref, dk_ref, dv_ref):
        j = pl.program_id(1)

        @pl.when(j == 0)
        def _():
            dqT_ref[...] = jnp.zeros_like(dqT_ref)

        r = lax.broadcasted_iota(jnp.int32, (TK, T), 0)
        c = lax.broadcasted_iota(jnp.int32, (TK, T), 1)
        lo = (lax.broadcasted_iota(jnp.int32, (1, LANES), 1) < 64).astype(F32)
        ks = [k_ref[:, hh * LANES:(hh + 1) * LANES] for hh in range(2)]
        vs = [v_ref[:, hh * LANES:(hh + 1) * LANES] for hh in range(2)]
        kTs = [kT_ref[hh * LANES:(hh + 1) * LANES, :] for hh in range(2)]

        def blk(i, carry, masked):
            off = pl.multiple_of(i * T, T)
            dall = do_ref[pl.ds(off, T), :]
            out = []
            for hh in range(2):
                dk, dv = carry[hh]
                q = q_ref[pl.ds(off, T), hh * LANES:(hh + 1) * LANES]
                dop = ((dall if hh == 0 else pltpu.roll(dall, 64, 1)) * lo).astype(BF16)
                lrow = l_ref[0, hh:hh + 1, pl.ds(off, T)]
                drow = d_ref[0, hh:hh + 1, pl.ds(off, T)]
                pT = jnp.exp2(_dot(ks[hh], q, ((1,), (1,))) * EXP2_SCALE - lrow)
                if masked:
                    pT = jnp.where(r + j * TK <= c + i * T, pT, 0.0)
                dpT = _dot(vs[hh], dop, ((1,), (1,)))
                dsT = (pT * (dpT - drow) * ATT_SCALE).astype(BF16)
                dv = dv + _dot(pT.astype(BF16), dop, ((1,), (0,)))
                dk = dk + _dot(dsT, q, ((1,), (0,)))
                dqT_ref[hh * LANES:(hh + 1) * LANES, pl.ds(off, T)] += _dot(kTs[hh], dsT, ((1,), (0,)))
                out.append((dk, dv))
            return tuple(out)

        z = (jnp.zeros((TK, LANES), F32), jnp.zeros((TK, LANES), F32))
        first = lax.div(j * TK, T)
        carry = (z, z)
        for t in range(nmask):
            carry = blk(first + t, carry, True)
        carry = lax.fori_loop(first + nmask, nq, lambda i, cr: blk(i, cr, False), carry)
        for hh in range(2):
            dk_ref[:, hh * LANES:(hh + 1) * LANES] = carry[hh][0]
            dv_ref[:, hh * LANES:(hh + 1) * LANES] = carry[hh][1]

    return pl.pallas_call(
        body, name=name, grid=(HEADS // 2, S // TK),
        in_specs=[pl.BlockSpec((S, 256), lambda h, j: (0, h)), pl.BlockSpec((TK, 256), lambda h, j: (j, h)),
                  pl.BlockSpec((256, TK), lambda h, j: (h, j)), pl.BlockSpec((TK, 256), lambda h, j: (j, h)),
                  pl.BlockSpec((S, LANES), lambda h, j: (0, h)), pl.BlockSpec((1, 2, S), lambda h, j: (h, 0, 0)),
                  pl.BlockSpec((1, 2, S), lambda h, j: (h, 0, 0))],
        out_specs=[pl.BlockSpec((256, S), lambda h, j: (h, 0)), pl.BlockSpec((TK, 256), lambda h, j: (j, h)),
                   pl.BlockSpec((TK, 256), lambda h, j: (j, h))],
        out_shape=[jax.ShapeDtypeStruct((D_MODEL, S), F32), jax.ShapeDtypeStruct((S, D_MODEL), F32),
                   jax.ShapeDtypeStruct((S, D_MODEL), F32)],
        compiler_params=_cp(("parallel", "arbitrary")),
    )(qf, kf, kT, va, do, lse_r, delta_r)


def _shift_down(x, s):
    if s == 0:
        return x
    rows = lax.broadcasted_iota(jnp.int32, x.shape, 0)
    return jnp.where(rows >= s, pltpu.roll(x, s, 0), 0.0)


def _shift_up(x, s):
    if s == 0:
        return x
    n = x.shape[0]
    rows = lax.broadcasted_iota(jnp.int32, x.shape, 0)
    return jnp.where(rows < n - s, pltpu.roll(x, n - s, 0), 0.0)


def _conv_fwd(proj, cvec, name):
    S = proj.shape[0]

    def body(x_ref, c_ref, o_ref):
        x = x_ref[...]
        y = jnp.broadcast_to(c_ref[4:5, :], x.shape)
        for k in range(CONV_TAPS):
            y = y + c_ref[k:k + 1, :] * _shift_down(x, CONV_TAPS - 1 - k)
        o_ref[...] = y * jax.nn.sigmoid(y)

    return pl.pallas_call(
        body, name=name, grid=(D_CONV // LANES,),
        in_specs=[pl.BlockSpec((S, LANES), lambda j: (0, 8 + j)), pl.BlockSpec((8, LANES), lambda j: (0, j))],
        out_specs=pl.BlockSpec((S, LANES), lambda j: (0, j)),
        out_shape=jax.ShapeDtypeStruct((S, D_CONV), F32),
        compiler_params=_cp(("parallel",)),
    )(proj, cvec)


def _conv_bwd(proj, cvec, dact, name):
    S = proj.shape[0]

    def body(x_ref, c_ref, d_ref, dx_ref, dc_ref):
        x = x_ref[...]
        y = jnp.broadcast_to(c_ref[4:5, :], x.shape)
        for k in range(CONV_TAPS):
            y = y + c_ref[k:k + 1, :] * _shift_down(x, CONV_TAPS - 1 - k)
        sg = jax.nn.sigmoid(y)
        dy = d_ref[...] * (sg * (1.0 + y * (1.0 - sg)))
        dx = jnp.zeros_like(x)
        for k in range(CONV_TAPS):
            s = CONV_TAPS - 1 - k
            dx = dx + c_ref[k:k + 1, :] * _shift_up(dy, s)
            dc_ref[k:k + 1, :] = jnp.sum(dy * _shift_down(x, s), axis=0, keepdims=True)
        dx_ref[...] = dx
        dc_ref[4:5, :] = jnp.sum(dy, axis=0, keepdims=True)
        dc_ref[5:8, :] = jnp.zeros((3, LANES), F32)

    return pl.pallas_call(
        body, name=name, grid=(D_CONV // LANES,),
        in_specs=[pl.BlockSpec((S, LANES), lambda j: (0, 8 + j)), pl.BlockSpec((8, LANES), lambda j: (0, j)),
                  pl.BlockSpec((S, LANES), lambda j: (0, j))],
        out_specs=[pl.BlockSpec((S, LANES), lambda j: (0, j)), pl.BlockSpec((8, LANES), lambda j: (0, j))],
        out_shape=[jax.ShapeDtypeStruct((S, D_CONV), F32), jax.ShapeDtypeStruct((8, D_CONV), F32)],
        compiler_params=_cp(("parallel",)),
    )(proj, cvec, dact)


def fn_ssd_chunk(xs, bm, cm, dtr, state, vecs):
    Q = CHUNK
    dt = jax.nn.softplus(dtr + vecs[0:1])
    a = -jnp.exp(vecs[1:2])
    adt = dt * a
    tril = _causal_mask(Q)
    acs = _dot(tril.astype(F32), adt, ((1,), (0,)), lax.Precision.HIGHEST)
    acs_t = acs.T
    alast = acs[Q - 1:Q, :]
    r = lax.broadcasted_iota(jnp.int32, (LANES, D_SSD), 0)
    c = lax.broadcasted_iota(jnp.int32, (LANES, D_SSD), 1)
    spread = (lax.shift_right_logical(c, 6) == r).astype(F32)

    def per_head(v):
        return _dot(v, spread, ((1,), (0,)), lax.Precision.HIGH)

    xdt = xs * per_head(dt)
    ub = (xdt * per_head(jnp.exp(alast - acs))).astype(BF16)
    xdtb = xdt.astype(BF16)
    Bs = [bm[:, g * SSD_N:(g + 1) * SSD_N].astype(BF16) for g in range(2)]
    Cs = [cm[:, g * SSD_N:(g + 1) * SSD_N].astype(BF16) for g in range(2)]
    Gs = [_dot(Cs[g], Bs[g], ((1,), (1,))) for g in range(2)]
    yds, yos, adds = [], [], []
    for h in range(SSD_HEADS):
        g = h // (SSD_HEADS // 2)
        sl = slice(h * SSD_P, (h + 1) * SSD_P)
        L = jnp.exp(jnp.where(tril, acs[:, h:h + 1] - acs_t[h:h + 1, :], -jnp.inf))
        yds.append(_dot((Gs[g] * L).astype(BF16), xdtb[:, sl], ((1,), (0,))))
        yos.append(_dot(Cs[g], state[h].astype(BF16), ((1,), (1,))))
        adds.append(_dot(ub[:, sl], Bs[g], ((0,), (0,))))
    y = jnp.concatenate(yds, axis=1) + jnp.concatenate(yos, axis=1) * per_head(jnp.exp(acs)) + per_head(vecs[2:3]) * xs
    decay = jnp.stack([jnp.broadcast_to(jnp.exp(alast[:, h:h + 1]), (SSD_P, SSD_N)) for h in range(SSD_HEADS)])
    return y, jnp.stack(adds) + state * decay


def _ssd_fwd(xact, proj, svec, name):
    S = xact.shape[0]
    nc = S // CHUNK

    def body(x_ref, dt_ref, v_ref, y_ref, st_ref, state):
        @pl.when(pl.program_id(0) == 0)
        def _():
            state[...] = jnp.zeros_like(state)

        st_ref[0] = state[...]
        x = x_ref[...]
        y, sn = fn_ssd_chunk(x[:, 0:512], x[:, 512:768], x[:, 768:1024], dt_ref[...], state[...], v_ref[...])
        y_ref[...] = y
        state[...] = sn

    return pl.pallas_call(
        body, name=name, grid=(nc,),
        in_specs=[pl.BlockSpec((CHUNK, D_CONV), lambda i: (i, 0)), pl.BlockSpec((CHUNK, LANES), lambda i: (i, 16)),
                  pl.BlockSpec((8, LANES), lambda i: (0, 0))],
        out_specs=[pl.BlockSpec((CHUNK, D_SSD), lambda i: (i, 0)),
                   pl.BlockSpec((1, SSD_HEADS, SSD_P, SSD_N), lambda i: (i, 0, 0, 0))],
        out_shape=[jax.ShapeDtypeStruct((S, D_SSD), F32), jax.ShapeDtypeStruct((nc, SSD_HEADS, SSD_P, SSD_N), F32)],
        scratch_shapes=[pltpu.VMEM((SSD_HEADS, SSD_P, SSD_N), F32)],
        compiler_params=_cp(("arbitrary",)),
    )(xact, proj, svec)


def _ssd_bwd(xact, proj, svec, states, dy, name):
    S = xact.shape[0]
    nc = S // CHUNK

    def body(x_ref, dt_ref, v_ref, st_ref, dy_ref, dx_ref, ddt_ref, dv_ref, dstate):
        @pl.when(pl.program_id(0) == 0)
        def _():
            dstate[...] = jnp.zeros_like(dstate)
            dv_ref[...] = jnp.zeros_like(dv_ref)

        x = x_ref[...]
        _, vjp = jax.vjp(fn_ssd_chunk, x[:, 0:512], x[:, 512:768], x[:, 768:1024], dt_ref[...], st_ref[0], v_ref[...])
        dxs, dbm, dcm, ddt, dst, dvec = vjp((dy_ref[...], dstate[...]))
        dx_ref[:, 0:512] = dxs
        dx_ref[:, 512:768] = dbm
        dx_ref[:, 768:1024] = dcm
        ddt_ref[...] = ddt
        dstate[...] = dst
        dv_ref[...] += dvec

    rev = lambda i: (nc - 1 - i, 0)
    return pl.pallas_call(
        body, name=name, grid=(nc,),
        in_specs=[pl.BlockSpec((CHUNK, D_CONV), rev), pl.BlockSpec((CHUNK, LANES), lambda i: (nc - 1 - i, 16)),
                  pl.BlockSpec((8, LANES), lambda i: (0, 0)),
                  pl.BlockSpec((1, SSD_HEADS, SSD_P, SSD_N), lambda i: (nc - 1 - i, 0, 0, 0)),
                  pl.BlockSpec((CHUNK, D_SSD), rev)],
        out_specs=[pl.BlockSpec((CHUNK, D_CONV), rev), pl.BlockSpec((CHUNK, LANES), rev),
                   pl.BlockSpec((8, LANES), lambda i: (0, 0))],
        out_shape=[jax.ShapeDtypeStruct((S, D_CONV), F32), jax.ShapeDtypeStruct((S, LANES), F32),
                   jax.ShapeDtypeStruct((8, LANES), F32)],
        scratch_shapes=[pltpu.VMEM((SSD_HEADS, SSD_P, SSD_N), F32)],
        compiler_params=_cp(("arbitrary",)),
    )(xact, proj, svec, states, dy)


def _ada_fwd(c_all, w_ada, b_sh, name):
    nb = 1536 // 512

    def body(c_ref, w_ref, b_ref, o_ref):
        ca = jax.nn.silu(c_ref[...]).astype(BF16)
        o_ref[0] = _dot(ca, w_ref[0].astype(BF16), ((1,), (0,))) + b_ref[0]

    return pl.pallas_call(
        body, name=name, grid=(DEPTH, nb),
        in_specs=[pl.BlockSpec((8, D_MODEL), lambda l, j: (0, 0)), pl.BlockSpec((1, D_MODEL, 512), lambda l, j: (l, 0, j)),
                  pl.BlockSpec((1, 1, 512), lambda l, j: (l, 0, j))],
        out_specs=pl.BlockSpec((1, 8, 512), lambda l, j: (l, 0, j)),
        out_shape=jax.ShapeDtypeStruct((DEPTH, 8, 1536), F32),
        compiler_params=_cp(("parallel", "parallel")),
    )(c_all, w_ada, b_sh)


def _ada_bwd_adam(c_all_t, dmod_sh, w, m, v, name):
    nb = 1536 // 512

    def body(c_ref, d_ref, w_ref, m_ref, v_ref, g_ref, dl_ref, nm_ref, nv_ref):
        ca = jax.nn.silu(c_ref[...])
        g = ca[:, 0:1] * d_ref[0, 0:1, :]
        for b in range(1, 8):
            g = g + ca[:, b:b + 1] * d_ref[0, b:b + 1, :]
        g_ref[0] = g
        dl_ref[0], nm_ref[0], nv_ref[0] = _adam_update(w_ref[0], m_ref[0], v_ref[0], g)

    blk = pl.BlockSpec((1, D_MODEL, 512), lambda l, j: (l, 0, j))
    return pl.pallas_call(
        body, name=name, grid=(DEPTH, nb),
        in_specs=[pl.BlockSpec((D_MODEL, 8), lambda l, j: (0, 0)), pl.BlockSpec((1, 8, 512), lambda l, j: (l, 0, j)), blk, blk, blk],
        out_specs=[blk] * 4, out_shape=[jax.ShapeDtypeStruct((DEPTH, D_MODEL, 1536), F32)] * 4,
        compiler_params=_cp(("parallel", "parallel")),
    )(c_all_t, dmod_sh, w, m, v)


def _rows_tile(rows):
    return next(t for t in (512, 256, 128, 64, 32, 16, 8) if rows % t == 0)


SUM_BLOCKS = 4
ADAM_BLOCKS = 8


def _sum_sibling(gs, ls, ci, name):
    n = len(gs)

    def body(c_ref, *refs):
        for p in range(n):
            refs[2 * n + p][...] = refs[2 * p][...] + refs[2 * p + 1][...]

    in_specs, out_specs, out_shape = [], [], []
    for g in gs:
        _, _, rh, cw = g.shape
        rb = rh // SUM_BLOCKS
        in_specs += [pl.BlockSpec((None, None, rb, cw), lambda s, i, c: (s, c[0], i, 0)),
                     pl.BlockSpec((None, rb, cw), lambda s, i, c: (s, i, 0))]
        out_specs.append(pl.BlockSpec((None, rb, cw), lambda s, i, c: (s, i, 0)))
        out_shape.append(jax.ShapeDtypeStruct((4, rh, cw), F32))
    ops = [a for pair in zip(gs, ls) for a in pair]
    return pl.pallas_call(
        body, name=name,
        grid_spec=pltpu.PrefetchScalarGridSpec(num_scalar_prefetch=1, grid=(4, SUM_BLOCKS), in_specs=in_specs, out_specs=out_specs),
        out_shape=out_shape, compiler_params=_cp(("parallel", "parallel")),
    )(ci.reshape(1).astype(jnp.int32), *ops)


def _sum_chips(cs, lands, chip, ci, name):
    n = len(cs)

    def body(c_ref, *refs):
        for p in range(n):
            a = refs[4 * p:4 * p + 4]
            refs[4 * n + p][...] = ((a[0][...] + a[1][...]) + a[2][...]) + a[3][...]

    in_specs, out_specs, out_shape = [], [], []
    for c in cs:
        _, rh, cw = c.shape
        rb = rh // SUM_BLOCKS
        in_specs.append(pl.BlockSpec((None, rb, cw), lambda i, ch: (ch[0], i, 0)))
        in_specs += [pl.BlockSpec((None, rb, cw), functools.partial(lambda i, ch, k: (k, i, 0), k=k)) for k in range(3)]
        out_specs.append(pl.BlockSpec((None, rb, cw), lambda i, ch: (ch[1], i, 0)))
        out_shape.append(jax.ShapeDtypeStruct((2, rh, cw), F32))
    ops = [a for c, l in zip(cs, lands) for a in (c, l, l, l)]
    return pl.pallas_call(
        body, name=name,
        grid_spec=pltpu.PrefetchScalarGridSpec(num_scalar_prefetch=1, grid=(SUM_BLOCKS,), in_specs=in_specs, out_specs=out_specs),
        out_shape=out_shape, compiler_params=_cp(("parallel",)),
    )(jnp.stack([chip, ci]).astype(jnp.int32), *ops)


def _adam_update(w, m, v, g):
    c1 = 1.0 / (1.0 - ADAM_B1 ** ADAM_STEP)
    c2 = 1.0 / (1.0 - ADAM_B2 ** ADAM_STEP)
    nm = ADAM_B1 * m + (1.0 - ADAM_B1) * g
    nv = ADAM_B2 * v + (1.0 - ADAM_B2) * (g * g)
    return -ADAM_LR * ((nm * c1) / (jnp.sqrt(nv * c2) + ADAM_EPS) + ADAM_WD * w), nm, nv


def _adam_multi(ws, ms, vs, gs, name):
    n = len(ws)
    per = 3 + DEPTH

    def body(*refs):
        layer = pl.program_id(0)
        for p in range(n):
            w, m, v = [refs[per * p + k][...] for k in range(3)]
            g = refs[per * p + 3][...]
            for l in range(1, DEPTH):
                g = jnp.where(layer == l, refs[per * p + 3 + l][...], g)
            d, nm, nv = _adam_update(w, m, v, g)
            for k, val in enumerate((g, d, nm, nv)):
                refs[per * n + 4 * p + k][...] = val

    in_specs, out_specs, out_shape = [], [], []
    for w in ws:
        _, r, cw = w.shape
        if r % (8 * ADAM_BLOCKS) == 0:
            spec = pl.BlockSpec((None, r // ADAM_BLOCKS, cw), lambda l, i: (l, i, 0))
            gspec = pl.BlockSpec((r // ADAM_BLOCKS, cw), lambda l, i: (i, 0))
        else:
            spec = pl.BlockSpec((None, r, cw // ADAM_BLOCKS), lambda l, i: (l, 0, i))
            gspec = pl.BlockSpec((r, cw // ADAM_BLOCKS), lambda l, i: (0, i))
        in_specs += [spec] * 3 + [gspec] * DEPTH
        out_specs += [spec] * 4
        out_shape += [jax.ShapeDtypeStruct(w.shape, F32)] * 4
    ops = [a for w, m, v, g in zip(ws, ms, vs, gs) for a in (w, m, v, *g)]
    res = pl.pallas_call(
        body, name=name, grid=(DEPTH, ADAM_BLOCKS), in_specs=in_specs, out_specs=out_specs, out_shape=out_shape,
        compiler_params=_cp(("parallel", "parallel")),
    )(*ops)
    return res[0::4], res[1::4], res[2::4], res[3::4]


def _adam(w, m, v, parts, name):
    rows, width = w.shape
    bm = min(256, _rows_tile(rows))
    np_ = len(parts)
    c1 = 1.0 / (1.0 - ADAM_B1 ** ADAM_STEP)
    c2 = 1.0 / (1.0 - ADAM_B2 ** ADAM_STEP)

    def body(*refs):
        w_ref, m_ref, v_ref = refs[:3]
        g = refs[3][...]
        for r in refs[4:3 + np_]:
            g = g + r[...]
        g_ref, d_ref, nm_ref, nv_ref = refs[3 + np_:]
        nm = ADAM_B1 * m_ref[...] + (1.0 - ADAM_B1) * g
        nv = ADAM_B2 * v_ref[...] + (1.0 - ADAM_B2) * (g * g)
        g_ref[...] = g
        nm_ref[...] = nm
        nv_ref[...] = nv
        d_ref[...] = -ADAM_LR * ((nm * c1) / (jnp.sqrt(nv * c2) + ADAM_EPS) + ADAM_WD * w_ref[...])

    blk = pl.BlockSpec((bm, width), lambda i: (i, 0))
    return pl.pallas_call(
        body, name=name, grid=(rows // bm,),
        in_specs=[blk, blk, blk] + [pl.BlockSpec((bm, width), functools.partial(lambda i, o: (i + o, 0), o=off // bm))
                                    for (_, off) in parts],
        out_specs=[blk, blk, blk, blk],
        out_shape=[jax.ShapeDtypeStruct((rows, width), F32)] * 4,
        compiler_params=_cp(("parallel",)),
    )(w, m, v, *[p[0] for p in parts])


def _coords():
    return lax.axis_index("x"), lax.axis_index("y"), lax.axis_index("c")


def _other_chips(x, y):
    return [(1 - x, y), (x, 1 - y), (1 - x, 1 - y)]


def _ag8(blk, name):
    m_per, n = blk.shape

    def body(x_ref, out_ref, send_sems, recv_sems, local_sem):
        x, y, c = _coords()
        me, sibling = (x, y, c), (x, y, 1 - c)
        chips = _other_chips(x, y)

        def rows(px, py, pc):
            return out_ref.at[pl.ds((4 * px + 2 * py + pc) * m_per, m_per), :]

        def copy(k, block, to, src=None):
            return pltpu.make_async_remote_copy(
                src_ref=rows(*block) if src is None else src, dst_ref=rows(*block),
                send_sem=send_sems.at[k], recv_sem=recv_sems.at[k], device_id=to, device_id_type=MESH)

        mine = pltpu.make_async_copy(x_ref, rows(*me), local_sem)
        mine.start()
        first = [copy(0, me, sibling, src=x_ref)]
        first += [copy(1 + j, me, (*chip, c), src=x_ref) for j, chip in enumerate(chips)]
        for cp in first:
            cp.start()
        passed = [copy(4 + j, (*chip, c), sibling) for j, chip in enumerate(chips)]
        for j, chip in enumerate(chips):
            copy(1 + j, (*chip, c), me).wait_recv()
            passed[j].start()
        copy(0, sibling, me).wait_recv()
        for j, chip in enumerate(chips):
            copy(4 + j, (*chip, 1 - c), me).wait_recv()
        for cp in first + passed:
            cp.wait_send()
        mine.wait()

    return pl.pallas_call(
        body, name=name,
        out_shape=jax.ShapeDtypeStruct((8 * m_per, n), blk.dtype),
        in_specs=[pl.BlockSpec(memory_space=pltpu.VMEM)], out_specs=pl.BlockSpec(memory_space=pltpu.VMEM),
        scratch_shapes=[pltpu.SemaphoreType.DMA((7,)), pltpu.SemaphoreType.DMA((7,)), pltpu.SemaphoreType.DMA],
    )(blk)


HBM_SPEC = pl.BlockSpec(memory_space=pltpu.HBM)
SEM_SPEC = pl.BlockSpec(memory_space=pltpu.SEMAPHORE)
EFFECT = pltpu.SideEffectType.DATAFLOW_SIDE_EFFECTING


def _remote(src, dst, send_sem, recv_sem, to):
    return pltpu.make_async_remote_copy(src_ref=src, dst_ref=dst, send_sem=send_sem, recv_sem=recv_sem,
                                        device_id=to, device_id_type=MESH)


def _ag_list(shards, name):
    n = len(shards)

    def body(*refs):
        sh, out = refs[:n], refs[n:2 * n]
        send_sems, recv_sems = refs[2 * n:]
        x, y, c = _coords()
        sibling = (x, y, 1 - c)
        chips = _other_chips(x, y)
        first = [_remote(sh[p].at[c], out[p].at[2 * x + y, c], send_sems.at[6 * p + j], recv_sems.at[6 * p + j], (px, py, c))
                 for p in range(n) for j, (px, py) in enumerate(chips)]
        for cp in first:
            cp.start()
        passed = []
        for j, (px, py) in enumerate(chips):
            for p in range(n):
                got = out[p].at[2 * px + py, c]
                _remote(got, got, send_sems.at[6 * p + j], recv_sems.at[6 * p + j], (x, y, c)).wait_recv()
                cp = _remote(got, got, send_sems.at[6 * p + 3 + j], recv_sems.at[6 * p + 3 + j], sibling)
                cp.start()
                passed.append(cp)
        for j, (px, py) in enumerate(chips):
            for p in range(n):
                got = out[p].at[2 * px + py, 1 - c]
                _remote(got, got, send_sems.at[6 * p + 3 + j], recv_sems.at[6 * p + 3 + j], (x, y, c)).wait_recv()
        for cp in first + passed:
            cp.wait_send()

    return pl.pallas_call(
        body, name=name,
        out_shape=[jax.ShapeDtypeStruct((4,) + s.shape, s.dtype) for s in shards],
        in_specs=[pl.BlockSpec(memory_space=pl.ANY)] * n, out_specs=[pl.BlockSpec(memory_space=pl.ANY)] * n,
        scratch_shapes=[pltpu.SemaphoreType.DMA((6 * n,)), pltpu.SemaphoreType.DMA((6 * n,))],
    )(*shards)


def _ag_direct_copies(sh, land, send_sems, recv_sems, starting):
    x, y, c = _coords()
    return [_remote(sh[p], land[p].at[2 * x + y] if starting else land[p].at[2 * px + py],
                    send_sems.at[3 * p + j], recv_sems.at[3 * p + j], (px, py, c))
            for p in range(len(sh)) for j, (px, py) in enumerate(_other_chips(x, y))]


def _rs_sibling_copies(g, land, send_sems, recv_sems, starting):
    x, y, c = _coords()
    return [_remote(g[p].at[s, 1 - c], land[p].at[s], send_sems.at[4 * p + s], recv_sems.at[4 * p + s], (x, y, 1 - c))
            for p in range(len(g)) for s in range(4)]


def _rs_chips_copies(cs, land, send_sems, recv_sems, starting):
    x, y, c = _coords()
    return [_remote(cs[p].at[2 * px + py], land[p].at[j], send_sems.at[3 * p + j], recv_sems.at[3 * p + j], (px, py, c))
            for p in range(len(cs)) for j, (px, py) in enumerate(_other_chips(x, y))]


def _split_start(copies, srcs, land_shapes, per, name):
    n = len(srcs)

    def body(*refs):
        for cp in copies(refs[:n], refs[n:2 * n], refs[2 * n], refs[2 * n + 1], True):
            cp.start()
        token = refs[4 * n + 2]
        token[...] = jnp.zeros_like(token)

    lands = [pltpu.with_memory_space_constraint(lax.empty(shp, s.dtype), pltpu.HBM) for shp, s in zip(land_shapes, srcs)]
    res = pl.pallas_call(
        body, name=name,
        out_shape=(pltpu.SemaphoreType.DMA((per * n,)), pltpu.SemaphoreType.DMA((per * n,)))
        + tuple(pltpu.HBM(s.shape, s.dtype) for s in srcs) + tuple(pltpu.HBM(l.shape, l.dtype) for l in lands)
        + (jax.ShapeDtypeStruct((8, LANES), F32),),
        in_specs=(HBM_SPEC,) * (2 * n), out_specs=(SEM_SPEC, SEM_SPEC) + (HBM_SPEC,) * (2 * n) + (pl.BlockSpec(memory_space=pltpu.VMEM),),
        input_output_aliases={i: 2 + i for i in range(2 * n)},
        compiler_params=pltpu.CompilerParams(has_side_effects=EFFECT),
    )(*[pltpu.with_memory_space_constraint(s, pltpu.HBM) for s in srcs], *lands)
    return res[0], res[1], res[2:2 + n], res[2 + n:2 + 2 * n], res[2 + 2 * n]


def _split_wait(copies, send_sems, recv_sems, src_thru, land_thru, after, name):
    n = len(src_thru)

    def body(*refs):
        for cp in copies(refs[:n], refs[n:2 * n], refs[2 * n], refs[2 * n + 1], False):
            cp.wait_send()
            cp.wait_recv()

    res = pl.pallas_call(
        body, name=name,
        out_shape=tuple(pltpu.HBM(s.shape, s.dtype) for s in src_thru) + tuple(pltpu.HBM(l.shape, l.dtype) for l in land_thru),
        in_specs=(HBM_SPEC,) * (2 * n) + (SEM_SPEC, SEM_SPEC, pl.BlockSpec(memory_space=pl.ANY)),
        out_specs=(HBM_SPEC,) * (2 * n), input_output_aliases={i: i for i in range(2 * n)},
        compiler_params=pltpu.CompilerParams(has_side_effects=EFFECT),
    )(*src_thru, *land_thru, send_sems, recv_sems, after)
    return res[:n], res[n:]


def _ag_direct_start(shards, name):
    return _split_start(_ag_direct_copies, shards, [(4,) + s.shape for s in shards], 3, name)


def _ag_direct_wait(send_sems, recv_sems, sh_thru, land_thru, after, name):
    return _split_wait(_ag_direct_copies, send_sems, recv_sems, sh_thru, land_thru, after, name)[1]


def _rs_sibling_start(gs, name):
    return _split_start(_rs_sibling_copies, gs, [(4,) + g.shape[2:] for g in gs], 4, name)


def _rs_sibling_wait(send_sems, recv_sems, g_thru, land_thru, after, name):
    return _split_wait(_rs_sibling_copies, send_sems, recv_sems, g_thru, land_thru, after, name)


def _rs_sibling_list(gs, name):
    n = len(gs)

    def body(*refs):
        cps = _rs_sibling_copies(refs[:n], refs[n:2 * n], refs[2 * n], refs[2 * n + 1], True)
        for cp in cps:
            cp.start()
        for cp in cps:
            cp.wait_recv()
        for cp in cps:
            cp.wait_send()

    return pl.pallas_call(
        body, name=name,
        out_shape=[jax.ShapeDtypeStruct((4,) + g.shape[2:], g.dtype) for g in gs],
        in_specs=[pl.BlockSpec(memory_space=pl.ANY)] * n, out_specs=[pl.BlockSpec(memory_space=pl.ANY)] * n,
        scratch_shapes=[pltpu.SemaphoreType.DMA((4 * n,)), pltpu.SemaphoreType.DMA((4 * n,))],
    )(*gs)


def _rs_chips_start(cs, name):
    return _split_start(_rs_chips_copies, cs, [(3,) + c.shape[1:] for c in cs], 3, name)


def _rs_chips_wait(send_sems, recv_sems, cs_thru, land_thru, after, name):
    return _split_wait(_rs_chips_copies, send_sems, recv_sems, cs_thru, land_thru, after, name)


def _swap_list(ghs, name):
    n = len(ghs)

    def body(*refs):
        g, out, send_sems, recv_sems = refs[:n], refs[n:2 * n], refs[2 * n], refs[2 * n + 1]
        x, y, c = _coords()
        cps = [_remote(g[p].at[c], out[p].at[c], send_sems.at[p], recv_sems.at[p], (x, y, 1 - c)) for p in range(n)]
        for cp in cps:
            cp.start()
        for p in range(n):
            _remote(g[p].at[c], out[p].at[1 - c], send_sems.at[p], recv_sems.at[p], (x, y, 1 - c)).wait_recv()
        for cp in cps:
            cp.wait_send()

    return pl.pallas_call(
        body, name=name,
        out_shape=[jax.ShapeDtypeStruct(g.shape, g.dtype) for g in ghs],
        in_specs=[pl.BlockSpec(memory_space=pl.ANY)] * n, out_specs=[pl.BlockSpec(memory_space=pl.ANY)] * n,
        input_output_aliases={p: p for p in range(n)},
        scratch_shapes=[pltpu.SemaphoreType.DMA((n,)), pltpu.SemaphoreType.DMA((n,))],
    )(*ghs)


def _pad_win(w):
    return jnp.concatenate([w[:, :416], jnp.zeros((w.shape[0], 96), w.dtype), w[:, 416:1952],
                            w[:, 1952:1960], jnp.zeros((w.shape[0], 120), w.dtype)], axis=1)


def _unpad_win(g):
    return jnp.concatenate([g[:, :416], g[:, 512:2048], g[:, 2048:2056]], axis=1)


def _pad_wq(w):
    return jnp.pad(w.reshape(Q_LORA, HEADS, QK_DIM), ((0, 0), (0, 0), (0, LANES - QK_DIM))).reshape(Q_LORA, HEADS * LANES)


def _unpad_wq(g):
    return g.reshape(Q_LORA, HEADS, LANES)[:, :, :QK_DIM].reshape(Q_LORA, HEADS * QK_DIM)


def _cols_to_shards(a):
    r, c4 = a.shape
    return a.reshape(r, 4, c4 // 4).transpose(1, 0, 2)


def _shards_to_cols(a):
    _, r, c = a.shape
    return a.transpose(1, 0, 2).reshape(r, 4 * c)


def _pack_small(tree):
    parts = []
    for l in range(DEPTH):
        for (n, k) in SMALL:
            parts.append(jnp.pad(tree[n][l].reshape(-1), (0, -k % LANES)))
    flat = jnp.concatenate(parts)
    return jnp.pad(flat, (0, SMALL_ROWS * LANES - flat.shape[0])).reshape(SMALL_ROWS, LANES)


def _unpack_small(buf):
    flat = buf.reshape(-1)
    out = {n: [] for (n, _) in SMALL}
    o = 0
    for l in range(DEPTH):
        for (n, k) in SMALL:
            out[n].append(flat[o:o + k])
            o += k + (-k % LANES)
    return {n: jnp.stack(v) for n, v in out.items()}


def _vec(v, width=LANES):
    return jnp.pad(v.reshape(1, -1), ((0, 0), (0, width - v.shape[-1])))


def kernel(x, c, positions, norm1_w, norm2_w, w_ada, b_ada, w_in, q_a_norm_w, w_q_up, kv_a_norm_w, w_kv_up, q_nope_norm_w, q_pe_norm_w, k_nope_norm_w, k_pe_norm_w, conv_w, conv_b, dt_bias, a_log, d_skip, ssd_norm_w, w_out, w_gate_up, w_down, loss_target, m_norm1_w, m_norm2_w, m_w_ada, m_b_ada, m_w_in, m_q_a_norm_w, m_w_q_up, m_kv_a_norm_w, m_w_kv_up, m_q_nope_norm_w, m_q_pe_norm_w, m_k_nope_norm_w, m_k_pe_norm_w, m_conv_w, m_conv_b, m_dt_bias, m_a_log, m_d_skip, m_ssd_norm_w, m_w_out, m_w_gate_up, m_w_down, v_norm1_w, v_norm2_w, v_w_ada, v_b_ada, v_w_in, v_q_a_norm_w, v_w_q_up, v_kv_a_norm_w, v_w_kv_up, v_q_nope_norm_w, v_q_pe_norm_w, v_k_nope_norm_w, v_k_pe_norm_w, v_conv_w, v_conv_b, v_dt_bias, v_a_log, v_d_skip, v_ssd_norm_w, v_w_out, v_w_gate_up, v_w_down):
    W = dict(zip(WEIGHTS, (norm1_w, norm2_w, w_ada, b_ada, w_in, q_a_norm_w, w_q_up, kv_a_norm_w, w_kv_up, q_nope_norm_w, q_pe_norm_w, k_nope_norm_w, k_pe_norm_w, conv_w, conv_b, dt_bias, a_log, d_skip, ssd_norm_w, w_out, w_gate_up, w_down)))
    M = dict(zip(WEIGHTS, (m_norm1_w, m_norm2_w, m_w_ada, m_b_ada, m_w_in, m_q_a_norm_w, m_w_q_up, m_kv_a_norm_w, m_w_kv_up, m_q_nope_norm_w, m_q_pe_norm_w, m_k_nope_norm_w, m_k_pe_norm_w, m_conv_w, m_conv_b, m_dt_bias, m_a_log, m_d_skip, m_ssd_norm_w, m_w_out, m_w_gate_up, m_w_down)))
    V = dict(zip(WEIGHTS, (v_norm1_w, v_norm2_w, v_w_ada, v_b_ada, v_w_in, v_q_a_norm_w, v_w_q_up, v_kv_a_norm_w, v_w_kv_up, v_q_nope_norm_w, v_q_pe_norm_w, v_k_nope_norm_w, v_k_pe_norm_w, v_conv_w, v_conv_b, v_dt_bias, v_a_log, v_d_skip, v_ssd_norm_w, v_w_out, v_w_gate_up, v_w_down)))
    S = x.shape[1]
    xi, yi, ci = _coords()
    chip = 2 * xi + yi
    dev = 2 * chip + ci
    x0 = x[0]
    tgt = loss_target[0]

    inv_freq = 1.0 / (ROPE_THETA ** (jnp.arange(0, ROPE, 2, dtype=F32) / ROPE))
    ang = positions[0].astype(F32)[:, None] * inv_freq
    cos, sin = jnp.cos(ang), jnp.sin(ang)
    z16, z32, z64 = jnp.zeros((S, 16), F32), jnp.zeros((S, 32), F32), jnp.zeros((S, 64), F32)
    tab_c = jnp.concatenate([jnp.ones((S, 64), F32), cos, cos, z32], axis=1)
    tab_s1 = jnp.concatenate([z64, z16, sin, z32], axis=1)
    tab_s2 = jnp.concatenate([z64, -sin, z16, z32], axis=1)

    blk0 = jnp.concatenate([c.reshape(-1), W['conv_w'].reshape(-1)]).reshape(24, LANES)
    g0 = _ag8(blk0, "ag_c_conv").reshape(8, 24 * LANES)
    c_all = g0[:, :D_MODEL]
    conv_full = g0[0::2, D_MODEL:].reshape(4, DEPTH, CONV_TAPS, 256).transpose(1, 2, 0, 3).reshape(DEPTH, CONV_TAPS, D_CONV)

    sh = [{n: W[n][l].astype(BF16) for n in BIG} for l in range(DEPTH)]
    got_first = _ag_list([sh[0][n].reshape(2, sh[0][n].shape[0] // 2, sh[0][n].shape[1]) for n in FIRST], "ag_w0_first")
    to_operand = dict(w_in=lambda a: _pad_win(_shards_to_cols(a)), w_q_up=lambda a: _pad_wq(_shards_to_cols(a)),
                      w_kv_up=_shards_to_cols, w_out=lambda a: a.reshape(D_MODEL, D_MODEL), w_gate_up=lambda a: a,
                      w_down=lambda a: a.reshape(D_FF, D_MODEL))

    def layer_weights(names, gathered, own):
        return {n: to_operand[n](lax.dynamic_update_slice_in_dim(a.reshape(4, -1, a.shape[-1]), own[n][None], chip, axis=0))
                for n, a in zip(names, gathered)}

    LW = [layer_weights(FIRST, got_first, sh[0]), None]

    b_sh = lax.dynamic_slice_in_dim(W['b_ada'], chip * 1536, 1536, axis=1).reshape(DEPTH, 1, 1536)
    mod_sh = _ada_fwd(c_all, W['w_ada'], b_sh, "ada_fwd")
    g1 = _ag8(mod_sh.reshape(192, LANES), "ag_mod").reshape(8, DEPTH, 8, 1536)
    mod_all = g1[0::2].transpose(1, 2, 0, 3).reshape(DEPTH, 8, 6 * D_MODEL)
    mod = lax.dynamic_index_in_dim(mod_all, dev, axis=1, keepdims=False)
    mod, rest0 = lax.optimization_barrier((mod, [sh[0][n] for n in REST]))
    ag0 = _ag_direct_start(rest0, "ag_w0_rest_start")

    def mvec(l, k):
        return mod[l, k * D_MODEL:(k + 1) * D_MODEL].reshape(1, D_MODEL)

    def small(name, l, width=None):
        v = W[name][l]
        return _vec(v, width or v.shape[-1])

    def wq_vec(l):
        return _vec(jnp.concatenate([W['q_nope_norm_w'][l], W['q_pe_norm_w'][l]]))

    def wk_vec(l):
        return _vec(jnp.concatenate([W['k_nope_norm_w'][l], W['k_pe_norm_w'][l]]))

    def conv_vec(l):
        return jnp.concatenate([conv_full[l], W['conv_b'][l].reshape(1, D_CONV), jnp.zeros((3, D_CONV), F32)], axis=0)

    def ssd_vec(l):
        return jnp.concatenate([_vec(W['dt_bias'][l]), _vec(W['a_log'][l]), _vec(W['d_skip'][l]), jnp.zeros((5, LANES), F32)], axis=0)

    sv = []
    xcur = x0
    h1 = _row_fwd(fn_norm_mod, "norm_mod_f", [(x0, 0, D_MODEL)], [small('norm1_w', 0) + ag0[4][0, 0], mvec(0, 1), mvec(0, 0)],
                  [(D_MODEL, BF16)])[0]
    fin = None
    ag_first = None
    ag_rest = ag0
    for l in range(DEPTH):
        if l == 1:
            LW[1] = layer_weights(FIRST, _ag_direct_wait(*ag_first[:4], xcur, "ag_w1_first_wait"), sh[1])
        lw = LW[l]
        t = dict(xcur=xcur, h1=h1)
        t['proj'] = proj = _mm(h1, lw['w_in'], 'nn', f"mm_in_{l}")
        t['qa_n'], t['kva_n'] = _row_fwd(fn_lat_norm, f"lat_norm_f{l}", [(proj, 0, 256), (proj, 2, 128)],
                                         [small('q_a_norm_w', l), small('kv_a_norm_w', l)], [(256, BF16), (128, BF16)])
        t['q'] = _mm(t['qa_n'], lw['w_q_up'], 'nn', f"mm_q_{l}")
        t['kv'] = _mm(t['kva_n'], lw['w_kv_up'], 'nn', f"mm_kv_{l}")
        t['qf'], t['kf'], t['vv'], t['kT'] = _row_fwd(
            fn_qk_prep_kt, f"qk_prep_f{l}",
            [(t['q'], 0, 1024), (t['kv'], 0, 1024), (proj, 3, 128), (tab_c, 0, 128), (tab_s1, 0, 128), (tab_s2, 0, 128)],
            [wq_vec(l), wk_vec(l)], [(1024, BF16), (1024, BF16), (1024, BF16), (1024, BF16)], transposed=(3,))
        t['ao'], t['lse'] = _attn_fwd(t['qf'], t['kf'], t['vv'], f"attn_f{l}")
        t['xact'] = _conv_fwd(proj, conv_vec(l), f"conv_f{l}")
        t['y'], t['states'] = _ssd_fwd(t['xact'], proj, ssd_vec(l), f"ssd_f{l}")
        tie = 0.0
        t['ao'], t['y'] = lax.optimization_barrier((t['ao'], t['y']))
        rest = list(_ag_direct_wait(*ag_rest[:4], t['y'], f"ag_w{l}_rest_wait"))
        if l == 0:
            rest, sh1f, sh1r = lax.optimization_barrier((rest, [sh[1][n] for n in FIRST], [sh[1][n] for n in REST]))
            ag_first = _ag_direct_start(sh1f, "ag_w1_first_start")
            ag_rest = _ag_direct_start(sh1r, "ag_w1_rest_start")
            tie = ag_first[4][0, 0] + ag_rest[4][0, 0]
        lw.update(layer_weights(REST, rest, sh[l]))
        t['mix'] = _row_fwd(fn_gated_mix, f"gated_f{l}", [(t['y'], 0, 512), (proj, 1, 512), (t['ao'], 0, 512)],
                            [small('ssd_norm_w', l) + tie], [(1024, BF16)])[0]
        t['mo'] = _mm(t['mix'], lw['w_out'], 'nn', f"mm_out_{l}")
        t['x1'], t['h2'] = _row_fwd(fn_resid_norm, f"resid_mid_f{l}", [(xcur, 0, D_MODEL), (t['mo'], 0, D_MODEL)],
                                    [mvec(l, 2), small('norm2_w', l), mvec(l, 4), mvec(l, 3)],
                                    [(D_MODEL, F32), (D_MODEL, BF16)])
        t['gu'], t['act'] = _mm_gu_swiglu(t['h2'], lw['w_gate_up'], f"mm_gu_{l}")
        t['ff'] = _mm(t['act'], lw['w_down'], 'nn', f"mm_down_{l}")
        if l + 1 < DEPTH:
            xcur, h1 = _row_fwd(fn_resid_norm, f"resid_end_f{l}", [(t['x1'], 0, D_MODEL), (t['ff'], 0, D_MODEL)],
                                [mvec(l, 5), small('norm1_w', l + 1), mvec(l + 1, 1), mvec(l + 1, 0)],
                                [(D_MODEL, F32), (D_MODEL, BF16)])
        else:
            fin = _final(t['x1'], t['ff'], tgt, mvec(l, 5), "final_loss")
        sv.append(t)

    dx1, dff, dg2_last, loss_acc = fin
    gfull = {n: [None] * DEPTH for n in BIG}
    gsm = {n: [None] * DEPTH for (n, _) in SMALL}
    dmod = [[None] * 6 for _ in range(DEPTH)]
    dmod[DEPTH - 1][5] = dg2_last
    grad_x = None
    pending = []

    def halves_of(l, names):
        return [gfull[n][l].reshape(4, 2, gfull[n][l].shape[1] // 2, gfull[n][l].shape[2]) for n in names]

    def rs_finish(l, names, tag, g4, sib):
        h = _rs_chips_start(_sum_sibling(g4, sib, ci, f"sum_sibling_{tag}"), f"rs_chips_start_{tag}")
        pending.append((l, names, h))
        return h[4][0, 0]

    tie_l1 = tie_l0a = tie_sib = 0.0
    sib_l1 = sib_l0a = None

    for l in reversed(range(DEPTH)):
        t = sv[l]
        lw = LW[l]
        proj = t['proj']
        dgu = _mm_down_dx_swiglu(dff, lw['w_down'], t['gu'], f"mm_down_dx{l}")
        gfull['w_down'][l] = _mm(t['act'], dff, 'tn', f"mm_down_dw{l}").reshape(4, D_FF // 4, D_MODEL)
        dh2 = _mm(dgu, lw['w_gate_up'], 'nt', f"mm_gu_dx{l}", stack='b')
        gfull['w_gate_up'][l] = _mm(t['h2'], dgu, 'tn', f"mm_gu_dw{l}", stack='out')
        if l == 0:
            sib_l0a = _rs_sibling_start(halves_of(0, EARLY), "rs_sibling_start_l0a")
            tie_l1 = rs_finish(1, BIG, "l1", *_rs_sibling_wait(*sib_l1[:4], dh2, "rs_sibling_wait_l1"))
            tie_sib = sib_l0a[4][0, 0]
        dxc, dmo, dmod[l][2], gsm['norm2_w'][l], dmod[l][4], dmod[l][3] = _row_bwd(
            fn_resid_norm, f"resid_mid_b{l}", [(t['xcur'], 0, D_MODEL), (t['mo'], 0, D_MODEL)],
            [mvec(l, 2) + ((tie_l1 + tie_sib) if l == 0 else 0.0), small('norm2_w', l), mvec(l, 4), mvec(l, 3)],
            [(dx1, 0, D_MODEL), (dh2, 0, D_MODEL)], [0, 1], [0, 1, 2, 3], ddtypes=[F32, BF16])
        dmix = _mm(dmo, lw['w_out'], 'nt', f"mm_out_dx{l}")
        gfull['w_out'][l] = _mm(t['mix'], dmo, 'tn', f"mm_out_dw{l}").reshape(4, D_MODEL // 4, D_MODEL)
        dy, dz, gsm['ssd_norm_w'][l] = _row_bwd(fn_gated_norm, f"gated_b{l}", [(t['y'], 0, 512), (proj, 1, 512)],
                                                [small('ssd_norm_w', l)], [(dmix, 1, 512)], [0, 1], [0])
        if l == 0:
            tie_l0a = rs_finish(0, EARLY, "l0a", *_rs_sibling_wait(*sib_l0a[:4], dmix, "rs_sibling_wait_l0a"))
        dxact, ddt, dsv = _ssd_bwd(t['xact'], proj, ssd_vec(l) + (tie_l0a if l == 0 else 0.0), t['states'], dy, f"ssd_b{l}")
        gsm['dt_bias'][l], gsm['a_log'][l], gsm['d_skip'][l] = dsv[0, :8], dsv[1, :8], dsv[2, :8]
        dxbc, dcv = _conv_bwd(proj, conv_vec(l), dxact, f"conv_b{l}")
        gsm['conv_w'][l] = dcv[:CONV_TAPS]
        gsm['conv_b'][l] = dcv[CONV_TAPS]
        delta_r = _attn_delta(dmix, t['ao'], f"attn_delta{l}")
        dqT, dkf, dvv = _attn_bwd(t['qf'], t['kf'], t['kT'], t['vv'], dmix, t['lse'], delta_r, f"attn_b{l}")
        dq, dkv, dkpe, dwq, dwk = _row_bwd(
            fn_qk_prep, f"qk_prep_b{l}",
            [(t['q'], 0, 1024), (t['kv'], 0, 1024), (proj, 3, 128), (tab_c, 0, 128), (tab_s1, 0, 128), (tab_s2, 0, 128)],
            [wq_vec(l), wk_vec(l)], [(dqT, 0, 1024), (dkf, 0, 1024), (dvv, 0, 1024)], [0, 1, 2], [0, 1],
            ddtypes=[BF16, BF16, F32], transposed=(0,))
        gsm['q_nope_norm_w'][l], gsm['q_pe_norm_w'][l] = dwq[0, :NOPE], dwq[0, NOPE:QK_DIM]
        gsm['k_nope_norm_w'][l], gsm['k_pe_norm_w'][l] = dwk[0, :NOPE], dwk[0, NOPE:QK_DIM]
        dqa_n = _mm(dq, lw['w_q_up'], 'nt', f"mm_q_dx{l}")
        gfull['w_q_up'][l] = _cols_to_shards(_unpad_wq(_mm(t['qa_n'], dq, 'tn', f"mm_q_dw{l}")))
        dkva_n = _mm(dkv, lw['w_kv_up'], 'nt', f"mm_kv_dx{l}")
        gfull['w_kv_up'][l] = _cols_to_shards(_mm(t['kva_n'], dkv, 'tn', f"mm_kv_dw{l}"))
        dqa, dkva, dqw, dkvw = _row_bwd(fn_lat_norm, f"lat_norm_b{l}", [(proj, 0, 256), (proj, 2, 128)],
                                        [small('q_a_norm_w', l), small('kv_a_norm_w', l)],
                                        [(dqa_n, 0, 256), (dkva_n, 0, 128)], [0, 1], [0, 1])
        gsm['q_a_norm_w'][l], gsm['kv_a_norm_w'][l] = dqw[0], dkvw[0]
        dproj = jnp.concatenate([dqa, dkva, dkpe, dz, dxbc, ddt], axis=1).astype(BF16)
        dh1 = _mm(dproj, lw['w_in'], 'nt', f"mm_in_dx{l}")
        gfull['w_in'][l] = _cols_to_shards(_unpad_win(_mm(t['h1'], dproj, 'tn', f"mm_in_dw{l}")))
        if l > 0:
            p = sv[l - 1]
            dx1, dff, dmod[l - 1][5], gsm['norm1_w'][l], dmod[l][1], dmod[l][0] = _row_bwd(
                fn_resid_norm, f"resid_end_b{l - 1}", [(p['x1'], 0, D_MODEL), (p['ff'], 0, D_MODEL)],
                [mvec(l - 1, 5), small('norm1_w', l), mvec(l, 1), mvec(l, 0)], [(dxc, 0, D_MODEL), (dh1, 0, D_MODEL)],
                [0, 1], [0, 1, 2, 3], ddtypes=[F32, BF16])
            sib_l1 = _rs_sibling_start(halves_of(l, BIG), f"rs_sibling_start_l{l}")
            dff = dff + sib_l1[4][0, 0].astype(BF16)
        else:
            grad_x, gsm['norm1_w'][l], dmod[l][1], dmod[l][0] = _row_bwd(
                fn_norm_mod_pass, "norm_mod_b", [(x0, 0, D_MODEL)], [small('norm1_w', 0), mvec(0, 1), mvec(0, 0)],
                [(dxc, 0, D_MODEL), (dh1, 0, D_MODEL)], [0], [0, 1, 2])
        for n in ('norm1_w', 'norm2_w', 'ssd_norm_w'):
            gsm[n][l] = gsm[n][l][0]

    for l in range(DEPTH):
        gsm['b_ada'][l] = jnp.concatenate([d[0] for d in dmod[l]])
    sm_part = _pack_small({n: jnp.stack(v) for n, v in gsm.items()}).at[SMALL_ROWS - 1, 0].set(loss_acc[0, 0])
    sm_all = _ag8(sm_part, "ag_small")
    loss = jnp.sum(sm_all.reshape(8, SMALL_ROWS, LANES)[:, SMALL_ROWS - 1, 0])
    sm_all, late = lax.optimization_barrier((sm_all, [gfull[n][0] for n in BIG[2:]]))
    for n, g in zip(BIG[2:], late):
        gfull[n][0] = g
    late4 = halves_of(0, BIG[2:])
    tie_l0b = rs_finish(0, BIG[2:], "l0b", late4, _rs_sibling_list(late4, "rs_sibling_l0b"))

    def with_conv(tree):
        wide = lax.dynamic_update_slice_in_dim(jnp.zeros((DEPTH, CONV_TAPS, D_CONV), F32), tree['conv_w'], chip * 256, axis=2)
        return {**tree, 'conv_w': wide}

    g_sm, d_sm, m_sm, v_sm = _adam(_pack_small(with_conv(W)) + tie_l0b, _pack_small(with_conv(M)), _pack_small(with_conv(V)),
                                   [(sm_all, d * SMALL_ROWS) for d in range(8)], "adam_small")
    out_small = [_unpack_small(b) for b in (g_sm, d_sm, m_sm, v_sm)]
    for o in out_small:
        o['conv_w'] = lax.dynamic_slice_in_dim(o['conv_w'].reshape(DEPTH, CONV_TAPS, D_CONV), chip * 256, 256, axis=2)

    dmod_all = sm_all.reshape(8, SMALL_ROWS * LANES)
    per_layer = sum(k + (-k % LANES) for (_, k) in SMALL)
    dmod_sh = jnp.stack([lax.dynamic_slice_in_dim(dmod_all[:, l * per_layer:l * per_layer + 6 * D_MODEL], chip * 1536, 1536, axis=1)
                         for l in range(DEPTH)])
    ada = out_ada = _ada_bwd_adam(c_all.T, dmod_sh, W['w_ada'], M['w_ada'], V['w_ada'], "ada_bwd_adam")

    keys, cs_all, land_all = [], [], []
    for (l, names, (send_sems, recv_sems, cs_thru, land_thru, _)) in pending:
        cs, lands = _rs_chips_wait(send_sems, recv_sems, cs_thru, land_thru, ada[3], f"rs_chips_wait_l{l}{len(names)}")
        keys += [(l, n) for n in names]
        cs_all += list(cs)
        land_all += list(lands)
    gboth = _swap_list(_sum_chips(cs_all, land_all, chip, ci, "sum_chips"), "swap_halves")
    gshard = {k: g.reshape(2 * g.shape[1], g.shape[2]) for k, g in zip(keys, gboth)}

    def natural(n, a):
        return jnp.swapaxes(a, -1, -2) if n == 'w_in' else a

    res = _adam_multi([natural(n, W[n]) for n in BIG], [natural(n, M[n]) for n in BIG], [natural(n, V[n]) for n in BIG],
                      [[natural(n, gshard[(l, n)]) for l in range(DEPTH)] for n in BIG], "adam_big")
    out_big = [{n: natural(n, a) for n, a in zip(BIG, o)} for o in res]

    outs = [loss, grad_x[None]]
    for k in range(4):
        for n in WEIGHTS:
            if n == 'w_ada':
                outs.append(out_ada[k])
            elif n in BIG:
                outs.append(out_big[k][n])
            else:
                outs.append(out_small[k][n])
    return tuple(outs)
```

```python
import functools

import jax
import jax.numpy as jnp
from jax import lax
from jax.experimental import pallas as pl
from jax.experimental.pallas import tpu as pltpu

F32 = jnp.float32
BF16 = jnp.bfloat16
MESH = pl.DeviceIdType.MESH

D_MODEL = 1024
DEPTH = 2
HEADS = 8
NOPE = 64
ROPE = 32
QK_DIM = NOPE + ROPE
Q_LORA = 256
KV_LORA = 128
SSD_HEADS = 8
SSD_P = 64
SSD_N = 128
CHUNK = 256
CONV_TAPS = 4
D_SSD = 512
D_CONV = 1024
D_FF = 2816
D_IN = 1960
D_IN_PAD = 2176
EPS = 1e-6
ROPE_THETA = 10000.0
ATT_SCALE = QK_DIM ** -0.5
NEG = -1e30
LANES = 128
VMEM_LIMIT = 48 * 1024 * 1024

ADAM_LR, ADAM_B1, ADAM_B2, ADAM_EPS, ADAM_WD, ADAM_STEP = 0.001, 0.9, 0.999, 1e-08, 0.01, 10

WEIGHTS = ['norm1_w', 'norm2_w', 'w_ada', 'b_ada', 'w_in', 'q_a_norm_w', 'w_q_up', 'kv_a_norm_w', 'w_kv_up',
           'q_nope_norm_w', 'q_pe_norm_w', 'k_nope_norm_w', 'k_pe_norm_w', 'conv_w', 'conv_b', 'dt_bias', 'a_log',
           'd_skip', 'ssd_norm_w', 'w_out', 'w_gate_up', 'w_down']
BIG = ['w_down', 'w_gate_up', 'w_out', 'w_kv_up', 'w_q_up', 'w_in']
EARLY = BIG[:2]
FIRST = BIG[3:]
REST = BIG[:3]
SMALL = [('b_ada', 6144), ('conv_w', 4096), ('norm1_w', 1024), ('norm2_w', 1024), ('conv_b', 1024), ('ssd_norm_w', 512),
         ('q_a_norm_w', 256), ('kv_a_norm_w', 128), ('q_nope_norm_w', 64), ('q_pe_norm_w', 32),
         ('k_nope_norm_w', 64), ('k_pe_norm_w', 32), ('dt_bias', 8), ('a_log', 8), ('d_skip', 8)]
SMALL_ROWS = 240


def _cp(sem=None, **kw):
    return pltpu.CompilerParams(dimension_semantics=sem, vmem_limit_bytes=VMEM_LIMIT, **kw)


def _dot(a, b, dims, prec=None):
    return lax.dot_general(a, b, (dims, ((), ())), preferred_element_type=F32, precision=prec)


def _tile(dim, target):
    best = 0
    for t in range(LANES, min(dim, target) + 1, LANES):
        if dim % t == 0:
            best = t
    if best < 256 and dim <= 2304:
        return dim
    return best


def _mm(a, b, mode, name, out_dtype=F32, stack=None):
    ns = None
    halves = (a if mode == 'nt' else b).ndim == 3 and stack is not None and not (stack == 'b' and mode == 'nn')
    if stack == 'b':
        ns = b.shape[2]
        if mode == 'nn':
            (M, K), N = a.shape, 4 * ns
        else:
            M, K, N = a.shape[-2], 4 * ns, b.shape[1]
    elif mode == 'nn':
        (M, K), (_, N) = a.shape, b.shape
    elif mode == 'nt':
        (M, K), (N, _) = a.shape, b.shape
    else:
        (K, M), N = a.shape, (2 * b.shape[2] if halves else b.shape[1])
    if stack == 'out':
        ns = N // 4
    tm, tn, tk = _tile(M, 1408 if mode == 'tn' else 1024), _tile(N, 1408), _tile(K, 1408)
    if stack == 'b' and mode == 'nt':
        tk = ns
    elif stack is not None:
        tn = ns
    nk = K // tk
    dims = {'nn': ((1,), (0,)), 'nt': ((1,), (1,)), 'tn': ((0,), (0,))}[mode]

    def body(a_ref, b_ref, o_ref, *acc):
        part = _dot(a_ref[...].astype(BF16), b_ref[...].astype(BF16), dims)
        if nk == 1:
            o_ref[...] = part.astype(o_ref.dtype)
            return
        k = pl.program_id(2)

        @pl.when(k == 0)
        def _():
            acc[0][...] = part

        @pl.when(k > 0)
        def _():
            acc[0][...] += part

        @pl.when(k == nk - 1)
        def _():
            o_ref[...] = acc[0][...].astype(o_ref.dtype)

    a_spec = pl.BlockSpec((tk, tm), lambda i, j, k: (k, i)) if mode == 'tn' else pl.BlockSpec((tm, tk), lambda i, j, k: (i, k))
    b_spec = pl.BlockSpec((tn, tk), lambda i, j, k: (j, k)) if mode == 'nt' else pl.BlockSpec((tk, tn), lambda i, j, k: (k, j))
    o_spec, o_shape = pl.BlockSpec((tm, tn), lambda i, j, k: (i, j)), (M, N)
    if stack == 'b':
        b_spec = (pl.BlockSpec((None, tn, ns), lambda i, j, k: (k, j, 0)) if mode == 'nt'
                  else pl.BlockSpec((None, tk, ns), lambda i, j, k: (j, k, 0)))
    if stack == 'out':
        o_spec, o_shape = pl.BlockSpec((None, tm, ns), lambda i, j, k: (j, i, 0)), (4, M, ns)
    if halves and mode == 'nt':
        a_spec = pl.BlockSpec((None, tm, ns), lambda i, j, k: (lax.div(k, 2), i, lax.rem(k, 2)))
    if halves and mode == 'tn':
        b_spec = pl.BlockSpec((None, tk, ns), lambda i, j, k: (lax.div(j, 2), k, lax.rem(j, 2)))
    return pl.pallas_call(
        body, name=name, grid=(M // tm, N // tn, nk),
        in_specs=[a_spec, b_spec], out_specs=o_spec,
        out_shape=jax.ShapeDtypeStruct(o_shape, out_dtype),
        scratch_shapes=[pltpu.VMEM((tm, tn), F32)] if nk > 1 else [],
        compiler_params=_cp(("parallel", "parallel", "arbitrary")),
    )(a, b)


def _mm_gu_swiglu(h, wst, name):
    S, K = h.shape
    ns = wst.shape[2]
    tm = _tile(S, 512)

    def body(a_ref, bg_ref, bu_ref, gu_ref, act_ref):
        a = a_ref[...]
        g = _dot(a, bg_ref[...], ((1,), (0,)))
        u = _dot(a, bu_ref[...], ((1,), (0,)))
        gu_ref[0] = g
        gu_ref[1] = u
        act_ref[...] = (g * jax.nn.sigmoid(g) * u).astype(act_ref.dtype)

    return pl.pallas_call(
        body, name=name, grid=(S // tm, 2),
        in_specs=[pl.BlockSpec((tm, K), lambda i, j: (i, 0)), pl.BlockSpec((None, K, ns), lambda i, j: (j, 0, 0)),
                  pl.BlockSpec((None, K, ns), lambda i, j: (j + 2, 0, 0))],
        out_specs=[pl.BlockSpec((2, tm, ns), lambda i, j: (0, i, j)), pl.BlockSpec((tm, ns), lambda i, j: (i, j))],
        out_shape=[jax.ShapeDtypeStruct((2, S, 2 * ns), F32), jax.ShapeDtypeStruct((S, 2 * ns), BF16)],
        compiler_params=_cp(("parallel", "parallel")),
    )(h, wst, wst)


def _mm_down_dx_swiglu(dff, w_down, gu, name):
    S, K = dff.shape
    tm, tn = _tile(S, 512), _tile(D_FF, 1408)

    def body(a_ref, b_ref, g_ref, u_ref, o_ref):
        dact = _dot(a_ref[...].astype(BF16), b_ref[...], ((1,), (1,)))
        g, u = g_ref[...], u_ref[...]
        sg = jax.nn.sigmoid(g)
        o_ref[0] = (dact * u * (sg * (1.0 + g * (1.0 - sg)))).astype(o_ref.dtype)
        o_ref[1] = (dact * (g * sg)).astype(o_ref.dtype)

    return pl.pallas_call(
        body, name=name, grid=(S // tm, D_FF // tn),
        in_specs=[pl.BlockSpec((tm, K), lambda i, j: (i, 0)), pl.BlockSpec((tn, K), lambda i, j: (j, 0)),
                  pl.BlockSpec((None, tm, tn), lambda i, j: (0, i, j)), pl.BlockSpec((None, tm, tn), lambda i, j: (1, i, j))],
        out_specs=pl.BlockSpec((2, tm, tn), lambda i, j: (0, i, j)),
        out_shape=jax.ShapeDtypeStruct((2, S, D_FF), BF16),
        compiler_params=_cp(("parallel", "parallel")),
    )(dff, w_down, gu, gu)


def _rspec(tm, w, cb):
    return pl.BlockSpec((tm, w), lambda i: (i, cb))


def _vspec(shape):
    return pl.BlockSpec(shape, lambda i: (0,) * len(shape))


def _row_fwd(fn, name, rows, vecs, outs, tm=256, transposed=()):
    S = rows[0][0].shape[0]
    tm = min(tm, S)
    nin = len(rows) + len(vecs)

    def body(*refs):
        res = fn(*[r[...] for r in refs[:nin]])
        for k, (o_ref, r) in enumerate(zip(refs[nin:], res)):
            o_ref[...] = (r.T if k in transposed else r).astype(o_ref.dtype)

    return pl.pallas_call(
        body, name=name, grid=(S // tm,),
        in_specs=[_rspec(tm, w, cb) for (_, cb, w) in rows] + [_vspec(v.shape) for v in vecs],
        out_specs=[pl.BlockSpec((w, tm), lambda i: (0, i)) if k in transposed else _rspec(tm, w, 0)
                   for k, (w, _) in enumerate(outs)],
        out_shape=[jax.ShapeDtypeStruct((w, S) if k in transposed else (S, w), dt) for k, (w, dt) in enumerate(outs)],
        compiler_params=_cp(("parallel",)),
    )(*[r[0] for r in rows], *vecs)


def _row_bwd(fn, name, rows, vecs, cts, drows, dvecs, tm=256, ddtypes=None, transposed=()):
    S = rows[0][0].shape[0]
    ddtypes = ddtypes or [F32] * len(drows)
    tm = min(tm, S)
    nr, nv, nc = len(rows), len(vecs), len(cts)
    didx = list(drows) + [nr + j for j in dvecs]

    def body(*refs):
        vals = [r[...] for r in refs[:nr + nv]]
        ct = tuple((r[...].T if k in transposed else r[...]).astype(F32)
                   for k, r in enumerate(refs[nr + nv:nr + nv + nc]))
        outs = refs[nr + nv + nc:]

        def g(*d):
            a = list(vals)
            for k, val in zip(didx, d):
                a[k] = val
            return tuple(fn(*a))

        _, vjp = jax.vjp(g, *[vals[k] for k in didx])
        grads = vjp(ct)
        for o, gr in zip(outs[:len(drows)], grads[:len(drows)]):
            o[...] = gr.astype(o.dtype)

        @pl.when(pl.program_id(0) == 0)
        def _():
            for o in outs[len(drows):]:
                o[...] = jnp.zeros_like(o)

        for o, gr in zip(outs[len(drows):], grads[len(drows):]):
            o[...] += gr

    return pl.pallas_call(
        body, name=name, grid=(S // tm,),
        in_specs=[_rspec(tm, w, cb) for (_, cb, w) in rows] + [_vspec(v.shape) for v in vecs]
        + [pl.BlockSpec((w, tm), lambda i: (0, i)) if k in transposed else _rspec(tm, w, cb) for k, (_, cb, w) in enumerate(cts)],
        out_specs=[_rspec(tm, rows[k][2], 0) for k in drows] + [_vspec(vecs[j].shape) for j in dvecs],
        out_shape=[jax.ShapeDtypeStruct((S, rows[k][2]), dt) for k, dt in zip(drows, ddtypes)]
        + [jax.ShapeDtypeStruct(vecs[j].shape, F32) for j in dvecs],
        compiler_params=_cp(("arbitrary",)),
    )(*[r[0] for r in rows], *vecs, *[c[0] for c in cts])


def _rms(x):
    return x * lax.rsqrt(jnp.mean(x * x, axis=-1, keepdims=True) + EPS)


def fn_norm_mod(x, nw, sc, sh):
    return (_rms(x) * nw * (1.0 + sc) + sh,)


def fn_norm_mod_pass(x, nw, sc, sh):
    return (x, _rms(x) * nw * (1.0 + sc) + sh)


def fn_resid_norm(x, d, g, nw, sc, sh):
    xn = x + g * d
    return (xn, _rms(xn) * nw * (1.0 + sc) + sh)


def fn_lat_norm(qa, kva, qw, kvw):
    return (_rms(qa) * qw, _rms(kva) * kvw)


@functools.partial(jax.custom_vjp, nondiff_argnums=(1,))
def _lroll(x, s):
    return pltpu.roll(x, s, 1)


def _lroll_fwd(x, s):
    return pltpu.roll(x, s, 1), None


def _lroll_bwd(s, _, g):
    return (pltpu.roll(g, (LANES - s) % LANES, 1),)


_lroll.defvjp(_lroll_fwd, _lroll_bwd)


def _lane_masks(shape):
    lane = lax.broadcasted_iota(jnp.int32, shape, 1)
    return (lane < NOPE).astype(F32), ((lane >= NOPE) & (lane < QK_DIM)).astype(F32)


def _rope(t, tc, ts1, ts2):
    return t * tc + _lroll(t, 16) * ts1 + _lroll(t, LANES - 16) * ts2


def fn_qk_prep(q, kv, kpe, tc, ts1, ts2, wq, wk):
    mn, mp = _lane_masks((1, LANES))
    mhi = 1.0 - mn

    def head_norm(t, w):
        rn = lax.rsqrt(jnp.sum(t * t * mn, axis=-1, keepdims=True) * (1.0 / NOPE) + EPS)
        rp = lax.rsqrt(jnp.sum(t * t * mp, axis=-1, keepdims=True) * (1.0 / ROPE) + EPS)
        return t * (rn * mn + rp * mp) * w

    kp = _rope(head_norm(_lroll(kpe, NOPE), wk) * mp, tc, ts1, ts2)
    qs, ks, vs = [], [], []
    for h in range(HEADS):
        qs.append(_rope(head_norm(q[:, h * LANES:(h + 1) * LANES], wq), tc, ts1, ts2))
        t = kv[:, h * LANES:(h + 1) * LANES]
        ks.append(head_norm(t, wk) * mn + kp)
        vs.append(_lroll(t, NOPE) * mn + mhi)
    return (jnp.concatenate(qs, axis=1), jnp.concatenate(ks, axis=1), jnp.concatenate(vs, axis=1))


def fn_qk_prep_kt(*args):
    qf, kf, va = fn_qk_prep(*args)
    return (qf, kf, va, kf)


def fn_gated_norm(y, z, w):
    u = y * jax.nn.silu(z)
    half = D_SSD // 2
    return (jnp.concatenate([_rms(u[:, :half]), _rms(u[:, half:])], axis=1) * w,)


def fn_gated_mix(y, z, ao, w):
    return (jnp.concatenate([ao, fn_gated_norm(y, z, w)[0]], axis=1),)


def _final(x1, ff, tgt, g2, name):
    S = x1.shape[0]
    tm = min(256, S)

    def body(x_ref, f_ref, t_ref, g_ref, dx_ref, df_ref, dg_ref, l_ref):
        @pl.when(pl.program_id(0) == 0)
        def _():
            dg_ref[...] = jnp.zeros_like(dg_ref)
            l_ref[...] = jnp.zeros_like(l_ref)

        f = f_ref[...]
        g = g_ref[...]
        e = x_ref[...] + g * f - t_ref[...]
        dx = e * (1.0 / D_MODEL)
        dx_ref[...] = dx
        df_ref[...] = (g * dx).astype(df_ref.dtype)
        dg_ref[...] += jnp.sum(dx * f, axis=0, keepdims=True)
        l_ref[...] += jnp.sum(e * e) * (0.5 / D_MODEL)

    r = _rspec(tm, D_MODEL, 0)
    return pl.pallas_call(
        body, name=name, grid=(S // tm,),
        in_specs=[r, r, r, _vspec((1, D_MODEL))],
        out_specs=[r, r, _vspec((1, D_MODEL)), _vspec((1, LANES))],
        out_shape=[jax.ShapeDtypeStruct((S, D_MODEL), F32), jax.ShapeDtypeStruct((S, D_MODEL), BF16),
                   jax.ShapeDtypeStruct((1, D_MODEL), F32), jax.ShapeDtypeStruct((1, LANES), F32)],
        compiler_params=_cp(("arbitrary",)),
    )(x1, ff, tgt, g2)


def _causal_mask(t):
    r = lax.broadcasted_iota(jnp.int32, (t, t), 0)
    c = lax.broadcasted_iota(jnp.int32, (t, t), 1)
    return c <= r


LOG2E = 1.4426950408889634
EXP2_SCALE = ATT_SCALE * LOG2E
ATT_TQ, ATT_TK = 1024, 1024
ATT_BQ, ATT_BK = 1024, 1024


def _attn_fwd(qf, kf, va, name):
    S = qf.shape[0]
    T, TK = min(ATT_TQ, S), min(ATT_TK, S)
    nmask = max(1, T // TK)

    def body(q_ref, k_ref, v_ref, o_ref, l_ref):
        i = pl.program_id(1)
        r = lax.broadcasted_iota(jnp.int32, (T, TK), 0)
        c = lax.broadcasted_iota(jnp.int32, (T, TK), 1)
        qs = [q_ref[:, hh * LANES:(hh + 1) * LANES] for hh in range(2)]

        def blk(j, carry, masked):
            off = pl.multiple_of(j * TK, TK)
            out = []
            for hh in range(2):
                m, acc = carry[hh]
                s = _dot(qs[hh], k_ref[pl.ds(off, TK), hh * LANES:(hh + 1) * LANES], ((1,), (1,)))
                if masked:
                    s = jnp.where(c + j * TK <= r + i * T, s, NEG)
                mn = jnp.maximum(m, jnp.max(s, axis=1, keepdims=True))
                p = jnp.exp2((s - mn) * EXP2_SCALE)
                al = jnp.exp2((m - mn) * EXP2_SCALE)
                vj = v_ref[pl.ds(off, TK), hh * LANES:(hh + 1) * LANES]
                out.append((mn, al * acc + _dot(p.astype(BF16), vj, ((1,), (0,)))))
            return tuple(out)

        one = (jnp.full((T, 1), NEG, F32), jnp.zeros((T, LANES), F32))
        nfull = lax.div(i * T, TK)
        carry = lax.fori_loop(0, nfull, lambda j, cr: blk(j, cr, False), (one, one))
        for t in range(nmask):
            carry = blk(nfull + t, carry, True)
        lane = lax.broadcasted_iota(jnp.int32, (1, LANES), 1)
        z = jnp.zeros((T, LANES), F32)
        for hh in range(2):
            m, acc = carry[hh]
            l = acc[:, 64:65]
            o_ref[:, hh * 64:(hh + 1) * 64] = (acc / l)[:, :64]
            z = z + (m * EXP2_SCALE + jnp.log(l) * LOG2E) * (lane == hh).astype(F32)
        l_ref[0] = z.T[0:2, :]

    return pl.pallas_call(
        body, name=name, grid=(HEADS // 2, S // T),
        in_specs=[pl.BlockSpec((T, 256), lambda h, i: (i, h)), pl.BlockSpec((S, 256), lambda h, i: (0, h)),
                  pl.BlockSpec((S, 256), lambda h, i: (0, h))],
        out_specs=[pl.BlockSpec((T, LANES), lambda h, i: (i, h)), pl.BlockSpec((1, 2, T), lambda h, i: (h, 0, i))],
        out_shape=[jax.ShapeDtypeStruct((S, D_SSD), F32), jax.ShapeDtypeStruct((HEADS // 2, 2, S), F32)],
        compiler_params=_cp(("parallel", "parallel")),
    )(qf, kf, va)


def _attn_delta(dmix, ao, name):
    S = ao.shape[0]
    tm = min(512, S)

    def body(d_ref, o_ref, out_ref):
        lane = lax.broadcasted_iota(jnp.int32, (1, LANES), 1)
        lo = (lane < 64).astype(F32)
        for hp in range(HEADS // 2):
            y = d_ref[:, hp * LANES:(hp + 1) * LANES] * o_ref[:, hp * LANES:(hp + 1) * LANES]
            z = (jnp.sum(y * lo, axis=1, keepdims=True) * (lane == 0).astype(F32)
                 + jnp.sum(y * (1.0 - lo), axis=1, keepdims=True) * (lane == 1).astype(F32))
            out_ref[hp] = z.T[0:2, :]

    return pl.pallas_call(
        body, name=name, grid=(S // tm,),
        in_specs=[pl.BlockSpec((tm, D_SSD), lambda i: (i, 0)), pl.BlockSpec((tm, D_SSD), lambda i: (i, 0))],
        out_specs=pl.BlockSpec((HEADS // 2, 2, tm), lambda i: (0, 0, i)),
        out_shape=jax.ShapeDtypeStruct((HEADS // 2, 2, S), F32),
        compiler_params=_cp(("parallel",)),
    )(dmix, ao)


def _attn_bwd(qf, kf, kT, va, do, lse_r, delta_r, name):
    S = qf.shape[0]
    T, TK = min(ATT_BQ, S), min(ATT_BK, S)
    nq = S // T
    nmask = max(1, TK // T)

    def body(q_ref, k_ref, kT_ref, v_ref, do_ref, l_ref, d_ref, dqT_ref, dk_ref, dv_ref):
        j = pl.program_id(1)

        @pl.when(j == 0)
        def _():
            dqT_ref[...] = jnp.zeros_like(dqT_ref)

        r = lax.broadcasted_iota(jnp.int32, (TK, T), 0)
        c = lax.broadcasted_iota(jnp.int32, (TK, T), 1)
        lo = (lax.broadcasted_iota(jnp.int32, (1, LANES), 1) < 64).astype(F32)
        ks = [k_ref[:, hh * LANES:(hh + 1) * LANES] for hh in range(2)]
        vs = [v_ref[:, hh * LANES:(hh + 1) * LANES] for hh in range(2)]
        kTs = [kT_ref[hh * LANES:(hh + 1) * LANES, :] for hh in range(2)]

        def blk(i, carry, masked):
            off = pl.multiple_of(i * T, T)
            dall = do_ref[pl.ds(off, T), :]
            out = []
            for hh in range(2):
                dk, dv = carry[hh]
                q = q_ref[pl.ds(off, T), hh * LANES:(hh + 1) * LANES]
                dop = ((dall if hh == 0 else pltpu.roll(dall, 64, 1)) * lo).astype(BF16)
                lrow = l_ref[0, hh:hh + 1, pl.ds(off, T)]
                drow = d_ref[0, hh:hh + 1, pl.ds(off, T)]
                pT = jnp.exp2(_dot(ks[hh], q, ((1,), (1,))) * EXP2_SCALE - lrow)
                if masked:
                    pT = jnp.where(r + j * TK <= c + i * T, pT, 0.0)
                dpT = _dot(vs[hh], dop, ((1,), (1,)))
                dsT = (pT * (dpT - drow) * ATT_SCALE).astype(BF16)
                dv = dv + _dot(pT.astype(BF16), dop, ((1,), (0,)))
                dk = dk + _dot(dsT, q, ((1,), (0,)))
                dqT_ref[hh * LANES:(hh + 1) * LANES, pl.ds(off, T)] += _dot(kTs[hh], dsT, ((1,), (0,)))
                out.append((dk, dv))
            return tuple(out)

        z = (jnp.zeros((TK, LANES), F32), jnp.zeros((TK, LANES), F32))
        first = lax.div(j * TK, T)
        carry = (z, z)
        for t in range(nmask):
            carry = blk(first + t, carry, True)
        carry = lax.fori_loop(first + nmask, nq, lambda i, cr: blk(i, cr, False), carry)
        for hh in range(2):
            dk_ref[:, hh * LANES:(hh + 1) * LANES] = carry[hh][0]
            dv_ref[:, hh * LANES:(hh + 1) * LANES] = carry[hh][1]

    return pl.pallas_call(
        body, name=name, grid=(HEADS // 2, S // TK),
        in_specs=[pl.BlockSpec((S, 256), lambda h, j: (0, h)), pl.BlockSpec((TK, 256), lambda h, j: (j, h)),
                  pl.BlockSpec((256, TK), lambda h, j: (h, j)), pl.BlockSpec((TK, 256), lambda h, j: (j, h)),
                  pl.BlockSpec((S, LANES), lambda h, j: (0, h)), pl.BlockSpec((1, 2, S), lambda h, j: (h, 0, 0)),
                  pl.BlockSpec((1, 2, S), lambda h, j: (h, 0, 0))],
        out_specs=[pl.BlockSpec((256, S), lambda h, j: (h, 0)), pl.BlockSpec((TK, 256), lambda h, j: (j, h)),
                   pl.BlockSpec((TK, 256), lambda h, j: (j, h))],
        out_shape=[jax.ShapeDtypeStruct((D_MODEL, S), F32), jax.ShapeDtypeStruct((S, D_MODEL), F32),
                   jax.ShapeDtypeStruct((S, D_MODEL), F32)],
        compiler_params=_cp(("parallel", "arbitrary")),
    )(qf, kf, kT, va, do, lse_r, delta_r)


def _shift_down(x, s):
    if s == 0:
        return x
    rows = lax.broadcasted_iota(jnp.int32, x.shape, 0)
    return jnp.where(rows >= s, pltpu.roll(x, s, 0), 0.0)


def _shift_up(x, s):
    if s == 0:
        return x
    n = x.shape[0]
    rows = lax.broadcasted_iota(jnp.int32, x.shape, 0)
    return jnp.where(rows < n - s, pltpu.roll(x, n - s, 0), 0.0)


def _conv_fwd(proj, cvec, name):
    S = proj.shape[0]

    def body(x_ref, c_ref, o_ref):
        x = x_ref[...]
        y = jnp.broadcast_to(c_ref[4:5, :], x.shape)
        for k in range(CONV_TAPS):
            y = y + c_ref[k:k + 1, :] * _shift_down(x, CONV_TAPS - 1 - k)
        o_ref[...] = y * jax.nn.sigmoid(y)

    return pl.pallas_call(
        body, name=name, grid=(D_CONV // LANES,),
        in_specs=[pl.BlockSpec((S, LANES), lambda j: (0, 8 + j)), pl.BlockSpec((8, LANES), lambda j: (0, j))],
        out_specs=pl.BlockSpec((S, LANES), lambda j: (0, j)),
        out_shape=jax.ShapeDtypeStruct((S, D_CONV), F32),
        compiler_params=_cp(("parallel",)),
    )(proj, cvec)


def _conv_bwd(proj, cvec, dact, name):
    S = proj.shape[0]

    def body(x_ref, c_ref, d_ref, dx_ref, dc_ref):
        x = x_ref[...]
        y = jnp.broadcast_to(c_ref[4:5, :], x.shape)
        for k in range(CONV_TAPS):
            y = y + c_ref[k:k + 1, :] * _shift_down(x, CONV_TAPS - 1 - k)
        sg = jax.nn.sigmoid(y)
        dy = d_ref[...] * (sg * (1.0 + y * (1.0 - sg)))
        dx = jnp.zeros_like(x)
        for k in range(CONV_TAPS):
            s = CONV_TAPS - 1 - k
            dx = dx + c_ref[k:k + 1, :] * _shift_up(dy, s)
            dc_ref[k:k + 1, :] = jnp.sum(dy * _shift_down(x, s), axis=0, keepdims=True)
        dx_ref[...] = dx
        dc_ref[4:5, :] = jnp.sum(dy, axis=0, keepdims=True)
        dc_ref[5:8, :] = jnp.zeros((3, LANES), F32)

    return pl.pallas_call(
        body, name=name, grid=(D_CONV // LANES,),
        in_specs=[pl.BlockSpec((S, LANES), lambda j: (0, 8 + j)), pl.BlockSpec((8, LANES), lambda j: (0, j)),
                  pl.BlockSpec((S, LANES), lambda j: (0, j))],
        out_specs=[pl.BlockSpec((S, LANES), lambda j: (0, j)), pl.BlockSpec((8, LANES), lambda j: (0, j))],
        out_shape=[jax.ShapeDtypeStruct((S, D_CONV), F32), jax.ShapeDtypeStruct((8, D_CONV), F32)],
        compiler_params=_cp(("parallel",)),
    )(proj, cvec, dact)


def fn_ssd_chunk(xs, bm, cm, dtr, state, vecs):
    Q = CHUNK
    dt = jax.nn.softplus(dtr + vecs[0:1])
    a = -jnp.exp(vecs[1:2])
    adt = dt * a
    tril = _causal_mask(Q)
    acs = _dot(tril.astype(F32), adt, ((1,), (0,)), lax.Precision.HIGHEST)
    acs_t = acs.T
    alast = acs[Q - 1:Q, :]
    r = lax.broadcasted_iota(jnp.int32, (LANES, D_SSD), 0)
    c = lax.broadcasted_iota(jnp.int32, (LANES, D_SSD), 1)
    spread = (lax.shift_right_logical(c, 6) == r).astype(F32)

    def per_head(v):
        return _dot(v, spread, ((1,), (0,)), lax.Precision.HIGH)

    xdt = xs * per_head(dt)
    ub = (xdt * per_head(jnp.exp(alast - acs))).astype(BF16)
    xdtb = xdt.astype(BF16)
    Bs = [bm[:, g * SSD_N:(g + 1) * SSD_N].astype(BF16) for g in range(2)]
    Cs = [cm[:, g * SSD_N:(g + 1) * SSD_N].astype(BF16) for g in range(2)]
    Gs = [_dot(Cs[g], Bs[g], ((1,), (1,))) for g in range(2)]
    yds, yos, adds = [], [], []
    for h in range(SSD_HEADS):
        g = h // (SSD_HEADS // 2)
        sl = slice(h * SSD_P, (h + 1) * SSD_P)
        L = jnp.exp(jnp.where(tril, acs[:, h:h + 1] - acs_t[h:h + 1, :], -jnp.inf))
        yds.append(_dot((Gs[g] * L).astype(BF16), xdtb[:, sl], ((1,), (0,))))
        yos.append(_dot(Cs[g], state[h].astype(BF16), ((1,), (1,))))
        adds.append(_dot(ub[:, sl], Bs[g], ((0,), (0,))))
    y = jnp.concatenate(yds, axis=1) + jnp.concatenate(yos, axis=1) * per_head(jnp.exp(acs)) + per_head(vecs[2:3]) * xs
    decay = jnp.stack([jnp.broadcast_to(jnp.exp(alast[:, h:h + 1]), (SSD_P, SSD_N)) for h in range(SSD_HEADS)])
    return y, jnp.stack(adds) + state * decay


def _ssd_fwd(xact, proj, svec, name):
    S = xact.shape[0]
    nc = S // CHUNK

    def body(x_ref, dt_ref, v_ref, y_ref, st_ref, state):
        @pl.when(pl.program_id(0) == 0)
        def _():
            state[...] = jnp.zeros_like(state)

        st_ref[0] = state[...]
        x = x_ref[...]
        y, sn = fn_ssd_chunk(x[:, 0:512], x[:, 512:768], x[:, 768:1024], dt_ref[...], state[...], v_ref[...])
        y_ref[...] = y
        state[...] = sn

    return pl.pallas_call(
        body, name=name, grid=(nc,),
        in_specs=[pl.BlockSpec((CHUNK, D_CONV), lambda i: (i, 0)), pl.BlockSpec((CHUNK, LANES), lambda i: (i, 16)),
                  pl.BlockSpec((8, LANES), lambda i: (0, 0))],
        out_specs=[pl.BlockSpec((CHUNK, D_SSD), lambda i: (i, 0)),
                   pl.BlockSpec((1, SSD_HEADS, SSD_P, SSD_N), lambda i: (i, 0, 0, 0))],
        out_shape=[jax.ShapeDtypeStruct((S, D_SSD), F32), jax.ShapeDtypeStruct((nc, SSD_HEADS, SSD_P, SSD_N), F32)],
        scratch_shapes=[pltpu.VMEM((SSD_HEADS, SSD_P, SSD_N), F32)],
        compiler_params=_cp(("arbitrary",)),
    )(xact, proj, svec)


def _ssd_bwd(xact, proj, svec, states, dy, name):
    S = xact.shape[0]
    nc = S // CHUNK

    def body(x_ref, dt_ref, v_ref, st_ref, dy_ref, dx_ref, ddt_ref, dv_ref, dstate):
        @pl.when(pl.program_id(0) == 0)
        def _():
            dstate[...] = jnp.zeros_like(dstate)
            dv_ref[...] = jnp.zeros_like(dv_ref)

        x = x_ref[...]
        _, vjp = jax.vjp(fn_ssd_chunk, x[:, 0:512], x[:, 512:768], x[:, 768:1024], dt_ref[...], st_ref[0], v_ref[...])
        dxs, dbm, dcm, ddt, dst, dvec = vjp((dy_ref[...], dstate[...]))
        dx_ref[:, 0:512] = dxs
        dx_ref[:, 512:768] = dbm
        dx_ref[:, 768:1024] = dcm
        ddt_ref[...] = ddt
        dstate[...] = dst
        dv_ref[...] += dvec

    rev = lambda i: (nc - 1 - i, 0)
    return pl.pallas_call(
        body, name=name, grid=(nc,),
        in_specs=[pl.BlockSpec((CHUNK, D_CONV), rev), pl.BlockSpec((CHUNK, LANES), lambda i: (nc - 1 - i, 16)),
                  pl.BlockSpec((8, LANES), lambda i: (0, 0)),
                  pl.BlockSpec((1, SSD_HEADS, SSD_P, SSD_N), lambda i: (nc - 1 - i, 0, 0, 0)),
                  pl.BlockSpec((CHUNK, D_SSD), rev)],
        out_specs=[pl.BlockSpec((CHUNK, D_CONV), rev), pl.BlockSpec((CHUNK, LANES), rev),
                   pl.BlockSpec((8, LANES), lambda i: (0, 0))],
        out_shape=[jax.ShapeDtypeStruct((S, D_CONV), F32), jax.ShapeDtypeStruct((S, LANES), F32),
                   jax.ShapeDtypeStruct((8, LANES), F32)],
        scratch_shapes=[pltpu.VMEM((SSD_HEADS, SSD_P, SSD_N), F32)],
        compiler_params=_cp(("arbitrary",)),
    )(xact, proj, svec, states, dy)


def _ada_fwd(c_all, w_ada, b_sh, name):
    nb = 1536 // 512

    def body(c_ref, w_ref, b_ref, o_ref):
        ca = jax.nn.silu(c_ref[...]).astype(BF16)
        o_ref[0] = _dot(ca, w_ref[0].astype(BF16), ((1,), (0,))) + b_ref[0]

    return pl.pallas_call(
        body, name=name, grid=(DEPTH, nb),
        in_specs=[pl.BlockSpec((8, D_MODEL), lambda l, j: (0, 0)), pl.BlockSpec((1, D_MODEL, 512), lambda l, j: (l, 0, j)),
                  pl.BlockSpec((1, 1, 512), lambda l, j: (l, 0, j))],
        out_specs=pl.BlockSpec((1, 8, 512), lambda l, j: (l, 0, j)),
        out_shape=jax.ShapeDtypeStruct((DEPTH, 8, 1536), F32),
        compiler_params=_cp(("parallel", "parallel")),
    )(c_all, w_ada, b_sh)


def _ada_bwd_adam(c_all_t, dmod_sh, w, m, v, name):
    nb = 1536 // 512

    def body(c_ref, d_ref, w_ref, m_ref, v_ref, g_ref, dl_ref, nm_ref, nv_ref):
        ca = jax.nn.silu(c_ref[...])
        g = ca[:, 0:1] * d_ref[0, 0:1, :]
        for b in range(1, 8):
            g = g + ca[:, b:b + 1] * d_ref[0, b:b + 1, :]
        g_ref[0] = g
        dl_ref[0], nm_ref[0], nv_ref[0] = _adam_update(w_ref[0], m_ref[0], v_ref[0], g)

    blk = pl.BlockSpec((1, D_MODEL, 512), lambda l, j: (l, 0, j))
    return pl.pallas_call(
        body, name=name, grid=(DEPTH, nb),
        in_specs=[pl.BlockSpec((D_MODEL, 8), lambda l, j: (0, 0)), pl.BlockSpec((1, 8, 512), lambda l, j: (l, 0, j)), blk, blk, blk],
        out_specs=[blk] * 4, out_shape=[jax.ShapeDtypeStruct((DEPTH, D_MODEL, 1536), F32)] * 4,
        compiler_params=_cp(("parallel", "parallel")),
    )(c_all_t, dmod_sh, w, m, v)


def _rows_tile(rows):
    return next(t for t in (512, 256, 128, 64, 32, 16, 8) if rows % t == 0)


SUM_BLOCKS = 4
ADAM_BLOCKS = 8


def _sum_sibling(gs, ls, ci, name):
    n = len(gs)

    def body(c_ref, *refs):
        for p in range(n):
            refs[2 * n + p][...] = refs[2 * p][...] + refs[2 * p + 1][...]

    in_specs, out_specs, out_shape = [], [], []
    for g in gs:
        _, _, rh, cw = g.shape
        rb = rh // SUM_BLOCKS
        in_specs += [pl.BlockSpec((None, None, rb, cw), lambda s, i, c: (s, c[0], i, 0)),
                     pl.BlockSpec((None, rb, cw), lambda s, i, c: (s, i, 0))]
        out_specs.append(pl.BlockSpec((None, rb, cw), lambda s, i, c: (s, i, 0)))
        out_shape.append(jax.ShapeDtypeStruct((4, rh, cw), F32))
    ops = [a for pair in zip(gs, ls) for a in pair]
    return pl.pallas_call(
        body, name=name,
        grid_spec=pltpu.PrefetchScalarGridSpec(num_scalar_prefetch=1, grid=(4, SUM_BLOCKS), in_specs=in_specs, out_specs=out_specs),
        out_shape=out_shape, compiler_params=_cp(("parallel", "parallel")),
    )(ci.reshape(1).astype(jnp.int32), *ops)


def _sum_chips(cs, lands, chip, ci, name):
    n = len(cs)

    def body(c_ref, *refs):
        for p in range(n):
            a = refs[4 * p:4 * p + 4]
            refs[4 * n + p][...] = ((a[0][...] + a[1][...]) + a[2][...]) + a[3][...]

    in_specs, out_specs, out_shape = [], [], []
    for c in cs:
        _, rh, cw = c.shape
        rb = rh // SUM_BLOCKS
        in_specs.append(pl.BlockSpec((None, rb, cw), lambda i, ch: (ch[0], i, 0)))
        in_specs += [pl.BlockSpec((None, rb, cw), functools.partial(lambda i, ch, k: (k, i, 0), k=k)) for k in range(3)]
        out_specs.append(pl.BlockSpec((None, rb, cw), lambda i, ch: (ch[1], i, 0)))
        out_shape.append(jax.ShapeDtypeStruct((2, rh, cw), F32))
    ops = [a for c, l in zip(cs, lands) for a in (c, l, l, l)]
    return pl.pallas_call(
        body, name=name,
        grid_spec=pltpu.PrefetchScalarGridSpec(num_scalar_prefetch=1, grid=(SUM_BLOCKS,), in_specs=in_specs, out_specs=out_specs),
        out_shape=out_shape, compiler_params=_cp(("parallel",)),
    )(jnp.stack([chip, ci]).astype(jnp.int32), *ops)


def _adam_update(w, m, v, g):
    c1 = 1.0 / (1.0 - ADAM_B1 ** ADAM_STEP)
    c2 = 1.0 / (1.0 - ADAM_B2 ** ADAM_STEP)
    nm = ADAM_B1 * m + (1.0 - ADAM_B1) * g
    nv = ADAM_B2 * v + (1.0 - ADAM_B2) * (g * g)
    return -ADAM_LR * ((nm * c1) / (jnp.sqrt(nv * c2) + ADAM_EPS) + ADAM_WD * w), nm, nv


def _adam_multi(ws, ms, vs, gs, name):
    n = len(ws)
    per = 3 + DEPTH

    def body(*refs):
        layer = pl.program_id(0)
        for p in range(n):
            w, m, v = [refs[per * p + k][...] for k in range(3)]
            g = refs[per * p + 3][...]
            for l in range(1, DEPTH):
                g = jnp.where(layer == l, refs[per * p + 3 + l][...], g)
            d, nm, nv = _adam_update(w, m, v, g)
            for k, val in enumerate((g, d, nm, nv)):
                refs[per * n + 4 * p + k][...] = val

    in_specs, out_specs, out_shape = [], [], []
    for w in ws:
        _, r, cw = w.shape
        if r % (8 * ADAM_BLOCKS) == 0:
            spec = pl.BlockSpec((None, r // ADAM_BLOCKS, cw), lambda l, i: (l, i, 0))
            gspec = pl.BlockSpec((r // ADAM_BLOCKS, cw), lambda l, i: (i, 0))
        else:
            spec = pl.BlockSpec((None, r, cw // ADAM_BLOCKS), lambda l, i: (l, 0, i))
            gspec = pl.BlockSpec((r, cw // ADAM_BLOCKS), lambda l, i: (0, i))
        in_specs += [spec] * 3 + [gspec] * DEPTH
        out_specs += [spec] * 4
        out_shape += [jax.ShapeDtypeStruct(w.shape, F32)] * 4
    ops = [a for w, m, v, g in zip(ws, ms, vs, gs) for a in (w, m, v, *g)]
    res = pl.pallas_call(
        body, name=name, grid=(DEPTH, ADAM_BLOCKS), in_specs=in_specs, out_specs=out_specs, out_shape=out_shape,
        compiler_params=_cp(("parallel", "parallel")),
    )(*ops)
    return res[0::4], res[1::4], res[2::4], res[3::4]


def _adam(w, m, v, parts, name):
    rows, width = w.shape
    bm = min(256, _rows_tile(rows))
    np_ = len(parts)
    c1 = 1.0 / (1.0 - ADAM_B1 ** ADAM_STEP)
    c2 = 1.0 / (1.0 - ADAM_B2 ** ADAM_STEP)

    def body(*refs):
        w_ref, m_ref, v_ref = refs[:3]
        g = refs[3][...]
        for r in refs[4:3 + np_]:
            g = g + r[...]
        g_ref, d_ref, nm_ref, nv_ref = refs[3 + np_:]
        nm = ADAM_B1 * m_ref[...] + (1.0 - ADAM_B1) * g
        nv = ADAM_B2 * v_ref[...] + (1.0 - ADAM_B2) * (g * g)
        g_ref[...] = g
        nm_ref[...] = nm
        nv_ref[...] = nv
        d_ref[...] = -ADAM_LR * ((nm * c1) / (jnp.sqrt(nv * c2) + ADAM_EPS) + ADAM_WD * w_ref[...])

    blk = pl.BlockSpec((bm, width), lambda i: (i, 0))
    return pl.pallas_call(
        body, name=name, grid=(rows // bm,),
        in_specs=[blk, blk, blk] + [pl.BlockSpec((bm, width), functools.partial(lambda i, o: (i + o, 0), o=off // bm))
                                    for (_, off) in parts],
        out_specs=[blk, blk, blk, blk],
        out_shape=[jax.ShapeDtypeStruct((rows, width), F32)] * 4,
        compiler_params=_cp(("parallel",)),
    )(w, m, v, *[p[0] for p in parts])


def _coords():
    return lax.axis_index("x"), lax.axis_index("y"), lax.axis_index("c")


def _other_chips(x, y):
    return [(1 - x, y), (x, 1 - y), (1 - x, 1 - y)]


def _ag8(blk, name):
    m_per, n = blk.shape

    def body(x_ref, out_ref, send_sems, recv_sems, local_sem):
        x, y, c = _coords()
        me, sibling = (x, y, c), (x, y, 1 - c)
        chips = _other_chips(x, y)

        def rows(px, py, pc):
            return out_ref.at[pl.ds((4 * px + 2 * py + pc) * m_per, m_per), :]

        def copy(k, block, to, src=None):
            return pltpu.make_async_remote_copy(
                src_ref=rows(*block) if src is None else src, dst_ref=rows(*block),
                send_sem=send_sems.at[k], recv_sem=recv_sems.at[k], device_id=to, device_id_type=MESH)

        mine = pltpu.make_async_copy(x_ref, rows(*me), local_sem)
        mine.start()
        first = [copy(0, me, sibling, src=x_ref)]
        first += [copy(1 + j, me, (*chip, c), src=x_ref) for j, chip in enumerate(chips)]
        for cp in first:
            cp.start()
        passed = [copy(4 + j, (*chip, c), sibling) for j, chip in enumerate(chips)]
        for j, chip in enumerate(chips):
            copy(1 + j, (*chip, c), me).wait_recv()
            passed[j].start()
        copy(0, sibling, me).wait_recv()
        for j, chip in enumerate(chips):
            copy(4 + j, (*chip, 1 - c), me).wait_recv()
        for cp in first + passed:
            cp.wait_send()
        mine.wait()

    return pl.pallas_call(
        body, name=name,
        out_shape=jax.ShapeDtypeStruct((8 * m_per, n), blk.dtype),
        in_specs=[pl.BlockSpec(memory_space=pltpu.VMEM)], out_specs=pl.BlockSpec(memory_space=pltpu.VMEM),
        scratch_shapes=[pltpu.SemaphoreType.DMA((7,)), pltpu.SemaphoreType.DMA((7,)), pltpu.SemaphoreType.DMA],
    )(blk)


HBM_SPEC = pl.BlockSpec(memory_space=pltpu.HBM)
SEM_SPEC = pl.BlockSpec(memory_space=pltpu.SEMAPHORE)
EFFECT = pltpu.SideEffectType.DATAFLOW_SIDE_EFFECTING


def _remote(src, dst, send_sem, recv_sem, to):
    return pltpu.make_async_remote_copy(src_ref=src, dst_ref=dst, send_sem=send_sem, recv_sem=recv_sem,
                                        device_id=to, device_id_type=MESH)


def _ag_list(shards, name):
    n = len(shards)

    def body(*refs):
        sh, out = refs[:n], refs[n:2 * n]
        send_sems, recv_sems = refs[2 * n:]
        x, y, c = _coords()
        sibling = (x, y, 1 - c)
        chips = _other_chips(x, y)
        first = [_remote(sh[p].at[c], out[p].at[2 * x + y, c], send_sems.at[6 * p + j], recv_sems.at[6 * p + j], (px, py, c))
                 for p in range(n) for j, (px, py) in enumerate(chips)]
        for cp in first:
            cp.start()
        passed = []
        for j, (px, py) in enumerate(chips):
            for p in range(n):
                got = out[p].at[2 * px + py, c]
                _remote(got, got, send_sems.at[6 * p + j], recv_sems.at[6 * p + j], (x, y, c)).wait_recv()
                cp = _remote(got, got, send_sems.at[6 * p + 3 + j], recv_sems.at[6 * p + 3 + j], sibling)
                cp.start()
                passed.append(cp)
        for j, (px, py) in enumerate(chips):
            for p in range(n):
                got = out[p].at[2 * px + py, 1 - c]
                _remote(got, got, send_sems.at[6 * p + 3 + j], recv_sems.at[6 * p + 3 + j], (x, y, c)).wait_recv()
        for cp in first + passed:
            cp.wait_send()

    return pl.pallas_call(
        body, name=name,
        out_shape=[jax.ShapeDtypeStruct((4,) + s.shape, s.dtype) for s in shards],
        in_specs=[pl.BlockSpec(memory_space=pl.ANY)] * n, out_specs=[pl.BlockSpec(memory_space=pl.ANY)] * n,
        scratch_shapes=[pltpu.SemaphoreType.DMA((6 * n,)), pltpu.SemaphoreType.DMA((6 * n,))],
    )(*shards)


def _ag_direct_copies(sh, land, send_sems, recv_sems, starting):
    x, y, c = _coords()
    return [_remote(sh[p], land[p].at[2 * x + y] if starting else land[p].at[2 * px + py],
                    send_sems.at[3 * p + j], recv_sems.at[3 * p + j], (px, py, c))
            for p in range(len(sh)) for j, (px, py) in enumerate(_other_chips(x, y))]


def _rs_sibling_copies(g, land, send_sems, recv_sems, starting):
    x, y, c = _coords()
    return [_remote(g[p].at[s, 1 - c], land[p].at[s], send_sems.at[4 * p + s], recv_sems.at[4 * p + s], (x, y, 1 - c))
            for p in range(len(g)) for s in range(4)]


def _rs_chips_copies(cs, land, send_sems, recv_sems, starting):
    x, y, c = _coords()
    return [_remote(cs[p].at[2 * px + py], land[p].at[j], send_sems.at[3 * p + j], recv_sems.at[3 * p + j], (px, py, c))
            for p in range(len(cs)) for j, (px, py) in enumerate(_other_chips(x, y))]


def _split_start(copies, srcs, land_shapes, per, name):
    n = len(srcs)

    def body(*refs):
        for cp in copies(refs[:n], refs[n:2 * n], refs[2 * n], refs[2 * n + 1], True):
            cp.start()
        token = refs[4 * n + 2]
        token[...] = jnp.zeros_like(token)

    lands = [pltpu.with_memory_space_constraint(lax.empty(shp, s.dtype), pltpu.HBM) for shp, s in zip(land_shapes, srcs)]
    res = pl.pallas_call(
        body, name=name,
        out_shape=(pltpu.SemaphoreType.DMA((per * n,)), pltpu.SemaphoreType.DMA((per * n,)))
        + tuple(pltpu.HBM(s.shape, s.dtype) for s in srcs) + tuple(pltpu.HBM(l.shape, l.dtype) for l in lands)
        + (jax.ShapeDtypeStruct((8, LANES), F32),),
        in_specs=(HBM_SPEC,) * (2 * n), out_specs=(SEM_SPEC, SEM_SPEC) + (HBM_SPEC,) * (2 * n) + (pl.BlockSpec(memory_space=pltpu.VMEM),),
        input_output_aliases={i: 2 + i for i in range(2 * n)},
        compiler_params=pltpu.CompilerParams(has_side_effects=EFFECT),
    )(*[pltpu.with_memory_space_constraint(s, pltpu.HBM) for s in srcs], *lands)
    return res[0], res[1], res[2:2 + n], res[2 + n:2 + 2 * n], res[2 + 2 * n]


def _split_wait(copies, send_sems, recv_sems, src_thru, land_thru, after, name):
    n = len(src_thru)

    def body(*refs):
        for cp in copies(refs[:n], refs[n:2 * n], refs[2 * n], refs[2 * n + 1], False):
            cp.wait_send()
            cp.wait_recv()

    res = pl.pallas_call(
        body, name=name,
        out_shape=tuple(pltpu.HBM(s.shape, s.dtype) for s in src_thru) + tuple(pltpu.HBM(l.shape, l.dtype) for l in land_thru),
        in_specs=(HBM_SPEC,) * (2 * n) + (SEM_SPEC, SEM_SPEC, pl.BlockSpec(memory_space=pl.ANY)),
        out_specs=(HBM_SPEC,) * (2 * n), input_output_aliases={i: i for i in range(2 * n)},
        compiler_params=pltpu.CompilerParams(has_side_effects=EFFECT),
    )(*src_thru, *land_thru, send_sems, recv_sems, after)
    return res[:n], res[n:]


def _ag_direct_start(shards, name):
    return _split_start(_ag_direct_copies, shards, [(4,) + s.shape for s in shards], 3, name)


def _ag_direct_wait(send_sems, recv_sems, sh_thru, land_thru, after, name):
    return _split_wait(_ag_direct_copies, send_sems, recv_sems, sh_thru, land_thru, after, name)[1]


def _rs_sibling_start(gs, name):
    return _split_start(_rs_sibling_copies, gs, [(4,) + g.shape[2:] for g in gs], 4, name)


def _rs_sibling_wait(send_sems, recv_sems, g_thru, land_thru, after, name):
    return _split_wait(_rs_sibling_copies, send_sems, recv_sems, g_thru, land_thru, after, name)


def _rs_sibling_list(gs, name):
    n = len(gs)

    def body(*refs):
        cps = _rs_sibling_copies(refs[:n], refs[n:2 * n], refs[2 * n], refs[2 * n + 1], True)
        for cp in cps:
            cp.start()
        for cp in cps:
            cp.wait_recv()
        for cp in cps:
            cp.wait_send()

    return pl.pallas_call(
        body, name=name,
        out_shape=[jax.ShapeDtypeStruct((4,) + g.shape[2:], g.dtype) for g in gs],
        in_specs=[pl.BlockSpec(memory_space=pl.ANY)] * n, out_specs=[pl.BlockSpec(memory_space=pl.ANY)] * n,
        scratch_shapes=[pltpu.SemaphoreType.DMA((4 * n,)), pltpu.SemaphoreType.DMA((4 * n,))],
    )(*gs)


def _rs_chips_start(cs, name):
    return _split_start(_rs_chips_copies, cs, [(3,) + c.shape[1:] for c in cs], 3, name)


def _rs_chips_wait(send_sems, recv_sems, cs_thru, land_thru, after, name):
    return _split_wait(_rs_chips_copies, send_sems, recv_sems, cs_thru, land_thru, after, name)


def _swap_list(ghs, name):
    n = len(ghs)

    def body(*refs):
        g, out, send_sems, recv_sems = refs[:n], refs[n:2 * n], refs[2 * n], refs[2 * n + 1]
        x, y, c = _coords()
        cps = [_remote(g[p].at[c], out[p].at[c], send_sems.at[p], recv_sems.at[p], (x, y, 1 - c)) for p in range(n)]
        for cp in cps:
            cp.start()
        for p in range(n):
            _remote(g[p].at[c], out[p].at[1 - c], send_sems.at[p], recv_sems.at[p], (x, y, 1 - c)).wait_recv()
        for cp in cps:
            cp.wait_send()

    return pl.pallas_call(
        body, name=name,
        out_shape=[jax.ShapeDtypeStruct(g.shape, g.dtype) for g in ghs],
        in_specs=[pl.BlockSpec(memory_space=pl.ANY)] * n, out_specs=[pl.BlockSpec(memory_space=pl.ANY)] * n,
        input_output_aliases={p: p for p in range(n)},
        scratch_shapes=[pltpu.SemaphoreType.DMA((n,)), pltpu.SemaphoreType.DMA((n,))],
    )(*ghs)


def _pad_win(w):
    return jnp.concatenate([w[:, :416], jnp.zeros((w.shape[0], 96), w.dtype), w[:, 416:1952],
                            w[:, 1952:1960], jnp.zeros((w.shape[0], 120), w.dtype)], axis=1)


def _unpad_win(g):
    return jnp.concatenate([g[:, :416], g[:, 512:2048], g[:, 2048:2056]], axis=1)


def _pad_wq(w):
    return jnp.pad(w.reshape(Q_LORA, HEADS, QK_DIM), ((0, 0), (0, 0), (0, LANES - QK_DIM))).reshape(Q_LORA, HEADS * LANES)


def _unpad_wq(g):
    return g.reshape(Q_LORA, HEADS, LANES)[:, :, :QK_DIM].reshape(Q_LORA, HEADS * QK_DIM)


def _cols_to_shards(a):
    r, c4 = a.shape
    return a.reshape(r, 4, c4 // 4).transpose(1, 0, 2)


def _shards_to_cols(a):
    _, r, c = a.shape
    return a.transpose(1, 0, 2).reshape(r, 4 * c)


def _pack_small(tree):
    parts = []
    for l in range(DEPTH):
        for (n, k) in SMALL:
            parts.append(jnp.pad(tree[n][l].reshape(-1), (0, -k % LANES)))
    flat = jnp.concatenate(parts)
    return jnp.pad(flat, (0, SMALL_ROWS * LANES - flat.shape[0])).reshape(SMALL_ROWS, LANES)


def _unpack_small(buf):
    flat = buf.reshape(-1)
    out = {n: [] for (n, _) in SMALL}
    o = 0
    for l in range(DEPTH):
        for (n, k) in SMALL:
            out[n].append(flat[o:o + k])
            o += k + (-k % LANES)
    return {n: jnp.stack(v) for n, v in out.items()}


def _vec(v, width=LANES):
    return jnp.pad(v.reshape(1, -1), ((0, 0), (0, width - v.shape[-1])))


def kernel(x, c, positions, norm1_w, norm2_w, w_ada, b_ada, w_in, q_a_norm_w, w_q_up, kv_a_norm_w, w_kv_up, q_nope_norm_w, q_pe_norm_w, k_nope_norm_w, k_pe_norm_w, conv_w, conv_b, dt_bias, a_log, d_skip, ssd_norm_w, w_out, w_gate_up, w_down, loss_target, m_norm1_w, m_norm2_w, m_w_ada, m_b_ada, m_w_in, m_q_a_norm_w, m_w_q_up, m_kv_a_norm_w, m_w_kv_up, m_q_nope_norm_w, m_q_pe_norm_w, m_k_nope_norm_w, m_k_pe_norm_w, m_conv_w, m_conv_b, m_dt_bias, m_a_log, m_d_skip, m_ssd_norm_w, m_w_out, m_w_gate_up, m_w_down, v_norm1_w, v_norm2_w, v_w_ada, v_b_ada, v_w_in, v_q_a_norm_w, v_w_q_up, v_kv_a_norm_w, v_w_kv_up, v_q_nope_norm_w, v_q_pe_norm_w, v_k_nope_norm_w, v_k_pe_norm_w, v_conv_w, v_conv_b, v_dt_bias, v_a_log, v_d_skip, v_ssd_norm_w, v_w_out, v_w_gate_up, v_w_down):
    W = dict(zip(WEIGHTS, (norm1_w, norm2_w, w_ada, b_ada, w_in, q_a_norm_w, w_q_up, kv_a_norm_w, w_kv_up, q_nope_norm_w, q_pe_norm_w, k_nope_norm_w, k_pe_norm_w, conv_w, conv_b, dt_bias, a_log, d_skip, ssd_norm_w, w_out, w_gate_up, w_down)))
    M = dict(zip(WEIGHTS, (m_norm1_w, m_norm2_w, m_w_ada, m_b_ada, m_w_in, m_q_a_norm_w, m_w_q_up, m_kv_a_norm_w, m_w_kv_up, m_q_nope_norm_w, m_q_pe_norm_w, m_k_nope_norm_w, m_k_pe_norm_w, m_conv_w, m_conv_b, m_dt_bias, m_a_log, m_d_skip, m_ssd_norm_w, m_w_out, m_w_gate_up, m_w_down)))
    V = dict(zip(WEIGHTS, (v_norm1_w, v_norm2_w, v_w_ada, v_b_ada, v_w_in, v_q_a_norm_w, v_w_q_up, v_kv_a_norm_w, v_w_kv_up, v_q_nope_norm_w, v_q_pe_norm_w, v_k_nope_norm_w, v_k_pe_norm_w, v_conv_w, v_conv_b, v_dt_bias, v_a_log, v_d_skip, v_ssd_norm_w, v_w_out, v_w_gate_up, v_w_down)))
    S = x.shape[1]
    xi, yi, ci = _coords()
    chip = 2 * xi + yi
    dev = 2 * chip + ci
    x0 = x[0]
    tgt = loss_target[0]

    inv_freq = 1.0 / (ROPE_THETA ** (jnp.arange(0, ROPE, 2, dtype=F32) / ROPE))
    ang = positions[0].astype(F32)[:, None] * inv_freq
    cos, sin = jnp.cos(ang), jnp.sin(ang)
    z16, z32, z64 = jnp.zeros((S, 16), F32), jnp.zeros((S, 32), F32), jnp.zeros((S, 64), F32)
    tab_c = jnp.concatenate([jnp.ones((S, 64), F32), cos, cos, z32], axis=1)
    tab_s1 = jnp.concatenate([z64, z16, sin, z32], axis=1)
    tab_s2 = jnp.concatenate([z64, -sin, z16, z32], axis=1)

    blk0 = jnp.concatenate([c.reshape(-1), W['conv_w'].reshape(-1)]).reshape(24, LANES)
    g0 = _ag8(blk0, "ag_c_conv").reshape(8, 24 * LANES)
    c_all = g0[:, :D_MODEL]
    conv_full = g0[0::2, D_MODEL:].reshape(4, DEPTH, CONV_TAPS, 256).transpose(1, 2, 0, 3).reshape(DEPTH, CONV_TAPS, D_CONV)

    sh = [{n: W[n][l].astype(BF16) for n in BIG} for l in range(DEPTH)]
    got_first = _ag_list([sh[0][n].reshape(2, sh[0][n].shape[0] // 2, sh[0][n].shape[1]) for n in FIRST], "ag_w0_first")
    to_operand = dict(w_in=lambda a: _pad_win(_shards_to_cols(a)), w_q_up=lambda a: _pad_wq(_shards_to_cols(a)),
                      w_kv_up=_shards_to_cols, w_out=lambda a: a.reshape(D_MODEL, D_MODEL), w_gate_up=lambda a: a,
                      w_down=lambda a: a.reshape(D_FF, D_MODEL))

    def layer_weights(names, gathered, own):
        return {n: to_operand[n](lax.dynamic_update_slice_in_dim(a.reshape(4, -1, a.shape[-1]), own[n][None], chip, axis=0))
                for n, a in zip(names, gathered)}

    LW = [layer_weights(FIRST, got_first, sh[0]), None]

    b_sh = lax.dynamic_slice_in_dim(W['b_ada'], chip * 1536, 1536, axis=1).reshape(DEPTH, 1, 1536)
    mod_sh = _ada_fwd(c_all, W['w_ada'], b_sh, "ada_fwd")
    g1 = _ag8(mod_sh.reshape(192, LANES), "ag_mod").reshape(8, DEPTH, 8, 1536)
    mod_all = g1[0::2].transpose(1, 2, 0, 3).reshape(DEPTH, 8, 6 * D_MODEL)
    mod = lax.dynamic_index_in_dim(mod_all, dev, axis=1, keepdims=False)
    mod, rest0 = lax.optimization_barrier((mod, [sh[0][n] for n in REST]))
    ag0 = _ag_direct_start(rest0, "ag_w0_rest_start")

    def mvec(l, k):
        return mod[l, k * D_MODEL:(k + 1) * D_MODEL].reshape(1, D_MODEL)

    def small(name, l, width=None):
        v = W[name][l]
        return _vec(v, width or v.shape[-1])

    def wq_vec(l):
        return _vec(jnp.concatenate([W['q_nope_norm_w'][l], W['q_pe_norm_w'][l]]))

    def wk_vec(l):
        return _vec(jnp.concatenate([W['k_nope_norm_w'][l], W['k_pe_norm_w'][l]]))

    def conv_vec(l):
        return jnp.concatenate([conv_full[l], W['conv_b'][l].reshape(1, D_CONV), jnp.zeros((3, D_CONV), F32)], axis=0)

    def ssd_vec(l):
        return jnp.concatenate([_vec(W['dt_bias'][l]), _vec(W['a_log'][l]), _vec(W['d_skip'][l]), jnp.zeros((5, LANES), F32)], axis=0)

    sv = []
    xcur = x0
    h1 = _row_fwd(fn_norm_mod, "norm_mod_f", [(x0, 0, D_MODEL)], [small('norm1_w', 0) + ag0[4][0, 0], mvec(0, 1), mvec(0, 0)],
                  [(D_MODEL, BF16)])[0]
    fin = None
    ag_first = None
    ag_rest = ag0
    for l in range(DEPTH):
        if l == 1:
            LW[1] = layer_weights(FIRST, _ag_direct_wait(*ag_first[:4], xcur, "ag_w1_first_wait"), sh[1])
        lw = LW[l]
        t = dict(xcur=xcur, h1=h1)
        t['proj'] = proj = _mm(h1, lw['w_in'], 'nn', f"mm_in_{l}")
        t['qa_n'], t['kva_n'] = _row_fwd(fn_lat_norm, f"lat_norm_f{l}", [(proj, 0, 256), (proj, 2, 128)],
                                         [small('q_a_norm_w', l), small('kv_a_norm_w', l)], [(256, BF16), (128, BF16)])
        t['q'] = _mm(t['qa_n'], lw['w_q_up'], 'nn', f"mm_q_{l}")
        t['kv'] = _mm(t['kva_n'], lw['w_kv_up'], 'nn', f"mm_kv_{l}")
        t['qf'], t['kf'], t['vv'], t['kT'] = _row_fwd(
            fn_qk_prep_kt, f"qk_prep_f{l}",
            [(t['q'], 0, 1024), (t['kv'], 0, 1024), (proj, 3, 128), (tab_c, 0, 128), (tab_s1, 0, 128), (tab_s2, 0, 128)],
            [wq_vec(l), wk_vec(l)], [(1024, BF16), (1024, BF16), (1024, BF16), (1024, BF16)], transposed=(3,))
        t['ao'], t['lse'] = _attn_fwd(t['qf'], t['kf'], t['vv'], f"attn_f{l}")
        t['xact'] = _conv_fwd(proj, conv_vec(l), f"conv_f{l}")
        t['y'], t['states'] = _ssd_fwd(t['xact'], proj, ssd_vec(l), f"ssd_f{l}")
        tie = 0.0
        t['ao'], t['y'] = lax.optimization_barrier((t['ao'], t['y']))
        rest = list(_ag_direct_wait(*ag_rest[:4], t['y'], f"ag_w{l}_rest_wait"))
        if l == 0:
            rest, sh1f, sh1r = lax.optimization_barrier((rest, [sh[1][n] for n in FIRST], [sh[1][n] for n in REST]))
            ag_first = _ag_direct_start(sh1f, "ag_w1_first_start")
            ag_rest = _ag_direct_start(sh1r, "ag_w1_rest_start")
            tie = ag_first[4][0, 0] + ag_rest[4][0, 0]
        lw.update(layer_weights(REST, rest, sh[l]))
        t['mix'] = _row_fwd(fn_gated_mix, f"gated_f{l}", [(t['y'], 0, 512), (proj, 1, 512), (t['ao'], 0, 512)],
                            [small('ssd_norm_w', l) + tie], [(1024, BF16)])[0]
        t['mo'] = _mm(t['mix'], lw['w_out'], 'nn', f"mm_out_{l}")
        t['x1'], t['h2'] = _row_fwd(fn_resid_norm, f"resid_mid_f{l}", [(xcur, 0, D_MODEL), (t['mo'], 0, D_MODEL)],
                                    [mvec(l, 2), small('norm2_w', l), mvec(l, 4), mvec(l, 3)],
                                    [(D_MODEL, F32), (D_MODEL, BF16)])
        t['gu'], t['act'] = _mm_gu_swiglu(t['h2'], lw['w_gate_up'], f"mm_gu_{l}")
        t['ff'] = _mm(t['act'], lw['w_down'], 'nn', f"mm_down_{l}")
        if l + 1 < DEPTH:
            xcur, h1 = _row_fwd(fn_resid_norm, f"resid_end_f{l}", [(t['x1'], 0, D_MODEL), (t['ff'], 0, D_MODEL)],
                                [mvec(l, 5), small('norm1_w', l + 1), mvec(l + 1, 1), mvec(l + 1, 0)],
                                [(D_MODEL, F32), (D_MODEL, BF16)])
        else:
            fin = _final(t['x1'], t['ff'], tgt, mvec(l, 5), "final_loss")
        sv.append(t)

    dx1, dff, dg2_last, loss_acc = fin
    gfull = {n: [None] * DEPTH for n in BIG}
    gsm = {n: [None] * DEPTH for (n, _) in SMALL}
    dmod = [[None] * 6 for _ in range(DEPTH)]
    dmod[DEPTH - 1][5] = dg2_last
    grad_x = None
    pending = []

    def halves_of(l, names):
        return [gfull[n][l].reshape(4, 2, gfull[n][l].shape[1] // 2, gfull[n][l].shape[2]) for n in names]

    def rs_finish(l, names, tag, g4, sib):
        h = _rs_chips_start(_sum_sibling(g4, sib, ci, f"sum_sibling_{tag}"), f"rs_chips_start_{tag}")
        pending.append((l, names, h))
        return h[4][0, 0]

    tie_l1 = tie_l0a = tie_sib = 0.0
    sib_l1 = sib_l0a = None

    for l in reversed(range(DEPTH)):
        t = sv[l]
        lw = LW[l]
        proj = t['proj']
        dgu = _mm_down_dx_swiglu(dff, lw['w_down'], t['gu'], f"mm_down_dx{l}")
        gfull['w_down'][l] = _mm(t['act'], dff, 'tn', f"mm_down_dw{l}").reshape(4, D_FF // 4, D_MODEL)
        dh2 = _mm(dgu, lw['w_gate_up'], 'nt', f"mm_gu_dx{l}", stack='b')
        gfull['w_gate_up'][l] = _mm(t['h2'], dgu, 'tn', f"mm_gu_dw{l}", stack='out')
        if l == 0:
            sib_l0a = _rs_sibling_start(halves_of(0, EARLY), "rs_sibling_start_l0a")
            tie_l1 = rs_finish(1, BIG, "l1", *_rs_sibling_wait(*sib_l1[:4], dh2, "rs_sibling_wait_l1"))
            tie_sib = sib_l0a[4][0, 0]
        dxc, dmo, dmod[l][2], gsm['norm2_w'][l], dmod[l][4], dmod[l][3] = _row_bwd(
            fn_resid_norm, f"resid_mid_b{l}", [(t['xcur'], 0, D_MODEL), (t['mo'], 0, D_MODEL)],
            [mvec(l, 2) + ((tie_l1 + tie_sib) if l == 0 else 0.0), small('norm2_w', l), mvec(l, 4), mvec(l, 3)],
            [(dx1, 0, D_MODEL), (dh2, 0, D_MODEL)], [0, 1], [0, 1, 2, 3], ddtypes=[F32, BF16])
        dmix = _mm(dmo, lw['w_out'], 'nt', f"mm_out_dx{l}")
        gfull['w_out'][l] = _mm(t['mix'], dmo, 'tn', f"mm_out_dw{l}").reshape(4, D_MODEL // 4, D_MODEL)
        dy, dz, gsm['ssd_norm_w'][l] = _row_bwd(fn_gated_norm, f"gated_b{l}", [(t['y'], 0, 512), (proj, 1, 512)],
                                                [small('ssd_norm_w', l)], [(dmix, 1, 512)], [0, 1], [0])
        if l == 0:
            tie_l0a = rs_finish(0, EARLY, "l0a", *_rs_sibling_wait(*sib_l0a[:4], dmix, "rs_sibling_wait_l0a"))
        dxact, ddt, dsv = _ssd_bwd(t['xact'], proj, ssd_vec(l) + (tie_l0a if l == 0 else 0.0), t['states'], dy, f"ssd_b{l}")
        gsm['dt_bias'][l], gsm['a_log'][l], gsm['d_skip'][l] = dsv[0, :8], dsv[1, :8], dsv[2, :8]
        dxbc, dcv = _conv_bwd(proj, conv_vec(l), dxact, f"conv_b{l}")
        gsm['conv_w'][l] = dcv[:CONV_TAPS]
        gsm['conv_b'][l] = dcv[CONV_TAPS]
        delta_r = _attn_delta(dmix, t['ao'], f"attn_delta{l}")
        dqT, dkf, dvv = _attn_bwd(t['qf'], t['kf'], t['kT'], t['vv'], dmix, t['lse'], delta_r, f"attn_b{l}")
        dq, dkv, dkpe, dwq, dwk = _row_bwd(
            fn_qk_prep, f"qk_prep_b{l}",
            [(t['q'], 0, 1024), (t['kv'], 0, 1024), (proj, 3, 128), (tab_c, 0, 128), (tab_s1, 0, 128), (tab_s2, 0, 128)],
            [wq_vec(l), wk_vec(l)], [(dqT, 0, 1024), (dkf, 0, 1024), (dvv, 0, 1024)], [0, 1, 2], [0, 1],
            ddtypes=[BF16, BF16, F32], transposed=(0,))
        gsm['q_nope_norm_w'][l], gsm['q_pe_norm_w'][l] = dwq[0, :NOPE], dwq[0, NOPE:QK_DIM]
        gsm['k_nope_norm_w'][l], gsm['k_pe_norm_w'][l] = dwk[0, :NOPE], dwk[0, NOPE:QK_DIM]
        dqa_n = _mm(dq, lw['w_q_up'], 'nt', f"mm_q_dx{l}")
        gfull['w_q_up'][l] = _cols_to_shards(_unpad_wq(_mm(t['qa_n'], dq, 'tn', f"mm_q_dw{l}")))
        dkva_n = _mm(dkv, lw['w_kv_up'], 'nt', f"mm_kv_dx{l}")
        gfull['w_kv_up'][l] = _cols_to_shards(_mm(t['kva_n'], dkv, 'tn', f"mm_kv_dw{l}"))
        dqa, dkva, dqw, dkvw = _row_bwd(fn_lat_norm, f"lat_norm_b{l}", [(proj, 0, 256), (proj, 2, 128)],
                                        [small('q_a_norm_w', l), small('kv_a_norm_w', l)],
                                        [(dqa_n, 0, 256), (dkva_n, 0, 128)], [0, 1], [0, 1])
        gsm['q_a_norm_w'][l], gsm['kv_a_norm_w'][l] = dqw[0], dkvw[0]
        dproj = jnp.concatenate([dqa, dkva, dkpe, dz, dxbc, ddt], axis=1).astype(BF16)
        dh1 = _mm(dproj, lw['w_in'], 'nt', f"mm_in_dx{l}")
        gfull['w_in'][l] = _cols_to_shards(_unpad_win(_mm(t['h1'], dproj, 'tn', f"mm_in_dw{l}")))
        if l > 0:
            p = sv[l - 1]
            dx1, dff, dmod[l - 1][5], gsm['norm1_w'][l], dmod[l][1], dmod[l][0] = _row_bwd(
                fn_resid_norm, f"resid_end_b{l - 1}", [(p['x1'], 0, D_MODEL), (p['ff'], 0, D_MODEL)],
                [mvec(l - 1, 5), small('norm1_w', l), mvec(l, 1), mvec(l, 0)], [(dxc, 0, D_MODEL), (dh1, 0, D_MODEL)],
                [0, 1], [0, 1, 2, 3], ddtypes=[F32, BF16])
            sib_l1 = _rs_sibling_start(halves_of(l, BIG), f"rs_sibling_start_l{l}")
            dff = dff + sib_l1[4][0, 0].astype(BF16)
        else:
            grad_x, gsm['norm1_w'][l], dmod[l][1], dmod[l][0] = _row_bwd(
                fn_norm_mod_pass, "norm_mod_b", [(x0, 0, D_MODEL)], [small('norm1_w', 0), mvec(0, 1), mvec(0, 0)],
                [(dxc, 0, D_MODEL), (dh1, 0, D_MODEL)], [0], [0, 1, 2])
        for n in ('norm1_w', 'norm2_w', 'ssd_norm_w'):
            gsm[n][l] = gsm[n][l][0]

    for l in range(DEPTH):
        gsm['b_ada'][l] = jnp.concatenate([d[0] for d in dmod[l]])
    sm_part = _pack_small({n: jnp.stack(v) for n, v in gsm.items()}).at[SMALL_ROWS - 1, 0].set(loss_acc[0, 0])
    sm_all = _ag8(sm_part, "ag_small")
    loss = jnp.sum(sm_all.reshape(8, SMALL_ROWS, LANES)[:, SMALL_ROWS - 1, 0])
    sm_all, late = lax.optimization_barrier((sm_all, [gfull[n][0] for n in BIG[2:]]))
    for n, g in zip(BIG[2:], late):
        gfull[n][0] = g
    late4 = halves_of(0, BIG[2:])
    tie_l0b = rs_finish(0, BIG[2:], "l0b", late4, _rs_sibling_list(late4, "rs_sibling_l0b"))

    def with_conv(tree):
        wide = lax.dynamic_update_slice_in_dim(jnp.zeros((DEPTH, CONV_TAPS, D_CONV), F32), tree['conv_w'], chip * 256, axis=2)
        return {**tree, 'conv_w': wide}

    g_sm, d_sm, m_sm, v_sm = _adam(_pack_small(with_conv(W)) + tie_l0b, _pack_small(with_conv(M)), _pack_small(with_conv(V)),
                                   [(sm_all, d * SMALL_ROWS) for d in range(8)], "adam_small")
    out_small = [_unpack_small(b) for b in (g_sm, d_sm, m_sm, v_sm)]
    for o in out_small:
        o['conv_w'] = lax.dynamic_slice_in_dim(o['conv_w'].reshape(DEPTH, CONV_TAPS, D_CONV), chip * 256, 256, axis=2)

    dmod_all = sm_all.reshape(8, SMALL_ROWS * LANES)
    per_layer = sum(k + (-k % LANES) for (_, k) in SMALL)
    dmod_sh = jnp.stack([lax.dynamic_slice_in_dim(dmod_all[:, l * per_layer:l * per_layer + 6 * D_MODEL], chip * 1536, 1536, axis=1)
                         for l in range(DEPTH)])
    ada = out_ada = _ada_bwd_adam(c_all.T, dmod_sh, W['w_ada'], M['w_ada'], V['w_ada'], "ada_bwd_adam")

    keys, cs_all, land_all = [], [], []
    for (l, names, (send_sems, recv_sems, cs_thru, land_thru, _)) in pending:
        cs, lands = _rs_chips_wait(send_sems, recv_sems, cs_thru, land_thru, ada[3], f"rs_chips_wait_l{l}{len(names)}")
        keys += [(l, n) for n in names]
        cs_all += list(cs)
        land_all += list(lands)
    gboth = _swap_list(_sum_chips(cs_all, land_all, chip, ci, "sum_chips"), "swap_halves")
    gshard = {k: g.reshape(2 * g.shape[1], g.shape[2]) for k, g in zip(keys, gboth)}

    def natural(n, a):
        return jnp.swapaxes(a, -1, -2) if n == 'w_in' else a

    res = _adam_multi([natural(n, W[n]) for n in BIG], [natural(n, M[n]) for n in BIG], [natural(n, V[n]) for n in BIG],
                      [[natural(n, gshard[(l, n)]) for l in range(DEPTH)] for n in BIG], "adam_big")
    out_big = [{n: natural(n, a) for n, a in zip(BIG, o)} for o in res]

    outs = [loss, grad_x[None]]
    for k in range(4):
        for n in WEIGHTS:
            if n == 'w_ada':
                outs.append(out_ada[k])
            elif n in BIG:
                outs.append(out_big[k][n])
            else:
                outs.append(out_small[k][n])
    return tuple(outs)
```

```python
import functools

import jax
import jax.numpy as jnp
from jax import lax
from jax.experimental import pallas as pl
from jax.experimental.pallas import tpu as pltpu

F32 = jnp.float32
BF16 = jnp.bfloat16
MESH = pl.DeviceIdType.MESH

D_MODEL = 1024
DEPTH = 2
HEADS = 8
NOPE = 64
ROPE = 32
QK_DIM = NOPE + ROPE
Q_LORA = 256
KV_LORA = 128
SSD_HEADS = 8
SSD_P = 64
SSD_N = 128
CHUNK = 256
CONV_TAPS = 4
D_SSD = 512
D_CONV = 1024
D_FF = 2816
D_IN = 1960
D_IN_PAD = 2176
EPS = 1e-6
ROPE_THETA = 10000.0
ATT_SCALE = QK_DIM ** -0.5
NEG = -1e30
LANES = 128
VMEM_LIMIT = 48 * 1024 * 1024
MM_VMEM_BUDGET = 36 * 1024 * 1024
MM_SLOTS = 3

ADAM_LR, ADAM_B1, ADAM_B2, ADAM_EPS, ADAM_WD, ADAM_STEP = 0.001, 0.9, 0.999, 1e-08, 0.01, 10

WEIGHTS = ['norm1_w', 'norm2_w', 'w_ada', 'b_ada', 'w_in', 'q_a_norm_w', 'w_q_up', 'kv_a_norm_w', 'w_kv_up',
           'q_nope_norm_w', 'q_pe_norm_w', 'k_nope_norm_w', 'k_pe_norm_w', 'conv_w', 'conv_b', 'dt_bias', 'a_log',
           'd_skip', 'ssd_norm_w', 'w_out', 'w_gate_up', 'w_down']
BIG = ['w_down', 'w_gate_up', 'w_out', 'w_kv_up', 'w_q_up', 'w_in']
EARLY = BIG[:2]
FIRST = BIG[3:]
REST = BIG[:3]
SMALL = [('b_ada', 6144), ('conv_w', 4096), ('norm1_w', 1024), ('norm2_w', 1024), ('conv_b', 1024), ('ssd_norm_w', 512),
         ('q_a_norm_w', 256), ('kv_a_norm_w', 128), ('q_nope_norm_w', 64), ('q_pe_norm_w', 32),
         ('k_nope_norm_w', 64), ('k_pe_norm_w', 32), ('dt_bias', 8), ('a_log', 8), ('d_skip', 8)]
SMALL_ROWS = 240


def _cp(sem=None, **kw):
    return pltpu.CompilerParams(dimension_semantics=sem, vmem_limit_bytes=VMEM_LIMIT, **kw)


def _dot(a, b, dims, prec=None):
    return lax.dot_general(a, b, (dims, ((), ())), preferred_element_type=F32, precision=prec)


def _tile(dim, target):
    best = 0
    for t in range(LANES, min(dim, target) + 1, LANES):
        if dim % t == 0:
            best = t
    if best < 256 and dim <= 2304:
        return dim
    return best


def _mm(a, b, mode, name, out_dtype=F32, stack=None):
    ns = None
    halves = (a if mode == 'nt' else b).ndim == 3 and stack is not None and not (stack == 'b' and mode == 'nn')
    if stack == 'b':
        ns = b.shape[2]
        if mode == 'nn':
            (M, K), N = a.shape, 4 * ns
        else:
            M, K, N = a.shape[-2], 4 * ns, b.shape[1]
    elif mode == 'nn':
        (M, K), (_, N) = a.shape, b.shape
    elif mode == 'nt':
        (M, K), (N, _) = a.shape, b.shape
    else:
        (K, M), N = a.shape, (2 * b.shape[2] if halves else b.shape[1])
    if stack == 'out':
        ns = N // 4
    tm, tn, tk = _tile(M, 1408 if mode == 'tn' else 1024), _tile(N, 1408), _tile(K, 1408)
    if stack == 'b' and mode == 'nt':
        tk = ns
    elif stack is not None:
        tn = ns
    nk = K // tk
    dims = {'nn': ((1,), (0,)), 'nt': ((1,), (1,)), 'tn': ((0,), (0,))}[mode]

    def body(a_ref, b_ref, o_ref, *acc):
        part = _dot(a_ref[...].astype(BF16), b_ref[...].astype(BF16), dims)
        if nk == 1:
            o_ref[...] = part.astype(o_ref.dtype)
            return
        k = pl.program_id(2)

        @pl.when(k == 0)
        def _():
            acc[0][...] = part

        @pl.when(k > 0)
        def _():
            acc[0][...] += part

        @pl.when(k == nk - 1)
        def _():
            o_ref[...] = acc[0][...].astype(o_ref.dtype)

    a_spec = pl.BlockSpec((tk, tm), lambda i, j, k: (k, i)) if mode == 'tn' else pl.BlockSpec((tm, tk), lambda i, j, k: (i, k))
    b_spec = pl.BlockSpec((tn, tk), lambda i, j, k: (j, k)) if mode == 'nt' else pl.BlockSpec((tk, tn), lambda i, j, k: (k, j))
    o_spec, o_shape = pl.BlockSpec((tm, tn), lambda i, j, k: (i, j)), (M, N)
    if stack == 'b':
        b_spec = (pl.BlockSpec((None, tn, ns), lambda i, j, k: (k, j, 0)) if mode == 'nt'
                  else pl.BlockSpec((None, tk, ns), lambda i, j, k: (j, k, 0)))
    if stack == 'out':
        o_spec, o_shape = pl.BlockSpec((None, tm, ns), lambda i, j, k: (j, i, 0)), (4, M, ns)
    if halves and mode == 'nt':
        a_spec = pl.BlockSpec((None, tm, ns), lambda i, j, k: (lax.div(k, 2), i, lax.rem(k, 2)))
    if halves and mode == 'tn':
        b_spec = pl.BlockSpec((None, tk, ns), lambda i, j, k: (lax.div(j, 2), k, lax.rem(j, 2)))
    gi, gj = M // tm, N // tn
    total = gi * gj * nk
    in_bytes = tm * tk * a.dtype.itemsize + tk * tn * b.dtype.itemsize
    out_bytes = tm * tn * (2 * jnp.dtype(out_dtype).itemsize + (4 if nk > 1 else 0))
    if total >= 4 and MM_SLOTS * in_bytes + out_bytes <= MM_VMEM_BUDGET:
        shape_of = lambda spec: tuple(d for d in spec.block_shape if d is not None)

        def block_of(ref, spec, step):
            ijk = (lax.div(step, nk * gj), lax.rem(lax.div(step, nk), gj), lax.rem(step, nk))
            return ref.at[tuple(ix if d is None else pl.ds(pl.multiple_of(ix * d, d), d)
                                for ix, d in zip(spec.index_map(*ijk), spec.block_shape))]

        def ring(a_hbm, b_hbm, o_ref, abuf, bbuf, sems, *acc):
            step = (pl.program_id(0) * gj + pl.program_id(1)) * nk + pl.program_id(2)

            def copies(s):
                slot = lax.rem(s, MM_SLOTS)
                return (pltpu.make_async_copy(block_of(a_hbm, a_spec, s), abuf.at[slot], sems.at[0, slot]),
                        pltpu.make_async_copy(block_of(b_hbm, b_spec, s), bbuf.at[slot], sems.at[1, slot]))

            @pl.when(step == 0)
            def _():
                for s in range(MM_SLOTS - 1):
                    for cp in copies(jnp.int32(s)):
                        cp.start()

            @pl.when(step + MM_SLOTS - 1 < total)
            def _():
                for cp in copies(step + MM_SLOTS - 1):
                    cp.start()

            for cp in copies(step):
                cp.wait()
            slot = lax.rem(step, MM_SLOTS)
            body(abuf.at[slot], bbuf.at[slot], o_ref, *acc)

        return pl.pallas_call(
            ring, name=name, grid=(gi, gj, nk),
            in_specs=[pl.BlockSpec(memory_space=pl.ANY)] * 2, out_specs=o_spec,
            out_shape=jax.ShapeDtypeStruct(o_shape, out_dtype),
            scratch_shapes=[pltpu.VMEM((MM_SLOTS,) + shape_of(a_spec), a.dtype), pltpu.VMEM((MM_SLOTS,) + shape_of(b_spec), b.dtype),
                            pltpu.SemaphoreType.DMA((2, MM_SLOTS))] + ([pltpu.VMEM((tm, tn), F32)] if nk > 1 else []),
            compiler_params=_cp(("arbitrary", "arbitrary", "arbitrary")),
        )(a, b)
    return pl.pallas_call(
        body, name=name, grid=(M // tm, N // tn, nk),
        in_specs=[a_spec, b_spec], out_specs=o_spec,
        out_shape=jax.ShapeDtypeStruct(o_shape, out_dtype),
        scratch_shapes=[pltpu.VMEM((tm, tn), F32)] if nk > 1 else [],
        compiler_params=_cp(("parallel", "parallel", "arbitrary")),
    )(a, b)


def _mm_gu_swiglu(h, wst, name):
    S, K = h.shape
    ns = wst.shape[2]
    tm = _tile(S, 512)

    def body(a_ref, bg_ref, bu_ref, gu_ref, act_ref):
        a = a_ref[...]
        g = _dot(a, bg_ref[...], ((1,), (0,)))
        u = _dot(a, bu_ref[...], ((1,), (0,)))
        gu_ref[0] = g
        gu_ref[1] = u
        act_ref[...] = (g * jax.nn.sigmoid(g) * u).astype(act_ref.dtype)

    return pl.pallas_call(
        body, name=name, grid=(S // tm, 2),
        in_specs=[pl.BlockSpec((tm, K), lambda i, j: (i, 0)), pl.BlockSpec((None, K, ns), lambda i, j: (j, 0, 0)),
                  pl.BlockSpec((None, K, ns), lambda i, j: (j + 2, 0, 0))],
        out_specs=[pl.BlockSpec((2, tm, ns), lambda i, j: (0, i, j)), pl.BlockSpec((tm, ns), lambda i, j: (i, j))],
        out_shape=[jax.ShapeDtypeStruct((2, S, 2 * ns), F32), jax.ShapeDtypeStruct((S, 2 * ns), BF16)],
        compiler_params=_cp(("parallel", "parallel")),
    )(h, wst, wst)


def _mm_down_dx_swiglu(dff, w_down, gu, name):
    S, K = dff.shape
    tm, tn = _tile(S, 512), _tile(D_FF, 1408)

    def body(a_ref, b_ref, g_ref, u_ref, o_ref):
        dact = _dot(a_ref[...].astype(BF16), b_ref[...], ((1,), (1,)))
        g, u = g_ref[...], u_ref[...]
        sg = jax.nn.sigmoid(g)
        o_ref[0] = (dact * u * (sg * (1.0 + g * (1.0 - sg)))).astype(o_ref.dtype)
        o_ref[1] = (dact * (g * sg)).astype(o_ref.dtype)

    return pl.pallas_call(
        body, name=name, grid=(S // tm, D_FF // tn),
        in_specs=[pl.BlockSpec((tm, K), lambda i, j: (i, 0)), pl.BlockSpec((tn, K), lambda i, j: (j, 0)),
                  pl.BlockSpec((None, tm, tn), lambda i, j: (0, i, j)), pl.BlockSpec((None, tm, tn), lambda i, j: (1, i, j))],
        out_specs=pl.BlockSpec((2, tm, tn), lambda i, j: (0, i, j)),
        out_shape=jax.ShapeDtypeStruct((2, S, D_FF), BF16),
        compiler_params=_cp(("parallel", "parallel")),
    )(dff, w_down, gu, gu)


def _rspec(tm, w, cb):
    return pl.BlockSpec((tm, w), lambda i: (i, cb))


def _vspec(shape):
    return pl.BlockSpec(shape, lambda i: (0,) * len(shape))


def _row_fwd(fn, name, rows, vecs, outs, tm=256, transposed=()):
    S = rows[0][0].shape[0]
    tm = min(tm, S)
    nin = len(rows) + len(vecs)

    def body(*refs):
        res = fn(*[r[...] for r in refs[:nin]])
        for k, (o_ref, r) in enumerate(zip(refs[nin:], res)):
            o_ref[...] = (r.T if k in transposed else r).astype(o_ref.dtype)

    return pl.pallas_call(
        body, name=name, grid=(S // tm,),
        in_specs=[_rspec(tm, w, cb) for (_, cb, w) in rows] + [_vspec(v.shape) for v in vecs],
        out_specs=[pl.BlockSpec((w, tm), lambda i: (0, i)) if k in transposed else _rspec(tm, w, 0)
                   for k, (w, _) in enumerate(outs)],
        out_shape=[jax.ShapeDtypeStruct((w, S) if k in transposed else (S, w), dt) for k, (w, dt) in enumerate(outs)],
        compiler_params=_cp(("parallel",)),
    )(*[r[0] for r in rows], *vecs)


def _row_bwd(fn, name, rows, vecs, cts, drows, dvecs, tm=256, ddtypes=None, transposed=()):
    S = rows[0][0].shape[0]
    ddtypes = ddtypes or [F32] * len(drows)
    tm = min(tm, S)
    nr, nv, nc = len(rows), len(vecs), len(cts)
    didx = list(drows) + [nr + j for j in dvecs]

    def body(*refs):
        vals = [r[...] for r in refs[:nr + nv]]
        ct = tuple((r[...].T if k in transposed else r[...]).astype(F32)
                   for k, r in enumerate(refs[nr + nv:nr + nv + nc]))
        outs = refs[nr + nv + nc:]

        def g(*d):
            a = list(vals)
            for k, val in zip(didx, d):
                a[k] = val
            return tuple(fn(*a))

        _, vjp = jax.vjp(g, *[vals[k] for k in didx])
        grads = vjp(ct)
        for o, gr in zip(outs[:len(drows)], grads[:len(drows)]):
            o[...] = gr.astype(o.dtype)

        @pl.when(pl.program_id(0) == 0)
        def _():
            for o in outs[len(drows):]:
                o[...] = jnp.zeros_like(o)

        for o, gr in zip(outs[len(drows):], grads[len(drows):]):
            o[...] += gr

    return pl.pallas_call(
        body, name=name, grid=(S // tm,),
        in_specs=[_rspec(tm, w, cb) for (_, cb, w) in rows] + [_vspec(v.shape) for v in vecs]
        + [pl.BlockSpec((w, tm), lambda i: (0, i)) if k in transposed else _rspec(tm, w, cb) for k, (_, cb, w) in enumerate(cts)],
        out_specs=[_rspec(tm, rows[k][2], 0) for k in drows] + [_vspec(vecs[j].shape) for j in dvecs],
        out_shape=[jax.ShapeDtypeStruct((S, rows[k][2]), dt) for k, dt in zip(drows, ddtypes)]
        + [jax.ShapeDtypeStruct(vecs[j].shape, F32) for j in dvecs],
        compiler_params=_cp(("arbitrary",)),
    )(*[r[0] for r in rows], *vecs, *[c[0] for c in cts])


def _rms(x):
    return x * lax.rsqrt(jnp.mean(x * x, axis=-1, keepdims=True) + EPS)


def fn_norm_mod(x, nw, sc, sh):
    return (_rms(x) * nw * (1.0 + sc) + sh,)


def fn_norm_mod_pass(x, nw, sc, sh):
    return (x, _rms(x) * nw * (1.0 + sc) + sh)


def fn_resid_norm(x, d, g, nw, sc, sh):
    xn = x + g * d
    return (xn, _rms(xn) * nw * (1.0 + sc) + sh)


def fn_lat_norm(qa, kva, qw, kvw):
    return (_rms(qa) * qw, _rms(kva) * kvw)


@functools.partial(jax.custom_vjp, nondiff_argnums=(1,))
def _lroll(x, s):
    return pltpu.roll(x, s, 1)


def _lroll_fwd(x, s):
    return pltpu.roll(x, s, 1), None


def _lroll_bwd(s, _, g):
    return (pltpu.roll(g, (LANES - s) % LANES, 1),)


_lroll.defvjp(_lroll_fwd, _lroll_bwd)


def _lane_masks(shape):
    lane = lax.broadcasted_iota(jnp.int32, shape, 1)
    return (lane < NOPE).astype(F32), ((lane >= NOPE) & (lane < QK_DIM)).astype(F32)


def _rope(t, tc, ts1, ts2):
    return t * tc + _lroll(t, 16) * ts1 + _lroll(t, LANES - 16) * ts2


def fn_qk_prep(q, kv, kpe, tc, ts1, ts2, wq, wk):
    mn, mp = _lane_masks((1, LANES))
    mhi = 1.0 - mn

    def head_norm(t, w):
        rn = lax.rsqrt(jnp.sum(t * t * mn, axis=-1, keepdims=True) * (1.0 / NOPE) + EPS)
        rp = lax.rsqrt(jnp.sum(t * t * mp, axis=-1, keepdims=True) * (1.0 / ROPE) + EPS)
        return t * (rn * mn + rp * mp) * w

    kp = _rope(head_norm(_lroll(kpe, NOPE), wk) * mp, tc, ts1, ts2)
    qs, ks, vs = [], [], []
    for h in range(HEADS):
        qs.append(_rope(head_norm(q[:, h * LANES:(h + 1) * LANES], wq), tc, ts1, ts2))
        t = kv[:, h * LANES:(h + 1) * LANES]
        ks.append(head_norm(t, wk) * mn + kp)
        vs.append(_lroll(t, NOPE) * mn + mhi)
    return (jnp.concatenate(qs, axis=1), jnp.concatenate(ks, axis=1), jnp.concatenate(vs, axis=1))


def fn_qk_prep_kt(*args):
    qf, kf, va = fn_qk_prep(*args)
    return (qf, kf, va, kf)


def fn_gated_norm(y, z, w):
    u = y * jax.nn.silu(z)
    half = D_SSD // 2
    return (jnp.concatenate([_rms(u[:, :half]), _rms(u[:, half:])], axis=1) * w,)


def fn_gated_mix(y, z, ao, w):
    return (jnp.concatenate([ao, fn_gated_norm(y, z, w)[0]], axis=1),)


def _final(x1, ff, tgt, g2, name):
    S = x1.shape[0]
    tm = min(256, S)

    def body(x_ref, f_ref, t_ref, g_ref, dx_ref, df_ref, dg_ref, l_ref):
        @pl.when(pl.program_id(0) == 0)
        def _():
            dg_ref[...] = jnp.zeros_like(dg_ref)
            l_ref[...] = jnp.zeros_like(l_ref)

        f = f_ref[...]
        g = g_ref[...]
        e = x_ref[...] + g * f - t_ref[...]
        dx = e * (1.0 / D_MODEL)
        dx_ref[...] = dx
        df_ref[...] = (g * dx).astype(df_ref.dtype)
        dg_ref[...] += jnp.sum(dx * f, axis=0, keepdims=True)
        l_ref[...] += jnp.sum(e * e) * (0.5 / D_MODEL)

    r = _rspec(tm, D_MODEL, 0)
    return pl.pallas_call(
        body, name=name, grid=(S // tm,),
        in_specs=[r, r, r, _vspec((1, D_MODEL))],
        out_specs=[r, r, _vspec((1, D_MODEL)), _vspec((1, LANES))],
        out_shape=[jax.ShapeDtypeStruct((S, D_MODEL), F32), jax.ShapeDtypeStruct((S, D_MODEL), BF16),
                   jax.ShapeDtypeStruct((1, D_MODEL), F32), jax.ShapeDtypeStruct((1, LANES), F32)],
        compiler_params=_cp(("arbitrary",)),
    )(x1, ff, tgt, g2)


def _causal_mask(t):
    r = lax.broadcasted_iota(jnp.int32, (t, t), 0)
    c = lax.broadcasted_iota(jnp.int32, (t, t), 1)
    return c <= r


LOG2E = 1.4426950408889634
EXP2_SCALE = ATT_SCALE * LOG2E
ATT_TQ, ATT_TK = 1024, 1024
ATT_BQ, ATT_BK = 1024, 1024


def _attn_fwd(qf, kf, va, name):
    S = qf.shape[0]
    T, TK = min(ATT_TQ, S), min(ATT_TK, S)
    nmask = max(1, T // TK)

    def body(q_ref, k_ref, v_ref, o_ref, l_ref):
        i = pl.program_id(1)
        r = lax.broadcasted_iota(jnp.int32, (T, TK), 0)
        c = lax.broadcasted_iota(jnp.int32, (T, TK), 1)
        qs = [q_ref[:, hh * LANES:(hh + 1) * LANES] for hh in range(2)]

        def blk(j, carry, masked):
            off = pl.multiple_of(j * TK, TK)
            out = []
            for hh in range(2):
                m, acc = carry[hh]
                s = _dot(qs[hh], k_ref[pl.ds(off, TK), hh * LANES:(hh + 1) * LANES], ((1,), (1,)))
                if masked:
                    s = jnp.where(c + j * TK <= r + i * T, s, NEG)
                mn = jnp.maximum(m, jnp.max(s, axis=1, keepdims=True))
                p = jnp.exp2((s - mn) * EXP2_SCALE)
                al = jnp.exp2((m - mn) * EXP2_SCALE)
                vj = v_ref[pl.ds(off, TK), hh * LANES:(hh + 1) * LANES]
                out.append((mn, al * acc + _dot(p.astype(BF16), vj, ((1,), (0,)))))
            return tuple(out)

        one = (jnp.full((T, 1), NEG, F32), jnp.zeros((T, LANES), F32))
        nfull = lax.div(i * T, TK)
        carry = lax.fori_loop(0, nfull, lambda j, cr: blk(j, cr, False), (one, one))
        for t in range(nmask):
            carry = blk(nfull + t, carry, True)
        lane = lax.broadcasted_iota(jnp.int32, (1, LANES), 1)
        z = jnp.zeros((T, LANES), F32)
        for hh in range(2):
            m, acc = carry[hh]
            l = acc[:, 64:65]
            o_ref[:, hh * 64:(hh + 1) * 64] = (acc / l)[:, :64]
            z = z + (m * EXP2_SCALE + jnp.log(l) * LOG2E) * (lane == hh).astype(F32)
        l_ref[0] = z.T[0:2, :]

    return pl.pallas_call(
        body, name=name, grid=(HEADS // 2, S // T),
        in_specs=[pl.BlockSpec((T, 256), lambda h, i: (i, h)), pl.BlockSpec((S, 256), lambda h, i: (0, h)),
                  pl.BlockSpec((S, 256), lambda h, i: (0, h))],
        out_specs=[pl.BlockSpec((T, LANES), lambda h, i: (i, h)), pl.BlockSpec((1, 2, T), lambda h, i: (h, 0, i))],
        out_shape=[jax.ShapeDtypeStruct((S, D_SSD), F32), jax.ShapeDtypeStruct((HEADS // 2, 2, S), F32)],
        compiler_params=_cp(("parallel", "parallel")),
    )(qf, kf, va)


def _attn_delta(dmix, ao, name):
    S = ao.shape[0]
    tm = min(512, S)

    def body(d_ref, o_ref, out_ref):
        lane = lax.broadcasted_iota(jnp.int32, (1, LANES), 1)
        lo = (lane < 64).astype(F32)
        for hp in range(HEADS // 2):
            y = d_ref[:, hp * LANES:(hp + 1) * LANES] * o_ref[:, hp * LANES:(hp + 1) * LANES]
            z = (jnp.sum(y * lo, axis=1, keepdims=True) * (lane == 0).astype(F32)
                 + jnp.sum(y * (1.0 - lo), axis=1, keepdims=True) * (lane == 1).astype(F32))
            out_ref[hp] = z.T[0:2, :]

    return pl.pallas_call(
        body, name=name, grid=(S // tm,),
        in_specs=[pl.BlockSpec((tm, D_SSD), lambda i: (i, 0)), pl.BlockSpec((tm, D_SSD), lambda i: (i, 0))],
        out_specs=pl.BlockSpec((HEADS // 2, 2, tm), lambda i: (0, 0, i)),
        out_shape=jax.ShapeDtypeStruct((HEADS // 2, 2, S), F32),
        compiler_params=_cp(("parallel",)),
    )(dmix, ao)


def _attn_bwd(qf, kf, kT, va, do, lse_r, delta_r, name):
    S = qf.shape[0]
    T, TK = min(ATT_BQ, S), min(ATT_BK, S)
    nq = S // T
    nmask = max(1, TK // T)

    def body(q_ref, k_ref, kT_ref, v_ref, do_ref, l_ref, d_ref, dqT_ref, dk_ref, dv_ref):
        j = pl.program_id(1)

        @pl.when(j == 0)
        def _():
            dqT_ref[...] = jnp.zeros_like(dqT_ref)

        r = lax.broadcasted_iota(jnp.int32, (TK, T), 0)
        c = lax.broadcasted_iota(jnp.int32, (TK, T), 1)
        lo = (lax.broadcasted_iota(jnp.int32, (1, LANES), 1) < 64).astype(F32)
        ks = [k_ref[:, hh * LANES:(hh + 1) * LANES] for hh in range(2)]
        vs = [v_ref[:, hh * LANES:(hh + 1) * LANES] for hh in range(2)]
        kTs = [kT_ref[hh * LANES:(hh + 1) * LANES, :] for hh in range(2)]

        def blk(i, carry, masked):
            off = pl.multiple_of(i * T, T)
            dall = do_ref[pl.ds(off, T), :]
            out = []
            for hh in range(2):
                dk, dv = carry[hh]
                q = q_ref[pl.ds(off, T), hh * LANES:(hh + 1) * LANES]
                dop = ((dall if hh == 0 else pltpu.roll(dall, 64, 1)) * lo).astype(BF16)
                lrow = l_ref[0, hh:hh + 1, pl.ds(off, T)]
                drow = d_ref[0, hh:hh + 1, pl.ds(off, T)]
                pT = jnp.exp2(_dot(ks[hh], q, ((1,), (1,))) * EXP2_SCALE - lrow)
                if masked:
                    pT = jnp.where(r + j * TK <= c + i * T, pT, 0.0)
                dpT = _dot(vs[hh], dop, ((1,), (1,)))
                dsT = (pT * (dpT - drow) * ATT_SCALE).astype(BF16)
                dv = dv + _dot(pT.astype(BF16), dop, ((1,), (0,)))
                dk = dk + _dot(dsT, q, ((1,), (0,)))
                dqT_ref[hh * LANES:(hh + 1) * LANES, pl.ds(off, T)] += _dot(kTs[hh], dsT, ((1,), (0,)))
                out.append((dk, dv))
            return tuple(out)

        z = (jnp.zeros((TK, LANES), F32), jnp.zeros((TK, LANES), F32))
        first = lax.div(j * TK, T)
        carry = (z, z)
        for t in range(nmask):
            carry = blk(first + t, carry, True)
        carry = lax.fori_loop(first + nmask, nq, lambda i, cr: blk(i, cr, False), carry)
        for hh in range(2):
            dk_ref[:, hh * LANES:(hh + 1) * LANES] = carry[hh][0]
            dv_ref[:, hh * LANES:(hh + 1) * LANES] = carry[hh][1]

    return pl.pallas_call(
        body, name=name, grid=(HEADS // 2, S // TK),
        in_specs=[pl.BlockSpec((S, 256), lambda h, j: (0, h)), pl.BlockSpec((TK, 256), lambda h, j: (j, h)),
                  pl.BlockSpec((256, TK), lambda h, j: (h, j)), pl.BlockSpec((TK, 256), lambda h, j: (j, h)),
                  pl.BlockSpec((S, LANES), lambda h, j: (0, h)), pl.BlockSpec((1, 2, S), lambda h, j: (h, 0, 0)),
                  pl.BlockSpec((1, 2, S), lambda h, j: (h, 0, 0))],
        out_specs=[pl.BlockSpec((256, S), lambda h, j: (h, 0)), pl.BlockSpec((TK, 256), lambda h, j: (j, h)),
                   pl.BlockSpec((TK, 256), lambda h, j: (j, h))],
        out_shape=[jax.ShapeDtypeStruct((D_MODEL, S), F32), jax.ShapeDtypeStruct((S, D_MODEL), F32),
                   jax.ShapeDtypeStruct((S, D_MODEL), F32)],
        compiler_params=_cp(("parallel", "arbitrary")),
    )(qf, kf, kT, va, do, lse_r, delta_r)


def _shift_down(x, s):
    if s == 0:
        return x
    rows = lax.broadcasted_iota(jnp.int32, x.shape, 0)
    return jnp.where(rows >= s, pltpu.roll(x, s, 0), 0.0)


def _shift_up(x, s):
    if s == 0:
        return x
    n = x.shape[0]
    rows = lax.broadcasted_iota(jnp.int32, x.shape, 0)
    return jnp.where(rows < n - s, pltpu.roll(x, n - s, 0), 0.0)


def _conv_fwd(proj, cvec, name):
    S = proj.shape[0]

    def body(x_ref, c_ref, o_ref):
        x = x_ref[...]
        y = jnp.broadcast_to(c_ref[4:5, :], x.shape)
        for k in range(CONV_TAPS):
            y = y + c_ref[k:k + 1, :] * _shift_down(x, CONV_TAPS - 1 - k)
        o_ref[...] = y * jax.nn.sigmoid(y)

    return pl.pallas_call(
        body, name=name, grid=(D_CONV // LANES,),
        in_specs=[pl.BlockSpec((S, LANES), lambda j: (0, 8 + j)), pl.BlockSpec((8, LANES), lambda j: (0, j))],
        out_specs=pl.BlockSpec((S, LANES), lambda j: (0, j)),
        out_shape=jax.ShapeDtypeStruct((S, D_CONV), F32),
        compiler_params=_cp(("parallel",)),
    )(proj, cvec)


def _conv_bwd(proj, cvec, dact, name):
    S = proj.shape[0]

    def body(x_ref, c_ref, d_ref, dx_ref, dc_ref):
        x = x_ref[...]
        y = jnp.broadcast_to(c_ref[4:5, :], x.shape)
        for k in range(CONV_TAPS):
            y = y + c_ref[k:k + 1, :] * _shift_down(x, CONV_TAPS - 1 - k)
        sg = jax.nn.sigmoid(y)
        dy = d_ref[...] * (sg * (1.0 + y * (1.0 - sg)))
        dx = jnp.zeros_like(x)
        for k in range(CONV_TAPS):
            s = CONV_TAPS - 1 - k
            dx = dx + c_ref[k:k + 1, :] * _shift_up(dy, s)
            dc_ref[k:k + 1, :] = jnp.sum(dy * _shift_down(x, s), axis=0, keepdims=True)
        dx_ref[...] = dx
        dc_ref[4:5, :] = jnp.sum(dy, axis=0, keepdims=True)
        dc_ref[5:8, :] = jnp.zeros((3, LANES), F32)

    return pl.pallas_call(
        body, name=name, grid=(D_CONV // LANES,),
        in_specs=[pl.BlockSpec((S, LANES), lambda j: (0, 8 + j)), pl.BlockSpec((8, LANES), lambda j: (0, j)),
                  pl.BlockSpec((S, LANES), lambda j: (0, j))],
        out_specs=[pl.BlockSpec((S, LANES), lambda j: (0, j)), pl.BlockSpec((8, LANES), lambda j: (0, j))],
        out_shape=[jax.ShapeDtypeStruct((S, D_CONV), F32), jax.ShapeDtypeStruct((8, D_CONV), F32)],
        compiler_params=_cp(("parallel",)),
    )(proj, cvec, dact)


def fn_ssd_chunk(xs, bm, cm, dtr, state, vecs):
    Q = CHUNK
    dt = jax.nn.softplus(dtr + vecs[0:1])
    a = -jnp.exp(vecs[1:2])
    adt = dt * a
    tril = _causal_mask(Q)
    acs = _dot(tril.astype(F32), adt, ((1,), (0,)), lax.Precision.HIGHEST)
    acs_t = acs.T
    alast = acs[Q - 1:Q, :]
    r = lax.broadcasted_iota(jnp.int32, (LANES, D_SSD), 0)
    c = lax.broadcasted_iota(jnp.int32, (LANES, D_SSD), 1)
    spread = (lax.shift_right_logical(c, 6) == r).astype(F32)

    def per_head(v):
        return _dot(v, spread, ((1,), (0,)), lax.Precision.HIGH)

    xdt = xs * per_head(dt)
    ub = (xdt * per_head(jnp.exp(alast - acs))).astype(BF16)
    xdtb = xdt.astype(BF16)
    Bs = [bm[:, g * SSD_N:(g + 1) * SSD_N].astype(BF16) for g in range(2)]
    Cs = [cm[:, g * SSD_N:(g + 1) * SSD_N].astype(BF16) for g in range(2)]
    Gs = [_dot(Cs[g], Bs[g], ((1,), (1,))) for g in range(2)]
    yds, yos, adds = [], [], []
    for h in range(SSD_HEADS):
        g = h // (SSD_HEADS // 2)
        sl = slice(h * SSD_P, (h + 1) * SSD_P)
        L = jnp.exp(jnp.where(tril, acs[:, h:h + 1] - acs_t[h:h + 1, :], -jnp.inf))
        yds.append(_dot((Gs[g] * L).astype(BF16), xdtb[:, sl], ((1,), (0,))))
        yos.append(_dot(Cs[g], state[h].astype(BF16), ((1,), (1,))))
        adds.append(_dot(ub[:, sl], Bs[g], ((0,), (0,))))
    y = jnp.concatenate(yds, axis=1) + jnp.concatenate(yos, axis=1) * per_head(jnp.exp(acs)) + per_head(vecs[2:3]) * xs
    decay = jnp.stack([jnp.broadcast_to(jnp.exp(alast[:, h:h + 1]), (SSD_P, SSD_N)) for h in range(SSD_HEADS)])
    return y, jnp.stack(adds) + state * decay


def _ssd_fwd(xact, proj, svec, name):
    S = xact.shape[0]
    nc = S // CHUNK

    def body(x_ref, dt_ref, v_ref, y_ref, st_ref, state):
        @pl.when(pl.program_id(0) == 0)
        def _():
            state[...] = jnp.zeros_like(state)

        st_ref[0] = state[...]
        x = x_ref[...]
        y, sn = fn_ssd_chunk(x[:, 0:512], x[:, 512:768], x[:, 768:1024], dt_ref[...], state[...], v_ref[...])
        y_ref[...] = y
        state[...] = sn

    return pl.pallas_call(
        body, name=name, grid=(nc,),
        in_specs=[pl.BlockSpec((CHUNK, D_CONV), lambda i: (i, 0)), pl.BlockSpec((CHUNK, LANES), lambda i: (i, 16)),
                  pl.BlockSpec((8, LANES), lambda i: (0, 0))],
        out_specs=[pl.BlockSpec((CHUNK, D_SSD), lambda i: (i, 0)),
                   pl.BlockSpec((1, SSD_HEADS, SSD_P, SSD_N), lambda i: (i, 0, 0, 0))],
        out_shape=[jax.ShapeDtypeStruct((S, D_SSD), F32), jax.ShapeDtypeStruct((nc, SSD_HEADS, SSD_P, SSD_N), F32)],
        scratch_shapes=[pltpu.VMEM((SSD_HEADS, SSD_P, SSD_N), F32)],
        compiler_params=_cp(("arbitrary",)),
    )(xact, proj, svec)


def _ssd_bwd(xact, proj, svec, states, dy, name):
    S = xact.shape[0]
    nc = S // CHUNK

    def body(x_ref, dt_ref, v_ref, st_ref, dy_ref, dx_ref, ddt_ref, dv_ref, dstate):
        @pl.when(pl.program_id(0) == 0)
        def _():
            dstate[...] = jnp.zeros_like(dstate)
            dv_ref[...] = jnp.zeros_like(dv_ref)

        x = x_ref[...]
        _, vjp = jax.vjp(fn_ssd_chunk, x[:, 0:512], x[:, 512:768], x[:, 768:1024], dt_ref[...], st_ref[0], v_ref[...])
        dxs, dbm, dcm, ddt, dst, dvec = vjp((dy_ref[...], dstate[...]))
        dx_ref[:, 0:512] = dxs
        dx_ref[:, 512:768] = dbm
        dx_ref[:, 768:1024] = dcm
        ddt_ref[...] = ddt
        dstate[...] = dst
        dv_ref[...] += dvec

    rev = lambda i: (nc - 1 - i, 0)
    return pl.pallas_call(
        body, name=name, grid=(nc,),
        in_specs=[pl.BlockSpec((CHUNK, D_CONV), rev), pl.BlockSpec((CHUNK, LANES), lambda i: (nc - 1 - i, 16)),
                  pl.BlockSpec((8, LANES), lambda i: (0, 0)),
                  pl.BlockSpec((1, SSD_HEADS, SSD_P, SSD_N), lambda i: (nc - 1 - i, 0, 0, 0)),
                  pl.BlockSpec((CHUNK, D_SSD), rev)],
        out_specs=[pl.BlockSpec((CHUNK, D_CONV), rev), pl.BlockSpec((CHUNK, LANES), rev),
                   pl.BlockSpec((8, LANES), lambda i: (0, 0))],
        out_shape=[jax.ShapeDtypeStruct((S, D_CONV), F32), jax.ShapeDtypeStruct((S, LANES), F32),
                   jax.ShapeDtypeStruct((8, LANES), F32)],
        scratch_shapes=[pltpu.VMEM((SSD_HEADS, SSD_P, SSD_N), F32)],
        compiler_params=_cp(("arbitrary",)),
    )(xact, proj, svec, states, dy)


def _ada_fwd(c_all, w_ada, b_sh, name):
    nb = 1536 // 512

    def body(c_ref, w_ref, b_ref, o_ref):
        ca = jax.nn.silu(c_ref[...]).astype(BF16)
        o_ref[0] = _dot(ca, w_ref[0].astype(BF16), ((1,), (0,))) + b_ref[0]

    return pl.pallas_call(
        body, name=name, grid=(DEPTH, nb),
        in_specs=[pl.BlockSpec((8, D_MODEL), lambda l, j: (0, 0)), pl.BlockSpec((1, D_MODEL, 512), lambda l, j: (l, 0, j)),
                  pl.BlockSpec((1, 1, 512), lambda l, j: (l, 0, j))],
        out_specs=pl.BlockSpec((1, 8, 512), lambda l, j: (l, 0, j)),
        out_shape=jax.ShapeDtypeStruct((DEPTH, 8, 1536), F32),
        compiler_params=_cp(("parallel", "parallel")),
    )(c_all, w_ada, b_sh)


def _ada_bwd_adam(c_all_t, dmod_sh, w, m, v, name):
    nb = 1536 // 512

    def body(c_ref, d_ref, w_ref, m_ref, v_ref, g_ref, dl_ref, nm_ref, nv_ref):
        ca = jax.nn.silu(c_ref[...])
        g = ca[:, 0:1] * d_ref[0, 0:1, :]
        for b in range(1, 8):
            g = g + ca[:, b:b + 1] * d_ref[0, b:b + 1, :]
        g_ref[0] = g
        dl_ref[0], nm_ref[0], nv_ref[0] = _adam_update(w_ref[0], m_ref[0], v_ref[0], g)

    blk = pl.BlockSpec((1, D_MODEL, 512), lambda l, j: (l, 0, j))
    return pl.pallas_call(
        body, name=name, grid=(DEPTH, nb),
        in_specs=[pl.BlockSpec((D_MODEL, 8), lambda l, j: (0, 0)), pl.BlockSpec((1, 8, 512), lambda l, j: (l, 0, j)), blk, blk, blk],
        out_specs=[blk] * 4, out_shape=[jax.ShapeDtypeStruct((DEPTH, D_MODEL, 1536), F32)] * 4,
        compiler_params=_cp(("parallel", "parallel")),
    )(c_all_t, dmod_sh, w, m, v)


def _rows_tile(rows):
    return next(t for t in (512, 256, 128, 64, 32, 16, 8) if rows % t == 0)


SUM_BLOCKS = 4
ADAM_BLOCKS = 8


def _sum_sibling(gs, ls, ci, name):
    n = len(gs)

    def body(c_ref, *refs):
        for p in range(n):
            refs[2 * n + p][...] = refs[2 * p][...] + refs[2 * p + 1][...]

    in_specs, out_specs, out_shape = [], [], []
    for g in gs:
        _, _, rh, cw = g.shape
        rb = rh // SUM_BLOCKS
        in_specs += [pl.BlockSpec((None, None, rb, cw), lambda s, i, c: (s, c[0], i, 0)),
                     pl.BlockSpec((None, rb, cw), lambda s, i, c: (s, i, 0))]
        out_specs.append(pl.BlockSpec((None, rb, cw), lambda s, i, c: (s, i, 0)))
        out_shape.append(jax.ShapeDtypeStruct((4, rh, cw), F32))
    ops = [a for pair in zip(gs, ls) for a in pair]
    return pl.pallas_call(
        body, name=name,
        grid_spec=pltpu.PrefetchScalarGridSpec(num_scalar_prefetch=1, grid=(4, SUM_BLOCKS), in_specs=in_specs, out_specs=out_specs),
        out_shape=out_shape, compiler_params=_cp(("parallel", "parallel")),
    )(ci.reshape(1).astype(jnp.int32), *ops)


def _sum_chips(cs, lands, chip, ci, name):
    n = len(cs)

    def body(c_ref, *refs):
        for p in range(n):
            a = refs[4 * p:4 * p + 4]
            refs[4 * n + p][...] = ((a[0][...] + a[1][...]) + a[2][...]) + a[3][...]

    in_specs, out_specs, out_shape = [], [], []
    for c in cs:
        _, rh, cw = c.shape
        rb = rh // SUM_BLOCKS
        in_specs.append(pl.BlockSpec((None, rb, cw), lambda i, ch: (ch[0], i, 0)))
        in_specs += [pl.BlockSpec((None, rb, cw), functools.partial(lambda i, ch, k: (k, i, 0), k=k)) for k in range(3)]
        out_specs.append(pl.BlockSpec((None, rb, cw), lambda i, ch: (ch[1], i, 0)))
        out_shape.append(jax.ShapeDtypeStruct((2, rh, cw), F32))
    ops = [a for c, l in zip(cs, lands) for a in (c, l, l, l)]
    return pl.pallas_call(
        body, name=name,
        grid_spec=pltpu.PrefetchScalarGridSpec(num_scalar_prefetch=1, grid=(SUM_BLOCKS,), in_specs=in_specs, out_specs=out_specs),
        out_shape=out_shape, compiler_params=_cp(("parallel",)),
    )(jnp.stack([chip, ci]).astype(jnp.int32), *ops)


def _adam_update(w, m, v, g):
    c1 = 1.0 / (1.0 - ADAM_B1 ** ADAM_STEP)
    c2 = 1.0 / (1.0 - ADAM_B2 ** ADAM_STEP)
    nm = ADAM_B1 * m + (1.0 - ADAM_B1) * g
    nv = ADAM_B2 * v + (1.0 - ADAM_B2) * (g * g)
    return -ADAM_LR * ((nm * c1) / (jnp.sqrt(nv * c2) + ADAM_EPS) + ADAM_WD * w), nm, nv


def _adam_multi(ws, ms, vs, gs, name):
    n = len(ws)
    per = 3 + DEPTH

    def body(*refs):
        layer = pl.program_id(0)
        for p in range(n):
            w, m, v = [refs[per * p + k][...] for k in range(3)]
            g = refs[per * p + 3][...]
            for l in range(1, DEPTH):
                g = jnp.where(layer == l, refs[per * p + 3 + l][...], g)
            d, nm, nv = _adam_update(w, m, v, g)
            for k, val in enumerate((g, d, nm, nv)):
                refs[per * n + 4 * p + k][...] = val

    in_specs, out_specs, out_shape = [], [], []
    for w in ws:
        _, r, cw = w.shape
        if r % (8 * ADAM_BLOCKS) == 0:
            spec = pl.BlockSpec((None, r // ADAM_BLOCKS, cw), lambda l, i: (l, i, 0))
            gspec = pl.BlockSpec((r // ADAM_BLOCKS, cw), lambda l, i: (i, 0))
        else:
            spec = pl.BlockSpec((None, r, cw // ADAM_BLOCKS), lambda l, i: (l, 0, i))
            gspec = pl.BlockSpec((r, cw // ADAM_BLOCKS), lambda l, i: (0, i))
        in_specs += [spec] * 3 + [gspec] * DEPTH
        out_specs += [spec] * 4
        out_shape += [jax.ShapeDtypeStruct(w.shape, F32)] * 4
    ops = [a for w, m, v, g in zip(ws, ms, vs, gs) for a in (w, m, v, *g)]
    res = pl.pallas_call(
        body, name=name, grid=(DEPTH, ADAM_BLOCKS), in_specs=in_specs, out_specs=out_specs, out_shape=out_shape,
        compiler_params=_cp(("parallel", "parallel")),
    )(*ops)
    return res[0::4], res[1::4], res[2::4], res[3::4]


def _adam(w, m, v, parts, name):
    rows, width = w.shape
    bm = min(256, _rows_tile(rows))
    np_ = len(parts)
    c1 = 1.0 / (1.0 - ADAM_B1 ** ADAM_STEP)
    c2 = 1.0 / (1.0 - ADAM_B2 ** ADAM_STEP)

    def body(*refs):
        w_ref, m_ref, v_ref = refs[:3]
        g = refs[3][...]
        for r in refs[4:3 + np_]:
            g = g + r[...]
        g_ref, d_ref, nm_ref, nv_ref = refs[3 + np_:]
        nm = ADAM_B1 * m_ref[...] + (1.0 - ADAM_B1) * g
        nv = ADAM_B2 * v_ref[...] + (1.0 - ADAM_B2) * (g * g)
        g_ref[...] = g
        nm_ref[...] = nm
        nv_ref[...] = nv
        d_ref[...] = -ADAM_LR * ((nm * c1) / (jnp.sqrt(nv * c2) + ADAM_EPS) + ADAM_WD * w_ref[...])

    blk = pl.BlockSpec((bm, width), lambda i: (i, 0))
    return pl.pallas_call(
        body, name=name, grid=(rows // bm,),
        in_specs=[blk, blk, blk] + [pl.BlockSpec((bm, width), functools.partial(lambda i, o: (i + o, 0), o=off // bm))
                                    for (_, off) in parts],
        out_specs=[blk, blk, blk, blk],
        out_shape=[jax.ShapeDtypeStruct((rows, width), F32)] * 4,
        compiler_params=_cp(("parallel",)),
    )(w, m, v, *[p[0] for p in parts])


def _coords():
    return lax.axis_index("x"), lax.axis_index("y"), lax.axis_index("c")


def _other_chips(x, y):
    return [(1 - x, y), (x, 1 - y), (1 - x, 1 - y)]


def _ag8(blk, name):
    m_per, n = blk.shape

    def body(x_ref, out_ref, send_sems, recv_sems, local_sem):
        x, y, c = _coords()
        me, sibling = (x, y, c), (x, y, 1 - c)
        chips = _other_chips(x, y)

        def rows(px, py, pc):
            return out_ref.at[pl.ds((4 * px + 2 * py + pc) * m_per, m_per), :]

        def copy(k, block, to, src=None):
            return pltpu.make_async_remote_copy(
                src_ref=rows(*block) if src is None else src, dst_ref=rows(*block),
                send_sem=send_sems.at[k], recv_sem=recv_sems.at[k], device_id=to, device_id_type=MESH)

        mine = pltpu.make_async_copy(x_ref, rows(*me), local_sem)
        mine.start()
        first = [copy(0, me, sibling, src=x_ref)]
        first += [copy(1 + j, me, (*chip, c), src=x_ref) for j, chip in enumerate(chips)]
        for cp in first:
            cp.start()
        passed = [copy(4 + j, (*chip, c), sibling) for j, chip in enumerate(chips)]
        for j, chip in enumerate(chips):
            copy(1 + j, (*chip, c), me).wait_recv()
            passed[j].start()
        copy(0, sibling, me).wait_recv()
        for j, chip in enumerate(chips):
            copy(4 + j, (*chip, 1 - c), me).wait_recv()
        for cp in first + passed:
            cp.wait_send()
        mine.wait()

    return pl.pallas_call(
        body, name=name,
        out_shape=jax.ShapeDtypeStruct((8 * m_per, n), blk.dtype),
        in_specs=[pl.BlockSpec(memory_space=pltpu.VMEM)], out_specs=pl.BlockSpec(memory_space=pltpu.VMEM),
        scratch_shapes=[pltpu.SemaphoreType.DMA((7,)), pltpu.SemaphoreType.DMA((7,)), pltpu.SemaphoreType.DMA],
    )(blk)


HBM_SPEC = pl.BlockSpec(memory_space=pltpu.HBM)
SEM_SPEC = pl.BlockSpec(memory_space=pltpu.SEMAPHORE)
EFFECT = pltpu.SideEffectType.DATAFLOW_SIDE_EFFECTING


def _remote(src, dst, send_sem, recv_sem, to):
    return pltpu.make_async_remote_copy(src_ref=src, dst_ref=dst, send_sem=send_sem, recv_sem=recv_sem,
                                        device_id=to, device_id_type=MESH)


def _ag_list(shards, name):
    n = len(shards)

    def body(*refs):
        sh, out = refs[:n], refs[n:2 * n]
        send_sems, recv_sems = refs[2 * n:]
        x, y, c = _coords()
        sibling = (x, y, 1 - c)
        chips = _other_chips(x, y)
        first = [_remote(sh[p].at[c], out[p].at[2 * x + y, c], send_sems.at[6 * p + j], recv_sems.at[6 * p + j], (px, py, c))
                 for p in range(n) for j, (px, py) in enumerate(chips)]
        for cp in first:
            cp.start()
        passed = []
        for j, (px, py) in enumerate(chips):
            for p in range(n):
                got = out[p].at[2 * px + py, c]
                _remote(got, got, send_sems.at[6 * p + j], recv_sems.at[6 * p + j], (x, y, c)).wait_recv()
                cp = _remote(got, got, send_sems.at[6 * p + 3 + j], recv_sems.at[6 * p + 3 + j], sibling)
                cp.start()
                passed.append(cp)
        for j, (px, py) in enumerate(chips):
            for p in range(n):
                got = out[p].at[2 * px + py, 1 - c]
                _remote(got, got, send_sems.at[6 * p + 3 + j], recv_sems.at[6 * p + 3 + j], (x, y, c)).wait_recv()
        for cp in first + passed:
            cp.wait_send()

    return pl.pallas_call(
        body, name=name,
        out_shape=[jax.ShapeDtypeStruct((4,) + s.shape, s.dtype) for s in shards],
        in_specs=[pl.BlockSpec(memory_space=pl.ANY)] * n, out_specs=[pl.BlockSpec(memory_space=pl.ANY)] * n,
        scratch_shapes=[pltpu.SemaphoreType.DMA((6 * n,)), pltpu.SemaphoreType.DMA((6 * n,))],
    )(*shards)


def _ag_direct_copies(sh, land, send_sems, recv_sems, starting):
    x, y, c = _coords()
    return [_remote(sh[p], land[p].at[2 * x + y] if starting else land[p].at[2 * px + py],
                    send_sems.at[3 * p + j], recv_sems.at[3 * p + j], (px, py, c))
            for p in range(len(sh)) for j, (px, py) in enumerate(_other_chips(x, y))]


def _rs_sibling_copies(g, land, send_sems, recv_sems, starting):
    x, y, c = _coords()
    return [_remote(g[p].at[s, 1 - c], land[p].at[s], send_sems.at[4 * p + s], recv_sems.at[4 * p + s], (x, y, 1 - c))
            for p in range(len(g)) for s in range(4)]


def _rs_chips_copies(cs, land, send_sems, recv_sems, starting):
    x, y, c = _coords()
    return [_remote(cs[p].at[2 * px + py], land[p].at[j], send_sems.at[3 * p + j], recv_sems.at[3 * p + j], (px, py, c))
            for p in range(len(cs)) for j, (px, py) in enumerate(_other_chips(x, y))]


def _split_start(copies, srcs, land_shapes, per, name):
    n = len(srcs)

    def body(*refs):
        for cp in copies(refs[:n], refs[n:2 * n], refs[2 * n], refs[2 * n + 1], True):
            cp.start()
        token = refs[4 * n + 2]
        token[...] = jnp.zeros_like(token)

    lands = [pltpu.with_memory_space_constraint(lax.empty(shp, s.dtype), pltpu.HBM) for shp, s in zip(land_shapes, srcs)]
    res = pl.pallas_call(
        body, name=name,
        out_shape=(pltpu.SemaphoreType.DMA((per * n,)), pltpu.SemaphoreType.DMA((per * n,)))
        + tuple(pltpu.HBM(s.shape, s.dtype) for s in srcs) + tuple(pltpu.HBM(l.shape, l.dtype) for l in lands)
        + (jax.ShapeDtypeStruct((8, LANES), F32),),
        in_specs=(HBM_SPEC,) * (2 * n), out_specs=(SEM_SPEC, SEM_SPEC) + (HBM_SPEC,) * (2 * n) + (pl.BlockSpec(memory_space=pltpu.VMEM),),
        input_output_aliases={i: 2 + i for i in range(2 * n)},
        compiler_params=pltpu.CompilerParams(has_side_effects=EFFECT),
    )(*[pltpu.with_memory_space_constraint(s, pltpu.HBM) for s in srcs], *lands)
    return res[0], res[1], res[2:2 + n], res[2 + n:2 + 2 * n], res[2 + 2 * n]


def _split_wait(copies, send_sems, recv_sems, src_thru, land_thru, after, name):
    n = len(src_thru)

    def body(*refs):
        for cp in copies(refs[:n], refs[n:2 * n], refs[2 * n], refs[2 * n + 1], False):
            cp.wait_send()
            cp.wait_recv()

    res = pl.pallas_call(
        body, name=name,
        out_shape=tuple(pltpu.HBM(s.shape, s.dtype) for s in src_thru) + tuple(pltpu.HBM(l.shape, l.dtype) for l in land_thru),
        in_specs=(HBM_SPEC,) * (2 * n) + (SEM_SPEC, SEM_SPEC, pl.BlockSpec(memory_space=pl.ANY)),
        out_specs=(HBM_SPEC,) * (2 * n), input_output_aliases={i: i for i in range(2 * n)},
        compiler_params=pltpu.CompilerParams(has_side_effects=EFFECT),
    )(*src_thru, *land_thru, send_sems, recv_sems, after)
    return res[:n], res[n:]


def _ag_direct_start(shards, name):
    return _split_start(_ag_direct_copies, shards, [(4,) + s.shape for s in shards], 3, name)


def _ag_direct_wait(send_sems, recv_sems, sh_thru, land_thru, after, name):
    return _split_wait(_ag_direct_copies, send_sems, recv_sems, sh_thru, land_thru, after, name)[1]


def _rs_sibling_start(gs, name):
    return _split_start(_rs_sibling_copies, gs, [(4,) + g.shape[2:] for g in gs], 4, name)


def _rs_sibling_wait(send_sems, recv_sems, g_thru, land_thru, after, name):
    return _split_wait(_rs_sibling_copies, send_sems, recv_sems, g_thru, land_thru, after, name)


def _rs_sibling_list(gs, name):
    n = len(gs)

    def body(*refs):
        cps = _rs_sibling_copies(refs[:n], refs[n:2 * n], refs[2 * n], refs[2 * n + 1], True)
        for cp in cps:
            cp.start()
        for cp in cps:
            cp.wait_recv()
        for cp in cps:
            cp.wait_send()

    return pl.pallas_call(
        body, name=name,
        out_shape=[jax.ShapeDtypeStruct((4,) + g.shape[2:], g.dtype) for g in gs],
        in_specs=[pl.BlockSpec(memory_space=pl.ANY)] * n, out_specs=[pl.BlockSpec(memory_space=pl.ANY)] * n,
        scratch_shapes=[pltpu.SemaphoreType.DMA((4 * n,)), pltpu.SemaphoreType.DMA((4 * n,))],
    )(*gs)


def _rs_chips_start(cs, name):
    return _split_start(_rs_chips_copies, cs, [(3,) + c.shape[1:] for c in cs], 3, name)


def _rs_chips_wait(send_sems, recv_sems, cs_thru, land_thru, after, name):
    return _split_wait(_rs_chips_copies, send_sems, recv_sems, cs_thru, land_thru, after, name)


def _swap_list(ghs, name):
    n = len(ghs)

    def body(*refs):
        g, out, send_sems, recv_sems = refs[:n], refs[n:2 * n], refs[2 * n], refs[2 * n + 1]
        x, y, c = _coords()
        cps = [_remote(g[p].at[c], out[p].at[c], send_sems.at[p], recv_sems.at[p], (x, y, 1 - c)) for p in range(n)]
        for cp in cps:
            cp.start()
        for p in range(n):
            _remote(g[p].at[c], out[p].at[1 - c], send_sems.at[p], recv_sems.at[p], (x, y, 1 - c)).wait_recv()
        for cp in cps:
            cp.wait_send()

    return pl.pallas_call(
        body, name=name,
        out_shape=[jax.ShapeDtypeStruct(g.shape, g.dtype) for g in ghs],
        in_specs=[pl.BlockSpec(memory_space=pl.ANY)] * n, out_specs=[pl.BlockSpec(memory_space=pl.ANY)] * n,
        input_output_aliases={p: p for p in range(n)},
        scratch_shapes=[pltpu.SemaphoreType.DMA((n,)), pltpu.SemaphoreType.DMA((n,))],
    )(*ghs)


def _pad_win(w):
    return jnp.concatenate([w[:, :416], jnp.zeros((w.shape[0], 96), w.dtype), w[:, 416:1952],
                            w[:, 1952:1960], jnp.zeros((w.shape[0], 120), w.dtype)], axis=1)


def _unpad_win(g):
    return jnp.concatenate([g[:, :416], g[:, 512:2048], g[:, 2048:2056]], axis=1)


def _pad_wq(w):
    return jnp.pad(w.reshape(Q_LORA, HEADS, QK_DIM), ((0, 0), (0, 0), (0, LANES - QK_DIM))).reshape(Q_LORA, HEADS * LANES)


def _unpad_wq(g):
    return g.reshape(Q_LORA, HEADS, LANES)[:, :, :QK_DIM].reshape(Q_LORA, HEADS * QK_DIM)


def _cols_to_shards(a):
    r, c4 = a.shape
    return a.reshape(r, 4, c4 // 4).transpose(1, 0, 2)


def _shards_to_cols(a):
    _, r, c = a.shape
    return a.transpose(1, 0, 2).reshape(r, 4 * c)


def _pack_small(tree):
    parts = []
    for l in range(DEPTH):
        for (n, k) in SMALL:
            parts.append(jnp.pad(tree[n][l].reshape(-1), (0, -k % LANES)))
    flat = jnp.concatenate(parts)
    return jnp.pad(flat, (0, SMALL_ROWS * LANES - flat.shape[0])).reshape(SMALL_ROWS, LANES)


def _unpack_small(buf):
    flat = buf.reshape(-1)
    out = {n: [] for (n, _) in SMALL}
    o = 0
    for l in range(DEPTH):
        for (n, k) in SMALL:
            out[n].append(flat[o:o + k])
            o += k + (-k % LANES)
    return {n: jnp.stack(v) for n, v in out.items()}


def _vec(v, width=LANES):
    return jnp.pad(v.reshape(1, -1), ((0, 0), (0, width - v.shape[-1])))


def kernel(x, c, positions, norm1_w, norm2_w, w_ada, b_ada, w_in, q_a_norm_w, w_q_up, kv_a_norm_w, w_kv_up, q_nope_norm_w, q_pe_norm_w, k_nope_norm_w, k_pe_norm_w, conv_w, conv_b, dt_bias, a_log, d_skip, ssd_norm_w, w_out, w_gate_up, w_down, loss_target, m_norm1_w, m_norm2_w, m_w_ada, m_b_ada, m_w_in, m_q_a_norm_w, m_w_q_up, m_kv_a_norm_w, m_w_kv_up, m_q_nope_norm_w, m_q_pe_norm_w, m_k_nope_norm_w, m_k_pe_norm_w, m_conv_w, m_conv_b, m_dt_bias, m_a_log, m_d_skip, m_ssd_norm_w, m_w_out, m_w_gate_up, m_w_down, v_norm1_w, v_norm2_w, v_w_ada, v_b_ada, v_w_in, v_q_a_norm_w, v_w_q_up, v_kv_a_norm_w, v_w_kv_up, v_q_nope_norm_w, v_q_pe_norm_w, v_k_nope_norm_w, v_k_pe_norm_w, v_conv_w, v_conv_b, v_dt_bias, v_a_log, v_d_skip, v_ssd_norm_w, v_w_out, v_w_gate_up, v_w_down):
    W = dict(zip(WEIGHTS, (norm1_w, norm2_w, w_ada, b_ada, w_in, q_a_norm_w, w_q_up, kv_a_norm_w, w_kv_up, q_nope_norm_w, q_pe_norm_w, k_nope_norm_w, k_pe_norm_w, conv_w, conv_b, dt_bias, a_log, d_skip, ssd_norm_w, w_out, w_gate_up, w_down)))
    M = dict(zip(WEIGHTS, (m_norm1_w, m_norm2_w, m_w_ada, m_b_ada, m_w_in, m_q_a_norm_w, m_w_q_up, m_kv_a_norm_w, m_w_kv_up, m_q_nope_norm_w, m_q_pe_norm_w, m_k_nope_norm_w, m_k_pe_norm_w, m_conv_w, m_conv_b, m_dt_bias, m_a_log, m_d_skip, m_ssd_norm_w, m_w_out, m_w_gate_up, m_w_down)))
    V = dict(zip(WEIGHTS, (v_norm1_w, v_norm2_w, v_w_ada, v_b_ada, v_w_in, v_q_a_norm_w, v_w_q_up, v_kv_a_norm_w, v_w_kv_up, v_q_nope_norm_w, v_q_pe_norm_w, v_k_nope_norm_w, v_k_pe_norm_w, v_conv_w, v_conv_b, v_dt_bias, v_a_log, v_d_skip, v_ssd_norm_w, v_w_out, v_w_gate_up, v_w_down)))
    S = x.shape[1]
    xi, yi, ci = _coords()
    chip = 2 * xi + yi
    dev = 2 * chip + ci
    x0 = x[0]
    tgt = loss_target[0]

    inv_freq = 1.0 / (ROPE_THETA ** (jnp.arange(0, ROPE, 2, dtype=F32) / ROPE))
    ang = positions[0].astype(F32)[:, None] * inv_freq
    cos, sin = jnp.cos(ang), jnp.sin(ang)
    z16, z32, z64 = jnp.zeros((S, 16), F32), jnp.zeros((S, 32), F32), jnp.zeros((S, 64), F32)
    tab_c = jnp.concatenate([jnp.ones((S, 64), F32), cos, cos, z32], axis=1)
    tab_s1 = jnp.concatenate([z64, z16, sin, z32], axis=1)
    tab_s2 = jnp.concatenate([z64, -sin, z16, z32], axis=1)

    blk0 = jnp.concatenate([c.reshape(-1), W['conv_w'].reshape(-1)]).reshape(24, LANES)
    g0 = _ag8(blk0, "ag_c_conv").reshape(8, 24 * LANES)
    c_all = g0[:, :D_MODEL]
    conv_full = g0[0::2, D_MODEL:].reshape(4, DEPTH, CONV_TAPS, 256).transpose(1, 2, 0, 3).reshape(DEPTH, CONV_TAPS, D_CONV)

    sh = [{n: W[n][l].astype(BF16) for n in BIG} for l in range(DEPTH)]
    got_first = _ag_list([sh[0][n].reshape(2, sh[0][n].shape[0] // 2, sh[0][n].shape[1]) for n in FIRST], "ag_w0_first")
    to_operand = dict(w_in=lambda a: _pad_win(_shards_to_cols(a)), w_q_up=lambda a: _pad_wq(_shards_to_cols(a)),
                      w_kv_up=_shards_to_cols, w_out=lambda a: a.reshape(D_MODEL, D_MODEL), w_gate_up=lambda a: a,
                      w_down=lambda a: a.reshape(D_FF, D_MODEL))

    def layer_weights(names, gathered, own):
        return {n: to_operand[n](lax.dynamic_update_slice_in_dim(a.reshape(4, -1, a.shape[-1]), own[n][None], chip, axis=0))
                for n, a in zip(names, gathered)}

    LW = [layer_weights(FIRST, got_first, sh[0]), None]

    b_sh = lax.dynamic_slice_in_dim(W['b_ada'], chip * 1536, 1536, axis=1).reshape(DEPTH, 1, 1536)
    mod_sh = _ada_fwd(c_all, W['w_ada'], b_sh, "ada_fwd")
    g1 = _ag8(mod_sh.reshape(192, LANES), "ag_mod").reshape(8, DEPTH, 8, 1536)
    mod_all = g1[0::2].transpose(1, 2, 0, 3).reshape(DEPTH, 8, 6 * D_MODEL)
    mod = lax.dynamic_index_in_dim(mod_all, dev, axis=1, keepdims=False)
    mod, rest0 = lax.optimization_barrier((mod, [sh[0][n] for n in REST]))
    ag0 = _ag_direct_start(rest0, "ag_w0_rest_start")

    def mvec(l, k):
        return mod[l, k * D_MODEL:(k + 1) * D_MODEL].reshape(1, D_MODEL)

    def small(name, l, width=None):
        v = W[name][l]
        return _vec(v, width or v.shape[-1])

    def wq_vec(l):
        return _vec(jnp.concatenate([W['q_nope_norm_w'][l], W['q_pe_norm_w'][l]]))

    def wk_vec(l):
        return _vec(jnp.concatenate([W['k_nope_norm_w'][l], W['k_pe_norm_w'][l]]))

    def conv_vec(l):
        return jnp.concatenate([conv_full[l], W['conv_b'][l].reshape(1, D_CONV), jnp.zeros((3, D_CONV), F32)], axis=0)

    def ssd_vec(l):
        return jnp.concatenate([_vec(W['dt_bias'][l]), _vec(W['a_log'][l]), _vec(W['d_skip'][l]), jnp.zeros((5, LANES), F32)], axis=0)

    sv = []
    xcur = x0
    h1 = _row_fwd(fn_norm_mod, "norm_mod_f", [(x0, 0, D_MODEL)], [small('norm1_w', 0) + ag0[4][0, 0], mvec(0, 1), mvec(0, 0)],
                  [(D_MODEL, BF16)])[0]
    fin = None
    ag_first = None
    ag_rest = ag0
    for l in range(DEPTH):
        if l == 1:
            LW[1] = layer_weights(FIRST, _ag_direct_wait(*ag_first[:4], xcur, "ag_w1_first_wait"), sh[1])
        lw = LW[l]
        t = dict(xcur=xcur, h1=h1)
        t['proj'] = proj = _mm(h1, lw['w_in'], 'nn', f"mm_in_{l}")
        t['qa_n'], t['kva_n'] = _row_fwd(fn_lat_norm, f"lat_norm_f{l}", [(proj, 0, 256), (proj, 2, 128)],
                                         [small('q_a_norm_w', l), small('kv_a_norm_w', l)], [(256, BF16), (128, BF16)])
        t['q'] = _mm(t['qa_n'], lw['w_q_up'], 'nn', f"mm_q_{l}")
        t['kv'] = _mm(t['kva_n'], lw['w_kv_up'], 'nn', f"mm_kv_{l}")
        t['qf'], t['kf'], t['vv'], t['kT'] = _row_fwd(
            fn_qk_prep_kt, f"qk_prep_f{l}",
            [(t['q'], 0, 1024), (t['kv'], 0, 1024), (proj, 3, 128), (tab_c, 0, 128), (tab_s1, 0, 128), (tab_s2, 0, 128)],
            [wq_vec(l), wk_vec(l)], [(1024, BF16), (1024, BF16), (1024, BF16), (1024, BF16)], transposed=(3,))
        t['ao'], t['lse'] = _attn_fwd(t['qf'], t['kf'], t['vv'], f"attn_f{l}")
        t['xact'] = _conv_fwd(proj, conv_vec(l), f"conv_f{l}")
        t['y'], t['states'] = _ssd_fwd(t['xact'], proj, ssd_vec(l), f"ssd_f{l}")
        tie = 0.0
        t['ao'], t['y'] = lax.optimization_barrier((t['ao'], t['y']))
        rest = list(_ag_direct_wait(*ag_rest[:4], t['y'], f"ag_w{l}_rest_wait"))
        if l == 0:
            rest, sh1f, sh1r = lax.optimization_barrier((rest, [sh[1][n] for n in FIRST], [sh[1][n] for n in REST]))
            ag_first = _ag_direct_start(sh1f, "ag_w1_first_start")
            ag_rest = _ag_direct_start(sh1r, "ag_w1_rest_start")
            tie = ag_first[4][0, 0] + ag_rest[4][0, 0]
        lw.update(layer_weights(REST, rest, sh[l]))
        t['mix'] = _row_fwd(fn_gated_mix, f"gated_f{l}", [(t['y'], 0, 512), (proj, 1, 512), (t['ao'], 0, 512)],
                            [small('ssd_norm_w', l) + tie], [(1024, BF16)])[0]
        t['mo'] = _mm(t['mix'], lw['w_out'], 'nn', f"mm_out_{l}")
        t['x1'], t['h2'] = _row_fwd(fn_resid_norm, f"resid_mid_f{l}", [(xcur, 0, D_MODEL), (t['mo'], 0, D_MODEL)],
                                    [mvec(l, 2), small('norm2_w', l), mvec(l, 4), mvec(l, 3)],
                                    [(D_MODEL, F32), (D_MODEL, BF16)])
        t['gu'], t['act'] = _mm_gu_swiglu(t['h2'], lw['w_gate_up'], f"mm_gu_{l}")
        t['ff'] = _mm(t['act'], lw['w_down'], 'nn', f"mm_down_{l}")
        if l + 1 < DEPTH:
            xcur, h1 = _row_fwd(fn_resid_norm, f"resid_end_f{l}", [(t['x1'], 0, D_MODEL), (t['ff'], 0, D_MODEL)],
                                [mvec(l, 5), small('norm1_w', l + 1), mvec(l + 1, 1), mvec(l + 1, 0)],
                                [(D_MODEL, F32), (D_MODEL, BF16)])
        else:
            fin = _final(t['x1'], t['ff'], tgt, mvec(l, 5), "final_loss")
        sv.append(t)

    dx1, dff, dg2_last, loss_acc = fin
    gfull = {n: [None] * DEPTH for n in BIG}
    gsm = {n: [None] * DEPTH for (n, _) in SMALL}
    dmod = [[None] * 6 for _ in range(DEPTH)]
    dmod[DEPTH - 1][5] = dg2_last
    grad_x = None
    pending = []

    def halves_of(l, names):
        return [gfull[n][l].reshape(4, 2, gfull[n][l].shape[1] // 2, gfull[n][l].shape[2]) for n in names]

    def rs_finish(l, names, tag, g4, sib):
        h = _rs_chips_start(_sum_sibling(g4, sib, ci, f"sum_sibling_{tag}"), f"rs_chips_start_{tag}")
        pending.append((l, names, h))
        return h[4][0, 0]

    tie_l1 = tie_l0a = tie_sib = 0.0
    sib_l1 = sib_l0a = None

    for l in reversed(range(DEPTH)):
        t = sv[l]
        lw = LW[l]
        proj = t['proj']
        dgu = _mm_down_dx_swiglu(dff, lw['w_down'], t['gu'], f"mm_down_dx{l}")
        gfull['w_down'][l] = _mm(t['act'], dff, 'tn', f"mm_down_dw{l}").reshape(4, D_FF // 4, D_MODEL)
        dh2 = _mm(dgu, lw['w_gate_up'], 'nt', f"mm_gu_dx{l}", stack='b')
        gfull['w_gate_up'][l] = _mm(t['h2'], dgu, 'tn', f"mm_gu_dw{l}", stack='out')
        if l == 0:
            sib_l0a = _rs_sibling_start(halves_of(0, EARLY), "rs_sibling_start_l0a")
            tie_l1 = rs_finish(1, BIG, "l1", *_rs_sibling_wait(*sib_l1[:4], dh2, "rs_sibling_wait_l1"))
            tie_sib = sib_l0a[4][0, 0]
        dxc, dmo, dmod[l][2], gsm['norm2_w'][l], dmod[l][4], dmod[l][3] = _row_bwd(
            fn_resid_norm, f"resid_mid_b{l}", [(t['xcur'], 0, D_MODEL), (t['mo'], 0, D_MODEL)],
            [mvec(l, 2) + ((tie_l1 + tie_sib) if l == 0 else 0.0), small('norm2_w', l), mvec(l, 4), mvec(l, 3)],
            [(dx1, 0, D_MODEL), (dh2, 0, D_MODEL)], [0, 1], [0, 1, 2, 3], ddtypes=[F32, BF16])
        dmix = _mm(dmo, lw['w_out'], 'nt', f"mm_out_dx{l}")
        gfull['w_out'][l] = _mm(t['mix'], dmo, 'tn', f"mm_out_dw{l}").reshape(4, D_MODEL // 4, D_MODEL)
        dy, dz, gsm['ssd_norm_w'][l] = _row_bwd(fn_gated_norm, f"gated_b{l}", [(t['y'], 0, 512), (proj, 1, 512)],
                                                [small('ssd_norm_w', l)], [(dmix, 1, 512)], [0, 1], [0])
        if l == 0:
            tie_l0a = rs_finish(0, EARLY, "l0a", *_rs_sibling_wait(*sib_l0a[:4], dmix, "rs_sibling_wait_l0a"))
        dxact, ddt, dsv = _ssd_bwd(t['xact'], proj, ssd_vec(l) + (tie_l0a if l == 0 else 0.0), t['states'], dy, f"ssd_b{l}")
        gsm['dt_bias'][l], gsm['a_log'][l], gsm['d_skip'][l] = dsv[0, :8], dsv[1, :8], dsv[2, :8]
        dxbc, dcv = _conv_bwd(proj, conv_vec(l), dxact, f"conv_b{l}")
        gsm['conv_w'][l] = dcv[:CONV_TAPS]
        gsm['conv_b'][l] = dcv[CONV_TAPS]
        delta_r = _attn_delta(dmix, t['ao'], f"attn_delta{l}")
        dqT, dkf, dvv = _attn_bwd(t['qf'], t['kf'], t['kT'], t['vv'], dmix, t['lse'], delta_r, f"attn_b{l}")
        dq, dkv, dkpe, dwq, dwk = _row_bwd(
            fn_qk_prep, f"qk_prep_b{l}",
            [(t['q'], 0, 1024), (t['kv'], 0, 1024), (proj, 3, 128), (tab_c, 0, 128), (tab_s1, 0, 128), (tab_s2, 0, 128)],
            [wq_vec(l), wk_vec(l)], [(dqT, 0, 1024), (dkf, 0, 1024), (dvv, 0, 1024)], [0, 1, 2], [0, 1],
            ddtypes=[BF16, BF16, F32], transposed=(0,))
        gsm['q_nope_norm_w'][l], gsm['q_pe_norm_w'][l] = dwq[0, :NOPE], dwq[0, NOPE:QK_DIM]
        gsm['k_nope_norm_w'][l], gsm['k_pe_norm_w'][l] = dwk[0, :NOPE], dwk[0, NOPE:QK_DIM]
        dqa_n = _mm(dq, lw['w_q_up'], 'nt', f"mm_q_dx{l}")
        gfull['w_q_up'][l] = _cols_to_shards(_unpad_wq(_mm(t['qa_n'], dq, 'tn', f"mm_q_dw{l}")))
        dkva_n = _mm(dkv, lw['w_kv_up'], 'nt', f"mm_kv_dx{l}")
        gfull['w_kv_up'][l] = _cols_to_shards(_mm(t['kva_n'], dkv, 'tn', f"mm_kv_dw{l}"))
        dqa, dkva, dqw, dkvw = _row_bwd(fn_lat_norm, f"lat_norm_b{l}", [(proj, 0, 256), (proj, 2, 128)],
                                        [small('q_a_norm_w', l), small('kv_a_norm_w', l)],
                                        [(dqa_n, 0, 256), (dkva_n, 0, 128)], [0, 1], [0, 1])
        gsm['q_a_norm_w'][l], gsm['kv_a_norm_w'][l] = dqw[0], dkvw[0]
        dproj = jnp.concatenate([dqa, dkva, dkpe, dz, dxbc, ddt], axis=1).astype(BF16)
        dh1 = _mm(dproj, lw['w_in'], 'nt', f"mm_in_dx{l}")
        gfull['w_in'][l] = _cols_to_shards(_unpad_win(_mm(t['h1'], dproj, 'tn', f"mm_in_dw{l}")))
        if l > 0:
            p = sv[l - 1]
            dx1, dff, dmod[l - 1][5], gsm['norm1_w'][l], dmod[l][1], dmod[l][0] = _row_bwd(
                fn_resid_norm, f"resid_end_b{l - 1}", [(p['x1'], 0, D_MODEL), (p['ff'], 0, D_MODEL)],
                [mvec(l - 1, 5), small('norm1_w', l), mvec(l, 1), mvec(l, 0)], [(dxc, 0, D_MODEL), (dh1, 0, D_MODEL)],
                [0, 1], [0, 1, 2, 3], ddtypes=[F32, BF16])
            sib_l1 = _rs_sibling_start(halves_of(l, BIG), f"rs_sibling_start_l{l}")
            dff = dff + sib_l1[4][0, 0].astype(BF16)
        else:
            grad_x, gsm['norm1_w'][l], dmod[l][1], dmod[l][0] = _row_bwd(
                fn_norm_mod_pass, "norm_mod_b", [(x0, 0, D_MODEL)], [small('norm1_w', 0), mvec(0, 1), mvec(0, 0)],
                [(dxc, 0, D_MODEL), (dh1, 0, D_MODEL)], [0], [0, 1, 2])
        for n in ('norm1_w', 'norm2_w', 'ssd_norm_w'):
            gsm[n][l] = gsm[n][l][0]

    for l in range(DEPTH):
        gsm['b_ada'][l] = jnp.concatenate([d[0] for d in dmod[l]])
    sm_part = _pack_small({n: jnp.stack(v) for n, v in gsm.items()}).at[SMALL_ROWS - 1, 0].set(loss_acc[0, 0])
    sm_all = _ag8(sm_part, "ag_small")
    loss = jnp.sum(sm_all.reshape(8, SMALL_ROWS, LANES)[:, SMALL_ROWS - 1, 0])
    sm_all, late = lax.optimization_barrier((sm_all, [gfull[n][0] for n in BIG[2:]]))
    for n, g in zip(BIG[2:], late):
        gfull[n][0] = g
    late4 = halves_of(0, BIG[2:])
    tie_l0b = rs_finish(0, BIG[2:], "l0b", late4, _rs_sibling_list(late4, "rs_sibling_l0b"))

    def with_conv(tree):
        wide = lax.dynamic_update_slice_in_dim(jnp.zeros((DEPTH, CONV_TAPS, D_CONV), F32), tree['conv_w'], chip * 256, axis=2)
        return {**tree, 'conv_w': wide}

    g_sm, d_sm, m_sm, v_sm = _adam(_pack_small(with_conv(W)) + tie_l0b, _pack_small(with_conv(M)), _pack_small(with_conv(V)),
                                   [(sm_all, d * SMALL_ROWS) for d in range(8)], "adam_small")
    out_small = [_unpack_small(b) for b in (g_sm, d_sm, m_sm, v_sm)]
    for o in out_small:
        o['conv_w'] = lax.dynamic_slice_in_dim(o['conv_w'].reshape(DEPTH, CONV_TAPS, D_CONV), chip * 256, 256, axis=2)

    dmod_all = sm_all.reshape(8, SMALL_ROWS * LANES)
    per_layer = sum(k + (-k % LANES) for (_, k) in SMALL)
    dmod_sh = jnp.stack([lax.dynamic_slice_in_dim(dmod_all[:, l * per_layer:l * per_layer + 6 * D_MODEL], chip * 1536, 1536, axis=1)
                         for l in range(DEPTH)])
    ada = out_ada = _ada_bwd_adam(c_all.T, dmod_sh, W['w_ada'], M['w_ada'], V['w_ada'], "ada_bwd_adam")

    keys, cs_all, land_all = [], [], []
    for (l, names, (send_sems, recv_sems, cs_thru, land_thru, _)) in pending:
        cs, lands = _rs_chips_wait(send_sems, recv_sems, cs_thru, land_thru, ada[3], f"rs_chips_wait_l{l}{len(names)}")
        keys += [(l, n) for n in names]
        cs_all += list(cs)
        land_all += list(lands)
    gboth = _swap_list(_sum_chips(cs_all, land_all, chip, ci, "sum_chips"), "swap_halves")
    gshard = {k: g.reshape(2 * g.shape[1], g.shape[2]) for k, g in zip(keys, gboth)}

    def natural(n, a):
        return jnp.swapaxes(a, -1, -2) if n == 'w_in' else a

    res = _adam_multi([natural(n, W[n]) for n in BIG], [natural(n, M[n]) for n in BIG], [natural(n, V[n]) for n in BIG],
                      [[natural(n, gshard[(l, n)]) for l in range(DEPTH)] for n in BIG], "adam_big")
    out_big = [{n: natural(n, a) for n, a in zip(BIG, o)} for o in res]

    outs = [loss, grad_x[None]]
    for k in range(4):
        for n in WEIGHTS:
            if n == 'w_ada':
                outs.append(out_ada[k])
            elif n in BIG:
                outs.append(out_big[k][n])
            else:
                outs.append(out_small[k][n])
    return tuple(outs)
```

```python
import functools

import jax
import jax.numpy as jnp
from jax import lax
from jax.experimental import pallas as pl
from jax.experimental.pallas import tpu as pltpu

F32 = jnp.float32
BF16 = jnp.bfloat16
MESH = pl.DeviceIdType.MESH

D_MODEL = 1024
DEPTH = 2
HEADS = 8
NOPE = 64
ROPE = 32
QK_DIM = NOPE + ROPE
Q_LORA = 256
KV_LORA = 128
SSD_HEADS = 8
SSD_P = 64
SSD_N = 128
CHUNK = 256
CONV_TAPS = 4
D_SSD = 512
D_CONV = 1024
D_FF = 2816
D_IN = 1960
D_IN_PAD = 2176
EPS = 1e-6
ROPE_THETA = 10000.0
ATT_SCALE = QK_DIM ** -0.5
NEG = -1e30
LANES = 128
VMEM_LIMIT = 48 * 1024 * 1024
MM_VMEM_BUDGET = 36 * 1024 * 1024
MM_SLOTS = 3

ADAM_LR, ADAM_B1, ADAM_B2, ADAM_EPS, ADAM_WD, ADAM_STEP = 0.001, 0.9, 0.999, 1e-08, 0.01, 10

WEIGHTS = ['norm1_w', 'norm2_w', 'w_ada', 'b_ada', 'w_in', 'q_a_norm_w', 'w_q_up', 'kv_a_norm_w', 'w_kv_up',
           'q_nope_norm_w', 'q_pe_norm_w', 'k_nope_norm_w', 'k_pe_norm_w', 'conv_w', 'conv_b', 'dt_bias', 'a_log',
           'd_skip', 'ssd_norm_w', 'w_out', 'w_gate_up', 'w_down']
BIG = ['w_down', 'w_gate_up', 'w_out', 'w_kv_up', 'w_q_up', 'w_in']
EARLY = BIG[:2]
FIRST = BIG[3:]
REST = BIG[:3]
SMALL = [('b_ada', 6144), ('conv_w', 4096), ('norm1_w', 1024), ('norm2_w', 1024), ('conv_b', 1024), ('ssd_norm_w', 512),
         ('q_a_norm_w', 256), ('kv_a_norm_w', 128), ('q_nope_norm_w', 64), ('q_pe_norm_w', 32),
         ('k_nope_norm_w', 64), ('k_pe_norm_w', 32), ('dt_bias', 8), ('a_log', 8), ('d_skip', 8)]
SMALL_ROWS = 240


def _cp(sem=None, **kw):
    return pltpu.CompilerParams(dimension_semantics=sem, vmem_limit_bytes=VMEM_LIMIT, **kw)


def _dot(a, b, dims, prec=None):
    return lax.dot_general(a, b, (dims, ((), ())), preferred_element_type=F32, precision=prec)


def _tile(dim, target):
    best = 0
    for t in range(LANES, min(dim, target) + 1, LANES):
        if dim % t == 0:
            best = t
    if best < 256 and dim <= 2304:
        return dim
    return best


def _mm(a, b, mode, name, out_dtype=F32, stack=None):
    ns = None
    halves = (a if mode == 'nt' else b).ndim == 3 and stack is not None and not (stack == 'b' and mode == 'nn')
    if stack == 'b':
        ns = b.shape[2]
        if mode == 'nn':
            (M, K), N = a.shape, 4 * ns
        else:
            M, K, N = a.shape[-2], 4 * ns, b.shape[1]
    elif mode == 'nn':
        (M, K), (_, N) = a.shape, b.shape
    elif mode == 'nt':
        (M, K), (N, _) = a.shape, b.shape
    else:
        (K, M), N = a.shape, (2 * b.shape[2] if halves else b.shape[1])
    if stack == 'out':
        ns = N // 4
    tm, tn, tk = _tile(M, 1408 if mode == 'tn' else 1024), _tile(N, 1408), _tile(K, 1408)
    if stack == 'b' and mode == 'nt':
        tk = ns
    elif stack is not None:
        tn = ns
    nk = K // tk
    dims = {'nn': ((1,), (0,)), 'nt': ((1,), (1,)), 'tn': ((0,), (0,))}[mode]

    def body(a_ref, b_ref, o_ref, *acc):
        part = _dot(a_ref[...].astype(BF16), b_ref[...].astype(BF16), dims)
        if nk == 1:
            o_ref[...] = part.astype(o_ref.dtype)
            return
        k = pl.program_id(2)

        @pl.when(k == 0)
        def _():
            acc[0][...] = part

        @pl.when(k > 0)
        def _():
            acc[0][...] += part

        @pl.when(k == nk - 1)
        def _():
            o_ref[...] = acc[0][...].astype(o_ref.dtype)

    a_spec = pl.BlockSpec((tk, tm), lambda i, j, k: (k, i)) if mode == 'tn' else pl.BlockSpec((tm, tk), lambda i, j, k: (i, k))
    b_spec = pl.BlockSpec((tn, tk), lambda i, j, k: (j, k)) if mode == 'nt' else pl.BlockSpec((tk, tn), lambda i, j, k: (k, j))
    o_spec, o_shape = pl.BlockSpec((tm, tn), lambda i, j, k: (i, j)), (M, N)
    if stack == 'b':
        b_spec = (pl.BlockSpec((None, tn, ns), lambda i, j, k: (k, j, 0)) if mode == 'nt'
                  else pl.BlockSpec((None, tk, ns), lambda i, j, k: (j, k, 0)))
    if stack == 'out':
        o_spec, o_shape = pl.BlockSpec((None, tm, ns), lambda i, j, k: (j, i, 0)), (4, M, ns)
    if halves and mode == 'nt':
        a_spec = pl.BlockSpec((None, tm, ns), lambda i, j, k: (lax.div(k, 2), i, lax.rem(k, 2)))
    if halves and mode == 'tn':
        b_spec = pl.BlockSpec((None, tk, ns), lambda i, j, k: (lax.div(j, 2), k, lax.rem(j, 2)))
    gi, gj = M // tm, N // tn
    total = gi * gj * nk
    in_bytes = tm * tk * a.dtype.itemsize + tk * tn * b.dtype.itemsize
    out_bytes = tm * tn * (2 * jnp.dtype(out_dtype).itemsize + (4 if nk > 1 else 0))
    if total >= 4 and MM_SLOTS * in_bytes + out_bytes <= MM_VMEM_BUDGET:
        shape_of = lambda spec: tuple(d for d in spec.block_shape if d is not None)

        def block_of(ref, spec, step):
            ijk = (lax.div(step, nk * gj), lax.rem(lax.div(step, nk), gj), lax.rem(step, nk))
            return ref.at[tuple(ix if d is None else pl.ds(pl.multiple_of(ix * d, d), d)
                                for ix, d in zip(spec.index_map(*ijk), spec.block_shape))]

        def ring(a_hbm, b_hbm, o_ref, abuf, bbuf, sems, *acc):
            step = (pl.program_id(0) * gj + pl.program_id(1)) * nk + pl.program_id(2)

            def copies(s):
                slot = lax.rem(s, MM_SLOTS)
                return (pltpu.make_async_copy(block_of(a_hbm, a_spec, s), abuf.at[slot], sems.at[0, slot]),
                        pltpu.make_async_copy(block_of(b_hbm, b_spec, s), bbuf.at[slot], sems.at[1, slot]))

            @pl.when(step == 0)
            def _():
                for s in range(MM_SLOTS - 1):
                    for cp in copies(jnp.int32(s)):
                        cp.start()

            @pl.when(step + MM_SLOTS - 1 < total)
            def _():
                for cp in copies(step + MM_SLOTS - 1):
                    cp.start()

            for cp in copies(step):
                cp.wait()
            slot = lax.rem(step, MM_SLOTS)
            body(abuf.at[slot], bbuf.at[slot], o_ref, *acc)

        return pl.pallas_call(
            ring, name=name, grid=(gi, gj, nk),
            in_specs=[pl.BlockSpec(memory_space=pl.ANY)] * 2, out_specs=o_spec,
            out_shape=jax.ShapeDtypeStruct(o_shape, out_dtype),
            scratch_shapes=[pltpu.VMEM((MM_SLOTS,) + shape_of(a_spec), a.dtype), pltpu.VMEM((MM_SLOTS,) + shape_of(b_spec), b.dtype),
                            pltpu.SemaphoreType.DMA((2, MM_SLOTS))] + ([pltpu.VMEM((tm, tn), F32)] if nk > 1 else []),
            compiler_params=_cp(("arbitrary", "arbitrary", "arbitrary")),
        )(a, b)
    return pl.pallas_call(
        body, name=name, grid=(M // tm, N // tn, nk),
        in_specs=[a_spec, b_spec], out_specs=o_spec,
        out_shape=jax.ShapeDtypeStruct(o_shape, out_dtype),
        scratch_shapes=[pltpu.VMEM((tm, tn), F32)] if nk > 1 else [],
        compiler_params=_cp(("parallel", "parallel", "arbitrary")),
    )(a, b)


def _mm_gu_swiglu(h, wst, name):
    S, K = h.shape
    ns = wst.shape[2]
    tm = _tile(S, 512)

    def body(a_ref, bg_ref, bu_ref, gu_ref, act_ref):
        a = a_ref[...]
        g = _dot(a, bg_ref[...], ((1,), (0,)))
        u = _dot(a, bu_ref[...], ((1,), (0,)))
        gu_ref[0] = g
        gu_ref[1] = u
        act_ref[...] = (g * jax.nn.sigmoid(g) * u).astype(act_ref.dtype)

    return pl.pallas_call(
        body, name=name, grid=(S // tm, 2),
        in_specs=[pl.BlockSpec((tm, K), lambda i, j: (i, 0)), pl.BlockSpec((None, K, ns), lambda i, j: (j, 0, 0)),
                  pl.BlockSpec((None, K, ns), lambda i, j: (j + 2, 0, 0))],
        out_specs=[pl.BlockSpec((2, tm, ns), lambda i, j: (0, i, j)), pl.BlockSpec((tm, ns), lambda i, j: (i, j))],
        out_shape=[jax.ShapeDtypeStruct((2, S, 2 * ns), F32), jax.ShapeDtypeStruct((S, 2 * ns), BF16)],
        compiler_params=_cp(("parallel", "parallel")),
    )(h, wst, wst)


def _mm_down_dx_swiglu(dff, w_down, gu, name):
    S, K = dff.shape
    tm, tn = _tile(S, 512), _tile(D_FF, 1408)

    def body(a_ref, b_ref, g_ref, u_ref, o_ref):
        dact = _dot(a_ref[...].astype(BF16), b_ref[...], ((1,), (1,)))
        g, u = g_ref[...], u_ref[...]
        sg = jax.nn.sigmoid(g)
        o_ref[0] = (dact * u * (sg * (1.0 + g * (1.0 - sg)))).astype(o_ref.dtype)
        o_ref[1] = (dact * (g * sg)).astype(o_ref.dtype)

    return pl.pallas_call(
        body, name=name, grid=(S // tm, D_FF // tn),
        in_specs=[pl.BlockSpec((tm, K), lambda i, j: (i, 0)), pl.BlockSpec((tn, K), lambda i, j: (j, 0)),
                  pl.BlockSpec((None, tm, tn), lambda i, j: (0, i, j)), pl.BlockSpec((None, tm, tn), lambda i, j: (1, i, j))],
        out_specs=pl.BlockSpec((2, tm, tn), lambda i, j: (0, i, j)),
        out_shape=jax.ShapeDtypeStruct((2, S, D_FF), BF16),
        compiler_params=_cp(("parallel", "parallel")),
    )(dff, w_down, gu, gu)


def _rspec(tm, w, cb):
    return pl.BlockSpec((tm, w), lambda i: (i, cb))


def _vspec(shape):
    return pl.BlockSpec(shape, lambda i: (0,) * len(shape))


def _row_fwd(fn, name, rows, vecs, outs, tm=256, transposed=()):
    S = rows[0][0].shape[0]
    tm = min(tm, S)
    nin = len(rows) + len(vecs)

    def body(*refs):
        res = fn(*[r[...] for r in refs[:nin]])
        for k, (o_ref, r) in enumerate(zip(refs[nin:], res)):
            o_ref[...] = (r.T if k in transposed else r).astype(o_ref.dtype)

    return pl.pallas_call(
        body, name=name, grid=(S // tm,),
        in_specs=[_rspec(tm, w, cb) for (_, cb, w) in rows] + [_vspec(v.shape) for v in vecs],
        out_specs=[pl.BlockSpec((w, tm), lambda i: (0, i)) if k in transposed else _rspec(tm, w, 0)
                   for k, (w, _) in enumerate(outs)],
        out_shape=[jax.ShapeDtypeStruct((w, S) if k in transposed else (S, w), dt) for k, (w, dt) in enumerate(outs)],
        compiler_params=_cp(("parallel",)),
    )(*[r[0] for r in rows], *vecs)


def _row_bwd(fn, name, rows, vecs, cts, drows, dvecs, tm=256, ddtypes=None, transposed=()):
    S = rows[0][0].shape[0]
    ddtypes = ddtypes or [F32] * len(drows)
    tm = min(tm, S)
    nr, nv, nc = len(rows), len(vecs), len(cts)
    didx = list(drows) + [nr + j for j in dvecs]

    def body(*refs):
        vals = [r[...] for r in refs[:nr + nv]]
        ct = tuple((r[...].T if k in transposed else r[...]).astype(F32)
                   for k, r in enumerate(refs[nr + nv:nr + nv + nc]))
        outs = refs[nr + nv + nc:]

        def g(*d):
            a = list(vals)
            for k, val in zip(didx, d):
                a[k] = val
            return tuple(fn(*a))

        _, vjp = jax.vjp(g, *[vals[k] for k in didx])
        grads = vjp(ct)
        for o, gr in zip(outs[:len(drows)], grads[:len(drows)]):
            o[...] = gr.astype(o.dtype)

        @pl.when(pl.program_id(0) == 0)
        def _():
            for o in outs[len(drows):]:
                o[...] = jnp.zeros_like(o)

        for o, gr in zip(outs[len(drows):], grads[len(drows):]):
            o[...] += gr

    return pl.pallas_call(
        body, name=name, grid=(S // tm,),
        in_specs=[_rspec(tm, w, cb) for (_, cb, w) in rows] + [_vspec(v.shape) for v in vecs]
        + [pl.BlockSpec((w, tm), lambda i: (0, i)) if k in transposed else _rspec(tm, w, cb) for k, (_, cb, w) in enumerate(cts)],
        out_specs=[_rspec(tm, rows[k][2], 0) for k in drows] + [_vspec(vecs[j].shape) for j in dvecs],
        out_shape=[jax.ShapeDtypeStruct((S, rows[k][2]), dt) for k, dt in zip(drows, ddtypes)]
        + [jax.ShapeDtypeStruct(vecs[j].shape, F32) for j in dvecs],
        compiler_params=_cp(("arbitrary",)),
    )(*[r[0] for r in rows], *vecs, *[c[0] for c in cts])


def _rms(x):
    return x * lax.rsqrt(jnp.mean(x * x, axis=-1, keepdims=True) + EPS)


def fn_norm_mod(x, nw, sc, sh):
    return (_rms(x) * nw * (1.0 + sc) + sh,)


def fn_norm_mod_pass(x, nw, sc, sh):
    return (x, _rms(x) * nw * (1.0 + sc) + sh)


def fn_resid_norm(x, d, g, nw, sc, sh):
    xn = x + g * d
    return (xn, _rms(xn) * nw * (1.0 + sc) + sh)


def fn_lat_norm(qa, kva, qw, kvw):
    return (_rms(qa) * qw, _rms(kva) * kvw)


@functools.partial(jax.custom_vjp, nondiff_argnums=(1,))
def _lroll(x, s):
    return pltpu.roll(x, s, 1)


def _lroll_fwd(x, s):
    return pltpu.roll(x, s, 1), None


def _lroll_bwd(s, _, g):
    return (pltpu.roll(g, (LANES - s) % LANES, 1),)


_lroll.defvjp(_lroll_fwd, _lroll_bwd)


def _lane_masks(shape):
    lane = lax.broadcasted_iota(jnp.int32, shape, 1)
    return (lane < NOPE).astype(F32), ((lane >= NOPE) & (lane < QK_DIM)).astype(F32)


def _rope(t, tc, ts1, ts2):
    return t * tc + _lroll(t, 16) * ts1 + _lroll(t, LANES - 16) * ts2


def fn_qk_prep(q, kv, kpe, tc, ts1, ts2, wq, wk):
    mn, mp = _lane_masks((1, LANES))
    mhi = 1.0 - mn

    def head_norm(t, w):
        rn = lax.rsqrt(jnp.sum(t * t * mn, axis=-1, keepdims=True) * (1.0 / NOPE) + EPS)
        rp = lax.rsqrt(jnp.sum(t * t * mp, axis=-1, keepdims=True) * (1.0 / ROPE) + EPS)
        return t * (rn * mn + rp * mp) * w

    kp = _rope(head_norm(_lroll(kpe, NOPE), wk) * mp, tc, ts1, ts2)
    qs, ks, vs = [], [], []
    for h in range(HEADS):
        qs.append(_rope(head_norm(q[:, h * LANES:(h + 1) * LANES], wq), tc, ts1, ts2))
        t = kv[:, h * LANES:(h + 1) * LANES]
        ks.append(head_norm(t, wk) * mn + kp)
        vs.append(_lroll(t, NOPE) * mn + mhi)
    return (jnp.concatenate(qs, axis=1), jnp.concatenate(ks, axis=1), jnp.concatenate(vs, axis=1))


def fn_qk_prep_kt(*args):
    qf, kf, va = fn_qk_prep(*args)
    return (qf, kf, va, kf)


def fn_gated_norm(y, z, w):
    u = y * jax.nn.silu(z)
    half = D_SSD // 2
    return (jnp.concatenate([_rms(u[:, :half]), _rms(u[:, half:])], axis=1) * w,)


def fn_gated_mix(y, z, ao, w):
    return (jnp.concatenate([ao, fn_gated_norm(y, z, w)[0]], axis=1),)


def _final(x1, ff, tgt, g2, name):
    S = x1.shape[0]
    tm = min(256, S)

    def body(x_ref, f_ref, t_ref, g_ref, dx_ref, df_ref, dg_ref, l_ref):
        @pl.when(pl.program_id(0) == 0)
        def _():
            dg_ref[...] = jnp.zeros_like(dg_ref)
            l_ref[...] = jnp.zeros_like(l_ref)

        f = f_ref[...]
        g = g_ref[...]
        e = x_ref[...] + g * f - t_ref[...]
        dx = e * (1.0 / D_MODEL)
        dx_ref[...] = dx
        df_ref[...] = (g * dx).astype(df_ref.dtype)
        dg_ref[...] += jnp.sum(dx * f, axis=0, keepdims=True)
        l_ref[...] += jnp.sum(e * e) * (0.5 / D_MODEL)

    r = _rspec(tm, D_MODEL, 0)
    return pl.pallas_call(
        body, name=name, grid=(S // tm,),
        in_specs=[r, r, r, _vspec((1, D_MODEL))],
        out_specs=[r, r, _vspec((1, D_MODEL)), _vspec((1, LANES))],
        out_shape=[jax.ShapeDtypeStruct((S, D_MODEL), F32), jax.ShapeDtypeStruct((S, D_MODEL), BF16),
                   jax.ShapeDtypeStruct((1, D_MODEL), F32), jax.ShapeDtypeStruct((1, LANES), F32)],
        compiler_params=_cp(("arbitrary",)),
    )(x1, ff, tgt, g2)


def _causal_mask(t):
    r = lax.broadcasted_iota(jnp.int32, (t, t), 0)
    c = lax.broadcasted_iota(jnp.int32, (t, t), 1)
    return c <= r


LOG2E = 1.4426950408889634
EXP2_SCALE = ATT_SCALE * LOG2E
ATT_TQ, ATT_TK = 1024, 1024
ATT_BQ, ATT_BK = 1024, 1024


def _attn_fwd(qf, kf, va, name):
    S = qf.shape[0]
    T, TK = min(ATT_TQ, S), min(ATT_TK, S)
    nmask = max(1, T // TK)

    def body(q_ref, k_ref, v_ref, o_ref, l_ref):
        i = pl.program_id(1)
        r = lax.broadcasted_iota(jnp.int32, (T, TK), 0)
        c = lax.broadcasted_iota(jnp.int32, (T, TK), 1)
        qs = [q_ref[:, hh * LANES:(hh + 1) * LANES] for hh in range(2)]

        def blk(j, carry, masked):
            off = pl.multiple_of(j * TK, TK)
            out = []
            for hh in range(2):
                m, acc = carry[hh]
                s = _dot(qs[hh], k_ref[pl.ds(off, TK), hh * LANES:(hh + 1) * LANES], ((1,), (1,)))
                if masked:
                    s = jnp.where(c + j * TK <= r + i * T, s, NEG)
                mn = jnp.maximum(m, jnp.max(s, axis=1, keepdims=True))
                p = jnp.exp2((s - mn) * EXP2_SCALE)
                al = jnp.exp2((m - mn) * EXP2_SCALE)
                vj = v_ref[pl.ds(off, TK), hh * LANES:(hh + 1) * LANES]
                out.append((mn, al * acc + _dot(p.astype(BF16), vj, ((1,), (0,)))))
            return tuple(out)

        one = (jnp.full((T, 1), NEG, F32), jnp.zeros((T, LANES), F32))
        nfull = lax.div(i * T, TK)
        carry = lax.fori_loop(0, nfull, lambda j, cr: blk(j, cr, False), (one, one))
        for t in range(nmask):
            carry = blk(nfull + t, carry, True)
        lane = lax.broadcasted_iota(jnp.int32, (1, LANES), 1)
        z = jnp.zeros((T, LANES), F32)
        for hh in range(2):
            m, acc = carry[hh]
            l = acc[:, 64:65]
            o_ref[:, hh * 64:(hh + 1) * 64] = (acc / l)[:, :64]
            z = z + (m * EXP2_SCALE + jnp.log(l) * LOG2E) * (lane == hh).astype(F32)
        l_ref[0] = z.T[0:2, :]

    return pl.pallas_call(
        body, name=name, grid=(HEADS // 2, S // T),
        in_specs=[pl.BlockSpec((T, 256), lambda h, i: (i, h)), pl.BlockSpec((S, 256), lambda h, i: (0, h)),
                  pl.BlockSpec((S, 256), lambda h, i: (0, h))],
        out_specs=[pl.BlockSpec((T, LANES), lambda h, i: (i, h)), pl.BlockSpec((1, 2, T), lambda h, i: (h, 0, i))],
        out_shape=[jax.ShapeDtypeStruct((S, D_SSD), F32), jax.ShapeDtypeStruct((HEADS // 2, 2, S), F32)],
        compiler_params=_cp(("parallel", "parallel")),
    )(qf, kf, va)


def _attn_delta(dmix, ao, name):
    S = ao.shape[0]
    tm = min(512, S)

    def body(d_ref, o_ref, out_ref):
        lane = lax.broadcasted_iota(jnp.int32, (1, LANES), 1)
        lo = (lane < 64).astype(F32)
        for hp in range(HEADS // 2):
            y = d_ref[:, hp * LANES:(hp + 1) * LANES] * o_ref[:, hp * LANES:(hp + 1) * LANES]
            z = (jnp.sum(y * lo, axis=1, keepdims=True) * (lane == 0).astype(F32)
                 + jnp.sum(y * (1.0 - lo), axis=1, keepdims=True) * (lane == 1).astype(F32))
            out_ref[hp] = z.T[0:2, :]

    return pl.pallas_call(
        body, name=name, grid=(S // tm,),
        in_specs=[pl.BlockSpec((tm, D_SSD), lambda i: (i, 0)), pl.BlockSpec((tm, D_SSD), lambda i: (i, 0))],
        out_specs=pl.BlockSpec((HEADS // 2, 2, tm), lambda i: (0, 0, i)),
        out_shape=jax.ShapeDtypeStruct((HEADS // 2, 2, S), F32),
        compiler_params=_cp(("parallel",)),
    )(dmix, ao)


def _attn_bwd(qf, kf, kT, va, do, lse_r, delta_r, name):
    S = qf.shape[0]
    T, TK = min(ATT_BQ, S), min(ATT_BK, S)
    nq = S // T
    nmask = max(1, TK // T)

    def body(q_ref, k_ref, kT_ref, v_ref, do_ref, l_ref, d_ref, dqT_ref, dk_ref, dv_ref):
        j = pl.program_id(1)

        @pl.when(j == 0)
        def _():
            dqT_ref[...] = jnp.zeros_like(dqT_ref)

        r = lax.broadcasted_iota(jnp.int32, (TK, T), 0)
        c = lax.broadcasted_iota(jnp.int32, (TK, T), 1)
        lo = (lax.broadcasted_iota(jnp.int32, (1, LANES), 1) < 64).astype(F32)
        ks = [k_ref[:, hh * LANES:(hh + 1) * LANES] for hh in range(2)]
        vs = [v_ref[:, hh * LANES:(hh + 1) * LANES] for hh in range(2)]
        kTs = [kT_ref[hh * LANES:(hh + 1) * LANES, :] for hh in range(2)]

        def blk(i, carry, masked):
            off = pl.multiple_of(i * T, T)
            dall = do_ref[pl.ds(off, T), :]
            out = []
            for hh in range(2):
                dk, dv = carry[hh]
                q = q_ref[pl.ds(off, T), hh * LANES:(hh + 1) * LANES]
                dop = ((dall if hh == 0 else pltpu.roll(dall, 64, 1)) * lo).astype(BF16)
                lrow = l_ref[0, hh:hh + 1, pl.ds(off, T)]
                drow = d_ref[0, hh:hh + 1, pl.ds(off, T)]
                pT = jnp.exp2(_dot(ks[hh], q, ((1,), (1,))) * EXP2_SCALE - lrow)
                if masked:
                    pT = jnp.where(r + j * TK <= c + i * T, pT, 0.0)
                dpT = _dot(vs[hh], dop, ((1,), (1,)))
                dsT = (pT * (dpT - drow) * ATT_SCALE).astype(BF16)
                dv = dv + _dot(pT.astype(BF16), dop, ((1,), (0,)))
                dk = dk + _dot(dsT, q, ((1,), (0,)))
                dqT_ref[hh * LANES:(hh + 1) * LANES, pl.ds(off, T)] += _dot(kTs[hh], dsT, ((1,), (0,)))
                out.append((dk, dv))
            return tuple(out)

        z = (jnp.zeros((TK, LANES), F32), jnp.zeros((TK, LANES), F32))
        first = lax.div(j * TK, T)
        carry = (z, z)
        for t in range(nmask):
            carry = blk(first + t, carry, True)
        carry = lax.fori_loop(first + nmask, nq, lambda i, cr: blk(i, cr, False), carry)
        for hh in range(2):
            dk_ref[:, hh * LANES:(hh + 1) * LANES] = carry[hh][0]
            dv_ref[:, hh * LANES:(hh + 1) * LANES] = carry[hh][1]

    return pl.pallas_call(
        body, name=name, grid=(HEADS // 2, S // TK),
        in_specs=[pl.BlockSpec((S, 256), lambda h, j: (0, h)), pl.BlockSpec((TK, 256), lambda h, j: (j, h)),
                  pl.BlockSpec((256, TK), lambda h, j: (h, j)), pl.BlockSpec((TK, 256), lambda h, j: (j, h)),
                  pl.BlockSpec((S, LANES), lambda h, j: (0, h)), pl.BlockSpec((1, 2, S), lambda h, j: (h, 0, 0)),
                  pl.BlockSpec((1, 2, S), lambda h, j: (h, 0, 0))],
        out_specs=[pl.BlockSpec((256, S), lambda h, j: (h, 0)), pl.BlockSpec((TK, 256), lambda h, j: (j, h)),
                   pl.BlockSpec((TK, 256), lambda h, j: (j, h))],
        out_shape=[jax.ShapeDtypeStruct((D_MODEL, S), F32), jax.ShapeDtypeStruct((S, D_MODEL), F32),
                   jax.ShapeDtypeStruct((S, D_MODEL), F32)],
        compiler_params=_cp(("parallel", "arbitrary")),
    )(qf, kf, kT, va, do, lse_r, delta_r)


def _shift_down(x, s):
    if s == 0:
        return x
    rows = lax.broadcasted_iota(jnp.int32, x.shape, 0)
    return jnp.where(rows >= s, pltpu.roll(x, s, 0), 0.0)


def _shift_up(x, s):
    if s == 0:
        return x
    n = x.shape[0]
    rows = lax.broadcasted_iota(jnp.int32, x.shape, 0)
    return jnp.where(rows < n - s, pltpu.roll(x, n - s, 0), 0.0)


def _conv_fwd(proj, cvec, name):
    S = proj.shape[0]

    def body(x_ref, c_ref, o_ref):
        x = x_ref[...]
        y = jnp.broadcast_to(c_ref[4:5, :], x.shape)
        for k in range(CONV_TAPS):
            y = y + c_ref[k:k + 1, :] * _shift_down(x, CONV_TAPS - 1 - k)
        o_ref[...] = y * jax.nn.sigmoid(y)

    return pl.pallas_call(
        body, name=name, grid=(D_CONV // LANES,),
        in_specs=[pl.BlockSpec((S, LANES), lambda j: (0, 8 + j)), pl.BlockSpec((8, LANES), lambda j: (0, j))],
        out_specs=pl.BlockSpec((S, LANES), lambda j: (0, j)),
        out_shape=jax.ShapeDtypeStruct((S, D_CONV), F32),
        compiler_params=_cp(("parallel",)),
    )(proj, cvec)


def _conv_bwd(proj, cvec, dact, name):
    S = proj.shape[0]

    def body(x_ref, c_ref, d_ref, dx_ref, dc_ref):
        x = x_ref[...]
        y = jnp.broadcast_to(c_ref[4:5, :], x.shape)
        for k in range(CONV_TAPS):
            y = y + c_ref[k:k + 1, :] * _shift_down(x, CONV_TAPS - 1 - k)
        sg = jax.nn.sigmoid(y)
        dy = d_ref[...] * (sg * (1.0 + y * (1.0 - sg)))
        dx = jnp.zeros_like(x)
        for k in range(CONV_TAPS):
            s = CONV_TAPS - 1 - k
            dx = dx + c_ref[k:k + 1, :] * _shift_up(dy, s)
            dc_ref[k:k + 1, :] = jnp.sum(dy * _shift_down(x, s), axis=0, keepdims=True)
        dx_ref[...] = dx
        dc_ref[4:5, :] = jnp.sum(dy, axis=0, keepdims=True)
        dc_ref[5:8, :] = jnp.zeros((3, LANES), F32)

    return pl.pallas_call(
        body, name=name, grid=(D_CONV // LANES,),
        in_specs=[pl.BlockSpec((S, LANES), lambda j: (0, 8 + j)), pl.BlockSpec((8, LANES), lambda j: (0, j)),
                  pl.BlockSpec((S, LANES), lambda j: (0, j))],
        out_specs=[pl.BlockSpec((S, LANES), lambda j: (0, j)), pl.BlockSpec((8, LANES), lambda j: (0, j))],
        out_shape=[jax.ShapeDtypeStruct((S, D_CONV), F32), jax.ShapeDtypeStruct((8, D_CONV), F32)],
        compiler_params=_cp(("parallel",)),
    )(proj, cvec, dact)


def fn_ssd_chunk(xs, bm, cm, dtr, state, vecs):
    Q = CHUNK
    dt = jax.nn.softplus(dtr + vecs[0:1])
    a = -jnp.exp(vecs[1:2])
    adt = dt * a
    tril = _causal_mask(Q)
    acs = _dot(tril.astype(F32), adt, ((1,), (0,)), lax.Precision.HIGHEST)
    acs_t = acs.T
    alast = acs[Q - 1:Q, :]
    r = lax.broadcasted_iota(jnp.int32, (LANES, D_SSD), 0)
    c = lax.broadcasted_iota(jnp.int32, (LANES, D_SSD), 1)
    spread = (lax.shift_right_logical(c, 6) == r).astype(F32)

    def per_head(v):
        return _dot(v, spread, ((1,), (0,)), lax.Precision.HIGH)

    xdt = xs * per_head(dt)
    ub = (xdt * per_head(jnp.exp(alast - acs))).astype(BF16)
    xdtb = xdt.astype(BF16)
    Bs = [bm[:, g * SSD_N:(g + 1) * SSD_N].astype(BF16) for g in range(2)]
    Cs = [cm[:, g * SSD_N:(g + 1) * SSD_N].astype(BF16) for g in range(2)]
    Gs = [_dot(Cs[g], Bs[g], ((1,), (1,))) for g in range(2)]
    yds, yos, adds = [], [], []
    for h in range(SSD_HEADS):
        g = h // (SSD_HEADS // 2)
        sl = slice(h * SSD_P, (h + 1) * SSD_P)
        L = jnp.exp(jnp.where(tril, acs[:, h:h + 1] - acs_t[h:h + 1, :], -jnp.inf))
        yds.append(_dot((Gs[g] * L).astype(BF16), xdtb[:, sl], ((1,), (0,))))
        yos.append(_dot(Cs[g], state[h].astype(BF16), ((1,), (1,))))
        adds.append(_dot(ub[:, sl], Bs[g], ((0,), (0,))))
    y = jnp.concatenate(yds, axis=1) + jnp.concatenate(yos, axis=1) * per_head(jnp.exp(acs)) + per_head(vecs[2:3]) * xs
    decay = jnp.stack([jnp.broadcast_to(jnp.exp(alast[:, h:h + 1]), (SSD_P, SSD_N)) for h in range(SSD_HEADS)])
    return y, jnp.stack(adds) + state * decay


def _ssd_fwd(xact, proj, svec, name):
    S = xact.shape[0]
    nc = S // CHUNK

    def body(x_ref, dt_ref, v_ref, y_ref, st_ref, state):
        @pl.when(pl.program_id(0) == 0)
        def _():
            state[...] = jnp.zeros_like(state)

        st_ref[0] = state[...]
        x = x_ref[...]
        y, sn = fn_ssd_chunk(x[:, 0:512], x[:, 512:768], x[:, 768:1024], dt_ref[...], state[...], v_ref[...])
        y_ref[...] = y
        state[...] = sn

    return pl.pallas_call(
        body, name=name, grid=(nc,),
        in_specs=[pl.BlockSpec((CHUNK, D_CONV), lambda i: (i, 0)), pl.BlockSpec((CHUNK, LANES), lambda i: (i, 16)),
                  pl.BlockSpec((8, LANES), lambda i: (0, 0))],
        out_specs=[pl.BlockSpec((CHUNK, D_SSD), lambda i: (i, 0)),
                   pl.BlockSpec((1, SSD_HEADS, SSD_P, SSD_N), lambda i: (i, 0, 0, 0))],
        out_shape=[jax.ShapeDtypeStruct((S, D_SSD), F32), jax.ShapeDtypeStruct((nc, SSD_HEADS, SSD_P, SSD_N), F32)],
        scratch_shapes=[pltpu.VMEM((SSD_HEADS, SSD_P, SSD_N), F32)],
        compiler_params=_cp(("arbitrary",)),
    )(xact, proj, svec)


def _ssd_bwd(xact, proj, svec, states, dy, name):
    S = xact.shape[0]
    nc = S // CHUNK

    def body(x_ref, dt_ref, v_ref, st_ref, dy_ref, dx_ref, ddt_ref, dv_ref, dstate):
        @pl.when(pl.program_id(0) == 0)
        def _():
            dstate[...] = jnp.zeros_like(dstate)
            dv_ref[...] = jnp.zeros_like(dv_ref)

        x = x_ref[...]
        _, vjp = jax.vjp(fn_ssd_chunk, x[:, 0:512], x[:, 512:768], x[:, 768:1024], dt_ref[...], st_ref[0], v_ref[...])
        dxs, dbm, dcm, ddt, dst, dvec = vjp((dy_ref[...], dstate[...]))
        dx_ref[:, 0:512] = dxs
        dx_ref[:, 512:768] = dbm
        dx_ref[:, 768:1024] = dcm
        ddt_ref[...] = ddt
        dstate[...] = dst
        dv_ref[...] += dvec

    rev = lambda i: (nc - 1 - i, 0)
    return pl.pallas_call(
        body, name=name, grid=(nc,),
        in_specs=[pl.BlockSpec((CHUNK, D_CONV), rev), pl.BlockSpec((CHUNK, LANES), lambda i: (nc - 1 - i, 16)),
                  pl.BlockSpec((8, LANES), lambda i: (0, 0)),
                  pl.BlockSpec((1, SSD_HEADS, SSD_P, SSD_N), lambda i: (nc - 1 - i, 0, 0, 0)),
                  pl.BlockSpec((CHUNK, D_SSD), rev)],
        out_specs=[pl.BlockSpec((CHUNK, D_CONV), rev), pl.BlockSpec((CHUNK, LANES), rev),
                   pl.BlockSpec((8, LANES), lambda i: (0, 0))],
        out_shape=[jax.ShapeDtypeStruct((S, D_CONV), F32), jax.ShapeDtypeStruct((S, LANES), F32),
                   jax.ShapeDtypeStruct((8, LANES), F32)],
        scratch_shapes=[pltpu.VMEM((SSD_HEADS, SSD_P, SSD_N), F32)],
        compiler_params=_cp(("arbitrary",)),
    )(xact, proj, svec, states, dy)


def _ada_fwd(c_all, w_ada, b_sh, name):
    nb = 1536 // 512

    def body(c_ref, w_ref, b_ref, o_ref):
        ca = jax.nn.silu(c_ref[...]).astype(BF16)
        o_ref[0] = _dot(ca, w_ref[0].astype(BF16), ((1,), (0,))) + b_ref[0]

    return pl.pallas_call(
        body, name=name, grid=(DEPTH, nb),
        in_specs=[pl.BlockSpec((8, D_MODEL), lambda l, j: (0, 0)), pl.BlockSpec((1, D_MODEL, 512), lambda l, j: (l, 0, j)),
                  pl.BlockSpec((1, 1, 512), lambda l, j: (l, 0, j))],
        out_specs=pl.BlockSpec((1, 8, 512), lambda l, j: (l, 0, j)),
        out_shape=jax.ShapeDtypeStruct((DEPTH, 8, 1536), F32),
        compiler_params=_cp(("parallel", "parallel")),
    )(c_all, w_ada, b_sh)


def _ada_bwd_adam(c_all_t, dmod_sh, w, m, v, name):
    nb = 1536 // 512

    def body(c_ref, d_ref, w_ref, m_ref, v_ref, g_ref, dl_ref, nm_ref, nv_ref):
        ca = jax.nn.silu(c_ref[...])
        g = ca[:, 0:1] * d_ref[0, 0:1, :]
        for b in range(1, 8):
            g = g + ca[:, b:b + 1] * d_ref[0, b:b + 1, :]
        g_ref[0] = g
        dl_ref[0], nm_ref[0], nv_ref[0] = _adam_update(w_ref[0], m_ref[0], v_ref[0], g)

    blk = pl.BlockSpec((1, D_MODEL, 512), lambda l, j: (l, 0, j))
    return pl.pallas_call(
        body, name=name, grid=(DEPTH, nb),
        in_specs=[pl.BlockSpec((D_MODEL, 8), lambda l, j: (0, 0)), pl.BlockSpec((1, 8, 512), lambda l, j: (l, 0, j)), blk, blk, blk],
        out_specs=[blk] * 4, out_shape=[jax.ShapeDtypeStruct((DEPTH, D_MODEL, 1536), F32)] * 4,
        compiler_params=_cp(("parallel", "parallel")),
    )(c_all_t, dmod_sh, w, m, v)


def _rows_tile(rows):
    return next(t for t in (512, 256, 128, 64, 32, 16, 8) if rows % t == 0)


SUM_BLOCKS = 4
ADAM_BLOCKS = 8


def _sum_sibling(gs, ls, ci, name):
    n = len(gs)

    def body(c_ref, *refs):
        for p in range(n):
            refs[2 * n + p][...] = refs[2 * p][...] + refs[2 * p + 1][...]

    in_specs, out_specs, out_shape = [], [], []
    for g in gs:
        _, _, rh, cw = g.shape
        rb = rh // SUM_BLOCKS
        in_specs += [pl.BlockSpec((None, None, rb, cw), lambda s, i, c: (s, c[0], i, 0)),
                     pl.BlockSpec((None, rb, cw), lambda s, i, c: (s, i, 0))]
        out_specs.append(pl.BlockSpec((None, rb, cw), lambda s, i, c: (s, i, 0)))
        out_shape.append(jax.ShapeDtypeStruct((4, rh, cw), F32))
    ops = [a for pair in zip(gs, ls) for a in pair]
    return pl.pallas_call(
        body, name=name,
        grid_spec=pltpu.PrefetchScalarGridSpec(num_scalar_prefetch=1, grid=(4, SUM_BLOCKS), in_specs=in_specs, out_specs=out_specs),
        out_shape=out_shape, compiler_params=_cp(("parallel", "parallel")),
    )(ci.reshape(1).astype(jnp.int32), *ops)


def _sum_chips(cs, lands, chip, ci, name):
    n = len(cs)

    def body(c_ref, *refs):
        for p in range(n):
            a = refs[4 * p:4 * p + 4]
            refs[4 * n + p][...] = ((a[0][...] + a[1][...]) + a[2][...]) + a[3][...]

    in_specs, out_specs, out_shape = [], [], []
    for c in cs:
        _, rh, cw = c.shape
        rb = rh // SUM_BLOCKS
        in_specs.append(pl.BlockSpec((None, rb, cw), lambda i, ch: (ch[0], i, 0)))
        in_specs += [pl.BlockSpec((None, rb, cw), functools.partial(lambda i, ch, k: (k, i, 0), k=k)) for k in range(3)]
        out_specs.append(pl.BlockSpec((None, rb, cw), lambda i, ch: (ch[1], i, 0)))
        out_shape.append(jax.ShapeDtypeStruct((2, rh, cw), F32))
    ops = [a for c, l in zip(cs, lands) for a in (c, l, l, l)]
    return pl.pallas_call(
        body, name=name,
        grid_spec=pltpu.PrefetchScalarGridSpec(num_scalar_prefetch=1, grid=(SUM_BLOCKS,), in_specs=in_specs, out_specs=out_specs),
        out_shape=out_shape, compiler_params=_cp(("parallel",)),
    )(jnp.stack([chip, ci]).astype(jnp.int32), *ops)


def _adam_update(w, m, v, g):
    c1 = 1.0 / (1.0 - ADAM_B1 ** ADAM_STEP)
    c2 = 1.0 / (1.0 - ADAM_B2 ** ADAM_STEP)
    nm = ADAM_B1 * m + (1.0 - ADAM_B1) * g
    nv = ADAM_B2 * v + (1.0 - ADAM_B2) * (g * g)
    return -ADAM_LR * ((nm * c1) / (jnp.sqrt(nv * c2) + ADAM_EPS) + ADAM_WD * w), nm, nv


def _adam_multi(ws, ms, vs, gs, name):
    n = len(ws)
    per = 3 + DEPTH

    def body(*refs):
        layer = pl.program_id(0)
        for p in range(n):
            w, m, v = [refs[per * p + k][...] for k in range(3)]
            g = refs[per * p + 3][...]
            for l in range(1, DEPTH):
                g = jnp.where(layer == l, refs[per * p + 3 + l][...], g)
            d, nm, nv = _adam_update(w, m, v, g)
            for k, val in enumerate((g, d, nm, nv)):
                refs[per * n + 4 * p + k][...] = val

    in_specs, out_specs, out_shape = [], [], []
    for w in ws:
        _, r, cw = w.shape
        if r % (8 * ADAM_BLOCKS) == 0:
            spec = pl.BlockSpec((None, r // ADAM_BLOCKS, cw), lambda l, i: (l, i, 0))
            gspec = pl.BlockSpec((r // ADAM_BLOCKS, cw), lambda l, i: (i, 0))
        else:
            spec = pl.BlockSpec((None, r, cw // ADAM_BLOCKS), lambda l, i: (l, 0, i))
            gspec = pl.BlockSpec((r, cw // ADAM_BLOCKS), lambda l, i: (0, i))
        in_specs += [spec] * 3 + [gspec] * DEPTH
        out_specs += [spec] * 4
        out_shape += [jax.ShapeDtypeStruct(w.shape, F32)] * 4
    ops = [a for w, m, v, g in zip(ws, ms, vs, gs) for a in (w, m, v, *g)]
    res = pl.pallas_call(
        body, name=name, grid=(DEPTH, ADAM_BLOCKS), in_specs=in_specs, out_specs=out_specs, out_shape=out_shape,
        compiler_params=_cp(("parallel", "parallel")),
    )(*ops)
    return res[0::4], res[1::4], res[2::4], res[3::4]


def _adam(w, m, v, parts, name):
    rows, width = w.shape
    bm = min(256, _rows_tile(rows))
    np_ = len(parts)
    c1 = 1.0 / (1.0 - ADAM_B1 ** ADAM_STEP)
    c2 = 1.0 / (1.0 - ADAM_B2 ** ADAM_STEP)

    def body(*refs):
        w_ref, m_ref, v_ref = refs[:3]
        g = refs[3][...]
        for r in refs[4:3 + np_]:
            g = g + r[...]
        g_ref, d_ref, nm_ref, nv_ref = refs[3 + np_:]
        nm = ADAM_B1 * m_ref[...] + (1.0 - ADAM_B1) * g
        nv = ADAM_B2 * v_ref[...] + (1.0 - ADAM_B2) * (g * g)
        g_ref[...] = g
        nm_ref[...] = nm
        nv_ref[...] = nv
        d_ref[...] = -ADAM_LR * ((nm * c1) / (jnp.sqrt(nv * c2) + ADAM_EPS) + ADAM_WD * w_ref[...])

    blk = pl.BlockSpec((bm, width), lambda i: (i, 0))
    return pl.pallas_call(
        body, name=name, grid=(rows // bm,),
        in_specs=[blk, blk, blk] + [pl.BlockSpec((bm, width), functools.partial(lambda i, o: (i + o, 0), o=off // bm))
                                    for (_, off) in parts],
        out_specs=[blk, blk, blk, blk],
        out_shape=[jax.ShapeDtypeStruct((rows, width), F32)] * 4,
        compiler_params=_cp(("parallel",)),
    )(w, m, v, *[p[0] for p in parts])


def _coords():
    return lax.axis_index("x"), lax.axis_index("y"), lax.axis_index("c")


def _other_chips(x, y):
    return [(1 - x, y), (x, 1 - y), (1 - x, 1 - y)]


def _ag8(blk, name):
    m_per, n = blk.shape

    def body(x_ref, out_ref, send_sems, recv_sems, local_sem):
        x, y, c = _coords()
        me, sibling = (x, y, c), (x, y, 1 - c)
        chips = _other_chips(x, y)

        def rows(px, py, pc):
            return out_ref.at[pl.ds((4 * px + 2 * py + pc) * m_per, m_per), :]

        def copy(k, block, to, src=None):
            return pltpu.make_async_remote_copy(
                src_ref=rows(*block) if src is None else src, dst_ref=rows(*block),
                send_sem=send_sems.at[k], recv_sem=recv_sems.at[k], device_id=to, device_id_type=MESH)

        mine = pltpu.make_async_copy(x_ref, rows(*me), local_sem)
        mine.start()
        first = [copy(0, me, sibling, src=x_ref)]
        first += [copy(1 + j, me, (*chip, c), src=x_ref) for j, chip in enumerate(chips)]
        for cp in first:
            cp.start()
        passed = [copy(4 + j, (*chip, c), sibling) for j, chip in enumerate(chips)]
        for j, chip in enumerate(chips):
            copy(1 + j, (*chip, c), me).wait_recv()
            passed[j].start()
        copy(0, sibling, me).wait_recv()
        for j, chip in enumerate(chips):
            copy(4 + j, (*chip, 1 - c), me).wait_recv()
        for cp in first + passed:
            cp.wait_send()
        mine.wait()

    return pl.pallas_call(
        body, name=name,
        out_shape=jax.ShapeDtypeStruct((8 * m_per, n), blk.dtype),
        in_specs=[pl.BlockSpec(memory_space=pltpu.VMEM)], out_specs=pl.BlockSpec(memory_space=pltpu.VMEM),
        scratch_shapes=[pltpu.SemaphoreType.DMA((7,)), pltpu.SemaphoreType.DMA((7,)), pltpu.SemaphoreType.DMA],
    )(blk)


HBM_SPEC = pl.BlockSpec(memory_space=pltpu.HBM)
SEM_SPEC = pl.BlockSpec(memory_space=pltpu.SEMAPHORE)
EFFECT = pltpu.SideEffectType.DATAFLOW_SIDE_EFFECTING


def _remote(src, dst, send_sem, recv_sem, to):
    return pltpu.make_async_remote_copy(src_ref=src, dst_ref=dst, send_sem=send_sem, recv_sem=recv_sem,
                                        device_id=to, device_id_type=MESH)


def _ag_list(shards, name):
    n = len(shards)

    def body(*refs):
        sh, out = refs[:n], refs[n:2 * n]
        send_sems, recv_sems = refs[2 * n:]
        x, y, c = _coords()
        sibling = (x, y, 1 - c)
        chips = _other_chips(x, y)
        first = [_remote(sh[p].at[c], out[p].at[2 * x + y, c], send_sems.at[6 * p + j], recv_sems.at[6 * p + j], (px, py, c))
                 for p in range(n) for j, (px, py) in enumerate(chips)]
        for cp in first:
            cp.start()
        passed = []
        for j, (px, py) in enumerate(chips):
            for p in range(n):
                got = out[p].at[2 * px + py, c]
                _remote(got, got, send_sems.at[6 * p + j], recv_sems.at[6 * p + j], (x, y, c)).wait_recv()
                cp = _remote(got, got, send_sems.at[6 * p + 3 + j], recv_sems.at[6 * p + 3 + j], sibling)
                cp.start()
                passed.append(cp)
        for j, (px, py) in enumerate(chips):
            for p in range(n):
                got = out[p].at[2 * px + py, 1 - c]
                _remote(got, got, send_sems.at[6 * p + 3 + j], recv_sems.at[6 * p + 3 + j], (x, y, c)).wait_recv()
        for cp in first + passed:
            cp.wait_send()

    return pl.pallas_call(
        body, name=name,
        out_shape=[jax.ShapeDtypeStruct((4,) + s.shape, s.dtype) for s in shards],
        in_specs=[pl.BlockSpec(memory_space=pl.ANY)] * n, out_specs=[pl.BlockSpec(memory_space=pl.ANY)] * n,
        scratch_shapes=[pltpu.SemaphoreType.DMA((6 * n,)), pltpu.SemaphoreType.DMA((6 * n,))],
    )(*shards)


def _ag_direct_copies(sh, land, send_sems, recv_sems, starting):
    x, y, c = _coords()
    return [_remote(sh[p], land[p].at[2 * x + y] if starting else land[p].at[2 * px + py],
                    send_sems.at[3 * p + j], recv_sems.at[3 * p + j], (px, py, c))
            for p in range(len(sh)) for j, (px, py) in enumerate(_other_chips(x, y))]


def _rs_sibling_copies(g, land, send_sems, recv_sems, starting):
    x, y, c = _coords()
    return [_remote(g[p].at[s, 1 - c], land[p].at[s], send_sems.at[4 * p + s], recv_sems.at[4 * p + s], (x, y, 1 - c))
            for p in range(len(g)) for s in range(4)]


def _rs_chips_copies(cs, land, send_sems, recv_sems, starting):
    x, y, c = _coords()
    return [_remote(cs[p].at[2 * px + py], land[p].at[j], send_sems.at[3 * p + j], recv_sems.at[3 * p + j], (px, py, c))
            for p in range(len(cs)) for j, (px, py) in enumerate(_other_chips(x, y))]


def _split_start(copies, srcs, land_shapes, per, name):
    n = len(srcs)

    def body(*refs):
        for cp in copies(refs[:n], refs[n:2 * n], refs[2 * n], refs[2 * n + 1], True):
            cp.start()
        token = refs[4 * n + 2]
        token[...] = jnp.zeros_like(token)

    lands = [pltpu.with_memory_space_constraint(lax.empty(shp, s.dtype), pltpu.HBM) for shp, s in zip(land_shapes, srcs)]
    res = pl.pallas_call(
        body, name=name,
        out_shape=(pltpu.SemaphoreType.DMA((per * n,)), pltpu.SemaphoreType.DMA((per * n,)))
        + tuple(pltpu.HBM(s.shape, s.dtype) for s in srcs) + tuple(pltpu.HBM(l.shape, l.dtype) for l in lands)
        + (jax.ShapeDtypeStruct((8, LANES), F32),),
        in_specs=(HBM_SPEC,) * (2 * n), out_specs=(SEM_SPEC, SEM_SPEC) + (HBM_SPEC,) * (2 * n) + (pl.BlockSpec(memory_space=pltpu.VMEM),),
        input_output_aliases={i: 2 + i for i in range(2 * n)},
        compiler_params=pltpu.CompilerParams(has_side_effects=EFFECT),
    )(*[pltpu.with_memory_space_constraint(s, pltpu.HBM) for s in srcs], *lands)
    return res[0], res[1], res[2:2 + n], res[2 + n:2 + 2 * n], res[2 + 2 * n]


def _split_wait(copies, send_sems, recv_sems, src_thru, land_thru, after, name):
    n = len(src_thru)

    def body(*refs):
        for cp in copies(refs[:n], refs[n:2 * n], refs[2 * n], refs[2 * n + 1], False):
            cp.wait_send()
            cp.wait_recv()

    res = pl.pallas_call(
        body, name=name,
        out_shape=tuple(pltpu.HBM(s.shape, s.dtype) for s in src_thru) + tuple(pltpu.HBM(l.shape, l.dtype) for l in land_thru),
        in_specs=(HBM_SPEC,) * (2 * n) + (SEM_SPEC, SEM_SPEC, pl.BlockSpec(memory_space=pl.ANY)),
        out_specs=(HBM_SPEC,) * (2 * n), input_output_aliases={i: i for i in range(2 * n)},
        compiler_params=pltpu.CompilerParams(has_side_effects=EFFECT),
    )(*src_thru, *land_thru, send_sems, recv_sems, after)
    return res[:n], res[n:]


def _ag_direct_start(shards, name):
    return _split_start(_ag_direct_copies, shards, [(4,) + s.shape for s in shards], 3, name)


def _ag_direct_wait(send_sems, recv_sems, sh_thru, land_thru, after, name):
    return _split_wait(_ag_direct_copies, send_sems, recv_sems, sh_thru, land_thru, after, name)[1]


def _rs_sibling_start(gs, name):
    return _split_start(_rs_sibling_copies, gs, [(4,) + g.shape[2:] for g in gs], 4, name)


def _rs_sibling_wait(send_sems, recv_sems, g_thru, land_thru, after, name):
    return _split_wait(_rs_sibling_copies, send_sems, recv_sems, g_thru, land_thru, after, name)


def _rs_sibling_list(gs, name):
    n = len(gs)

    def body(*refs):
        cps = _rs_sibling_copies(refs[:n], refs[n:2 * n], refs[2 * n], refs[2 * n + 1], True)
        for cp in cps:
            cp.start()
        for cp in cps:
            cp.wait_recv()
        for cp in cps:
            cp.wait_send()

    return pl.pallas_call(
        body, name=name,
        out_shape=[jax.ShapeDtypeStruct((4,) + g.shape[2:], g.dtype) for g in gs],
        in_specs=[pl.BlockSpec(memory_space=pl.ANY)] * n, out_specs=[pl.BlockSpec(memory_space=pl.ANY)] * n,
        scratch_shapes=[pltpu.SemaphoreType.DMA((4 * n,)), pltpu.SemaphoreType.DMA((4 * n,))],
    )(*gs)


def _rs_chips_start(cs, name):
    return _split_start(_rs_chips_copies, cs, [(3,) + c.shape[1:] for c in cs], 3, name)


def _rs_chips_wait(send_sems, recv_sems, cs_thru, land_thru, after, name):
    return _split_wait(_rs_chips_copies, send_sems, recv_sems, cs_thru, land_thru, after, name)


def _swap_list(ghs, name):
    n = len(ghs)

    def body(*refs):
        g, out, send_sems, recv_sems = refs[:n], refs[n:2 * n], refs[2 * n], refs[2 * n + 1]
        x, y, c = _coords()
        cps = [_remote(g[p].at[c], out[p].at[c], send_sems.at[p], recv_sems.at[p], (x, y, 1 - c)) for p in range(n)]
        for cp in cps:
            cp.start()
        for p in range(n):
            _remote(g[p].at[c], out[p].at[1 - c], send_sems.at[p], recv_sems.at[p], (x, y, 1 - c)).wait_recv()
        for cp in cps:
            cp.wait_send()

    return pl.pallas_call(
        body, name=name,
        out_shape=[jax.ShapeDtypeStruct(g.shape, g.dtype) for g in ghs],
        in_specs=[pl.BlockSpec(memory_space=pl.ANY)] * n, out_specs=[pl.BlockSpec(memory_space=pl.ANY)] * n,
        input_output_aliases={p: p for p in range(n)},
        scratch_shapes=[pltpu.SemaphoreType.DMA((n,)), pltpu.SemaphoreType.DMA((n,))],
    )(*ghs)


def _pad_win(w):
    return jnp.concatenate([w[:, :416], jnp.zeros((w.shape[0], 96), w.dtype), w[:, 416:1952],
                            w[:, 1952:1960], jnp.zeros((w.shape[0], 120), w.dtype)], axis=1)


def _unpad_win(g):
    return jnp.concatenate([g[:, :416], g[:, 512:2048], g[:, 2048:2056]], axis=1)


def _pad_wq(w):
    return jnp.pad(w.reshape(Q_LORA, HEADS, QK_DIM), ((0, 0), (0, 0), (0, LANES - QK_DIM))).reshape(Q_LORA, HEADS * LANES)


def _unpad_wq(g):
    return g.reshape(Q_LORA, HEADS, LANES)[:, :, :QK_DIM].reshape(Q_LORA, HEADS * QK_DIM)


def _cols_to_shards(a):
    r, c4 = a.shape
    return a.reshape(r, 4, c4 // 4).transpose(1, 0, 2)


def _shards_to_cols(a):
    _, r, c = a.shape
    return a.transpose(1, 0, 2).reshape(r, 4 * c)


def _pack_small(tree):
    parts = []
    for l in range(DEPTH):
        for (n, k) in SMALL:
            parts.append(jnp.pad(tree[n][l].reshape(-1), (0, -k % LANES)))
    flat = jnp.concatenate(parts)
    return jnp.pad(flat, (0, SMALL_ROWS * LANES - flat.shape[0])).reshape(SMALL_ROWS, LANES)


def _unpack_small(buf):
    flat = buf.reshape(-1)
    out = {n: [] for (n, _) in SMALL}
    o = 0
    for l in range(DEPTH):
        for (n, k) in SMALL:
            out[n].append(flat[o:o + k])
            o += k + (-k % LANES)
    return {n: jnp.stack(v) for n, v in out.items()}


def _vec(v, width=LANES):
    return jnp.pad(v.reshape(1, -1), ((0, 0), (0, width - v.shape[-1])))


def kernel(x, c, positions, norm1_w, norm2_w, w_ada, b_ada, w_in, q_a_norm_w, w_q_up, kv_a_norm_w, w_kv_up, q_nope_norm_w, q_pe_norm_w, k_nope_norm_w, k_pe_norm_w, conv_w, conv_b, dt_bias, a_log, d_skip, ssd_norm_w, w_out, w_gate_up, w_down, loss_target, m_norm1_w, m_norm2_w, m_w_ada, m_b_ada, m_w_in, m_q_a_norm_w, m_w_q_up, m_kv_a_norm_w, m_w_kv_up, m_q_nope_norm_w, m_q_pe_norm_w, m_k_nope_norm_w, m_k_pe_norm_w, m_conv_w, m_conv_b, m_dt_bias, m_a_log, m_d_skip, m_ssd_norm_w, m_w_out, m_w_gate_up, m_w_down, v_norm1_w, v_norm2_w, v_w_ada, v_b_ada, v_w_in, v_q_a_norm_w, v_w_q_up, v_kv_a_norm_w, v_w_kv_up, v_q_nope_norm_w, v_q_pe_norm_w, v_k_nope_norm_w, v_k_pe_norm_w, v_conv_w, v_conv_b, v_dt_bias, v_a_log, v_d_skip, v_ssd_norm_w, v_w_out, v_w_gate_up, v_w_down):
    W = dict(zip(WEIGHTS, (norm1_w, norm2_w, w_ada, b_ada, w_in, q_a_norm_w, w_q_up, kv_a_norm_w, w_kv_up, q_nope_norm_w, q_pe_norm_w, k_nope_norm_w, k_pe_norm_w, conv_w, conv_b, dt_bias, a_log, d_skip, ssd_norm_w, w_out, w_gate_up, w_down)))
    M = dict(zip(WEIGHTS, (m_norm1_w, m_norm2_w, m_w_ada, m_b_ada, m_w_in, m_q_a_norm_w, m_w_q_up, m_kv_a_norm_w, m_w_kv_up, m_q_nope_norm_w, m_q_pe_norm_w, m_k_nope_norm_w, m_k_pe_norm_w, m_conv_w, m_conv_b, m_dt_bias, m_a_log, m_d_skip, m_ssd_norm_w, m_w_out, m_w_gate_up, m_w_down)))
    V = dict(zip(WEIGHTS, (v_norm1_w, v_norm2_w, v_w_ada, v_b_ada, v_w_in, v_q_a_norm_w, v_w_q_up, v_kv_a_norm_w, v_w_kv_up, v_q_nope_norm_w, v_q_pe_norm_w, v_k_nope_norm_w, v_k_pe_norm_w, v_conv_w, v_conv_b, v_dt_bias, v_a_log, v_d_skip, v_ssd_norm_w, v_w_out, v_w_gate_up, v_w_down)))
    S = x.shape[1]
    xi, yi, ci = _coords()
    chip = 2 * xi + yi
    dev = 2 * chip + ci
    x0 = x[0]
    tgt = loss_target[0]

    inv_freq = 1.0 / (ROPE_THETA ** (jnp.arange(0, ROPE, 2, dtype=F32) / ROPE))
    ang = positions[0].astype(F32)[:, None] * inv_freq
    cos, sin = jnp.cos(ang), jnp.sin(ang)
    z16, z32, z64 = jnp.zeros((S, 16), F32), jnp.zeros((S, 32), F32), jnp.zeros((S, 64), F32)
    tab_c = jnp.concatenate([jnp.ones((S, 64), F32), cos, cos, z32], axis=1)
    tab_s1 = jnp.concatenate([z64, z16, sin, z32], axis=1)
    tab_s2 = jnp.concatenate([z64, -sin, z16, z32], axis=1)

    sh = [{n: W[n][l].astype(BF16) for n in BIG} for l in range(DEPTH)]
    ag_f0 = _ag_direct_start([sh[0][n] for n in FIRST], "ag_w0_first_start")
    blk0 = jnp.concatenate([c.reshape(-1), W['conv_w'].reshape(-1)]).reshape(24, LANES) + ag_f0[4][0, 0]
    g0 = _ag8(blk0, "ag_c_conv").reshape(8, 24 * LANES)
    c_all = g0[:, :D_MODEL]
    conv_full = g0[0::2, D_MODEL:].reshape(4, DEPTH, CONV_TAPS, 256).transpose(1, 2, 0, 3).reshape(DEPTH, CONV_TAPS, D_CONV)

    to_operand = dict(w_in=lambda a: _pad_win(_shards_to_cols(a)), w_q_up=lambda a: _pad_wq(_shards_to_cols(a)),
                      w_kv_up=_shards_to_cols, w_out=lambda a: a.reshape(D_MODEL, D_MODEL), w_gate_up=lambda a: a,
                      w_down=lambda a: a.reshape(D_FF, D_MODEL))

    def layer_weights(names, gathered, own):
        return {n: to_operand[n](lax.dynamic_update_slice_in_dim(a.reshape(4, -1, a.shape[-1]), own[n][None], chip, axis=0))
                for n, a in zip(names, gathered)}

    LW = [None, None]

    b_sh = lax.dynamic_slice_in_dim(W['b_ada'], chip * 1536, 1536, axis=1).reshape(DEPTH, 1, 1536)
    mod_sh = _ada_fwd(c_all, W['w_ada'], b_sh, "ada_fwd")
    g1 = _ag8(mod_sh.reshape(192, LANES), "ag_mod").reshape(8, DEPTH, 8, 1536)
    mod_all = g1[0::2].transpose(1, 2, 0, 3).reshape(DEPTH, 8, 6 * D_MODEL)
    mod = lax.dynamic_index_in_dim(mod_all, dev, axis=1, keepdims=False)
    mod, rest0 = lax.optimization_barrier((mod, [sh[0][n] for n in REST]))
    ag0 = _ag_direct_start(rest0, "ag_w0_rest_start")

    def mvec(l, k):
        return mod[l, k * D_MODEL:(k + 1) * D_MODEL].reshape(1, D_MODEL)

    def small(name, l, width=None):
        v = W[name][l]
        return _vec(v, width or v.shape[-1])

    def wq_vec(l):
        return _vec(jnp.concatenate([W['q_nope_norm_w'][l], W['q_pe_norm_w'][l]]))

    def wk_vec(l):
        return _vec(jnp.concatenate([W['k_nope_norm_w'][l], W['k_pe_norm_w'][l]]))

    def conv_vec(l):
        return jnp.concatenate([conv_full[l], W['conv_b'][l].reshape(1, D_CONV), jnp.zeros((3, D_CONV), F32)], axis=0)

    def ssd_vec(l):
        return jnp.concatenate([_vec(W['dt_bias'][l]), _vec(W['a_log'][l]), _vec(W['d_skip'][l]), jnp.zeros((5, LANES), F32)], axis=0)

    sv = []
    xcur = x0
    h1 = _row_fwd(fn_norm_mod, "norm_mod_f", [(x0, 0, D_MODEL)], [small('norm1_w', 0) + ag0[4][0, 0], mvec(0, 1), mvec(0, 0)],
                  [(D_MODEL, BF16)])[0]
    LW[0] = layer_weights(FIRST, _ag_direct_wait(*ag_f0[:4], h1, "ag_w0_first_wait"), sh[0])
    fin = None
    ag_first = None
    ag_rest = ag0
    for l in range(DEPTH):
        if l == 1:
            LW[1] = layer_weights(FIRST, _ag_direct_wait(*ag_first[:4], xcur, "ag_w1_first_wait"), sh[1])
        lw = LW[l]
        t = dict(xcur=xcur, h1=h1)
        t['proj'] = proj = _mm(h1, lw['w_in'], 'nn', f"mm_in_{l}")
        t['qa_n'], t['kva_n'] = _row_fwd(fn_lat_norm, f"lat_norm_f{l}", [(proj, 0, 256), (proj, 2, 128)],
                                         [small('q_a_norm_w', l), small('kv_a_norm_w', l)], [(256, BF16), (128, BF16)])
        t['q'] = _mm(t['qa_n'], lw['w_q_up'], 'nn', f"mm_q_{l}")
        t['kv'] = _mm(t['kva_n'], lw['w_kv_up'], 'nn', f"mm_kv_{l}")
        t['qf'], t['kf'], t['vv'], t['kT'] = _row_fwd(
            fn_qk_prep_kt, f"qk_prep_f{l}",
            [(t['q'], 0, 1024), (t['kv'], 0, 1024), (proj, 3, 128), (tab_c, 0, 128), (tab_s1, 0, 128), (tab_s2, 0, 128)],
            [wq_vec(l), wk_vec(l)], [(1024, BF16), (1024, BF16), (1024, BF16), (1024, BF16)], transposed=(3,))
        t['ao'], t['lse'] = _attn_fwd(t['qf'], t['kf'], t['vv'], f"attn_f{l}")
        t['xact'] = _conv_fwd(proj, conv_vec(l), f"conv_f{l}")
        t['y'], t['states'] = _ssd_fwd(t['xact'], proj, ssd_vec(l), f"ssd_f{l}")
        tie = 0.0
        t['ao'], t['y'] = lax.optimization_barrier((t['ao'], t['y']))
        rest = list(_ag_direct_wait(*ag_rest[:4], t['y'], f"ag_w{l}_rest_wait"))
        if l == 0:
            rest, sh1f, sh1r = lax.optimization_barrier((rest, [sh[1][n] for n in FIRST], [sh[1][n] for n in REST]))
            ag_first = _ag_direct_start(sh1f, "ag_w1_first_start")
            ag_rest = _ag_direct_start(sh1r, "ag_w1_rest_start")
            tie = ag_first[4][0, 0] + ag_rest[4][0, 0]
        lw.update(layer_weights(REST, rest, sh[l]))
        t['mix'] = _row_fwd(fn_gated_mix, f"gated_f{l}", [(t['y'], 0, 512), (proj, 1, 512), (t['ao'], 0, 512)],
                            [small('ssd_norm_w', l) + tie], [(1024, BF16)])[0]
        t['mo'] = _mm(t['mix'], lw['w_out'], 'nn', f"mm_out_{l}")
        t['x1'], t['h2'] = _row_fwd(fn_resid_norm, f"resid_mid_f{l}", [(xcur, 0, D_MODEL), (t['mo'], 0, D_MODEL)],
                                    [mvec(l, 2), small('norm2_w', l), mvec(l, 4), mvec(l, 3)],
                                    [(D_MODEL, F32), (D_MODEL, BF16)])
        t['gu'], t['act'] = _mm_gu_swiglu(t['h2'], lw['w_gate_up'], f"mm_gu_{l}")
        t['ff'] = _mm(t['act'], lw['w_down'], 'nn', f"mm_down_{l}")
        if l + 1 < DEPTH:
            xcur, h1 = _row_fwd(fn_resid_norm, f"resid_end_f{l}", [(t['x1'], 0, D_MODEL), (t['ff'], 0, D_MODEL)],
                                [mvec(l, 5), small('norm1_w', l + 1), mvec(l + 1, 1), mvec(l + 1, 0)],
                                [(D_MODEL, F32), (D_MODEL, BF16)])
        else:
            fin = _final(t['x1'], t['ff'], tgt, mvec(l, 5), "final_loss")
        sv.append(t)

    dx1, dff, dg2_last, loss_acc = fin
    gfull = {n: [None] * DEPTH for n in BIG}
    gsm = {n: [None] * DEPTH for (n, _) in SMALL}
    dmod = [[None] * 6 for _ in range(DEPTH)]
    dmod[DEPTH - 1][5] = dg2_last
    grad_x = None
    pending = []

    def halves_of(l, names):
        return [gfull[n][l].reshape(4, 2, gfull[n][l].shape[1] // 2, gfull[n][l].shape[2]) for n in names]

    def rs_finish(l, names, tag, g4, sib):
        h = _rs_chips_start(_sum_sibling(g4, sib, ci, f"sum_sibling_{tag}"), f"rs_chips_start_{tag}")
        pending.append((l, names, h))
        return h[4][0, 0]

    tie_l1 = tie_l0a = tie_sib = 0.0
    sib_l1 = sib_l0a = None

    for l in reversed(range(DEPTH)):
        t = sv[l]
        lw = LW[l]
        proj = t['proj']
        dgu = _mm_down_dx_swiglu(dff, lw['w_down'], t['gu'], f"mm_down_dx{l}")
        gfull['w_down'][l] = _mm(t['act'], dff, 'tn', f"mm_down_dw{l}").reshape(4, D_FF // 4, D_MODEL)
        dh2 = _mm(dgu, lw['w_gate_up'], 'nt', f"mm_gu_dx{l}", stack='b')
        gfull['w_gate_up'][l] = _mm(t['h2'], dgu, 'tn', f"mm_gu_dw{l}", stack='out')
        if l == 0:
            sib_l0a = _rs_sibling_start(halves_of(0, EARLY), "rs_sibling_start_l0a")
            tie_l1 = rs_finish(1, BIG, "l1", *_rs_sibling_wait(*sib_l1[:4], dh2, "rs_sibling_wait_l1"))
            tie_sib = sib_l0a[4][0, 0]
        dxc, dmo, dmod[l][2], gsm['norm2_w'][l], dmod[l][4], dmod[l][3] = _row_bwd(
            fn_resid_norm, f"resid_mid_b{l}", [(t['xcur'], 0, D_MODEL), (t['mo'], 0, D_MODEL)],
            [mvec(l, 2) + ((tie_l1 + tie_sib) if l == 0 else 0.0), small('norm2_w', l), mvec(l, 4), mvec(l, 3)],
            [(dx1, 0, D_MODEL), (dh2, 0, D_MODEL)], [0, 1], [0, 1, 2, 3], ddtypes=[F32, BF16])
        dmix = _mm(dmo, lw['w_out'], 'nt', f"mm_out_dx{l}")
        gfull['w_out'][l] = _mm(t['mix'], dmo, 'tn', f"mm_out_dw{l}").reshape(4, D_MODEL // 4, D_MODEL)
        dy, dz, gsm['ssd_norm_w'][l] = _row_bwd(fn_gated_norm, f"gated_b{l}", [(t['y'], 0, 512), (proj, 1, 512)],
                                                [small('ssd_norm_w', l)], [(dmix, 1, 512)], [0, 1], [0])
        if l == 0:
            tie_l0a = rs_finish(0, EARLY, "l0a", *_rs_sibling_wait(*sib_l0a[:4], dmix, "rs_sibling_wait_l0a"))
        dxact, ddt, dsv = _ssd_bwd(t['xact'], proj, ssd_vec(l) + (tie_l0a if l == 0 else 0.0), t['states'], dy, f"ssd_b{l}")
        gsm['dt_bias'][l], gsm['a_log'][l], gsm['d_skip'][l] = dsv[0, :8], dsv[1, :8], dsv[2, :8]
        dxbc, dcv = _conv_bwd(proj, conv_vec(l), dxact, f"conv_b{l}")
        gsm['conv_w'][l] = dcv[:CONV_TAPS]
        gsm['conv_b'][l] = dcv[CONV_TAPS]
        delta_r = _attn_delta(dmix, t['ao'], f"attn_delta{l}")
        dqT, dkf, dvv = _attn_bwd(t['qf'], t['kf'], t['kT'], t['vv'], dmix, t['lse'], delta_r, f"attn_b{l}")
        dq, dkv, dkpe, dwq, dwk = _row_bwd(
            fn_qk_prep, f"qk_prep_b{l}",
            [(t['q'], 0, 1024), (t['kv'], 0, 1024), (proj, 3, 128), (tab_c, 0, 128), (tab_s1, 0, 128), (tab_s2, 0, 128)],
            [wq_vec(l), wk_vec(l)], [(dqT, 0, 1024), (dkf, 0, 1024), (dvv, 0, 1024)], [0, 1, 2], [0, 1],
            ddtypes=[BF16, BF16, F32], transposed=(0,))
        gsm['q_nope_norm_w'][l], gsm['q_pe_norm_w'][l] = dwq[0, :NOPE], dwq[0, NOPE:QK_DIM]
        gsm['k_nope_norm_w'][l], gsm['k_pe_norm_w'][l] = dwk[0, :NOPE], dwk[0, NOPE:QK_DIM]
        dqa_n = _mm(dq, lw['w_q_up'], 'nt', f"mm_q_dx{l}")
        gfull['w_q_up'][l] = _cols_to_shards(_unpad_wq(_mm(t['qa_n'], dq, 'tn', f"mm_q_dw{l}")))
        dkva_n = _mm(dkv, lw['w_kv_up'], 'nt', f"mm_kv_dx{l}")
        gfull['w_kv_up'][l] = _cols_to_shards(_mm(t['kva_n'], dkv, 'tn', f"mm_kv_dw{l}"))
        dqa, dkva, dqw, dkvw = _row_bwd(fn_lat_norm, f"lat_norm_b{l}", [(proj, 0, 256), (proj, 2, 128)],
                                        [small('q_a_norm_w', l), small('kv_a_norm_w', l)],
                                        [(dqa_n, 0, 256), (dkva_n, 0, 128)], [0, 1], [0, 1])
        gsm['q_a_norm_w'][l], gsm['kv_a_norm_w'][l] = dqw[0], dkvw[0]
        dproj = jnp.concatenate([dqa, dkva, dkpe, dz, dxbc, ddt], axis=1).astype(BF16)
        dh1 = _mm(dproj, lw['w_in'], 'nt', f"mm_in_dx{l}")
        gfull['w_in'][l] = _cols_to_shards(_unpad_win(_mm(t['h1'], dproj, 'tn', f"mm_in_dw{l}")))
        if l > 0:
            p = sv[l - 1]
            dx1, dff, dmod[l - 1][5], gsm['norm1_w'][l], dmod[l][1], dmod[l][0] = _row_bwd(
                fn_resid_norm, f"resid_end_b{l - 1}", [(p['x1'], 0, D_MODEL), (p['ff'], 0, D_MODEL)],
                [mvec(l - 1, 5), small('norm1_w', l), mvec(l, 1), mvec(l, 0)], [(dxc, 0, D_MODEL), (dh1, 0, D_MODEL)],
                [0, 1], [0, 1, 2, 3], ddtypes=[F32, BF16])
            sib_l1 = _rs_sibling_start(halves_of(l, BIG), f"rs_sibling_start_l{l}")
            dff = dff + sib_l1[4][0, 0].astype(BF16)
        else:
            grad_x, gsm['norm1_w'][l], dmod[l][1], dmod[l][0] = _row_bwd(
                fn_norm_mod_pass, "norm_mod_b", [(x0, 0, D_MODEL)], [small('norm1_w', 0), mvec(0, 1), mvec(0, 0)],
                [(dxc, 0, D_MODEL), (dh1, 0, D_MODEL)], [0], [0, 1, 2])
        for n in ('norm1_w', 'norm2_w', 'ssd_norm_w'):
            gsm[n][l] = gsm[n][l][0]

    for l in range(DEPTH):
        gsm['b_ada'][l] = jnp.concatenate([d[0] for d in dmod[l]])
    sm_part = _pack_small({n: jnp.stack(v) for n, v in gsm.items()}).at[SMALL_ROWS - 1, 0].set(loss_acc[0, 0])
    sm_all = _ag8(sm_part, "ag_small")
    loss = jnp.sum(sm_all.reshape(8, SMALL_ROWS, LANES)[:, SMALL_ROWS - 1, 0])
    sm_all, late = lax.optimization_barrier((sm_all, [gfull[n][0] for n in BIG[2:]]))
    for n, g in zip(BIG[2:], late):
        gfull[n][0] = g
    late4 = halves_of(0, BIG[2:])
    tie_l0b = rs_finish(0, BIG[2:], "l0b", late4, _rs_sibling_list(late4, "rs_sibling_l0b"))

    def with_conv(tree):
        wide = lax.dynamic_update_slice_in_dim(jnp.zeros((DEPTH, CONV_TAPS, D_CONV), F32), tree['conv_w'], chip * 256, axis=2)
        return {**tree, 'conv_w': wide}

    g_sm, d_sm, m_sm, v_sm = _adam(_pack_small(with_conv(W)) + tie_l0b, _pack_small(with_conv(M)), _pack_small(with_conv(V)),
                                   [(sm_all, d * SMALL_ROWS) for d in range(8)], "adam_small")
    out_small = [_unpack_small(b) for b in (g_sm, d_sm, m_sm, v_sm)]
    for o in out_small:
        o['conv_w'] = lax.dynamic_slice_in_dim(o['conv_w'].reshape(DEPTH, CONV_TAPS, D_CONV), chip * 256, 256, axis=2)

    dmod_all = sm_all.reshape(8, SMALL_ROWS * LANES)
    per_layer = sum(k + (-k % LANES) for (_, k) in SMALL)
    dmod_sh = jnp.stack([lax.dynamic_slice_in_dim(dmod_all[:, l * per_layer:l * per_layer + 6 * D_MODEL], chip * 1536, 1536, axis=1)
                         for l in range(DEPTH)])
    ada = out_ada = _ada_bwd_adam(c_all.T, dmod_sh, W['w_ada'], M['w_ada'], V['w_ada'], "ada_bwd_adam")

    keys, cs_all, land_all = [], [], []
    for (l, names, (send_sems, recv_sems, cs_thru, land_thru, _)) in pending:
        cs, lands = _rs_chips_wait(send_sems, recv_sems, cs_thru, land_thru, ada[3], f"rs_chips_wait_l{l}{len(names)}")
        keys += [(l, n) for n in names]
        cs_all += list(cs)
        land_all += list(lands)
    gboth = _swap_list(_sum_chips(cs_all, land_all, chip, ci, "sum_chips"), "swap_halves")
    gshard = {k: g.reshape(2 * g.shape[1], g.shape[2]) for k, g in zip(keys, gboth)}

    def natural(n, a):
        return jnp.swapaxes(a, -1, -2) if n == 'w_in' else a

    res = _adam_multi([natural(n, W[n]) for n in BIG], [natural(n, M[n]) for n in BIG], [natural(n, V[n]) for n in BIG],
                      [[natural(n, gshard[(l, n)]) for l in range(DEPTH)] for n in BIG], "adam_big")
    out_big = [{n: natural(n, a) for n, a in zip(BIG, o)} for o in res]

    outs = [loss, grad_x[None]]
    for k in range(4):
        for n in WEIGHTS:
            if n == 'w_ada':
                outs.append(out_ada[k])
            elif n in BIG:
                outs.append(out_big[k][n])
            else:
                outs.append(out_small[k][n])
    return tuple(outs)
```
